```python
import math
import jax, jax.numpy as jnp
from jax import lax
import numpy as np

D_MODEL = 1024
BATCH = 8
SEQ = 4096
DEPTH = 1

MEM_LEN = 256
MLA_HEADS = 8
MLA_NOPE = 64
MLA_ROPE = 32
MLA_V = 64
MLA_Q_LORA = 256
MLA_KV_LORA = 128
MLA_WIDTH = MLA_HEADS * MLA_V
SB_HEADS = 8
SB_HEAD_DIM = 64
SB_WIDTH = SB_HEADS * SB_HEAD_DIM
MEM_HEADS = 4
MEM_HEAD_DIM = 128
MEM_WIDTH = MEM_HEADS * MEM_HEAD_DIM
N_BRANCHES = 3

BLOCK_Q = 128
ROPE_BASE = 10000.0
RMS_EPS = 1e-6
LN_EPS = 1e-5
DEEPNORM_ALPHA = (2.0 * DEPTH) ** 0.25
DEEPNORM_BETA = (8.0 * DEPTH) ** -0.25

IN_SIZES = [
    MLA_Q_LORA, MLA_KV_LORA, MLA_ROPE, MLA_WIDTH,
    SB_WIDTH, SB_WIDTH, SB_WIDTH, SB_WIDTH,
    MEM_WIDTH, MEM_WIDTH,
]
IN_WIDTH = int(sum(IN_SIZES))
IN_OFFSETS = [int(o) for o in np.cumsum(IN_SIZES)[:-1]]

kernel_name = "hybrid_mla_stickbreaking_memxattn_deepnorm"


def _rms_norm(x, g):
    x32 = x.astype(jnp.float32)
    y = x32 * lax.rsqrt(jnp.mean(x32 * x32, axis=-1, keepdims=True) + RMS_EPS)
    return (y * g.astype(jnp.float32)).astype(x.dtype)


def _layer_norm(x, g, b):
    x32 = x.astype(jnp.float32)
    mu = jnp.mean(x32, axis=-1, keepdims=True)
    xc = x32 - mu
    var = jnp.mean(xc * xc, axis=-1, keepdims=True)
    y = xc * lax.rsqrt(var + LN_EPS) * g.astype(jnp.float32) + b.astype(jnp.float32)
    return y.astype(x.dtype)


def _rope(x, pos):
    half = x.shape[-1] // 2
    freqs = ROPE_BASE ** (-jnp.arange(half, dtype=jnp.float32) / half)
    ang = pos.astype(jnp.float32)[:, None] * freqs[None, :]
    cos = jnp.cos(ang)[None, :, None, :]
    sin = jnp.sin(ang)[None, :, None, :]
    x32 = x.astype(jnp.float32)
    x1, x2 = x32[..., :half], x32[..., half:]
    out = jnp.concatenate([x1 * cos - x2 * sin, x1 * sin + x2 * cos], axis=-1)
    return out.astype(x.dtype)


def _sweep_query_blocks(q, k, v, weights_fn):
    seq = q.shape[1]
    outs = []
    for start in range(0, seq, BLOCK_Q):
        end = start + BLOCK_Q
        scores = jnp.einsum('bqhd,bkhd->bhqk', q[:, start:end], k[:, :end]).astype(jnp.float32)
        q_pos = (start + jnp.arange(BLOCK_Q))[:, None]
        k_pos = jnp.arange(end)[None, :]
        w = weights_fn(scores, q_pos, k_pos)
        outs.append(jnp.einsum('bhqk,bkhd->bqhd', w.astype(v.dtype), v[:, :end]))
    return jnp.concatenate(outs, axis=1)


def _softmax_causal_weights(scale):
    def fn(scores, q_pos, k_pos):
        s = jnp.where(k_pos <= q_pos, scores * scale, jnp.finfo(jnp.float32).min)
        return jax.nn.softmax(s, axis=-1)
    return fn


def _stick_breaking_weights(scale):
    def fn(scores, q_pos, k_pos):
        z = scores * scale
        strict = k_pos < q_pos
        log_beta = jax.nn.log_sigmoid(z)
        log_keep = jnp.where(strict, jax.nn.log_sigmoid(-z), 0.0)
        after = lax.cumsum(log_keep, axis=log_keep.ndim - 1, reverse=True) - log_keep
        return jnp.where(strict, jnp.exp(log_beta + after), 0.0)
    return fn


def _hybrid_layer(x, mem, w_in, w_mem_kv, q_a_gain, w_q_b, kv_a_gain, w_kv_b,
                  w_branch_mla, w_branch_sb, w_branch_mem, w_merge_gate, b_merge_gate,
                  w_out, ln_gain, ln_bias):
    b, s, _ = x.shape
    pos = jnp.arange(s, dtype=jnp.int32)
    proj = x @ w_in
    (c_q, c_kv, k_rope, gate_a, q_b, k_b, v_b, gate_b, q_m, gate_m) = jnp.split(proj, IN_OFFSETS, axis=-1)

    q_a = (_rms_norm(c_q, q_a_gain) @ w_q_b).reshape(b, s, MLA_HEADS, MLA_NOPE + MLA_ROPE)
    kv_a = (_rms_norm(c_kv, kv_a_gain) @ w_kv_b).reshape(b, s, MLA_HEADS, MLA_NOPE + MLA_V)
    q_nope, q_pe = q_a[..., :MLA_NOPE], q_a[..., MLA_NOPE:]
    k_nope, v_a = kv_a[..., :MLA_NOPE], kv_a[..., MLA_NOPE:]
    k_pe = _rope(k_rope.reshape(b, s, 1, MLA_ROPE), pos)
    q_full = jnp.concatenate([q_nope, _rope(q_pe, pos)], axis=-1)
    k_full = jnp.concatenate([k_nope, jnp.broadcast_to(k_pe, (b, s, MLA_HEADS, MLA_ROPE))], axis=-1)
    o_a = _sweep_query_blocks(q_full, k_full, v_a,
                              _softmax_causal_weights(1.0 / math.sqrt(MLA_NOPE + MLA_ROPE)))
    y_a = (o_a.reshape(b, s, MLA_WIDTH) * jax.nn.silu(gate_a)) @ w_branch_mla

    q_sb = q_b.reshape(b, s, SB_HEADS, SB_HEAD_DIM)
    k_sb = k_b.reshape(b, s, SB_HEADS, SB_HEAD_DIM)
    v_sb = v_b.reshape(b, s, SB_HEADS, SB_HEAD_DIM)
    o_b = _sweep_query_blocks(q_sb, k_sb, v_sb, _stick_breaking_weights(1.0 / math.sqrt(SB_HEAD_DIM)))
    y_b = (o_b.reshape(b, s, SB_WIDTH) * jax.nn.silu(gate_b)) @ w_branch_sb

    mem_kv = (mem @ w_mem_kv).reshape(b, mem.shape[1], 2, MEM_HEADS, MEM_HEAD_DIM)
    k_m, v_m = mem_kv[:, :, 0], mem_kv[:, :, 1]
    q_mh = q_m.reshape(b, s, MEM_HEADS, MEM_HEAD_DIM)
    sc = jnp.einsum('bshd,bmhd->bhsm', q_mh, k_m).astype(jnp.float32) / math.sqrt(MEM_HEAD_DIM)
    p_m = jax.nn.softmax(sc, axis=-1).astype(v_m.dtype)
    o_m = jnp.einsum('bhsm,bmhd->bshd', p_m, v_m).reshape(b, s, MEM_WIDTH)
    y_m = (o_m * jax.nn.silu(gate_m)) @ w_branch_mem

    g = jax.nn.sigmoid(x @ w_merge_gate + b_merge_gate)
    g_a, g_b, g_m = jnp.split(g, N_BRANCHES, axis=-1)
    merged = g_a * y_a + g_b * y_b + g_m * y_m
    out = merged @ w_out

    return _layer_norm(DEEPNORM_ALPHA * x + out, ln_gain, ln_bias)


def _fwd_setup_inputs(seed: int = 0) -> dict:
    key = jax.random.key(seed)
    ks = jax.random.split(key, 18)
    f32 = jnp.float32

    def nrm(k, shape, fan_in, gain=1.0):
        return jax.random.normal(k, shape, f32) * (gain * fan_in ** -0.5)

    L = DEPTH
    return {
        "x": jax.random.normal(ks[0], (BATCH, SEQ, D_MODEL), f32),
        "mem": jax.random.normal(ks[1], (BATCH, MEM_LEN, D_MODEL), f32),
        "w_in": nrm(ks[2], (L, D_MODEL, IN_WIDTH), D_MODEL),
        "w_mem_kv": nrm(ks[3], (L, D_MODEL, 2 * MEM_WIDTH), D_MODEL),
        "q_a_gain": 1.0 + 0.01 * jax.random.normal(ks[4], (L, MLA_Q_LORA), f32),
        "w_q_b": nrm(ks[5], (L, MLA_Q_LORA, MLA_HEADS * (MLA_NOPE + MLA_ROPE)), MLA_Q_LORA),
        "kv_a_gain": 1.0 + 0.01 * jax.random.normal(ks[6], (L, MLA_KV_LORA), f32),
        "w_kv_b": nrm(ks[7], (L, MLA_KV_LORA, MLA_HEADS * (MLA_NOPE + MLA_V)), MLA_KV_LORA),
        "w_branch_mla": nrm(ks[8], (L, MLA_WIDTH, D_MODEL), MLA_WIDTH, DEEPNORM_BETA),
        "w_branch_sb": nrm(ks[9], (L, SB_WIDTH, D_MODEL), SB_WIDTH, DEEPNORM_BETA),
        "w_branch_mem": nrm(ks[10], (L, MEM_WIDTH, D_MODEL), MEM_WIDTH, DEEPNORM_BETA),
        "w_merge_gate": nrm(ks[11], (L, D_MODEL, N_BRANCHES * D_MODEL), D_MODEL),
        "b_merge_gate": 0.01 * jax.random.normal(ks[12], (L, N_BRANCHES * D_MODEL), f32),
        "w_out": nrm(ks[13], (L, D_MODEL, D_MODEL), D_MODEL, DEEPNORM_BETA),
        "ln_gain": 1.0 + 0.01 * jax.random.normal(ks[14], (L, D_MODEL), f32),
        "ln_bias": 0.01 * jax.random.normal(ks[15], (L, D_MODEL), f32),
    }


def _fwd_reference(x, mem, w_in, w_mem_kv, q_a_gain, w_q_b, kv_a_gain, w_kv_b,
              w_branch_mla, w_branch_sb, w_branch_mem, w_merge_gate, b_merge_gate,
              w_out, ln_gain, ln_bias):
    h = x
    for l in range(DEPTH):
        h = _hybrid_layer(h, mem, w_in[l], w_mem_kv[l], q_a_gain[l], w_q_b[l], kv_a_gain[l], w_kv_b[l],
                          w_branch_mla[l], w_branch_sb[l], w_branch_mem[l], w_merge_gate[l],
                          b_merge_gate[l], w_out[l], ln_gain[l], ln_bias[l])
    return h


import jax as _jax
import jax.numpy as _jnp

TWIN_FORMAT = 'train_step'
FWD_PARAMS = ['x', 'mem', 'w_in', 'w_mem_kv', 'q_a_gain', 'w_q_b', 'kv_a_gain', 'w_kv_b', 'w_branch_mla', 'w_branch_sb', 'w_branch_mem', 'w_merge_gate', 'b_merge_gate', 'w_out', 'ln_gain', 'ln_bias']
TWIN_WEIGHTS = ['w_in', 'w_mem_kv', 'q_a_gain', 'w_q_b', 'kv_a_gain', 'w_kv_b', 'w_branch_mla', 'w_branch_sb', 'w_branch_mem', 'w_merge_gate', 'b_merge_gate', 'w_out', 'ln_gain', 'ln_bias']
TWIN_DIFF_INPUT = 'x'
TWIN_INPUTS = ['x', 'mem', 'w_in', 'w_mem_kv', 'q_a_gain', 'w_q_b', 'kv_a_gain', 'w_kv_b', 'w_branch_mla', 'w_branch_sb', 'w_branch_mem', 'w_merge_gate', 'b_merge_gate', 'w_out', 'ln_gain', 'ln_bias', 'loss_target', 'm_w_in', 'm_w_mem_kv', 'm_q_a_gain', 'm_w_q_b', 'm_kv_a_gain', 'm_w_kv_b', 'm_w_branch_mla', 'm_w_branch_sb', 'm_w_branch_mem', 'm_w_merge_gate', 'm_b_merge_gate', 'm_w_out', 'm_ln_gain', 'm_ln_bias', 'v_w_in', 'v_w_mem_kv', 'v_q_a_gain', 'v_w_q_b', 'v_kv_a_gain', 'v_w_kv_b', 'v_w_branch_mla', 'v_w_branch_sb', 'v_w_branch_mem', 'v_w_merge_gate', 'v_b_merge_gate', 'v_w_out', 'v_ln_gain', 'v_ln_bias']
TWIN_OUTPUTS = ['loss', 'grad_x', 'grad_w_in', 'grad_w_mem_kv', 'grad_q_a_gain', 'grad_w_q_b', 'grad_kv_a_gain', 'grad_w_kv_b', 'grad_w_branch_mla', 'grad_w_branch_sb', 'grad_w_branch_mem', 'grad_w_merge_gate', 'grad_b_merge_gate', 'grad_w_out', 'grad_ln_gain', 'grad_ln_bias', 'delta_w_in', 'delta_w_mem_kv', 'delta_q_a_gain', 'delta_w_q_b', 'delta_kv_a_gain', 'delta_w_kv_b', 'delta_w_branch_mla', 'delta_w_branch_sb', 'delta_w_branch_mem', 'delta_w_merge_gate', 'delta_b_merge_gate', 'delta_w_out', 'delta_ln_gain', 'delta_ln_bias', 'new_m_w_in', 'new_m_w_mem_kv', 'new_m_q_a_gain', 'new_m_w_q_b', 'new_m_kv_a_gain', 'new_m_w_kv_b', 'new_m_w_branch_mla', 'new_m_w_branch_sb', 'new_m_w_branch_mem', 'new_m_w_merge_gate', 'new_m_b_merge_gate', 'new_m_w_out', 'new_m_ln_gain', 'new_m_ln_bias', 'new_v_w_in', 'new_v_w_mem_kv', 'new_v_q_a_gain', 'new_v_w_q_b', 'new_v_kv_a_gain', 'new_v_w_kv_b', 'new_v_w_branch_mla', 'new_v_w_branch_sb', 'new_v_w_branch_mem', 'new_v_w_merge_gate', 'new_v_b_merge_gate', 'new_v_w_out', 'new_v_ln_gain', 'new_v_ln_bias']
TWIN_LEAF_KINDS = {'loss': 'loss', 'grad_x': 'grad_x', 'grad_w_in': 'grad_w', 'grad_w_mem_kv': 'grad_w', 'grad_q_a_gain': 'grad_w', 'grad_w_q_b': 'grad_w', 'grad_kv_a_gain': 'grad_w', 'grad_w_kv_b': 'grad_w', 'grad_w_branch_mla': 'grad_w', 'grad_w_branch_sb': 'grad_w', 'grad_w_branch_mem': 'grad_w', 'grad_w_merge_gate': 'grad_w', 'grad_b_merge_gate': 'grad_w', 'grad_w_out': 'grad_w', 'grad_ln_gain': 'grad_w', 'grad_ln_bias': 'grad_w', 'delta_w_in': 'delta_w', 'delta_w_mem_kv': 'delta_w', 'delta_q_a_gain': 'delta_w', 'delta_w_q_b': 'delta_w', 'delta_kv_a_gain': 'delta_w', 'delta_w_kv_b': 'delta_w', 'delta_w_branch_mla': 'delta_w', 'delta_w_branch_sb': 'delta_w', 'delta_w_branch_mem': 'delta_w', 'delta_w_merge_gate': 'delta_w', 'delta_b_merge_gate': 'delta_w', 'delta_w_out': 'delta_w', 'delta_ln_gain': 'delta_w', 'delta_ln_bias': 'delta_w', 'new_m_w_in': 'new_m', 'new_m_w_mem_kv': 'new_m', 'new_m_q_a_gain': 'new_m', 'new_m_w_q_b': 'new_m', 'new_m_kv_a_gain': 'new_m', 'new_m_w_kv_b': 'new_m', 'new_m_w_branch_mla': 'new_m', 'new_m_w_branch_sb': 'new_m', 'new_m_w_branch_mem': 'new_m', 'new_m_w_merge_gate': 'new_m', 'new_m_b_merge_gate': 'new_m', 'new_m_w_out': 'new_m', 'new_m_ln_gain': 'new_m', 'new_m_ln_bias': 'new_m', 'new_v_w_in': 'new_v', 'new_v_w_mem_kv': 'new_v', 'new_v_q_a_gain': 'new_v', 'new_v_w_q_b': 'new_v', 'new_v_kv_a_gain': 'new_v', 'new_v_w_kv_b': 'new_v', 'new_v_w_branch_mla': 'new_v', 'new_v_w_branch_sb': 'new_v', 'new_v_w_branch_mem': 'new_v', 'new_v_w_merge_gate': 'new_v', 'new_v_b_merge_gate': 'new_v', 'new_v_w_out': 'new_v', 'new_v_ln_gain': 'new_v', 'new_v_ln_bias': 'new_v'}


def _forward(args):
    return _fwd_reference(*[args[k] for k in FWD_PARAMS])


def _output_shape():
    out = _jax.eval_shape(lambda: _forward(_fwd_setup_inputs(0)))
    return out.shape, out.dtype

N_MICROBATCH = 1
ADAM_LR = 0.001
ADAM_B1 = 0.9
ADAM_B2 = 0.999
ADAM_EPS = 1e-08
ADAM_WD = 0.01
ADAM_STEP = 10
PER_EXAMPLE_BATCH_AXIS = {'x': 0, 'mem': 0, 'loss_target': 0}
SHARED_INPUTS = []
_WEIGHT_DTYPES = {'w_in': _jnp.float32, 'w_mem_kv': _jnp.float32, 'q_a_gain': _jnp.float32, 'w_q_b': _jnp.float32, 'kv_a_gain': _jnp.float32, 'w_kv_b': _jnp.float32, 'w_branch_mla': _jnp.float32, 'w_branch_sb': _jnp.float32, 'w_branch_mem': _jnp.float32, 'w_merge_gate': _jnp.float32, 'b_merge_gate': _jnp.float32, 'w_out': _jnp.float32, 'ln_gain': _jnp.float32, 'ln_bias': _jnp.float32}
MOMENT_SCALE = {'w_in': 9.511103e-03, 'w_mem_kv': 2.532110e-03, 'q_a_gain': 6.411306e-03, 'w_q_b': 3.675956e-03, 'kv_a_gain': 1.291807e-02, 'w_kv_b': 4.627098e-03, 'w_branch_mla': 6.445538e-03, 'w_branch_sb': 1.843224e-02, 'w_branch_mem': 3.015538e-03, 'w_merge_gate': 2.647646e-03, 'b_merge_gate': 2.707909e-03, 'w_out': 1.970843e-02, 'ln_gain': 3.197588e+01, 'ln_bias': 3.020128e-01}


def _to_microbatches(a, axis):
    t = _jnp.moveaxis(a, axis, 0)
    t = t.reshape((N_MICROBATCH, t.shape[0] // N_MICROBATCH) + t.shape[1:])
    return _jnp.moveaxis(t, 1, axis + 1)


def setup_inputs(seed: int = 0) -> dict:
    inp = _fwd_setup_inputs(seed)
    key = _jax.random.fold_in(_jax.random.key(seed), 7919)
    shape, _ = _output_shape()
    out = dict(inp)
    out["loss_target"] = _jax.random.normal(_jax.random.fold_in(key, 0), shape, _jnp.float32)
    for i, name in enumerate(TWIN_WEIGHTS):
        w = inp[name].astype(_jnp.float32)
        if MOMENT_SCALE is None:
            s = _jnp.sqrt(_jnp.mean(_jnp.square(w)) + 1e-30)
        else:
            s = MOMENT_SCALE[name]
        km, kv = _jax.random.split(_jax.random.fold_in(key, i + 1))
        out[name] = w
        out["m_" + name] = s * _jax.random.normal(km, w.shape, _jnp.float32)
        out["v_" + name] = (s * s) * _jax.random.uniform(kv, w.shape, _jnp.float32, 0.5, 1.5)
    if N_MICROBATCH > 1:
        for name, axis in PER_EXAMPLE_BATCH_AXIS.items():
            out[name] = _to_microbatches(out[name], axis)
    return {'x': out['x'], 'mem': out['mem'], 'w_in': out['w_in'], 'w_mem_kv': out['w_mem_kv'], 'q_a_gain': out['q_a_gain'], 'w_q_b': out['w_q_b'], 'kv_a_gain': out['kv_a_gain'], 'w_kv_b': out['w_kv_b'], 'w_branch_mla': out['w_branch_mla'], 'w_branch_sb': out['w_branch_sb'], 'w_branch_mem': out['w_branch_mem'], 'w_merge_gate': out['w_merge_gate'], 'b_merge_gate': out['b_merge_gate'], 'w_out': out['w_out'], 'ln_gain': out['ln_gain'], 'ln_bias': out['ln_bias'], 'loss_target': out['loss_target'], 'm_w_in': out['m_w_in'], 'm_w_mem_kv': out['m_w_mem_kv'], 'm_q_a_gain': out['m_q_a_gain'], 'm_w_q_b': out['m_w_q_b'], 'm_kv_a_gain': out['m_kv_a_gain'], 'm_w_kv_b': out['m_w_kv_b'], 'm_w_branch_mla': out['m_w_branch_mla'], 'm_w_branch_sb': out['m_w_branch_sb'], 'm_w_branch_mem': out['m_w_branch_mem'], 'm_w_merge_gate': out['m_w_merge_gate'], 'm_b_merge_gate': out['m_b_merge_gate'], 'm_w_out': out['m_w_out'], 'm_ln_gain': out['m_ln_gain'], 'm_ln_bias': out['m_ln_bias'], 'v_w_in': out['v_w_in'], 'v_w_mem_kv': out['v_w_mem_kv'], 'v_q_a_gain': out['v_q_a_gain'], 'v_w_q_b': out['v_w_q_b'], 'v_kv_a_gain': out['v_kv_a_gain'], 'v_w_kv_b': out['v_w_kv_b'], 'v_w_branch_mla': out['v_w_branch_mla'], 'v_w_branch_sb': out['v_w_branch_sb'], 'v_w_branch_mem': out['v_w_branch_mem'], 'v_w_merge_gate': out['v_w_merge_gate'], 'v_b_merge_gate': out['v_b_merge_gate'], 'v_w_out': out['v_w_out'], 'v_ln_gain': out['v_ln_gain'], 'v_ln_bias': out['v_ln_bias']}


def _loss(weights, diff, rest, loss_target):
    with _jax.named_scope("forward"):
        args = {**rest, TWIN_DIFF_INPUT: diff, **{k: w.astype(_WEIGHT_DTYPES[k]) for k, w in weights.items()}}
        y = _forward(args)
    with _jax.named_scope("loss_head"):
        err = _jnp.square(y.astype(_jnp.float32) - loss_target)
        return 0.5 * _jnp.sum(_jnp.mean(err, axis=-1)) if err.ndim else 0.5 * err


def _adamw(w, g, m, v):
    m = ADAM_B1 * m + (1.0 - ADAM_B1) * g
    v = ADAM_B2 * v + (1.0 - ADAM_B2) * _jnp.square(g)
    m_hat = m / (1.0 - ADAM_B1 ** ADAM_STEP)
    v_hat = v / (1.0 - ADAM_B2 ** ADAM_STEP)
    delta = -ADAM_LR * (m_hat / (_jnp.sqrt(v_hat) + ADAM_EPS) + ADAM_WD * w)
    return delta, m, v


def reference(x, mem, w_in, w_mem_kv, q_a_gain, w_q_b, kv_a_gain, w_kv_b, w_branch_mla, w_branch_sb, w_branch_mem, w_merge_gate, b_merge_gate, w_out, ln_gain, ln_bias, loss_target, m_w_in, m_w_mem_kv, m_q_a_gain, m_w_q_b, m_kv_a_gain, m_w_kv_b, m_w_branch_mla, m_w_branch_sb, m_w_branch_mem, m_w_merge_gate, m_b_merge_gate, m_w_out, m_ln_gain, m_ln_bias, v_w_in, v_w_mem_kv, v_q_a_gain, v_w_q_b, v_kv_a_gain, v_w_kv_b, v_w_branch_mla, v_w_branch_sb, v_w_branch_mem, v_w_merge_gate, v_b_merge_gate, v_w_out, v_ln_gain, v_ln_bias):
    given = dict(x=x, mem=mem, w_in=w_in, w_mem_kv=w_mem_kv, q_a_gain=q_a_gain, w_q_b=w_q_b, kv_a_gain=kv_a_gain, w_kv_b=w_kv_b, w_branch_mla=w_branch_mla, w_branch_sb=w_branch_sb, w_branch_mem=w_branch_mem, w_merge_gate=w_merge_gate, b_merge_gate=b_merge_gate, w_out=w_out, ln_gain=ln_gain, ln_bias=ln_bias, loss_target=loss_target, m_w_in=m_w_in, m_w_mem_kv=m_w_mem_kv, m_q_a_gain=m_q_a_gain, m_w_q_b=m_w_q_b, m_kv_a_gain=m_kv_a_gain, m_w_kv_b=m_w_kv_b, m_w_branch_mla=m_w_branch_mla, m_w_branch_sb=m_w_branch_sb, m_w_branch_mem=m_w_branch_mem, m_w_merge_gate=m_w_merge_gate, m_b_merge_gate=m_b_merge_gate, m_w_out=m_w_out, m_ln_gain=m_ln_gain, m_ln_bias=m_ln_bias, v_w_in=v_w_in, v_w_mem_kv=v_w_mem_kv, v_q_a_gain=v_q_a_gain, v_w_q_b=v_w_q_b, v_kv_a_gain=v_kv_a_gain, v_w_kv_b=v_w_kv_b, v_w_branch_mla=v_w_branch_mla, v_w_branch_sb=v_w_branch_sb, v_w_branch_mem=v_w_branch_mem, v_w_merge_gate=v_w_merge_gate, v_b_merge_gate=v_b_merge_gate, v_w_out=v_w_out, v_ln_gain=v_ln_gain, v_ln_bias=v_ln_bias)
    weights = {n: given[n] for n in TWIN_WEIGHTS}
    shared = {n: given[n] for n in SHARED_INPUTS}
    per_example = {n: given[n] for n in ['x', 'mem']}
    grad_fn = _jax.value_and_grad(_loss, argnums=(0, 1))

    def one_microbatch(ex, loss_target):
        ex = dict(ex)
        diff = ex.pop(TWIN_DIFF_INPUT)
        return grad_fn(weights, diff, {**shared, **ex}, loss_target)

    if N_MICROBATCH == 1:
        loss, (grad_w, grad_x) = one_microbatch(per_example, given["loss_target"])
    else:
        def body(carry, xs):
            loss_sum, grad_sum = carry
            l_k, (gw_k, gx_k) = one_microbatch(xs[0], xs[1])
            with _jax.named_scope("update"):
                return (loss_sum + l_k, _jax.tree.map(_jnp.add, grad_sum, gw_k)), gx_k

        init = (_jnp.zeros((), _jnp.float32), _jax.tree.map(_jnp.zeros_like, weights))
        (loss, grad_w), grad_x = _jax.lax.scan(body, init, (per_example, given["loss_target"]))
    with _jax.named_scope("update"):
        delta_w, new_m, new_v = {}, {}, {}
        for n in TWIN_WEIGHTS:
            delta_w[n], new_m[n], new_v[n] = _adamw(weights[n], grad_w[n], given["m_" + n], given["v_" + n])
    return (loss, grad_x, *[grad_w[n] for n in TWIN_WEIGHTS], *[delta_w[n] for n in TWIN_WEIGHTS],
            *[new_m[n] for n in TWIN_WEIGHTS], *[new_v[n] for n in TWIN_WEIGHTS])
```

```python
import functools
import math

import numpy as np
import jax
import jax.numpy as jnp
from jax import lax
from jax.experimental import pallas as pl
from jax.experimental.pallas import tpu as pltpu

F32 = jnp.float32
BF16 = jnp.bfloat16
MESH_ID = pl.DeviceIdType.MESH

D_MODEL = 1024
MEM_LEN = 256
MLA_HEADS = 8
MLA_NOPE = 64
MLA_ROPE = 32
MLA_V = 64
MLA_Q_LORA = 256
MLA_KV_LORA = 128
SB_HEADS = 8
SB_HEAD_DIM = 64
MEM_HEADS = 4
MEM_HEAD_DIM = 128
BRANCH_WIDTH = 512
ROPE_BASE = 10000.0
RMS_EPS = 1e-6
LN_EPS = 1e-5
DEEPNORM_ALPHA = 2.0 ** 0.25
MLA_SCALE = 1.0 / math.sqrt(MLA_NOPE + MLA_ROPE)
SB_SCALE = 1.0 / math.sqrt(SB_HEAD_DIM)
MEM_SCALE = 1.0 / math.sqrt(MEM_HEAD_DIM)

ADAM_LR = 0.001
ADAM_B1 = 0.9
ADAM_B2 = 0.999
ADAM_EPS = 1e-08
ADAM_WD = 0.01
ADAM_STEP = 10

LANES = 128
HALF = 64
N_CHIPS = 4
PACK_COLS = 1024
VMEM_LIMIT = 56 * 1024 * 1024

IN_WIDTH_P = 4096
BLK_LAT, BLK_GATE_A, BLK_QB, BLK_KB, BLK_VB, BLK_GATE_B, BLK_QM, BLK_GATE_M = range(8)
N_MERGE = 3 * D_MODEL
CAT_WIDTH = N_MERGE + IN_WIDTH_P

PACK_ROWS = (("w_in", 1000), ("w_mem_kv", 256), ("w_q_b", 48), ("w_kv_b", 32), ("w_branch_mla", 128),
             ("w_branch_sb", 128), ("w_branch_mem", 128), ("w_merge_gate", 768), ("w_out", 256), ("small", 8))
PACK_TOTAL = sum(r for _, r in PACK_ROWS)
PACK_HALF = PACK_TOTAL // 2
SMALL_SIZES = (("q_a_gain", 256), ("kv_a_gain", 128), ("b_merge_gate", 3072), ("ln_gain", 1024), ("ln_bias", 1024))
SMALL_TOTAL = sum(s for _, s in SMALL_SIZES)
COL_SHARDED = ("w_in", "w_q_b", "w_kv_b", "w_branch_mla", "w_branch_sb", "w_branch_mem", "w_merge_gate")
ROW_SHARDED = ("w_mem_kv", "w_out")
FULL_SHAPES = {"w_in": (1024, 4000), "w_mem_kv": (1024, 1024), "w_q_b": (256, 768), "w_kv_b": (128, 1024),
               "w_branch_mla": (512, 1024), "w_branch_sb": (512, 1024), "w_branch_mem": (512, 1024),
               "w_merge_gate": (1024, 3072), "w_out": (1024, 1024)}


def _cparams(sem=None):
    return pltpu.CompilerParams(dimension_semantics=sem, vmem_limit_bytes=VMEM_LIMIT)


def _dot(a, b):
    return jnp.dot(a, b, preferred_element_type=F32)


def _dot_nt(a, b):
    return lax.dot_general(a, b, (((1,), (1,)), ((), ())), preferred_element_type=F32)


def _dot_tn(a, b):
    return lax.dot_general(a, b, (((0,), (0,)), ((), ())), preferred_element_type=F32)


def _bf(x):
    return x.astype(BF16)


def _sigmoid(x):
    return 1.0 / (1.0 + jnp.exp(-x))


def _matmul(a, b, *, mode, tm, tn, tk, out_dtypes, name, add=None, add_scale=1.0):
    if mode == "nn":
        (m, k), n = a.shape, b.shape[1]
        a_spec = pl.BlockSpec((tm, tk), lambda i, j, kk: (i, kk))
        b_spec = pl.BlockSpec((tk, tn), lambda i, j, kk: (kk, j))
        dot = _dot
    elif mode == "nt":
        (m, k), n = a.shape, b.shape[0]
        a_spec = pl.BlockSpec((tm, tk), lambda i, j, kk: (i, kk))
        b_spec = pl.BlockSpec((tn, tk), lambda i, j, kk: (j, kk))
        dot = _dot_nt
    else:
        (k, m), n = a.shape, b.shape[1]
        a_spec = pl.BlockSpec((tk, tm), lambda i, j, kk: (kk, i))
        b_spec = pl.BlockSpec((tk, tn), lambda i, j, kk: (kk, j))
        dot = _dot_tn
    assert m % tm == 0 and n % tn == 0 and k % tk == 0, (name, m, n, k)
    nk = k // tk
    n_out = len(out_dtypes)
    has_add = add is not None

    def body(*refs):
        a_ref, b_ref = refs[0], refs[1]
        add_ref = refs[2] if has_add else None
        outs = refs[2 + has_add: 2 + has_add + n_out]
        acc = refs[-1]
        kk = pl.program_id(2)

        @pl.when(kk == 0)
        def _():
            acc[...] = jnp.zeros_like(acc)

        acc[...] += dot(_bf(a_ref[...]), _bf(b_ref[...]))

        @pl.when(kk == nk - 1)
        def _():
            r = acc[...]
            if has_add:
                r = r + add_scale * add_ref[...]
            for o in outs:
                o[...] = r.astype(o.dtype)

    in_specs = [a_spec, b_spec]
    args = [a, b]
    if has_add:
        in_specs.append(pl.BlockSpec((tm, tn), lambda i, j, kk: (i, j)))
        args.append(add)
    out_spec = pl.BlockSpec((tm, tn), lambda i, j, kk: (i, j))
    res = pl.pallas_call(
        body, name=name, grid=(m // tm, n // tn, nk),
        in_specs=in_specs, out_specs=[out_spec] * n_out,
        out_shape=[jax.ShapeDtypeStruct((m, n), dt) for dt in out_dtypes],
        scratch_shapes=[pltpu.VMEM((tm, tn), F32)],
        compiler_params=_cparams(("parallel", "parallel", "arbitrary")),
    )(*args)
    return res


def _rope_tables(seq):
    half = MLA_ROPE // 2
    freqs = ROPE_BASE ** (-jnp.arange(half, dtype=F32) / half)
    ang = jnp.arange(seq, dtype=jnp.int32).astype(F32)[:, None] * freqs[None, :]
    cos, sin = jnp.cos(ang), jnp.sin(ang)
    z = lambda w: jnp.zeros((seq, w), F32)
    c_q = jnp.concatenate([jnp.ones((seq, MLA_NOPE), F32), cos, cos, z(32)], axis=1)
    c_k = jnp.concatenate([z(MLA_NOPE), cos, cos, z(32)], axis=1)
    s_lo = jnp.concatenate([z(MLA_NOPE), -sin, z(half), z(32)], axis=1)
    s_hi = jnp.concatenate([z(MLA_NOPE), z(half), sin, z(32)], axis=1)
    return c_q, c_k, s_lo, s_hi


def _rope_fwd(x, c, s_lo, s_hi):
    return x * c + pltpu.roll(x, LANES - 16, 1) * s_lo + pltpu.roll(x, 16, 1) * s_hi


def _rope_bwd(d, c, s_lo, s_hi):
    return d * c - pltpu.roll(d, 16, 1) * s_hi - pltpu.roll(d, LANES - 16, 1) * s_lo


def _rms_fwd(x, g):
    r = lax.rsqrt(jnp.mean(x * x, axis=-1, keepdims=True) + RMS_EPS)
    xn = x * r
    return xn * g, xn, r


def _mla_prep(p32, gq, gkv, wqb, wkvb, tabs, *, t):
    seq = p32.shape[0]

    def body(lat_ref, gq_ref, gkv_ref, wqb_ref, wkvb_ref, cq_ref, ck_ref, slo_ref, shi_ref, q_ref, k_ref, v_ref):
        lat = lat_ref[...]
        slo, shi = slo_ref[...], shi_ref[...]
        nq, _, _ = _rms_fwd(lat[:, 0:MLA_Q_LORA], gq_ref[...])
        qa = _dot(_bf(nq), wqb_ref[...])
        cq = cq_ref[...]
        for h in range(MLA_HEADS):
            blk = qa[:, h * LANES:(h + 1) * LANES]
            q_ref[:, h * LANES:(h + 1) * LANES] = _bf(_rope_fwd(blk, cq, slo, shi))
        nkv, _, _ = _rms_fwd(lat[:, MLA_Q_LORA:MLA_Q_LORA + MLA_KV_LORA], gkv_ref[...])
        kv = _dot(_bf(nkv), wkvb_ref[...])
        kpe = _rope_fwd(lat[:, 384:512], ck_ref[...], slo, shi)
        for h in range(MLA_HEADS):
            k_ref[:, h * LANES:(h + 1) * LANES] = _bf(kv[:, h * LANES:(h + 1) * LANES] + kpe)
        v_ref[...] = _bf(kv[:, MLA_HEADS * LANES:])

    row = lambda w: pl.BlockSpec((t, w), lambda i: (i, 0))
    full = lambda shp: pl.BlockSpec(shp, lambda i: (0, 0))
    return pl.pallas_call(
        body, name="mla_prep", grid=(seq // t,),
        in_specs=[row(512), full((1, MLA_Q_LORA)), full((1, MLA_KV_LORA)), full(wqb.shape), full(wkvb.shape),
                  row(LANES), row(LANES), row(LANES), row(LANES)],
        out_specs=[row(1024), row(1024), row(512)],
        out_shape=[jax.ShapeDtypeStruct((seq, 1024), BF16), jax.ShapeDtypeStruct((seq, 1024), BF16),
                   jax.ShapeDtypeStruct((seq, 512), BF16)],
        compiler_params=_cparams(("parallel",)),
    )(p32, gq, gkv, wqb, wkvb, *tabs)


def _mla_post(p32, dq, dk, dv, gq, gkv, wqb, wkvb, tabs, *, t):
    seq = p32.shape[0]

    def body(lat_ref, dq_ref, dk_ref, dv_ref, gq_ref, gkv_ref, wqb_ref, wkvb_ref, cq_ref, ck_ref, slo_ref, shi_ref,
             dlat_ref, dwqb_ref, dwkvb_ref, dgq_ref, dgkv_ref):
        @pl.when(pl.program_id(0) == 0)
        def _():
            dwqb_ref[...] = jnp.zeros_like(dwqb_ref)
            dwkvb_ref[...] = jnp.zeros_like(dwkvb_ref)
            dgq_ref[...] = jnp.zeros_like(dgq_ref)
            dgkv_ref[...] = jnp.zeros_like(dgkv_ref)

        lat = lat_ref[...]
        slo, shi = slo_ref[...], shi_ref[...]
        cq = cq_ref[...]
        gq_v, gkv_v = gq_ref[...], gkv_ref[...]
        nq, xq, rq = _rms_fwd(lat[:, 0:MLA_Q_LORA], gq_v)
        nkv, xkv, rkv = _rms_fwd(lat[:, MLA_Q_LORA:MLA_Q_LORA + MLA_KV_LORA], gkv_v)

        dqa = jnp.concatenate(
            [_rope_bwd(dq_ref[:, h * LANES:(h + 1) * LANES], cq, slo, shi) for h in range(MLA_HEADS)], axis=1)
        dqa_b = _bf(dqa)
        dwqb_ref[...] += _dot_tn(_bf(nq), dqa_b)
        dnq = _dot_nt(dqa_b, wqb_ref[...])
        dgq_ref[...] += jnp.sum(dnq * xq, axis=0, keepdims=True)
        dxn = dnq * gq_v
        dcq = rq * (dxn - xq * jnp.mean(dxn * xq, axis=-1, keepdims=True))

        dkf = dk_ref[...]
        dkv_b = _bf(jnp.concatenate([dkf, dv_ref[...]], axis=1))
        dwkvb_ref[...] += _dot_tn(_bf(nkv), dkv_b)
        dnkv = _dot_nt(dkv_b, wkvb_ref[...])
        dgkv_ref[...] += jnp.sum(dnkv * xkv, axis=0, keepdims=True)
        dxn = dnkv * gkv_v
        dckv = rkv * (dxn - xkv * jnp.mean(dxn * xkv, axis=-1, keepdims=True))

        dkpe = dkf[:, 0:LANES]
        for h in range(1, MLA_HEADS):
            dkpe = dkpe + dkf[:, h * LANES:(h + 1) * LANES]
        dkr = _rope_bwd(dkpe, ck_ref[...], slo, shi)
        dlat_ref[...] = _bf(jnp.concatenate([dcq, dckv, dkr], axis=1))

    row = lambda w: pl.BlockSpec((t, w), lambda i: (i, 0))
    full = lambda shp: pl.BlockSpec(shp, lambda i: (0, 0))
    return pl.pallas_call(
        body, name="mla_post", grid=(seq // t,),
        in_specs=[row(512), row(1024), row(1024), row(512), full((1, MLA_Q_LORA)), full((1, MLA_KV_LORA)),
                  full(wqb.shape), full(wkvb.shape), row(LANES), row(LANES), row(LANES), row(LANES)],
        out_specs=[row(512), full(wqb.shape), full(wkvb.shape), full((1, MLA_Q_LORA)), full((1, MLA_KV_LORA))],
        out_shape=[jax.ShapeDtypeStruct((seq, 512), BF16), jax.ShapeDtypeStruct(wqb.shape, F32),
                   jax.ShapeDtypeStruct(wkvb.shape, F32), jax.ShapeDtypeStruct((1, MLA_Q_LORA), F32),
                   jax.ShapeDtypeStruct((1, MLA_KV_LORA), F32)],
        compiler_params=_cparams(("arbitrary",)),
    )(p32, dq, dk, dv, gq, gkv, wqb, wkvb, *tabs)


def _causal_masks(t):
    row = lax.broadcasted_iota(jnp.int32, (t, t), 0)
    col = lax.broadcasted_iota(jnp.int32, (t, t), 1)
    return col <= row, col < row


def _mla_fwd(qp, kp, vp, *, t):
    seq = qp.shape[0]
    neg = float(np.finfo(np.float32).min)

    def body(q_ref, k_ref, v_ref, o_ref, lse_ref):
        i = pl.program_id(1)
        incl, _ = _causal_masks(t)
        left = lax.broadcasted_iota(jnp.int32, (t, LANES), 1) < HALF

        def head(off):
            q = q_ref[:, off:off + LANES]

            def block(j, carry, diag):
                m, l, acc = carry
                start = pl.multiple_of(j * t, t)
                s = _dot_nt(q, k_ref[pl.ds(start, t), off:off + LANES]) * MLA_SCALE
                if diag:
                    s = jnp.where(incl, s, neg)
                m_new = jnp.maximum(m, jnp.max(s, axis=-1, keepdims=True))
                a = jnp.exp(m - m_new)
                p = jnp.exp(s - m_new)
                l = a * l + jnp.sum(p, axis=-1, keepdims=True)
                acc = a * acc + _dot(_bf(p), v_ref[pl.ds(start, t), :])
                return m_new, l, acc

            init = (jnp.full((t, 1), -1e30, F32), jnp.zeros((t, 1), F32), jnp.zeros((t, LANES), F32))
            carry = lax.fori_loop(0, i, lambda j, c: block(j, c, False), init)
            m, l, acc = block(i, carry, True)
            return acc / l, m + jnp.log(l)

        o0, lse0 = head(0)
        o1, lse1 = head(LANES)
        o_ref[...] = jnp.where(left, o0, o1)
        lse_ref[...] = jnp.where(left, lse0, lse1)

    return pl.pallas_call(
        body, name="mla_fwd", grid=(MLA_HEADS // 2, seq // t),
        in_specs=[pl.BlockSpec((t, 2 * LANES), lambda p, i: (i, p)), pl.BlockSpec((seq, 2 * LANES), lambda p, i: (0, p)),
                  pl.BlockSpec((seq, LANES), lambda p, i: (0, p))],
        out_specs=[pl.BlockSpec((t, LANES), lambda p, i: (i, p)), pl.BlockSpec((t, LANES), lambda p, i: (i, p))],
        out_shape=[jax.ShapeDtypeStruct((seq, 512), F32), jax.ShapeDtypeStruct((seq, 512), F32)],
        compiler_params=_cparams(("parallel", "parallel")),
    )(qp, kp, vp)


def _mla_bwd(qp, kp, vp, o, lse, do, *, t):
    seq = qp.shape[0]

    def body(q_ref, k_ref, v_ref, o_ref, lse_ref, do_ref, dq_ref, dk_ref, dv_ref):
        i = pl.program_id(1)

        @pl.when(i == 0)
        def _():
            dk_ref[...] = jnp.zeros_like(dk_ref)
            dv_ref[...] = jnp.zeros_like(dv_ref)

        incl, _ = _causal_masks(t)
        left = lax.broadcasted_iota(jnp.int32, (t, LANES), 1) < HALF
        do_f = do_ref[...]
        prod = do_f * o_ref[...]
        lse_v = lse_ref[...]

        def head(off, mask, lane0):
            q = q_ref[:, off:off + LANES]
            do_h = _bf(jnp.where(mask, do_f, 0.0))
            delta = jnp.sum(jnp.where(mask, prod, 0.0), axis=-1, keepdims=True)
            lse_h = lse_v[:, lane0:lane0 + 1]

            def block(j, dq_acc, diag):
                start = pl.multiple_of(j * t, t)
                k = k_ref[pl.ds(start, t), off:off + LANES]
                v = v_ref[pl.ds(start, t), :]
                p = jnp.exp(_dot_nt(q, k) * MLA_SCALE - lse_h)
                if diag:
                    p = jnp.where(incl, p, 0.0)
                dp = _dot_nt(do_h, v)
                ds = _bf(p * (dp - delta) * MLA_SCALE)
                dk_ref[pl.ds(start, t), off:off + LANES] += _dot_tn(ds, q)
                dv_ref[pl.ds(start, t), :] += _dot_tn(_bf(p), do_h)
                return dq_acc + _dot(ds, k)

            dq_acc = lax.fori_loop(0, i, lambda j, c: block(j, c, False), jnp.zeros((t, LANES), F32))
            dq_ref[:, off:off + LANES] = block(i, dq_acc, True)

        head(0, left, 0)
        head(LANES, jnp.logical_not(left), HALF)

    two_t = pl.BlockSpec((t, 2 * LANES), lambda p, i: (i, p))
    two_s = pl.BlockSpec((seq, 2 * LANES), lambda p, i: (0, p))
    pair_t = pl.BlockSpec((t, LANES), lambda p, i: (i, p))
    pair_s = pl.BlockSpec((seq, LANES), lambda p, i: (0, p))
    return pl.pallas_call(
        body, name="mla_bwd", grid=(MLA_HEADS // 2, seq // t),
        in_specs=[two_t, two_s, pair_s, pair_t, pair_t, pair_t],
        out_specs=[two_t, two_s, pair_s],
        out_shape=[jax.ShapeDtypeStruct((seq, 1024), F32), jax.ShapeDtypeStruct((seq, 1024), F32),
                   jax.ShapeDtypeStruct((seq, 512), F32)],
        compiler_params=_cparams(("parallel", "arbitrary")),
    )(qp, kp, vp, o, lse, do)


def _split_bf16(x):
    hi = _bf(x)
    return hi, _bf(x - hi.astype(F32))


def _tri_sum(x, u):
    hi, lo = _split_bf16(x)
    return _dot(hi, u) + _dot(lo, u)


def _sb_scores(qh, k, strict, diag):
    z = _dot_nt(qh, k) * SB_SCALE
    tlog = jnp.log1p(jnp.exp(-jnp.abs(z)))
    log_keep = -jnp.maximum(z, 0.0) - tlog
    if diag:
        log_keep = jnp.where(strict, log_keep, 0.0)
    log_beta = jnp.minimum(z, 0.0) - tlog
    return log_keep, log_beta


def _sb_fwd(pbf, *, t):
    seq = pbf.shape[0]
    qb, kb, vb = BLK_QB * 4, BLK_KB * 4, BLK_VB * 4

    def body(q_ref, k_ref, v_ref, o_ref, tot_ref):
        i = pl.program_id(1)
        _, strict = _causal_masks(t)
        rr = lax.broadcasted_iota(jnp.int32, (t, t), 0)
        cc = lax.broadcasted_iota(jnp.int32, (t, t), 1)
        u_excl = (rr > cc).astype(BF16)
        left = lax.broadcasted_iota(jnp.int32, (t, LANES), 1) < HALF
        q = q_ref[...]

        def head(mask):
            qh = jnp.where(mask, q, jnp.zeros_like(q))

            def block(j, carry, diag):
                c, acc = carry
                start = pl.multiple_of(j * t, t)
                log_keep, log_beta = _sb_scores(qh, k_ref[pl.ds(start, t), :], strict, diag)
                after = _tri_sum(log_keep, u_excl)
                a = jnp.exp(log_beta + after + c)
                if diag:
                    a = jnp.where(strict, a, 0.0)
                acc = acc + _dot(_bf(a), v_ref[pl.ds(start, t), :])
                c = c + after[:, 0:1] + log_keep[:, 0:1]
                return c, acc

            carry = block(i, (jnp.zeros((t, 1), F32), jnp.zeros((t, LANES), F32)), True)
            return lax.fori_loop(0, i, lambda jj, cr: block(i - 1 - jj, cr, False), carry)

        tot0, acc0 = head(left)
        tot1, acc1 = head(jnp.logical_not(left))
        o_ref[...] = jnp.where(left, acc0, acc1)
        tot_ref[...] = jnp.where(left, tot0, tot1)

    pair_t = pl.BlockSpec((t, LANES), lambda p, i: (i, p))
    return pl.pallas_call(
        body, name="sb_fwd", grid=(SB_HEADS // 2, seq // t),
        in_specs=[pl.BlockSpec((t, LANES), lambda p, i: (i, qb + p)), pl.BlockSpec((seq, LANES), lambda p, i: (0, kb + p)),
                  pl.BlockSpec((seq, LANES), lambda p, i: (0, vb + p))],
        out_specs=[pair_t, pair_t],
        out_shape=[jax.ShapeDtypeStruct((seq, 512), F32), jax.ShapeDtypeStruct((seq, 512), F32)],
        compiler_params=_cparams(("parallel", "parallel")),
    )(pbf, pbf, pbf)


def _sb_bwd(pbf, tot, do, *, t):
    seq = pbf.shape[0]
    qb, kb, vb = BLK_QB * 4, BLK_KB * 4, BLK_VB * 4

    def body(q_ref, k_ref, v_ref, tot_ref, do_ref, dq_ref, dk_ref, dv_ref):
        i = pl.program_id(1)

        @pl.when(i == 0)
        def _():
            dk_ref[...] = jnp.zeros_like(dk_ref)
            dv_ref[...] = jnp.zeros_like(dv_ref)

        _, strict = _causal_masks(t)
        rr = lax.broadcasted_iota(jnp.int32, (t, t), 0)
        cc = lax.broadcasted_iota(jnp.int32, (t, t), 1)
        u_upto = (rr <= cc).astype(BF16)
        u_below = (rr < cc).astype(BF16)
        left = lax.broadcasted_iota(jnp.int32, (t, LANES), 1) < HALF
        q = q_ref[...]
        do_f = do_ref[...]
        tot_v = tot_ref[...]

        def head(mask, lane0):
            qh = jnp.where(mask, q, jnp.zeros_like(q))
            do_h = _bf(jnp.where(mask, do_f, 0.0))
            total = tot_v[:, lane0:lane0 + 1]

            def block(j, carry, diag):
                keep_before, g_before, dq_acc = carry
                start = pl.multiple_of(j * t, t)
                k = k_ref[pl.ds(start, t), :]
                v = v_ref[pl.ds(start, t), :]
                log_keep, log_beta = _sb_scores(qh, k, strict, diag)
                upto = _tri_sum(log_keep, u_upto)
                a = jnp.exp(log_beta + (total - keep_before - upto))
                if diag:
                    a = jnp.where(strict, a, 0.0)
                g = _dot_nt(do_h, v) * a
                g_below = _tri_sum(g, u_below)
                beta = jnp.exp(log_beta)
                dz = (g * (1.0 - beta) - (g_before + g_below) * beta) * SB_SCALE
                if diag:
                    dz = jnp.where(strict, dz, 0.0)
                dz = _bf(dz)
                dk_ref[pl.ds(start, t), :] += _dot_tn(dz, qh)
                dv_ref[pl.ds(start, t), :] += _dot_tn(_bf(a), do_h)
                dq_acc = dq_acc + _dot(dz, k)
                keep_before = keep_before + upto[:, t - 1:t]
                g_before = g_before + g_below[:, t - 1:t] + g[:, t - 1:t]
                return keep_before, g_before, dq_acc

            zero = jnp.zeros((t, 1), F32)
            carry = lax.fori_loop(0, i, lambda j, cr: block(j, cr, False), (zero, zero, jnp.zeros((t, LANES), F32)))
            return block(i, carry, True)[2]

        dq_ref[...] = jnp.where(left, head(left, 0), head(jnp.logical_not(left), HALF))

    pair_t = pl.BlockSpec((t, LANES), lambda p, i: (i, p))
    pair_s = pl.BlockSpec((seq, LANES), lambda p, i: (0, p))
    return pl.pallas_call(
        body, name="sb_bwd", grid=(SB_HEADS // 2, seq // t),
        in_specs=[pl.BlockSpec((t, LANES), lambda p, i: (i, qb + p)), pl.BlockSpec((seq, LANES), lambda p, i: (0, kb + p)),
                  pl.BlockSpec((seq, LANES), lambda p, i: (0, vb + p)), pair_t, pair_t],
        out_specs=[pair_t, pair_s, pair_s],
        out_shape=[jax.ShapeDtypeStruct((seq, 512), F32)] * 3,
        compiler_params=_cparams(("parallel", "arbitrary")),
    )(pbf, pbf, pbf, tot, do)


def _mem_probs(q, k):
    s = _dot_nt(q, k) * MEM_SCALE
    e = jnp.exp(s - jnp.max(s, axis=-1, keepdims=True))
    return e / jnp.sum(e, axis=-1, keepdims=True)


def _mem_fwd(pbf, mkv, *, t):
    seq = pbf.shape[0]

    def body(q_ref, kv_ref, o_ref):
        for h in range(MEM_HEADS):
            lo, hi = h * LANES, (h + 1) * LANES
            p = _mem_probs(q_ref[:, lo:hi], kv_ref[:, lo:hi])
            o_ref[:, lo:hi] = _dot(_bf(p), kv_ref[:, 512 + lo:512 + hi])

    return pl.pallas_call(
        body, name="mem_fwd", grid=(seq // t,),
        in_specs=[pl.BlockSpec((t, 512), lambda i: (i, BLK_QM)), pl.BlockSpec((MEM_LEN, 1024), lambda i: (0, 0))],
        out_specs=pl.BlockSpec((t, 512), lambda i: (i, 0)),
        out_shape=jax.ShapeDtypeStruct((seq, 512), F32),
        compiler_params=_cparams(("parallel",)),
    )(pbf, mkv)


def _mem_bwd(pbf, mkv, do, *, t):
    seq = pbf.shape[0]

    def body(q_ref, kv_ref, do_ref, dq_ref, dkv_ref):
        @pl.when(pl.program_id(0) == 0)
        def _():
            dkv_ref[...] = jnp.zeros_like(dkv_ref)

        for h in range(MEM_HEADS):
            lo, hi = h * LANES, (h + 1) * LANES
            q, k, v = q_ref[:, lo:hi], kv_ref[:, lo:hi], kv_ref[:, 512 + lo:512 + hi]
            do_h = _bf(do_ref[:, lo:hi])
            p = _mem_probs(q, k)
            dp = _dot_nt(do_h, v)
            ds = _bf(p * (dp - jnp.sum(dp * p, axis=-1, keepdims=True)) * MEM_SCALE)
            dq_ref[:, lo:hi] = _dot(ds, k)
            dkv_ref[:, lo:hi] += _dot_tn(ds, q)
            dkv_ref[:, 512 + lo:512 + hi] += _dot_tn(_bf(p), do_h)

    return pl.pallas_call(
        body, name="mem_bwd", grid=(seq // t,),
        in_specs=[pl.BlockSpec((t, 512), lambda i: (i, BLK_QM)), pl.BlockSpec((MEM_LEN, 1024), lambda i: (0, 0)),
                  pl.BlockSpec((t, 512), lambda i: (i, 0))],
        out_specs=[pl.BlockSpec((t, 512), lambda i: (i, 0)), pl.BlockSpec((MEM_LEN, 1024), lambda i: (0, 0))],
        out_shape=[jax.ShapeDtypeStruct((seq, 512), F32), jax.ShapeDtypeStruct((MEM_LEN, 1024), F32)],
        compiler_params=_cparams(("arbitrary",)),
    )(pbf, mkv, do)


def _mid(x, tgt, o_a, o_b, o_m, p32, wmg, bmg, wba, wbb, wbm, wout, ln_g, ln_b, *, t):
    seq = x.shape[0]
    inv_d = 1.0 / D_MODEL

    def body(x_ref, t_ref, oa_ref, ob_ref, om_ref, ga_ref, gb_ref, gm_ref, wmg_ref, bmg_ref, wba_ref, wbb_ref,
             wbm_ref, wout_ref, lg_ref, lb_ref,
             du_ref, mrg_ref, dgp_ref, ha_ref, hb_ref, hm_ref, dya_ref, dyb_ref, dym_ref, doa_ref, dob_ref, dom_ref,
             dga_ref, dgb_ref, dgm_ref, dgain_ref, dbias_ref, dbmg_ref, loss_ref):
        @pl.when(pl.program_id(0) == 0)
        def _():
            dgain_ref[...] = jnp.zeros_like(dgain_ref)
            dbias_ref[...] = jnp.zeros_like(dbias_ref)
            dbmg_ref[...] = jnp.zeros_like(dbmg_ref)
            loss_ref[...] = jnp.zeros_like(loss_ref)

        xv = x_ref[...]
        gate = _sigmoid(_dot(_bf(xv), wmg_ref[...]) + bmg_ref[...])

        branches = []
        merged = None
        for b, (o_ref, g_ref, w_ref, h_ref) in enumerate(((oa_ref, ga_ref, wba_ref, ha_ref), (ob_ref, gb_ref, wbb_ref, hb_ref),
                                                         (om_ref, gm_ref, wbm_ref, hm_ref))):
            o, gt = o_ref[...], g_ref[...]
            sg = _sigmoid(gt)
            silu = gt * sg
            h = _bf(o * silu)
            h_ref[...] = h
            y = _dot(h, w_ref[...])
            g_b = gate[:, b * D_MODEL:(b + 1) * D_MODEL]
            term = g_b * y
            merged = term if merged is None else merged + term
            branches.append((o, gt, sg, silu, y, g_b))
        mrg_b = _bf(merged)
        mrg_ref[...] = mrg_b

        u = DEEPNORM_ALPHA * xv + _dot(mrg_b, wout_ref[...])
        mu = jnp.mean(u, axis=-1, keepdims=True)
        uc = u - mu
        rstd = lax.rsqrt(jnp.mean(uc * uc, axis=-1, keepdims=True) + LN_EPS)
        xhat = uc * rstd
        lg = lg_ref[...]
        y_out = xhat * lg + lb_ref[...]
        err = y_out - t_ref[...]
        loss_ref[...] += 0.5 * jnp.sum(jnp.mean(err * err, axis=-1, keepdims=True), axis=0, keepdims=True)
        dy = err * inv_d
        dgain_ref[...] += jnp.sum(dy * xhat, axis=0, keepdims=True)
        dbias_ref[...] += jnp.sum(dy, axis=0, keepdims=True)
        dxh = dy * lg
        du = rstd * (dxh - jnp.mean(dxh, axis=-1, keepdims=True) - xhat * jnp.mean(dxh * xhat, axis=-1, keepdims=True))
        du_ref[...] = du

        dmerged = _dot_nt(_bf(du), wout_ref[...])
        outs = ((dya_ref, doa_ref, dga_ref, wba_ref), (dyb_ref, dob_ref, dgb_ref, wbb_ref), (dym_ref, dom_ref, dgm_ref, wbm_ref))
        dgp = []
        for (o, gt, sg, silu, y, g_b), (dy_ref, do_ref, dg_ref, w_ref) in zip(branches, outs):
            dyb = _bf(dmerged * g_b)
            dy_ref[...] = dyb
            dgp.append(dmerged * y * g_b * (1.0 - g_b))
            dh = _dot_nt(dyb, w_ref[...])
            do_ref[...] = dh * silu
            dg_ref[...] = _bf(dh * o * (sg * (1.0 + gt * (1.0 - sg))))
        dgp = jnp.concatenate(dgp, axis=1)
        dgp_ref[...] = _bf(dgp)
        dbmg_ref[...] += jnp.sum(dgp, axis=0, keepdims=True)

    row = lambda w: pl.BlockSpec((t, w), lambda i: (i, 0))
    pblk = lambda c: pl.BlockSpec((t, 512), lambda i: (i, c))
    full = lambda shp: pl.BlockSpec(shp, lambda i: (0, 0))
    sds = jax.ShapeDtypeStruct
    return pl.pallas_call(
        body, name="mid", grid=(seq // t,),
        in_specs=[row(1024), row(1024), row(512), row(512), row(512), pblk(BLK_GATE_A), pblk(BLK_GATE_B), pblk(BLK_GATE_M),
                  full(wmg.shape), full((1, N_MERGE)), full(wba.shape), full(wbb.shape), full(wbm.shape), full(wout.shape),
                  full((1, D_MODEL)), full((1, D_MODEL))],
        out_specs=[row(1024), row(1024), row(N_MERGE), row(512), row(512), row(512), row(1024), row(1024), row(1024),
                   row(512), row(512), row(512), row(512), row(512), row(512),
                   full((1, D_MODEL)), full((1, D_MODEL)), full((1, N_MERGE)), full((1, 1))],
        out_shape=[sds((seq, 1024), F32), sds((seq, 1024), BF16), sds((seq, N_MERGE), BF16),
                   sds((seq, 512), BF16), sds((seq, 512), BF16), sds((seq, 512), BF16),
                   sds((seq, 1024), BF16), sds((seq, 1024), BF16), sds((seq, 1024), BF16),
                   sds((seq, 512), F32), sds((seq, 512), F32), sds((seq, 512), F32),
                   sds((seq, 512), BF16), sds((seq, 512), BF16), sds((seq, 512), BF16),
                   sds((1, D_MODEL), F32), sds((1, D_MODEL), F32), sds((1, N_MERGE), F32), sds((1, 1), F32)],
        compiler_params=_cparams(("arbitrary",)),
    )(x, tgt, o_a, o_b, o_m, p32, p32, p32, wmg, bmg, wba, wbb, wbm, wout, ln_g, ln_b)


def _primed_weights(w):
    w_in = w["w_in"]
    zc = lambda n: jnp.zeros((D_MODEL, n), w_in.dtype)
    w_in_p = jnp.concatenate([w_in[:, 0:384], zc(64), w_in[:, 384:416], zc(32), w_in[:, 416:]], axis=1)
    wqb = jnp.pad(w["w_q_b"].reshape(MLA_Q_LORA, MLA_HEADS, 96), ((0, 0), (0, 0), (0, 32))).reshape(MLA_Q_LORA, 1024)
    kv3 = w["w_kv_b"].reshape(MLA_KV_LORA, MLA_HEADS, 128)
    wk = jnp.pad(kv3[:, :, :MLA_NOPE], ((0, 0), (0, 0), (0, 64))).reshape(MLA_KV_LORA, 1024)
    wv = kv3[:, :, MLA_NOPE:].reshape(MLA_KV_LORA, 512)
    return w_in_p, wqb, jnp.concatenate([wk, wv], axis=1)


def _local_step(x, mem, tgt, w, small, *, t_attn, t_row, t_mm):
    seq = x.shape[0]
    w_in_p, wqb, wkvb = _primed_weights(w)
    tabs = _rope_tables(seq)
    wmg, wout = w["w_merge_gate"], w["w_out"]
    wba, wbb, wbm = w["w_branch_mla"], w["w_branch_sb"], w["w_branch_mem"]

    p32, pbf = _matmul(x, w_in_p, mode="nn", tm=t_mm, tn=512, tk=D_MODEL, out_dtypes=(F32, BF16), name="proj_in")
    qp, kp, vp = _mla_prep(p32, small["q_a_gain"], small["kv_a_gain"], wqb, wkvb, tabs, t=t_row)
    o_a, lse = _mla_fwd(qp, kp, vp, t=t_attn)
    o_b, keep_total = _sb_fwd(pbf, t=t_attn)
    (mkv,) = _matmul(mem, w["w_mem_kv"], mode="nn", tm=MEM_LEN, tn=512, tk=D_MODEL, out_dtypes=(BF16,), name="mem_kv")
    o_m = _mem_fwd(pbf, mkv, t=t_row)

    (du, merged, dgpre, h_a, h_b, h_m, dy_a, dy_b, dy_m, do_a, do_b, do_m, dgate_a, dgate_b, dgate_m,
     d_ln_g, d_ln_b, d_bmg, loss) = _mid(x, tgt, o_a, o_b, o_m, p32, wmg, small["b_merge_gate"], wba, wbb, wbm, wout,
                                         small["ln_gain"], small["ln_bias"], t=t_row)

    dqp, dkp, dvp = _mla_bwd(qp, kp, vp, o_a, lse, do_a, t=t_attn)
    dlat, d_wqb, d_wkvb, d_gq, d_gkv = _mla_post(p32, dqp, dkp, dvp, small["q_a_gain"], small["kv_a_gain"], wqb, wkvb, tabs,
                                                 t=t_row)
    dq_b, dk_b, dv_b = _sb_bwd(pbf, keep_total, do_b, t=t_attn)
    dq_m, dmkv = _mem_bwd(pbf, mkv, do_m, t=t_row)

    dcat = jnp.concatenate([dgpre, dlat, dgate_a, _bf(dq_b), _bf(dk_b), _bf(dv_b), dgate_b, _bf(dq_m), dgate_m], axis=1)
    wcat = jnp.concatenate([wmg, w_in_p], axis=1)
    (grad_x,) = _matmul(dcat, wcat, mode="nt", tm=t_mm, tn=D_MODEL, tk=1024, out_dtypes=(F32,), name="grad_x",
                        add=du, add_scale=DEEPNORM_ALPHA)
    (d_wcat,) = _matmul(x, dcat, mode="tn", tm=512, tn=1024, tk=t_mm, out_dtypes=(F32,), name="grad_w_cat")
    (d_wout,) = _matmul(merged, du, mode="tn", tm=512, tn=1024, tk=t_mm, out_dtypes=(F32,), name="grad_w_out")
    (d_wba,) = _matmul(h_a, dy_a, mode="tn", tm=512, tn=1024, tk=t_mm, out_dtypes=(F32,), name="grad_w_branch_a")
    (d_wbb,) = _matmul(h_b, dy_b, mode="tn", tm=512, tn=1024, tk=t_mm, out_dtypes=(F32,), name="grad_w_branch_b")
    (d_wbm,) = _matmul(h_m, dy_m, mode="tn", tm=512, tn=1024, tk=t_mm, out_dtypes=(F32,), name="grad_w_branch_m")
    (d_wmemkv,) = _matmul(mem, dmkv, mode="tn", tm=512, tn=1024, tk=MEM_LEN, out_dtypes=(F32,), name="grad_w_mem_kv")

    d_wmg = d_wcat[:, :N_MERGE]
    d_winp = d_wcat[:, N_MERGE:]
    d_win = jnp.concatenate([d_winp[:, 0:384], d_winp[:, 448:480], d_winp[:, 512:]], axis=1)
    d_wq = d_wqb.reshape(MLA_Q_LORA, MLA_HEADS, 128)[:, :, :96].reshape(MLA_Q_LORA, 768)
    d_wk = d_wkvb[:, :1024].reshape(MLA_KV_LORA, MLA_HEADS, 128)[:, :, :MLA_NOPE]
    d_wv = d_wkvb[:, 1024:].reshape(MLA_KV_LORA, MLA_HEADS, MLA_V)
    d_wkv = jnp.concatenate([d_wk, d_wv], axis=2).reshape(MLA_KV_LORA, 1024)
    grads = {"w_in": d_win, "w_mem_kv": d_wmemkv, "w_q_b": d_wq, "w_kv_b": d_wkv, "w_branch_mla": d_wba,
             "w_branch_sb": d_wbb, "w_branch_mem": d_wbm, "w_merge_gate": d_wmg, "w_out": d_wout}
    small_grads = {"q_a_gain": d_gq, "kv_a_gain": d_gkv, "b_merge_gate": d_bmg, "ln_gain": d_ln_g, "ln_bias": d_ln_b}
    return loss[0, 0], grad_x, grads, small_grads


def _pack_shards(shards, small):
    flat_small = jnp.concatenate([small[n].reshape(-1) for n, _ in SMALL_SIZES])
    flat_small = jnp.pad(flat_small, (0, 8 * PACK_COLS - SMALL_TOTAL)).reshape(8, PACK_COLS)
    parts = [shards[n].reshape(-1, PACK_COLS) for n, _ in PACK_ROWS[:-1]] + [flat_small]
    return jnp.concatenate(parts, axis=0)


def _unpack_shards(pack, shapes):
    out, r0 = {}, 0
    for n, rows in PACK_ROWS[:-1]:
        out[n] = pack[r0:r0 + rows].reshape(shapes[n])
        r0 += rows
    flat = pack[r0:r0 + 8].reshape(-1)
    small, c0 = {}, 0
    for n, size in SMALL_SIZES:
        small[n] = flat[c0:c0 + size].reshape(1, size)
        c0 += size
    return out, small


def _split_by_chip(name, full):
    r, c = full.shape
    if name in COL_SHARDED:
        s = full.reshape(r, N_CHIPS, c // N_CHIPS).transpose(1, 0, 2)
    else:
        s = full.reshape(N_CHIPS, r // N_CHIPS, c)
    return s.reshape(N_CHIPS, -1, PACK_COLS)


def _join_chips(name, packed4):
    r, c = FULL_SHAPES[name]
    if name in COL_SHARDED:
        return packed4.reshape(N_CHIPS, r, c // N_CHIPS).transpose(1, 0, 2).reshape(r, c)
    return packed4.reshape(r, c)


def _place():
    x, y, c = lax.axis_index("x"), lax.axis_index("y"), lax.axis_index("c")
    return x, y, c


def _other_chips(x, y):
    return ((1 - x, y), (x, 1 - y), (1 - x, 1 - y))


def _gather_weights(wpack):
    rows = wpack.shape[0]
    chunk = rows // 4

    def body(w_ref, out_ref, wb_ref, send_sems, recv_sems, local_sem):
        x, y, c = _place()
        me = 2 * x + y
        for r in range(4):
            wb_ref[r * chunk:(r + 1) * chunk, :] = _bf(w_ref[r * chunk:(r + 1) * chunk, :])
        mine = pltpu.make_async_copy(wb_ref, out_ref.at[me], local_sem)
        mine.start()
        copies = []
        for k, (px, py) in enumerate(_other_chips(x, y)):
            cp = pltpu.make_async_remote_copy(src_ref=wb_ref, dst_ref=out_ref.at[me], send_sem=send_sems.at[k],
                                              recv_sem=recv_sems.at[k], device_id=(px, py, c), device_id_type=MESH_ID)
            cp.start()
            copies.append(cp)
        for k, (px, py) in enumerate(_other_chips(x, y)):
            pltpu.make_async_remote_copy(src_ref=wb_ref, dst_ref=out_ref.at[2 * px + py], send_sem=send_sems.at[k],
                                         recv_sem=recv_sems.at[k], device_id=(px, py, c), device_id_type=MESH_ID).wait_recv()
        for cp in copies:
            cp.wait_send()
        mine.wait()

    return pl.pallas_call(
        body, name="gather_weights",
        in_specs=[pl.BlockSpec(memory_space=pltpu.VMEM)],
        out_specs=pl.BlockSpec(memory_space=pltpu.HBM),
        out_shape=jax.ShapeDtypeStruct((N_CHIPS, rows, PACK_COLS), BF16),
        scratch_shapes=[pltpu.VMEM((rows, PACK_COLS), BF16), pltpu.SemaphoreType.DMA((3,)), pltpu.SemaphoreType.DMA((3,)),
                        pltpu.SemaphoreType.DMA],
        compiler_params=pltpu.CompilerParams(vmem_limit_bytes=VMEM_LIMIT),
    )(wpack)


def _to_sibling_half(gpack):
    def body(g_ref, out_ref, send_sems, recv_sems):
        x, y, c = _place()
        theirs = pl.ds(pl.multiple_of((1 - c) * PACK_HALF, 8), PACK_HALF)
        copies = [pltpu.make_async_remote_copy(src_ref=g_ref.at[j, theirs, :], dst_ref=out_ref.at[j],
                                               send_sem=send_sems.at[j], recv_sem=recv_sems.at[j],
                                               device_id=(x, y, 1 - c), device_id_type=MESH_ID) for j in range(N_CHIPS)]
        for cp in copies:
            cp.start()
        for cp in copies:
            cp.wait()

    return pl.pallas_call(
        body, name="rs_sibling",
        in_specs=[pl.BlockSpec(memory_space=pltpu.HBM)],
        out_specs=pl.BlockSpec(memory_space=pltpu.HBM),
        out_shape=jax.ShapeDtypeStruct((N_CHIPS, PACK_HALF, PACK_COLS), F32),
        scratch_shapes=[pltpu.SemaphoreType.DMA((N_CHIPS,)), pltpu.SemaphoreType.DMA((N_CHIPS,))],
    )(gpack)


def _add_sibling(gpack, got):
    tr = PACK_HALF // 4

    def body(c_ref, g_ref, r_ref, o_ref):
        o_ref[...] = g_ref[...] + r_ref[...]

    grid_spec = pltpu.PrefetchScalarGridSpec(
        num_scalar_prefetch=1, grid=(N_CHIPS, 4),
        in_specs=[pl.BlockSpec((1, tr, PACK_COLS), lambda j, i, c_ref: (j, c_ref[0] * 4 + i, 0)),
                  pl.BlockSpec((1, tr, PACK_COLS), lambda j, i, c_ref: (j, i, 0))],
        out_specs=pl.BlockSpec((1, tr, PACK_COLS), lambda j, i, c_ref: (j, i, 0)))
    return pl.pallas_call(
        body, name="rs_add_sibling", grid_spec=grid_spec,
        out_shape=jax.ShapeDtypeStruct((N_CHIPS, PACK_HALF, PACK_COLS), F32),
        compiler_params=_cparams(("parallel", "parallel")),
    )(lax.axis_index("c").astype(jnp.int32).reshape(1), gpack, got)


def _to_owner_chips(chipsum):
    def body(s_ref, out_ref, send_sems, recv_sems, local_sem):
        x, y, c = _place()
        me = 2 * x + y
        mine = pltpu.make_async_copy(s_ref.at[me], out_ref.at[me], local_sem)
        mine.start()
        copies = []
        for k, (px, py) in enumerate(_other_chips(x, y)):
            cp = pltpu.make_async_remote_copy(src_ref=s_ref.at[2 * px + py], dst_ref=out_ref.at[me], send_sem=send_sems.at[k],
                                              recv_sem=recv_sems.at[k], device_id=(px, py, c), device_id_type=MESH_ID)
            cp.start()
            copies.append(cp)
        for k, (px, py) in enumerate(_other_chips(x, y)):
            pltpu.make_async_remote_copy(src_ref=s_ref.at[me], dst_ref=out_ref.at[2 * px + py], send_sem=send_sems.at[k],
                                         recv_sem=recv_sems.at[k], device_id=(px, py, c), device_id_type=MESH_ID).wait_recv()
        for cp in copies:
            cp.wait_send()
        mine.wait()

    return pl.pallas_call(
        body, name="rs_chips",
        in_specs=[pl.BlockSpec(memory_space=pltpu.HBM)],
        out_specs=pl.BlockSpec(memory_space=pltpu.HBM),
        out_shape=jax.ShapeDtypeStruct(chipsum.shape, F32),
        scratch_shapes=[pltpu.SemaphoreType.DMA((3,)), pltpu.SemaphoreType.DMA((3,)), pltpu.SemaphoreType.DMA],
    )(chipsum)


def _add_chips(parts):
    tr = PACK_HALF // 4

    def body(p_ref, o_ref):
        o_ref[...] = ((p_ref[0] + p_ref[1]) + p_ref[2]) + p_ref[3]

    return pl.pallas_call(
        body, name="rs_add_chips", grid=(4,),
        in_specs=[pl.BlockSpec((N_CHIPS, tr, PACK_COLS), lambda i: (0, i, 0))],
        out_specs=pl.BlockSpec((tr, PACK_COLS), lambda i: (i, 0)),
        out_shape=jax.ShapeDtypeStruct((PACK_HALF, PACK_COLS), F32),
        compiler_params=_cparams(("parallel",)),
    )(parts)


def _swap_halves(half):
    def body(h_ref, out_ref, send_sem, recv_sem, local_sem):
        x, y, c = _place()
        my_rows = pl.ds(pl.multiple_of(c * PACK_HALF, 8), PACK_HALF)
        mine = pltpu.make_async_copy(h_ref, out_ref.at[my_rows, :], local_sem)
        mine.start()
        cp = pltpu.make_async_remote_copy(src_ref=h_ref, dst_ref=out_ref.at[my_rows, :], send_sem=send_sem, recv_sem=recv_sem,
                                          device_id=(x, y, 1 - c), device_id_type=MESH_ID)
        cp.start()
        their_rows = pl.ds(pl.multiple_of((1 - c) * PACK_HALF, 8), PACK_HALF)
        pltpu.make_async_remote_copy(src_ref=h_ref, dst_ref=out_ref.at[their_rows, :], send_sem=send_sem, recv_sem=recv_sem,
                                     device_id=(x, y, 1 - c), device_id_type=MESH_ID).wait_recv()
        cp.wait_send()
        mine.wait()

    return pl.pallas_call(
        body, name="rs_swap_halves",
        in_specs=[pl.BlockSpec(memory_space=pltpu.HBM)],
        out_specs=pl.BlockSpec(memory_space=pltpu.HBM),
        out_shape=jax.ShapeDtypeStruct((PACK_TOTAL, PACK_COLS), F32),
        scratch_shapes=[pltpu.SemaphoreType.DMA, pltpu.SemaphoreType.DMA, pltpu.SemaphoreType.DMA],
    )(half)


def _adamw(w, g, m, v):
    tr = PACK_TOTAL // 8

    def body(w_ref, g_ref, m_ref, v_ref, d_ref, nm_ref, nv_ref):
        gv = g_ref[...]
        m_new = ADAM_B1 * m_ref[...] + (1.0 - ADAM_B1) * gv
        v_new = ADAM_B2 * v_ref[...] + (1.0 - ADAM_B2) * (gv * gv)
        m_hat = m_new / (1.0 - ADAM_B1 ** ADAM_STEP)
        v_hat = v_new / (1.0 - ADAM_B2 ** ADAM_STEP)
        d_ref[...] = -ADAM_LR * (m_hat / (jnp.sqrt(v_hat) + ADAM_EPS) + ADAM_WD * w_ref[...])
        nm_ref[...] = m_new
        nv_ref[...] = v_new

    blk = pl.BlockSpec((tr, PACK_COLS), lambda i: (i, 0))
    return pl.pallas_call(
        body, name="adamw", grid=(8,),
        in_specs=[blk] * 4, out_specs=[blk] * 3,
        out_shape=[jax.ShapeDtypeStruct((PACK_TOTAL, PACK_COLS), F32)] * 3,
        compiler_params=_cparams(("parallel",)),
    )(w, g, m, v)


WEIGHT_NAMES = ("w_in", "w_mem_kv", "q_a_gain", "w_q_b", "kv_a_gain", "w_kv_b", "w_branch_mla", "w_branch_sb",
                "w_branch_mem", "w_merge_gate", "b_merge_gate", "w_out", "ln_gain", "ln_bias")
BIG_NAMES = tuple(n for n, _ in PACK_ROWS[:-1])
SMALL_NAMES = tuple(n for n, _ in SMALL_SIZES)


def kernel(x, mem, w_in, w_mem_kv, q_a_gain, w_q_b, kv_a_gain, w_kv_b, w_branch_mla, w_branch_sb, w_branch_mem, w_merge_gate, b_merge_gate, w_out, ln_gain, ln_bias, loss_target, m_w_in, m_w_mem_kv, m_q_a_gain, m_w_q_b, m_kv_a_gain, m_w_kv_b, m_w_branch_mla, m_w_branch_sb, m_w_branch_mem, m_w_merge_gate, m_b_merge_gate, m_w_out, m_ln_gain, m_ln_bias, v_w_in, v_w_mem_kv, v_q_a_gain, v_w_q_b, v_kv_a_gain, v_w_kv_b, v_w_branch_mla, v_w_branch_sb, v_w_branch_mem, v_w_merge_gate, v_b_merge_gate, v_w_out, v_ln_gain, v_ln_bias):
    weights = dict(zip(WEIGHT_NAMES, (w_in, w_mem_kv, q_a_gain, w_q_b, kv_a_gain, w_kv_b, w_branch_mla, w_branch_sb,
                                      w_branch_mem, w_merge_gate, b_merge_gate, w_out, ln_gain, ln_bias)))
    mom1 = dict(zip(WEIGHT_NAMES, (m_w_in, m_w_mem_kv, m_q_a_gain, m_w_q_b, m_kv_a_gain, m_w_kv_b, m_w_branch_mla,
                                   m_w_branch_sb, m_w_branch_mem, m_w_merge_gate, m_b_merge_gate, m_w_out, m_ln_gain,
                                   m_ln_bias)))
    mom2 = dict(zip(WEIGHT_NAMES, (v_w_in, v_w_mem_kv, v_q_a_gain, v_w_q_b, v_kv_a_gain, v_w_kv_b, v_w_branch_mla,
                                   v_w_branch_sb, v_w_branch_mem, v_w_merge_gate, v_b_merge_gate, v_w_out, v_ln_gain,
                                   v_ln_bias)))
    shard_shapes = {n: weights[n].shape for n in BIG_NAMES}

    def pack(d):
        return _pack_shards({n: d[n] for n in BIG_NAMES}, {n: d[n] for n in SMALL_NAMES})

    wpack, mpack, vpack = pack(weights), pack(mom1), pack(mom2)

    gathered = _gather_weights(wpack)
    full_w, r0 = {}, 0
    for n, rows in PACK_ROWS[:-1]:
        full_w[n] = _join_chips(n, gathered[:, r0:r0 + rows])
        r0 += rows
    small = {n: weights[n] for n in SMALL_NAMES}

    loss, grad_x, grads, small_grads = _local_step(x[0], mem[0], loss_target[0], full_w, small,
                                                   t_attn=256, t_row=256, t_mm=512)

    flat_small = jnp.concatenate([small_grads[n].reshape(-1) for n in SMALL_NAMES])
    flat_small = jnp.pad(flat_small, (0, 8 * PACK_COLS - SMALL_TOTAL)).reshape(1, 8, PACK_COLS)
    gpack = jnp.concatenate([_split_by_chip(n, grads[n]) for n in BIG_NAMES]
                            + [jnp.broadcast_to(flat_small, (N_CHIPS, 8, PACK_COLS))], axis=1)

    got = _to_sibling_half(gpack)
    chipsum = _add_sibling(gpack, got)
    parts = _to_owner_chips(chipsum)
    half = _add_chips(parts)
    gred = _swap_halves(half)
    delta, new_m, new_v = _adamw(wpack, gred, mpack, vpack)

    loss = lax.psum(loss, ("x", "y", "c"))
    outs = [loss, grad_x[None]]
    for packed in (gred, delta, new_m, new_v):
        big, sm = _unpack_shards(packed, shard_shapes)
        outs.extend(big[n] if n in big else sm[n] for n in WEIGHT_NAMES)
    return tuple(outs)
```

```python
import functools
import math

import numpy as np
import jax
import jax.numpy as jnp
from jax import lax
from jax.experimental import pallas as pl
from jax.experimental.pallas import tpu as pltpu

F32 = jnp.float32
BF16 = jnp.bfloat16
MESH_ID = pl.DeviceIdType.MESH

D_MODEL = 1024
MEM_LEN = 256
MLA_HEADS = 8
MLA_NOPE = 64
MLA_ROPE = 32
MLA_V = 64
MLA_Q_LORA = 256
MLA_KV_LORA = 128
SB_HEADS = 8
SB_HEAD_DIM = 64
MEM_HEADS = 4
MEM_HEAD_DIM = 128
BRANCH_WIDTH = 512
ROPE_BASE = 10000.0
RMS_EPS = 1e-6
LN_EPS = 1e-5
DEEPNORM_ALPHA = 2.0 ** 0.25
MLA_SCALE = 1.0 / math.sqrt(MLA_NOPE + MLA_ROPE)
SB_SCALE = 1.0 / math.sqrt(SB_HEAD_DIM)
MEM_SCALE = 1.0 / math.sqrt(MEM_HEAD_DIM)

ADAM_LR = 0.001
ADAM_B1 = 0.9
ADAM_B2 = 0.999
ADAM_EPS = 1e-08
ADAM_WD = 0.01
ADAM_STEP = 10

LANES = 128
HALF = 64
N_CHIPS = 4
PACK_COLS = 1024
VMEM_LIMIT = 56 * 1024 * 1024

IN_WIDTH_P = 4096
BLK_LAT, BLK_GATE_A, BLK_QB, BLK_KB, BLK_VB, BLK_GATE_B, BLK_QM, BLK_GATE_M = range(8)
N_MERGE = 3 * D_MODEL
CAT_WIDTH = N_MERGE + IN_WIDTH_P

PACK_ROWS = (("w_in", 1000), ("w_mem_kv", 256), ("w_q_b", 48), ("w_kv_b", 32), ("w_branch_mla", 128),
             ("w_branch_sb", 128), ("w_branch_mem", 128), ("w_merge_gate", 768), ("w_out", 256), ("small", 8))
PACK_TOTAL = sum(r for _, r in PACK_ROWS)
PACK_HALF = PACK_TOTAL // 2
SMALL_SIZES = (("q_a_gain", 256), ("kv_a_gain", 128), ("b_merge_gate", 3072), ("ln_gain", 1024), ("ln_bias", 1024))
SMALL_TOTAL = sum(s for _, s in SMALL_SIZES)
COL_SHARDED = ("w_in", "w_q_b", "w_kv_b", "w_branch_mla", "w_branch_sb", "w_branch_mem", "w_merge_gate")
ROW_SHARDED = ("w_mem_kv", "w_out")
FULL_SHAPES = {"w_in": (1024, 4000), "w_mem_kv": (1024, 1024), "w_q_b": (256, 768), "w_kv_b": (128, 1024),
               "w_branch_mla": (512, 1024), "w_branch_sb": (512, 1024), "w_branch_mem": (512, 1024),
               "w_merge_gate": (1024, 3072), "w_out": (1024, 1024)}


def _cparams(sem=None):
    return pltpu.CompilerParams(dimension_semantics=sem, vmem_limit_bytes=VMEM_LIMIT)


def _dot(a, b):
    return jnp.dot(a, b, preferred_element_type=F32)


def _dot_nt(a, b):
    return lax.dot_general(a, b, (((1,), (1,)), ((), ())), preferred_element_type=F32)


def _dot_tn(a, b):
    return lax.dot_general(a, b, (((0,), (0,)), ((), ())), preferred_element_type=F32)


def _bf(x):
    return x.astype(BF16)


def _sigmoid(x):
    return 1.0 / (1.0 + jnp.exp(-x))


def _matmul(a, b, *, mode, tm, tn, tk, out_dtypes, name, add=None, add_scale=1.0):
    if mode == "nn":
        (m, k), n = a.shape, b.shape[1]
        a_spec = pl.BlockSpec((tm, tk), lambda i, j, kk: (i, kk))
        b_spec = pl.BlockSpec((tk, tn), lambda i, j, kk: (kk, j))
        dot = _dot
    elif mode == "nt":
        (m, k), n = a.shape, b.shape[0]
        a_spec = pl.BlockSpec((tm, tk), lambda i, j, kk: (i, kk))
        b_spec = pl.BlockSpec((tn, tk), lambda i, j, kk: (j, kk))
        dot = _dot_nt
    else:
        (k, m), n = a.shape, b.shape[1]
        a_spec = pl.BlockSpec((tk, tm), lambda i, j, kk: (kk, i))
        b_spec = pl.BlockSpec((tk, tn), lambda i, j, kk: (kk, j))
        dot = _dot_tn
    assert m % tm == 0 and n % tn == 0 and k % tk == 0, (name, m, n, k)
    nk = k // tk
    n_out = len(out_dtypes)
    has_add = add is not None

    def body(*refs):
        a_ref, b_ref = refs[0], refs[1]
        add_ref = refs[2] if has_add else None
        outs = refs[2 + has_add: 2 + has_add + n_out]
        acc = refs[-1]
        kk = pl.program_id(2)

        @pl.when(kk == 0)
        def _():
            acc[...] = jnp.zeros_like(acc)

        acc[...] += dot(_bf(a_ref[...]), _bf(b_ref[...]))

        @pl.when(kk == nk - 1)
        def _():
            r = acc[...]
            if has_add:
                r = r + add_scale * add_ref[...]
            for o in outs:
                o[...] = r.astype(o.dtype)

    in_specs = [a_spec, b_spec]
    args = [a, b]
    if has_add:
        in_specs.append(pl.BlockSpec((tm, tn), lambda i, j, kk: (i, j)))
        args.append(add)
    out_spec = pl.BlockSpec((tm, tn), lambda i, j, kk: (i, j))
    res = pl.pallas_call(
        body, name=name, grid=(m // tm, n // tn, nk),
        in_specs=in_specs, out_specs=[out_spec] * n_out,
        out_shape=[jax.ShapeDtypeStruct((m, n), dt) for dt in out_dtypes],
        scratch_shapes=[pltpu.VMEM((tm, tn), F32)],
        compiler_params=_cparams(("parallel", "parallel", "arbitrary")),
    )(*args)
    return res


def _rope_tables(seq):
    half = MLA_ROPE // 2
    freqs = ROPE_BASE ** (-jnp.arange(half, dtype=F32) / half)
    ang = jnp.arange(seq, dtype=jnp.int32).astype(F32)[:, None] * freqs[None, :]
    cos, sin = jnp.cos(ang), jnp.sin(ang)
    z = lambda w: jnp.zeros((seq, w), F32)
    c_q = jnp.concatenate([jnp.ones((seq, MLA_NOPE), F32), cos, cos, z(32)], axis=1)
    c_k = jnp.concatenate([z(MLA_NOPE), cos, cos, z(32)], axis=1)
    s_lo = jnp.concatenate([z(MLA_NOPE), -sin, z(half), z(32)], axis=1)
    s_hi = jnp.concatenate([z(MLA_NOPE), z(half), sin, z(32)], axis=1)
    return c_q, c_k, s_lo, s_hi


def _rope_fwd(x, c, s_lo, s_hi):
    return x * c + pltpu.roll(x, LANES - 16, 1) * s_lo + pltpu.roll(x, 16, 1) * s_hi


def _rope_bwd(d, c, s_lo, s_hi):
    return d * c - pltpu.roll(d, 16, 1) * s_hi - pltpu.roll(d, LANES - 16, 1) * s_lo


def _rms_fwd(x, g):
    r = lax.rsqrt(jnp.mean(x * x, axis=-1, keepdims=True) + RMS_EPS)
    xn = x * r
    return xn * g, xn, r


def _mla_prep(p32, gq, gkv, wqb, wkvb, tabs, *, t):
    seq = p32.shape[0]

    def body(lat_ref, gq_ref, gkv_ref, wqb_ref, wkvb_ref, cq_ref, ck_ref, slo_ref, shi_ref, q_ref, k_ref, v_ref):
        lat = lat_ref[...]
        slo, shi = slo_ref[...], shi_ref[...]
        nq, _, _ = _rms_fwd(lat[:, 0:MLA_Q_LORA], gq_ref[...])
        qa = _dot(_bf(nq), wqb_ref[...])
        cq = cq_ref[...]
        for h in range(MLA_HEADS):
            blk = qa[:, h * LANES:(h + 1) * LANES]
            q_ref[:, h * LANES:(h + 1) * LANES] = _bf(_rope_fwd(blk, cq, slo, shi))
        nkv, _, _ = _rms_fwd(lat[:, MLA_Q_LORA:MLA_Q_LORA + MLA_KV_LORA], gkv_ref[...])
        kv = _dot(_bf(nkv), wkvb_ref[...])
        kpe = _rope_fwd(lat[:, 384:512], ck_ref[...], slo, shi)
        for h in range(MLA_HEADS):
            k_ref[:, h * LANES:(h + 1) * LANES] = _bf(kv[:, h * LANES:(h + 1) * LANES] + kpe)
        v_ref[...] = _bf(kv[:, MLA_HEADS * LANES:])

    row = lambda w: pl.BlockSpec((t, w), lambda i: (i, 0))
    full = lambda shp: pl.BlockSpec(shp, lambda i: (0, 0))
    return pl.pallas_call(
        body, name="mla_prep", grid=(seq // t,),
        in_specs=[row(512), full((1, MLA_Q_LORA)), full((1, MLA_KV_LORA)), full(wqb.shape), full(wkvb.shape),
                  row(LANES), row(LANES), row(LANES), row(LANES)],
        out_specs=[row(1024), row(1024), row(512)],
        out_shape=[jax.ShapeDtypeStruct((seq, 1024), BF16), jax.ShapeDtypeStruct((seq, 1024), BF16),
                   jax.ShapeDtypeStruct((seq, 512), BF16)],
        compiler_params=_cparams(("parallel",)),
    )(p32, gq, gkv, wqb, wkvb, *tabs)


def _mla_post(p32, dq, dk, dv, gq, gkv, wqb, wkvb, tabs, *, t):
    seq = p32.shape[0]

    def body(lat_ref, dq_ref, dk_ref, dv_ref, gq_ref, gkv_ref, wqb_ref, wkvb_ref, cq_ref, ck_ref, slo_ref, shi_ref,
             dlat_ref, dwqb_ref, dwkvb_ref, dgq_ref, dgkv_ref):
        @pl.when(pl.program_id(0) == 0)
        def _():
            dwqb_ref[...] = jnp.zeros_like(dwqb_ref)
            dwkvb_ref[...] = jnp.zeros_like(dwkvb_ref)
            dgq_ref[...] = jnp.zeros_like(dgq_ref)
            dgkv_ref[...] = jnp.zeros_like(dgkv_ref)

        lat = lat_ref[...]
        slo, shi = slo_ref[...], shi_ref[...]
        cq = cq_ref[...]
        gq_v, gkv_v = gq_ref[...], gkv_ref[...]
        nq, xq, rq = _rms_fwd(lat[:, 0:MLA_Q_LORA], gq_v)
        nkv, xkv, rkv = _rms_fwd(lat[:, MLA_Q_LORA:MLA_Q_LORA + MLA_KV_LORA], gkv_v)

        dqa = jnp.concatenate(
            [_rope_bwd(dq_ref[:, h * LANES:(h + 1) * LANES], cq, slo, shi) for h in range(MLA_HEADS)], axis=1)
        dqa_b = _bf(dqa)
        dwqb_ref[...] += _dot_tn(_bf(nq), dqa_b)
        dnq = _dot_nt(dqa_b, wqb_ref[...])
        dgq_ref[...] += jnp.sum(dnq * xq, axis=0, keepdims=True)
        dxn = dnq * gq_v
        dcq = rq * (dxn - xq * jnp.mean(dxn * xq, axis=-1, keepdims=True))

        dkf = dk_ref[...]
        dkv_b = _bf(jnp.concatenate([dkf, dv_ref[...]], axis=1))
        dwkvb_ref[...] += _dot_tn(_bf(nkv), dkv_b)
        dnkv = _dot_nt(dkv_b, wkvb_ref[...])
        dgkv_ref[...] += jnp.sum(dnkv * xkv, axis=0, keepdims=True)
        dxn = dnkv * gkv_v
        dckv = rkv * (dxn - xkv * jnp.mean(dxn * xkv, axis=-1, keepdims=True))

        dkpe = dkf[:, 0:LANES]
        for h in range(1, MLA_HEADS):
            dkpe = dkpe + dkf[:, h * LANES:(h + 1) * LANES]
        dkr = _rope_bwd(dkpe, ck_ref[...], slo, shi)
        dlat_ref[...] = _bf(jnp.concatenate([dcq, dckv, dkr], axis=1))

    row = lambda w: pl.BlockSpec((t, w), lambda i: (i, 0))
    full = lambda shp: pl.BlockSpec(shp, lambda i: (0, 0))
    return pl.pallas_call(
        body, name="mla_post", grid=(seq // t,),
        in_specs=[row(512), row(1024), row(1024), row(512), full((1, MLA_Q_LORA)), full((1, MLA_KV_LORA)),
                  full(wqb.shape), full(wkvb.shape), row(LANES), row(LANES), row(LANES), row(LANES)],
        out_specs=[row(512), full(wqb.shape), full(wkvb.shape), full((1, MLA_Q_LORA)), full((1, MLA_KV_LORA))],
        out_shape=[jax.ShapeDtypeStruct((seq, 512), BF16), jax.ShapeDtypeStruct(wqb.shape, F32),
                   jax.ShapeDtypeStruct(wkvb.shape, F32), jax.ShapeDtypeStruct((1, MLA_Q_LORA), F32),
                   jax.ShapeDtypeStruct((1, MLA_KV_LORA), F32)],
        compiler_params=_cparams(("arbitrary",)),
    )(p32, dq, dk, dv, gq, gkv, wqb, wkvb, *tabs)


def _split_bf16(x):
    hi = _bf(x)
    return hi, _bf(x - hi.astype(F32))


def _tri_sum(x, u):
    hi, lo = _split_bf16(x)
    return _dot(hi, u) + _dot(lo, u)


def _softplus(z):
    return jnp.maximum(z, 0.0) + jnp.log(1.0 + jnp.exp(-jnp.abs(z)))


def _head_queries(q, left):
    zero = jnp.zeros_like(q)
    return jnp.where(left, q, zero) * SB_SCALE, jnp.where(left, zero, q) * SB_SCALE


ROW_GROUP = 128


def _chains(tq):
    return [(h, g) for g in range(tq // ROW_GROUP) for h in range(2)]


def _chain_pattern(g, m, tk, strict):
    r_lo, r_hi = g * ROW_GROUP, (g + 1) * ROW_GROUP - 1
    c_lo, c_hi = m * tk, (m + 1) * tk - 1
    if (c_lo >= r_hi) if strict else (c_lo > r_hi):
        return None
    if (c_hi < r_lo) if strict else (c_hi <= r_lo):
        return True
    rr = lax.broadcasted_iota(jnp.int32, (ROW_GROUP, tk), 0) + r_lo
    cc = lax.broadcasted_iota(jnp.int32, (ROW_GROUP, tk), 1) + c_lo
    return (cc < rr) if strict else (cc <= rr)


def _masked(x, pat, fill=0.0):
    return x if pat is True else jnp.where(pat, x, fill)


def _rows(g):
    return slice(g * ROW_GROUP, (g + 1) * ROW_GROUP)


def _tri_matrix(tk, cmp):
    rr = lax.broadcasted_iota(jnp.int32, (tk, tk), 0)
    cc = lax.broadcasted_iota(jnp.int32, (tk, tk), 1)
    return cmp(rr, cc).astype(BF16)


def _mla_attn_fwd(qp, kp, vp, *, tq, tk):
    seq = qp.shape[0]
    nd = tq // tk
    neg = float(np.finfo(np.float32).min)
    chains = _chains(tq)

    def body(q_ref, k_ref, v_ref, o_ref, lse_ref):
        i = pl.program_id(1)
        left = lax.broadcasted_iota(jnp.int32, (tq, LANES), 1) < HALF
        qs = [q_ref[_rows(g), h * LANES:(h + 1) * LANES] for h, g in chains]

        def block(j, carry, m):
            start = pl.multiple_of(j * tk, tk)
            v = v_ref[pl.ds(start, tk), :]
            pats = [True if m is None else _chain_pattern(g, m, tk, False) for _, g in chains]
            live = [n for n, p in enumerate(pats) if p is not None]
            ss = {n: _dot_nt(qs[n], k_ref[pl.ds(start, tk), chains[n][0] * LANES:(chains[n][0] + 1) * LANES]) for n in live}
            new = list(carry)
            for n in live:
                m_old, l_old, acc = carry[n]
                s = _masked(ss[n] * MLA_SCALE, pats[n], neg)
                m_new = jnp.maximum(m_old, jnp.max(s, axis=-1, keepdims=True))
                a = jnp.exp(m_old - m_new)
                p = jnp.exp(s - m_new)
                new[n] = (m_new, a * l_old + jnp.sum(p, axis=-1, keepdims=True), a * acc + _dot(_bf(p), v))
            return tuple(new)

        init = (jnp.full((ROW_GROUP, 1), -1e30, F32), jnp.zeros((ROW_GROUP, 1), F32), jnp.zeros((ROW_GROUP, LANES), F32))
        carry = lax.fori_loop(0, i * nd, lambda j, c: block(j, c, None), (init,) * len(chains))
        for m in range(nd):
            carry = block(i * nd + m, carry, m)
        per_head = []
        for h in range(2):
            mine = [carry[n] for n, (ch, _) in enumerate(chains) if ch == h]
            per_head.append((jnp.concatenate([acc / l for _, l, acc in mine], axis=0),
                             jnp.concatenate([mm + jnp.log(l) for mm, l, _ in mine], axis=0)))
        o_ref[...] = jnp.where(left, per_head[0][0], per_head[1][0])
        lse_ref[...] = jnp.where(left, per_head[0][1], per_head[1][1])

    return pl.pallas_call(
        body, name="mla_fwd", grid=(MLA_HEADS // 2, seq // tq),
        in_specs=[pl.BlockSpec((tq, 2 * LANES), lambda p, i: (i, p)), pl.BlockSpec((seq, 2 * LANES), lambda p, i: (0, p)),
                  pl.BlockSpec((seq, LANES), lambda p, i: (0, p))],
        out_specs=[pl.BlockSpec((tq, LANES), lambda p, i: (i, p)), pl.BlockSpec((tq, LANES), lambda p, i: (i, p))],
        out_shape=[jax.ShapeDtypeStruct((seq, 512), F32), jax.ShapeDtypeStruct((seq, 512), F32)],
        compiler_params=_cparams(("parallel", "parallel")),
    )(qp, kp, vp)


def _mla_attn_bwd(qp, kp, vp, o, lse, do, *, tq, tk):
    seq = qp.shape[0]
    nd = tq // tk
    chains = _chains(tq)

    def body(q_ref, k_ref, v_ref, o_ref, lse_ref, do_ref, dq_ref, dk_ref, dv_ref):
        i = pl.program_id(1)

        @pl.when(i == 0)
        def _():
            dk_ref[...] = jnp.zeros_like(dk_ref)
            dv_ref[...] = jnp.zeros_like(dv_ref)

        left = lax.broadcasted_iota(jnp.int32, (tq, LANES), 1) < HALF
        do_f = do_ref[...]
        prod = do_f * o_ref[...]
        lse_v = lse_ref[...]
        do_heads = (_bf(jnp.where(left, do_f, 0.0)), _bf(jnp.where(left, 0.0, do_f)))
        delta_heads = (jnp.sum(jnp.where(left, prod, 0.0), axis=-1, keepdims=True),
                       jnp.sum(jnp.where(left, 0.0, prod), axis=-1, keepdims=True))
        qs = [q_ref[_rows(g), h * LANES:(h + 1) * LANES] for h, g in chains]
        dos = [do_heads[h][_rows(g)] for h, g in chains]
        deltas = [delta_heads[h][_rows(g)] for h, g in chains]
        lses = [lse_v[_rows(g), h * HALF:h * HALF + 1] for h, g in chains]

        def block(j, carry, m):
            start = pl.multiple_of(j * tk, tk)
            v = v_ref[pl.ds(start, tk), :]
            pats = [True if m is None else _chain_pattern(g, m, tk, False) for _, g in chains]
            live = [n for n, p in enumerate(pats) if p is not None]
            ks = [k_ref[pl.ds(start, tk), h * LANES:(h + 1) * LANES] for h in range(2)]
            ss = {n: _dot_nt(qs[n], ks[chains[n][0]]) for n in live}
            dps = {n: _dot_nt(dos[n], v) for n in live}
            new = list(carry)
            ps, dss = {}, {}
            for n in live:
                p = _masked(jnp.exp(ss[n] * MLA_SCALE - lses[n]), pats[n])
                ps[n] = _bf(p)
                dss[n] = _bf(p * (dps[n] - deltas[n]) * MLA_SCALE)
                new[n] = carry[n] + _dot(dss[n], ks[chains[n][0]])
            for h in range(2):
                mine = [n for n in live if chains[n][0] == h]
                ds_cat = jnp.concatenate([dss[n] for n in mine], axis=0)
                q_cat = jnp.concatenate([qs[n] for n in mine], axis=0)
                dk_ref[pl.ds(start, tk), h * LANES:(h + 1) * LANES] += _dot_tn(ds_cat, q_cat)
            p_cat = jnp.concatenate([ps[n] for n in live], axis=0)
            do_cat = jnp.concatenate([dos[n] for n in live], axis=0)
            dv_ref[pl.ds(start, tk), :] += _dot_tn(p_cat, do_cat)
            return tuple(new)

        zero = jnp.zeros((ROW_GROUP, LANES), F32)
        carry = lax.fori_loop(0, i * nd, lambda j, c: block(j, c, None), (zero,) * len(chains))
        for m in range(nd):
            carry = block(i * nd + m, carry, m)
        for n, (h, g) in enumerate(chains):
            dq_ref[_rows(g), h * LANES:(h + 1) * LANES] = carry[n]

    two_t = pl.BlockSpec((tq, 2 * LANES), lambda p, i: (i, p))
    two_s = pl.BlockSpec((seq, 2 * LANES), lambda p, i: (0, p))
    pair_t = pl.BlockSpec((tq, LANES), lambda p, i: (i, p))
    pair_s = pl.BlockSpec((seq, LANES), lambda p, i: (0, p))
    return pl.pallas_call(
        body, name="mla_bwd", grid=(MLA_HEADS // 2, seq // tq),
        in_specs=[two_t, two_s, pair_s, pair_t, pair_t, pair_t],
        out_specs=[two_t, two_s, pair_s],
        out_shape=[jax.ShapeDtypeStruct((seq, 1024), F32), jax.ShapeDtypeStruct((seq, 1024), F32),
                   jax.ShapeDtypeStruct((seq, 512), F32)],
        compiler_params=_cparams(("parallel", "arbitrary")),
    )(qp, kp, vp, o, lse, do)


def _sb_attn_fwd(pbf, *, tq, tk):
    seq = pbf.shape[0]
    nd = tq // tk
    qb, kb, vb = BLK_QB * 4, BLK_KB * 4, BLK_VB * 4
    chains = _chains(tq)

    def body(q_ref, k_ref, v_ref, o_ref, tot_ref):
        i = pl.program_id(1)
        u_later = _tri_matrix(tk, lambda r, c: r > c)
        left = lax.broadcasted_iota(jnp.int32, (tq, LANES), 1) < HALF
        q_heads = _head_queries(q_ref[...], left)
        qs = [q_heads[h][_rows(g)] for h, g in chains]

        def block(j, carry, m):
            start = pl.multiple_of(j * tk, tk)
            k = k_ref[pl.ds(start, tk), :]
            v = v_ref[pl.ds(start, tk), :]
            pats = [True if m is None else _chain_pattern(g, m, tk, True) for _, g in chains]
            live = [n for n, p in enumerate(pats) if p is not None]
            zs = {n: _dot_nt(qs[n], k) for n in live}
            raws = {n: _softplus(zs[n]) for n in live}
            sps = {n: _masked(raws[n], pats[n]) for n in live}
            laters = {n: _tri_sum(sps[n], u_later) for n in live}
            new = list(carry)
            for n in live:
                c, acc = carry[n]
                a = _masked(jnp.exp(zs[n] - raws[n] - laters[n] - c), pats[n])
                new[n] = (c + laters[n][:, 0:1] + sps[n][:, 0:1], acc + _dot(_bf(a), v))
            return tuple(new)

        init = (jnp.zeros((ROW_GROUP, 1), F32), jnp.zeros((ROW_GROUP, LANES), F32))
        carry = (init,) * len(chains)
        for m in reversed(range(nd)):
            carry = block(i * nd + m, carry, m)
        carry = lax.fori_loop(0, i * nd, lambda jj, cr: block(i * nd - 1 - jj, cr, None), carry)
        per_head = []
        for h in range(2):
            mine = [carry[n] for n, (ch, _) in enumerate(chains) if ch == h]
            per_head.append((jnp.concatenate([acc for _, acc in mine], axis=0), jnp.concatenate([c for c, _ in mine], axis=0)))
        o_ref[...] = jnp.where(left, per_head[0][0], per_head[1][0])
        tot_ref[...] = jnp.where(left, per_head[0][1], per_head[1][1])

    pair_t = pl.BlockSpec((tq, LANES), lambda p, i: (i, p))
    return pl.pallas_call(
        body, name="sb_fwd", grid=(SB_HEADS // 2, seq // tq),
        in_specs=[pl.BlockSpec((tq, LANES), lambda p, i: (i, qb + p)), pl.BlockSpec((seq, LANES), lambda p, i: (0, kb + p)),
                  pl.BlockSpec((seq, LANES), lambda p, i: (0, vb + p))],
        out_specs=[pair_t, pair_t],
        out_shape=[jax.ShapeDtypeStruct((seq, 512), F32), jax.ShapeDtypeStruct((seq, 512), F32)],
        compiler_params=_cparams(("parallel", "parallel")),
    )(pbf, pbf, pbf)


def _sb_attn_bwd(pbf, tot, do, *, tq, tk):
    seq = pbf.shape[0]
    nd = tq // tk
    qb, kb, vb = BLK_QB * 4, BLK_KB * 4, BLK_VB * 4
    chains = _chains(tq)

    def body(q_ref, k_ref, v_ref, tot_ref, do_ref, dq_ref, dk_ref, dv_ref):
        i = pl.program_id(1)

        @pl.when(i == 0)
        def _():
            dk_ref[...] = jnp.zeros_like(dk_ref)
            dv_ref[...] = jnp.zeros_like(dv_ref)

        u_upto = _tri_matrix(tk, lambda r, c: r <= c)
        u_below = _tri_matrix(tk, lambda r, c: r < c)
        left = lax.broadcasted_iota(jnp.int32, (tq, LANES), 1) < HALF
        q_heads = _head_queries(q_ref[...], left)
        do_f = do_ref[...]
        do_heads = (_bf(jnp.where(left, do_f, 0.0)), _bf(jnp.where(left, 0.0, do_f)))
        tot_v = tot_ref[...]
        qs = [q_heads[h][_rows(g)] for h, g in chains]
        dos = [do_heads[h][_rows(g)] for h, g in chains]
        totals = [tot_v[_rows(g), h * HALF:h * HALF + 1] for h, g in chains]

        def block(j, carry, m):
            start = pl.multiple_of(j * tk, tk)
            k = k_ref[pl.ds(start, tk), :]
            v = v_ref[pl.ds(start, tk), :]
            pats = [True if m is None else _chain_pattern(g, m, tk, True) for _, g in chains]
            live = [n for n, p in enumerate(pats) if p is not None]
            zs = {n: _dot_nt(qs[n], k) for n in live}
            das = {n: _dot_nt(dos[n], v) for n in live}
            raws = {n: _softplus(zs[n]) for n in live}
            sps = {n: _masked(raws[n], pats[n]) for n in live}
            uptos = {n: _tri_sum(sps[n], u_upto) for n in live}
            lbs, a_s, gs = {}, {}, {}
            for n in live:
                lbs[n] = zs[n] - raws[n]
                a =_masked(jnp.exp(lbs[n] - (totals[n] - carry[n][0] - uptos[n])), pats[n])
                a_s[n] = _bf(a)
                gs[n] = das[n] * a
            belows = {n: _tri_sum(gs[n], u_below) for n in live}
            new = list(carry)
            dzs = {}
            for n in live:
                sp_before, g_before, dq_acc = carry[n]
                beta = jnp.exp(lbs[n])
                dz = _masked(gs[n] * (1.0 - beta) - (g_before + belows[n]) * beta, pats[n])
                dzs[n] = _bf(dz)
                new[n] = (sp_before + uptos[n][:, tk - 1:tk], g_before + belows[n][:, tk - 1:tk] + gs[n][:, tk - 1:tk],
                          dq_acc + _dot(dzs[n], k))
            dz_cat = jnp.concatenate([dzs[n] for n in live], axis=0)
            q_cat = jnp.concatenate([qs[n] for n in live], axis=0)
            dk_ref[pl.ds(start, tk), :] += _dot_tn(dz_cat, q_cat)
            a_cat = jnp.concatenate([a_s[n] for n in live], axis=0)
            do_cat = jnp.concatenate([dos[n] for n in live], axis=0)
            dv_ref[pl.ds(start, tk), :] += _dot_tn(a_cat, do_cat)
            return tuple(new)

        zero = jnp.zeros((ROW_GROUP, 1), F32)
        init = (zero, zero, jnp.zeros((ROW_GROUP, LANES), F32))
        carry = lax.fori_loop(0, i * nd, lambda j, cr: block(j, cr, None), (init,) * len(chains))
        for m in range(nd):
            carry = block(i * nd + m, carry, m)
        per_head = [jnp.concatenate([carry[n][2] for n, (ch, _) in enumerate(chains) if ch == h], axis=0) for h in range(2)]
        dq_ref[...] = jnp.where(left, per_head[0], per_head[1]) * SB_SCALE

    pair_t = pl.BlockSpec((tq, LANES), lambda p, i: (i, p))
    pair_s = pl.BlockSpec((seq, LANES), lambda p, i: (0, p))
    return pl.pallas_call(
        body, name="sb_bwd", grid=(SB_HEADS // 2, seq // tq),
        in_specs=[pl.BlockSpec((tq, LANES), lambda p, i: (i, qb + p)), pl.BlockSpec((seq, LANES), lambda p, i: (0, kb + p)),
                  pl.BlockSpec((seq, LANES), lambda p, i: (0, vb + p)), pair_t, pair_t],
        out_specs=[pair_t, pair_s, pair_s],
        out_shape=[jax.ShapeDtypeStruct((seq, 512), F32)] * 3,
        compiler_params=_cparams(("parallel", "arbitrary")),
    )(pbf, pbf, pbf, tot, do)


def _mem_probs(q, k):
    s = _dot_nt(q, k) * MEM_SCALE
    e = jnp.exp(s - jnp.max(s, axis=-1, keepdims=True))
    return e / jnp.sum(e, axis=-1, keepdims=True)


def _mem_fwd(pbf, mkv, *, t):
    seq = pbf.shape[0]

    def body(q_ref, kv_ref, o_ref):
        for h in range(MEM_HEADS):
            lo, hi = h * LANES, (h + 1) * LANES
            p = _mem_probs(q_ref[:, lo:hi], kv_ref[:, lo:hi])
            o_ref[:, lo:hi] = _dot(_bf(p), kv_ref[:, 512 + lo:512 + hi])

    return pl.pallas_call(
        body, name="mem_fwd", grid=(seq // t,),
        in_specs=[pl.BlockSpec((t, 512), lambda i: (i, BLK_QM)), pl.BlockSpec((MEM_LEN, 1024), lambda i: (0, 0))],
        out_specs=pl.BlockSpec((t, 512), lambda i: (i, 0)),
        out_shape=jax.ShapeDtypeStruct((seq, 512), F32),
        compiler_params=_cparams(("parallel",)),
    )(pbf, mkv)


def _mem_bwd(pbf, mkv, do, *, t):
    seq = pbf.shape[0]

    def body(q_ref, kv_ref, do_ref, dq_ref, dkv_ref):
        @pl.when(pl.program_id(0) == 0)
        def _():
            dkv_ref[...] = jnp.zeros_like(dkv_ref)

        for h in range(MEM_HEADS):
            lo, hi = h * LANES, (h + 1) * LANES
            q, k, v = q_ref[:, lo:hi], kv_ref[:, lo:hi], kv_ref[:, 512 + lo:512 + hi]
            do_h = _bf(do_ref[:, lo:hi])
            p = _mem_probs(q, k)
            dp = _dot_nt(do_h, v)
            ds = _bf(p * (dp - jnp.sum(dp * p, axis=-1, keepdims=True)) * MEM_SCALE)
            dq_ref[:, lo:hi] = _dot(ds, k)
            dkv_ref[:, lo:hi] += _dot_tn(ds, q)
            dkv_ref[:, 512 + lo:512 + hi] += _dot_tn(_bf(p), do_h)

    return pl.pallas_call(
        body, name="mem_bwd", grid=(seq // t,),
        in_specs=[pl.BlockSpec((t, 512), lambda i: (i, BLK_QM)), pl.BlockSpec((MEM_LEN, 1024), lambda i: (0, 0)),
                  pl.BlockSpec((t, 512), lambda i: (i, 0))],
        out_specs=[pl.BlockSpec((t, 512), lambda i: (i, 0)), pl.BlockSpec((MEM_LEN, 1024), lambda i: (0, 0))],
        out_shape=[jax.ShapeDtypeStruct((seq, 512), F32), jax.ShapeDtypeStruct((MEM_LEN, 1024), F32)],
        compiler_params=_cparams(("arbitrary",)),
    )(pbf, mkv, do)


def _mid(x, tgt, o_a, o_b, o_m, p32, wmg, bmg, wba, wbb, wbm, wout, ln_g, ln_b, *, t):
    seq = x.shape[0]
    inv_d = 1.0 / D_MODEL

    def body(x_ref, t_ref, oa_ref, ob_ref, om_ref, ga_ref, gb_ref, gm_ref, wmg_ref, bmg_ref, wba_ref, wbb_ref,
             wbm_ref, wout_ref, lg_ref, lb_ref,
             du_ref, mrg_ref, dgp_ref, ha_ref, hb_ref, hm_ref, dya_ref, dyb_ref, dym_ref, doa_ref, dob_ref, dom_ref,
             dga_ref, dgb_ref, dgm_ref, dgain_ref, dbias_ref, dbmg_ref, loss_ref):
        @pl.when(pl.program_id(0) == 0)
        def _():
            dgain_ref[...] = jnp.zeros_like(dgain_ref)
            dbias_ref[...] = jnp.zeros_like(dbias_ref)
            dbmg_ref[...] = jnp.zeros_like(dbmg_ref)
            loss_ref[...] = jnp.zeros_like(loss_ref)

        xv = x_ref[...]
        gate = _sigmoid(_dot(_bf(xv), wmg_ref[...]) + bmg_ref[...])

        branches = []
        merged = None
        for b, (o_ref, g_ref, w_ref, h_ref) in enumerate(((oa_ref, ga_ref, wba_ref, ha_ref), (ob_ref, gb_ref, wbb_ref, hb_ref),
                                                         (om_ref, gm_ref, wbm_ref, hm_ref))):
            o, gt = o_ref[...], g_ref[...]
            sg = _sigmoid(gt)
            silu = gt * sg
            h = _bf(o * silu)
            h_ref[...] = h
            y = _dot(h, w_ref[...])
            g_b = gate[:, b * D_MODEL:(b + 1) * D_MODEL]
            term = g_b * y
            merged = term if merged is None else merged + term
            branches.append((o, gt, sg, silu, y, g_b))
        mrg_b = _bf(merged)
        mrg_ref[...] = mrg_b

        u = DEEPNORM_ALPHA * xv + _dot(mrg_b, wout_ref[...])
        mu = jnp.mean(u, axis=-1, keepdims=True)
        uc = u - mu
        rstd = lax.rsqrt(jnp.mean(uc * uc, axis=-1, keepdims=True) + LN_EPS)
        xhat = uc * rstd
        lg = lg_ref[...]
        y_out = xhat * lg + lb_ref[...]
        err = y_out - t_ref[...]
        loss_ref[...] += 0.5 * jnp.sum(jnp.mean(err * err, axis=-1, keepdims=True), axis=0, keepdims=True)
        dy = err * inv_d
        dgain_ref[...] += jnp.sum(dy * xhat, axis=0, keepdims=True)
        dbias_ref[...] += jnp.sum(dy, axis=0, keepdims=True)
        dxh = dy * lg
        du = rstd * (dxh - jnp.mean(dxh, axis=-1, keepdims=True) - xhat * jnp.mean(dxh * xhat, axis=-1, keepdims=True))
        du_ref[...] = du

        dmerged = _dot_nt(_bf(du), wout_ref[...])
        outs = ((dya_ref, doa_ref, dga_ref, wba_ref), (dyb_ref, dob_ref, dgb_ref, wbb_ref), (dym_ref, dom_ref, dgm_ref, wbm_ref))
        dgp = []
        for (o, gt, sg, silu, y, g_b), (dy_ref, do_ref, dg_ref, w_ref) in zip(branches, outs):
            dyb = _bf(dmerged * g_b)
            dy_ref[...] = dyb
            dgp.append(dmerged * y * g_b * (1.0 - g_b))
            dh = _dot_nt(dyb, w_ref[...])
            do_ref[...] = dh * silu
            dg_ref[...] = _bf(dh * o * (sg * (1.0 + gt * (1.0 - sg))))
        dgp = jnp.concatenate(dgp, axis=1)
        dgp_ref[...] = _bf(dgp)
        dbmg_ref[...] += jnp.sum(dgp, axis=0, keepdims=True)

    row = lambda w: pl.BlockSpec((t, w), lambda i: (i, 0))
    pblk = lambda c: pl.BlockSpec((t, 512), lambda i: (i, c))
    full = lambda shp: pl.BlockSpec(shp, lambda i: (0, 0))
    sds = jax.ShapeDtypeStruct
    return pl.pallas_call(
        body, name="mid", grid=(seq // t,),
        in_specs=[row(1024), row(1024), row(512), row(512), row(512), pblk(BLK_GATE_A), pblk(BLK_GATE_B), pblk(BLK_GATE_M),
                  full(wmg.shape), full((1, N_MERGE)), full(wba.shape), full(wbb.shape), full(wbm.shape), full(wout.shape),
                  full((1, D_MODEL)), full((1, D_MODEL))],
        out_specs=[row(1024), row(1024), row(N_MERGE), row(512), row(512), row(512), row(1024), row(1024), row(1024),
                   row(512), row(512), row(512), row(512), row(512), row(512),
                   full((1, D_MODEL)), full((1, D_MODEL)), full((1, N_MERGE)), full((1, 1))],
        out_shape=[sds((seq, 1024), F32), sds((seq, 1024), BF16), sds((seq, N_MERGE), BF16),
                   sds((seq, 512), BF16), sds((seq, 512), BF16), sds((seq, 512), BF16),
                   sds((seq, 1024), BF16), sds((seq, 1024), BF16), sds((seq, 1024), BF16),
                   sds((seq, 512), F32), sds((seq, 512), F32), sds((seq, 512), F32),
                   sds((seq, 512), BF16), sds((seq, 512), BF16), sds((seq, 512), BF16),
                   sds((1, D_MODEL), F32), sds((1, D_MODEL), F32), sds((1, N_MERGE), F32), sds((1, 1), F32)],
        compiler_params=_cparams(("arbitrary",)),
    )(x, tgt, o_a, o_b, o_m, p32, p32, p32, wmg, bmg, wba, wbb, wbm, wout, ln_g, ln_b)


def _primed_weights(w):
    w_in = w["w_in"]
    zc = lambda n: jnp.zeros((D_MODEL, n), w_in.dtype)
    w_in_p = jnp.concatenate([w_in[:, 0:384], zc(64), w_in[:, 384:416], zc(32), w_in[:, 416:]], axis=1)
    wqb = jnp.pad(w["w_q_b"].reshape(MLA_Q_LORA, MLA_HEADS, 96), ((0, 0), (0, 0), (0, 32))).reshape(MLA_Q_LORA, 1024)
    kv3 = w["w_kv_b"].reshape(MLA_KV_LORA, MLA_HEADS, 128)
    wk = jnp.pad(kv3[:, :, :MLA_NOPE], ((0, 0), (0, 0), (0, 64))).reshape(MLA_KV_LORA, 1024)
    wv = kv3[:, :, MLA_NOPE:].reshape(MLA_KV_LORA, 512)
    return w_in_p, wqb, jnp.concatenate([wk, wv], axis=1)


def _local_step(x, mem, tgt, w, small, *, tq, tk, t_row, t_mm):
    seq = x.shape[0]
    w_in_p, wqb, wkvb = _primed_weights(w)
    tabs = _rope_tables(seq)
    wmg, wout = w["w_merge_gate"], w["w_out"]
    wba, wbb, wbm = w["w_branch_mla"], w["w_branch_sb"], w["w_branch_mem"]

    p32, pbf = _matmul(x, w_in_p, mode="nn", tm=t_mm, tn=512, tk=D_MODEL, out_dtypes=(F32, BF16), name="proj_in")
    qp, kp, vp = _mla_prep(p32, small["q_a_gain"], small["kv_a_gain"], wqb, wkvb, tabs, t=t_row)
    o_a, lse = _mla_attn_fwd(qp, kp, vp, tq=tq, tk=tk)
    o_b, keep_total = _sb_attn_fwd(pbf, tq=tq, tk=tk)
    (mkv,) = _matmul(mem, w["w_mem_kv"], mode="nn", tm=MEM_LEN, tn=512, tk=D_MODEL, out_dtypes=(BF16,), name="mem_kv")
    o_m = _mem_fwd(pbf, mkv, t=t_row)

    (du, merged, dgpre, h_a, h_b, h_m, dy_a, dy_b, dy_m, do_a, do_b, do_m, dgate_a, dgate_b, dgate_m,
     d_ln_g, d_ln_b, d_bmg, loss) = _mid(x, tgt, o_a, o_b, o_m, p32, wmg, small["b_merge_gate"], wba, wbb, wbm, wout,
                                         small["ln_gain"], small["ln_bias"], t=t_row)

    dqp, dkp, dvp = _mla_attn_bwd(qp, kp, vp, o_a, lse, do_a, tq=tq, tk=tk)
    dlat, d_wqb, d_wkvb, d_gq, d_gkv = _mla_post(p32, dqp, dkp, dvp, small["q_a_gain"], small["kv_a_gain"], wqb, wkvb, tabs,
                                                 t=t_row)
    dq_b, dk_b, dv_b = _sb_attn_bwd(pbf, keep_total, do_b, tq=tq, tk=tk)
    dq_m, dmkv = _mem_bwd(pbf, mkv, do_m, t=t_row)

    dcat = jnp.concatenate([dgpre, dlat, dgate_a, _bf(dq_b), _bf(dk_b), _bf(dv_b), dgate_b, _bf(dq_m), dgate_m], axis=1)
    wcat = jnp.concatenate([wmg, w_in_p], axis=1)
    (grad_x,) = _matmul(dcat, wcat, mode="nt", tm=t_mm, tn=D_MODEL, tk=1024, out_dtypes=(F32,), name="grad_x",
                        add=du, add_scale=DEEPNORM_ALPHA)
    (d_wcat,) = _matmul(x, dcat, mode="tn", tm=512, tn=1024, tk=t_mm, out_dtypes=(F32,), name="grad_w_cat")
    (d_wout,) = _matmul(merged, du, mode="tn", tm=512, tn=1024, tk=t_mm, out_dtypes=(F32,), name="grad_w_out")
    (d_wba,) = _matmul(h_a, dy_a, mode="tn", tm=512, tn=1024, tk=t_mm, out_dtypes=(F32,), name="grad_w_branch_a")
    (d_wbb,) = _matmul(h_b, dy_b, mode="tn", tm=512, tn=1024, tk=t_mm, out_dtypes=(F32,), name="grad_w_branch_b")
    (d_wbm,) = _matmul(h_m, dy_m, mode="tn", tm=512, tn=1024, tk=t_mm, out_dtypes=(F32,), name="grad_w_branch_m")
    (d_wmemkv,) = _matmul(mem, dmkv, mode="tn", tm=512, tn=1024, tk=MEM_LEN, out_dtypes=(F32,), name="grad_w_mem_kv")

    d_wmg = d_wcat[:, :N_MERGE]
    d_winp = d_wcat[:, N_MERGE:]
    d_win = jnp.concatenate([d_winp[:, 0:384], d_winp[:, 448:480], d_winp[:, 512:]], axis=1)
    d_wq = d_wqb.reshape(MLA_Q_LORA, MLA_HEADS, 128)[:, :, :96].reshape(MLA_Q_LORA, 768)
    d_wk = d_wkvb[:, :1024].reshape(MLA_KV_LORA, MLA_HEADS, 128)[:, :, :MLA_NOPE]
    d_wv = d_wkvb[:, 1024:].reshape(MLA_KV_LORA, MLA_HEADS, MLA_V)
    d_wkv = jnp.concatenate([d_wk, d_wv], axis=2).reshape(MLA_KV_LORA, 1024)
    grads = {"w_in": d_win, "w_mem_kv": d_wmemkv, "w_q_b": d_wq, "w_kv_b": d_wkv, "w_branch_mla": d_wba,
             "w_branch_sb": d_wbb, "w_branch_mem": d_wbm, "w_merge_gate": d_wmg, "w_out": d_wout}
    small_grads = {"q_a_gain": d_gq, "kv_a_gain": d_gkv, "b_merge_gate": d_bmg, "ln_gain": d_ln_g, "ln_bias": d_ln_b}
    return loss[0, 0], grad_x, grads, small_grads


def _pack_shards(shards, small):
    flat_small = jnp.concatenate([small[n].reshape(-1) for n, _ in SMALL_SIZES])
    flat_small = jnp.pad(flat_small, (0, 8 * PACK_COLS - SMALL_TOTAL)).reshape(8, PACK_COLS)
    parts = [shards[n].reshape(-1, PACK_COLS) for n, _ in PACK_ROWS[:-1]] + [flat_small]
    return jnp.concatenate(parts, axis=0)


def _unpack_shards(pack, shapes):
    out, r0 = {}, 0
    for n, rows in PACK_ROWS[:-1]:
        out[n] = pack[r0:r0 + rows].reshape(shapes[n])
        r0 += rows
    flat = pack[r0:r0 + 8].reshape(-1)
    small, c0 = {}, 0
    for n, size in SMALL_SIZES:
        small[n] = flat[c0:c0 + size].reshape(1, size)
        c0 += size
    return out, small


def _split_by_chip(name, full):
    r, c = full.shape
    if name in COL_SHARDED:
        s = full.reshape(r, N_CHIPS, c // N_CHIPS).transpose(1, 0, 2)
    else:
        s = full.reshape(N_CHIPS, r // N_CHIPS, c)
    return s.reshape(N_CHIPS, -1, PACK_COLS)


def _join_chips(name, packed4):
    r, c = FULL_SHAPES[name]
    if name in COL_SHARDED:
        return packed4.reshape(N_CHIPS, r, c // N_CHIPS).transpose(1, 0, 2).reshape(r, c)
    return packed4.reshape(r, c)


def _place():
    x, y, c = lax.axis_index("x"), lax.axis_index("y"), lax.axis_index("c")
    return x, y, c


def _other_chips(x, y):
    return ((1 - x, y), (x, 1 - y), (1 - x, 1 - y))


def _gather_weights(wpack):
    rows = wpack.shape[0]
    chunk = rows // 4

    def body(w_ref, out_ref, wb_ref, send_sems, recv_sems, local_sem):
        x, y, c = _place()
        me = 2 * x + y
        for r in range(4):
            wb_ref[r * chunk:(r + 1) * chunk, :] = _bf(w_ref[r * chunk:(r + 1) * chunk, :])
        mine = pltpu.make_async_copy(wb_ref, out_ref.at[me], local_sem)
        mine.start()
        copies = []
        for k, (px, py) in enumerate(_other_chips(x, y)):
            cp = pltpu.make_async_remote_copy(src_ref=wb_ref, dst_ref=out_ref.at[me], send_sem=send_sems.at[k],
                                              recv_sem=recv_sems.at[k], device_id=(px, py, c), device_id_type=MESH_ID)
            cp.start()
            copies.append(cp)
        for k, (px, py) in enumerate(_other_chips(x, y)):
            pltpu.make_async_remote_copy(src_ref=wb_ref, dst_ref=out_ref.at[2 * px + py], send_sem=send_sems.at[k],
                                         recv_sem=recv_sems.at[k], device_id=(px, py, c), device_id_type=MESH_ID).wait_recv()
        for cp in copies:
            cp.wait_send()
        mine.wait()

    return pl.pallas_call(
        body, name="gather_weights",
        in_specs=[pl.BlockSpec(memory_space=pltpu.VMEM)],
        out_specs=pl.BlockSpec(memory_space=pltpu.HBM),
        out_shape=jax.ShapeDtypeStruct((N_CHIPS, rows, PACK_COLS), BF16),
        scratch_shapes=[pltpu.VMEM((rows, PACK_COLS), BF16), pltpu.SemaphoreType.DMA((3,)), pltpu.SemaphoreType.DMA((3,)),
                        pltpu.SemaphoreType.DMA],
        compiler_params=pltpu.CompilerParams(vmem_limit_bytes=VMEM_LIMIT),
    )(wpack)


def _to_sibling_half(gpack):
    def body(g_ref, out_ref, send_sems, recv_sems):
        x, y, c = _place()
        theirs = pl.ds(pl.multiple_of((1 - c) * PACK_HALF, 8), PACK_HALF)
        copies = [pltpu.make_async_remote_copy(src_ref=g_ref.at[j, theirs, :], dst_ref=out_ref.at[j],
                                               send_sem=send_sems.at[j], recv_sem=recv_sems.at[j],
                                               device_id=(x, y, 1 - c), device_id_type=MESH_ID) for j in range(N_CHIPS)]
        for cp in copies:
            cp.start()
        for cp in copies:
            cp.wait()

    return pl.pallas_call(
        body, name="rs_sibling",
        in_specs=[pl.BlockSpec(memory_space=pltpu.HBM)],
        out_specs=pl.BlockSpec(memory_space=pltpu.HBM),
        out_shape=jax.ShapeDtypeStruct((N_CHIPS, PACK_HALF, PACK_COLS), F32),
        scratch_shapes=[pltpu.SemaphoreType.DMA((N_CHIPS,)), pltpu.SemaphoreType.DMA((N_CHIPS,))],
    )(gpack)


def _add_sibling(gpack, got):
    tr = PACK_HALF // 4

    def body(c_ref, g_ref, r_ref, o_ref):
        o_ref[...] = g_ref[...] + r_ref[...]

    grid_spec = pltpu.PrefetchScalarGridSpec(
        num_scalar_prefetch=1, grid=(N_CHIPS, 4),
        in_specs=[pl.BlockSpec((1, tr, PACK_COLS), lambda j, i, c_ref: (j, c_ref[0] * 4 + i, 0)),
                  pl.BlockSpec((1, tr, PACK_COLS), lambda j, i, c_ref: (j, i, 0))],
        out_specs=pl.BlockSpec((1, tr, PACK_COLS), lambda j, i, c_ref: (j, i, 0)))
    return pl.pallas_call(
        body, name="rs_add_sibling", grid_spec=grid_spec,
        out_shape=jax.ShapeDtypeStruct((N_CHIPS, PACK_HALF, PACK_COLS), F32),
        compiler_params=_cparams(("parallel", "parallel")),
    )(lax.axis_index("c").astype(jnp.int32).reshape(1), gpack, got)


def _to_owner_chips(chipsum):
    def body(s_ref, out_ref, send_sems, recv_sems, local_sem):
        x, y, c = _place()
        me = 2 * x + y
        mine = pltpu.make_async_copy(s_ref.at[me], out_ref.at[me], local_sem)
        mine.start()
        copies = []
        for k, (px, py) in enumerate(_other_chips(x, y)):
            cp = pltpu.make_async_remote_copy(src_ref=s_ref.at[2 * px + py], dst_ref=out_ref.at[me], send_sem=send_sems.at[k],
                                              recv_sem=recv_sems.at[k], device_id=(px, py, c), device_id_type=MESH_ID)
            cp.start()
            copies.append(cp)
        for k, (px, py) in enumerate(_other_chips(x, y)):
            pltpu.make_async_remote_copy(src_ref=s_ref.at[me], dst_ref=out_ref.at[2 * px + py], send_sem=send_sems.at[k],
                                         recv_sem=recv_sems.at[k], device_id=(px, py, c), device_id_type=MESH_ID).wait_recv()
        for cp in copies:
            cp.wait_send()
        mine.wait()

    return pl.pallas_call(
        body, name="rs_chips",
        in_specs=[pl.BlockSpec(memory_space=pltpu.HBM)],
        out_specs=pl.BlockSpec(memory_space=pltpu.HBM),
        out_shape=jax.ShapeDtypeStruct(chipsum.shape, F32),
        scratch_shapes=[pltpu.SemaphoreType.DMA((3,)), pltpu.SemaphoreType.DMA((3,)), pltpu.SemaphoreType.DMA],
    )(chipsum)


def _add_chips(parts):
    tr = PACK_HALF // 4

    def body(p_ref, o_ref):
        o_ref[...] = ((p_ref[0] + p_ref[1]) + p_ref[2]) + p_ref[3]

    return pl.pallas_call(
        body, name="rs_add_chips", grid=(4,),
        in_specs=[pl.BlockSpec((N_CHIPS, tr, PACK_COLS), lambda i: (0, i, 0))],
        out_specs=pl.BlockSpec((tr, PACK_COLS), lambda i: (i, 0)),
        out_shape=jax.ShapeDtypeStruct((PACK_HALF, PACK_COLS), F32),
        compiler_params=_cparams(("parallel",)),
    )(parts)


def _swap_halves(half):
    def body(h_ref, out_ref, send_sem, recv_sem, local_sem):
        x, y, c = _place()
        my_rows = pl.ds(pl.multiple_of(c * PACK_HALF, 8), PACK_HALF)
        mine = pltpu.make_async_copy(h_ref, out_ref.at[my_rows, :], local_sem)
        mine.start()
        cp = pltpu.make_async_remote_copy(src_ref=h_ref, dst_ref=out_ref.at[my_rows, :], send_sem=send_sem, recv_sem=recv_sem,
                                          device_id=(x, y, 1 - c), device_id_type=MESH_ID)
        cp.start()
        their_rows = pl.ds(pl.multiple_of((1 - c) * PACK_HALF, 8), PACK_HALF)
        pltpu.make_async_remote_copy(src_ref=h_ref, dst_ref=out_ref.at[their_rows, :], send_sem=send_sem, recv_sem=recv_sem,
                                     device_id=(x, y, 1 - c), device_id_type=MESH_ID).wait_recv()
        cp.wait_send()
        mine.wait()

    return pl.pallas_call(
        body, name="rs_swap_halves",
        in_specs=[pl.BlockSpec(memory_space=pltpu.HBM)],
        out_specs=pl.BlockSpec(memory_space=pltpu.HBM),
        out_shape=jax.ShapeDtypeStruct((PACK_TOTAL, PACK_COLS), F32),
        scratch_shapes=[pltpu.SemaphoreType.DMA, pltpu.SemaphoreType.DMA, pltpu.SemaphoreType.DMA],
    )(half)


def _adamw(w, g, m, v):
    tr = PACK_TOTAL // 8

    def body(w_ref, g_ref, m_ref, v_ref, d_ref, nm_ref, nv_ref):
        gv = g_ref[...]
        m_new = ADAM_B1 * m_ref[...] + (1.0 - ADAM_B1) * gv
        v_new = ADAM_B2 * v_ref[...] + (1.0 - ADAM_B2) * (gv * gv)
        m_hat = m_new / (1.0 - ADAM_B1 ** ADAM_STEP)
        v_hat = v_new / (1.0 - ADAM_B2 ** ADAM_STEP)
        d_ref[...] = -ADAM_LR * (m_hat / (jnp.sqrt(v_hat) + ADAM_EPS) + ADAM_WD * w_ref[...])
        nm_ref[...] = m_new
        nv_ref[...] = v_new

    blk = pl.BlockSpec((tr, PACK_COLS), lambda i: (i, 0))
    return pl.pallas_call(
        body, name="adamw", grid=(8,),
        in_specs=[blk] * 4, out_specs=[blk] * 3,
        out_shape=[jax.ShapeDtypeStruct((PACK_TOTAL, PACK_COLS), F32)] * 3,
        compiler_params=_cparams(("parallel",)),
    )(w, g, m, v)


WEIGHT_NAMES = ("w_in", "w_mem_kv", "q_a_gain", "w_q_b", "kv_a_gain", "w_kv_b", "w_branch_mla", "w_branch_sb",
                "w_branch_mem", "w_merge_gate", "b_merge_gate", "w_out", "ln_gain", "ln_bias")
BIG_NAMES = tuple(n for n, _ in PACK_ROWS[:-1])
SMALL_NAMES = tuple(n for n, _ in SMALL_SIZES)


def kernel(x, mem, w_in, w_mem_kv, q_a_gain, w_q_b, kv_a_gain, w_kv_b, w_branch_mla, w_branch_sb, w_branch_mem, w_merge_gate, b_merge_gate, w_out, ln_gain, ln_bias, loss_target, m_w_in, m_w_mem_kv, m_q_a_gain, m_w_q_b, m_kv_a_gain, m_w_kv_b, m_w_branch_mla, m_w_branch_sb, m_w_branch_mem, m_w_merge_gate, m_b_merge_gate, m_w_out, m_ln_gain, m_ln_bias, v_w_in, v_w_mem_kv, v_q_a_gain, v_w_q_b, v_kv_a_gain, v_w_kv_b, v_w_branch_mla, v_w_branch_sb, v_w_branch_mem, v_w_merge_gate, v_b_merge_gate, v_w_out, v_ln_gain, v_ln_bias):
    weights = dict(zip(WEIGHT_NAMES, (w_in, w_mem_kv, q_a_gain, w_q_b, kv_a_gain, w_kv_b, w_branch_mla, w_branch_sb,
                                      w_branch_mem, w_merge_gate, b_merge_gate, w_out, ln_gain, ln_bias)))
    mom1 = dict(zip(WEIGHT_NAMES, (m_w_in, m_w_mem_kv, m_q_a_gain, m_w_q_b, m_kv_a_gain, m_w_kv_b, m_w_branch_mla,
                                   m_w_branch_sb, m_w_branch_mem, m_w_merge_gate, m_b_merge_gate, m_w_out, m_ln_gain,
                                   m_ln_bias)))
    mom2 = dict(zip(WEIGHT_NAMES, (v_w_in, v_w_mem_kv, v_q_a_gain, v_w_q_b, v_kv_a_gain, v_w_kv_b, v_w_branch_mla,
                                   v_w_branch_sb, v_w_branch_mem, v_w_merge_gate, v_b_merge_gate, v_w_out, v_ln_gain,
                                   v_ln_bias)))
    shard_shapes = {n: weights[n].shape for n in BIG_NAMES}

    def pack(d):
        return _pack_shards({n: d[n] for n in BIG_NAMES}, {n: d[n] for n in SMALL_NAMES})

    wpack, mpack, vpack = pack(weights), pack(mom1), pack(mom2)

    gathered = _gather_weights(wpack)
    full_w, r0 = {}, 0
    for n, rows in PACK_ROWS[:-1]:
        full_w[n] = _join_chips(n, gathered[:, r0:r0 + rows])
        r0 += rows
    small = {n: weights[n] for n in SMALL_NAMES}

    loss, grad_x, grads, small_grads = _local_step(x[0], mem[0], loss_target[0], full_w, small,
                                                   tq=512, tk=256, t_row=256, t_mm=512)

    flat_small = jnp.concatenate([small_grads[n].reshape(-1) for n in SMALL_NAMES])
    flat_small = jnp.pad(flat_small, (0, 8 * PACK_COLS - SMALL_TOTAL)).reshape(1, 8, PACK_COLS)
    gpack = jnp.concatenate([_split_by_chip(n, grads[n]) for n in BIG_NAMES]
                            + [jnp.broadcast_to(flat_small, (N_CHIPS, 8, PACK_COLS))], axis=1)

    got = _to_sibling_half(gpack)
    chipsum = _add_sibling(gpack, got)
    parts = _to_owner_chips(chipsum)
    half = _add_chips(parts)
    gred = _swap_halves(half)
    delta, new_m, new_v = _adamw(wpack, gred, mpack, vpack)

    loss = lax.psum(loss, ("x", "y", "c"))
    outs = [loss, grad_x[None]]
    for packed in (gred, delta, new_m, new_v):
        big, sm = _unpack_shards(packed, shard_shapes)
        outs.extend(big[n] if n in big else sm[n] for n in WEIGHT_NAMES)
    return tuple(outs)
```

```python
import functools
import math

import numpy as np
import jax
import jax.numpy as jnp
from jax import lax
from jax.experimental import pallas as pl
from jax.experimental.pallas import tpu as pltpu

F32 = jnp.float32
BF16 = jnp.bfloat16
MESH_ID = pl.DeviceIdType.MESH

D_MODEL = 1024
MEM_LEN = 256
MLA_HEADS = 8
MLA_NOPE = 64
MLA_ROPE = 32
MLA_V = 64
MLA_Q_LORA = 256
MLA_KV_LORA = 128
SB_HEADS = 8
SB_HEAD_DIM = 64
MEM_HEADS = 4
MEM_HEAD_DIM = 128
BRANCH_WIDTH = 512
ROPE_BASE = 10000.0
RMS_EPS = 1e-6
LN_EPS = 1e-5
DEEPNORM_ALPHA = 2.0 ** 0.25
MLA_SCALE = 1.0 / math.sqrt(MLA_NOPE + MLA_ROPE)
SB_SCALE = 1.0 / math.sqrt(SB_HEAD_DIM)
MEM_SCALE = 1.0 / math.sqrt(MEM_HEAD_DIM)

ADAM_LR = 0.001
ADAM_B1 = 0.9
ADAM_B2 = 0.999
ADAM_EPS = 1e-08
ADAM_WD = 0.01
ADAM_STEP = 10

LANES = 128
HALF = 64
N_CHIPS = 4
PACK_COLS = 1024
VMEM_LIMIT = 56 * 1024 * 1024

IN_WIDTH_P = 4096
BLK_LAT, BLK_GATE_A, BLK_QB, BLK_KB, BLK_VB, BLK_GATE_B, BLK_QM, BLK_GATE_M = range(8)
N_MERGE = 3 * D_MODEL
CAT_WIDTH = N_MERGE + IN_WIDTH_P

PACK_ROWS = (("w_in", 1000), ("w_mem_kv", 256), ("w_q_b", 48), ("w_kv_b", 32), ("w_branch_mla", 128),
             ("w_branch_sb", 128), ("w_branch_mem", 128), ("w_merge_gate", 768), ("w_out", 256), ("small", 8))
PACK_TOTAL = sum(r for _, r in PACK_ROWS)
PACK_HALF = PACK_TOTAL // 2
SMALL_SIZES = (("q_a_gain", 256), ("kv_a_gain", 128), ("b_merge_gate", 3072), ("ln_gain", 1024), ("ln_bias", 1024))
SMALL_TOTAL = sum(s for _, s in SMALL_SIZES)
COL_SHARDED = ("w_in", "w_q_b", "w_kv_b", "w_branch_mla", "w_branch_sb", "w_branch_mem", "w_merge_gate")
ROW_SHARDED = ("w_mem_kv", "w_out")
FULL_SHAPES = {"w_in": (1024, 4000), "w_mem_kv": (1024, 1024), "w_q_b": (256, 768), "w_kv_b": (128, 1024),
               "w_branch_mla": (512, 1024), "w_branch_sb": (512, 1024), "w_branch_mem": (512, 1024),
               "w_merge_gate": (1024, 3072), "w_out": (1024, 1024)}


def _cparams(sem=None):
    return pltpu.CompilerParams(dimension_semantics=sem, vmem_limit_bytes=VMEM_LIMIT)


def _dot(a, b):
    return jnp.dot(a, b, preferred_element_type=F32)


def _dot_nt(a, b):
    return lax.dot_general(a, b, (((1,), (1,)), ((), ())), preferred_element_type=F32)


def _dot_tn(a, b):
    return lax.dot_general(a, b, (((0,), (0,)), ((), ())), preferred_element_type=F32)


def _bf(x):
    return x.astype(BF16)


def _sigmoid(x):
    return 1.0 / (1.0 + jnp.exp(-x))


def _matmul(a, b, *, mode, tm, tn, tk, out_dtypes, name, add=None, add_scale=1.0):
    if mode == "nn":
        (m, k), n = a.shape, b.shape[1]
        a_spec = pl.BlockSpec((tm, tk), lambda i, j, kk: (i, kk))
        b_spec = pl.BlockSpec((tk, tn), lambda i, j, kk: (kk, j))
        dot = _dot
    elif mode == "nt":
        (m, k), n = a.shape, b.shape[0]
        a_spec = pl.BlockSpec((tm, tk), lambda i, j, kk: (i, kk))
        b_spec = pl.BlockSpec((tn, tk), lambda i, j, kk: (j, kk))
        dot = _dot_nt
    else:
        (k, m), n = a.shape, b.shape[1]
        a_spec = pl.BlockSpec((tk, tm), lambda i, j, kk: (kk, i))
        b_spec = pl.BlockSpec((tk, tn), lambda i, j, kk: (kk, j))
        dot = _dot_tn
    assert m % tm == 0 and n % tn == 0 and k % tk == 0, (name, m, n, k)
    nk = k // tk
    n_out = len(out_dtypes)
    has_add = add is not None

    def body(*refs):
        a_ref, b_ref = refs[0], refs[1]
        add_ref = refs[2] if has_add else None
        outs = refs[2 + has_add: 2 + has_add + n_out]
        acc = refs[-1]
        kk = pl.program_id(2)

        @pl.when(kk == 0)
        def _():
            acc[...] = jnp.zeros_like(acc)

        acc[...] += dot(_bf(a_ref[...]), _bf(b_ref[...]))

        @pl.when(kk == nk - 1)
        def _():
            r = acc[...]
            if has_add:
                r = r + add_scale * add_ref[...]
            for o in outs:
                o[...] = r.astype(o.dtype)

    in_specs = [a_spec, b_spec]
    args = [a, b]
    if has_add:
        in_specs.append(pl.BlockSpec((tm, tn), lambda i, j, kk: (i, j)))
        args.append(add)
    out_spec = pl.BlockSpec((tm, tn), lambda i, j, kk: (i, j))
    res = pl.pallas_call(
        body, name=name, grid=(m // tm, n // tn, nk),
        in_specs=in_specs, out_specs=[out_spec] * n_out,
        out_shape=[jax.ShapeDtypeStruct((m, n), dt) for dt in out_dtypes],
        scratch_shapes=[pltpu.VMEM((tm, tn), F32)],
        compiler_params=_cparams(("parallel", "parallel", "arbitrary")),
    )(*args)
    return res


def _rope_tables(seq):
    half = MLA_ROPE // 2
    freqs = ROPE_BASE ** (-jnp.arange(half, dtype=F32) / half)
    ang = jnp.arange(seq, dtype=jnp.int32).astype(F32)[:, None] * freqs[None, :]
    cos, sin = jnp.cos(ang), jnp.sin(ang)
    z = lambda w: jnp.zeros((seq, w), F32)
    c_q = jnp.concatenate([jnp.ones((seq, MLA_NOPE), F32), cos, cos, z(32)], axis=1)
    c_k = jnp.concatenate([z(MLA_NOPE), cos, cos, z(32)], axis=1)
    s_lo = jnp.concatenate([z(MLA_NOPE), -sin, z(half), z(32)], axis=1)
    s_hi = jnp.concatenate([z(MLA_NOPE), z(half), sin, z(32)], axis=1)
    return c_q, c_k, s_lo, s_hi


def _rope_fwd(x, c, s_lo, s_hi):
    return x * c + pltpu.roll(x, LANES - 16, 1) * s_lo + pltpu.roll(x, 16, 1) * s_hi


def _rope_bwd(d, c, s_lo, s_hi):
    return d * c - pltpu.roll(d, 16, 1) * s_hi - pltpu.roll(d, LANES - 16, 1) * s_lo


def _rms_fwd(x, g):
    r = lax.rsqrt(jnp.mean(x * x, axis=-1, keepdims=True) + RMS_EPS)
    xn = x * r
    return xn * g, xn, r


def _mla_prep(p32, gq, gkv, wqb, wkvb, tabs, *, t):
    seq = p32.shape[0]

    def body(lat_ref, gq_ref, gkv_ref, wqb_ref, wkvb_ref, cq_ref, ck_ref, slo_ref, shi_ref, q_ref, k_ref, v_ref):
        lat = lat_ref[...]
        slo, shi = slo_ref[...], shi_ref[...]
        nq, _, _ = _rms_fwd(lat[:, 0:MLA_Q_LORA], gq_ref[...])
        qa = _dot(_bf(nq), wqb_ref[...])
        cq = cq_ref[...]
        for h in range(MLA_HEADS):
            blk = qa[:, h * LANES:(h + 1) * LANES]
            q_ref[:, h * LANES:(h + 1) * LANES] = _bf(_rope_fwd(blk, cq, slo, shi))
        nkv, _, _ = _rms_fwd(lat[:, MLA_Q_LORA:MLA_Q_LORA + MLA_KV_LORA], gkv_ref[...])
        kv = _dot(_bf(nkv), wkvb_ref[...])
        kpe = _rope_fwd(lat[:, 384:512], ck_ref[...], slo, shi)
        for h in range(MLA_HEADS):
            k_ref[:, h * LANES:(h + 1) * LANES] = _bf(kv[:, h * LANES:(h + 1) * LANES] + kpe)
        v_ref[...] = _bf(kv[:, MLA_HEADS * LANES:])

    row = lambda w: pl.BlockSpec((t, w), lambda i: (i, 0))
    full = lambda shp: pl.BlockSpec(shp, lambda i: (0, 0))
    return pl.pallas_call(
        body, name="mla_prep", grid=(seq // t,),
        in_specs=[row(512), full((1, MLA_Q_LORA)), full((1, MLA_KV_LORA)), full(wqb.shape), full(wkvb.shape),
                  row(LANES), row(LANES), row(LANES), row(LANES)],
        out_specs=[row(1024), row(1024), row(512)],
        out_shape=[jax.ShapeDtypeStruct((seq, 1024), BF16), jax.ShapeDtypeStruct((seq, 1024), BF16),
                   jax.ShapeDtypeStruct((seq, 512), BF16)],
        compiler_params=_cparams(("parallel",)),
    )(p32, gq, gkv, wqb, wkvb, *tabs)


def _mla_post(p32, dq, dk, dv, gq, gkv, wqb, wkvb, tabs, *, t):
    seq = p32.shape[0]

    def body(lat_ref, dq_ref, dk_ref, dv_ref, gq_ref, gkv_ref, wqb_ref, wkvb_ref, cq_ref, ck_ref, slo_ref, shi_ref,
             dlat_ref, dwqb_ref, dwkvb_ref, dgq_ref, dgkv_ref):
        @pl.when(pl.program_id(0) == 0)
        def _():
            dwqb_ref[...] = jnp.zeros_like(dwqb_ref)
            dwkvb_ref[...] = jnp.zeros_like(dwkvb_ref)
            dgq_ref[...] = jnp.zeros_like(dgq_ref)
            dgkv_ref[...] = jnp.zeros_like(dgkv_ref)

        lat = lat_ref[...]
        slo, shi = slo_ref[...], shi_ref[...]
        cq = cq_ref[...]
        gq_v, gkv_v = gq_ref[...], gkv_ref[...]
        nq, xq, rq = _rms_fwd(lat[:, 0:MLA_Q_LORA], gq_v)
        nkv, xkv, rkv = _rms_fwd(lat[:, MLA_Q_LORA:MLA_Q_LORA + MLA_KV_LORA], gkv_v)

        dqa = jnp.concatenate(
            [_rope_bwd(dq_ref[:, h * LANES:(h + 1) * LANES], cq, slo, shi) for h in range(MLA_HEADS)], axis=1)
        dqa_b = _bf(dqa)
        dwqb_ref[...] += _dot_tn(_bf(nq), dqa_b)
        dnq = _dot_nt(dqa_b, wqb_ref[...])
        dgq_ref[...] += jnp.sum(dnq * xq, axis=0, keepdims=True)
        dxn = dnq * gq_v
        dcq = rq * (dxn - xq * jnp.mean(dxn * xq, axis=-1, keepdims=True))

        dkf = dk_ref[...]
        dkv_b = _bf(jnp.concatenate([dkf, dv_ref[...]], axis=1))
        dwkvb_ref[...] += _dot_tn(_bf(nkv), dkv_b)
        dnkv = _dot_nt(dkv_b, wkvb_ref[...])
        dgkv_ref[...] += jnp.sum(dnkv * xkv, axis=0, keepdims=True)
        dxn = dnkv * gkv_v
        dckv = rkv * (dxn - xkv * jnp.mean(dxn * xkv, axis=-1, keepdims=True))

        dkpe = dkf[:, 0:LANES]
        for h in range(1, MLA_HEADS):
            dkpe = dkpe + dkf[:, h * LANES:(h + 1) * LANES]
        dkr = _rope_bwd(dkpe, ck_ref[...], slo, shi)
        dlat_ref[...] = _bf(jnp.concatenate([dcq, dckv, dkr], axis=1))

    row = lambda w: pl.BlockSpec((t, w), lambda i: (i, 0))
    full = lambda shp: pl.BlockSpec(shp, lambda i: (0, 0))
    return pl.pallas_call(
        body, name="mla_post", grid=(seq // t,),
        in_specs=[row(512), row(1024), row(1024), row(512), full((1, MLA_Q_LORA)), full((1, MLA_KV_LORA)),
                  full(wqb.shape), full(wkvb.shape), row(LANES), row(LANES), row(LANES), row(LANES)],
        out_specs=[row(512), full(wqb.shape), full(wkvb.shape), full((1, MLA_Q_LORA)), full((1, MLA_KV_LORA))],
        out_shape=[jax.ShapeDtypeStruct((seq, 512), BF16), jax.ShapeDtypeStruct(wqb.shape, F32),
                   jax.ShapeDtypeStruct(wkvb.shape, F32), jax.ShapeDtypeStruct((1, MLA_Q_LORA), F32),
                   jax.ShapeDtypeStruct((1, MLA_KV_LORA), F32)],
        compiler_params=_cparams(("arbitrary",)),
    )(p32, dq, dk, dv, gq, gkv, wqb, wkvb, *tabs)


def _split_bf16(x):
    hi = _bf(x)
    return hi, _bf(x - hi.astype(F32))


def _tri_sum(x, u):
    hi, lo = _split_bf16(x)
    return _dot(hi, u) + _dot(lo, u)


def _softplus(z):
    return jnp.maximum(z, 0.0) + jnp.log(1.0 + jnp.exp(-jnp.abs(z)))


def _head_queries(q, left):
    zero = jnp.zeros_like(q)
    return jnp.where(left, q, zero) * SB_SCALE, jnp.where(left, zero, q) * SB_SCALE


ROW_GROUP = 128


def _chains(tq):
    return [(h, g) for g in range(tq // ROW_GROUP) for h in range(2)]


def _chain_pattern(g, m, tk, strict):
    r_lo, r_hi = g * ROW_GROUP, (g + 1) * ROW_GROUP - 1
    c_lo, c_hi = m * tk, (m + 1) * tk - 1
    if (c_lo >= r_hi) if strict else (c_lo > r_hi):
        return None
    if (c_hi < r_lo) if strict else (c_hi <= r_lo):
        return True
    rr = lax.broadcasted_iota(jnp.int32, (ROW_GROUP, tk), 0) + r_lo
    cc = lax.broadcasted_iota(jnp.int32, (ROW_GROUP, tk), 1) + c_lo
    return (cc < rr) if strict else (cc <= rr)


def _masked(x, pat, fill=0.0):
    return x if pat is True else jnp.where(pat, x, fill)


def _rows(g):
    return slice(g * ROW_GROUP, (g + 1) * ROW_GROUP)


def _tri_matrix(tk, cmp):
    rr = lax.broadcasted_iota(jnp.int32, (tk, tk), 0)
    cc = lax.broadcasted_iota(jnp.int32, (tk, tk), 1)
    return cmp(rr, cc).astype(BF16)


def _mla_attn_fwd(qp, kp, vp, *, tq, tk):
    seq = qp.shape[0]
    nd = tq // tk
    neg = float(np.finfo(np.float32).min)
    chains = _chains(tq)

    def body(q_ref, k_ref, v_ref, o_ref, lse_ref):
        i = pl.program_id(1)
        left = lax.broadcasted_iota(jnp.int32, (tq, LANES), 1) < HALF
        qs = [q_ref[_rows(g), h * LANES:(h + 1) * LANES] for h, g in chains]

        def block(j, carry, m):
            start = pl.multiple_of(j * tk, tk)
            v = v_ref[pl.ds(start, tk), :]
            pats = [True if m is None else _chain_pattern(g, m, tk, False) for _, g in chains]
            live = [n for n, p in enumerate(pats) if p is not None]
            ss = {n: _dot_nt(qs[n], k_ref[pl.ds(start, tk), chains[n][0] * LANES:(chains[n][0] + 1) * LANES]) for n in live}
            new = list(carry)
            for n in live:
                m_old, l_old, acc = carry[n]
                s = _masked(ss[n] * MLA_SCALE, pats[n], neg)
                m_new = jnp.maximum(m_old, jnp.max(s, axis=-1, keepdims=True))
                a = jnp.exp(m_old - m_new)
                p = jnp.exp(s - m_new)
                new[n] = (m_new, a * l_old + jnp.sum(p, axis=-1, keepdims=True), a * acc + _dot(_bf(p), v))
            return tuple(new)

        init = (jnp.full((ROW_GROUP, 1), -1e30, F32), jnp.zeros((ROW_GROUP, 1), F32), jnp.zeros((ROW_GROUP, LANES), F32))
        carry = lax.fori_loop(0, i * nd, lambda j, c: block(j, c, None), (init,) * len(chains))
        for m in range(nd):
            carry = block(i * nd + m, carry, m)
        per_head = []
        for h in range(2):
            mine = [carry[n] for n, (ch, _) in enumerate(chains) if ch == h]
            per_head.append((jnp.concatenate([acc / l for _, l, acc in mine], axis=0),
                             jnp.concatenate([mm + jnp.log(l) for mm, l, _ in mine], axis=0)))
        o_ref[...] = jnp.where(left, per_head[0][0], per_head[1][0])
        lse_ref[...] = jnp.where(left, per_head[0][1], per_head[1][1])

    return pl.pallas_call(
        body, name="mla_fwd", grid=(MLA_HEADS // 2, seq // tq),
        in_specs=[pl.BlockSpec((tq, 2 * LANES), lambda p, i: (i, p)), pl.BlockSpec((seq, 2 * LANES), lambda p, i: (0, p)),
                  pl.BlockSpec((seq, LANES), lambda p, i: (0, p))],
        out_specs=[pl.BlockSpec((tq, LANES), lambda p, i: (i, p)), pl.BlockSpec((tq, LANES), lambda p, i: (i, p))],
        out_shape=[jax.ShapeDtypeStruct((seq, 512), F32), jax.ShapeDtypeStruct((seq, 512), F32)],
        compiler_params=_cparams(("parallel", "parallel")),
    )(qp, kp, vp)


def _mla_attn_bwd(qp, kp, vp, o, lse, do, *, tq, tk):
    seq = qp.shape[0]
    nd = tq // tk
    chains = _chains(tq)

    def body(q_ref, k_ref, v_ref, o_ref, lse_ref, do_ref, dq_ref, dk_ref, dv_ref):
        i = pl.program_id(1)

        @pl.when(i == 0)
        def _():
            dk_ref[...] = jnp.zeros_like(dk_ref)
            dv_ref[...] = jnp.zeros_like(dv_ref)

        left = lax.broadcasted_iota(jnp.int32, (tq, LANES), 1) < HALF
        do_f = do_ref[...]
        prod = do_f * o_ref[...]
        lse_v = lse_ref[...]
        do_heads = (_bf(jnp.where(left, do_f, 0.0)), _bf(jnp.where(left, 0.0, do_f)))
        delta_heads = (jnp.sum(jnp.where(left, prod, 0.0), axis=-1, keepdims=True),
                       jnp.sum(jnp.where(left, 0.0, prod), axis=-1, keepdims=True))
        qs = [q_ref[_rows(g), h * LANES:(h + 1) * LANES] for h, g in chains]
        dos = [do_heads[h][_rows(g)] for h, g in chains]
        deltas = [delta_heads[h][_rows(g)] for h, g in chains]
        lses = [lse_v[_rows(g), h * HALF:h * HALF + 1] for h, g in chains]

        def block(j, carry, m):
            start = pl.multiple_of(j * tk, tk)
            v = v_ref[pl.ds(start, tk), :]
            pats = [True if m is None else _chain_pattern(g, m, tk, False) for _, g in chains]
            live = [n for n, p in enumerate(pats) if p is not None]
            ks = [k_ref[pl.ds(start, tk), h * LANES:(h + 1) * LANES] for h in range(2)]
            ss = {n: _dot_nt(qs[n], ks[chains[n][0]]) for n in live}
            dps = {n: _dot_nt(dos[n], v) for n in live}
            new = list(carry)
            ps, dss = {}, {}
            for n in live:
                p = _masked(jnp.exp(ss[n] * MLA_SCALE - lses[n]), pats[n])
                ps[n] = _bf(p)
                dss[n] = _bf(p * (dps[n] - deltas[n]) * MLA_SCALE)
                new[n] = carry[n] + _dot(dss[n], ks[chains[n][0]])
            for h in range(2):
                mine = [n for n in live if chains[n][0] == h]
                ds_cat = jnp.concatenate([dss[n] for n in mine], axis=0)
                q_cat = jnp.concatenate([qs[n] for n in mine], axis=0)
                dk_ref[pl.ds(start, tk), h * LANES:(h + 1) * LANES] += _dot_tn(ds_cat, q_cat)
            p_cat = jnp.concatenate([ps[n] for n in live], axis=0)
            do_cat = jnp.concatenate([dos[n] for n in live], axis=0)
            dv_ref[pl.ds(start, tk), :] += _dot_tn(p_cat, do_cat)
            return tuple(new)

        zero = jnp.zeros((ROW_GROUP, LANES), F32)
        carry = lax.fori_loop(0, i * nd, lambda j, c: block(j, c, None), (zero,) * len(chains))
        for m in range(nd):
            carry = block(i * nd + m, carry, m)
        for n, (h, g) in enumerate(chains):
            dq_ref[_rows(g), h * LANES:(h + 1) * LANES] = carry[n]

    two_t = pl.BlockSpec((tq, 2 * LANES), lambda p, i: (i, p))
    two_s = pl.BlockSpec((seq, 2 * LANES), lambda p, i: (0, p))
    pair_t = pl.BlockSpec((tq, LANES), lambda p, i: (i, p))
    pair_s = pl.BlockSpec((seq, LANES), lambda p, i: (0, p))
    return pl.pallas_call(
        body, name="mla_bwd", grid=(MLA_HEADS // 2, seq // tq),
        in_specs=[two_t, two_s, pair_s, pair_t, pair_t, pair_t],
        out_specs=[two_t, two_s, pair_s],
        out_shape=[jax.ShapeDtypeStruct((seq, 1024), F32), jax.ShapeDtypeStruct((seq, 1024), F32),
                   jax.ShapeDtypeStruct((seq, 512), F32)],
        compiler_params=_cparams(("parallel", "arbitrary")),
    )(qp, kp, vp, o, lse, do)


def _sb_attn_fwd(pbf, *, tq, tk):
    seq = pbf.shape[0]
    nd = tq // tk
    qb, kb, vb = BLK_QB * 4, BLK_KB * 4, BLK_VB * 4
    chains = _chains(tq)

    def body(q_ref, k_ref, v_ref, o_ref, tot_ref):
        i = pl.program_id(1)
        u_later = _tri_matrix(tk, lambda r, c: r > c)
        left = lax.broadcasted_iota(jnp.int32, (tq, LANES), 1) < HALF
        q_heads = _head_queries(q_ref[...], left)
        qs = [q_heads[h][_rows(g)] for h, g in chains]

        def block(j, carry, m):
            start = pl.multiple_of(j * tk, tk)
            k = k_ref[pl.ds(start, tk), :]
            v = v_ref[pl.ds(start, tk), :]
            pats = [True if m is None else _chain_pattern(g, m, tk, True) for _, g in chains]
            live = [n for n, p in enumerate(pats) if p is not None]
            zs = {n: _dot_nt(qs[n], k) for n in live}
            raws = {n: _softplus(zs[n]) for n in live}
            sps = {n: _masked(raws[n], pats[n]) for n in live}
            laters = {n: _tri_sum(sps[n], u_later) for n in live}
            new = list(carry)
            for n in live:
                c, acc = carry[n]
                a = _masked(jnp.exp(zs[n] - raws[n] - laters[n] - c), pats[n])
                new[n] = (c + laters[n][:, 0:1] + sps[n][:, 0:1], acc + _dot(_bf(a), v))
            return tuple(new)

        init = (jnp.zeros((ROW_GROUP, 1), F32), jnp.zeros((ROW_GROUP, LANES), F32))
        carry = (init,) * len(chains)
        for m in reversed(range(nd)):
            carry = block(i * nd + m, carry, m)
        carry = lax.fori_loop(0, i * nd, lambda jj, cr: block(i * nd - 1 - jj, cr, None), carry)
        per_head = []
        for h in range(2):
            mine = [carry[n] for n, (ch, _) in enumerate(chains) if ch == h]
            per_head.append((jnp.concatenate([acc for _, acc in mine], axis=0), jnp.concatenate([c for c, _ in mine], axis=0)))
        o_ref[...] = jnp.where(left, per_head[0][0], per_head[1][0])
        tot_ref[...] = jnp.where(left, per_head[0][1], per_head[1][1])

    pair_t = pl.BlockSpec((tq, LANES), lambda p, i: (i, p))
    return pl.pallas_call(
        body, name="sb_fwd", grid=(SB_HEADS // 2, seq // tq),
        in_specs=[pl.BlockSpec((tq, LANES), lambda p, i: (i, qb + p)), pl.BlockSpec((seq, LANES), lambda p, i: (0, kb + p)),
                  pl.BlockSpec((seq, LANES), lambda p, i: (0, vb + p))],
        out_specs=[pair_t, pair_t],
        out_shape=[jax.ShapeDtypeStruct((seq, 512), F32), jax.ShapeDtypeStruct((seq, 512), F32)],
        compiler_params=_cparams(("parallel", "parallel")),
    )(pbf, pbf, pbf)


def _sb_attn_bwd(pbf, tot, do, *, tq, tk):
    seq = pbf.shape[0]
    nd = tq // tk
    qb, kb, vb = BLK_QB * 4, BLK_KB * 4, BLK_VB * 4
    chains = _chains(tq)

    def body(q_ref, k_ref, v_ref, tot_ref, do_ref, dq_ref, dk_ref, dv_ref):
        i = pl.program_id(1)

        @pl.when(i == 0)
        def _():
            dk_ref[...] = jnp.zeros_like(dk_ref)
            dv_ref[...] = jnp.zeros_like(dv_ref)

        u_upto = _tri_matrix(tk, lambda r, c: r <= c)
        u_below = _tri_matrix(tk, lambda r, c: r < c)
        left = lax.broadcasted_iota(jnp.int32, (tq, LANES), 1) < HALF
        q_heads = _head_queries(q_ref[...], left)
        do_f = do_ref[...]
        do_heads = (_bf(jnp.where(left, do_f, 0.0)), _bf(jnp.where(left, 0.0, do_f)))
        tot_v = tot_ref[...]
        qs = [q_heads[h][_rows(g)] for h, g in chains]
        dos = [do_heads[h][_rows(g)] for h, g in chains]
        totals = [tot_v[_rows(g), h * HALF:h * HALF + 1] for h, g in chains]

        def block(j, carry, m):
            start = pl.multiple_of(j * tk, tk)
            k = k_ref[pl.ds(start, tk), :]
            v = v_ref[pl.ds(start, tk), :]
            pats = [True if m is None else _chain_pattern(g, m, tk, True) for _, g in chains]
            live = [n for n, p in enumerate(pats) if p is not None]
            zs = {n: _dot_nt(qs[n], k) for n in live}
            das = {n: _dot_nt(dos[n], v) for n in live}
            raws = {n: _softplus(zs[n]) for n in live}
            sps = {n: _masked(raws[n], pats[n]) for n in live}
            uptos = {n: _tri_sum(sps[n], u_upto) for n in live}
            lbs, a_s, gs = {}, {}, {}
            for n in live:
                lbs[n] = zs[n] - raws[n]
                a =_masked(jnp.exp(lbs[n] - (totals[n] - carry[n][0] - uptos[n])), pats[n])
                a_s[n] = _bf(a)
                gs[n] = das[n] * a
            belows = {n: _tri_sum(gs[n], u_below) for n in live}
            new = list(carry)
            dzs = {}
            for n in live:
                sp_before, g_before, dq_acc = carry[n]
                beta = jnp.exp(lbs[n])
                dz = _masked(gs[n] * (1.0 - beta) - (g_before + belows[n]) * beta, pats[n])
                dzs[n] = _bf(dz)
                new[n] = (sp_before + uptos[n][:, tk - 1:tk], g_before + belows[n][:, tk - 1:tk] + gs[n][:, tk - 1:tk],
                          dq_acc + _dot(dzs[n], k))
            dz_cat = jnp.concatenate([dzs[n] for n in live], axis=0)
            q_cat = jnp.concatenate([qs[n] for n in live], axis=0)
            dk_ref[pl.ds(start, tk), :] += _dot_tn(dz_cat, q_cat)
            a_cat = jnp.concatenate([a_s[n] for n in live], axis=0)
            do_cat = jnp.concatenate([dos[n] for n in live], axis=0)
            dv_ref[pl.ds(start, tk), :] += _dot_tn(a_cat, do_cat)
            return tuple(new)

        zero = jnp.zeros((ROW_GROUP, 1), F32)
        init = (zero, zero, jnp.zeros((ROW_GROUP, LANES), F32))
        carry = lax.fori_loop(0, i * nd, lambda j, cr: block(j, cr, None), (init,) * len(chains))
        for m in range(nd):
            carry = block(i * nd + m, carry, m)
        per_head = [jnp.concatenate([carry[n][2] for n, (ch, _) in enumerate(chains) if ch == h], axis=0) for h in range(2)]
        dq_ref[...] = jnp.where(left, per_head[0], per_head[1]) * SB_SCALE

    pair_t = pl.BlockSpec((tq, LANES), lambda p, i: (i, p))
    pair_s = pl.BlockSpec((seq, LANES), lambda p, i: (0, p))
    return pl.pallas_call(
        body, name="sb_bwd", grid=(SB_HEADS // 2, seq // tq),
        in_specs=[pl.BlockSpec((tq, LANES), lambda p, i: (i, qb + p)), pl.BlockSpec((seq, LANES), lambda p, i: (0, kb + p)),
                  pl.BlockSpec((seq, LANES), lambda p, i: (0, vb + p)), pair_t, pair_t],
        out_specs=[pair_t, pair_s, pair_s],
        out_shape=[jax.ShapeDtypeStruct((seq, 512), F32)] * 3,
        compiler_params=_cparams(("parallel", "arbitrary")),
    )(pbf, pbf, pbf, tot, do)


def _mem_probs(q, k):
    s = _dot_nt(q, k) * MEM_SCALE
    e = jnp.exp(s - jnp.max(s, axis=-1, keepdims=True))
    return e / jnp.sum(e, axis=-1, keepdims=True)


def _mem_fwd(pbf, mkv, *, t):
    seq = pbf.shape[0]

    def body(q_ref, kv_ref, o_ref):
        for h in range(MEM_HEADS):
            lo, hi = h * LANES, (h + 1) * LANES
            p = _mem_probs(q_ref[:, lo:hi], kv_ref[:, lo:hi])
            o_ref[:, lo:hi] = _dot(_bf(p), kv_ref[:, 512 + lo:512 + hi])

    return pl.pallas_call(
        body, name="mem_fwd", grid=(seq // t,),
        in_specs=[pl.BlockSpec((t, 512), lambda i: (i, BLK_QM)), pl.BlockSpec((MEM_LEN, 1024), lambda i: (0, 0))],
        out_specs=pl.BlockSpec((t, 512), lambda i: (i, 0)),
        out_shape=jax.ShapeDtypeStruct((seq, 512), F32),
        compiler_params=_cparams(("parallel",)),
    )(pbf, mkv)


def _mem_bwd(pbf, mkv, do, *, t):
    seq = pbf.shape[0]

    def body(q_ref, kv_ref, do_ref, dq_ref, dkv_ref):
        @pl.when(pl.program_id(0) == 0)
        def _():
            dkv_ref[...] = jnp.zeros_like(dkv_ref)

        for h in range(MEM_HEADS):
            lo, hi = h * LANES, (h + 1) * LANES
            q, k, v = q_ref[:, lo:hi], kv_ref[:, lo:hi], kv_ref[:, 512 + lo:512 + hi]
            do_h = _bf(do_ref[:, lo:hi])
            p = _mem_probs(q, k)
            dp = _dot_nt(do_h, v)
            ds = _bf(p * (dp - jnp.sum(dp * p, axis=-1, keepdims=True)) * MEM_SCALE)
            dq_ref[:, lo:hi] = _dot(ds, k)
            dkv_ref[:, lo:hi] += _dot_tn(ds, q)
            dkv_ref[:, 512 + lo:512 + hi] += _dot_tn(_bf(p), do_h)

    return pl.pallas_call(
        body, name="mem_bwd", grid=(seq // t,),
        in_specs=[pl.BlockSpec((t, 512), lambda i: (i, BLK_QM)), pl.BlockSpec((MEM_LEN, 1024), lambda i: (0, 0)),
                  pl.BlockSpec((t, 512), lambda i: (i, 0))],
        out_specs=[pl.BlockSpec((t, 512), lambda i: (i, 0)), pl.BlockSpec((MEM_LEN, 1024), lambda i: (0, 0))],
        out_shape=[jax.ShapeDtypeStruct((seq, 512), F32), jax.ShapeDtypeStruct((MEM_LEN, 1024), F32)],
        compiler_params=_cparams(("arbitrary",)),
    )(pbf, mkv, do)


def _mid(x, tgt, o_a, o_b, o_m, p32, wmg, bmg, wba, wbb, wbm, wout, ln_g, ln_b, *, t):
    seq = x.shape[0]
    inv_d = 1.0 / D_MODEL

    def body(x_ref, t_ref, oa_ref, ob_ref, om_ref, ga_ref, gb_ref, gm_ref, wmg_ref, bmg_ref, wba_ref, wbb_ref,
             wbm_ref, wout_ref, lg_ref, lb_ref,
             du_ref, mrg_ref, dgp_ref, ha_ref, hb_ref, hm_ref, dya_ref, dyb_ref, dym_ref, doa_ref, dob_ref, dom_ref,
             dga_ref, dgb_ref, dgm_ref, dgain_ref, dbias_ref, dbmg_ref, loss_ref):
        @pl.when(pl.program_id(0) == 0)
        def _():
            dgain_ref[...] = jnp.zeros_like(dgain_ref)
            dbias_ref[...] = jnp.zeros_like(dbias_ref)
            dbmg_ref[...] = jnp.zeros_like(dbmg_ref)
            loss_ref[...] = jnp.zeros_like(loss_ref)

        xv = x_ref[...]
        gate = _sigmoid(_dot(_bf(xv), wmg_ref[...]) + bmg_ref[...])

        branches = []
        merged = None
        for b, (o_ref, g_ref, w_ref, h_ref) in enumerate(((oa_ref, ga_ref, wba_ref, ha_ref), (ob_ref, gb_ref, wbb_ref, hb_ref),
                                                         (om_ref, gm_ref, wbm_ref, hm_ref))):
            o, gt = o_ref[...], g_ref[...]
            sg = _sigmoid(gt)
            silu = gt * sg
            h = _bf(o * silu)
            h_ref[...] = h
            y = _dot(h, w_ref[...])
            g_b = gate[:, b * D_MODEL:(b + 1) * D_MODEL]
            term = g_b * y
            merged = term if merged is None else merged + term
            branches.append((o, gt, sg, silu, y, g_b))
        mrg_b = _bf(merged)
        mrg_ref[...] = mrg_b

        u = DEEPNORM_ALPHA * xv + _dot(mrg_b, wout_ref[...])
        mu = jnp.mean(u, axis=-1, keepdims=True)
        uc = u - mu
        rstd = lax.rsqrt(jnp.mean(uc * uc, axis=-1, keepdims=True) + LN_EPS)
        xhat = uc * rstd
        lg = lg_ref[...]
        y_out = xhat * lg + lb_ref[...]
        err = y_out - t_ref[...]
        loss_ref[...] += 0.5 * jnp.sum(jnp.mean(err * err, axis=-1, keepdims=True), axis=0, keepdims=True)
        dy = err * inv_d
        dgain_ref[...] += jnp.sum(dy * xhat, axis=0, keepdims=True)
        dbias_ref[...] += jnp.sum(dy, axis=0, keepdims=True)
        dxh = dy * lg
        du = rstd * (dxh - jnp.mean(dxh, axis=-1, keepdims=True) - xhat * jnp.mean(dxh * xhat, axis=-1, keepdims=True))
        du_ref[...] = du

        dmerged = _dot_nt(_bf(du), wout_ref[...])
        outs = ((dya_ref, doa_ref, dga_ref, wba_ref), (dyb_ref, dob_ref, dgb_ref, wbb_ref), (dym_ref, dom_ref, dgm_ref, wbm_ref))
        dgp = []
        for (o, gt, sg, silu, y, g_b), (dy_ref, do_ref, dg_ref, w_ref) in zip(branches, outs):
            dyb = _bf(dmerged * g_b)
            dy_ref[...] = dyb
            dgp.append(dmerged * y * g_b * (1.0 - g_b))
            dh = _dot_nt(dyb, w_ref[...])
            do_ref[...] = dh * silu
            dg_ref[...] = _bf(dh * o * (sg * (1.0 + gt * (1.0 - sg))))
        dgp = jnp.concatenate(dgp, axis=1)
        dgp_ref[...] = _bf(dgp)
        dbmg_ref[...] += jnp.sum(dgp, axis=0, keepdims=True)

    row = lambda w: pl.BlockSpec((t, w), lambda i: (i, 0))
    pblk = lambda c: pl.BlockSpec((t, 512), lambda i: (i, c))
    full = lambda shp: pl.BlockSpec(shp, lambda i: (0, 0))
    sds = jax.ShapeDtypeStruct
    return pl.pallas_call(
        body, name="mid", grid=(seq // t,),
        in_specs=[row(1024), row(1024), row(512), row(512), row(512), pblk(BLK_GATE_A), pblk(BLK_GATE_B), pblk(BLK_GATE_M),
                  full(wmg.shape), full((1, N_MERGE)), full(wba.shape), full(wbb.shape), full(wbm.shape), full(wout.shape),
                  full((1, D_MODEL)), full((1, D_MODEL))],
        out_specs=[row(1024), row(1024), row(N_MERGE), row(512), row(512), row(512), row(1024), row(1024), row(1024),
                   row(512), row(512), row(512), row(512), row(512), row(512),
                   full((1, D_MODEL)), full((1, D_MODEL)), full((1, N_MERGE)), full((1, 1))],
        out_shape=[sds((seq, 1024), F32), sds((seq, 1024), BF16), sds((seq, N_MERGE), BF16),
                   sds((seq, 512), BF16), sds((seq, 512), BF16), sds((seq, 512), BF16),
                   sds((seq, 1024), BF16), sds((seq, 1024), BF16), sds((seq, 1024), BF16),
                   sds((seq, 512), F32), sds((seq, 512), F32), sds((seq, 512), F32),
                   sds((seq, 512), BF16), sds((seq, 512), BF16), sds((seq, 512), BF16),
                   sds((1, D_MODEL), F32), sds((1, D_MODEL), F32), sds((1, N_MERGE), F32), sds((1, 1), F32)],
        compiler_params=_cparams(("arbitrary",)),
    )(x, tgt, o_a, o_b, o_m, p32, p32, p32, wmg, bmg, wba, wbb, wbm, wout, ln_g, ln_b)


def _primed_weights(w):
    w_in = w["w_in"]
    zc = lambda n: jnp.zeros((D_MODEL, n), w_in.dtype)
    w_in_p = jnp.concatenate([w_in[:, 0:384], zc(64), w_in[:, 384:416], zc(32), w_in[:, 416:]], axis=1)
    wqb = jnp.pad(w["w_q_b"].reshape(MLA_Q_LORA, MLA_HEADS, 96), ((0, 0), (0, 0), (0, 32))).reshape(MLA_Q_LORA, 1024)
    kv3 = w["w_kv_b"].reshape(MLA_KV_LORA, MLA_HEADS, 128)
    wk = jnp.pad(kv3[:, :, :MLA_NOPE], ((0, 0), (0, 0), (0, 64))).reshape(MLA_KV_LORA, 1024)
    wv = kv3[:, :, MLA_NOPE:].reshape(MLA_KV_LORA, 512)
    return w_in_p, wqb, jnp.concatenate([wk, wv], axis=1)


def _local_step(x, mem, tgt, w, small, *, tq, tk, t_row, t_mm):
    seq = x.shape[0]
    w_in_p, wqb, wkvb = _primed_weights(w)
    tabs = _rope_tables(seq)
    wmg, wout = w["w_merge_gate"], w["w_out"]
    wba, wbb, wbm = w["w_branch_mla"], w["w_branch_sb"], w["w_branch_mem"]

    p32, pbf = _matmul(x, w_in_p, mode="nn", tm=t_mm, tn=512, tk=D_MODEL, out_dtypes=(F32, BF16), name="proj_in")
    qp, kp, vp = _mla_prep(p32, small["q_a_gain"], small["kv_a_gain"], wqb, wkvb, tabs, t=t_row)
    o_a, lse = _mla_attn_fwd(qp, kp, vp, tq=tq, tk=tk)
    o_b, keep_total = _sb_attn_fwd(pbf, tq=tq, tk=tk)
    (mkv,) = _matmul(mem, w["w_mem_kv"], mode="nn", tm=MEM_LEN, tn=512, tk=D_MODEL, out_dtypes=(BF16,), name="mem_kv")
    o_m = _mem_fwd(pbf, mkv, t=t_row)

    (du, merged, dgpre, h_a, h_b, h_m, dy_a, dy_b, dy_m, do_a, do_b, do_m, dgate_a, dgate_b, dgate_m,
     d_ln_g, d_ln_b, d_bmg, loss) = _mid(x, tgt, o_a, o_b, o_m, p32, wmg, small["b_merge_gate"], wba, wbb, wbm, wout,
                                         small["ln_gain"], small["ln_bias"], t=t_row)

    dqp, dkp, dvp = _mla_attn_bwd(qp, kp, vp, o_a, lse, do_a, tq=tq, tk=tk)
    dlat, d_wqb, d_wkvb, d_gq, d_gkv = _mla_post(p32, dqp, dkp, dvp, small["q_a_gain"], small["kv_a_gain"], wqb, wkvb, tabs,
                                                 t=t_row)
    dq_b, dk_b, dv_b = _sb_attn_bwd(pbf, keep_total, do_b, tq=tq, tk=tk)
    dq_m, dmkv = _mem_bwd(pbf, mkv, do_m, t=t_row)

    dcat = jnp.concatenate([dgpre, dlat, dgate_a, _bf(dq_b), _bf(dk_b), _bf(dv_b), dgate_b, _bf(dq_m), dgate_m], axis=1)
    wcat = jnp.concatenate([wmg, w_in_p], axis=1)
    (grad_x,) = _matmul(dcat, wcat, mode="nt", tm=t_mm, tn=D_MODEL, tk=1024, out_dtypes=(F32,), name="grad_x",
                        add=du, add_scale=DEEPNORM_ALPHA)
    (d_wcat,) = _matmul(x, dcat, mode="tn", tm=512, tn=1024, tk=t_mm, out_dtypes=(F32,), name="grad_w_cat")
    (d_wout,) = _matmul(merged, du, mode="tn", tm=512, tn=1024, tk=t_mm, out_dtypes=(F32,), name="grad_w_out")
    (d_wba,) = _matmul(h_a, dy_a, mode="tn", tm=512, tn=1024, tk=t_mm, out_dtypes=(F32,), name="grad_w_branch_a")
    (d_wbb,) = _matmul(h_b, dy_b, mode="tn", tm=512, tn=1024, tk=t_mm, out_dtypes=(F32,), name="grad_w_branch_b")
    (d_wbm,) = _matmul(h_m, dy_m, mode="tn", tm=512, tn=1024, tk=t_mm, out_dtypes=(F32,), name="grad_w_branch_m")
    (d_wmemkv,) = _matmul(mem, dmkv, mode="tn", tm=512, tn=1024, tk=MEM_LEN, out_dtypes=(F32,), name="grad_w_mem_kv")

    d_wmg = d_wcat[:, :N_MERGE]
    d_winp = d_wcat[:, N_MERGE:]
    d_win = jnp.concatenate([d_winp[:, 0:384], d_winp[:, 448:480], d_winp[:, 512:]], axis=1)
    d_wq = d_wqb.reshape(MLA_Q_LORA, MLA_HEADS, 128)[:, :, :96].reshape(MLA_Q_LORA, 768)
    d_wk = d_wkvb[:, :1024].reshape(MLA_KV_LORA, MLA_HEADS, 128)[:, :, :MLA_NOPE]
    d_wv = d_wkvb[:, 1024:].reshape(MLA_KV_LORA, MLA_HEADS, MLA_V)
    d_wkv = jnp.concatenate([d_wk, d_wv], axis=2).reshape(MLA_KV_LORA, 1024)
    grads = {"w_in": d_win, "w_mem_kv": d_wmemkv, "w_q_b": d_wq, "w_kv_b": d_wkv, "w_branch_mla": d_wba,
             "w_branch_sb": d_wbb, "w_branch_mem": d_wbm, "w_merge_gate": d_wmg, "w_out": d_wout}
    small_grads = {"q_a_gain": d_gq, "kv_a_gain": d_gkv, "b_merge_gate": d_bmg, "ln_gain": d_ln_g, "ln_bias": d_ln_b}
    return loss[0, 0], grad_x, grads, small_grads


def _pack_shards(shards, small):
    flat_small = jnp.concatenate([small[n].reshape(-1) for n, _ in SMALL_SIZES])
    flat_small = jnp.pad(flat_small, (0, 8 * PACK_COLS - SMALL_TOTAL)).reshape(8, PACK_COLS)
    parts = [shards[n].reshape(-1, PACK_COLS) for n, _ in PACK_ROWS[:-1]] + [flat_small]
    return jnp.concatenate(parts, axis=0)


def _unpack_shards(pack, shapes):
    out, r0 = {}, 0
    for n, rows in PACK_ROWS[:-1]:
        out[n] = pack[r0:r0 + rows].reshape(shapes[n])
        r0 += rows
    flat = pack[r0:r0 + 8].reshape(-1)
    small, c0 = {}, 0
    for n, size in SMALL_SIZES:
        small[n] = flat[c0:c0 + size].reshape(1, size)
        c0 += size
    return out, small


def _split_by_chip(name, full):
    r, c = full.shape
    if name in COL_SHARDED:
        s = full.reshape(r, N_CHIPS, c // N_CHIPS).transpose(1, 0, 2)
    else:
        s = full.reshape(N_CHIPS, r // N_CHIPS, c)
    return s.reshape(N_CHIPS, -1, PACK_COLS)


def _join_chips(name, packed4):
    r, c = FULL_SHAPES[name]
    if name in COL_SHARDED:
        return packed4.reshape(N_CHIPS, r, c // N_CHIPS).transpose(1, 0, 2).reshape(r, c)
    return packed4.reshape(r, c)


def _place():
    x, y, c = lax.axis_index("x"), lax.axis_index("y"), lax.axis_index("c")
    return x, y, c


def _other_chips(x, y):
    return ((1 - x, y), (x, 1 - y), (1 - x, 1 - y))


def _gather_weights(wpack):
    rows = wpack.shape[0]
    chunk = rows // 4

    def body(w_ref, out_ref, wb_ref, send_sems, recv_sems, local_sem):
        x, y, c = _place()
        me = 2 * x + y
        for r in range(4):
            wb_ref[r * chunk:(r + 1) * chunk, :] = _bf(w_ref[r * chunk:(r + 1) * chunk, :])
        mine = pltpu.make_async_copy(wb_ref, out_ref.at[me], local_sem)
        mine.start()
        copies = []
        for k, (px, py) in enumerate(_other_chips(x, y)):
            cp = pltpu.make_async_remote_copy(src_ref=wb_ref, dst_ref=out_ref.at[me], send_sem=send_sems.at[k],
                                              recv_sem=recv_sems.at[k], device_id=(px, py, c), device_id_type=MESH_ID)
            cp.start()
            copies.append(cp)
        for k, (px, py) in enumerate(_other_chips(x, y)):
            pltpu.make_async_remote_copy(src_ref=wb_ref, dst_ref=out_ref.at[2 * px + py], send_sem=send_sems.at[k],
                                         recv_sem=recv_sems.at[k], device_id=(px, py, c), device_id_type=MESH_ID).wait_recv()
        for cp in copies:
            cp.wait_send()
        mine.wait()

    return pl.pallas_call(
        body, name="gather_weights",
        in_specs=[pl.BlockSpec(memory_space=pltpu.VMEM)],
        out_specs=pl.BlockSpec(memory_space=pltpu.HBM),
        out_shape=jax.ShapeDtypeStruct((N_CHIPS, rows, PACK_COLS), BF16),
        scratch_shapes=[pltpu.VMEM((rows, PACK_COLS), BF16), pltpu.SemaphoreType.DMA((3,)), pltpu.SemaphoreType.DMA((3,)),
                        pltpu.SemaphoreType.DMA],
        compiler_params=pltpu.CompilerParams(vmem_limit_bytes=VMEM_LIMIT),
    )(wpack)


def _to_sibling_half(gpack):
    def body(g_ref, out_ref, send_sems, recv_sems):
        x, y, c = _place()
        theirs = pl.ds(pl.multiple_of((1 - c) * PACK_HALF, 8), PACK_HALF)
        copies = [pltpu.make_async_remote_copy(src_ref=g_ref.at[j, theirs, :], dst_ref=out_ref.at[j],
                                               send_sem=send_sems.at[j], recv_sem=recv_sems.at[j],
                                               device_id=(x, y, 1 - c), device_id_type=MESH_ID) for j in range(N_CHIPS)]
        for cp in copies:
            cp.start()
        for cp in copies:
            cp.wait()

    return pl.pallas_call(
        body, name="rs_sibling",
        in_specs=[pl.BlockSpec(memory_space=pltpu.HBM)],
        out_specs=pl.BlockSpec(memory_space=pltpu.HBM),
        out_shape=jax.ShapeDtypeStruct((N_CHIPS, PACK_HALF, PACK_COLS), F32),
        scratch_shapes=[pltpu.SemaphoreType.DMA((N_CHIPS,)), pltpu.SemaphoreType.DMA((N_CHIPS,))],
    )(gpack)


def _add_sibling(gpack, got):
    tr = PACK_HALF // 4

    def body(c_ref, g_ref, r_ref, o_ref):
        o_ref[...] = g_ref[...] + r_ref[...]

    grid_spec = pltpu.PrefetchScalarGridSpec(
        num_scalar_prefetch=1, grid=(N_CHIPS, 4),
        in_specs=[pl.BlockSpec((1, tr, PACK_COLS), lambda j, i, c_ref: (j, c_ref[0] * 4 + i, 0)),
                  pl.BlockSpec((1, tr, PACK_COLS), lambda j, i, c_ref: (j, i, 0))],
        out_specs=pl.BlockSpec((1, tr, PACK_COLS), lambda j, i, c_ref: (j, i, 0)))
    return pl.pallas_call(
        body, name="rs_add_sibling", grid_spec=grid_spec,
        out_shape=jax.ShapeDtypeStruct((N_CHIPS, PACK_HALF, PACK_COLS), F32),
        compiler_params=_cparams(("parallel", "parallel")),
    )(lax.axis_index("c").astype(jnp.int32).reshape(1), gpack, got)


def _to_owner_chips(chipsum):
    def body(s_ref, out_ref, send_sems, recv_sems, local_sem):
        x, y, c = _place()
        me = 2 * x + y
        mine = pltpu.make_async_copy(s_ref.at[me], out_ref.at[me], local_sem)
        mine.start()
        copies = []
        for k, (px, py) in enumerate(_other_chips(x, y)):
            cp = pltpu.make_async_remote_copy(src_ref=s_ref.at[2 * px + py], dst_ref=out_ref.at[me], send_sem=send_sems.at[k],
                                              recv_sem=recv_sems.at[k], device_id=(px, py, c), device_id_type=MESH_ID)
            cp.start()
            copies.append(cp)
        for k, (px, py) in enumerate(_other_chips(x, y)):
            pltpu.make_async_remote_copy(src_ref=s_ref.at[me], dst_ref=out_ref.at[2 * px + py], send_sem=send_sems.at[k],
                                         recv_sem=recv_sems.at[k], device_id=(px, py, c), device_id_type=MESH_ID).wait_recv()
        for cp in copies:
            cp.wait_send()
        mine.wait()

    return pl.pallas_call(
        body, name="rs_chips",
        in_specs=[pl.BlockSpec(memory_space=pltpu.HBM)],
        out_specs=pl.BlockSpec(memory_space=pltpu.HBM),
        out_shape=jax.ShapeDtypeStruct(chipsum.shape, F32),
        scratch_shapes=[pltpu.SemaphoreType.DMA((3,)), pltpu.SemaphoreType.DMA((3,)), pltpu.SemaphoreType.DMA],
    )(chipsum)


def _add_chips(parts):
    tr = PACK_HALF // 4

    def body(p_ref, o_ref):
        o_ref[...] = ((p_ref[0] + p_ref[1]) + p_ref[2]) + p_ref[3]

    return pl.pallas_call(
        body, name="rs_add_chips", grid=(4,),
        in_specs=[pl.BlockSpec((N_CHIPS, tr, PACK_COLS), lambda i: (0, i, 0))],
        out_specs=pl.BlockSpec((tr, PACK_COLS), lambda i: (i, 0)),
        out_shape=jax.ShapeDtypeStruct((PACK_HALF, PACK_COLS), F32),
        compiler_params=_cparams(("parallel",)),
    )(parts)


def _swap_halves(half):
    def body(h_ref, out_ref, send_sem, recv_sem, local_sem):
        x, y, c = _place()
        my_rows = pl.ds(pl.multiple_of(c * PACK_HALF, 8), PACK_HALF)
        mine = pltpu.make_async_copy(h_ref, out_ref.at[my_rows, :], local_sem)
        mine.start()
        cp = pltpu.make_async_remote_copy(src_ref=h_ref, dst_ref=out_ref.at[my_rows, :], send_sem=send_sem, recv_sem=recv_sem,
                                          device_id=(x, y, 1 - c), device_id_type=MESH_ID)
        cp.start()
        their_rows = pl.ds(pl.multiple_of((1 - c) * PACK_HALF, 8), PACK_HALF)
        pltpu.make_async_remote_copy(src_ref=h_ref, dst_ref=out_ref.at[their_rows, :], send_sem=send_sem, recv_sem=recv_sem,
                                     device_id=(x, y, 1 - c), device_id_type=MESH_ID).wait_recv()
        cp.wait_send()
        mine.wait()

    return pl.pallas_call(
        body, name="rs_swap_halves",
        in_specs=[pl.BlockSpec(memory_space=pltpu.HBM)],
        out_specs=pl.BlockSpec(memory_space=pltpu.HBM),
        out_shape=jax.ShapeDtypeStruct((PACK_TOTAL, PACK_COLS), F32),
        scratch_shapes=[pltpu.SemaphoreType.DMA, pltpu.SemaphoreType.DMA, pltpu.SemaphoreType.DMA],
    )(half)


def _adamw(w, g, m, v):
    tr = PACK_TOTAL // 8

    def body(w_ref, g_ref, m_ref, v_ref, d_ref, nm_ref, nv_ref):
        gv = g_ref[...]
        m_new = ADAM_B1 * m_ref[...] + (1.0 - ADAM_B1) * gv
        v_new = ADAM_B2 * v_ref[...] + (1.0 - ADAM_B2) * (gv * gv)
        m_hat = m_new / (1.0 - ADAM_B1 ** ADAM_STEP)
        v_hat = v_new / (1.0 - ADAM_B2 ** ADAM_STEP)
        d_ref[...] = -ADAM_LR * (m_hat / (jnp.sqrt(v_hat) + ADAM_EPS) + ADAM_WD * w_ref[...])
        nm_ref[...] = m_new
        nv_ref[...] = v_new

    blk = pl.BlockSpec((tr, PACK_COLS), lambda i: (i, 0))
    return pl.pallas_call(
        body, name="adamw", grid=(8,),
        in_specs=[blk] * 4, out_specs=[blk] * 3,
        out_shape=[jax.ShapeDtypeStruct((PACK_TOTAL, PACK_COLS), F32)] * 3,
        compiler_params=_cparams(("parallel",)),
    )(w, g, m, v)


SMALL_ROWS = 32


def _pack_small(d):
    flat = jnp.concatenate([d[n].reshape(-1) for n, _ in SMALL_SIZES])
    return jnp.pad(flat, (0, SMALL_ROWS * PACK_COLS - SMALL_TOTAL)).reshape(SMALL_ROWS, PACK_COLS)


def _unpack_small(a):
    flat, out, c0 = a.reshape(-1), {}, 0
    for n, size in SMALL_SIZES:
        out[n] = flat[c0:c0 + size].reshape(1, size)
        c0 += size
    return out


def _split_by_chip(name, full):
    r, c = full.shape
    if name in COL_SHARDED:
        return full.reshape(r, N_CHIPS, c // N_CHIPS).transpose(1, 0, 2)
    return full.reshape(N_CHIPS, r // N_CHIPS, c)


def _join_chips(name, slots):
    _, r, cs = slots.shape
    if name in COL_SHARDED:
        return slots.transpose(1, 0, 2).reshape(r, N_CHIPS * cs)
    return slots.reshape(N_CHIPS * r, cs)


HBM_SPEC = pl.BlockSpec(memory_space=pltpu.HBM)


def _gather_shards(shards):
    n = len(shards)

    def body(*refs):
        w_refs, out_refs, wb_refs = refs[:n], refs[n:2 * n], refs[2 * n:3 * n]
        send_sems, recv_sems, local_sems = refs[3 * n:]
        x, y, c = _place()
        me = 2 * x + y
        for w_ref, wb_ref in zip(w_refs, wb_refs):
            rows = w_ref.shape[0]
            chunk = min(rows, 128)

            def cast(i, carry, w_ref=w_ref, wb_ref=wb_ref, chunk=chunk):
                r0 = pl.multiple_of(i * chunk, chunk)
                wb_ref[pl.ds(r0, chunk), :] = _bf(w_ref[pl.ds(r0, chunk), :])
                return carry

            lax.fori_loop(0, rows // chunk, cast, 0)
        sends, locals_ = [], []
        for a, (wb_ref, out_ref) in enumerate(zip(wb_refs, out_refs)):
            mine = pltpu.make_async_copy(wb_ref, out_ref.at[me], local_sems.at[a])
            mine.start()
            locals_.append(mine)
            for k, (px, py) in enumerate(_other_chips(x, y)):
                cp = pltpu.make_async_remote_copy(src_ref=wb_ref, dst_ref=out_ref.at[me], send_sem=send_sems.at[k, a],
                                                  recv_sem=recv_sems.at[k, a], device_id=(px, py, c), device_id_type=MESH_ID)
                cp.start()
                sends.append(cp)
        for a, (wb_ref, out_ref) in enumerate(zip(wb_refs, out_refs)):
            for k, (px, py) in enumerate(_other_chips(x, y)):
                pltpu.make_async_remote_copy(src_ref=wb_ref, dst_ref=out_ref.at[2 * px + py], send_sem=send_sems.at[k, a],
                                             recv_sem=recv_sems.at[k, a], device_id=(px, py, c),
                                             device_id_type=MESH_ID).wait_recv()
        for cp in sends:
            cp.wait_send()
        for cp in locals_:
            cp.wait()

    return pl.pallas_call(
        body, name="gather_weights",
        in_specs=[pl.BlockSpec(memory_space=pltpu.VMEM)] * n,
        out_specs=[HBM_SPEC] * n,
        out_shape=[jax.ShapeDtypeStruct((N_CHIPS,) + s.shape, BF16) for s in shards],
        scratch_shapes=[pltpu.VMEM(s.shape, BF16) for s in shards]
        + [pltpu.SemaphoreType.DMA((3, n)), pltpu.SemaphoreType.DMA((3, n)), pltpu.SemaphoreType.DMA((n,))],
        compiler_params=pltpu.CompilerParams(vmem_limit_bytes=VMEM_LIMIT),
    )(*shards)


def _rs_to_sibling(g4):
    n = len(g4)

    def body(*refs):
        g_refs, out_refs = refs[:n], refs[n:2 * n]
        send_sems, recv_sems = refs[2 * n:]
        x, y, c = _place()
        copies = []
        for a, (g_ref, out_ref) in enumerate(zip(g_refs, out_refs)):
            half = g_ref.shape[1] // 2
            theirs = pl.ds(pl.multiple_of((1 - c) * half, 8), half)
            copies.append(pltpu.make_async_remote_copy(src_ref=g_ref.at[:, theirs, :], dst_ref=out_ref, send_sem=send_sems.at[a],
                                                       recv_sem=recv_sems.at[a], device_id=(x, y, 1 - c),
                                                       device_id_type=MESH_ID))
        for cp in copies:
            cp.start()
        for cp in copies:
            cp.wait()

    return pl.pallas_call(
        body, name="rs_sibling", in_specs=[HBM_SPEC] * n, out_specs=[HBM_SPEC] * n,
        out_shape=[jax.ShapeDtypeStruct((N_CHIPS, g.shape[1] // 2, g.shape[2]), F32) for g in g4],
        scratch_shapes=[pltpu.SemaphoreType.DMA((n,)), pltpu.SemaphoreType.DMA((n,))],
    )(*g4)


def _rs_add_sibling(g4, got):
    n = len(g4)

    def body(c_ref, *refs):
        for g_ref, r_ref, o_ref in zip(refs[:n], refs[n:2 * n], refs[2 * n:]):
            o_ref[...] = g_ref[...] + r_ref[...]

    blk = lambda r: (1, r.shape[1], r.shape[2])
    grid_spec = pltpu.PrefetchScalarGridSpec(
        num_scalar_prefetch=1, grid=(N_CHIPS,),
        in_specs=[pl.BlockSpec(blk(r), lambda j, c_ref: (j, c_ref[0], 0)) for r in got]
        + [pl.BlockSpec(blk(r), lambda j, c_ref: (j, 0, 0)) for r in got],
        out_specs=[pl.BlockSpec(blk(r), lambda j, c_ref: (j, 0, 0)) for r in got])
    return pl.pallas_call(
        body, name="rs_add_sibling", grid_spec=grid_spec,
        out_shape=[jax.ShapeDtypeStruct(r.shape, F32) for r in got],
        compiler_params=_cparams(("parallel",)),
    )(lax.axis_index("c").astype(jnp.int32).reshape(1), *g4, *got)


def _rs_to_chips(chipsum):
    n = len(chipsum)

    def body(*refs):
        s_refs, out_refs = refs[:n], refs[n:2 * n]
        send_sems, recv_sems, local_sems = refs[2 * n:]
        x, y, c = _place()
        me = 2 * x + y
        sends, locals_ = [], []
        for a, (s_ref, out_ref) in enumerate(zip(s_refs, out_refs)):
            mine = pltpu.make_async_copy(s_ref.at[me], out_ref.at[me], local_sems.at[a])
            mine.start()
            locals_.append(mine)
            for k, (px, py) in enumerate(_other_chips(x, y)):
                cp = pltpu.make_async_remote_copy(src_ref=s_ref.at[2 * px + py], dst_ref=out_ref.at[me],
                                                  send_sem=send_sems.at[k, a], recv_sem=recv_sems.at[k, a],
                                                  device_id=(px, py, c), device_id_type=MESH_ID)
                cp.start()
                sends.append(cp)
        for a, (s_ref, out_ref) in enumerate(zip(s_refs, out_refs)):
            for k, (px, py) in enumerate(_other_chips(x, y)):
                pltpu.make_async_remote_copy(src_ref=s_ref.at[me], dst_ref=out_ref.at[2 * px + py], send_sem=send_sems.at[k, a],
                                             recv_sem=recv_sems.at[k, a], device_id=(px, py, c),
                                             device_id_type=MESH_ID).wait_recv()
        for cp in sends:
            cp.wait_send()
        for cp in locals_:
            cp.wait()

    return pl.pallas_call(
        body, name="rs_chips", in_specs=[HBM_SPEC] * n, out_specs=[HBM_SPEC] * n,
        out_shape=[jax.ShapeDtypeStruct(s.shape, F32) for s in chipsum],
        scratch_shapes=[pltpu.SemaphoreType.DMA((3, n)), pltpu.SemaphoreType.DMA((3, n)), pltpu.SemaphoreType.DMA((n,))],
    )(*chipsum)


def _rs_add_chips(parts):
    n = len(parts)

    def body(*refs):
        for p_ref, o_ref in zip(refs[:n], refs[n:]):
            o_ref[...] = ((p_ref[0] + p_ref[1]) + p_ref[2]) + p_ref[3]

    return pl.pallas_call(
        body, name="rs_add_chips", grid=(2,),
        in_specs=[pl.BlockSpec((N_CHIPS, p.shape[1] // 2, p.shape[2]), lambda i: (0, i, 0)) for p in parts],
        out_specs=[pl.BlockSpec((p.shape[1] // 2, p.shape[2]), lambda i: (i, 0)) for p in parts],
        out_shape=[jax.ShapeDtypeStruct(p.shape[1:], F32) for p in parts],
        compiler_params=_cparams(("parallel",)),
    )(*parts)


def _rs_swap_halves(halves):
    n = len(halves)

    def body(*refs):
        h_refs, out_refs = refs[:n], refs[n:2 * n]
        send_sems, recv_sems, local_sems = refs[2 * n:]
        x, y, c = _place()
        sends, locals_ = [], []
        for a, (h_ref, out_ref) in enumerate(zip(h_refs, out_refs)):
            half = h_ref.shape[0]
            my_rows = pl.ds(pl.multiple_of(c * half, 8), half)
            mine = pltpu.make_async_copy(h_ref, out_ref.at[my_rows, :], local_sems.at[a])
            mine.start()
            locals_.append(mine)
            cp = pltpu.make_async_remote_copy(src_ref=h_ref, dst_ref=out_ref.at[my_rows, :], send_sem=send_sems.at[a],
                                              recv_sem=recv_sems.at[a], device_id=(x, y, 1 - c), device_id_type=MESH_ID)
            cp.start()
            sends.append(cp)
        for a, (h_ref, out_ref) in enumerate(zip(h_refs, out_refs)):
            half = h_ref.shape[0]
            their_rows = pl.ds(pl.multiple_of((1 - c) * half, 8), half)
            pltpu.make_async_remote_copy(src_ref=h_ref, dst_ref=out_ref.at[their_rows, :], send_sem=send_sems.at[a],
                                         recv_sem=recv_sems.at[a], device_id=(x, y, 1 - c), device_id_type=MESH_ID).wait_recv()
        for cp in sends:
            cp.wait_send()
        for cp in locals_:
            cp.wait()

    return pl.pallas_call(
        body, name="rs_swap_halves", in_specs=[HBM_SPEC] * n, out_specs=[HBM_SPEC] * n,
        out_shape=[jax.ShapeDtypeStruct((2 * h.shape[0], h.shape[1]), F32) for h in halves],
        scratch_shapes=[pltpu.SemaphoreType.DMA((n,)), pltpu.SemaphoreType.DMA((n,)), pltpu.SemaphoreType.DMA((n,))],
    )(*halves)


def _adamw_list(ws, gs, ms, vs):
    n = len(ws)

    def body(*refs):
        w_refs, g_refs, m_refs, v_refs = refs[:n], refs[n:2 * n], refs[2 * n:3 * n], refs[3 * n:4 * n]
        d_refs, nm_refs, nv_refs = refs[4 * n:5 * n], refs[5 * n:6 * n], refs[6 * n:]
        for w_ref, g_ref, m_ref, v_ref, d_ref, nm_ref, nv_ref in zip(w_refs, g_refs, m_refs, v_refs, d_refs, nm_refs, nv_refs):
            gv = g_ref[...]
            m_new = ADAM_B1 * m_ref[...] + (1.0 - ADAM_B1) * gv
            v_new = ADAM_B2 * v_ref[...] + (1.0 - ADAM_B2) * (gv * gv)
            m_hat = m_new / (1.0 - ADAM_B1 ** ADAM_STEP)
            v_hat = v_new / (1.0 - ADAM_B2 ** ADAM_STEP)
            d_ref[...] = -ADAM_LR * (m_hat / (jnp.sqrt(v_hat) + ADAM_EPS) + ADAM_WD * w_ref[...])
            nm_ref[...] = m_new
            nv_ref[...] = v_new

    specs = [pl.BlockSpec((w.shape[0] // 4, w.shape[1]), lambda i: (i, 0)) for w in ws]
    shapes = [jax.ShapeDtypeStruct(w.shape, F32) for w in ws]
    res = pl.pallas_call(
        body, name="adamw", grid=(4,),
        in_specs=specs * 4, out_specs=specs * 3, out_shape=shapes * 3,
        compiler_params=_cparams(("parallel",)),
    )(*ws, *gs, *ms, *vs)
    return res[:n], res[n:2 * n], res[2 * n:]


WEIGHT_NAMES = ("w_in", "w_mem_kv", "q_a_gain", "w_q_b", "kv_a_gain", "w_kv_b", "w_branch_mla", "w_branch_sb",
                "w_branch_mem", "w_merge_gate", "b_merge_gate", "w_out", "ln_gain", "ln_bias")
BIG_NAMES = tuple(n for n, _ in PACK_ROWS[:-1])
SMALL_NAMES = tuple(n for n, _ in SMALL_SIZES)


def kernel(x, mem, w_in, w_mem_kv, q_a_gain, w_q_b, kv_a_gain, w_kv_b, w_branch_mla, w_branch_sb, w_branch_mem, w_merge_gate, b_merge_gate, w_out, ln_gain, ln_bias, loss_target, m_w_in, m_w_mem_kv, m_q_a_gain, m_w_q_b, m_kv_a_gain, m_w_kv_b, m_w_branch_mla, m_w_branch_sb, m_w_branch_mem, m_w_merge_gate, m_b_merge_gate, m_w_out, m_ln_gain, m_ln_bias, v_w_in, v_w_mem_kv, v_q_a_gain, v_w_q_b, v_kv_a_gain, v_w_kv_b, v_w_branch_mla, v_w_branch_sb, v_w_branch_mem, v_w_merge_gate, v_b_merge_gate, v_w_out, v_ln_gain, v_ln_bias):
    weights = dict(zip(WEIGHT_NAMES, (w_in, w_mem_kv, q_a_gain, w_q_b, kv_a_gain, w_kv_b, w_branch_mla, w_branch_sb,
                                      w_branch_mem, w_merge_gate, b_merge_gate, w_out, ln_gain, ln_bias)))
    mom1 = dict(zip(WEIGHT_NAMES, (m_w_in, m_w_mem_kv, m_q_a_gain, m_w_q_b, m_kv_a_gain, m_w_kv_b, m_w_branch_mla,
                                   m_w_branch_sb, m_w_branch_mem, m_w_merge_gate, m_b_merge_gate, m_w_out, m_ln_gain,
                                   m_ln_bias)))
    mom2 = dict(zip(WEIGHT_NAMES, (v_w_in, v_w_mem_kv, v_q_a_gain, v_w_q_b, v_kv_a_gain, v_w_kv_b, v_w_branch_mla,
                                   v_w_branch_sb, v_w_branch_mem, v_w_merge_gate, v_b_merge_gate, v_w_out, v_ln_gain,
                                   v_ln_bias)))
    def as_list(d):
        return [d[n][0] for n in BIG_NAMES] + [_pack_small({n: d[n] for n in SMALL_NAMES})]

    w_list, m_list, v_list = as_list(weights), as_list(mom1), as_list(mom2)

    gathered = _gather_shards(w_list[:-1])
    full_w = {n: _join_chips(n, g) for n, g in zip(BIG_NAMES, gathered)}
    small = {n: weights[n] for n in SMALL_NAMES}

    loss, grad_x, grads, small_grads = _local_step(x[0], mem[0], loss_target[0], full_w, small,
                                                   tq=512, tk=256, t_row=256, t_mm=512)

    g4 = [_split_by_chip(n, grads[n]) for n in BIG_NAMES]
    g4.append(jnp.broadcast_to(_pack_small(small_grads)[None], (N_CHIPS, SMALL_ROWS, PACK_COLS)))
    got = _rs_to_sibling(g4)
    chipsum = _rs_add_sibling(g4, got)
    parts = _rs_to_chips(chipsum)
    halves = _rs_add_chips(parts)
    g_list = _rs_swap_halves(halves)
    d_list, nm_list, nv_list = _adamw_list(w_list, g_list, m_list, v_list)

    loss = lax.psum(loss, ("x", "y", "c"))
    outs = [loss, grad_x[None]]
    for arrays in (g_list, d_list, nm_list, nv_list):
        big = dict(zip(BIG_NAMES, arrays[:-1]))
        sm = _unpack_small(arrays[-1])
        outs.extend(big[n][None] if n in big else sm[n] for n in WEIGHT_NAMES)
    return tuple(outs)
```

```python
import functools
import math

import numpy as np
import jax
import jax.numpy as jnp
from jax import lax
from jax.experimental import pallas as pl
from jax.experimental.pallas import tpu as pltpu

F32 = jnp.float32
BF16 = jnp.bfloat16
MESH_ID = pl.DeviceIdType.MESH

D_MODEL = 1024
MEM_LEN = 256
MLA_HEADS = 8
MLA_NOPE = 64
MLA_ROPE = 32
MLA_V = 64
MLA_Q_LORA = 256
MLA_KV_LORA = 128
SB_HEADS = 8
SB_HEAD_DIM = 64
MEM_HEADS = 4
MEM_HEAD_DIM = 128
BRANCH_WIDTH = 512
ROPE_BASE = 10000.0
RMS_EPS = 1e-6
LN_EPS = 1e-5
DEEPNORM_ALPHA = 2.0 ** 0.25
MLA_SCALE = 1.0 / math.sqrt(MLA_NOPE + MLA_ROPE)
SB_SCALE = 1.0 / math.sqrt(SB_HEAD_DIM)
MEM_SCALE = 1.0 / math.sqrt(MEM_HEAD_DIM)

ADAM_LR = 0.001
ADAM_B1 = 0.9
ADAM_B2 = 0.999
ADAM_EPS = 1e-08
ADAM_WD = 0.01
ADAM_STEP = 10

LANES = 128
HALF = 64
N_CHIPS = 4
PACK_COLS = 1024
VMEM_LIMIT = 56 * 1024 * 1024

IN_WIDTH_P = 4096
BLK_LAT, BLK_GATE_A, BLK_QB, BLK_KB, BLK_VB, BLK_GATE_B, BLK_QM, BLK_GATE_M = range(8)
N_MERGE = 3 * D_MODEL
CAT_WIDTH = N_MERGE + IN_WIDTH_P

PACK_ROWS = (("w_in", 1000), ("w_mem_kv", 256), ("w_q_b", 48), ("w_kv_b", 32), ("w_branch_mla", 128),
             ("w_branch_sb", 128), ("w_branch_mem", 128), ("w_merge_gate", 768), ("w_out", 256), ("small", 8))
PACK_TOTAL = sum(r for _, r in PACK_ROWS)
PACK_HALF = PACK_TOTAL // 2
SMALL_SIZES = (("q_a_gain", 256), ("kv_a_gain", 128), ("b_merge_gate", 3072), ("ln_gain", 1024), ("ln_bias", 1024))
SMALL_TOTAL = sum(s for _, s in SMALL_SIZES)
COL_SHARDED = ("w_in", "w_q_b", "w_kv_b", "w_branch_mla", "w_branch_sb", "w_branch_mem", "w_merge_gate")
ROW_SHARDED = ("w_mem_kv", "w_out")
FULL_SHAPES = {"w_in": (1024, 4000), "w_mem_kv": (1024, 1024), "w_q_b": (256, 768), "w_kv_b": (128, 1024),
               "w_branch_mla": (512, 1024), "w_branch_sb": (512, 1024), "w_branch_mem": (512, 1024),
               "w_merge_gate": (1024, 3072), "w_out": (1024, 1024)}


def _cparams(sem=None):
    return pltpu.CompilerParams(dimension_semantics=sem, vmem_limit_bytes=VMEM_LIMIT)


def _dot(a, b):
    return jnp.dot(a, b, preferred_element_type=F32)


def _dot_nt(a, b):
    return lax.dot_general(a, b, (((1,), (1,)), ((), ())), preferred_element_type=F32)


def _dot_tn(a, b):
    return lax.dot_general(a, b, (((0,), (0,)), ((), ())), preferred_element_type=F32)


def _bf(x):
    return x.astype(BF16)


def _sigmoid(x):
    return 1.0 / (1.0 + jnp.exp(-x))


def _matmul(a, b, *, mode, tm, tn, tk, out_dtypes, name, add=None, add_scale=1.0):
    if mode == "nn":
        (m, k), n = a.shape, b.shape[1]
        a_spec = pl.BlockSpec((tm, tk), lambda i, j, kk: (i, kk))
        b_spec = pl.BlockSpec((tk, tn), lambda i, j, kk: (kk, j))
        dot = _dot
    elif mode == "nt":
        (m, k), n = a.shape, b.shape[0]
        a_spec = pl.BlockSpec((tm, tk), lambda i, j, kk: (i, kk))
        b_spec = pl.BlockSpec((tn, tk), lambda i, j, kk: (j, kk))
        dot = _dot_nt
    else:
        (k, m), n = a.shape, b.shape[1]
        a_spec = pl.BlockSpec((tk, tm), lambda i, j, kk: (kk, i))
        b_spec = pl.BlockSpec((tk, tn), lambda i, j, kk: (kk, j))
        dot = _dot_tn
    assert m % tm == 0 and n % tn == 0 and k % tk == 0, (name, m, n, k)
    nk = k // tk
    n_out = len(out_dtypes)
    has_add = add is not None

    def body(*refs):
        a_ref, b_ref = refs[0], refs[1]
        add_ref = refs[2] if has_add else None
        outs = refs[2 + has_add: 2 + has_add + n_out]
        acc = refs[-1]
        kk = pl.program_id(2)

        @pl.when(kk == 0)
        def _():
            acc[...] = jnp.zeros_like(acc)

        acc[...] += dot(_bf(a_ref[...]), _bf(b_ref[...]))

        @pl.when(kk == nk - 1)
        def _():
            r = acc[...]
            if has_add:
                r = r + add_scale * add_ref[...]
            for o in outs:
                o[...] = r.astype(o.dtype)

    in_specs = [a_spec, b_spec]
    args = [a, b]
    if has_add:
        in_specs.append(pl.BlockSpec((tm, tn), lambda i, j, kk: (i, j)))
        args.append(add)
    out_spec = pl.BlockSpec((tm, tn), lambda i, j, kk: (i, j))
    res = pl.pallas_call(
        body, name=name, grid=(m // tm, n // tn, nk),
        in_specs=in_specs, out_specs=[out_spec] * n_out,
        out_shape=[jax.ShapeDtypeStruct((m, n), dt) for dt in out_dtypes],
        scratch_shapes=[pltpu.VMEM((tm, tn), F32)],
        compiler_params=_cparams(("parallel", "parallel", "arbitrary")),
    )(*args)
    return res


def _rope_tables(seq):
    half = MLA_ROPE // 2
    freqs = ROPE_BASE ** (-jnp.arange(half, dtype=F32) / half)
    ang = jnp.arange(seq, dtype=jnp.int32).astype(F32)[:, None] * freqs[None, :]
    cos, sin = jnp.cos(ang), jnp.sin(ang)
    z = lambda w: jnp.zeros((seq, w), F32)
    c_q = jnp.concatenate([jnp.ones((seq, MLA_NOPE), F32), cos, cos, z(32)], axis=1)
    c_k = jnp.concatenate([z(MLA_NOPE), cos, cos, z(32)], axis=1)
    s_lo = jnp.concatenate([z(MLA_NOPE), -sin, z(half), z(32)], axis=1)
    s_hi = jnp.concatenate([z(MLA_NOPE), z(half), sin, z(32)], axis=1)
    return c_q, c_k, s_lo, s_hi


def _rope_fwd(x, c, s_lo, s_hi):
    return x * c + pltpu.roll(x, LANES - 16, 1) * s_lo + pltpu.roll(x, 16, 1) * s_hi


def _rope_bwd(d, c, s_lo, s_hi):
    return d * c - pltpu.roll(d, 16, 1) * s_hi - pltpu.roll(d, LANES - 16, 1) * s_lo


def _rms_fwd(x, g):
    r = lax.rsqrt(jnp.mean(x * x, axis=-1, keepdims=True) + RMS_EPS)
    xn = x * r
    return xn * g, xn, r


def _mla_prep(p32, gq, gkv, wqb, wkvb, tabs, *, t):
    seq = p32.shape[0]

    def body(lat_ref, gq_ref, gkv_ref, wqb_ref, wkvb_ref, cq_ref, ck_ref, slo_ref, shi_ref, q_ref, k_ref, v_ref):
        lat = lat_ref[...]
        slo, shi = slo_ref[...], shi_ref[...]
        nq, _, _ = _rms_fwd(lat[:, 0:MLA_Q_LORA], gq_ref[...])
        qa = _dot(_bf(nq), wqb_ref[...])
        cq = cq_ref[...]
        for h in range(MLA_HEADS):
            blk = qa[:, h * LANES:(h + 1) * LANES]
            q_ref[:, h * LANES:(h + 1) * LANES] = _bf(_rope_fwd(blk, cq, slo, shi))
        nkv, _, _ = _rms_fwd(lat[:, MLA_Q_LORA:MLA_Q_LORA + MLA_KV_LORA], gkv_ref[...])
        kv = _dot(_bf(nkv), wkvb_ref[...])
        kpe = _rope_fwd(lat[:, 384:512], ck_ref[...], slo, shi)
        for h in range(MLA_HEADS):
            k_ref[:, h * LANES:(h + 1) * LANES] = _bf(kv[:, h * LANES:(h + 1) * LANES] + kpe)
        v_ref[...] = _bf(kv[:, MLA_HEADS * LANES:])

    row = lambda w: pl.BlockSpec((t, w), lambda i: (i, 0))
    full = lambda shp: pl.BlockSpec(shp, lambda i: (0, 0))
    return pl.pallas_call(
        body, name="mla_prep", grid=(seq // t,),
        in_specs=[row(512), full((1, MLA_Q_LORA)), full((1, MLA_KV_LORA)), full(wqb.shape), full(wkvb.shape),
                  row(LANES), row(LANES), row(LANES), row(LANES)],
        out_specs=[row(1024), row(1024), row(512)],
        out_shape=[jax.ShapeDtypeStruct((seq, 1024), BF16), jax.ShapeDtypeStruct((seq, 1024), BF16),
                   jax.ShapeDtypeStruct((seq, 512), BF16)],
        compiler_params=_cparams(("parallel",)),
    )(p32, gq, gkv, wqb, wkvb, *tabs)


def _mla_post(p32, dq, dk, dv, gq, gkv, wqb, wkvb, tabs, *, t):
    seq = p32.shape[0]

    def body(lat_ref, dq_ref, dk_ref, dv_ref, gq_ref, gkv_ref, wqb_ref, wkvb_ref, cq_ref, ck_ref, slo_ref, shi_ref,
             dlat_ref, dwqb_ref, dwkvb_ref, dgq_ref, dgkv_ref):
        @pl.when(pl.program_id(0) == 0)
        def _():
            dwqb_ref[...] = jnp.zeros_like(dwqb_ref)
            dwkvb_ref[...] = jnp.zeros_like(dwkvb_ref)
            dgq_ref[...] = jnp.zeros_like(dgq_ref)
            dgkv_ref[...] = jnp.zeros_like(dgkv_ref)

        lat = lat_ref[...]
        slo, shi = slo_ref[...], shi_ref[...]
        cq = cq_ref[...]
        gq_v, gkv_v = gq_ref[...], gkv_ref[...]
        nq, xq, rq = _rms_fwd(lat[:, 0:MLA_Q_LORA], gq_v)
        nkv, xkv, rkv = _rms_fwd(lat[:, MLA_Q_LORA:MLA_Q_LORA + MLA_KV_LORA], gkv_v)

        dqa = jnp.concatenate(
            [_rope_bwd(dq_ref[:, h * LANES:(h + 1) * LANES], cq, slo, shi) for h in range(MLA_HEADS)], axis=1)
        dqa_b = _bf(dqa)
        dwqb_ref[...] += _dot_tn(_bf(nq), dqa_b)
        dnq = _dot_nt(dqa_b, wqb_ref[...])
        dgq_ref[...] += jnp.sum(dnq * xq, axis=0, keepdims=True)
        dxn = dnq * gq_v
        dcq = rq * (dxn - xq * jnp.mean(dxn * xq, axis=-1, keepdims=True))

        dkf = dk_ref[...]
        dkv_b = _bf(jnp.concatenate([dkf, dv_ref[...]], axis=1))
        dwkvb_ref[...] += _dot_tn(_bf(nkv), dkv_b)
        dnkv = _dot_nt(dkv_b, wkvb_ref[...])
        dgkv_ref[...] += jnp.sum(dnkv * xkv, axis=0, keepdims=True)
        dxn = dnkv * gkv_v
        dckv = rkv * (dxn - xkv * jnp.mean(dxn * xkv, axis=-1, keepdims=True))

        dkpe = dkf[:, 0:LANES]
        for h in range(1, MLA_HEADS):
            dkpe = dkpe + dkf[:, h * LANES:(h + 1) * LANES]
        dkr = _rope_bwd(dkpe, ck_ref[...], slo, shi)
        dlat_ref[...] = _bf(jnp.concatenate([dcq, dckv, dkr], axis=1))

    row = lambda w: pl.BlockSpec((t, w), lambda i: (i, 0))
    full = lambda shp: pl.BlockSpec(shp, lambda i: (0, 0))
    return pl.pallas_call(
        body, name="mla_post", grid=(seq // t,),
        in_specs=[row(512), row(1024), row(1024), row(512), full((1, MLA_Q_LORA)), full((1, MLA_KV_LORA)),
                  full(wqb.shape), full(wkvb.shape), row(LANES), row(LANES), row(LANES), row(LANES)],
        out_specs=[row(512), full(wqb.shape), full(wkvb.shape), full((1, MLA_Q_LORA)), full((1, MLA_KV_LORA))],
        out_shape=[jax.ShapeDtypeStruct((seq, 512), BF16), jax.ShapeDtypeStruct(wqb.shape, F32),
                   jax.ShapeDtypeStruct(wkvb.shape, F32), jax.ShapeDtypeStruct((1, MLA_Q_LORA), F32),
                   jax.ShapeDtypeStruct((1, MLA_KV_LORA), F32)],
        compiler_params=_cparams(("arbitrary",)),
    )(p32, dq, dk, dv, gq, gkv, wqb, wkvb, *tabs)


def _split_bf16(x):
    hi = _bf(x)
    return hi, _bf(x - hi.astype(F32))


def _tri_sum(x, u):
    hi, lo = _split_bf16(x)
    return _dot(hi, u) + _dot(lo, u)


def _softplus(z):
    return jnp.maximum(z, 0.0) + jnp.log(1.0 + jnp.exp(-jnp.abs(z)))


def _head_queries(q, left):
    zero = jnp.zeros_like(q)
    return jnp.where(left, q, zero) * SB_SCALE, jnp.where(left, zero, q) * SB_SCALE


ROW_GROUP = 128


def _chains(tq):
    return [(h, g) for g in range(tq // ROW_GROUP) for h in range(2)]


def _chain_pattern(g, m, tk, strict):
    r_lo, r_hi = g * ROW_GROUP, (g + 1) * ROW_GROUP - 1
    c_lo, c_hi = m * tk, (m + 1) * tk - 1
    if (c_lo >= r_hi) if strict else (c_lo > r_hi):
        return None
    if (c_hi < r_lo) if strict else (c_hi <= r_lo):
        return True
    rr = lax.broadcasted_iota(jnp.int32, (ROW_GROUP, tk), 0) + r_lo
    cc = lax.broadcasted_iota(jnp.int32, (ROW_GROUP, tk), 1) + c_lo
    return (cc < rr) if strict else (cc <= rr)


def _masked(x, pat, fill=0.0):
    return x if pat is True else jnp.where(pat, x, fill)


def _rows(g):
    return slice(g * ROW_GROUP, (g + 1) * ROW_GROUP)


def _tri_matrix(tk, cmp):
    rr = lax.broadcasted_iota(jnp.int32, (tk, tk), 0)
    cc = lax.broadcasted_iota(jnp.int32, (tk, tk), 1)
    return cmp(rr, cc).astype(BF16)


def _mla_attn_fwd(qp, kp, vp, *, tq, tk):
    seq = qp.shape[0]
    nd = tq // tk
    neg = float(np.finfo(np.float32).min)
    chains = _chains(tq)

    def body(q_ref, k_ref, v_ref, o_ref, lse_ref):
        i = pl.program_id(1)
        left = lax.broadcasted_iota(jnp.int32, (tq, LANES), 1) < HALF
        qs = [q_ref[_rows(g), h * LANES:(h + 1) * LANES] for h, g in chains]

        def block(j, carry, m):
            start = pl.multiple_of(j * tk, tk)
            v = v_ref[pl.ds(start, tk), :]
            pats = [True if m is None else _chain_pattern(g, m, tk, False) for _, g in chains]
            live = [n for n, p in enumerate(pats) if p is not None]
            ss = {n: _dot_nt(qs[n], k_ref[pl.ds(start, tk), chains[n][0] * LANES:(chains[n][0] + 1) * LANES]) for n in live}
            new = list(carry)
            for n in live:
                m_old, l_old, acc = carry[n]
                s = _masked(ss[n] * MLA_SCALE, pats[n], neg)
                m_new = jnp.maximum(m_old, jnp.max(s, axis=-1, keepdims=True))
                a = jnp.exp(m_old - m_new)
                p = jnp.exp(s - m_new)
                new[n] = (m_new, a * l_old + jnp.sum(p, axis=-1, keepdims=True), a * acc + _dot(_bf(p), v))
            return tuple(new)

        init = (jnp.full((ROW_GROUP, 1), -1e30, F32), jnp.zeros((ROW_GROUP, 1), F32), jnp.zeros((ROW_GROUP, LANES), F32))
        carry = lax.fori_loop(0, i * nd, lambda j, c: block(j, c, None), (init,) * len(chains))
        for m in range(nd):
            carry = block(i * nd + m, carry, m)
        per_head = []
        for h in range(2):
            mine = [carry[n] for n, (ch, _) in enumerate(chains) if ch == h]
            per_head.append((jnp.concatenate([acc / l for _, l, acc in mine], axis=0),
                             jnp.concatenate([mm + jnp.log(l) for mm, l, _ in mine], axis=0)))
        o_ref[...] = jnp.where(left, per_head[0][0], per_head[1][0])
        lse_ref[...] = jnp.where(left, per_head[0][1], per_head[1][1])

    return pl.pallas_call(
        body, name="mla_fwd", grid=(MLA_HEADS // 2, seq // tq),
        in_specs=[pl.BlockSpec((tq, 2 * LANES), lambda p, i: (i, p)), pl.BlockSpec((seq, 2 * LANES), lambda p, i: (0, p)),
                  pl.BlockSpec((seq, LANES), lambda p, i: (0, p))],
        out_specs=[pl.BlockSpec((tq, LANES), lambda p, i: (i, p)), pl.BlockSpec((tq, LANES), lambda p, i: (i, p))],
        out_shape=[jax.ShapeDtypeStruct((seq, 512), F32), jax.ShapeDtypeStruct((seq, 512), F32)],
        compiler_params=_cparams(("parallel", "parallel")),
    )(qp, kp, vp)


def _mla_attn_bwd(qp, kp, vp, o, lse, do, *, tq, tk):
    seq = qp.shape[0]
    nd = tq // tk
    chains = _chains(tq)

    def body(q_ref, k_ref, v_ref, o_ref, lse_ref, do_ref, dq_ref, dk_ref, dv_ref):
        i = pl.program_id(1)

        @pl.when(i == 0)
        def _():
            dk_ref[...] = jnp.zeros_like(dk_ref)
            dv_ref[...] = jnp.zeros_like(dv_ref)

        left = lax.broadcasted_iota(jnp.int32, (tq, LANES), 1) < HALF
        do_f = do_ref[...]
        prod = do_f * o_ref[...]
        lse_v = lse_ref[...]
        do_heads = (_bf(jnp.where(left, do_f, 0.0)), _bf(jnp.where(left, 0.0, do_f)))
        delta_heads = (jnp.sum(jnp.where(left, prod, 0.0), axis=-1, keepdims=True),
                       jnp.sum(jnp.where(left, 0.0, prod), axis=-1, keepdims=True))
        qs = [q_ref[_rows(g), h * LANES:(h + 1) * LANES] for h, g in chains]
        dos = [do_heads[h][_rows(g)] for h, g in chains]
        deltas = [delta_heads[h][_rows(g)] for h, g in chains]
        lses = [lse_v[_rows(g), h * HALF:h * HALF + 1] for h, g in chains]

        def block(j, carry, m):
            start = pl.multiple_of(j * tk, tk)
            v = v_ref[pl.ds(start, tk), :]
            pats = [True if m is None else _chain_pattern(g, m, tk, False) for _, g in chains]
            live = [n for n, p in enumerate(pats) if p is not None]
            ks = [k_ref[pl.ds(start, tk), h * LANES:(h + 1) * LANES] for h in range(2)]
            ss = {n: _dot_nt(qs[n], ks[chains[n][0]]) for n in live}
            dps = {n: _dot_nt(dos[n], v) for n in live}
            new = list(carry)
            ps, dss = {}, {}
            for n in live:
                p = _masked(jnp.exp(ss[n] * MLA_SCALE - lses[n]), pats[n])
                ps[n] = _bf(p)
                dss[n] = _bf(p * (dps[n] - deltas[n]) * MLA_SCALE)
                new[n] = carry[n] + _dot(dss[n], ks[chains[n][0]])
            for h in range(2):
                mine = [n for n in live if chains[n][0] == h]
                ds_cat = jnp.concatenate([dss[n] for n in mine], axis=0)
                q_cat = jnp.concatenate([qs[n] for n in mine], axis=0)
                dk_ref[pl.ds(start, tk), h * LANES:(h + 1) * LANES] += _dot_tn(ds_cat, q_cat)
            p_cat = jnp.concatenate([ps[n] for n in live], axis=0)
            do_cat = jnp.concatenate([dos[n] for n in live], axis=0)
            dv_ref[pl.ds(start, tk), :] += _dot_tn(p_cat, do_cat)
            return tuple(new)

        zero = jnp.zeros((ROW_GROUP, LANES), F32)
        carry = lax.fori_loop(0, i * nd, lambda j, c: block(j, c, None), (zero,) * len(chains))
        for m in range(nd):
            carry = block(i * nd + m, carry, m)
        for n, (h, g) in enumerate(chains):
            dq_ref[_rows(g), h * LANES:(h + 1) * LANES] = carry[n]

    two_t = pl.BlockSpec((tq, 2 * LANES), lambda p, i: (i, p))
    two_s = pl.BlockSpec((seq, 2 * LANES), lambda p, i: (0, p))
    pair_t = pl.BlockSpec((tq, LANES), lambda p, i: (i, p))
    pair_s = pl.BlockSpec((seq, LANES), lambda p, i: (0, p))
    return pl.pallas_call(
        body, name="mla_bwd", grid=(MLA_HEADS // 2, seq // tq),
        in_specs=[two_t, two_s, pair_s, pair_t, pair_t, pair_t],
        out_specs=[two_t, two_s, pair_s],
        out_shape=[jax.ShapeDtypeStruct((seq, 1024), F32), jax.ShapeDtypeStruct((seq, 1024), F32),
                   jax.ShapeDtypeStruct((seq, 512), F32)],
        compiler_params=_cparams(("parallel", "arbitrary")),
    )(qp, kp, vp, o, lse, do)


def _sb_attn_fwd(pbf, *, tq, tk):
    seq = pbf.shape[0]
    nd = tq // tk
    qb, kb, vb = BLK_QB * 4, BLK_KB * 4, BLK_VB * 4
    chains = _chains(tq)

    def body(q_ref, k_ref, v_ref, o_ref, tot_ref):
        i = pl.program_id(1)
        u_later = _tri_matrix(tk, lambda r, c: r > c)
        left = lax.broadcasted_iota(jnp.int32, (tq, LANES), 1) < HALF
        q_heads = _head_queries(q_ref[...], left)
        qs = [q_heads[h][_rows(g)] for h, g in chains]

        def block(j, carry, m):
            start = pl.multiple_of(j * tk, tk)
            k = k_ref[pl.ds(start, tk), :]
            v = v_ref[pl.ds(start, tk), :]
            pats = [True if m is None else _chain_pattern(g, m, tk, True) for _, g in chains]
            live = [n for n, p in enumerate(pats) if p is not None]
            zs = {n: _dot_nt(qs[n], k) for n in live}
            raws = {n: _softplus(zs[n]) for n in live}
            sps = {n: _masked(raws[n], pats[n]) for n in live}
            laters = {n: _tri_sum(sps[n], u_later) for n in live}
            new = list(carry)
            for n in live:
                c, acc = carry[n]
                a = _masked(jnp.exp(zs[n] - raws[n] - laters[n] - c), pats[n])
                new[n] = (c + laters[n][:, 0:1] + sps[n][:, 0:1], acc + _dot(_bf(a), v))
            return tuple(new)

        init = (jnp.zeros((ROW_GROUP, 1), F32), jnp.zeros((ROW_GROUP, LANES), F32))
        carry = (init,) * len(chains)
        for m in reversed(range(nd)):
            carry = block(i * nd + m, carry, m)
        carry = lax.fori_loop(0, i * nd, lambda jj, cr: block(i * nd - 1 - jj, cr, None), carry)
        per_head = []
        for h in range(2):
            mine = [carry[n] for n, (ch, _) in enumerate(chains) if ch == h]
            per_head.append((jnp.concatenate([acc for _, acc in mine], axis=0), jnp.concatenate([c for c, _ in mine], axis=0)))
        o_ref[...] = jnp.where(left, per_head[0][0], per_head[1][0])
        tot_ref[...] = jnp.where(left, per_head[0][1], per_head[1][1])

    pair_t = pl.BlockSpec((tq, LANES), lambda p, i: (i, p))
    return pl.pallas_call(
        body, name="sb_fwd", grid=(SB_HEADS // 2, seq // tq),
        in_specs=[pl.BlockSpec((tq, LANES), lambda p, i: (i, qb + p)), pl.BlockSpec((seq, LANES), lambda p, i: (0, kb + p)),
                  pl.BlockSpec((seq, LANES), lambda p, i: (0, vb + p))],
        out_specs=[pair_t, pair_t],
        out_shape=[jax.ShapeDtypeStruct((seq, 512), F32), jax.ShapeDtypeStruct((seq, 512), F32)],
        compiler_params=_cparams(("parallel", "parallel")),
    )(pbf, pbf, pbf)


def _sb_attn_bwd(pbf, tot, do, *, tq, tk):
    seq = pbf.shape[0]
    nd = tq // tk
    qb, kb, vb = BLK_QB * 4, BLK_KB * 4, BLK_VB * 4
    chains = _chains(tq)

    def body(q_ref, k_ref, v_ref, tot_ref, do_ref, dq_ref, dk_ref, dv_ref):
        i = pl.program_id(1)

        @pl.when(i == 0)
        def _():
            dk_ref[...] = jnp.zeros_like(dk_ref)
            dv_ref[...] = jnp.zeros_like(dv_ref)

        u_upto = _tri_matrix(tk, lambda r, c: r <= c)
        u_below = _tri_matrix(tk, lambda r, c: r < c)
        left = lax.broadcasted_iota(jnp.int32, (tq, LANES), 1) < HALF
        q_heads = _head_queries(q_ref[...], left)
        do_f = do_ref[...]
        do_heads = (_bf(jnp.where(left, do_f, 0.0)), _bf(jnp.where(left, 0.0, do_f)))
        tot_v = tot_ref[...]
        qs = [q_heads[h][_rows(g)] for h, g in chains]
        dos = [do_heads[h][_rows(g)] for h, g in chains]
        totals = [tot_v[_rows(g), h * HALF:h * HALF + 1] for h, g in chains]

        def block(j, carry, m):
            start = pl.multiple_of(j * tk, tk)
            k = k_ref[pl.ds(start, tk), :]
            v = v_ref[pl.ds(start, tk), :]
            pats = [True if m is None else _chain_pattern(g, m, tk, True) for _, g in chains]
            live = [n for n, p in enumerate(pats) if p is not None]
            zs = {n: _dot_nt(qs[n], k) for n in live}
            das = {n: _dot_nt(dos[n], v) for n in live}
            raws = {n: _softplus(zs[n]) for n in live}
            sps = {n: _masked(raws[n], pats[n]) for n in live}
            uptos = {n: _tri_sum(sps[n], u_upto) for n in live}
            lbs, a_s, gs = {}, {}, {}
            for n in live:
                lbs[n] = zs[n] - raws[n]
                a =_masked(jnp.exp(lbs[n] - (totals[n] - carry[n][0] - uptos[n])), pats[n])
                a_s[n] = _bf(a)
                gs[n] = das[n] * a
            belows = {n: _tri_sum(gs[n], u_below) for n in live}
            new = list(carry)
            dzs = {}
            for n in live:
                sp_before, g_before, dq_acc = carry[n]
                beta = jnp.exp(lbs[n])
                dz = _masked(gs[n] * (1.0 - beta) - (g_before + belows[n]) * beta, pats[n])
                dzs[n] = _bf(dz)
                new[n] = (sp_before + uptos[n][:, tk - 1:tk], g_before + belows[n][:, tk - 1:tk] + gs[n][:, tk - 1:tk],
                          dq_acc + _dot(dzs[n], k))
            dz_cat = jnp.concatenate([dzs[n] for n in live], axis=0)
            q_cat = jnp.concatenate([qs[n] for n in live], axis=0)
            dk_ref[pl.ds(start, tk), :] += _dot_tn(dz_cat, q_cat)
            a_cat = jnp.concatenate([a_s[n] for n in live], axis=0)
            do_cat = jnp.concatenate([dos[n] for n in live], axis=0)
            dv_ref[pl.ds(start, tk), :] += _dot_tn(a_cat, do_cat)
            return tuple(new)

        zero = jnp.zeros((ROW_GROUP, 1), F32)
        init = (zero, zero, jnp.zeros((ROW_GROUP, LANES), F32))
        carry = lax.fori_loop(0, i * nd, lambda j, cr: block(j, cr, None), (init,) * len(chains))
        for m in range(nd):
            carry = block(i * nd + m, carry, m)
        per_head = [jnp.concatenate([carry[n][2] for n, (ch, _) in enumerate(chains) if ch == h], axis=0) for h in range(2)]
        dq_ref[...] = jnp.where(left, per_head[0], per_head[1]) * SB_SCALE

    pair_t = pl.BlockSpec((tq, LANES), lambda p, i: (i, p))
    pair_s = pl.BlockSpec((seq, LANES), lambda p, i: (0, p))
    return pl.pallas_call(
        body, name="sb_bwd", grid=(SB_HEADS // 2, seq // tq),
        in_specs=[pl.BlockSpec((tq, LANES), lambda p, i: (i, qb + p)), pl.BlockSpec((seq, LANES), lambda p, i: (0, kb + p)),
                  pl.BlockSpec((seq, LANES), lambda p, i: (0, vb + p)), pair_t, pair_t],
        out_specs=[pair_t, pair_s, pair_s],
        out_shape=[jax.ShapeDtypeStruct((seq, 512), F32)] * 3,
        compiler_params=_cparams(("parallel", "arbitrary")),
    )(pbf, pbf, pbf, tot, do)


def _mem_probs(q, k):
    s = _dot_nt(q, k) * MEM_SCALE
    e = jnp.exp(s - jnp.max(s, axis=-1, keepdims=True))
    return e / jnp.sum(e, axis=-1, keepdims=True)


def _mem_fwd(pbf, mkv, *, t):
    seq = pbf.shape[0]

    def body(q_ref, kv_ref, o_ref):
        for h in range(MEM_HEADS):
            lo, hi = h * LANES, (h + 1) * LANES
            p = _mem_probs(q_ref[:, lo:hi], kv_ref[:, lo:hi])
            o_ref[:, lo:hi] = _dot(_bf(p), kv_ref[:, 512 + lo:512 + hi])

    return pl.pallas_call(
        body, name="mem_fwd", grid=(seq // t,),
        in_specs=[pl.BlockSpec((t, 512), lambda i: (i, BLK_QM)), pl.BlockSpec((MEM_LEN, 1024), lambda i: (0, 0))],
        out_specs=pl.BlockSpec((t, 512), lambda i: (i, 0)),
        out_shape=jax.ShapeDtypeStruct((seq, 512), F32),
        compiler_params=_cparams(("parallel",)),
    )(pbf, mkv)


def _mem_bwd(pbf, mkv, do, *, t):
    seq = pbf.shape[0]

    def body(q_ref, kv_ref, do_ref, dq_ref, dkv_ref):
        @pl.when(pl.program_id(0) == 0)
        def _():
            dkv_ref[...] = jnp.zeros_like(dkv_ref)

        for h in range(MEM_HEADS):
            lo, hi = h * LANES, (h + 1) * LANES
            q, k, v = q_ref[:, lo:hi], kv_ref[:, lo:hi], kv_ref[:, 512 + lo:512 + hi]
            do_h = _bf(do_ref[:, lo:hi])
            p = _mem_probs(q, k)
            dp = _dot_nt(do_h, v)
            ds = _bf(p * (dp - jnp.sum(dp * p, axis=-1, keepdims=True)) * MEM_SCALE)
            dq_ref[:, lo:hi] = _dot(ds, k)
            dkv_ref[:, lo:hi] += _dot_tn(ds, q)
            dkv_ref[:, 512 + lo:512 + hi] += _dot_tn(_bf(p), do_h)

    return pl.pallas_call(
        body, name="mem_bwd", grid=(seq // t,),
        in_specs=[pl.BlockSpec((t, 512), lambda i: (i, BLK_QM)), pl.BlockSpec((MEM_LEN, 1024), lambda i: (0, 0)),
                  pl.BlockSpec((t, 512), lambda i: (i, 0))],
        out_specs=[pl.BlockSpec((t, 512), lambda i: (i, 0)), pl.BlockSpec((MEM_LEN, 1024), lambda i: (0, 0))],
        out_shape=[jax.ShapeDtypeStruct((seq, 512), F32), jax.ShapeDtypeStruct((MEM_LEN, 1024), F32)],
        compiler_params=_cparams(("arbitrary",)),
    )(pbf, mkv, do)


def _mid(x, tgt, o_a, o_b, o_m, p32, wmg, bmg, wba, wbb, wbm, wout, ln_g, ln_b, *, t):
    seq = x.shape[0]
    inv_d = 1.0 / D_MODEL

    def body(x_ref, t_ref, oa_ref, ob_ref, om_ref, ga_ref, gb_ref, gm_ref, wmg_ref, bmg_ref, wba_ref, wbb_ref,
             wbm_ref, wout_ref, lg_ref, lb_ref,
             du_ref, mrg_ref, dgp_ref, ha_ref, hb_ref, hm_ref, dya_ref, dyb_ref, dym_ref, doa_ref, dob_ref, dom_ref,
             dga_ref, dgb_ref, dgm_ref, dgain_ref, dbias_ref, dbmg_ref, loss_ref):
        @pl.when(pl.program_id(0) == 0)
        def _():
            dgain_ref[...] = jnp.zeros_like(dgain_ref)
            dbias_ref[...] = jnp.zeros_like(dbias_ref)
            dbmg_ref[...] = jnp.zeros_like(dbmg_ref)
            loss_ref[...] = jnp.zeros_like(loss_ref)

        xv = x_ref[...]
        gate = _sigmoid(_dot(_bf(xv), wmg_ref[...]) + bmg_ref[...])

        branches = []
        merged = None
        for b, (o_ref, g_ref, w_ref, h_ref) in enumerate(((oa_ref, ga_ref, wba_ref, ha_ref), (ob_ref, gb_ref, wbb_ref, hb_ref),
                                                         (om_ref, gm_ref, wbm_ref, hm_ref))):
            o, gt = o_ref[...], g_ref[...]
            sg = _sigmoid(gt)
            silu = gt * sg
            h = _bf(o * silu)
            h_ref[...] = h
            y = _dot(h, w_ref[...])
            g_b = gate[:, b * D_MODEL:(b + 1) * D_MODEL]
            term = g_b * y
            merged = term if merged is None else merged + term
            branches.append((o, gt, sg, silu, y, g_b))
        mrg_b = _bf(merged)
        mrg_ref[...] = mrg_b

        u = DEEPNORM_ALPHA * xv + _dot(mrg_b, wout_ref[...])
        mu = jnp.mean(u, axis=-1, keepdims=True)
        uc = u - mu
        rstd = lax.rsqrt(jnp.mean(uc * uc, axis=-1, keepdims=True) + LN_EPS)
        xhat = uc * rstd
        lg = lg_ref[...]
        y_out = xhat * lg + lb_ref[...]
        err = y_out - t_ref[...]
        loss_ref[...] += 0.5 * jnp.sum(jnp.mean(err * err, axis=-1, keepdims=True), axis=0, keepdims=True)
        dy = err * inv_d
        dgain_ref[...] += jnp.sum(dy * xhat, axis=0, keepdims=True)
        dbias_ref[...] += jnp.sum(dy, axis=0, keepdims=True)
        dxh = dy * lg
        du = rstd * (dxh - jnp.mean(dxh, axis=-1, keepdims=True) - xhat * jnp.mean(dxh * xhat, axis=-1, keepdims=True))
        du_ref[...] = du

        dmerged = _dot_nt(_bf(du), wout_ref[...])
        outs = ((dya_ref, doa_ref, dga_ref, wba_ref), (dyb_ref, dob_ref, dgb_ref, wbb_ref), (dym_ref, dom_ref, dgm_ref, wbm_ref))
        dgp = []
        for (o, gt, sg, silu, y, g_b), (dy_ref, do_ref, dg_ref, w_ref) in zip(branches, outs):
            dyb = _bf(dmerged * g_b)
            dy_ref[...] = dyb
            dgp.append(dmerged * y * g_b * (1.0 - g_b))
            dh = _dot_nt(dyb, w_ref[...])
            do_ref[...] = dh * silu
            dg_ref[...] = _bf(dh * o * (sg * (1.0 + gt * (1.0 - sg))))
        dgp = jnp.concatenate(dgp, axis=1)
        dgp_ref[...] = _bf(dgp)
        dbmg_ref[...] += jnp.sum(dgp, axis=0, keepdims=True)

    row = lambda w: pl.BlockSpec((t, w), lambda i: (i, 0))
    pblk = lambda c: pl.BlockSpec((t, 512), lambda i: (i, c))
    full = lambda shp: pl.BlockSpec(shp, lambda i: (0, 0))
    sds = jax.ShapeDtypeStruct
    return pl.pallas_call(
        body, name="mid", grid=(seq // t,),
        in_specs=[row(1024), row(1024), row(512), row(512), row(512), pblk(BLK_GATE_A), pblk(BLK_GATE_B), pblk(BLK_GATE_M),
                  full(wmg.shape), full((1, N_MERGE)), full(wba.shape), full(wbb.shape), full(wbm.shape), full(wout.shape),
                  full((1, D_MODEL)), full((1, D_MODEL))],
        out_specs=[row(1024), row(1024), row(N_MERGE), row(512), row(512), row(512), row(1024), row(1024), row(1024),
                   row(512), row(512), row(512), row(512), row(512), row(512),
                   full((1, D_MODEL)), full((1, D_MODEL)), full((1, N_MERGE)), full((1, 1))],
        out_shape=[sds((seq, 1024), F32), sds((seq, 1024), BF16), sds((seq, N_MERGE), BF16),
                   sds((seq, 512), BF16), sds((seq, 512), BF16), sds((seq, 512), BF16),
                   sds((seq, 1024), BF16), sds((seq, 1024), BF16), sds((seq, 1024), BF16),
                   sds((seq, 512), F32), sds((seq, 512), F32), sds((seq, 512), F32),
                   sds((seq, 512), BF16), sds((seq, 512), BF16), sds((seq, 512), BF16),
                   sds((1, D_MODEL), F32), sds((1, D_MODEL), F32), sds((1, N_MERGE), F32), sds((1, 1), F32)],
        compiler_params=_cparams(("arbitrary",)),
    )(x, tgt, o_a, o_b, o_m, p32, p32, p32, wmg, bmg, wba, wbb, wbm, wout, ln_g, ln_b)


def _primed_weights(w):
    w_in = w["w_in"]
    zc = lambda n: jnp.zeros((D_MODEL, n), w_in.dtype)
    w_in_p = jnp.concatenate([w_in[:, 0:384], zc(64), w_in[:, 384:416], zc(32), w_in[:, 416:]], axis=1)
    wqb = jnp.pad(w["w_q_b"].reshape(MLA_Q_LORA, MLA_HEADS, 96), ((0, 0), (0, 0), (0, 32))).reshape(MLA_Q_LORA, 1024)
    kv3 = w["w_kv_b"].reshape(MLA_KV_LORA, MLA_HEADS, 128)
    wk = jnp.pad(kv3[:, :, :MLA_NOPE], ((0, 0), (0, 0), (0, 64))).reshape(MLA_KV_LORA, 1024)
    wv = kv3[:, :, MLA_NOPE:].reshape(MLA_KV_LORA, 512)
    return w_in_p, wqb, jnp.concatenate([wk, wv], axis=1)


def _local_step(x, mem, tgt, w, small, *, tq, tk, t_row, t_mm):
    seq = x.shape[0]
    w_in_p, wqb, wkvb = _primed_weights(w)
    tabs = _rope_tables(seq)
    wmg, wout = w["w_merge_gate"], w["w_out"]
    wba, wbb, wbm = w["w_branch_mla"], w["w_branch_sb"], w["w_branch_mem"]

    p32, pbf = _matmul(x, w_in_p, mode="nn", tm=t_mm, tn=1024, tk=D_MODEL, out_dtypes=(F32, BF16), name="proj_in")
    qp, kp, vp = _mla_prep(p32, small["q_a_gain"], small["kv_a_gain"], wqb, wkvb, tabs, t=t_row)
    o_a, lse = _mla_attn_fwd(qp, kp, vp, tq=tq, tk=tk)
    o_b, keep_total = _sb_attn_fwd(pbf, tq=tq, tk=tk)
    (mkv,) = _matmul(mem, w["w_mem_kv"], mode="nn", tm=MEM_LEN, tn=512, tk=D_MODEL, out_dtypes=(BF16,), name="mem_kv")
    o_m = _mem_fwd(pbf, mkv, t=t_row)

    (du, merged, dgpre, h_a, h_b, h_m, dy_a, dy_b, dy_m, do_a, do_b, do_m, dgate_a, dgate_b, dgate_m,
     d_ln_g, d_ln_b, d_bmg, loss) = _mid(x, tgt, o_a, o_b, o_m, p32, wmg, small["b_merge_gate"], wba, wbb, wbm, wout,
                                         small["ln_gain"], small["ln_bias"], t=t_row)

    dqp, dkp, dvp = _mla_attn_bwd(qp, kp, vp, o_a, lse, do_a, tq=tq, tk=tk)
    dlat, d_wqb, d_wkvb, d_gq, d_gkv = _mla_post(p32, dqp, dkp, dvp, small["q_a_gain"], small["kv_a_gain"], wqb, wkvb, tabs,
                                                 t=t_row)
    dq_b, dk_b, dv_b = _sb_attn_bwd(pbf, keep_total, do_b, tq=tq, tk=tk)
    dq_m, dmkv = _mem_bwd(pbf, mkv, do_m, t=t_row)

    dcat = jnp.concatenate([dgpre, dlat, dgate_a, _bf(dq_b), _bf(dk_b), _bf(dv_b), dgate_b, _bf(dq_m), dgate_m], axis=1)
    wcat = jnp.concatenate([wmg, w_in_p], axis=1)
    (grad_x,) = _matmul(dcat, wcat, mode="nt", tm=t_mm, tn=D_MODEL, tk=1024, out_dtypes=(F32,), name="grad_x",
                        add=du, add_scale=DEEPNORM_ALPHA)
    (d_wcat,) = _matmul(x, dcat, mode="tn", tm=512, tn=1024, tk=t_mm, out_dtypes=(F32,), name="grad_w_cat")
    (d_wout,) = _matmul(merged, du, mode="tn", tm=512, tn=1024, tk=t_mm, out_dtypes=(F32,), name="grad_w_out")
    (d_wba,) = _matmul(h_a, dy_a, mode="tn", tm=512, tn=1024, tk=t_mm, out_dtypes=(F32,), name="grad_w_branch_a")
    (d_wbb,) = _matmul(h_b, dy_b, mode="tn", tm=512, tn=1024, tk=t_mm, out_dtypes=(F32,), name="grad_w_branch_b")
    (d_wbm,) = _matmul(h_m, dy_m, mode="tn", tm=512, tn=1024, tk=t_mm, out_dtypes=(F32,), name="grad_w_branch_m")
    (d_wmemkv,) = _matmul(mem, dmkv, mode="tn", tm=512, tn=1024, tk=MEM_LEN, out_dtypes=(F32,), name="grad_w_mem_kv")

    d_wmg = d_wcat[:, :N_MERGE]
    d_winp = d_wcat[:, N_MERGE:]
    d_win = jnp.concatenate([d_winp[:, 0:384], d_winp[:, 448:480], d_winp[:, 512:]], axis=1)
    d_wq = d_wqb.reshape(MLA_Q_LORA, MLA_HEADS, 128)[:, :, :96].reshape(MLA_Q_LORA, 768)
    d_wk = d_wkvb[:, :1024].reshape(MLA_KV_LORA, MLA_HEADS, 128)[:, :, :MLA_NOPE]
    d_wv = d_wkvb[:, 1024:].reshape(MLA_KV_LORA, MLA_HEADS, MLA_V)
    d_wkv = jnp.concatenate([d_wk, d_wv], axis=2).reshape(MLA_KV_LORA, 1024)
    grads = {"w_in": d_win, "w_mem_kv": d_wmemkv, "w_q_b": d_wq, "w_kv_b": d_wkv, "w_branch_mla": d_wba,
             "w_branch_sb": d_wbb, "w_branch_mem": d_wbm, "w_merge_gate": d_wmg, "w_out": d_wout}
    small_grads = {"q_a_gain": d_gq, "kv_a_gain": d_gkv, "b_merge_gate": d_bmg, "ln_gain": d_ln_g, "ln_bias": d_ln_b}
    return loss[0, 0], grad_x, grads, small_grads


def _pack_shards(shards, small):
    flat_small = jnp.concatenate([small[n].reshape(-1) for n, _ in SMALL_SIZES])
    flat_small = jnp.pad(flat_small, (0, 8 * PACK_COLS - SMALL_TOTAL)).reshape(8, PACK_COLS)
    parts = [shards[n].reshape(-1, PACK_COLS) for n, _ in PACK_ROWS[:-1]] + [flat_small]
    return jnp.concatenate(parts, axis=0)


def _unpack_shards(pack, shapes):
    out, r0 = {}, 0
    for n, rows in PACK_ROWS[:-1]:
        out[n] = pack[r0:r0 + rows].reshape(shapes[n])
        r0 += rows
    flat = pack[r0:r0 + 8].reshape(-1)
    small, c0 = {}, 0
    for n, size in SMALL_SIZES:
        small[n] = flat[c0:c0 + size].reshape(1, size)
        c0 += size
    return out, small


def _split_by_chip(name, full):
    r, c = full.shape
    if name in COL_SHARDED:
        s = full.reshape(r, N_CHIPS, c // N_CHIPS).transpose(1, 0, 2)
    else:
        s = full.reshape(N_CHIPS, r // N_CHIPS, c)
    return s.reshape(N_CHIPS, -1, PACK_COLS)


def _join_chips(name, packed4):
    r, c = FULL_SHAPES[name]
    if name in COL_SHARDED:
        return packed4.reshape(N_CHIPS, r, c // N_CHIPS).transpose(1, 0, 2).reshape(r, c)
    return packed4.reshape(r, c)


def _place():
    x, y, c = lax.axis_index("x"), lax.axis_index("y"), lax.axis_index("c")
    return x, y, c


def _other_chips(x, y):
    return ((1 - x, y), (x, 1 - y), (1 - x, 1 - y))


def _gather_weights(wpack):
    rows = wpack.shape[0]
    chunk = rows // 4

    def body(w_ref, out_ref, wb_ref, send_sems, recv_sems, local_sem):
        x, y, c = _place()
        me = 2 * x + y
        for r in range(4):
            wb_ref[r * chunk:(r + 1) * chunk, :] = _bf(w_ref[r * chunk:(r + 1) * chunk, :])
        mine = pltpu.make_async_copy(wb_ref, out_ref.at[me], local_sem)
        mine.start()
        copies = []
        for k, (px, py) in enumerate(_other_chips(x, y)):
            cp = pltpu.make_async_remote_copy(src_ref=wb_ref, dst_ref=out_ref.at[me], send_sem=send_sems.at[k],
                                              recv_sem=recv_sems.at[k], device_id=(px, py, c), device_id_type=MESH_ID)
            cp.start()
            copies.append(cp)
        for k, (px, py) in enumerate(_other_chips(x, y)):
            pltpu.make_async_remote_copy(src_ref=wb_ref, dst_ref=out_ref.at[2 * px + py], send_sem=send_sems.at[k],
                                         recv_sem=recv_sems.at[k], device_id=(px, py, c), device_id_type=MESH_ID).wait_recv()
        for cp in copies:
            cp.wait_send()
        mine.wait()

    return pl.pallas_call(
        body, name="gather_weights",
        in_specs=[pl.BlockSpec(memory_space=pltpu.VMEM)],
        out_specs=pl.BlockSpec(memory_space=pltpu.HBM),
        out_shape=jax.ShapeDtypeStruct((N_CHIPS, rows, PACK_COLS), BF16),
        scratch_shapes=[pltpu.VMEM((rows, PACK_COLS), BF16), pltpu.SemaphoreType.DMA((3,)), pltpu.SemaphoreType.DMA((3,)),
                        pltpu.SemaphoreType.DMA],
        compiler_params=pltpu.CompilerParams(vmem_limit_bytes=VMEM_LIMIT),
    )(wpack)


def _to_sibling_half(gpack):
    def body(g_ref, out_ref, send_sems, recv_sems):
        x, y, c = _place()
        theirs = pl.ds(pl.multiple_of((1 - c) * PACK_HALF, 8), PACK_HALF)
        copies = [pltpu.make_async_remote_copy(src_ref=g_ref.at[j, theirs, :], dst_ref=out_ref.at[j],
                                               send_sem=send_sems.at[j], recv_sem=recv_sems.at[j],
                                               device_id=(x, y, 1 - c), device_id_type=MESH_ID) for j in range(N_CHIPS)]
        for cp in copies:
            cp.start()
        for cp in copies:
            cp.wait()

    return pl.pallas_call(
        body, name="rs_sibling",
        in_specs=[pl.BlockSpec(memory_space=pltpu.HBM)],
        out_specs=pl.BlockSpec(memory_space=pltpu.HBM),
        out_shape=jax.ShapeDtypeStruct((N_CHIPS, PACK_HALF, PACK_COLS), F32),
        scratch_shapes=[pltpu.SemaphoreType.DMA((N_CHIPS,)), pltpu.SemaphoreType.DMA((N_CHIPS,))],
    )(gpack)


def _add_sibling(gpack, got):
    tr = PACK_HALF // 4

    def body(c_ref, g_ref, r_ref, o_ref):
        o_ref[...] = g_ref[...] + r_ref[...]

    grid_spec = pltpu.PrefetchScalarGridSpec(
        num_scalar_prefetch=1, grid=(N_CHIPS, 4),
        in_specs=[pl.BlockSpec((1, tr, PACK_COLS), lambda j, i, c_ref: (j, c_ref[0] * 4 + i, 0)),
                  pl.BlockSpec((1, tr, PACK_COLS), lambda j, i, c_ref: (j, i, 0))],
        out_specs=pl.BlockSpec((1, tr, PACK_COLS), lambda j, i, c_ref: (j, i, 0)))
    return pl.pallas_call(
        body, name="rs_add_sibling", grid_spec=grid_spec,
        out_shape=jax.ShapeDtypeStruct((N_CHIPS, PACK_HALF, PACK_COLS), F32),
        compiler_params=_cparams(("parallel", "parallel")),
    )(lax.axis_index("c").astype(jnp.int32).reshape(1), gpack, got)


def _to_owner_chips(chipsum):
    def body(s_ref, out_ref, send_sems, recv_sems, local_sem):
        x, y, c = _place()
        me = 2 * x + y
        mine = pltpu.make_async_copy(s_ref.at[me], out_ref.at[me], local_sem)
        mine.start()
        copies = []
        for k, (px, py) in enumerate(_other_chips(x, y)):
            cp = pltpu.make_async_remote_copy(src_ref=s_ref.at[2 * px + py], dst_ref=out_ref.at[me], send_sem=send_sems.at[k],
                                              recv_sem=recv_sems.at[k], device_id=(px, py, c), device_id_type=MESH_ID)
            cp.start()
            copies.append(cp)
        for k, (px, py) in enumerate(_other_chips(x, y)):
            pltpu.make_async_remote_copy(src_ref=s_ref.at[me], dst_ref=out_ref.at[2 * px + py], send_sem=send_sems.at[k],
                                         recv_sem=recv_sems.at[k], device_id=(px, py, c), device_id_type=MESH_ID).wait_recv()
        for cp in copies:
            cp.wait_send()
        mine.wait()

    return pl.pallas_call(
        body, name="rs_chips",
        in_specs=[pl.BlockSpec(memory_space=pltpu.HBM)],
        out_specs=pl.BlockSpec(memory_space=pltpu.HBM),
        out_shape=jax.ShapeDtypeStruct(chipsum.shape, F32),
        scratch_shapes=[pltpu.SemaphoreType.DMA((3,)), pltpu.SemaphoreType.DMA((3,)), pltpu.SemaphoreType.DMA],
    )(chipsum)


def _add_chips(parts):
    tr = PACK_HALF // 4

    def body(p_ref, o_ref):
        o_ref[...] = ((p_ref[0] + p_ref[1]) + p_ref[2]) + p_ref[3]

    return pl.pallas_call(
        body, name="rs_add_chips", grid=(4,),
        in_specs=[pl.BlockSpec((N_CHIPS, tr, PACK_COLS), lambda i: (0, i, 0))],
        out_specs=pl.BlockSpec((tr, PACK_COLS), lambda i: (i, 0)),
        out_shape=jax.ShapeDtypeStruct((PACK_HALF, PACK_COLS), F32),
        compiler_params=_cparams(("parallel",)),
    )(parts)


def _swap_halves(half):
    def body(h_ref, out_ref, send_sem, recv_sem, local_sem):
        x, y, c = _place()
        my_rows = pl.ds(pl.multiple_of(c * PACK_HALF, 8), PACK_HALF)
        mine = pltpu.make_async_copy(h_ref, out_ref.at[my_rows, :], local_sem)
        mine.start()
        cp = pltpu.make_async_remote_copy(src_ref=h_ref, dst_ref=out_ref.at[my_rows, :], send_sem=send_sem, recv_sem=recv_sem,
                                          device_id=(x, y, 1 - c), device_id_type=MESH_ID)
        cp.start()
        their_rows = pl.ds(pl.multiple_of((1 - c) * PACK_HALF, 8), PACK_HALF)
        pltpu.make_async_remote_copy(src_ref=h_ref, dst_ref=out_ref.at[their_rows, :], send_sem=send_sem, recv_sem=recv_sem,
                                     device_id=(x, y, 1 - c), device_id_type=MESH_ID).wait_recv()
        cp.wait_send()
        mine.wait()

    return pl.pallas_call(
        body, name="rs_swap_halves",
        in_specs=[pl.BlockSpec(memory_space=pltpu.HBM)],
        out_specs=pl.BlockSpec(memory_space=pltpu.HBM),
        out_shape=jax.ShapeDtypeStruct((PACK_TOTAL, PACK_COLS), F32),
        scratch_shapes=[pltpu.SemaphoreType.DMA, pltpu.SemaphoreType.DMA, pltpu.SemaphoreType.DMA],
    )(half)


def _adamw(w, g, m, v):
    tr = PACK_TOTAL // 8

    def body(w_ref, g_ref, m_ref, v_ref, d_ref, nm_ref, nv_ref):
        gv = g_ref[...]
        m_new = ADAM_B1 * m_ref[...] + (1.0 - ADAM_B1) * gv
        v_new = ADAM_B2 * v_ref[...] + (1.0 - ADAM_B2) * (gv * gv)
        m_hat = m_new / (1.0 - ADAM_B1 ** ADAM_STEP)
        v_hat = v_new / (1.0 - ADAM_B2 ** ADAM_STEP)
        d_ref[...] = -ADAM_LR * (m_hat / (jnp.sqrt(v_hat) + ADAM_EPS) + ADAM_WD * w_ref[...])
        nm_ref[...] = m_new
        nv_ref[...] = v_new

    blk = pl.BlockSpec((tr, PACK_COLS), lambda i: (i, 0))
    return pl.pallas_call(
        body, name="adamw", grid=(8,),
        in_specs=[blk] * 4, out_specs=[blk] * 3,
        out_shape=[jax.ShapeDtypeStruct((PACK_TOTAL, PACK_COLS), F32)] * 3,
        compiler_params=_cparams(("parallel",)),
    )(w, g, m, v)


SMALL_ROWS = 64
ADAM_STEPS_PER_HALF = 4


def _pack_small(d):
    flat = jnp.concatenate([d[n].reshape(-1) for n, _ in SMALL_SIZES])
    return jnp.pad(flat, (0, SMALL_ROWS * PACK_COLS - SMALL_TOTAL)).reshape(SMALL_ROWS, PACK_COLS)


def _unpack_small(a):
    flat, out, c0 = a.reshape(-1), {}, 0
    for n, size in SMALL_SIZES:
        out[n] = flat[c0:c0 + size].reshape(1, size)
        c0 += size
    return out


def _split_by_chip(name, full):
    r, c = full.shape
    if name in COL_SHARDED:
        return full.reshape(r, N_CHIPS, c // N_CHIPS).transpose(1, 0, 2)
    return full.reshape(N_CHIPS, r // N_CHIPS, c)


def _join_chips(name, slots):
    _, r, cs = slots.shape
    if name in COL_SHARDED:
        return slots.transpose(1, 0, 2).reshape(r, N_CHIPS * cs)
    return slots.reshape(N_CHIPS * r, cs)


HBM_SPEC = pl.BlockSpec(memory_space=pltpu.HBM)


def _gather_shards(shards):
    n = len(shards)

    def body(*refs):
        w_refs, out_refs, wb_refs = refs[:n], refs[n:2 * n], refs[2 * n:3 * n]
        send_sems, recv_sems, pass_send_sems, pass_recv_sems, local_sems = refs[3 * n:]
        x, y, c = _place()
        me = 2 * x + y
        sibling = (x, y, 1 - c)

        def halves(ref):
            half = ref.shape[-2] // 2
            return (pl.ds(pl.multiple_of(c * half, 16), half), pl.ds(pl.multiple_of((1 - c) * half, 16), half))
        for w_ref, wb_ref in zip(w_refs, wb_refs):
            rows = w_ref.shape[0]
            chunk = min(rows, 128)

            def cast(i, carry, w_ref=w_ref, wb_ref=wb_ref, chunk=chunk):
                r0 = pl.multiple_of(i * chunk, chunk)
                wb_ref[pl.ds(r0, chunk), :] = _bf(w_ref[pl.ds(r0, chunk), :])
                return carry

            lax.fori_loop(0, rows // chunk, cast, 0)
        sends, locals_ = [], []
        for a, (wb_ref, out_ref) in enumerate(zip(wb_refs, out_refs)):
            mine = pltpu.make_async_copy(wb_ref, out_ref.at[me], local_sems.at[a])
            mine.start()
            locals_.append(mine)
            mine_rows, _ = halves(wb_ref)
            for k, (px, py) in enumerate(_other_chips(x, y)):
                cp = pltpu.make_async_remote_copy(src_ref=wb_ref.at[mine_rows, :], dst_ref=out_ref.at[me, mine_rows, :],
                                                  send_sem=send_sems.at[k, a], recv_sem=recv_sems.at[k, a],
                                                  device_id=(px, py, c), device_id_type=MESH_ID)
                cp.start()
                sends.append(cp)
        for a, (wb_ref, out_ref) in enumerate(zip(wb_refs, out_refs)):
            mine_rows, _ = halves(wb_ref)
            for k, (px, py) in enumerate(_other_chips(x, y)):
                landed = out_ref.at[2 * px + py, mine_rows, :]
                pltpu.make_async_remote_copy(src_ref=wb_ref.at[mine_rows, :], dst_ref=landed, send_sem=send_sems.at[k, a],
                                             recv_sem=recv_sems.at[k, a], device_id=(px, py, c),
                                             device_id_type=MESH_ID).wait_recv()
                cp = pltpu.make_async_remote_copy(src_ref=landed, dst_ref=landed, send_sem=pass_send_sems.at[k, a],
                                                  recv_sem=pass_recv_sems.at[k, a], device_id=sibling, device_id_type=MESH_ID)
                cp.start()
                sends.append(cp)
        for a, (wb_ref, out_ref) in enumerate(zip(wb_refs, out_refs)):
            _, their_rows = halves(wb_ref)
            for k, (px, py) in enumerate(_other_chips(x, y)):
                passed = out_ref.at[2 * px + py, their_rows, :]
                pltpu.make_async_remote_copy(src_ref=passed, dst_ref=passed, send_sem=pass_send_sems.at[k, a],
                                             recv_sem=pass_recv_sems.at[k, a], device_id=sibling,
                                             device_id_type=MESH_ID).wait_recv()
        for cp in sends:
            cp.wait_send()
        for cp in locals_:
            cp.wait()

    return pl.pallas_call(
        body, name="gather_weights",
        in_specs=[pl.BlockSpec(memory_space=pltpu.VMEM)] * n,
        out_specs=[HBM_SPEC] * n,
        out_shape=[jax.ShapeDtypeStruct((N_CHIPS,) + s.shape, BF16) for s in shards],
        scratch_shapes=[pltpu.VMEM(s.shape, BF16) for s in shards]
        + [pltpu.SemaphoreType.DMA((3, n))] * 4 + [pltpu.SemaphoreType.DMA((n,))],
        compiler_params=pltpu.CompilerParams(vmem_limit_bytes=VMEM_LIMIT),
    )(*shards)


def _rs_to_sibling(g4):
    n = len(g4)

    def body(*refs):
        g_refs, out_refs = refs[:n], refs[n:2 * n]
        send_sems, recv_sems = refs[2 * n:]
        x, y, c = _place()
        copies = []
        for a, (g_ref, out_ref) in enumerate(zip(g_refs, out_refs)):
            half = g_ref.shape[1] // 2
            theirs = pl.ds(pl.multiple_of((1 - c) * half, 8), half)
            copies.append(pltpu.make_async_remote_copy(src_ref=g_ref.at[:, theirs, :], dst_ref=out_ref, send_sem=send_sems.at[a],
                                                       recv_sem=recv_sems.at[a], device_id=(x, y, 1 - c),
                                                       device_id_type=MESH_ID))
        for cp in copies:
            cp.start()
        for cp in copies:
            cp.wait()

    return pl.pallas_call(
        body, name="rs_sibling", in_specs=[HBM_SPEC] * n, out_specs=[HBM_SPEC] * n,
        out_shape=[jax.ShapeDtypeStruct((N_CHIPS, g.shape[1] // 2, g.shape[2]), F32) for g in g4],
        scratch_shapes=[pltpu.SemaphoreType.DMA((n,)), pltpu.SemaphoreType.DMA((n,))],
    )(*g4)


def _rs_add_sibling(g4, got, wire_dtypes):
    n = len(g4)
    narrow = [a for a in range(n) if wire_dtypes[a] != F32]

    def body(c_ref, *refs):
        outs = refs[2 * n:3 * n]
        wires = dict(zip(narrow, refs[3 * n:]))
        for a, (g_ref, r_ref, o_ref) in enumerate(zip(refs[:n], refs[n:2 * n], outs)):
            s = g_ref[...] + r_ref[...]
            o_ref[...] = s
            if a in wires:
                wires[a][...] = s.astype(wires[a].dtype)

    blk = lambda r: (1, r.shape[1], r.shape[2])
    plain = lambda r: pl.BlockSpec(blk(r), lambda j, c_ref: (j, 0, 0))
    grid_spec = pltpu.PrefetchScalarGridSpec(
        num_scalar_prefetch=1, grid=(N_CHIPS,),
        in_specs=[pl.BlockSpec(blk(r), lambda j, c_ref: (j, c_ref[0], 0)) for r in got] + [plain(r) for r in got],
        out_specs=[plain(r) for r in got] + [plain(got[a]) for a in narrow])
    res = pl.pallas_call(
        body, name="rs_add_sibling", grid_spec=grid_spec,
        out_shape=[jax.ShapeDtypeStruct(r.shape, F32) for r in got]
        + [jax.ShapeDtypeStruct(got[a].shape, wire_dtypes[a]) for a in narrow],
        compiler_params=_cparams(("parallel",)),
    )(lax.axis_index("c").astype(jnp.int32).reshape(1), *g4, *got)
    chipsum = list(res[:n])
    wire = list(chipsum)
    for a, w in zip(narrow, res[n:]):
        wire[a] = w
    return chipsum, wire


RELATION_XOR = (2, 1, 3)


def _rs_to_chips(wire):
    n = len(wire)

    def body(*refs):
        s_refs, out_refs = refs[:n], refs[n:2 * n]
        send_sems, recv_sems = refs[2 * n:]
        x, y, c = _place()
        sends = []
        for a, (s_ref, out_ref) in enumerate(zip(s_refs, out_refs)):
            for k, (px, py) in enumerate(_other_chips(x, y)):
                cp = pltpu.make_async_remote_copy(src_ref=s_ref.at[2 * px + py], dst_ref=out_ref.at[RELATION_XOR[k] - 1],
                                                  send_sem=send_sems.at[k, a], recv_sem=recv_sems.at[k, a],
                                                  device_id=(px, py, c), device_id_type=MESH_ID)
                cp.start()
                sends.append(cp)
        for cp in sends:
            cp.wait_recv()
        for cp in sends:
            cp.wait_send()

    return pl.pallas_call(
        body, name="rs_chips", in_specs=[HBM_SPEC] * n, out_specs=[HBM_SPEC] * n,
        out_shape=[jax.ShapeDtypeStruct((3,) + s.shape[1:], s.dtype) for s in wire],
        scratch_shapes=[pltpu.SemaphoreType.DMA((3, n)), pltpu.SemaphoreType.DMA((3, n))],
    )(*wire)


def _rs_add_chips(chipsum, parts):
    n = len(parts)

    def body(me_ref, *refs):
        me = me_ref[0]
        for s_ref, p_ref, o_ref in zip(refs[:n], refs[n:2 * n], refs[2 * n:]):
            own = s_ref[0]
            total = None
            for k in range(N_CHIPS):
                theirs = p_ref[jnp.maximum(jnp.bitwise_xor(me, k) - 1, 0)].astype(F32)
                term = jnp.where(me == k, own, theirs)
                total = term if total is None else total + term
            o_ref[...] = total

    grid_spec = pltpu.PrefetchScalarGridSpec(
        num_scalar_prefetch=1, grid=(2,),
        in_specs=[pl.BlockSpec((1, p.shape[1] // 2, p.shape[2]), lambda i, me_ref: (me_ref[0], i, 0)) for p in parts]
        + [pl.BlockSpec((3, p.shape[1] // 2, p.shape[2]), lambda i, me_ref: (0, i, 0)) for p in parts],
        out_specs=[pl.BlockSpec((p.shape[1] // 2, p.shape[2]), lambda i, me_ref: (i, 0)) for p in parts])
    me = (2 * lax.axis_index("x") + lax.axis_index("y")).astype(jnp.int32).reshape(1)
    return pl.pallas_call(
        body, name="rs_add_chips", grid_spec=grid_spec,
        out_shape=[jax.ShapeDtypeStruct(p.shape[1:], F32) for p in parts],
        compiler_params=_cparams(("parallel",)),
    )(me, *chipsum, *parts)


def _rs_swap_halves(halves):
    n = len(halves)

    def body(*refs):
        h_refs, out_refs = refs[:n], refs[n:2 * n]
        send_sems, recv_sems = refs[2 * n:]
        x, y, c = _place()
        copies = [pltpu.make_async_remote_copy(src_ref=h_ref, dst_ref=out_ref, send_sem=send_sems.at[a], recv_sem=recv_sems.at[a],
                                               device_id=(x, y, 1 - c), device_id_type=MESH_ID)
                  for a, (h_ref, out_ref) in enumerate(zip(h_refs, out_refs))]
        for cp in copies:
            cp.start()
        for cp in copies:
            cp.wait()

    return pl.pallas_call(
        body, name="rs_swap_halves", in_specs=[HBM_SPEC] * n, out_specs=[HBM_SPEC] * n,
        out_shape=[jax.ShapeDtypeStruct(h.shape, F32) for h in halves],
        scratch_shapes=[pltpu.SemaphoreType.DMA((n,)), pltpu.SemaphoreType.DMA((n,))],
    )(*halves)


def _adamw_list(ws, g_mine, g_theirs, ms, vs):
    n = len(ws)

    def body(c_ref, *refs):
        w_refs, gm_refs, gt_refs, m_refs, v_refs = (refs[k * n:(k + 1) * n] for k in range(5))
        g_refs, d_refs, nm_refs, nv_refs = (refs[k * n:(k + 1) * n] for k in range(5, 9))
        mine = (pl.program_id(0) // ADAM_STEPS_PER_HALF) == c_ref[0]
        for a in range(n):
            gv = jnp.where(mine, gm_refs[a][...], gt_refs[a][...])
            g_refs[a][...] = gv
            m_new = ADAM_B1 * m_refs[a][...] + (1.0 - ADAM_B1) * gv
            v_new = ADAM_B2 * v_refs[a][...] + (1.0 - ADAM_B2) * (gv * gv)
            m_hat = m_new / (1.0 - ADAM_B1 ** ADAM_STEP)
            v_hat = v_new / (1.0 - ADAM_B2 ** ADAM_STEP)
            d_refs[a][...] = -ADAM_LR * (m_hat / (jnp.sqrt(v_hat) + ADAM_EPS) + ADAM_WD * w_refs[a][...])
            nm_refs[a][...] = m_new
            nv_refs[a][...] = v_new

    steps = 2 * ADAM_STEPS_PER_HALF
    whole = [pl.BlockSpec((w.shape[0] // steps, w.shape[1]), lambda i, c_ref: (i, 0)) for w in ws]
    half = [pl.BlockSpec((w.shape[0] // steps, w.shape[1]), lambda i, c_ref: (i % ADAM_STEPS_PER_HALF, 0)) for w in ws]
    shapes = [jax.ShapeDtypeStruct(w.shape, F32) for w in ws]
    grid_spec = pltpu.PrefetchScalarGridSpec(num_scalar_prefetch=1, grid=(steps,),
                                             in_specs=whole + half + half + whole + whole, out_specs=whole * 4)
    res = pl.pallas_call(
        body, name="adamw", grid_spec=grid_spec, out_shape=shapes * 4,
        compiler_params=_cparams(("parallel",)),
    )(lax.axis_index("c").astype(jnp.int32).reshape(1), *ws, *g_mine, *g_theirs, *ms, *vs)
    return res[:n], res[n:2 * n], res[2 * n:3 * n], res[3 * n:]


WEIGHT_NAMES = ("w_in", "w_mem_kv", "q_a_gain", "w_q_b", "kv_a_gain", "w_kv_b", "w_branch_mla", "w_branch_sb",
                "w_branch_mem", "w_merge_gate", "b_merge_gate", "w_out", "ln_gain", "ln_bias")
BIG_NAMES = tuple(n for n, _ in PACK_ROWS[:-1])
SMALL_NAMES = tuple(n for n, _ in SMALL_SIZES)


def kernel(x, mem, w_in, w_mem_kv, q_a_gain, w_q_b, kv_a_gain, w_kv_b, w_branch_mla, w_branch_sb, w_branch_mem, w_merge_gate, b_merge_gate, w_out, ln_gain, ln_bias, loss_target, m_w_in, m_w_mem_kv, m_q_a_gain, m_w_q_b, m_kv_a_gain, m_w_kv_b, m_w_branch_mla, m_w_branch_sb, m_w_branch_mem, m_w_merge_gate, m_b_merge_gate, m_w_out, m_ln_gain, m_ln_bias, v_w_in, v_w_mem_kv, v_q_a_gain, v_w_q_b, v_kv_a_gain, v_w_kv_b, v_w_branch_mla, v_w_branch_sb, v_w_branch_mem, v_w_merge_gate, v_b_merge_gate, v_w_out, v_ln_gain, v_ln_bias):
    weights = dict(zip(WEIGHT_NAMES, (w_in, w_mem_kv, q_a_gain, w_q_b, kv_a_gain, w_kv_b, w_branch_mla, w_branch_sb,
                                      w_branch_mem, w_merge_gate, b_merge_gate, w_out, ln_gain, ln_bias)))
    mom1 = dict(zip(WEIGHT_NAMES, (m_w_in, m_w_mem_kv, m_q_a_gain, m_w_q_b, m_kv_a_gain, m_w_kv_b, m_w_branch_mla,
                                   m_w_branch_sb, m_w_branch_mem, m_w_merge_gate, m_b_merge_gate, m_w_out, m_ln_gain,
                                   m_ln_bias)))
    mom2 = dict(zip(WEIGHT_NAMES, (v_w_in, v_w_mem_kv, v_q_a_gain, v_w_q_b, v_kv_a_gain, v_w_kv_b, v_w_branch_mla,
                                   v_w_branch_sb, v_w_branch_mem, v_w_merge_gate, v_b_merge_gate, v_w_out, v_ln_gain,
                                   v_ln_bias)))
    def as_list(d):
        return [d[n][0] for n in BIG_NAMES] + [_pack_small({n: d[n] for n in SMALL_NAMES})]

    w_list, m_list, v_list = as_list(weights), as_list(mom1), as_list(mom2)

    gathered = _gather_shards(w_list[:-1])
    full_w = {n: _join_chips(n, g) for n, g in zip(BIG_NAMES, gathered)}
    small = {n: weights[n] for n in SMALL_NAMES}

    loss, grad_x, grads, small_grads = _local_step(x[0], mem[0], loss_target[0], full_w, small,
                                                   tq=512, tk=256, t_row=256, t_mm=512)

    g4 = [_split_by_chip(n, grads[n]) for n in BIG_NAMES]
    g4.append(jnp.broadcast_to(_pack_small(small_grads)[None], (N_CHIPS, SMALL_ROWS, PACK_COLS)))
    got = _rs_to_sibling(g4)
    chipsum, wire = _rs_add_sibling(g4, got, [BF16] * len(BIG_NAMES) + [F32])
    parts = _rs_to_chips(wire)
    mine = _rs_add_chips(chipsum, parts)
    theirs = _rs_swap_halves(mine)
    g_list, d_list, nm_list, nv_list = _adamw_list(w_list, mine, theirs, m_list, v_list)

    loss = lax.psum(loss, ("x", "y", "c"))
    outs = [loss, grad_x[None]]
    for arrays in (g_list, d_list, nm_list, nv_list):
        big = dict(zip(BIG_NAMES, arrays[:-1]))
        sm = _unpack_small(arrays[-1])
        outs.extend(big[n][None] if n in big else sm[n] for n in WEIGHT_NAMES)
    return tuple(outs)
```

```python
import functools
import math

import numpy as np
import jax
import jax.numpy as jnp
from jax import lax
from jax.experimental import pallas as pl
from jax.experimental.pallas import tpu as pltpu

F32 = jnp.float32
BF16 = jnp.bfloat16
MESH_ID = pl.DeviceIdType.MESH

D_MODEL = 1024
MEM_LEN = 256
MLA_HEADS = 8
MLA_NOPE = 64
MLA_ROPE = 32
MLA_V = 64
MLA_Q_LORA = 256
MLA_KV_LORA = 128
SB_HEADS = 8
SB_HEAD_DIM = 64
MEM_HEADS = 4
MEM_HEAD_DIM = 128
BRANCH_WIDTH = 512
ROPE_BASE = 10000.0
RMS_EPS = 1e-6
LN_EPS = 1e-5
DEEPNORM_ALPHA = 2.0 ** 0.25
MLA_SCALE = 1.0 / math.sqrt(MLA_NOPE + MLA_ROPE)
SB_SCALE = 1.0 / math.sqrt(SB_HEAD_DIM)
MEM_SCALE = 1.0 / math.sqrt(MEM_HEAD_DIM)

ADAM_LR = 0.001
ADAM_B1 = 0.9
ADAM_B2 = 0.999
ADAM_EPS = 1e-08
ADAM_WD = 0.01
ADAM_STEP = 10

LANES = 128
HALF = 64
N_CHIPS = 4
PACK_COLS = 1024
VMEM_LIMIT = 56 * 1024 * 1024

IN_WIDTH_P = 4096
BLK_LAT, BLK_GATE_A, BLK_QB, BLK_KB, BLK_VB, BLK_GATE_B, BLK_QM, BLK_GATE_M = range(8)
N_MERGE = 3 * D_MODEL
CAT_WIDTH = N_MERGE + IN_WIDTH_P

PACK_ROWS = (("w_in", 1000), ("w_mem_kv", 256), ("w_q_b", 48), ("w_kv_b", 32), ("w_branch_mla", 128),
             ("w_branch_sb", 128), ("w_branch_mem", 128), ("w_merge_gate", 768), ("w_out", 256), ("small", 8))
PACK_TOTAL = sum(r for _, r in PACK_ROWS)
PACK_HALF = PACK_TOTAL // 2
SMALL_SIZES = (("q_a_gain", 256), ("kv_a_gain", 128), ("b_merge_gate", 3072), ("ln_gain", 1024), ("ln_bias", 1024))
SMALL_TOTAL = sum(s for _, s in SMALL_SIZES)
COL_SHARDED = ("w_in", "w_q_b", "w_kv_b", "w_branch_mla", "w_branch_sb", "w_branch_mem", "w_merge_gate")
ROW_SHARDED = ("w_mem_kv", "w_out")
FULL_SHAPES = {"w_in": (1024, 4000), "w_mem_kv": (1024, 1024), "w_q_b": (256, 768), "w_kv_b": (128, 1024),
               "w_branch_mla": (512, 1024), "w_branch_sb": (512, 1024), "w_branch_mem": (512, 1024),
               "w_merge_gate": (1024, 3072), "w_out": (1024, 1024)}


def _cparams(sem=None):
    return pltpu.CompilerParams(dimension_semantics=sem, vmem_limit_bytes=VMEM_LIMIT)


def _dot(a, b):
    return jnp.dot(a, b, preferred_element_type=F32)


def _dot_nt(a, b):
    return lax.dot_general(a, b, (((1,), (1,)), ((), ())), preferred_element_type=F32)


def _dot_tn(a, b):
    return lax.dot_general(a, b, (((0,), (0,)), ((), ())), preferred_element_type=F32)


def _bf(x):
    return x.astype(BF16)


def _sigmoid(x):
    return 1.0 / (1.0 + jnp.exp(-x))


def _matmul(a, b, *, mode, tm, tn, tk, out_dtypes, name, add=None, add_scale=1.0, b_block0=0, n=None):
    if mode == "nn":
        (m, k), n = a.shape, b.shape[1]
        a_spec = pl.BlockSpec((tm, tk), lambda i, j, kk: (i, kk))
        b_spec = pl.BlockSpec((tk, tn), lambda i, j, kk: (kk, j))
        dot = _dot
    elif mode == "nt":
        (m, k), n = a.shape, b.shape[0]
        a_spec = pl.BlockSpec((tm, tk), lambda i, j, kk: (i, kk))
        b_spec = pl.BlockSpec((tn, tk), lambda i, j, kk: (j, kk))
        dot = _dot_nt
    else:
        (k, m), n = a.shape, (b.shape[1] if n is None else n)
        a_spec = pl.BlockSpec((tk, tm), lambda i, j, kk: (kk, i))
        b_spec = pl.BlockSpec((tk, tn), lambda i, j, kk: (kk, j + b_block0))
        dot = _dot_tn
    assert m % tm == 0 and n % tn == 0 and k % tk == 0, (name, m, n, k)
    nk = k // tk
    n_out = len(out_dtypes)
    has_add = add is not None

    def body(*refs):
        a_ref, b_ref = refs[0], refs[1]
        add_ref = refs[2] if has_add else None
        outs = refs[2 + has_add: 2 + has_add + n_out]
        acc = refs[-1]
        kk = pl.program_id(2)

        @pl.when(kk == 0)
        def _():
            acc[...] = jnp.zeros_like(acc)

        acc[...] += dot(_bf(a_ref[...]), _bf(b_ref[...]))

        @pl.when(kk == nk - 1)
        def _():
            r = acc[...]
            if has_add:
                r = r + add_scale * add_ref[...]
            for o in outs:
                o[...] = r.astype(o.dtype)

    in_specs = [a_spec, b_spec]
    args = [a, b]
    if has_add:
        in_specs.append(pl.BlockSpec((tm, tn), lambda i, j, kk: (i, j)))
        args.append(add)
    out_spec = pl.BlockSpec((tm, tn), lambda i, j, kk: (i, j))
    res = pl.pallas_call(
        body, name=name, grid=(m // tm, n // tn, nk),
        in_specs=in_specs, out_specs=[out_spec] * n_out,
        out_shape=[jax.ShapeDtypeStruct((m, n), dt) for dt in out_dtypes],
        scratch_shapes=[pltpu.VMEM((tm, tn), F32)],
        compiler_params=_cparams(("parallel", "parallel", "arbitrary")),
    )(*args)
    return res


def _rope_tables(seq):
    half = MLA_ROPE // 2
    freqs = ROPE_BASE ** (-jnp.arange(half, dtype=F32) / half)
    ang = jnp.arange(seq, dtype=jnp.int32).astype(F32)[:, None] * freqs[None, :]
    cos, sin = jnp.cos(ang), jnp.sin(ang)
    z = lambda w: jnp.zeros((seq, w), F32)
    c_q = jnp.concatenate([jnp.ones((seq, MLA_NOPE), F32), cos, cos, z(32)], axis=1)
    c_k = jnp.concatenate([z(MLA_NOPE), cos, cos, z(32)], axis=1)
    s_lo = jnp.concatenate([z(MLA_NOPE), -sin, z(half), z(32)], axis=1)
    s_hi = jnp.concatenate([z(MLA_NOPE), z(half), sin, z(32)], axis=1)
    return c_q, c_k, s_lo, s_hi


def _rope_fwd(x, c, s_lo, s_hi):
    return x * c + pltpu.roll(x, LANES - 16, 1) * s_lo + pltpu.roll(x, 16, 1) * s_hi


def _rope_bwd(d, c, s_lo, s_hi):
    return d * c - pltpu.roll(d, 16, 1) * s_hi - pltpu.roll(d, LANES - 16, 1) * s_lo


def _rms_fwd(x, g):
    r = lax.rsqrt(jnp.mean(x * x, axis=-1, keepdims=True) + RMS_EPS)
    xn = x * r
    return xn * g, xn, r


def _mla_prep(p32, gq, gkv, wqb, wkvb, tabs, *, t):
    seq = p32.shape[0]

    def body(lat_ref, gq_ref, gkv_ref, wqb_ref, wkvb_ref, cq_ref, ck_ref, slo_ref, shi_ref, q_ref, k_ref, v_ref):
        lat = lat_ref[...]
        slo, shi = slo_ref[...], shi_ref[...]
        nq, _, _ = _rms_fwd(lat[:, 0:MLA_Q_LORA], gq_ref[...])
        qa = _dot(_bf(nq), wqb_ref[...])
        cq = cq_ref[...]
        for h in range(MLA_HEADS):
            blk = qa[:, h * LANES:(h + 1) * LANES]
            q_ref[:, h * LANES:(h + 1) * LANES] = _bf(_rope_fwd(blk, cq, slo, shi))
        nkv, _, _ = _rms_fwd(lat[:, MLA_Q_LORA:MLA_Q_LORA + MLA_KV_LORA], gkv_ref[...])
        kv = _dot(_bf(nkv), wkvb_ref[...])
        kpe = _rope_fwd(lat[:, 384:512], ck_ref[...], slo, shi)
        for h in range(MLA_HEADS):
            k_ref[:, h * LANES:(h + 1) * LANES] = _bf(kv[:, h * LANES:(h + 1) * LANES] + kpe)
        v_ref[...] = _bf(kv[:, MLA_HEADS * LANES:])

    row = lambda w: pl.BlockSpec((t, w), lambda i: (i, 0))
    full = lambda shp: pl.BlockSpec(shp, lambda i: (0, 0))
    return pl.pallas_call(
        body, name="mla_prep", grid=(seq // t,),
        in_specs=[row(512), full((1, MLA_Q_LORA)), full((1, MLA_KV_LORA)), full(wqb.shape), full(wkvb.shape),
                  row(LANES), row(LANES), row(LANES), row(LANES)],
        out_specs=[row(1024), row(1024), row(512)],
        out_shape=[jax.ShapeDtypeStruct((seq, 1024), BF16), jax.ShapeDtypeStruct((seq, 1024), BF16),
                   jax.ShapeDtypeStruct((seq, 512), BF16)],
        compiler_params=_cparams(("parallel",)),
    )(p32, gq, gkv, wqb, wkvb, *tabs)


def _mla_post(p32, dq, dk, dv, gq, gkv, wqb, wkvb, tabs, *, t):
    seq = p32.shape[0]

    def body(lat_ref, dq_ref, dk_ref, dv_ref, gq_ref, gkv_ref, wqb_ref, wkvb_ref, cq_ref, ck_ref, slo_ref, shi_ref,
             dlat_ref, dwqb_ref, dwkvb_ref, dgq_ref, dgkv_ref):
        @pl.when(pl.program_id(0) == 0)
        def _():
            dwqb_ref[...] = jnp.zeros_like(dwqb_ref)
            dwkvb_ref[...] = jnp.zeros_like(dwkvb_ref)
            dgq_ref[...] = jnp.zeros_like(dgq_ref)
            dgkv_ref[...] = jnp.zeros_like(dgkv_ref)

        lat = lat_ref[...]
        slo, shi = slo_ref[...], shi_ref[...]
        cq = cq_ref[...]
        gq_v, gkv_v = gq_ref[...], gkv_ref[...]
        nq, xq, rq = _rms_fwd(lat[:, 0:MLA_Q_LORA], gq_v)
        nkv, xkv, rkv = _rms_fwd(lat[:, MLA_Q_LORA:MLA_Q_LORA + MLA_KV_LORA], gkv_v)

        dqa = jnp.concatenate(
            [_rope_bwd(dq_ref[:, h * LANES:(h + 1) * LANES], cq, slo, shi) for h in range(MLA_HEADS)], axis=1)
        dqa_b = _bf(dqa)
        dwqb_ref[...] += _dot_tn(_bf(nq), dqa_b)
        dnq = _dot_nt(dqa_b, wqb_ref[...])
        dgq_ref[...] += jnp.sum(dnq * xq, axis=0, keepdims=True)
        dxn = dnq * gq_v
        dcq = rq * (dxn - xq * jnp.mean(dxn * xq, axis=-1, keepdims=True))

        dkf = dk_ref[...]
        dkv_b = _bf(jnp.concatenate([dkf, dv_ref[...]], axis=1))
        dwkvb_ref[...] += _dot_tn(_bf(nkv), dkv_b)
        dnkv = _dot_nt(dkv_b, wkvb_ref[...])
        dgkv_ref[...] += jnp.sum(dnkv * xkv, axis=0, keepdims=True)
        dxn = dnkv * gkv_v
        dckv = rkv * (dxn - xkv * jnp.mean(dxn * xkv, axis=-1, keepdims=True))

        dkpe = dkf[:, 0:LANES]
        for h in range(1, MLA_HEADS):
            dkpe = dkpe + dkf[:, h * LANES:(h + 1) * LANES]
        dkr = _rope_bwd(dkpe, ck_ref[...], slo, shi)
        dlat_ref[...] = _bf(jnp.concatenate([dcq, dckv, dkr], axis=1))

    row = lambda w: pl.BlockSpec((t, w), lambda i: (i, 0))
    full = lambda shp: pl.BlockSpec(shp, lambda i: (0, 0))
    return pl.pallas_call(
        body, name="mla_post", grid=(seq // t,),
        in_specs=[row(512), row(1024), row(1024), row(512), full((1, MLA_Q_LORA)), full((1, MLA_KV_LORA)),
                  full(wqb.shape), full(wkvb.shape), row(LANES), row(LANES), row(LANES), row(LANES)],
        out_specs=[row(512), full(wqb.shape), full(wkvb.shape), full((1, MLA_Q_LORA)), full((1, MLA_KV_LORA))],
        out_shape=[jax.ShapeDtypeStruct((seq, 512), BF16), jax.ShapeDtypeStruct(wqb.shape, F32),
                   jax.ShapeDtypeStruct(wkvb.shape, F32), jax.ShapeDtypeStruct((1, MLA_Q_LORA), F32),
                   jax.ShapeDtypeStruct((1, MLA_KV_LORA), F32)],
        compiler_params=_cparams(("arbitrary",)),
    )(p32, dq, dk, dv, gq, gkv, wqb, wkvb, *tabs)


def _split_bf16(x):
    hi = _bf(x)
    return hi, _bf(x - hi.astype(F32))


def _tri_sum(x, u):
    hi, lo = _split_bf16(x)
    return _dot(hi, u) + _dot(lo, u)


def _softplus(z):
    return jnp.maximum(z, 0.0) + jnp.log(1.0 + jnp.exp(-jnp.abs(z)))


def _head_queries(q, left):
    zero = jnp.zeros_like(q)
    return jnp.where(left, q, zero) * SB_SCALE, jnp.where(left, zero, q) * SB_SCALE


ROW_GROUP = 128
ANY_HBM = pl.BlockSpec(memory_space=pltpu.HBM)


class _Exchange:
    def __init__(self, send, landing):
        self.send, self.landing = send, landing

    def start(self):
        self.send.start()

    def wait(self):
        self.landing.wait_recv()
        self.send.wait_send()


class _Rider:
    def __init__(self, operands, out_shapes, sem_shapes, copies):
        self.operands, self.out_shapes, self.sem_shapes, self.copies = list(operands), list(out_shapes), list(sem_shapes), copies


def _call_with_rider(body, rider, *, name, grid, in_specs, out_specs, out_shape, args, semantics):
    if rider is None:
        return pl.pallas_call(body, name=name, grid=grid, in_specs=in_specs, out_specs=out_specs, out_shape=out_shape,
                              compiler_params=_cparams(semantics))(*args)
    n_in, n_out, n_rin, n_rout = len(in_specs), len(out_specs), len(rider.operands), len(rider.out_shapes)

    def full_body(*refs):
        ins, r_ins = refs[:n_in], refs[n_in:n_in + n_rin]
        outs = refs[n_in + n_rin:n_in + n_rin + n_out]
        r_outs = refs[n_in + n_rin + n_out:n_in + n_rin + n_out + n_rout]
        sems = refs[n_in + n_rin + n_out + n_rout:]
        p, i = pl.program_id(0), pl.program_id(1)

        @pl.when((p == 0) & (i == 0))
        def _():
            for cp in rider.copies(r_ins, r_outs, sems):
                cp.start()

        body(*ins, *outs)

        @pl.when((p == grid[0] - 1) & (i == grid[1] - 1))
        def _():
            for cp in rider.copies(r_ins, r_outs, sems):
                cp.wait()

    return pl.pallas_call(
        full_body, name=name, grid=grid, in_specs=list(in_specs) + [ANY_HBM] * n_rin,
        out_specs=list(out_specs) + [ANY_HBM] * n_rout, out_shape=list(out_shape) + rider.out_shapes,
        scratch_shapes=rider.sem_shapes, compiler_params=_cparams(("arbitrary", "arbitrary")),
    )(*args, *rider.operands)


def _chains(tq):
    return [(h, g) for g in range(tq // ROW_GROUP) for h in range(2)]


def _chain_pattern(g, m, tk, strict):
    r_lo, r_hi = g * ROW_GROUP, (g + 1) * ROW_GROUP - 1
    c_lo, c_hi = m * tk, (m + 1) * tk - 1
    if (c_lo >= r_hi) if strict else (c_lo > r_hi):
        return None
    if (c_hi < r_lo) if strict else (c_hi <= r_lo):
        return True
    rr = lax.broadcasted_iota(jnp.int32, (ROW_GROUP, tk), 0) + r_lo
    cc = lax.broadcasted_iota(jnp.int32, (ROW_GROUP, tk), 1) + c_lo
    return (cc < rr) if strict else (cc <= rr)


def _masked(x, pat, fill=0.0):
    return x if pat is True else jnp.where(pat, x, fill)


def _rows(g):
    return slice(g * ROW_GROUP, (g + 1) * ROW_GROUP)


def _tri_matrix(tk, cmp):
    rr = lax.broadcasted_iota(jnp.int32, (tk, tk), 0)
    cc = lax.broadcasted_iota(jnp.int32, (tk, tk), 1)
    return cmp(rr, cc).astype(BF16)


def _mla_attn_fwd(qp, kp, vp, *, tq, tk, rider=None):
    seq = qp.shape[0]
    nd = tq // tk
    neg = float(np.finfo(np.float32).min)
    chains = _chains(tq)

    def body(q_ref, k_ref, v_ref, o_ref, lse_ref):
        i = pl.program_id(1)
        left = lax.broadcasted_iota(jnp.int32, (tq, LANES), 1) < HALF
        qs = [q_ref[_rows(g), h * LANES:(h + 1) * LANES] for h, g in chains]

        def block(j, carry, m):
            start = pl.multiple_of(j * tk, tk)
            v = v_ref[pl.ds(start, tk), :]
            pats = [True if m is None else _chain_pattern(g, m, tk, False) for _, g in chains]
            live = [n for n, p in enumerate(pats) if p is not None]
            ss = {n: _dot_nt(qs[n], k_ref[pl.ds(start, tk), chains[n][0] * LANES:(chains[n][0] + 1) * LANES]) for n in live}
            new = list(carry)
            for n in live:
                m_old, l_old, acc = carry[n]
                s = _masked(ss[n] * MLA_SCALE, pats[n], neg)
                m_new = jnp.maximum(m_old, jnp.max(s, axis=-1, keepdims=True))
                a = jnp.exp(m_old - m_new)
                p = jnp.exp(s - m_new)
                new[n] = (m_new, a * l_old + jnp.sum(p, axis=-1, keepdims=True), a * acc + _dot(_bf(p), v))
            return tuple(new)

        init = (jnp.full((ROW_GROUP, 1), -1e30, F32), jnp.zeros((ROW_GROUP, 1), F32), jnp.zeros((ROW_GROUP, LANES), F32))
        carry = lax.fori_loop(0, i * nd, lambda j, c: block(j, c, None), (init,) * len(chains))
        for m in range(nd):
            carry = block(i * nd + m, carry, m)
        per_head = []
        for h in range(2):
            mine = [carry[n] for n, (ch, _) in enumerate(chains) if ch == h]
            per_head.append((jnp.concatenate([acc / l for _, l, acc in mine], axis=0),
                             jnp.concatenate([mm + jnp.log(l) for mm, l, _ in mine], axis=0)))
        o_ref[...] = jnp.where(left, per_head[0][0], per_head[1][0])
        lse_ref[...] = jnp.where(left, per_head[0][1], per_head[1][1])

    return _call_with_rider(
        body, rider, name="mla_fwd", grid=(MLA_HEADS // 2, seq // tq),
        in_specs=[pl.BlockSpec((tq, 2 * LANES), lambda p, i: (i, p)), pl.BlockSpec((seq, 2 * LANES), lambda p, i: (0, p)),
                  pl.BlockSpec((seq, LANES), lambda p, i: (0, p))],
        out_specs=[pl.BlockSpec((tq, LANES), lambda p, i: (i, p)), pl.BlockSpec((tq, LANES), lambda p, i: (i, p))],
        out_shape=[jax.ShapeDtypeStruct((seq, 512), F32), jax.ShapeDtypeStruct((seq, 512), F32)],
        args=(qp, kp, vp), semantics=("parallel", "parallel"))


def _mla_attn_bwd(qp, kp, vp, o, lse, do, *, tq, tk, rider=None):
    seq = qp.shape[0]
    nd = tq // tk
    chains = _chains(tq)

    def body(q_ref, k_ref, v_ref, o_ref, lse_ref, do_ref, dq_ref, dk_ref, dv_ref):
        i = pl.program_id(1)

        @pl.when(i == 0)
        def _():
            dk_ref[...] = jnp.zeros_like(dk_ref)
            dv_ref[...] = jnp.zeros_like(dv_ref)

        left = lax.broadcasted_iota(jnp.int32, (tq, LANES), 1) < HALF
        do_f = do_ref[...]
        prod = do_f * o_ref[...]
        lse_v = lse_ref[...]
        do_heads = (_bf(jnp.where(left, do_f, 0.0)), _bf(jnp.where(left, 0.0, do_f)))
        delta_heads = (jnp.sum(jnp.where(left, prod, 0.0), axis=-1, keepdims=True),
                       jnp.sum(jnp.where(left, 0.0, prod), axis=-1, keepdims=True))
        qs = [q_ref[_rows(g), h * LANES:(h + 1) * LANES] for h, g in chains]
        dos = [do_heads[h][_rows(g)] for h, g in chains]
        deltas = [delta_heads[h][_rows(g)] for h, g in chains]
        lses = [lse_v[_rows(g), h * HALF:h * HALF + 1] for h, g in chains]

        def block(j, carry, m):
            start = pl.multiple_of(j * tk, tk)
            v = v_ref[pl.ds(start, tk), :]
            pats = [True if m is None else _chain_pattern(g, m, tk, False) for _, g in chains]
            live = [n for n, p in enumerate(pats) if p is not None]
            ks = [k_ref[pl.ds(start, tk), h * LANES:(h + 1) * LANES] for h in range(2)]
            ss = {n: _dot_nt(qs[n], ks[chains[n][0]]) for n in live}
            dps = {n: _dot_nt(dos[n], v) for n in live}
            new = list(carry)
            ps, dss = {}, {}
            for n in live:
                p = _masked(jnp.exp(ss[n] * MLA_SCALE - lses[n]), pats[n])
                ps[n] = _bf(p)
                dss[n] = _bf(p * (dps[n] - deltas[n]) * MLA_SCALE)
                new[n] = carry[n] + _dot(dss[n], ks[chains[n][0]])
            for h in range(2):
                mine = [n for n in live if chains[n][0] == h]
                ds_cat = jnp.concatenate([dss[n] for n in mine], axis=0)
                q_cat = jnp.concatenate([qs[n] for n in mine], axis=0)
                dk_ref[pl.ds(start, tk), h * LANES:(h + 1) * LANES] += _dot_tn(ds_cat, q_cat)
            p_cat = jnp.concatenate([ps[n] for n in live], axis=0)
            do_cat = jnp.concatenate([dos[n] for n in live], axis=0)
            dv_ref[pl.ds(start, tk), :] += _dot_tn(p_cat, do_cat)
            return tuple(new)

        zero = jnp.zeros((ROW_GROUP, LANES), F32)
        carry = lax.fori_loop(0, i * nd, lambda j, c: block(j, c, None), (zero,) * len(chains))
        for m in range(nd):
            carry = block(i * nd + m, carry, m)
        for n, (h, g) in enumerate(chains):
            dq_ref[_rows(g), h * LANES:(h + 1) * LANES] = carry[n]

    two_t = pl.BlockSpec((tq, 2 * LANES), lambda p, i: (i, p))
    two_s = pl.BlockSpec((seq, 2 * LANES), lambda p, i: (0, p))
    pair_t = pl.BlockSpec((tq, LANES), lambda p, i: (i, p))
    pair_s = pl.BlockSpec((seq, LANES), lambda p, i: (0, p))
    return _call_with_rider(
        body, rider, name="mla_bwd", grid=(MLA_HEADS // 2, seq // tq),
        in_specs=[two_t, two_s, pair_s, pair_t, pair_t, pair_t],
        out_specs=[two_t, two_s, pair_s],
        out_shape=[jax.ShapeDtypeStruct((seq, 1024), F32), jax.ShapeDtypeStruct((seq, 1024), F32),
                   jax.ShapeDtypeStruct((seq, 512), F32)],
        args=(qp, kp, vp, o, lse, do), semantics=("parallel", "arbitrary"))


def _sb_attn_fwd(pbf, *, tq, tk):
    seq = pbf.shape[0]
    nd = tq // tk
    qb, kb, vb = BLK_QB * 4, BLK_KB * 4, BLK_VB * 4
    chains = _chains(tq)

    def body(q_ref, k_ref, v_ref, o_ref, tot_ref):
        i = pl.program_id(1)
        u_later = _tri_matrix(tk, lambda r, c: r > c)
        left = lax.broadcasted_iota(jnp.int32, (tq, LANES), 1) < HALF
        q_heads = _head_queries(q_ref[...], left)
        qs = [q_heads[h][_rows(g)] for h, g in chains]

        def block(j, carry, m):
            start = pl.multiple_of(j * tk, tk)
            k = k_ref[pl.ds(start, tk), :]
            v = v_ref[pl.ds(start, tk), :]
            pats = [True if m is None else _chain_pattern(g, m, tk, True) for _, g in chains]
            live = [n for n, p in enumerate(pats) if p is not None]
            zs = {n: _dot_nt(qs[n], k) for n in live}
            raws = {n: _softplus(zs[n]) for n in live}
            sps = {n: _masked(raws[n], pats[n]) for n in live}
            laters = {n: _tri_sum(sps[n], u_later) for n in live}
            new = list(carry)
            for n in live:
                c, acc = carry[n]
                a = _masked(jnp.exp(zs[n] - raws[n] - laters[n] - c), pats[n])
                new[n] = (c + laters[n][:, 0:1] + sps[n][:, 0:1], acc + _dot(_bf(a), v))
            return tuple(new)

        init = (jnp.zeros((ROW_GROUP, 1), F32), jnp.zeros((ROW_GROUP, LANES), F32))
        carry = (init,) * len(chains)
        for m in reversed(range(nd)):
            carry = block(i * nd + m, carry, m)
        carry = lax.fori_loop(0, i * nd, lambda jj, cr: block(i * nd - 1 - jj, cr, None), carry)
        per_head = []
        for h in range(2):
            mine = [carry[n] for n, (ch, _) in enumerate(chains) if ch == h]
            per_head.append((jnp.concatenate([acc for _, acc in mine], axis=0), jnp.concatenate([c for c, _ in mine], axis=0)))
        o_ref[...] = jnp.where(left, per_head[0][0], per_head[1][0])
        tot_ref[...] = jnp.where(left, per_head[0][1], per_head[1][1])

    pair_t = pl.BlockSpec((tq, LANES), lambda p, i: (i, p))
    return pl.pallas_call(
        body, name="sb_fwd", grid=(SB_HEADS // 2, seq // tq),
        in_specs=[pl.BlockSpec((tq, LANES), lambda p, i: (i, qb + p)), pl.BlockSpec((seq, LANES), lambda p, i: (0, kb + p)),
                  pl.BlockSpec((seq, LANES), lambda p, i: (0, vb + p))],
        out_specs=[pair_t, pair_t],
        out_shape=[jax.ShapeDtypeStruct((seq, 512), F32), jax.ShapeDtypeStruct((seq, 512), F32)],
        compiler_params=_cparams(("parallel", "parallel")),
    )(pbf, pbf, pbf)


def _sb_attn_bwd(pbf, tot, do, *, tq, tk, rider=None):
    seq = pbf.shape[0]
    nd = tq // tk
    qb, kb, vb = BLK_QB * 4, BLK_KB * 4, BLK_VB * 4
    chains = _chains(tq)

    def body(q_ref, k_ref, v_ref, tot_ref, do_ref, dq_ref, dk_ref, dv_ref):
        i = pl.program_id(1)

        @pl.when(i == 0)
        def _():
            dk_ref[...] = jnp.zeros_like(dk_ref)
            dv_ref[...] = jnp.zeros_like(dv_ref)

        u_upto = _tri_matrix(tk, lambda r, c: r <= c)
        u_below = _tri_matrix(tk, lambda r, c: r < c)
        left = lax.broadcasted_iota(jnp.int32, (tq, LANES), 1) < HALF
        q_heads = _head_queries(q_ref[...], left)
        do_f = do_ref[...]
        do_heads = (_bf(jnp.where(left, do_f, 0.0)), _bf(jnp.where(left, 0.0, do_f)))
        tot_v = tot_ref[...]
        qs = [q_heads[h][_rows(g)] for h, g in chains]
        dos = [do_heads[h][_rows(g)] for h, g in chains]
        totals = [tot_v[_rows(g), h * HALF:h * HALF + 1] for h, g in chains]

        def block(j, carry, m):
            start = pl.multiple_of(j * tk, tk)
            k = k_ref[pl.ds(start, tk), :]
            v = v_ref[pl.ds(start, tk), :]
            pats = [True if m is None else _chain_pattern(g, m, tk, True) for _, g in chains]
            live = [n for n, p in enumerate(pats) if p is not None]
            zs = {n: _dot_nt(qs[n], k) for n in live}
            das = {n: _dot_nt(dos[n], v) for n in live}
            raws = {n: _softplus(zs[n]) for n in live}
            sps = {n: _masked(raws[n], pats[n]) for n in live}
            uptos = {n: _tri_sum(sps[n], u_upto) for n in live}
            lbs, a_s, gs = {}, {}, {}
            for n in live:
                lbs[n] = zs[n] - raws[n]
                a =_masked(jnp.exp(lbs[n] - (totals[n] - carry[n][0] - uptos[n])), pats[n])
                a_s[n] = _bf(a)
                gs[n] = das[n] * a
            belows = {n: _tri_sum(gs[n], u_below) for n in live}
            new = list(carry)
            dzs = {}
            for n in live:
                sp_before, g_before, dq_acc = carry[n]
                beta = jnp.exp(lbs[n])
                dz = _masked(gs[n] * (1.0 - beta) - (g_before + belows[n]) * beta, pats[n])
                dzs[n] = _bf(dz)
                new[n] = (sp_before + uptos[n][:, tk - 1:tk], g_before + belows[n][:, tk - 1:tk] + gs[n][:, tk - 1:tk],
                          dq_acc + _dot(dzs[n], k))
            dz_cat = jnp.concatenate([dzs[n] for n in live], axis=0)
            q_cat = jnp.concatenate([qs[n] for n in live], axis=0)
            dk_ref[pl.ds(start, tk), :] += _dot_tn(dz_cat, q_cat)
            a_cat = jnp.concatenate([a_s[n] for n in live], axis=0)
            do_cat = jnp.concatenate([dos[n] for n in live], axis=0)
            dv_ref[pl.ds(start, tk), :] += _dot_tn(a_cat, do_cat)
            return tuple(new)

        zero = jnp.zeros((ROW_GROUP, 1), F32)
        init = (zero, zero, jnp.zeros((ROW_GROUP, LANES), F32))
        carry = lax.fori_loop(0, i * nd, lambda j, cr: block(j, cr, None), (init,) * len(chains))
        for m in range(nd):
            carry = block(i * nd + m, carry, m)
        per_head = [jnp.concatenate([carry[n][2] for n, (ch, _) in enumerate(chains) if ch == h], axis=0) for h in range(2)]
        dq_ref[...] = jnp.where(left, per_head[0], per_head[1]) * SB_SCALE

    pair_t = pl.BlockSpec((tq, LANES), lambda p, i: (i, p))
    pair_s = pl.BlockSpec((seq, LANES), lambda p, i: (0, p))
    return _call_with_rider(
        body, rider, name="sb_bwd", grid=(SB_HEADS // 2, seq // tq),
        in_specs=[pl.BlockSpec((tq, LANES), lambda p, i: (i, qb + p)), pl.BlockSpec((seq, LANES), lambda p, i: (0, kb + p)),
                  pl.BlockSpec((seq, LANES), lambda p, i: (0, vb + p)), pair_t, pair_t],
        out_specs=[pair_t, pair_s, pair_s],
        out_shape=[jax.ShapeDtypeStruct((seq, 512), F32)] * 3,
        args=(pbf, pbf, pbf, tot, do), semantics=("parallel", "arbitrary"))


def _mem_probs(q, k):
    s = _dot_nt(q, k) * MEM_SCALE
    e = jnp.exp(s - jnp.max(s, axis=-1, keepdims=True))
    return e / jnp.sum(e, axis=-1, keepdims=True)


def _mem_fwd(pbf, mkv, *, t):
    seq = pbf.shape[0]

    def body(q_ref, kv_ref, o_ref):
        for h in range(MEM_HEADS):
            lo, hi = h * LANES, (h + 1) * LANES
            p = _mem_probs(q_ref[:, lo:hi], kv_ref[:, lo:hi])
            o_ref[:, lo:hi] = _dot(_bf(p), kv_ref[:, 512 + lo:512 + hi])

    return pl.pallas_call(
        body, name="mem_fwd", grid=(seq // t,),
        in_specs=[pl.BlockSpec((t, 512), lambda i: (i, BLK_QM)), pl.BlockSpec((MEM_LEN, 1024), lambda i: (0, 0))],
        out_specs=pl.BlockSpec((t, 512), lambda i: (i, 0)),
        out_shape=jax.ShapeDtypeStruct((seq, 512), F32),
        compiler_params=_cparams(("parallel",)),
    )(pbf, mkv)


def _mem_bwd(pbf, mkv, do, *, t):
    seq = pbf.shape[0]

    def body(q_ref, kv_ref, do_ref, dq_ref, dkv_ref):
        @pl.when(pl.program_id(0) == 0)
        def _():
            dkv_ref[...] = jnp.zeros_like(dkv_ref)

        for h in range(MEM_HEADS):
            lo, hi = h * LANES, (h + 1) * LANES
            q, k, v = q_ref[:, lo:hi], kv_ref[:, lo:hi], kv_ref[:, 512 + lo:512 + hi]
            do_h = _bf(do_ref[:, lo:hi])
            p = _mem_probs(q, k)
            dp = _dot_nt(do_h, v)
            ds = _bf(p * (dp - jnp.sum(dp * p, axis=-1, keepdims=True)) * MEM_SCALE)
            dq_ref[:, lo:hi] = _dot(ds, k)
            dkv_ref[:, lo:hi] += _dot_tn(ds, q)
            dkv_ref[:, 512 + lo:512 + hi] += _dot_tn(_bf(p), do_h)

    return pl.pallas_call(
        body, name="mem_bwd", grid=(seq // t,),
        in_specs=[pl.BlockSpec((t, 512), lambda i: (i, BLK_QM)), pl.BlockSpec((MEM_LEN, 1024), lambda i: (0, 0)),
                  pl.BlockSpec((t, 512), lambda i: (i, 0))],
        out_specs=[pl.BlockSpec((t, 512), lambda i: (i, 0)), pl.BlockSpec((MEM_LEN, 1024), lambda i: (0, 0))],
        out_shape=[jax.ShapeDtypeStruct((seq, 512), F32), jax.ShapeDtypeStruct((MEM_LEN, 1024), F32)],
        compiler_params=_cparams(("arbitrary",)),
    )(pbf, mkv, do)


def _mid(x, tgt, o_a, o_b, o_m, p32, wmg, bmg, wba, wbb, wbm, wout, ln_g, ln_b, *, t):
    seq = x.shape[0]
    inv_d = 1.0 / D_MODEL

    def body(x_ref, t_ref, oa_ref, ob_ref, om_ref, ga_ref, gb_ref, gm_ref, wmg_ref, bmg_ref, wba_ref, wbb_ref,
             wbm_ref, wout_ref, lg_ref, lb_ref,
             du_ref, mrg_ref, dgp_ref, ha_ref, hb_ref, hm_ref, dya_ref, dyb_ref, dym_ref, doa_ref, dob_ref, dom_ref,
             dga_ref, dgb_ref, dgm_ref, dgain_ref, dbias_ref, dbmg_ref, loss_ref):
        @pl.when(pl.program_id(0) == 0)
        def _():
            dgain_ref[...] = jnp.zeros_like(dgain_ref)
            dbias_ref[...] = jnp.zeros_like(dbias_ref)
            dbmg_ref[...] = jnp.zeros_like(dbmg_ref)
            loss_ref[...] = jnp.zeros_like(loss_ref)

        xv = x_ref[...]
        gate = _sigmoid(_dot(_bf(xv), wmg_ref[...]) + bmg_ref[...])

        branches = []
        merged = None
        for b, (o_ref, g_ref, w_ref, h_ref) in enumerate(((oa_ref, ga_ref, wba_ref, ha_ref), (ob_ref, gb_ref, wbb_ref, hb_ref),
                                                         (om_ref, gm_ref, wbm_ref, hm_ref))):
            o, gt = o_ref[...], g_ref[...]
            sg = _sigmoid(gt)
            silu = gt * sg
            h = _bf(o * silu)
            h_ref[...] = h
            y = _dot(h, w_ref[...])
            g_b = gate[:, b * D_MODEL:(b + 1) * D_MODEL]
            term = g_b * y
            merged = term if merged is None else merged + term
            branches.append((o, gt, sg, silu, y, g_b))
        mrg_b = _bf(merged)
        mrg_ref[...] = mrg_b

        u = DEEPNORM_ALPHA * xv + _dot(mrg_b, wout_ref[...])
        mu = jnp.mean(u, axis=-1, keepdims=True)
        uc = u - mu
        rstd = lax.rsqrt(jnp.mean(uc * uc, axis=-1, keepdims=True) + LN_EPS)
        xhat = uc * rstd
        lg = lg_ref[...]
        y_out = xhat * lg + lb_ref[...]
        err = y_out - t_ref[...]
        loss_ref[...] += 0.5 * jnp.sum(jnp.mean(err * err, axis=-1, keepdims=True), axis=0, keepdims=True)
        dy = err * inv_d
        dgain_ref[...] += jnp.sum(dy * xhat, axis=0, keepdims=True)
        dbias_ref[...] += jnp.sum(dy, axis=0, keepdims=True)
        dxh = dy * lg
        du = rstd * (dxh - jnp.mean(dxh, axis=-1, keepdims=True) - xhat * jnp.mean(dxh * xhat, axis=-1, keepdims=True))
        du_ref[...] = du

        dmerged = _dot_nt(_bf(du), wout_ref[...])
        outs = ((dya_ref, doa_ref, dga_ref, wba_ref), (dyb_ref, dob_ref, dgb_ref, wbb_ref), (dym_ref, dom_ref, dgm_ref, wbm_ref))
        dgp = []
        for (o, gt, sg, silu, y, g_b), (dy_ref, do_ref, dg_ref, w_ref) in zip(branches, outs):
            dyb = _bf(dmerged * g_b)
            dy_ref[...] = dyb
            dgp.append(dmerged * y * g_b * (1.0 - g_b))
            dh = _dot_nt(dyb, w_ref[...])
            do_ref[...] = dh * silu
            dg_ref[...] = _bf(dh * o * (sg * (1.0 + gt * (1.0 - sg))))
        dgp = jnp.concatenate(dgp, axis=1)
        dgp_ref[...] = _bf(dgp)
        dbmg_ref[...] += jnp.sum(dgp, axis=0, keepdims=True)

    row = lambda w: pl.BlockSpec((t, w), lambda i: (i, 0))
    pblk = lambda c: pl.BlockSpec((t, 512), lambda i: (i, c))
    full = lambda shp: pl.BlockSpec(shp, lambda i: (0, 0))
    sds = jax.ShapeDtypeStruct
    return pl.pallas_call(
        body, name="mid", grid=(seq // t,),
        in_specs=[row(1024), row(1024), row(512), row(512), row(512), pblk(BLK_GATE_A), pblk(BLK_GATE_B), pblk(BLK_GATE_M),
                  full(wmg.shape), full((1, N_MERGE)), full(wba.shape), full(wbb.shape), full(wbm.shape), full(wout.shape),
                  full((1, D_MODEL)), full((1, D_MODEL))],
        out_specs=[row(1024), row(1024), row(N_MERGE), row(512), row(512), row(512), row(1024), row(1024), row(1024),
                   row(512), row(512), row(512), row(512), row(512), row(512),
                   full((1, D_MODEL)), full((1, D_MODEL)), full((1, N_MERGE)), full((1, 1))],
        out_shape=[sds((seq, 1024), F32), sds((seq, 1024), BF16), sds((seq, N_MERGE), BF16),
                   sds((seq, 512), BF16), sds((seq, 512), BF16), sds((seq, 512), BF16),
                   sds((seq, 1024), BF16), sds((seq, 1024), BF16), sds((seq, 1024), BF16),
                   sds((seq, 512), F32), sds((seq, 512), F32), sds((seq, 512), F32),
                   sds((seq, 512), BF16), sds((seq, 512), BF16), sds((seq, 512), BF16),
                   sds((1, D_MODEL), F32), sds((1, D_MODEL), F32), sds((1, N_MERGE), F32), sds((1, 1), F32)],
        compiler_params=_cparams(("arbitrary",)),
    )(x, tgt, o_a, o_b, o_m, p32, p32, p32, wmg, bmg, wba, wbb, wbm, wout, ln_g, ln_b)


def _primed_weights(w):
    w_in = w["w_in"]
    zc = lambda n: jnp.zeros((D_MODEL, n), w_in.dtype)
    w_in_p = jnp.concatenate([w_in[:, 0:384], zc(64), w_in[:, 384:416], zc(32), w_in[:, 416:]], axis=1)
    wqb = jnp.pad(w["w_q_b"].reshape(MLA_Q_LORA, MLA_HEADS, 96), ((0, 0), (0, 0), (0, 32))).reshape(MLA_Q_LORA, 1024)
    kv3 = w["w_kv_b"].reshape(MLA_KV_LORA, MLA_HEADS, 128)
    wk = jnp.pad(kv3[:, :, :MLA_NOPE], ((0, 0), (0, 0), (0, 64))).reshape(MLA_KV_LORA, 1024)
    wv = kv3[:, :, MLA_NOPE:].reshape(MLA_KV_LORA, 512)
    return w_in_p, wqb, jnp.concatenate([wk, wv], axis=1)


EARLY_NAMES = ("w_mem_kv", "w_branch_mla", "w_branch_sb", "w_branch_mem", "w_merge_gate", "w_out")
LATE_NAMES = ("w_in", "w_q_b", "w_kv_b")


def _remote(src, dst, send_sem, recv_sem, device):
    return pltpu.make_async_remote_copy(src_ref=src, dst_ref=dst, send_sem=send_sem, recv_sem=recv_sem, device_id=device,
                                        device_id_type=MESH_ID)


def _gather_rider(shards):
    n = len(shards)

    def copies(src_refs, out_refs, sems):
        send_sems, recv_sems, local_sems = sems
        x, y, c = _place()
        me = 2 * x + y
        out = []
        for a, (s, o) in enumerate(zip(src_refs, out_refs)):
            out.append(pltpu.make_async_copy(s, o.at[me], local_sems.at[a]))
            for k, (px, py) in enumerate(_other_chips(x, y)):
                out.append(_Exchange(_remote(s, o.at[me], send_sems.at[k, a], recv_sems.at[k, a], (px, py, c)),
                                     _remote(s, o.at[2 * px + py], send_sems.at[k, a], recv_sems.at[k, a], (px, py, c))))
        return out

    return _Rider(shards, [jax.ShapeDtypeStruct((N_CHIPS,) + s.shape, s.dtype) for s in shards],
                  [pltpu.SemaphoreType.DMA((3, n)), pltpu.SemaphoreType.DMA((3, n)), pltpu.SemaphoreType.DMA((n,))], copies)


def _sibling_rider(g4):
    n = len(g4)

    def copies(g_refs, out_refs, sems):
        send_sems, recv_sems = sems
        x, y, c = _place()
        out = []
        for a, (g, o) in enumerate(zip(g_refs, out_refs)):
            half = g.shape[1] // 2
            theirs = pl.ds(pl.multiple_of((1 - c) * half, 8), half)
            cp = _remote(g.at[:, theirs, :], o, send_sems.at[a], recv_sems.at[a], (x, y, 1 - c))
            out.append(_Exchange(cp, cp))
        return out

    return _Rider(g4, [jax.ShapeDtypeStruct((N_CHIPS, g.shape[1] // 2, g.shape[2]), g.dtype) for g in g4],
                  [pltpu.SemaphoreType.DMA((n,)), pltpu.SemaphoreType.DMA((n,))], copies)


def _chips_rider(wire):
    n = len(wire)

    def copies(s_refs, out_refs, sems):
        send_sems, recv_sems = sems
        x, y, c = _place()
        out = []
        for a, (s, o) in enumerate(zip(s_refs, out_refs)):
            for k, (px, py) in enumerate(_other_chips(x, y)):
                cp = _remote(s.at[2 * px + py], o.at[RELATION_XOR[k] - 1], send_sems.at[k, a], recv_sems.at[k, a], (px, py, c))
                out.append(_Exchange(cp, cp))
        return out

    return _Rider(wire, [jax.ShapeDtypeStruct((3,) + s.shape[1:], s.dtype) for s in wire],
                  [pltpu.SemaphoreType.DMA((3, n)), pltpu.SemaphoreType.DMA((3, n))], copies)


def _local_step(x, mem, tgt, w, small, *, tq, tq_sb_bwd, tk, t_row, t_mm, t_wg, rest_shards=None):
    seq = x.shape[0]
    on_mesh = rest_shards is not None
    w_in_p, wqb, wkvb = _primed_weights(w)
    tabs = _rope_tables(seq)

    p32, pbf = _matmul(x, w_in_p, mode="nn", tm=256, tn=IN_WIDTH_P, tk=D_MODEL, out_dtypes=(F32, BF16), name="proj_in")
    qp, kp, vp = _mla_prep(p32, small["q_a_gain"], small["kv_a_gain"], wqb, wkvb, tabs, t=t_row)
    res = _mla_attn_fwd(qp, kp, vp, tq=tq, tk=tk, rider=_gather_rider(rest_shards) if on_mesh else None)
    o_a, lse = res[0], res[1]
    if on_mesh:
        w = dict(w, **{n: _join_chips(n, g) for n, g in zip(EARLY_NAMES, res[2:])})
    wmg, wout = w["w_merge_gate"], w["w_out"]
    wba, wbb, wbm = w["w_branch_mla"], w["w_branch_sb"], w["w_branch_mem"]
    o_b, keep_total = _sb_attn_fwd(pbf, tq=tq, tk=tk)
    (mkv,) = _matmul(mem, w["w_mem_kv"], mode="nn", tm=MEM_LEN, tn=512, tk=D_MODEL, out_dtypes=(BF16,), name="mem_kv")
    o_m = _mem_fwd(pbf, mkv, t=t_row)

    (du, merged, dgpre, h_a, h_b, h_m, dy_a, dy_b, dy_m, do_a, do_b, do_m, dgate_a, dgate_b, dgate_m,
     d_ln_g, d_ln_b, d_bmg, loss) = _mid(x, tgt, o_a, o_b, o_m, p32, wmg, small["b_merge_gate"], wba, wbb, wbm, wout,
                                         small["ln_gain"], small["ln_bias"], t=t_row)

    wg = functools.partial(_matmul, mode="tn", tm=512, tn=1024, out_dtypes=(F32,))
    dq_m, dmkv = _mem_bwd(pbf, mkv, do_m, t=t_row)
    early = {"w_mem_kv": wg(mem, dmkv, tk=MEM_LEN, name="grad_w_mem_kv")[0],
             "w_branch_mla": wg(h_a, dy_a, tk=t_wg, name="grad_w_branch_a")[0],
             "w_branch_sb": wg(h_b, dy_b, tk=t_wg, name="grad_w_branch_b")[0],
             "w_branch_mem": wg(h_m, dy_m, tk=t_wg, name="grad_w_branch_m")[0],
             "w_merge_gate": wg(x, dgpre, tk=t_wg, name="grad_w_merge_gate")[0],
             "w_out": wg(merged, du, tk=t_wg, name="grad_w_out")[0]}

    if on_mesh:
        g4 = [_split_by_chip(n, early[n]) for n in EARLY_NAMES]
        res = _mla_attn_bwd(qp, kp, vp, o_a, lse, do_a, tq=tq, tk=tk, rider=_sibling_rider(g4))
        (dqp, dkp, dvp), got = res[:3], res[3:]
        chipsum, wire = _rs_add_sibling(g4, got, [BF16] * len(g4))
        res = _sb_attn_bwd(pbf, keep_total, do_b, tq=tq_sb_bwd, tk=tk, rider=_chips_rider(wire))
        (dq_b, dk_b, dv_b), parts = res[:3], res[3:]
        early = _rs_add_chips(chipsum, parts)
    else:
        dqp, dkp, dvp = _mla_attn_bwd(qp, kp, vp, o_a, lse, do_a, tq=tq, tk=tk)
        dq_b, dk_b, dv_b = _sb_attn_bwd(pbf, keep_total, do_b, tq=tq_sb_bwd, tk=tk)
    dlat, d_wqb, d_wkvb, d_gq, d_gkv = _mla_post(p32, dqp, dkp, dvp, small["q_a_gain"], small["kv_a_gain"], wqb, wkvb, tabs,
                                                 t=t_row)

    dcat = jnp.concatenate([dgpre, dlat, dgate_a, _bf(dq_b), _bf(dk_b), _bf(dv_b), dgate_b, _bf(dq_m), dgate_m], axis=1)
    wcat = jnp.concatenate([wmg, w_in_p], axis=1)
    (grad_x,) = _matmul(dcat, wcat, mode="nt", tm=t_mm, tn=D_MODEL, tk=1024, out_dtypes=(F32,), name="grad_x",
                        add=du, add_scale=DEEPNORM_ALPHA)
    (d_winp,) = wg(x, dcat, tk=t_wg, name="grad_w_in", b_block0=N_MERGE // 1024, n=IN_WIDTH_P)

    d_win = jnp.concatenate([d_winp[:, 0:384], d_winp[:, 448:480], d_winp[:, 512:]], axis=1)
    d_wq = d_wqb.reshape(MLA_Q_LORA, MLA_HEADS, 128)[:, :, :96].reshape(MLA_Q_LORA, 768)
    d_wk = d_wkvb[:, :1024].reshape(MLA_KV_LORA, MLA_HEADS, 128)[:, :, :MLA_NOPE]
    d_wv = d_wkvb[:, 1024:].reshape(MLA_KV_LORA, MLA_HEADS, MLA_V)
    d_wkv = jnp.concatenate([d_wk, d_wv], axis=2).reshape(MLA_KV_LORA, 1024)
    late = {"w_in": d_win, "w_q_b": d_wq, "w_kv_b": d_wkv}
    small_grads = {"q_a_gain": d_gq, "kv_a_gain": d_gkv, "b_merge_gate": d_bmg, "ln_gain": d_ln_g, "ln_bias": d_ln_b}
    return loss[0, 0], grad_x, late, small_grads, early


def _pack_shards(shards, small):
    flat_small = jnp.concatenate([small[n].reshape(-1) for n, _ in SMALL_SIZES])
    flat_small = jnp.pad(flat_small, (0, 8 * PACK_COLS - SMALL_TOTAL)).reshape(8, PACK_COLS)
    parts = [shards[n].reshape(-1, PACK_COLS) for n, _ in PACK_ROWS[:-1]] + [flat_small]
    return jnp.concatenate(parts, axis=0)


def _unpack_shards(pack, shapes):
    out, r0 = {}, 0
    for n, rows in PACK_ROWS[:-1]:
        out[n] = pack[r0:r0 + rows].reshape(shapes[n])
        r0 += rows
    flat = pack[r0:r0 + 8].reshape(-1)
    small, c0 = {}, 0
    for n, size in SMALL_SIZES:
        small[n] = flat[c0:c0 + size].reshape(1, size)
        c0 += size
    return out, small


def _split_by_chip(name, full):
    r, c = full.shape
    if name in COL_SHARDED:
        s = full.reshape(r, N_CHIPS, c // N_CHIPS).transpose(1, 0, 2)
    else:
        s = full.reshape(N_CHIPS, r // N_CHIPS, c)
    return s.reshape(N_CHIPS, -1, PACK_COLS)


def _join_chips(name, packed4):
    r, c = FULL_SHAPES[name]
    if name in COL_SHARDED:
        return packed4.reshape(N_CHIPS, r, c // N_CHIPS).transpose(1, 0, 2).reshape(r, c)
    return packed4.reshape(r, c)


def _place():
    x, y, c = lax.axis_index("x"), lax.axis_index("y"), lax.axis_index("c")
    return x, y, c


def _other_chips(x, y):
    return ((1 - x, y), (x, 1 - y), (1 - x, 1 - y))


def _gather_weights(wpack):
    rows = wpack.shape[0]
    chunk = rows // 4

    def body(w_ref, out_ref, wb_ref, send_sems, recv_sems, local_sem):
        x, y, c = _place()
        me = 2 * x + y
        for r in range(4):
            wb_ref[r * chunk:(r + 1) * chunk, :] = _bf(w_ref[r * chunk:(r + 1) * chunk, :])
        mine = pltpu.make_async_copy(wb_ref, out_ref.at[me], local_sem)
        mine.start()
        copies = []
        for k, (px, py) in enumerate(_other_chips(x, y)):
            cp = pltpu.make_async_remote_copy(src_ref=wb_ref, dst_ref=out_ref.at[me], send_sem=send_sems.at[k],
                                              recv_sem=recv_sems.at[k], device_id=(px, py, c), device_id_type=MESH_ID)
            cp.start()
            copies.append(cp)
        for k, (px, py) in enumerate(_other_chips(x, y)):
            pltpu.make_async_remote_copy(src_ref=wb_ref, dst_ref=out_ref.at[2 * px + py], send_sem=send_sems.at[k],
                                         recv_sem=recv_sems.at[k], device_id=(px, py, c), device_id_type=MESH_ID).wait_recv()
        for cp in copies:
            cp.wait_send()
        mine.wait()

    return pl.pallas_call(
        body, name="gather_weights",
        in_specs=[pl.BlockSpec(memory_space=pltpu.VMEM)],
        out_specs=pl.BlockSpec(memory_space=pltpu.HBM),
        out_shape=jax.ShapeDtypeStruct((N_CHIPS, rows, PACK_COLS), BF16),
        scratch_shapes=[pltpu.VMEM((rows, PACK_COLS), BF16), pltpu.SemaphoreType.DMA((3,)), pltpu.SemaphoreType.DMA((3,)),
                        pltpu.SemaphoreType.DMA],
        compiler_params=pltpu.CompilerParams(vmem_limit_bytes=VMEM_LIMIT),
    )(wpack)


def _to_sibling_half(gpack):
    def body(g_ref, out_ref, send_sems, recv_sems):
        x, y, c = _place()
        theirs = pl.ds(pl.multiple_of((1 - c) * PACK_HALF, 8), PACK_HALF)
        copies = [pltpu.make_async_remote_copy(src_ref=g_ref.at[j, theirs, :], dst_ref=out_ref.at[j],
                                               send_sem=send_sems.at[j], recv_sem=recv_sems.at[j],
                                               device_id=(x, y, 1 - c), device_id_type=MESH_ID) for j in range(N_CHIPS)]
        for cp in copies:
            cp.start()
        for cp in copies:
            cp.wait()

    return pl.pallas_call(
        body, name="rs_sibling",
        in_specs=[pl.BlockSpec(memory_space=pltpu.HBM)],
        out_specs=pl.BlockSpec(memory_space=pltpu.HBM),
        out_shape=jax.ShapeDtypeStruct((N_CHIPS, PACK_HALF, PACK_COLS), F32),
        scratch_shapes=[pltpu.SemaphoreType.DMA((N_CHIPS,)), pltpu.SemaphoreType.DMA((N_CHIPS,))],
    )(gpack)


def _add_sibling(gpack, got):
    tr = PACK_HALF // 4

    def body(c_ref, g_ref, r_ref, o_ref):
        o_ref[...] = g_ref[...] + r_ref[...]

    grid_spec = pltpu.PrefetchScalarGridSpec(
        num_scalar_prefetch=1, grid=(N_CHIPS, 4),
        in_specs=[pl.BlockSpec((1, tr, PACK_COLS), lambda j, i, c_ref: (j, c_ref[0] * 4 + i, 0)),
                  pl.BlockSpec((1, tr, PACK_COLS), lambda j, i, c_ref: (j, i, 0))],
        out_specs=pl.BlockSpec((1, tr, PACK_COLS), lambda j, i, c_ref: (j, i, 0)))
    return pl.pallas_call(
        body, name="rs_add_sibling", grid_spec=grid_spec,
        out_shape=jax.ShapeDtypeStruct((N_CHIPS, PACK_HALF, PACK_COLS), F32),
        compiler_params=_cparams(("parallel", "parallel")),
    )(lax.axis_index("c").astype(jnp.int32).reshape(1), gpack, got)


def _to_owner_chips(chipsum):
    def body(s_ref, out_ref, send_sems, recv_sems, local_sem):
        x, y, c = _place()
        me = 2 * x + y
        mine = pltpu.make_async_copy(s_ref.at[me], out_ref.at[me], local_sem)
        mine.start()
        copies = []
        for k, (px, py) in enumerate(_other_chips(x, y)):
            cp = pltpu.make_async_remote_copy(src_ref=s_ref.at[2 * px + py], dst_ref=out_ref.at[me], send_sem=send_sems.at[k],
                                              recv_sem=recv_sems.at[k], device_id=(px, py, c), device_id_type=MESH_ID)
            cp.start()
            copies.append(cp)
        for k, (px, py) in enumerate(_other_chips(x, y)):
            pltpu.make_async_remote_copy(src_ref=s_ref.at[me], dst_ref=out_ref.at[2 * px + py], send_sem=send_sems.at[k],
                                         recv_sem=recv_sems.at[k], device_id=(px, py, c), device_id_type=MESH_ID).wait_recv()
        for cp in copies:
            cp.wait_send()
        mine.wait()

    return pl.pallas_call(
        body, name="rs_chips",
        in_specs=[pl.BlockSpec(memory_space=pltpu.HBM)],
        out_specs=pl.BlockSpec(memory_space=pltpu.HBM),
        out_shape=jax.ShapeDtypeStruct(chipsum.shape, F32),
        scratch_shapes=[pltpu.SemaphoreType.DMA((3,)), pltpu.SemaphoreType.DMA((3,)), pltpu.SemaphoreType.DMA],
    )(chipsum)


def _add_chips(parts):
    tr = PACK_HALF // 4

    def body(p_ref, o_ref):
        o_ref[...] = ((p_ref[0] + p_ref[1]) + p_ref[2]) + p_ref[3]

    return pl.pallas_call(
        body, name="rs_add_chips", grid=(4,),
        in_specs=[pl.BlockSpec((N_CHIPS, tr, PACK_COLS), lambda i: (0, i, 0))],
        out_specs=pl.BlockSpec((tr, PACK_COLS), lambda i: (i, 0)),
        out_shape=jax.ShapeDtypeStruct((PACK_HALF, PACK_COLS), F32),
        compiler_params=_cparams(("parallel",)),
    )(parts)


def _swap_halves(half):
    def body(h_ref, out_ref, send_sem, recv_sem, local_sem):
        x, y, c = _place()
        my_rows = pl.ds(pl.multiple_of(c * PACK_HALF, 8), PACK_HALF)
        mine = pltpu.make_async_copy(h_ref, out_ref.at[my_rows, :], local_sem)
        mine.start()
        cp = pltpu.make_async_remote_copy(src_ref=h_ref, dst_ref=out_ref.at[my_rows, :], send_sem=send_sem, recv_sem=recv_sem,
                                          device_id=(x, y, 1 - c), device_id_type=MESH_ID)
        cp.start()
        their_rows = pl.ds(pl.multiple_of((1 - c) * PACK_HALF, 8), PACK_HALF)
        pltpu.make_async_remote_copy(src_ref=h_ref, dst_ref=out_ref.at[their_rows, :], send_sem=send_sem, recv_sem=recv_sem,
                                     device_id=(x, y, 1 - c), device_id_type=MESH_ID).wait_recv()
        cp.wait_send()
        mine.wait()

    return pl.pallas_call(
        body, name="rs_swap_halves",
        in_specs=[pl.BlockSpec(memory_space=pltpu.HBM)],
        out_specs=pl.BlockSpec(memory_space=pltpu.HBM),
        out_shape=jax.ShapeDtypeStruct((PACK_TOTAL, PACK_COLS), F32),
        scratch_shapes=[pltpu.SemaphoreType.DMA, pltpu.SemaphoreType.DMA, pltpu.SemaphoreType.DMA],
    )(half)


def _adamw(w, g, m, v):
    tr = PACK_TOTAL // 8

    def body(w_ref, g_ref, m_ref, v_ref, d_ref, nm_ref, nv_ref):
        gv = g_ref[...]
        m_new = ADAM_B1 * m_ref[...] + (1.0 - ADAM_B1) * gv
        v_new = ADAM_B2 * v_ref[...] + (1.0 - ADAM_B2) * (gv * gv)
        m_hat = m_new / (1.0 - ADAM_B1 ** ADAM_STEP)
        v_hat = v_new / (1.0 - ADAM_B2 ** ADAM_STEP)
        d_ref[...] = -ADAM_LR * (m_hat / (jnp.sqrt(v_hat) + ADAM_EPS) + ADAM_WD * w_ref[...])
        nm_ref[...] = m_new
        nv_ref[...] = v_new

    blk = pl.BlockSpec((tr, PACK_COLS), lambda i: (i, 0))
    return pl.pallas_call(
        body, name="adamw", grid=(8,),
        in_specs=[blk] * 4, out_specs=[blk] * 3,
        out_shape=[jax.ShapeDtypeStruct((PACK_TOTAL, PACK_COLS), F32)] * 3,
        compiler_params=_cparams(("parallel",)),
    )(w, g, m, v)


SMALL_ROWS = 64
ADAM_STEPS_PER_HALF = 4


def _pack_small(d):
    flat = jnp.concatenate([d[n].reshape(-1) for n, _ in SMALL_SIZES])
    return jnp.pad(flat, (0, SMALL_ROWS * PACK_COLS - SMALL_TOTAL)).reshape(SMALL_ROWS, PACK_COLS)


def _unpack_small(a):
    flat, out, c0 = a.reshape(-1), {}, 0
    for n, size in SMALL_SIZES:
        out[n] = flat[c0:c0 + size].reshape(1, size)
        c0 += size
    return out


def _split_by_chip(name, full):
    r, c = full.shape
    if name in COL_SHARDED:
        return full.reshape(r, N_CHIPS, c // N_CHIPS).transpose(1, 0, 2)
    return full.reshape(N_CHIPS, r // N_CHIPS, c)


def _join_chips(name, slots):
    _, r, cs = slots.shape
    if name in COL_SHARDED:
        return slots.transpose(1, 0, 2).reshape(r, N_CHIPS * cs)
    return slots.reshape(N_CHIPS * r, cs)


HBM_SPEC = pl.BlockSpec(memory_space=pltpu.HBM)


def _gather_shards(shards):
    n = len(shards)

    def body(*refs):
        w_refs, out_refs, wb_refs = refs[:n], refs[n:2 * n], refs[2 * n:3 * n]
        send_sems, recv_sems, pass_send_sems, pass_recv_sems, local_sems = refs[3 * n:]
        x, y, c = _place()
        me = 2 * x + y
        sibling = (x, y, 1 - c)

        def halves(ref):
            half = ref.shape[-2] // 2
            return (pl.ds(pl.multiple_of(c * half, 16), half), pl.ds(pl.multiple_of((1 - c) * half, 16), half))
        for w_ref, wb_ref in zip(w_refs, wb_refs):
            rows = w_ref.shape[0]
            chunk = min(rows, 128)

            def cast(i, carry, w_ref=w_ref, wb_ref=wb_ref, chunk=chunk):
                r0 = pl.multiple_of(i * chunk, chunk)
                wb_ref[pl.ds(r0, chunk), :] = _bf(w_ref[pl.ds(r0, chunk), :])
                return carry

            lax.fori_loop(0, rows // chunk, cast, 0)
        sends, locals_ = [], []
        for a, (wb_ref, out_ref) in enumerate(zip(wb_refs, out_refs)):
            mine = pltpu.make_async_copy(wb_ref, out_ref.at[me], local_sems.at[a])
            mine.start()
            locals_.append(mine)
            mine_rows, _ = halves(wb_ref)
            for k, (px, py) in enumerate(_other_chips(x, y)):
                cp = pltpu.make_async_remote_copy(src_ref=wb_ref.at[mine_rows, :], dst_ref=out_ref.at[me, mine_rows, :],
                                                  send_sem=send_sems.at[k, a], recv_sem=recv_sems.at[k, a],
                                                  device_id=(px, py, c), device_id_type=MESH_ID)
                cp.start()
                sends.append(cp)
        for a, (wb_ref, out_ref) in enumerate(zip(wb_refs, out_refs)):
            mine_rows, _ = halves(wb_ref)
            for k, (px, py) in enumerate(_other_chips(x, y)):
                landed = out_ref.at[2 * px + py, mine_rows, :]
                pltpu.make_async_remote_copy(src_ref=wb_ref.at[mine_rows, :], dst_ref=landed, send_sem=send_sems.at[k, a],
                                             recv_sem=recv_sems.at[k, a], device_id=(px, py, c),
                                             device_id_type=MESH_ID).wait_recv()
                cp = pltpu.make_async_remote_copy(src_ref=landed, dst_ref=landed, send_sem=pass_send_sems.at[k, a],
                                                  recv_sem=pass_recv_sems.at[k, a], device_id=sibling, device_id_type=MESH_ID)
                cp.start()
                sends.append(cp)
        for a, (wb_ref, out_ref) in enumerate(zip(wb_refs, out_refs)):
            _, their_rows = halves(wb_ref)
            for k, (px, py) in enumerate(_other_chips(x, y)):
                passed = out_ref.at[2 * px + py, their_rows, :]
                pltpu.make_async_remote_copy(src_ref=passed, dst_ref=passed, send_sem=pass_send_sems.at[k, a],
                                             recv_sem=pass_recv_sems.at[k, a], device_id=sibling,
                                             device_id_type=MESH_ID).wait_recv()
        for cp in sends:
            cp.wait_send()
        for cp in locals_:
            cp.wait()

    return pl.pallas_call(
        body, name="gather_weights",
        in_specs=[pl.BlockSpec(memory_space=pltpu.VMEM)] * n,
        out_specs=[HBM_SPEC] * n,
        out_shape=[jax.ShapeDtypeStruct((N_CHIPS,) + s.shape, BF16) for s in shards],
        scratch_shapes=[pltpu.VMEM(s.shape, BF16) for s in shards]
        + [pltpu.SemaphoreType.DMA((3, n))] * 4 + [pltpu.SemaphoreType.DMA((n,))],
        compiler_params=pltpu.CompilerParams(vmem_limit_bytes=VMEM_LIMIT),
    )(*shards)


def _cast_bf16_list(arrays):
    def body(*refs):
        for a_ref, o_ref in zip(refs[:len(arrays)], refs[len(arrays):]):
            o_ref[...] = _bf(a_ref[...])

    specs = [pl.BlockSpec((a.shape[0] // 4, a.shape[1]), lambda i: (i, 0)) for a in arrays]
    return pl.pallas_call(
        body, name="cast_shards", grid=(4,), in_specs=specs, out_specs=specs,
        out_shape=[jax.ShapeDtypeStruct(a.shape, BF16) for a in arrays],
        compiler_params=_cparams(("parallel",)),
    )(*arrays)


def _rs_to_sibling(g4):
    n = len(g4)

    def body(*refs):
        g_refs, out_refs = refs[:n], refs[n:2 * n]
        send_sems, recv_sems = refs[2 * n:]
        x, y, c = _place()
        copies = []
        for a, (g_ref, out_ref) in enumerate(zip(g_refs, out_refs)):
            half = g_ref.shape[1] // 2
            theirs = pl.ds(pl.multiple_of((1 - c) * half, 8), half)
            copies.append(pltpu.make_async_remote_copy(src_ref=g_ref.at[:, theirs, :], dst_ref=out_ref, send_sem=send_sems.at[a],
                                                       recv_sem=recv_sems.at[a], device_id=(x, y, 1 - c),
                                                       device_id_type=MESH_ID))
        for cp in copies:
            cp.start()
        for cp in copies:
            cp.wait()

    return pl.pallas_call(
        body, name="rs_sibling", in_specs=[HBM_SPEC] * n, out_specs=[HBM_SPEC] * n,
        out_shape=[jax.ShapeDtypeStruct((N_CHIPS, g.shape[1] // 2, g.shape[2]), F32) for g in g4],
        scratch_shapes=[pltpu.SemaphoreType.DMA((n,)), pltpu.SemaphoreType.DMA((n,))],
    )(*g4)


def _rs_add_sibling(g4, got, wire_dtypes):
    n = len(g4)
    narrow = [a for a in range(n) if wire_dtypes[a] != F32]

    def body(c_ref, *refs):
        outs = refs[2 * n:3 * n]
        wires = dict(zip(narrow, refs[3 * n:]))
        for a, (g_ref, r_ref, o_ref) in enumerate(zip(refs[:n], refs[n:2 * n], outs)):
            s = g_ref[...] + r_ref[...]
            o_ref[...] = s
            if a in wires:
                wires[a][...] = s.astype(wires[a].dtype)

    blk = lambda r: (1, r.shape[1], r.shape[2])
    plain = lambda r: pl.BlockSpec(blk(r), lambda j, c_ref: (j, 0, 0))
    grid_spec = pltpu.PrefetchScalarGridSpec(
        num_scalar_prefetch=1, grid=(N_CHIPS,),
        in_specs=[pl.BlockSpec(blk(r), lambda j, c_ref: (j, c_ref[0], 0)) for r in got] + [plain(r) for r in got],
        out_specs=[plain(r) for r in got] + [plain(got[a]) for a in narrow])
    res = pl.pallas_call(
        body, name="rs_add_sibling", grid_spec=grid_spec,
        out_shape=[jax.ShapeDtypeStruct(r.shape, F32) for r in got]
        + [jax.ShapeDtypeStruct(got[a].shape, wire_dtypes[a]) for a in narrow],
        compiler_params=_cparams(("parallel",)),
    )(lax.axis_index("c").astype(jnp.int32).reshape(1), *g4, *got)
    chipsum = list(res[:n])
    wire = list(chipsum)
    for a, w in zip(narrow, res[n:]):
        wire[a] = w
    return chipsum, wire


RELATION_XOR = (2, 1, 3)


def _rs_to_chips(wire):
    n = len(wire)

    def body(*refs):
        s_refs, out_refs = refs[:n], refs[n:2 * n]
        send_sems, recv_sems = refs[2 * n:]
        x, y, c = _place()
        sends = []
        for a, (s_ref, out_ref) in enumerate(zip(s_refs, out_refs)):
            for k, (px, py) in enumerate(_other_chips(x, y)):
                cp = pltpu.make_async_remote_copy(src_ref=s_ref.at[2 * px + py], dst_ref=out_ref.at[RELATION_XOR[k] - 1],
                                                  send_sem=send_sems.at[k, a], recv_sem=recv_sems.at[k, a],
                                                  device_id=(px, py, c), device_id_type=MESH_ID)
                cp.start()
                sends.append(cp)
        for cp in sends:
            cp.wait_recv()
        for cp in sends:
            cp.wait_send()

    return pl.pallas_call(
        body, name="rs_chips", in_specs=[HBM_SPEC] * n, out_specs=[HBM_SPEC] * n,
        out_shape=[jax.ShapeDtypeStruct((3,) + s.shape[1:], s.dtype) for s in wire],
        scratch_shapes=[pltpu.SemaphoreType.DMA((3, n)), pltpu.SemaphoreType.DMA((3, n))],
    )(*wire)


def _rs_add_chips(chipsum, parts):
    n = len(parts)

    def body(me_ref, *refs):
        me = me_ref[0]
        for s_ref, p_ref, o_ref in zip(refs[:n], refs[n:2 * n], refs[2 * n:]):
            own = s_ref[0]
            total = None
            for k in range(N_CHIPS):
                theirs = p_ref[jnp.maximum(jnp.bitwise_xor(me, k) - 1, 0)].astype(F32)
                term = jnp.where(me == k, own, theirs)
                total = term if total is None else total + term
            o_ref[...] = total

    grid_spec = pltpu.PrefetchScalarGridSpec(
        num_scalar_prefetch=1, grid=(2,),
        in_specs=[pl.BlockSpec((1, p.shape[1] // 2, p.shape[2]), lambda i, me_ref: (me_ref[0], i, 0)) for p in parts]
        + [pl.BlockSpec((3, p.shape[1] // 2, p.shape[2]), lambda i, me_ref: (0, i, 0)) for p in parts],
        out_specs=[pl.BlockSpec((p.shape[1] // 2, p.shape[2]), lambda i, me_ref: (i, 0)) for p in parts])
    me = (2 * lax.axis_index("x") + lax.axis_index("y")).astype(jnp.int32).reshape(1)
    return pl.pallas_call(
        body, name="rs_add_chips", grid_spec=grid_spec,
        out_shape=[jax.ShapeDtypeStruct(p.shape[1:], F32) for p in parts],
        compiler_params=_cparams(("parallel",)),
    )(me, *chipsum, *parts)


def _rs_swap_halves(halves):
    n = len(halves)

    def body(*refs):
        h_refs, out_refs = refs[:n], refs[n:2 * n]
        send_sems, recv_sems = refs[2 * n:]
        x, y, c = _place()
        copies = [pltpu.make_async_remote_copy(src_ref=h_ref, dst_ref=out_ref, send_sem=send_sems.at[a], recv_sem=recv_sems.at[a],
                                               device_id=(x, y, 1 - c), device_id_type=MESH_ID)
                  for a, (h_ref, out_ref) in enumerate(zip(h_refs, out_refs))]
        for cp in copies:
            cp.start()
        for cp in copies:
            cp.wait()

    return pl.pallas_call(
        body, name="rs_swap_halves", in_specs=[HBM_SPEC] * n, out_specs=[HBM_SPEC] * n,
        out_shape=[jax.ShapeDtypeStruct(h.shape, F32) for h in halves],
        scratch_shapes=[pltpu.SemaphoreType.DMA((n,)), pltpu.SemaphoreType.DMA((n,))],
    )(*halves)


def _adamw_list(ws, g_mine, g_theirs, ms, vs):
    n = len(ws)

    def body(c_ref, *refs):
        w_refs, gm_refs, gt_refs, m_refs, v_refs = (refs[k * n:(k + 1) * n] for k in range(5))
        g_refs, d_refs, nm_refs, nv_refs = (refs[k * n:(k + 1) * n] for k in range(5, 9))
        mine = (pl.program_id(0) // ADAM_STEPS_PER_HALF) == c_ref[0]
        for a in range(n):
            gv = jnp.where(mine, gm_refs[a][...], gt_refs[a][...])
            g_refs[a][...] = gv
            m_new = ADAM_B1 * m_refs[a][...] + (1.0 - ADAM_B1) * gv
            v_new = ADAM_B2 * v_refs[a][...] + (1.0 - ADAM_B2) * (gv * gv)
            m_hat = m_new / (1.0 - ADAM_B1 ** ADAM_STEP)
            v_hat = v_new / (1.0 - ADAM_B2 ** ADAM_STEP)
            d_refs[a][...] = -ADAM_LR * (m_hat / (jnp.sqrt(v_hat) + ADAM_EPS) + ADAM_WD * w_refs[a][...])
            nm_refs[a][...] = m_new
            nv_refs[a][...] = v_new

    steps = 2 * ADAM_STEPS_PER_HALF
    whole = [pl.BlockSpec((w.shape[0] // steps, w.shape[1]), lambda i, c_ref: (i, 0)) for w in ws]
    half = [pl.BlockSpec((w.shape[0] // steps, w.shape[1]), lambda i, c_ref: (i % ADAM_STEPS_PER_HALF, 0)) for w in ws]
    shapes = [jax.ShapeDtypeStruct(w.shape, F32) for w in ws]
    grid_spec = pltpu.PrefetchScalarGridSpec(num_scalar_prefetch=1, grid=(steps,),
                                             in_specs=whole + half + half + whole + whole, out_specs=whole * 4)
    res = pl.pallas_call(
        body, name="adamw", grid_spec=grid_spec, out_shape=shapes * 4,
        compiler_params=_cparams(("parallel",)),
    )(lax.axis_index("c").astype(jnp.int32).reshape(1), *ws, *g_mine, *g_theirs, *ms, *vs)
    return res[:n], res[n:2 * n], res[2 * n:3 * n], res[3 * n:]


WEIGHT_NAMES = ("w_in", "w_mem_kv", "q_a_gain", "w_q_b", "kv_a_gain", "w_kv_b", "w_branch_mla", "w_branch_sb",
                "w_branch_mem", "w_merge_gate", "b_merge_gate", "w_out", "ln_gain", "ln_bias")
BIG_NAMES = tuple(n for n, _ in PACK_ROWS[:-1])
SMALL_NAMES = tuple(n for n, _ in SMALL_SIZES)


def kernel(x, mem, w_in, w_mem_kv, q_a_gain, w_q_b, kv_a_gain, w_kv_b, w_branch_mla, w_branch_sb, w_branch_mem, w_merge_gate, b_merge_gate, w_out, ln_gain, ln_bias, loss_target, m_w_in, m_w_mem_kv, m_q_a_gain, m_w_q_b, m_kv_a_gain, m_w_kv_b, m_w_branch_mla, m_w_branch_sb, m_w_branch_mem, m_w_merge_gate, m_b_merge_gate, m_w_out, m_ln_gain, m_ln_bias, v_w_in, v_w_mem_kv, v_q_a_gain, v_w_q_b, v_kv_a_gain, v_w_kv_b, v_w_branch_mla, v_w_branch_sb, v_w_branch_mem, v_w_merge_gate, v_b_merge_gate, v_w_out, v_ln_gain, v_ln_bias):
    weights = dict(zip(WEIGHT_NAMES, (w_in, w_mem_kv, q_a_gain, w_q_b, kv_a_gain, w_kv_b, w_branch_mla, w_branch_sb,
                                      w_branch_mem, w_merge_gate, b_merge_gate, w_out, ln_gain, ln_bias)))
    mom1 = dict(zip(WEIGHT_NAMES, (m_w_in, m_w_mem_kv, m_q_a_gain, m_w_q_b, m_kv_a_gain, m_w_kv_b, m_w_branch_mla,
                                   m_w_branch_sb, m_w_branch_mem, m_w_merge_gate, m_b_merge_gate, m_w_out, m_ln_gain,
                                   m_ln_bias)))
    mom2 = dict(zip(WEIGHT_NAMES, (v_w_in, v_w_mem_kv, v_q_a_gain, v_w_q_b, v_kv_a_gain, v_w_kv_b, v_w_branch_mla,
                                   v_w_branch_sb, v_w_branch_mem, v_w_merge_gate, v_b_merge_gate, v_w_out, v_ln_gain,
                                   v_ln_bias)))
    def as_list(d):
        return [d[n][0] for n in BIG_NAMES] + [_pack_small({n: d[n] for n in SMALL_NAMES})]

    w_list, m_list, v_list = as_list(weights), as_list(mom1), as_list(mom2)

    gathered = _gather_shards([weights[n][0] for n in LATE_NAMES])
    first_w = {n: _join_chips(n, g) for n, g in zip(LATE_NAMES, gathered)}
    rest_shards = _cast_bf16_list([weights[n][0] for n in EARLY_NAMES])
    small = {n: weights[n] for n in SMALL_NAMES}

    seq = x.shape[1]
    loss, grad_x, late, small_grads, early_mine = _local_step(
        x[0], mem[0], loss_target[0], first_w, small, tq=min(1024, seq), tq_sb_bwd=512, tk=256, t_row=256, t_mm=512,
        t_wg=min(2048, seq), rest_shards=rest_shards)

    g4 = [_split_by_chip(n, late[n]) for n in LATE_NAMES]
    g4.append(jnp.broadcast_to(_pack_small(small_grads)[None], (N_CHIPS, SMALL_ROWS, PACK_COLS)))
    got = _rs_to_sibling(g4)
    chipsum, wire = _rs_add_sibling(g4, got, [BF16] * len(LATE_NAMES) + [F32])
    parts = _rs_to_chips(wire)
    late_mine = _rs_add_chips(chipsum, parts)
    by_name = dict(zip(EARLY_NAMES + LATE_NAMES + ("small",), list(early_mine) + list(late_mine)))
    mine = [by_name[n] for n in BIG_NAMES + ("small",)]
    theirs = _rs_swap_halves(mine)
    g_list, d_list, nm_list, nv_list = _adamw_list(w_list, mine, theirs, m_list, v_list)

    loss = lax.psum(loss, ("x", "y", "c"))
    outs = [loss, grad_x[None]]
    for arrays in (g_list, d_list, nm_list, nv_list):
        big = dict(zip(BIG_NAMES, arrays[:-1]))
        sm = _unpack_small(arrays[-1])
        outs.extend(big[n][None] if n in big else sm[n] for n in WEIGHT_NAMES)
    return tuple(outs)
```

```python
import functools
import math

import numpy as np
import jax
import jax.numpy as jnp
from jax import lax
from jax.experimental import pallas as pl
from jax.experimental.pallas import tpu as pltpu

F32 = jnp.float32
BF16 = jnp.bfloat16
MESH_ID = pl.DeviceIdType.MESH

D_MODEL = 1024
MEM_LEN = 256
MLA_HEADS = 8
MLA_NOPE = 64
MLA_ROPE = 32
MLA_V = 64
MLA_Q_LORA = 256
MLA_KV_LORA = 128
SB_HEADS = 8
SB_HEAD_DIM = 64
MEM_HEADS = 4
MEM_HEAD_DIM = 128
BRANCH_WIDTH = 512
ROPE_BASE = 10000.0
RMS_EPS = 1e-6
LN_EPS = 1e-5
DEEPNORM_ALPHA = 2.0 ** 0.25
MLA_SCALE = 1.0 / math.sqrt(MLA_NOPE + MLA_ROPE)
SB_SCALE = 1.0 / math.sqrt(SB_HEAD_DIM)
MEM_SCALE = 1.0 / math.sqrt(MEM_HEAD_DIM)

ADAM_LR = 0.001
ADAM_B1 = 0.9
ADAM_B2 = 0.999
ADAM_EPS = 1e-08
ADAM_WD = 0.01
ADAM_STEP = 10

LANES = 128
HALF = 64
N_CHIPS = 4
PACK_COLS = 1024
VMEM_LIMIT = 56 * 1024 * 1024

IN_WIDTH_P = 4096
BLK_LAT, BLK_GATE_A, BLK_QB, BLK_KB, BLK_VB, BLK_GATE_B, BLK_QM, BLK_GATE_M = range(8)
N_MERGE = 3 * D_MODEL
CAT_WIDTH = N_MERGE + IN_WIDTH_P

PACK_ROWS = (("w_in", 1000), ("w_mem_kv", 256), ("w_q_b", 48), ("w_kv_b", 32), ("w_branch_mla", 128),
             ("w_branch_sb", 128), ("w_branch_mem", 128), ("w_merge_gate", 768), ("w_out", 256), ("small", 8))
PACK_TOTAL = sum(r for _, r in PACK_ROWS)
PACK_HALF = PACK_TOTAL // 2
SMALL_SIZES = (("q_a_gain", 256), ("kv_a_gain", 128), ("b_merge_gate", 3072), ("ln_gain", 1024), ("ln_bias", 1024))
SMALL_TOTAL = sum(s for _, s in SMALL_SIZES)
COL_SHARDED = ("w_in", "w_q_b", "w_kv_b", "w_branch_mla", "w_branch_sb", "w_branch_mem", "w_merge_gate")
ROW_SHARDED = ("w_mem_kv", "w_out")
FULL_SHAPES = {"w_in": (1024, 4000), "w_mem_kv": (1024, 1024), "w_q_b": (256, 768), "w_kv_b": (128, 1024),
               "w_branch_mla": (512, 1024), "w_branch_sb": (512, 1024), "w_branch_mem": (512, 1024),
               "w_merge_gate": (1024, 3072), "w_out": (1024, 1024)}


def _cparams(sem=None):
    return pltpu.CompilerParams(dimension_semantics=sem, vmem_limit_bytes=VMEM_LIMIT)


def _dot(a, b):
    return jnp.dot(a, b, preferred_element_type=F32)


def _dot_nt(a, b):
    return lax.dot_general(a, b, (((1,), (1,)), ((), ())), preferred_element_type=F32)


def _dot_tn(a, b):
    return lax.dot_general(a, b, (((0,), (0,)), ((), ())), preferred_element_type=F32)


def _bf(x):
    return x.astype(BF16)


def _sigmoid(x):
    return 1.0 / (1.0 + jnp.exp(-x))


def _matmul(a, b, *, mode, tm, tn, tk, out_dtypes, name, add=None, add_scale=1.0, b_block0=0, n=None):
    if mode == "nn":
        (m, k), n = a.shape, b.shape[1]
        a_spec = pl.BlockSpec((tm, tk), lambda i, j, kk: (i, kk))
        b_spec = pl.BlockSpec((tk, tn), lambda i, j, kk: (kk, j))
        dot = _dot
    elif mode == "nt":
        (m, k), n = a.shape, b.shape[0]
        a_spec = pl.BlockSpec((tm, tk), lambda i, j, kk: (i, kk))
        b_spec = pl.BlockSpec((tn, tk), lambda i, j, kk: (j, kk))
        dot = _dot_nt
    else:
        (k, m), n = a.shape, (b.shape[1] if n is None else n)
        a_spec = pl.BlockSpec((tk, tm), lambda i, j, kk: (kk, i))
        b_spec = pl.BlockSpec((tk, tn), lambda i, j, kk: (kk, j + b_block0))
        dot = _dot_tn
    assert m % tm == 0 and n % tn == 0 and k % tk == 0, (name, m, n, k)
    nk = k // tk
    n_out = len(out_dtypes)
    has_add = add is not None

    def body(*refs):
        a_ref, b_ref = refs[0], refs[1]
        add_ref = refs[2] if has_add else None
        outs = refs[2 + has_add: 2 + has_add + n_out]
        acc = refs[-1]
        kk = pl.program_id(2)

        @pl.when(kk == 0)
        def _():
            acc[...] = jnp.zeros_like(acc)

        acc[...] += dot(_bf(a_ref[...]), _bf(b_ref[...]))

        @pl.when(kk == nk - 1)
        def _():
            r = acc[...]
            if has_add:
                r = r + add_scale * add_ref[...]
            for o in outs:
                o[...] = r.astype(o.dtype)

    in_specs = [a_spec, b_spec]
    args = [a, b]
    if has_add:
        in_specs.append(pl.BlockSpec((tm, tn), lambda i, j, kk: (i, j)))
        args.append(add)
    out_spec = pl.BlockSpec((tm, tn), lambda i, j, kk: (i, j))
    res = pl.pallas_call(
        body, name=name, grid=(m // tm, n // tn, nk),
        in_specs=in_specs, out_specs=[out_spec] * n_out,
        out_shape=[jax.ShapeDtypeStruct((m, n), dt) for dt in out_dtypes],
        scratch_shapes=[pltpu.VMEM((tm, tn), F32)],
        compiler_params=_cparams(("parallel", "parallel", "arbitrary")),
    )(*args)
    return res


def _rope_tables(seq):
    half = MLA_ROPE // 2
    freqs = ROPE_BASE ** (-jnp.arange(half, dtype=F32) / half)
    ang = jnp.arange(seq, dtype=jnp.int32).astype(F32)[:, None] * freqs[None, :]
    cos, sin = jnp.cos(ang), jnp.sin(ang)
    z = lambda w: jnp.zeros((seq, w), F32)
    c_q = jnp.concatenate([jnp.ones((seq, MLA_NOPE), F32), cos, cos, z(32)], axis=1)
    c_k = jnp.concatenate([z(MLA_NOPE), cos, cos, z(32)], axis=1)
    s_lo = jnp.concatenate([z(MLA_NOPE), -sin, z(half), z(32)], axis=1)
    s_hi = jnp.concatenate([z(MLA_NOPE), z(half), sin, z(32)], axis=1)
    return c_q, c_k, s_lo, s_hi


def _rope_fwd(x, c, s_lo, s_hi):
    return x * c + pltpu.roll(x, LANES - 16, 1) * s_lo + pltpu.roll(x, 16, 1) * s_hi


def _rope_bwd(d, c, s_lo, s_hi):
    return d * c - pltpu.roll(d, 16, 1) * s_hi - pltpu.roll(d, LANES - 16, 1) * s_lo


def _rms_fwd(x, g):
    r = lax.rsqrt(jnp.mean(x * x, axis=-1, keepdims=True) + RMS_EPS)
    xn = x * r
    return xn * g, xn, r


def _mla_prep(p32, gq, gkv, wqb, wkvb, tabs, *, t):
    seq = p32.shape[0]

    def body(lat_ref, gq_ref, gkv_ref, wqb_ref, wkvb_ref, cq_ref, ck_ref, slo_ref, shi_ref, q_ref, k_ref, v_ref):
        lat = lat_ref[...]
        slo, shi = slo_ref[...], shi_ref[...]
        nq, _, _ = _rms_fwd(lat[:, 0:MLA_Q_LORA], gq_ref[...])
        qa = _dot(_bf(nq), wqb_ref[...])
        cq = cq_ref[...]
        for h in range(MLA_HEADS):
            blk = qa[:, h * LANES:(h + 1) * LANES]
            q_ref[:, h * LANES:(h + 1) * LANES] = _bf(_rope_fwd(blk, cq, slo, shi))
        nkv, _, _ = _rms_fwd(lat[:, MLA_Q_LORA:MLA_Q_LORA + MLA_KV_LORA], gkv_ref[...])
        kv = _dot(_bf(nkv), wkvb_ref[...])
        kpe = _rope_fwd(lat[:, 384:512], ck_ref[...], slo, shi)
        for h in range(MLA_HEADS):
            k_ref[:, h * LANES:(h + 1) * LANES] = _bf(kv[:, h * LANES:(h + 1) * LANES] + kpe)
        v_ref[...] = _bf(kv[:, MLA_HEADS * LANES:])

    row = lambda w: pl.BlockSpec((t, w), lambda i: (i, 0))
    full = lambda shp: pl.BlockSpec(shp, lambda i: (0, 0))
    return pl.pallas_call(
        body, name="mla_prep", grid=(seq // t,),
        in_specs=[row(512), full((1, MLA_Q_LORA)), full((1, MLA_KV_LORA)), full(wqb.shape), full(wkvb.shape),
                  row(LANES), row(LANES), row(LANES), row(LANES)],
        out_specs=[row(1024), row(1024), row(512)],
        out_shape=[jax.ShapeDtypeStruct((seq, 1024), BF16), jax.ShapeDtypeStruct((seq, 1024), BF16),
                   jax.ShapeDtypeStruct((seq, 512), BF16)],
        compiler_params=_cparams(("parallel",)),
    )(p32, gq, gkv, wqb, wkvb, *tabs)


def _mla_post(p32, dq, dk, dv, gq, gkv, wqb, wkvb, tabs, *, t):
    seq = p32.shape[0]

    def body(lat_ref, dq_ref, dk_ref, dv_ref, gq_ref, gkv_ref, wqb_ref, wkvb_ref, cq_ref, ck_ref, slo_ref, shi_ref,
             dlat_ref, dwqb_ref, dwkvb_ref, dgq_ref, dgkv_ref):
        @pl.when(pl.program_id(0) == 0)
        def _():
            dwqb_ref[...] = jnp.zeros_like(dwqb_ref)
            dwkvb_ref[...] = jnp.zeros_like(dwkvb_ref)
            dgq_ref[...] = jnp.zeros_like(dgq_ref)
            dgkv_ref[...] = jnp.zeros_like(dgkv_ref)

        lat = lat_ref[...]
        slo, shi = slo_ref[...], shi_ref[...]
        cq = cq_ref[...]
        gq_v, gkv_v = gq_ref[...], gkv_ref[...]
        nq, xq, rq = _rms_fwd(lat[:, 0:MLA_Q_LORA], gq_v)
        nkv, xkv, rkv = _rms_fwd(lat[:, MLA_Q_LORA:MLA_Q_LORA + MLA_KV_LORA], gkv_v)

        dqa = jnp.concatenate(
            [_rope_bwd(dq_ref[:, h * LANES:(h + 1) * LANES], cq, slo, shi) for h in range(MLA_HEADS)], axis=1)
        dqa_b = _bf(dqa)
        dwqb_ref[...] += _dot_tn(_bf(nq), dqa_b)
        dnq = _dot_nt(dqa_b, wqb_ref[...])
        dgq_ref[...] += jnp.sum(dnq * xq, axis=0, keepdims=True)
        dxn = dnq * gq_v
        dcq = rq * (dxn - xq * jnp.mean(dxn * xq, axis=-1, keepdims=True))

        dkf = dk_ref[...]
        dkv_b = _bf(jnp.concatenate([dkf, dv_ref[...]], axis=1))
        dwkvb_ref[...] += _dot_tn(_bf(nkv), dkv_b)
        dnkv = _dot_nt(dkv_b, wkvb_ref[...])
        dgkv_ref[...] += jnp.sum(dnkv * xkv, axis=0, keepdims=True)
        dxn = dnkv * gkv_v
        dckv = rkv * (dxn - xkv * jnp.mean(dxn * xkv, axis=-1, keepdims=True))

        dkpe = dkf[:, 0:LANES]
        for h in range(1, MLA_HEADS):
            dkpe = dkpe + dkf[:, h * LANES:(h + 1) * LANES]
        dkr = _rope_bwd(dkpe, ck_ref[...], slo, shi)
        dlat_ref[...] = _bf(jnp.concatenate([dcq, dckv, dkr], axis=1))

    row = lambda w: pl.BlockSpec((t, w), lambda i: (i, 0))
    full = lambda shp: pl.BlockSpec(shp, lambda i: (0, 0))
    return pl.pallas_call(
        body, name="mla_post", grid=(seq // t,),
        in_specs=[row(512), row(1024), row(1024), row(512), full((1, MLA_Q_LORA)), full((1, MLA_KV_LORA)),
                  full(wqb.shape), full(wkvb.shape), row(LANES), row(LANES), row(LANES), row(LANES)],
        out_specs=[row(512), full(wqb.shape), full(wkvb.shape), full((1, MLA_Q_LORA)), full((1, MLA_KV_LORA))],
        out_shape=[jax.ShapeDtypeStruct((seq, 512), BF16), jax.ShapeDtypeStruct(wqb.shape, F32),
                   jax.ShapeDtypeStruct(wkvb.shape, F32), jax.ShapeDtypeStruct((1, MLA_Q_LORA), F32),
                   jax.ShapeDtypeStruct((1, MLA_KV_LORA), F32)],
        compiler_params=_cparams(("arbitrary",)),
    )(p32, dq, dk, dv, gq, gkv, wqb, wkvb, *tabs)


def _split_bf16(x):
    hi = _bf(x)
    return hi, _bf(x - hi.astype(F32))


def _tri_sum(x, u):
    hi, lo = _split_bf16(x)
    return _dot(hi, u) + _dot(lo, u)


def _softplus(z):
    return jnp.maximum(z, 0.0) + jnp.log(1.0 + jnp.exp(-jnp.abs(z)))


def _head_queries(q, left):
    zero = jnp.zeros_like(q)
    return jnp.where(left, q, zero) * SB_SCALE, jnp.where(left, zero, q) * SB_SCALE


ROW_GROUP = 128
SB_BWD_CHAINS_IN_FLIGHT = 8
ANY_HBM = pl.BlockSpec(memory_space=pltpu.HBM)


class _Exchange:
    def __init__(self, send, landing):
        self.send, self.landing = send, landing

    def start(self):
        self.send.start()

    def wait(self):
        self.landing.wait_recv()
        self.send.wait_send()


class _Rider:
    def __init__(self, operands, out_shapes, sem_shapes, copies):
        self.operands, self.out_shapes, self.sem_shapes, self.copies = list(operands), list(out_shapes), list(sem_shapes), copies


def _call_with_rider(body, rider, *, name, grid, in_specs, out_specs, out_shape, args, semantics, scratch=()):
    scratch = list(scratch)
    if rider is None:
        return pl.pallas_call(body, name=name, grid=grid, in_specs=in_specs, out_specs=out_specs, out_shape=out_shape,
                              scratch_shapes=scratch, compiler_params=_cparams(semantics))(*args)
    n_in, n_out, n_rin, n_rout = len(in_specs), len(out_specs), len(rider.operands), len(rider.out_shapes)

    def full_body(*refs):
        ins, r_ins = refs[:n_in], refs[n_in:n_in + n_rin]
        outs = refs[n_in + n_rin:n_in + n_rin + n_out]
        r_outs = refs[n_in + n_rin + n_out:n_in + n_rin + n_out + n_rout]
        rest = refs[n_in + n_rin + n_out + n_rout:]
        own_scratch, sems = rest[:len(scratch)], rest[len(scratch):]
        p, i = pl.program_id(0), pl.program_id(1)

        @pl.when((p == 0) & (i == 0))
        def _():
            for cp in rider.copies(r_ins, r_outs, sems):
                cp.start()

        body(*ins, *outs, *own_scratch)

        @pl.when((p == grid[0] - 1) & (i == grid[1] - 1))
        def _():
            for cp in rider.copies(r_ins, r_outs, sems):
                cp.wait()

    return pl.pallas_call(
        full_body, name=name, grid=grid, in_specs=list(in_specs) + [ANY_HBM] * n_rin,
        out_specs=list(out_specs) + [ANY_HBM] * n_rout, out_shape=list(out_shape) + rider.out_shapes,
        scratch_shapes=scratch + rider.sem_shapes, compiler_params=_cparams(("arbitrary", "arbitrary")),
    )(*args, *rider.operands)


def _chains(tq):
    return [(h, g) for g in range(tq // ROW_GROUP) for h in range(2)]


def _chain_pattern(g, m, tk, strict):
    r_lo, r_hi = g * ROW_GROUP, (g + 1) * ROW_GROUP - 1
    c_lo, c_hi = m * tk, (m + 1) * tk - 1
    if (c_lo >= r_hi) if strict else (c_lo > r_hi):
        return None
    if (c_hi < r_lo) if strict else (c_hi <= r_lo):
        return True
    rr = lax.broadcasted_iota(jnp.int32, (ROW_GROUP, tk), 0) + r_lo
    cc = lax.broadcasted_iota(jnp.int32, (ROW_GROUP, tk), 1) + c_lo
    return (cc < rr) if strict else (cc <= rr)


def _masked(x, pat, fill=0.0):
    return x if pat is True else jnp.where(pat, x, fill)


def _rows(g):
    return slice(g * ROW_GROUP, (g + 1) * ROW_GROUP)


def _tri_matrix(tk, cmp):
    rr = lax.broadcasted_iota(jnp.int32, (tk, tk), 0)
    cc = lax.broadcasted_iota(jnp.int32, (tk, tk), 1)
    return cmp(rr, cc).astype(BF16)


def _mla_attn_fwd(qp, kp, vp, *, tq, tk, rider=None):
    seq = qp.shape[0]
    nd = tq // tk
    neg = float(np.finfo(np.float32).min)
    chains = _chains(tq)

    def body(q_ref, k_ref, v_ref, o_ref, lse_ref):
        i = pl.program_id(1)
        left = lax.broadcasted_iota(jnp.int32, (tq, LANES), 1) < HALF
        qs = [q_ref[_rows(g), h * LANES:(h + 1) * LANES] for h, g in chains]

        def block(j, carry, m):
            start = pl.multiple_of(j * tk, tk)
            v = v_ref[pl.ds(start, tk), :]
            pats = [True if m is None else _chain_pattern(g, m, tk, False) for _, g in chains]
            live = [n for n, p in enumerate(pats) if p is not None]
            ss = {n: _dot_nt(qs[n], k_ref[pl.ds(start, tk), chains[n][0] * LANES:(chains[n][0] + 1) * LANES]) for n in live}
            new = list(carry)
            for n in live:
                m_old, l_old, acc = carry[n]
                s = _masked(ss[n] * MLA_SCALE, pats[n], neg)
                m_new = jnp.maximum(m_old, jnp.max(s, axis=-1, keepdims=True))
                a = jnp.exp(m_old - m_new)
                p = jnp.exp(s - m_new)
                new[n] = (m_new, a * l_old + jnp.sum(p, axis=-1, keepdims=True), a * acc + _dot(_bf(p), v))
            return tuple(new)

        init = (jnp.full((ROW_GROUP, 1), -1e30, F32), jnp.zeros((ROW_GROUP, 1), F32), jnp.zeros((ROW_GROUP, LANES), F32))
        carry = lax.fori_loop(0, i * nd, lambda j, c: block(j, c, None), (init,) * len(chains))
        for m in range(nd):
            carry = block(i * nd + m, carry, m)
        per_head = []
        for h in range(2):
            mine = [carry[n] for n, (ch, _) in enumerate(chains) if ch == h]
            per_head.append((jnp.concatenate([acc / l for _, l, acc in mine], axis=0),
                             jnp.concatenate([mm + jnp.log(l) for mm, l, _ in mine], axis=0)))
        o_ref[...] = jnp.where(left, per_head[0][0], per_head[1][0])
        lse_ref[...] = jnp.where(left, per_head[0][1], per_head[1][1])

    return _call_with_rider(
        body, rider, name="mla_fwd", grid=(MLA_HEADS // 2, seq // tq),
        in_specs=[pl.BlockSpec((tq, 2 * LANES), lambda p, i: (i, p)), pl.BlockSpec((seq, 2 * LANES), lambda p, i: (0, p)),
                  pl.BlockSpec((seq, LANES), lambda p, i: (0, p))],
        out_specs=[pl.BlockSpec((tq, LANES), lambda p, i: (i, p)), pl.BlockSpec((tq, LANES), lambda p, i: (i, p))],
        out_shape=[jax.ShapeDtypeStruct((seq, 512), F32), jax.ShapeDtypeStruct((seq, 512), F32)],
        args=(qp, kp, vp), semantics=("parallel", "parallel"))


def _mla_attn_bwd(qp, kp, vp, o, lse, do, *, tq, tk, rider=None):
    seq = qp.shape[0]
    nd = tq // tk
    chains = _chains(tq)

    def body(q_ref, k_ref, v_ref, o_ref, lse_ref, do_ref, dq_ref, dk_ref, dv_ref, qt_ref, dot_ref):
        i = pl.program_id(1)

        @pl.when(i == 0)
        def _():
            dk_ref[...] = jnp.zeros_like(dk_ref)
            dv_ref[...] = jnp.zeros_like(dv_ref)

        left = lax.broadcasted_iota(jnp.int32, (tq, LANES), 1) < HALF
        do_f = do_ref[...]
        prod = do_f * o_ref[...]
        lse_v = lse_ref[...]
        do_heads = (_bf(jnp.where(left, do_f, 0.0)), _bf(jnp.where(left, 0.0, do_f)))
        delta_heads = (jnp.sum(jnp.where(left, prod, 0.0), axis=-1, keepdims=True),
                       jnp.sum(jnp.where(left, 0.0, prod), axis=-1, keepdims=True))
        qs = [q_ref[_rows(g), h * LANES:(h + 1) * LANES] for h, g in chains]
        dos = [do_heads[h][_rows(g)] for h, g in chains]
        deltas = [delta_heads[h][_rows(g)] for h, g in chains]
        lses = [lse_v[_rows(g), h * HALF:h * HALF + 1] for h, g in chains]
        for h in range(2):
            qt_ref[h] = q_ref[:, h * LANES:(h + 1) * LANES].T
            dot_ref[h] = do_heads[h].T
        q_t = [qt_ref.at[h] for h in range(2)]
        do_t = [dot_ref.at[h] for h in range(2)]

        def block(j, carry, m):
            start = pl.multiple_of(j * tk, tk)
            v = v_ref[pl.ds(start, tk), :]
            pats = [True if m is None else _chain_pattern(g, m, tk, False) for _, g in chains]
            live = [n for n, p in enumerate(pats) if p is not None]
            ks = [k_ref[pl.ds(start, tk), h * LANES:(h + 1) * LANES] for h in range(2)]
            ss = {n: _dot_nt(qs[n], ks[chains[n][0]]) for n in live}
            dps = {n: _dot_nt(dos[n], v) for n in live}
            new = list(carry)
            ps, dss = {}, {}
            for n in live:
                p = _masked(jnp.exp(ss[n] * MLA_SCALE - lses[n]), pats[n])
                ps[n] = _bf(p)
                dss[n] = _bf(p * (dps[n] - deltas[n]) * MLA_SCALE)
                new[n] = carry[n] + _dot(dss[n], ks[chains[n][0]])
            dv_t, dk_t = None, []
            for h in range(2):
                mine = [n for n in live if chains[n][0] == h]
                first_row = chains[mine[0]][1] * ROW_GROUP
                ds_cat = jnp.concatenate([dss[n] for n in mine], axis=0)
                p_cat = jnp.concatenate([ps[n] for n in mine], axis=0)
                if first_row == 0:
                    q_rows_t, do_rows_t = q_t[h][...], do_t[h][...]
                else:
                    q_rows_t = q_ref[first_row:, h * LANES:(h + 1) * LANES].T
                    do_rows_t = do_heads[h][first_row:].T
                dk_t.append(_dot(q_rows_t, ds_cat))
                term = _dot(do_rows_t, p_cat)
                dv_t = term if dv_t is None else dv_t + term
            back = jnp.concatenate(dk_t + [dv_t], axis=0).T
            dk_ref[pl.ds(start, tk), :] += back[:, :2 * LANES]
            dv_ref[pl.ds(start, tk), :] += back[:, 2 * LANES:]
            return tuple(new)

        zero = jnp.zeros((ROW_GROUP, LANES), F32)
        carry = lax.fori_loop(0, i * nd, lambda j, c: block(j, c, None), (zero,) * len(chains))
        for m in range(nd):
            carry = block(i * nd + m, carry, m)
        for n, (h, g) in enumerate(chains):
            dq_ref[_rows(g), h * LANES:(h + 1) * LANES] = carry[n]

    two_t = pl.BlockSpec((tq, 2 * LANES), lambda p, i: (i, p))
    two_s = pl.BlockSpec((seq, 2 * LANES), lambda p, i: (0, p))
    pair_t = pl.BlockSpec((tq, LANES), lambda p, i: (i, p))
    pair_s = pl.BlockSpec((seq, LANES), lambda p, i: (0, p))
    return _call_with_rider(
        body, rider, name="mla_bwd", grid=(MLA_HEADS // 2, seq // tq),
        in_specs=[two_t, two_s, pair_s, pair_t, pair_t, pair_t],
        out_specs=[two_t, two_s, pair_s],
        out_shape=[jax.ShapeDtypeStruct((seq, 1024), F32), jax.ShapeDtypeStruct((seq, 1024), F32),
                   jax.ShapeDtypeStruct((seq, 512), F32)],
        args=(qp, kp, vp, o, lse, do), semantics=("parallel", "arbitrary"),
        scratch=[pltpu.VMEM((2, LANES, tq), BF16), pltpu.VMEM((2, LANES, tq), BF16)])


def _sb_attn_fwd(pbf, *, tq, tk):
    seq = pbf.shape[0]
    nd = tq // tk
    qb, kb, vb = BLK_QB * 4, BLK_KB * 4, BLK_VB * 4
    chains = _chains(tq)

    def body(q_ref, k_ref, v_ref, o_ref, tot_ref):
        i = pl.program_id(1)
        u_later = _tri_matrix(tk, lambda r, c: r > c)
        left = lax.broadcasted_iota(jnp.int32, (tq, LANES), 1) < HALF
        q_heads = _head_queries(q_ref[...], left)
        qs = [q_heads[h][_rows(g)] for h, g in chains]

        def block(j, carry, m):
            start = pl.multiple_of(j * tk, tk)
            k = k_ref[pl.ds(start, tk), :]
            v = v_ref[pl.ds(start, tk), :]
            pats = [True if m is None else _chain_pattern(g, m, tk, True) for _, g in chains]
            live = [n for n, p in enumerate(pats) if p is not None]
            zs = {n: _dot_nt(qs[n], k) for n in live}
            raws = {n: _softplus(zs[n]) for n in live}
            sps = {n: _masked(raws[n], pats[n]) for n in live}
            laters = {n: _tri_sum(sps[n], u_later) for n in live}
            new = list(carry)
            for n in live:
                c, acc = carry[n]
                a = _masked(jnp.exp(zs[n] - raws[n] - laters[n] - c), pats[n])
                new[n] = (c + laters[n][:, 0:1] + sps[n][:, 0:1], acc + _dot(_bf(a), v))
            return tuple(new)

        init = (jnp.zeros((ROW_GROUP, 1), F32), jnp.zeros((ROW_GROUP, LANES), F32))
        carry = (init,) * len(chains)
        for m in reversed(range(nd)):
            carry = block(i * nd + m, carry, m)
        carry = lax.fori_loop(0, i * nd, lambda jj, cr: block(i * nd - 1 - jj, cr, None), carry)
        per_head = []
        for h in range(2):
            mine = [carry[n] for n, (ch, _) in enumerate(chains) if ch == h]
            per_head.append((jnp.concatenate([acc for _, acc in mine], axis=0), jnp.concatenate([c for c, _ in mine], axis=0)))
        o_ref[...] = jnp.where(left, per_head[0][0], per_head[1][0])
        tot_ref[...] = jnp.where(left, per_head[0][1], per_head[1][1])

    pair_t = pl.BlockSpec((tq, LANES), lambda p, i: (i, p))
    return pl.pallas_call(
        body, name="sb_fwd", grid=(SB_HEADS // 2, seq // tq),
        in_specs=[pl.BlockSpec((tq, LANES), lambda p, i: (i, qb + p)), pl.BlockSpec((seq, LANES), lambda p, i: (0, kb + p)),
                  pl.BlockSpec((seq, LANES), lambda p, i: (0, vb + p))],
        out_specs=[pair_t, pair_t],
        out_shape=[jax.ShapeDtypeStruct((seq, 512), F32), jax.ShapeDtypeStruct((seq, 512), F32)],
        compiler_params=_cparams(("parallel", "parallel")),
    )(pbf, pbf, pbf)


def _sb_attn_bwd(pbf, tot, do, *, tq, tk, rider=None):
    seq = pbf.shape[0]
    nd = tq // tk
    qb, kb, vb = BLK_QB * 4, BLK_KB * 4, BLK_VB * 4
    chains = _chains(tq)
    group = SB_BWD_CHAINS_IN_FLIGHT

    def body(q_ref, k_ref, v_ref, tot_ref, do_ref, dq_ref, dk_ref, dv_ref, qt_ref, dot_ref):
        i = pl.program_id(1)

        @pl.when(i == 0)
        def _():
            dk_ref[...] = jnp.zeros_like(dk_ref)
            dv_ref[...] = jnp.zeros_like(dv_ref)

        u_upto = _tri_matrix(tk, lambda r, c: r <= c)
        u_below = _tri_matrix(tk, lambda r, c: r < c)
        left = lax.broadcasted_iota(jnp.int32, (tq, LANES), 1) < HALF
        q_heads = _head_queries(q_ref[...], left)
        do_f = do_ref[...]
        do_heads = (_bf(jnp.where(left, do_f, 0.0)), _bf(jnp.where(left, 0.0, do_f)))
        tot_v = tot_ref[...]
        qs = [q_heads[h][_rows(g)] for h, g in chains]
        dos = [do_heads[h][_rows(g)] for h, g in chains]
        totals = [tot_v[_rows(g), h * HALF:h * HALF + 1] for h, g in chains]
        qt_ref[...] = jnp.concatenate(qs, axis=0).T
        dot_ref[...] = jnp.concatenate(dos, axis=0).T

        def block(j, carry, m):
            start = pl.multiple_of(j * tk, tk)
            k = k_ref[pl.ds(start, tk), :]
            v = v_ref[pl.ds(start, tk), :]
            pats = [True if m is None else _chain_pattern(g, m, tk, True) for _, g in chains]
            all_live = [n for n, p in enumerate(pats) if p is not None]
            new = list(carry)
            for g0 in range(0, len(all_live), group):
                live = all_live[g0:g0 + group]
                zs = {n: _dot_nt(qs[n], k) for n in live}
                das = {n: _dot_nt(dos[n], v) for n in live}
                raws = {n: _softplus(zs[n]) for n in live}
                sps = {n: _masked(raws[n], pats[n]) for n in live}
                uptos = {n: _tri_sum(sps[n], u_upto) for n in live}
                lbs, a_s, gs = {}, {}, {}
                for n in live:
                    lbs[n] = zs[n] - raws[n]
                    a = _masked(jnp.exp(lbs[n] - (totals[n] - carry[n][0] - uptos[n])), pats[n])
                    a_s[n] = _bf(a)
                    gs[n] = das[n] * a
                belows = {n: _dot(_bf(gs[n]), u_below) for n in live}
                dzs = {}
                for n in live:
                    sp_before, g_before, dq_acc = carry[n]
                    beta = jnp.exp(lbs[n])
                    dz = _masked(gs[n] * (1.0 - beta) - (g_before + belows[n]) * beta, pats[n])
                    dzs[n] = _bf(dz)
                    new[n] = (sp_before + uptos[n][:, tk - 1:tk], g_before + belows[n][:, tk - 1:tk] + gs[n][:, tk - 1:tk],
                              dq_acc + _dot(dzs[n], k))
                dz_cat = jnp.concatenate([dzs[n] for n in live], axis=0)
                a_cat = jnp.concatenate([a_s[n] for n in live], axis=0)
                if len(live) == len(chains):
                    q_rows_t, do_rows_t = qt_ref[...], dot_ref[...]
                else:
                    q_rows_t = jnp.concatenate([qs[n] for n in live], axis=0).T
                    do_rows_t = jnp.concatenate([dos[n] for n in live], axis=0).T
                both = jnp.concatenate([_dot(q_rows_t, dz_cat), _dot(do_rows_t, a_cat)], axis=0).T
                dk_ref[pl.ds(start, tk), :] += both[:, :LANES]
                dv_ref[pl.ds(start, tk), :] += both[:, LANES:]
            return tuple(new)

        zero = jnp.zeros((ROW_GROUP, 1), F32)
        init = (zero, zero, jnp.zeros((ROW_GROUP, LANES), F32))
        carry = lax.fori_loop(0, i * nd, lambda j, cr: block(j, cr, None), (init,) * len(chains))
        for m in range(nd):
            carry = block(i * nd + m, carry, m)
        per_head = [jnp.concatenate([carry[n][2] for n, (ch, _) in enumerate(chains) if ch == h], axis=0) for h in range(2)]
        dq_ref[...] = jnp.where(left, per_head[0], per_head[1]) * SB_SCALE

    pair_t = pl.BlockSpec((tq, LANES), lambda p, i: (i, p))
    pair_s = pl.BlockSpec((seq, LANES), lambda p, i: (0, p))
    return _call_with_rider(
        body, rider, name="sb_bwd", grid=(SB_HEADS // 2, seq // tq),
        in_specs=[pl.BlockSpec((tq, LANES), lambda p, i: (i, qb + p)), pl.BlockSpec((seq, LANES), lambda p, i: (0, kb + p)),
                  pl.BlockSpec((seq, LANES), lambda p, i: (0, vb + p)), pair_t, pair_t],
        out_specs=[pair_t, pair_s, pair_s],
        out_shape=[jax.ShapeDtypeStruct((seq, 512), F32)] * 3,
        args=(pbf, pbf, pbf, tot, do), semantics=("parallel", "arbitrary"),
        scratch=[pltpu.VMEM((LANES, 2 * tq), BF16), pltpu.VMEM((LANES, 2 * tq), BF16)])


def _mem_probs(q, k):
    s = _dot_nt(q, k) * MEM_SCALE
    e = jnp.exp(s - jnp.max(s, axis=-1, keepdims=True))
    return e / jnp.sum(e, axis=-1, keepdims=True)


def _mem_fwd(pbf, mkv, *, t):
    seq = pbf.shape[0]

    def body(q_ref, kv_ref, o_ref):
        for h in range(MEM_HEADS):
            lo, hi = h * LANES, (h + 1) * LANES
            p = _mem_probs(q_ref[:, lo:hi], kv_ref[:, lo:hi])
            o_ref[:, lo:hi] = _dot(_bf(p), kv_ref[:, 512 + lo:512 + hi])

    return pl.pallas_call(
        body, name="mem_fwd", grid=(seq // t,),
        in_specs=[pl.BlockSpec((t, 512), lambda i: (i, BLK_QM)), pl.BlockSpec((MEM_LEN, 1024), lambda i: (0, 0))],
        out_specs=pl.BlockSpec((t, 512), lambda i: (i, 0)),
        out_shape=jax.ShapeDtypeStruct((seq, 512), F32),
        compiler_params=_cparams(("parallel",)),
    )(pbf, mkv)


def _mem_bwd(pbf, mkv, do, *, t):
    seq = pbf.shape[0]

    def body(q_ref, kv_ref, do_ref, dq_ref, dkv_ref):
        @pl.when(pl.program_id(0) == 0)
        def _():
            dkv_ref[...] = jnp.zeros_like(dkv_ref)

        for h in range(MEM_HEADS):
            lo, hi = h * LANES, (h + 1) * LANES
            q, k, v = q_ref[:, lo:hi], kv_ref[:, lo:hi], kv_ref[:, 512 + lo:512 + hi]
            do_h = _bf(do_ref[:, lo:hi])
            p = _mem_probs(q, k)
            dp = _dot_nt(do_h, v)
            ds = _bf(p * (dp - jnp.sum(dp * p, axis=-1, keepdims=True)) * MEM_SCALE)
            dq_ref[:, lo:hi] = _dot(ds, k)
            dkv_ref[:, lo:hi] += _dot_tn(ds, q)
            dkv_ref[:, 512 + lo:512 + hi] += _dot_tn(_bf(p), do_h)

    return pl.pallas_call(
        body, name="mem_bwd", grid=(seq // t,),
        in_specs=[pl.BlockSpec((t, 512), lambda i: (i, BLK_QM)), pl.BlockSpec((MEM_LEN, 1024), lambda i: (0, 0)),
                  pl.BlockSpec((t, 512), lambda i: (i, 0))],
        out_specs=[pl.BlockSpec((t, 512), lambda i: (i, 0)), pl.BlockSpec((MEM_LEN, 1024), lambda i: (0, 0))],
        out_shape=[jax.ShapeDtypeStruct((seq, 512), F32), jax.ShapeDtypeStruct((MEM_LEN, 1024), F32)],
        compiler_params=_cparams(("arbitrary",)),
    )(pbf, mkv, do)


def _mid(x, tgt, o_a, o_b, o_m, p32, wmg, bmg, wba, wbb, wbm, wout, ln_g, ln_b, *, t):
    seq = x.shape[0]
    inv_d = 1.0 / D_MODEL

    def body(x_ref, t_ref, oa_ref, ob_ref, om_ref, ga_ref, gb_ref, gm_ref, wmg_ref, bmg_ref, wba_ref, wbb_ref,
             wbm_ref, wout_ref, lg_ref, lb_ref,
             du_ref, mrg_ref, dgp_ref, ha_ref, hb_ref, hm_ref, dya_ref, dyb_ref, dym_ref, doa_ref, dob_ref, dom_ref,
             dga_ref, dgb_ref, dgm_ref, dgain_ref, dbias_ref, dbmg_ref, loss_ref):
        @pl.when(pl.program_id(0) == 0)
        def _():
            dgain_ref[...] = jnp.zeros_like(dgain_ref)
            dbias_ref[...] = jnp.zeros_like(dbias_ref)
            dbmg_ref[...] = jnp.zeros_like(dbmg_ref)
            loss_ref[...] = jnp.zeros_like(loss_ref)

        xv = x_ref[...]
        gate = _sigmoid(_dot(_bf(xv), wmg_ref[...]) + bmg_ref[...])

        branches = []
        merged = None
        for b, (o_ref, g_ref, w_ref, h_ref) in enumerate(((oa_ref, ga_ref, wba_ref, ha_ref), (ob_ref, gb_ref, wbb_ref, hb_ref),
                                                         (om_ref, gm_ref, wbm_ref, hm_ref))):
            o, gt = o_ref[...], g_ref[...]
            sg = _sigmoid(gt)
            silu = gt * sg
            h = _bf(o * silu)
            h_ref[...] = h
            y = _dot(h, w_ref[...])
            g_b = gate[:, b * D_MODEL:(b + 1) * D_MODEL]
            term = g_b * y
            merged = term if merged is None else merged + term
            branches.append((o, gt, sg, silu, y, g_b))
        mrg_b = _bf(merged)
        mrg_ref[...] = mrg_b

        u = DEEPNORM_ALPHA * xv + _dot(mrg_b, wout_ref[...])
        mu = jnp.mean(u, axis=-1, keepdims=True)
        uc = u - mu
        rstd = lax.rsqrt(jnp.mean(uc * uc, axis=-1, keepdims=True) + LN_EPS)
        xhat = uc * rstd
        lg = lg_ref[...]
        y_out = xhat * lg + lb_ref[...]
        err = y_out - t_ref[...]
        loss_ref[...] += 0.5 * jnp.sum(jnp.mean(err * err, axis=-1, keepdims=True), axis=0, keepdims=True)
        dy = err * inv_d
        dgain_ref[...] += jnp.sum(dy * xhat, axis=0, keepdims=True)
        dbias_ref[...] += jnp.sum(dy, axis=0, keepdims=True)
        dxh = dy * lg
        du = rstd * (dxh - jnp.mean(dxh, axis=-1, keepdims=True) - xhat * jnp.mean(dxh * xhat, axis=-1, keepdims=True))
        du_ref[...] = du

        dmerged = _dot_nt(_bf(du), wout_ref[...])
        outs = ((dya_ref, doa_ref, dga_ref, wba_ref), (dyb_ref, dob_ref, dgb_ref, wbb_ref), (dym_ref, dom_ref, dgm_ref, wbm_ref))
        dgp = []
        for (o, gt, sg, silu, y, g_b), (dy_ref, do_ref, dg_ref, w_ref) in zip(branches, outs):
            dyb = _bf(dmerged * g_b)
            dy_ref[...] = dyb
            dgp.append(dmerged * y * g_b * (1.0 - g_b))
            dh = _dot_nt(dyb, w_ref[...])
            do_ref[...] = dh * silu
            dg_ref[...] = _bf(dh * o * (sg * (1.0 + gt * (1.0 - sg))))
        dgp = jnp.concatenate(dgp, axis=1)
        dgp_ref[...] = _bf(dgp)
        dbmg_ref[...] += jnp.sum(dgp, axis=0, keepdims=True)

    row = lambda w: pl.BlockSpec((t, w), lambda i: (i, 0))
    pblk = lambda c: pl.BlockSpec((t, 512), lambda i: (i, c))
    full = lambda shp: pl.BlockSpec(shp, lambda i: (0, 0))
    sds = jax.ShapeDtypeStruct
    return pl.pallas_call(
        body, name="mid", grid=(seq // t,),
        in_specs=[row(1024), row(1024), row(512), row(512), row(512), pblk(BLK_GATE_A), pblk(BLK_GATE_B), pblk(BLK_GATE_M),
                  full(wmg.shape), full((1, N_MERGE)), full(wba.shape), full(wbb.shape), full(wbm.shape), full(wout.shape),
                  full((1, D_MODEL)), full((1, D_MODEL))],
        out_specs=[row(1024), row(1024), row(N_MERGE), row(512), row(512), row(512), row(1024), row(1024), row(1024),
                   row(512), row(512), row(512), row(512), row(512), row(512),
                   full((1, D_MODEL)), full((1, D_MODEL)), full((1, N_MERGE)), full((1, 1))],
        out_shape=[sds((seq, 1024), F32), sds((seq, 1024), BF16), sds((seq, N_MERGE), BF16),
                   sds((seq, 512), BF16), sds((seq, 512), BF16), sds((seq, 512), BF16),
                   sds((seq, 1024), BF16), sds((seq, 1024), BF16), sds((seq, 1024), BF16),
                   sds((seq, 512), F32), sds((seq, 512), F32), sds((seq, 512), F32),
                   sds((seq, 512), BF16), sds((seq, 512), BF16), sds((seq, 512), BF16),
                   sds((1, D_MODEL), F32), sds((1, D_MODEL), F32), sds((1, N_MERGE), F32), sds((1, 1), F32)],
        compiler_params=_cparams(("arbitrary",)),
    )(x, tgt, o_a, o_b, o_m, p32, p32, p32, wmg, bmg, wba, wbb, wbm, wout, ln_g, ln_b)


def _primed_weights(w):
    w_in = w["w_in"]
    zc = lambda n: jnp.zeros((D_MODEL, n), w_in.dtype)
    w_in_p = jnp.concatenate([w_in[:, 0:384], zc(64), w_in[:, 384:416], zc(32), w_in[:, 416:]], axis=1)
    wqb = jnp.pad(w["w_q_b"].reshape(MLA_Q_LORA, MLA_HEADS, 96), ((0, 0), (0, 0), (0, 32))).reshape(MLA_Q_LORA, 1024)
    kv3 = w["w_kv_b"].reshape(MLA_KV_LORA, MLA_HEADS, 128)
    wk = jnp.pad(kv3[:, :, :MLA_NOPE], ((0, 0), (0, 0), (0, 64))).reshape(MLA_KV_LORA, 1024)
    wv = kv3[:, :, MLA_NOPE:].reshape(MLA_KV_LORA, 512)
    return w_in_p, wqb, jnp.concatenate([wk, wv], axis=1)


PROJ_BLK = 512
N_GATE_BLKS = N_MERGE // PROJ_BLK


def _grad_x(du, dgpre, wmg, d_proj, w_in_p, *, tm):
    seq = du.shape[0]
    n_pieces = len(d_proj)
    n_steps = N_GATE_BLKS + n_pieces

    def body(du_ref, dg_ref, wmg_ref, *rest):
        piece_refs, win_ref, out_ref, acc = rest[:n_pieces], rest[n_pieces], rest[n_pieces + 1], rest[n_pieces + 2]
        kk = pl.program_id(1)

        @pl.when(kk == 0)
        def _():
            acc[...] = jnp.zeros_like(acc)

        @pl.when(kk < N_GATE_BLKS)
        def _():
            acc[...] += _dot_nt(dg_ref[...], wmg_ref[...])

        for s, p_ref in enumerate(piece_refs):
            @pl.when(kk == N_GATE_BLKS + s)
            def _(p_ref=p_ref):
                acc[...] += _dot_nt(_bf(p_ref[...]), win_ref[...])

        @pl.when(kk == n_steps - 1)
        def _():
            out_ref[...] = acc[...] + DEEPNORM_ALPHA * du_ref[...]

    gate_blk = lambda kk: jnp.minimum(kk, N_GATE_BLKS - 1)
    proj_blk = lambda kk: jnp.maximum(kk - N_GATE_BLKS, 0)
    return pl.pallas_call(
        body, name="grad_x", grid=(seq // tm, n_steps),
        in_specs=[pl.BlockSpec((tm, D_MODEL), lambda i, kk: (i, 0)),
                  pl.BlockSpec((tm, PROJ_BLK), lambda i, kk: (i, gate_blk(kk))),
                  pl.BlockSpec((D_MODEL, PROJ_BLK), lambda i, kk: (0, gate_blk(kk)))]
        + [pl.BlockSpec((tm, PROJ_BLK), lambda i, kk: (i, 0)) for _ in d_proj]
        + [pl.BlockSpec((D_MODEL, PROJ_BLK), lambda i, kk: (0, proj_blk(kk)))],
        out_specs=pl.BlockSpec((tm, D_MODEL), lambda i, kk: (i, 0)),
        out_shape=jax.ShapeDtypeStruct((seq, D_MODEL), F32),
        scratch_shapes=[pltpu.VMEM((tm, D_MODEL), F32)],
        compiler_params=_cparams(("parallel", "arbitrary")),
    )(du, dgpre, wmg, *d_proj, w_in_p)


def _grad_w_in(x, d_proj, *, tk):
    seq = x.shape[0]
    n_pieces = len(d_proj)
    tm = 512
    nk = seq // tk

    def body(x_ref, *rest):
        piece_refs, out_ref, acc = rest[:n_pieces], rest[n_pieces], rest[n_pieces + 1]
        j, kk = pl.program_id(1), pl.program_id(2)

        @pl.when(kk == 0)
        def _():
            acc[...] = jnp.zeros_like(acc)

        xb = _bf(x_ref[...])
        for s, p_ref in enumerate(piece_refs):
            @pl.when(j == s)
            def _(p_ref=p_ref):
                acc[...] += _dot_tn(xb, _bf(p_ref[...]))

        @pl.when(kk == nk - 1)
        def _():
            out_ref[...] = acc[...]

    def piece_spec(s):
        return pl.BlockSpec((tk, PROJ_BLK), lambda i, j, kk: (jnp.where(j == s, kk, 0), 0))

    return pl.pallas_call(
        body, name="grad_w_in", grid=(D_MODEL // tm, n_pieces, nk),
        in_specs=[pl.BlockSpec((tk, tm), lambda i, j, kk: (kk, i))] + [piece_spec(s) for s in range(n_pieces)],
        out_specs=pl.BlockSpec((tm, PROJ_BLK), lambda i, j, kk: (i, j)),
        out_shape=jax.ShapeDtypeStruct((D_MODEL, n_pieces * PROJ_BLK), F32),
        scratch_shapes=[pltpu.VMEM((tm, PROJ_BLK), F32)],
        compiler_params=_cparams(("parallel", "parallel", "arbitrary")),
    )(x, *d_proj)


EARLY_NAMES = ("w_mem_kv", "w_branch_mla", "w_branch_sb", "w_branch_mem", "w_merge_gate", "w_out")
LATE_NAMES = ("w_in", "w_q_b", "w_kv_b")


def _remote(src, dst, send_sem, recv_sem, device):
    return pltpu.make_async_remote_copy(src_ref=src, dst_ref=dst, send_sem=send_sem, recv_sem=recv_sem, device_id=device,
                                        device_id_type=MESH_ID)


def _gather_rider(shards):
    n = len(shards)

    def copies(src_refs, out_refs, sems):
        send_sems, recv_sems, local_sems = sems
        x, y, c = _place()
        me = 2 * x + y
        out = []
        for a, (s, o) in enumerate(zip(src_refs, out_refs)):
            out.append(pltpu.make_async_copy(s, o.at[me], local_sems.at[a]))
            for k, (px, py) in enumerate(_other_chips(x, y)):
                out.append(_Exchange(_remote(s, o.at[me], send_sems.at[k, a], recv_sems.at[k, a], (px, py, c)),
                                     _remote(s, o.at[2 * px + py], send_sems.at[k, a], recv_sems.at[k, a], (px, py, c))))
        return out

    return _Rider(shards, [jax.ShapeDtypeStruct((N_CHIPS,) + s.shape, s.dtype) for s in shards],
                  [pltpu.SemaphoreType.DMA((3, n)), pltpu.SemaphoreType.DMA((3, n)), pltpu.SemaphoreType.DMA((n,))], copies)


def _sibling_rider(g4):
    n = len(g4)

    def copies(g_refs, out_refs, sems):
        send_sems, recv_sems = sems
        x, y, c = _place()
        out = []
        for a, (g, o) in enumerate(zip(g_refs, out_refs)):
            half = g.shape[1] // 2
            theirs = pl.ds(pl.multiple_of((1 - c) * half, 8), half)
            cp = _remote(g.at[:, theirs, :], o, send_sems.at[a], recv_sems.at[a], (x, y, 1 - c))
            out.append(_Exchange(cp, cp))
        return out

    return _Rider(g4, [jax.ShapeDtypeStruct((N_CHIPS, g.shape[1] // 2, g.shape[2]), g.dtype) for g in g4],
                  [pltpu.SemaphoreType.DMA((n,)), pltpu.SemaphoreType.DMA((n,))], copies)


def _chips_rider(wire):
    n = len(wire)

    def copies(s_refs, out_refs, sems):
        send_sems, recv_sems = sems
        x, y, c = _place()
        out = []
        for a, (s, o) in enumerate(zip(s_refs, out_refs)):
            for k, (px, py) in enumerate(_other_chips(x, y)):
                cp = _remote(s.at[2 * px + py], o.at[RELATION_XOR[k] - 1], send_sems.at[k, a], recv_sems.at[k, a], (px, py, c))
                out.append(_Exchange(cp, cp))
        return out

    return _Rider(wire, [jax.ShapeDtypeStruct((3,) + s.shape[1:], s.dtype) for s in wire],
                  [pltpu.SemaphoreType.DMA((3, n)), pltpu.SemaphoreType.DMA((3, n))], copies)


def _local_step(x, mem, tgt, w, small, *, tq, tq_sb_bwd, tk, t_row, t_mm, t_wg, rest_shards=None):
    seq = x.shape[0]
    on_mesh = rest_shards is not None
    w_in_p, wqb, wkvb = _primed_weights(w)
    tabs = _rope_tables(seq)

    p32, pbf = _matmul(x, w_in_p, mode="nn", tm=256, tn=IN_WIDTH_P, tk=D_MODEL, out_dtypes=(F32, BF16), name="proj_in")
    qp, kp, vp = _mla_prep(p32, small["q_a_gain"], small["kv_a_gain"], wqb, wkvb, tabs, t=t_row)
    res = _mla_attn_fwd(qp, kp, vp, tq=tq, tk=tk, rider=_gather_rider(rest_shards) if on_mesh else None)
    o_a, lse = res[0], res[1]
    if on_mesh:
        w = dict(w, **{n: _join_chips(n, g) for n, g in zip(EARLY_NAMES, res[2:])})
    wmg, wout = w["w_merge_gate"], w["w_out"]
    wba, wbb, wbm = w["w_branch_mla"], w["w_branch_sb"], w["w_branch_mem"]
    o_b, keep_total = _sb_attn_fwd(pbf, tq=tq, tk=tk)
    (mkv,) = _matmul(mem, w["w_mem_kv"], mode="nn", tm=MEM_LEN, tn=512, tk=D_MODEL, out_dtypes=(BF16,), name="mem_kv")
    o_m = _mem_fwd(pbf, mkv, t=t_row)

    (du, merged, dgpre, h_a, h_b, h_m, dy_a, dy_b, dy_m, do_a, do_b, do_m, dgate_a, dgate_b, dgate_m,
     d_ln_g, d_ln_b, d_bmg, loss) = _mid(x, tgt, o_a, o_b, o_m, p32, wmg, small["b_merge_gate"], wba, wbb, wbm, wout,
                                         small["ln_gain"], small["ln_bias"], t=t_row)

    wg = functools.partial(_matmul, mode="tn", tm=512, tn=1024, out_dtypes=(F32,))
    dq_m, dmkv = _mem_bwd(pbf, mkv, do_m, t=t_row)
    early = {"w_mem_kv": wg(mem, dmkv, tk=MEM_LEN, name="grad_w_mem_kv")[0],
             "w_branch_mla": wg(h_a, dy_a, tk=t_wg, name="grad_w_branch_a")[0],
             "w_branch_sb": wg(h_b, dy_b, tk=t_wg, name="grad_w_branch_b")[0],
             "w_branch_mem": wg(h_m, dy_m, tk=t_wg, name="grad_w_branch_m")[0],
             "w_merge_gate": wg(x, dgpre, tk=t_wg, name="grad_w_merge_gate")[0],
             "w_out": wg(merged, du, tk=t_wg, name="grad_w_out")[0]}

    if on_mesh:
        g4 = [_split_by_chip(n, early[n]) for n in EARLY_NAMES]
        res = _mla_attn_bwd(qp, kp, vp, o_a, lse, do_a, tq=tq, tk=tk, rider=_sibling_rider(g4))
        (dqp, dkp, dvp), got = res[:3], res[3:]
        chipsum, wire = _rs_add_sibling(g4, got, [BF16] * len(g4))
        res = _sb_attn_bwd(pbf, keep_total, do_b, tq=tq_sb_bwd, tk=tk, rider=_chips_rider(wire))
        (dq_b, dk_b, dv_b), parts = res[:3], res[3:]
        early = _rs_add_chips(chipsum, parts)
    else:
        dqp, dkp, dvp = _mla_attn_bwd(qp, kp, vp, o_a, lse, do_a, tq=tq, tk=tk)
        dq_b, dk_b, dv_b = _sb_attn_bwd(pbf, keep_total, do_b, tq=tq_sb_bwd, tk=tk)
    dlat, d_wqb, d_wkvb, d_gq, d_gkv = _mla_post(p32, dqp, dkp, dvp, small["q_a_gain"], small["kv_a_gain"], wqb, wkvb, tabs,
                                                 t=t_row)

    d_proj = [dlat, dgate_a, dq_b, dk_b, dv_b, dgate_b, dq_m, dgate_m]
    grad_x = _grad_x(du, dgpre, wmg, d_proj, w_in_p, tm=t_mm)
    d_winp = _grad_w_in(x, d_proj, tk=min(1024, seq))

    d_win = jnp.concatenate([d_winp[:, 0:384], d_winp[:, 448:480], d_winp[:, 512:]], axis=1)
    d_wq = d_wqb.reshape(MLA_Q_LORA, MLA_HEADS, 128)[:, :, :96].reshape(MLA_Q_LORA, 768)
    d_wk = d_wkvb[:, :1024].reshape(MLA_KV_LORA, MLA_HEADS, 128)[:, :, :MLA_NOPE]
    d_wv = d_wkvb[:, 1024:].reshape(MLA_KV_LORA, MLA_HEADS, MLA_V)
    d_wkv = jnp.concatenate([d_wk, d_wv], axis=2).reshape(MLA_KV_LORA, 1024)
    late = {"w_in": d_win, "w_q_b": d_wq, "w_kv_b": d_wkv}
    small_grads = {"q_a_gain": d_gq, "kv_a_gain": d_gkv, "b_merge_gate": d_bmg, "ln_gain": d_ln_g, "ln_bias": d_ln_b}
    return loss[0, 0], grad_x, late, small_grads, early


def _pack_shards(shards, small):
    flat_small = jnp.concatenate([small[n].reshape(-1) for n, _ in SMALL_SIZES])
    flat_small = jnp.pad(flat_small, (0, 8 * PACK_COLS - SMALL_TOTAL)).reshape(8, PACK_COLS)
    parts = [shards[n].reshape(-1, PACK_COLS) for n, _ in PACK_ROWS[:-1]] + [flat_small]
    return jnp.concatenate(parts, axis=0)


def _unpack_shards(pack, shapes):
    out, r0 = {}, 0
    for n, rows in PACK_ROWS[:-1]:
        out[n] = pack[r0:r0 + rows].reshape(shapes[n])
        r0 += rows
    flat = pack[r0:r0 + 8].reshape(-1)
    small, c0 = {}, 0
    for n, size in SMALL_SIZES:
        small[n] = flat[c0:c0 + size].reshape(1, size)
        c0 += size
    return out, small


def _split_by_chip(name, full):
    r, c = full.shape
    if name in COL_SHARDED:
        s = full.reshape(r, N_CHIPS, c // N_CHIPS).transpose(1, 0, 2)
    else:
        s = full.reshape(N_CHIPS, r // N_CHIPS, c)
    return s.reshape(N_CHIPS, -1, PACK_COLS)


def _join_chips(name, packed4):
    r, c = FULL_SHAPES[name]
    if name in COL_SHARDED:
        return packed4.reshape(N_CHIPS, r, c // N_CHIPS).transpose(1, 0, 2).reshape(r, c)
    return packed4.reshape(r, c)


def _place():
    x, y, c = lax.axis_index("x"), lax.axis_index("y"), lax.axis_index("c")
    return x, y, c


def _other_chips(x, y):
    return ((1 - x, y), (x, 1 - y), (1 - x, 1 - y))


def _gather_weights(wpack):
    rows = wpack.shape[0]
    chunk = rows // 4

    def body(w_ref, out_ref, wb_ref, send_sems, recv_sems, local_sem):
        x, y, c = _place()
        me = 2 * x + y
        for r in range(4):
            wb_ref[r * chunk:(r + 1) * chunk, :] = _bf(w_ref[r * chunk:(r + 1) * chunk, :])
        mine = pltpu.make_async_copy(wb_ref, out_ref.at[me], local_sem)
        mine.start()
        copies = []
        for k, (px, py) in enumerate(_other_chips(x, y)):
            cp = pltpu.make_async_remote_copy(src_ref=wb_ref, dst_ref=out_ref.at[me], send_sem=send_sems.at[k],
                                              recv_sem=recv_sems.at[k], device_id=(px, py, c), device_id_type=MESH_ID)
            cp.start()
            copies.append(cp)
        for k, (px, py) in enumerate(_other_chips(x, y)):
            pltpu.make_async_remote_copy(src_ref=wb_ref, dst_ref=out_ref.at[2 * px + py], send_sem=send_sems.at[k],
                                         recv_sem=recv_sems.at[k], device_id=(px, py, c), device_id_type=MESH_ID).wait_recv()
        for cp in copies:
            cp.wait_send()
        mine.wait()

    return pl.pallas_call(
        body, name="gather_weights",
        in_specs=[pl.BlockSpec(memory_space=pltpu.VMEM)],
        out_specs=pl.BlockSpec(memory_space=pltpu.HBM),
        out_shape=jax.ShapeDtypeStruct((N_CHIPS, rows, PACK_COLS), BF16),
        scratch_shapes=[pltpu.VMEM((rows, PACK_COLS), BF16), pltpu.SemaphoreType.DMA((3,)), pltpu.SemaphoreType.DMA((3,)),
                        pltpu.SemaphoreType.DMA],
        compiler_params=pltpu.CompilerParams(vmem_limit_bytes=VMEM_LIMIT),
    )(wpack)


def _to_sibling_half(gpack):
    def body(g_ref, out_ref, send_sems, recv_sems):
        x, y, c = _place()
        theirs = pl.ds(pl.multiple_of((1 - c) * PACK_HALF, 8), PACK_HALF)
        copies = [pltpu.make_async_remote_copy(src_ref=g_ref.at[j, theirs, :], dst_ref=out_ref.at[j],
                                               send_sem=send_sems.at[j], recv_sem=recv_sems.at[j],
                                               device_id=(x, y, 1 - c), device_id_type=MESH_ID) for j in range(N_CHIPS)]
        for cp in copies:
            cp.start()
        for cp in copies:
            cp.wait()

    return pl.pallas_call(
        body, name="rs_sibling",
        in_specs=[pl.BlockSpec(memory_space=pltpu.HBM)],
        out_specs=pl.BlockSpec(memory_space=pltpu.HBM),
        out_shape=jax.ShapeDtypeStruct((N_CHIPS, PACK_HALF, PACK_COLS), F32),
        scratch_shapes=[pltpu.SemaphoreType.DMA((N_CHIPS,)), pltpu.SemaphoreType.DMA((N_CHIPS,))],
    )(gpack)


def _add_sibling(gpack, got):
    tr = PACK_HALF // 4

    def body(c_ref, g_ref, r_ref, o_ref):
        o_ref[...] = g_ref[...] + r_ref[...]

    grid_spec = pltpu.PrefetchScalarGridSpec(
        num_scalar_prefetch=1, grid=(N_CHIPS, 4),
        in_specs=[pl.BlockSpec((1, tr, PACK_COLS), lambda j, i, c_ref: (j, c_ref[0] * 4 + i, 0)),
                  pl.BlockSpec((1, tr, PACK_COLS), lambda j, i, c_ref: (j, i, 0))],
        out_specs=pl.BlockSpec((1, tr, PACK_COLS), lambda j, i, c_ref: (j, i, 0)))
    return pl.pallas_call(
        body, name="rs_add_sibling", grid_spec=grid_spec,
        out_shape=jax.ShapeDtypeStruct((N_CHIPS, PACK_HALF, PACK_COLS), F32),
        compiler_params=_cparams(("parallel", "parallel")),
    )(lax.axis_index("c").astype(jnp.int32).reshape(1), gpack, got)


def _to_owner_chips(chipsum):
    def body(s_ref, out_ref, send_sems, recv_sems, local_sem):
        x, y, c = _place()
        me = 2 * x + y
        mine = pltpu.make_async_copy(s_ref.at[me], out_ref.at[me], local_sem)
        mine.start()
        copies = []
        for k, (px, py) in enumerate(_other_chips(x, y)):
            cp = pltpu.make_async_remote_copy(src_ref=s_ref.at[2 * px + py], dst_ref=out_ref.at[me], send_sem=send_sems.at[k],
                                              recv_sem=recv_sems.at[k], device_id=(px, py, c), device_id_type=MESH_ID)
            cp.start()
            copies.append(cp)
        for k, (px, py) in enumerate(_other_chips(x, y)):
            pltpu.make_async_remote_copy(src_ref=s_ref.at[me], dst_ref=out_ref.at[2 * px + py], send_sem=send_sems.at[k],
                                         recv_sem=recv_sems.at[k], device_id=(px, py, c), device_id_type=MESH_ID).wait_recv()
        for cp in copies:
            cp.wait_send()
        mine.wait()

    return pl.pallas_call(
        body, name="rs_chips",
        in_specs=[pl.BlockSpec(memory_space=pltpu.HBM)],
        out_specs=pl.BlockSpec(memory_space=pltpu.HBM),
        out_shape=jax.ShapeDtypeStruct(chipsum.shape, F32),
        scratch_shapes=[pltpu.SemaphoreType.DMA((3,)), pltpu.SemaphoreType.DMA((3,)), pltpu.SemaphoreType.DMA],
    )(chipsum)


def _add_chips(parts):
    tr = PACK_HALF // 4

    def body(p_ref, o_ref):
        o_ref[...] = ((p_ref[0] + p_ref[1]) + p_ref[2]) + p_ref[3]

    return pl.pallas_call(
        body, name="rs_add_chips", grid=(4,),
        in_specs=[pl.BlockSpec((N_CHIPS, tr, PACK_COLS), lambda i: (0, i, 0))],
        out_specs=pl.BlockSpec((tr, PACK_COLS), lambda i: (i, 0)),
        out_shape=jax.ShapeDtypeStruct((PACK_HALF, PACK_COLS), F32),
        compiler_params=_cparams(("parallel",)),
    )(parts)


def _swap_halves(half):
    def body(h_ref, out_ref, send_sem, recv_sem, local_sem):
        x, y, c = _place()
        my_rows = pl.ds(pl.multiple_of(c * PACK_HALF, 8), PACK_HALF)
        mine = pltpu.make_async_copy(h_ref, out_ref.at[my_rows, :], local_sem)
        mine.start()
        cp = pltpu.make_async_remote_copy(src_ref=h_ref, dst_ref=out_ref.at[my_rows, :], send_sem=send_sem, recv_sem=recv_sem,
                                          device_id=(x, y, 1 - c), device_id_type=MESH_ID)
        cp.start()
        their_rows = pl.ds(pl.multiple_of((1 - c) * PACK_HALF, 8), PACK_HALF)
        pltpu.make_async_remote_copy(src_ref=h_ref, dst_ref=out_ref.at[their_rows, :], send_sem=send_sem, recv_sem=recv_sem,
                                     device_id=(x, y, 1 - c), device_id_type=MESH_ID).wait_recv()
        cp.wait_send()
        mine.wait()

    return pl.pallas_call(
        body, name="rs_swap_halves",
        in_specs=[pl.BlockSpec(memory_space=pltpu.HBM)],
        out_specs=pl.BlockSpec(memory_space=pltpu.HBM),
        out_shape=jax.ShapeDtypeStruct((PACK_TOTAL, PACK_COLS), F32),
        scratch_shapes=[pltpu.SemaphoreType.DMA, pltpu.SemaphoreType.DMA, pltpu.SemaphoreType.DMA],
    )(half)


def _adamw(w, g, m, v):
    tr = PACK_TOTAL // 8

    def body(w_ref, g_ref, m_ref, v_ref, d_ref, nm_ref, nv_ref):
        gv = g_ref[...]
        m_new = ADAM_B1 * m_ref[...] + (1.0 - ADAM_B1) * gv
        v_new = ADAM_B2 * v_ref[...] + (1.0 - ADAM_B2) * (gv * gv)
        m_hat = m_new / (1.0 - ADAM_B1 ** ADAM_STEP)
        v_hat = v_new / (1.0 - ADAM_B2 ** ADAM_STEP)
        d_ref[...] = -ADAM_LR * (m_hat / (jnp.sqrt(v_hat) + ADAM_EPS) + ADAM_WD * w_ref[...])
        nm_ref[...] = m_new
        nv_ref[...] = v_new

    blk = pl.BlockSpec((tr, PACK_COLS), lambda i: (i, 0))
    return pl.pallas_call(
        body, name="adamw", grid=(8,),
        in_specs=[blk] * 4, out_specs=[blk] * 3,
        out_shape=[jax.ShapeDtypeStruct((PACK_TOTAL, PACK_COLS), F32)] * 3,
        compiler_params=_cparams(("parallel",)),
    )(w, g, m, v)


SMALL_ROWS = 64
ADAM_STEPS_PER_HALF = 4


def _pack_small(d):
    flat = jnp.concatenate([d[n].reshape(-1) for n, _ in SMALL_SIZES])
    return jnp.pad(flat, (0, SMALL_ROWS * PACK_COLS - SMALL_TOTAL)).reshape(SMALL_ROWS, PACK_COLS)


def _unpack_small(a):
    flat, out, c0 = a.reshape(-1), {}, 0
    for n, size in SMALL_SIZES:
        out[n] = flat[c0:c0 + size].reshape(1, size)
        c0 += size
    return out


def _split_by_chip(name, full):
    r, c = full.shape
    if name in COL_SHARDED:
        return full.reshape(r, N_CHIPS, c // N_CHIPS).transpose(1, 0, 2)
    return full.reshape(N_CHIPS, r // N_CHIPS, c)


def _join_chips(name, slots):
    _, r, cs = slots.shape
    if name in COL_SHARDED:
        return slots.transpose(1, 0, 2).reshape(r, N_CHIPS * cs)
    return slots.reshape(N_CHIPS * r, cs)


HBM_SPEC = pl.BlockSpec(memory_space=pltpu.HBM)


def _gather_shards(shards):
    n = len(shards)

    def body(*refs):
        w_refs, out_refs, wb_refs = refs[:n], refs[n:2 * n], refs[2 * n:3 * n]
        send_sems, recv_sems, pass_send_sems, pass_recv_sems, local_sems = refs[3 * n:]
        x, y, c = _place()
        me = 2 * x + y
        sibling = (x, y, 1 - c)

        def halves(ref):
            half = ref.shape[-2] // 2
            return (pl.ds(pl.multiple_of(c * half, 16), half), pl.ds(pl.multiple_of((1 - c) * half, 16), half))
        for w_ref, wb_ref in zip(w_refs, wb_refs):
            rows = w_ref.shape[0]
            chunk = min(rows, 128)

            def cast(i, carry, w_ref=w_ref, wb_ref=wb_ref, chunk=chunk):
                r0 = pl.multiple_of(i * chunk, chunk)
                wb_ref[pl.ds(r0, chunk), :] = _bf(w_ref[pl.ds(r0, chunk), :])
                return carry

            lax.fori_loop(0, rows // chunk, cast, 0)
        sends, locals_ = [], []
        for a, (wb_ref, out_ref) in enumerate(zip(wb_refs, out_refs)):
            mine = pltpu.make_async_copy(wb_ref, out_ref.at[me], local_sems.at[a])
            mine.start()
            locals_.append(mine)
            mine_rows, _ = halves(wb_ref)
            for k, (px, py) in enumerate(_other_chips(x, y)):
                cp = pltpu.make_async_remote_copy(src_ref=wb_ref.at[mine_rows, :], dst_ref=out_ref.at[me, mine_rows, :],
                                                  send_sem=send_sems.at[k, a], recv_sem=recv_sems.at[k, a],
                                                  device_id=(px, py, c), device_id_type=MESH_ID)
                cp.start()
                sends.append(cp)
        for a, (wb_ref, out_ref) in enumerate(zip(wb_refs, out_refs)):
            mine_rows, _ = halves(wb_ref)
            for k, (px, py) in enumerate(_other_chips(x, y)):
                landed = out_ref.at[2 * px + py, mine_rows, :]
                pltpu.make_async_remote_copy(src_ref=wb_ref.at[mine_rows, :], dst_ref=landed, send_sem=send_sems.at[k, a],
                                             recv_sem=recv_sems.at[k, a], device_id=(px, py, c),
                                             device_id_type=MESH_ID).wait_recv()
                cp = pltpu.make_async_remote_copy(src_ref=landed, dst_ref=landed, send_sem=pass_send_sems.at[k, a],
                                                  recv_sem=pass_recv_sems.at[k, a], device_id=sibling, device_id_type=MESH_ID)
                cp.start()
                sends.append(cp)
        for a, (wb_ref, out_ref) in enumerate(zip(wb_refs, out_refs)):
            _, their_rows = halves(wb_ref)
            for k, (px, py) in enumerate(_other_chips(x, y)):
                passed = out_ref.at[2 * px + py, their_rows, :]
                pltpu.make_async_remote_copy(src_ref=passed, dst_ref=passed, send_sem=pass_send_sems.at[k, a],
                                             recv_sem=pass_recv_sems.at[k, a], device_id=sibling,
                                             device_id_type=MESH_ID).wait_recv()
        for cp in sends:
            cp.wait_send()
        for cp in locals_:
            cp.wait()

    return pl.pallas_call(
        body, name="gather_weights",
        in_specs=[pl.BlockSpec(memory_space=pltpu.VMEM)] * n,
        out_specs=[HBM_SPEC] * n,
        out_shape=[jax.ShapeDtypeStruct((N_CHIPS,) + s.shape, BF16) for s in shards],
        scratch_shapes=[pltpu.VMEM(s.shape, BF16) for s in shards]
        + [pltpu.SemaphoreType.DMA((3, n))] * 4 + [pltpu.SemaphoreType.DMA((n,))],
        compiler_params=pltpu.CompilerParams(vmem_limit_bytes=VMEM_LIMIT),
    )(*shards)


def _cast_bf16_list(arrays):
    def body(*refs):
        for a_ref, o_ref in zip(refs[:len(arrays)], refs[len(arrays):]):
            o_ref[...] = _bf(a_ref[...])

    specs = [pl.BlockSpec((a.shape[0] // 4, a.shape[1]), lambda i: (i, 0)) for a in arrays]
    return pl.pallas_call(
        body, name="cast_shards", grid=(4,), in_specs=specs, out_specs=specs,
        out_shape=[jax.ShapeDtypeStruct(a.shape, BF16) for a in arrays],
        compiler_params=_cparams(("parallel",)),
    )(*arrays)


def _rs_to_sibling(g4):
    n = len(g4)

    def body(*refs):
        g_refs, out_refs = refs[:n], refs[n:2 * n]
        send_sems, recv_sems = refs[2 * n:]
        x, y, c = _place()
        copies = []
        for a, (g_ref, out_ref) in enumerate(zip(g_refs, out_refs)):
            half = g_ref.shape[1] // 2
            theirs = pl.ds(pl.multiple_of((1 - c) * half, 8), half)
            copies.append(pltpu.make_async_remote_copy(src_ref=g_ref.at[:, theirs, :], dst_ref=out_ref, send_sem=send_sems.at[a],
                                                       recv_sem=recv_sems.at[a], device_id=(x, y, 1 - c),
                                                       device_id_type=MESH_ID))
        for cp in copies:
            cp.start()
        for cp in copies:
            cp.wait()

    return pl.pallas_call(
        body, name="rs_sibling", in_specs=[HBM_SPEC] * n, out_specs=[HBM_SPEC] * n,
        out_shape=[jax.ShapeDtypeStruct((N_CHIPS, g.shape[1] // 2, g.shape[2]), F32) for g in g4],
        scratch_shapes=[pltpu.SemaphoreType.DMA((n,)), pltpu.SemaphoreType.DMA((n,))],
    )(*g4)


def _rs_add_sibling(g4, got, wire_dtypes):
    n = len(g4)
    narrow = [a for a in range(n) if wire_dtypes[a] != F32]

    def body(c_ref, *refs):
        outs = refs[2 * n:3 * n]
        wires = dict(zip(narrow, refs[3 * n:]))
        for a, (g_ref, r_ref, o_ref) in enumerate(zip(refs[:n], refs[n:2 * n], outs)):
            s = g_ref[...] + r_ref[...]
            o_ref[...] = s
            if a in wires:
                wires[a][...] = s.astype(wires[a].dtype)

    blk = lambda r: (1, r.shape[1], r.shape[2])
    plain = lambda r: pl.BlockSpec(blk(r), lambda j, c_ref: (j, 0, 0))
    grid_spec = pltpu.PrefetchScalarGridSpec(
        num_scalar_prefetch=1, grid=(N_CHIPS,),
        in_specs=[pl.BlockSpec(blk(r), lambda j, c_ref: (j, c_ref[0], 0)) for r in got] + [plain(r) for r in got],
        out_specs=[plain(r) for r in got] + [plain(got[a]) for a in narrow])
    res = pl.pallas_call(
        body, name="rs_add_sibling", grid_spec=grid_spec,
        out_shape=[jax.ShapeDtypeStruct(r.shape, F32) for r in got]
        + [jax.ShapeDtypeStruct(got[a].shape, wire_dtypes[a]) for a in narrow],
        compiler_params=_cparams(("parallel",)),
    )(lax.axis_index("c").astype(jnp.int32).reshape(1), *g4, *got)
    chipsum = list(res[:n])
    wire = list(chipsum)
    for a, w in zip(narrow, res[n:]):
        wire[a] = w
    return chipsum, wire


RELATION_XOR = (2, 1, 3)


def _rs_to_chips(wire):
    n = len(wire)

    def body(*refs):
        s_refs, out_refs = refs[:n], refs[n:2 * n]
        send_sems, recv_sems = refs[2 * n:]
        x, y, c = _place()
        sends = []
        for a, (s_ref, out_ref) in enumerate(zip(s_refs, out_refs)):
            for k, (px, py) in enumerate(_other_chips(x, y)):
                cp = pltpu.make_async_remote_copy(src_ref=s_ref.at[2 * px + py], dst_ref=out_ref.at[RELATION_XOR[k] - 1],
                                                  send_sem=send_sems.at[k, a], recv_sem=recv_sems.at[k, a],
                                                  device_id=(px, py, c), device_id_type=MESH_ID)
                cp.start()
                sends.append(cp)
        for cp in sends:
            cp.wait_recv()
        for cp in sends:
            cp.wait_send()

    return pl.pallas_call(
        body, name="rs_chips", in_specs=[HBM_SPEC] * n, out_specs=[HBM_SPEC] * n,
        out_shape=[jax.ShapeDtypeStruct((3,) + s.shape[1:], s.dtype) for s in wire],
        scratch_shapes=[pltpu.SemaphoreType.DMA((3, n)), pltpu.SemaphoreType.DMA((3, n))],
    )(*wire)


def _rs_add_chips(chipsum, parts):
    n = len(parts)

    def body(me_ref, *refs):
        me = me_ref[0]
        for s_ref, p_ref, o_ref in zip(refs[:n], refs[n:2 * n], refs[2 * n:]):
            own = s_ref[0]
            total = None
            for k in range(N_CHIPS):
                theirs = p_ref[jnp.maximum(jnp.bitwise_xor(me, k) - 1, 0)].astype(F32)
                term = jnp.where(me == k, own, theirs)
                total = term if total is None else total + term
            o_ref[...] = total

    grid_spec = pltpu.PrefetchScalarGridSpec(
        num_scalar_prefetch=1, grid=(2,),
        in_specs=[pl.BlockSpec((1, p.shape[1] // 2, p.shape[2]), lambda i, me_ref: (me_ref[0], i, 0)) for p in parts]
        + [pl.BlockSpec((3, p.shape[1] // 2, p.shape[2]), lambda i, me_ref: (0, i, 0)) for p in parts],
        out_specs=[pl.BlockSpec((p.shape[1] // 2, p.shape[2]), lambda i, me_ref: (i, 0)) for p in parts])
    me = (2 * lax.axis_index("x") + lax.axis_index("y")).astype(jnp.int32).reshape(1)
    return pl.pallas_call(
        body, name="rs_add_chips", grid_spec=grid_spec,
        out_shape=[jax.ShapeDtypeStruct(p.shape[1:], F32) for p in parts],
        compiler_params=_cparams(("parallel",)),
    )(me, *chipsum, *parts)


def _rs_swap_halves(halves):
    n = len(halves)

    def body(*refs):
        h_refs, out_refs = refs[:n], refs[n:2 * n]
        send_sems, recv_sems = refs[2 * n:]
        x, y, c = _place()
        copies = [pltpu.make_async_remote_copy(src_ref=h_ref, dst_ref=out_ref, send_sem=send_sems.at[a], recv_sem=recv_sems.at[a],
                                               device_id=(x, y, 1 - c), device_id_type=MESH_ID)
                  for a, (h_ref, out_ref) in enumerate(zip(h_refs, out_refs))]
        for cp in copies:
            cp.start()
        for cp in copies:
            cp.wait()

    return pl.pallas_call(
        body, name="rs_swap_halves", in_specs=[HBM_SPEC] * n, out_specs=[HBM_SPEC] * n,
        out_shape=[jax.ShapeDtypeStruct(h.shape, F32) for h in halves],
        scratch_shapes=[pltpu.SemaphoreType.DMA((n,)), pltpu.SemaphoreType.DMA((n,))],
    )(*halves)


def _adamw_list(ws, g_mine, g_theirs, ms, vs):
    n = len(ws)

    def body(c_ref, *refs):
        w_refs, gm_refs, gt_refs, m_refs, v_refs = (refs[k * n:(k + 1) * n] for k in range(5))
        g_refs, d_refs, nm_refs, nv_refs = (refs[k * n:(k + 1) * n] for k in range(5, 9))
        mine = (pl.program_id(0) // ADAM_STEPS_PER_HALF) == c_ref[0]
        for a in range(n):
            gv = jnp.where(mine, gm_refs[a][...], gt_refs[a][...])
            g_refs[a][...] = gv
            m_new = ADAM_B1 * m_refs[a][...] + (1.0 - ADAM_B1) * gv
            v_new = ADAM_B2 * v_refs[a][...] + (1.0 - ADAM_B2) * (gv * gv)
            m_hat = m_new / (1.0 - ADAM_B1 ** ADAM_STEP)
            v_hat = v_new / (1.0 - ADAM_B2 ** ADAM_STEP)
            d_refs[a][...] = -ADAM_LR * (m_hat / (jnp.sqrt(v_hat) + ADAM_EPS) + ADAM_WD * w_refs[a][...])
            nm_refs[a][...] = m_new
            nv_refs[a][...] = v_new

    steps = 2 * ADAM_STEPS_PER_HALF
    whole = [pl.BlockSpec((w.shape[0] // steps, w.shape[1]), lambda i, c_ref: (i, 0)) for w in ws]
    half = [pl.BlockSpec((w.shape[0] // steps, w.shape[1]), lambda i, c_ref: (i % ADAM_STEPS_PER_HALF, 0)) for w in ws]
    shapes = [jax.ShapeDtypeStruct(w.shape, F32) for w in ws]
    grid_spec = pltpu.PrefetchScalarGridSpec(num_scalar_prefetch=1, grid=(steps,),
                                             in_specs=whole + half + half + whole + whole, out_specs=whole * 4)
    res = pl.pallas_call(
        body, name="adamw", grid_spec=grid_spec, out_shape=shapes * 4,
        compiler_params=_cparams(("parallel",)),
    )(lax.axis_index("c").astype(jnp.int32).reshape(1), *ws, *g_mine, *g_theirs, *ms, *vs)
    return res[:n], res[n:2 * n], res[2 * n:3 * n], res[3 * n:]


WEIGHT_NAMES = ("w_in", "w_mem_kv", "q_a_gain", "w_q_b", "kv_a_gain", "w_kv_b", "w_branch_mla", "w_branch_sb",
                "w_branch_mem", "w_merge_gate", "b_merge_gate", "w_out", "ln_gain", "ln_bias")
BIG_NAMES = tuple(n for n, _ in PACK_ROWS[:-1])
SMALL_NAMES = tuple(n for n, _ in SMALL_SIZES)


def kernel(x, mem, w_in, w_mem_kv, q_a_gain, w_q_b, kv_a_gain, w_kv_b, w_branch_mla, w_branch_sb, w_branch_mem, w_merge_gate, b_merge_gate, w_out, ln_gain, ln_bias, loss_target, m_w_in, m_w_mem_kv, m_q_a_gain, m_w_q_b, m_kv_a_gain, m_w_kv_b, m_w_branch_mla, m_w_branch_sb, m_w_branch_mem, m_w_merge_gate, m_b_merge_gate, m_w_out, m_ln_gain, m_ln_bias, v_w_in, v_w_mem_kv, v_q_a_gain, v_w_q_b, v_kv_a_gain, v_w_kv_b, v_w_branch_mla, v_w_branch_sb, v_w_branch_mem, v_w_merge_gate, v_b_merge_gate, v_w_out, v_ln_gain, v_ln_bias):
    weights = dict(zip(WEIGHT_NAMES, (w_in, w_mem_kv, q_a_gain, w_q_b, kv_a_gain, w_kv_b, w_branch_mla, w_branch_sb,
                                      w_branch_mem, w_merge_gate, b_merge_gate, w_out, ln_gain, ln_bias)))
    mom1 = dict(zip(WEIGHT_NAMES, (m_w_in, m_w_mem_kv, m_q_a_gain, m_w_q_b, m_kv_a_gain, m_w_kv_b, m_w_branch_mla,
                                   m_w_branch_sb, m_w_branch_mem, m_w_merge_gate, m_b_merge_gate, m_w_out, m_ln_gain,
                                   m_ln_bias)))
    mom2 = dict(zip(WEIGHT_NAMES, (v_w_in, v_w_mem_kv, v_q_a_gain, v_w_q_b, v_kv_a_gain, v_w_kv_b, v_w_branch_mla,
                                   v_w_branch_sb, v_w_branch_mem, v_w_merge_gate, v_b_merge_gate, v_w_out, v_ln_gain,
                                   v_ln_bias)))
    def as_list(d):
        return [d[n][0] for n in BIG_NAMES] + [_pack_small({n: d[n] for n in SMALL_NAMES})]

    w_list, m_list, v_list = as_list(weights), as_list(mom1), as_list(mom2)

    gathered = _gather_shards([weights[n][0] for n in LATE_NAMES])
    first_w = {n: _join_chips(n, g) for n, g in zip(LATE_NAMES, gathered)}
    rest_shards = _cast_bf16_list([weights[n][0] for n in EARLY_NAMES])
    small = {n: weights[n] for n in SMALL_NAMES}

    seq = x.shape[1]
    loss, grad_x, late, small_grads, early_mine = _local_step(
        x[0], mem[0], loss_target[0], first_w, small, tq=min(1024, seq), tq_sb_bwd=512, tk=256, t_row=256, t_mm=512,
        t_wg=min(2048, seq), rest_shards=rest_shards)

    g4 = [_split_by_chip(n, late[n]) for n in LATE_NAMES]
    g4.append(jnp.broadcast_to(_pack_small(small_grads)[None], (N_CHIPS, SMALL_ROWS, PACK_COLS)))
    got = _rs_to_sibling(g4)
    chipsum, wire = _rs_add_sibling(g4, got, [BF16] * len(LATE_NAMES) + [F32])
    parts = _rs_to_chips(wire)
    late_mine = _rs_add_chips(chipsum, parts)
    by_name = dict(zip(EARLY_NAMES + LATE_NAMES + ("small",), list(early_mine) + list(late_mine)))
    mine = [by_name[n] for n in BIG_NAMES + ("small",)]
    theirs = _rs_swap_halves(mine)
    g_list, d_list, nm_list, nv_list = _adamw_list(w_list, mine, theirs, m_list, v_list)

    loss = lax.psum(loss, ("x", "y", "c"))
    outs = [loss, grad_x[None]]
    for arrays in (g_list, d_list, nm_list, nv_list):
        big = dict(zip(BIG_NAMES, arrays[:-1]))
        sm = _unpack_small(arrays[-1])
        outs.extend(big[n][None] if n in big else sm[n] for n in WEIGHT_NAMES)
    return tuple(outs)
```

```python
import functools
import math

import numpy as np
import jax
import jax.numpy as jnp
from jax import lax
from jax.experimental import pallas as pl
from jax.experimental.pallas import tpu as pltpu

F32 = jnp.float32
BF16 = jnp.bfloat16
MESH_ID = pl.DeviceIdType.MESH

D_MODEL = 1024
MEM_LEN = 256
MLA_HEADS = 8
MLA_NOPE = 64
MLA_ROPE = 32
MLA_V = 64
MLA_Q_LORA = 256
MLA_KV_LORA = 128
SB_HEADS = 8
SB_HEAD_DIM = 64
MEM_HEADS = 4
MEM_HEAD_DIM = 128
BRANCH_WIDTH = 512
ROPE_BASE = 10000.0
RMS_EPS = 1e-6
LN_EPS = 1e-5
DEEPNORM_ALPHA = 2.0 ** 0.25
MLA_SCALE = 1.0 / math.sqrt(MLA_NOPE + MLA_ROPE)
SB_SCALE = 1.0 / math.sqrt(SB_HEAD_DIM)
MEM_SCALE = 1.0 / math.sqrt(MEM_HEAD_DIM)

ADAM_LR = 0.001
ADAM_B1 = 0.9
ADAM_B2 = 0.999
ADAM_EPS = 1e-08
ADAM_WD = 0.01
ADAM_STEP = 10

LANES = 128
HALF = 64
N_CHIPS = 4
PACK_COLS = 1024
VMEM_LIMIT = 56 * 1024 * 1024

IN_WIDTH_P = 4096
BLK_LAT, BLK_GATE_A, BLK_QB, BLK_KB, BLK_VB, BLK_GATE_B, BLK_QM, BLK_GATE_M = range(8)
N_MERGE = 3 * D_MODEL
CAT_WIDTH = N_MERGE + IN_WIDTH_P

PACK_ROWS = (("w_in", 1000), ("w_mem_kv", 256), ("w_q_b", 48), ("w_kv_b", 32), ("w_branch_mla", 128),
             ("w_branch_sb", 128), ("w_branch_mem", 128), ("w_merge_gate", 768), ("w_out", 256), ("small", 8))
PACK_TOTAL = sum(r for _, r in PACK_ROWS)
PACK_HALF = PACK_TOTAL // 2
SMALL_SIZES = (("q_a_gain", 256), ("kv_a_gain", 128), ("b_merge_gate", 3072), ("ln_gain", 1024), ("ln_bias", 1024))
SMALL_TOTAL = sum(s for _, s in SMALL_SIZES)
COL_SHARDED = ("w_in", "w_q_b", "w_kv_b", "w_branch_mla", "w_branch_sb", "w_branch_mem", "w_merge_gate")
ROW_SHARDED = ("w_mem_kv", "w_out")
FULL_SHAPES = {"w_in": (1024, 4000), "w_mem_kv": (1024, 1024), "w_q_b": (256, 768), "w_kv_b": (128, 1024),
               "w_branch_mla": (512, 1024), "w_branch_sb": (512, 1024), "w_branch_mem": (512, 1024),
               "w_merge_gate": (1024, 3072), "w_out": (1024, 1024)}


def _cparams(sem=None):
    return pltpu.CompilerParams(dimension_semantics=sem, vmem_limit_bytes=VMEM_LIMIT)


def _dot(a, b):
    return jnp.dot(a, b, preferred_element_type=F32)


def _dot_nt(a, b):
    return lax.dot_general(a, b, (((1,), (1,)), ((), ())), preferred_element_type=F32)


def _dot_tn(a, b):
    return lax.dot_general(a, b, (((0,), (0,)), ((), ())), preferred_element_type=F32)


def _bf(x):
    return x.astype(BF16)


def _sigmoid(x):
    return 1.0 / (1.0 + jnp.exp(-x))


def _matmul(a, b, *, mode, tm, tn, tk, out_dtypes, name, add=None, add_scale=1.0, b_block0=0, n=None):
    if mode == "nn":
        (m, k), n = a.shape, b.shape[1]
        a_spec = pl.BlockSpec((tm, tk), lambda i, j, kk: (i, kk))
        b_spec = pl.BlockSpec((tk, tn), lambda i, j, kk: (kk, j))
        dot = _dot
    elif mode == "nt":
        (m, k), n = a.shape, b.shape[0]
        a_spec = pl.BlockSpec((tm, tk), lambda i, j, kk: (i, kk))
        b_spec = pl.BlockSpec((tn, tk), lambda i, j, kk: (j, kk))
        dot = _dot_nt
    else:
        (k, m), n = a.shape, (b.shape[1] if n is None else n)
        a_spec = pl.BlockSpec((tk, tm), lambda i, j, kk: (kk, i))
        b_spec = pl.BlockSpec((tk, tn), lambda i, j, kk: (kk, j + b_block0))
        dot = _dot_tn
    assert m % tm == 0 and n % tn == 0 and k % tk == 0, (name, m, n, k)
    nk = k // tk
    n_out = len(out_dtypes)
    has_add = add is not None

    def body(*refs):
        a_ref, b_ref = refs[0], refs[1]
        add_ref = refs[2] if has_add else None
        outs = refs[2 + has_add: 2 + has_add + n_out]
        acc = refs[-1]
        kk = pl.program_id(2)

        @pl.when(kk == 0)
        def _():
            acc[...] = jnp.zeros_like(acc)

        acc[...] += dot(_bf(a_ref[...]), _bf(b_ref[...]))

        @pl.when(kk == nk - 1)
        def _():
            r = acc[...]
            if has_add:
                r = r + add_scale * add_ref[...]
            for o in outs:
                o[...] = r.astype(o.dtype)

    in_specs = [a_spec, b_spec]
    args = [a, b]
    if has_add:
        in_specs.append(pl.BlockSpec((tm, tn), lambda i, j, kk: (i, j)))
        args.append(add)
    out_spec = pl.BlockSpec((tm, tn), lambda i, j, kk: (i, j))
    res = pl.pallas_call(
        body, name=name, grid=(m // tm, n // tn, nk),
        in_specs=in_specs, out_specs=[out_spec] * n_out,
        out_shape=[jax.ShapeDtypeStruct((m, n), dt) for dt in out_dtypes],
        scratch_shapes=[pltpu.VMEM((tm, tn), F32)],
        compiler_params=_cparams(("parallel", "parallel", "arbitrary")),
    )(*args)
    return res


def _rope_tables(seq):
    half = MLA_ROPE // 2
    freqs = ROPE_BASE ** (-jnp.arange(half, dtype=F32) / half)
    ang = jnp.arange(seq, dtype=jnp.int32).astype(F32)[:, None] * freqs[None, :]
    cos, sin = jnp.cos(ang), jnp.sin(ang)
    z = lambda w: jnp.zeros((seq, w), F32)
    c_q = jnp.concatenate([jnp.ones((seq, MLA_NOPE), F32), cos, cos, z(32)], axis=1)
    c_k = jnp.concatenate([z(MLA_NOPE), cos, cos, z(32)], axis=1)
    s_lo = jnp.concatenate([z(MLA_NOPE), -sin, z(half), z(32)], axis=1)
    s_hi = jnp.concatenate([z(MLA_NOPE), z(half), sin, z(32)], axis=1)
    return c_q, c_k, s_lo, s_hi


def _rope_fwd(x, c, s_lo, s_hi):
    return x * c + pltpu.roll(x, LANES - 16, 1) * s_lo + pltpu.roll(x, 16, 1) * s_hi


def _rope_bwd(d, c, s_lo, s_hi):
    return d * c - pltpu.roll(d, 16, 1) * s_hi - pltpu.roll(d, LANES - 16, 1) * s_lo


def _rms_fwd(x, g):
    r = lax.rsqrt(jnp.mean(x * x, axis=-1, keepdims=True) + RMS_EPS)
    xn = x * r
    return xn * g, xn, r


def _mla_prep(p32, gq, gkv, wqb, wkvb, tabs, *, t):
    seq = p32.shape[0]

    def body(lat_ref, gq_ref, gkv_ref, wqb_ref, wkvb_ref, cq_ref, ck_ref, slo_ref, shi_ref, q_ref, k_ref, v_ref):
        lat = lat_ref[...]
        slo, shi = slo_ref[...], shi_ref[...]
        nq, _, _ = _rms_fwd(lat[:, 0:MLA_Q_LORA], gq_ref[...])
        qa = _dot(_bf(nq), wqb_ref[...])
        cq = cq_ref[...]
        for h in range(MLA_HEADS):
            blk = qa[:, h * LANES:(h + 1) * LANES]
            q_ref[:, h * LANES:(h + 1) * LANES] = _bf(_rope_fwd(blk, cq, slo, shi))
        nkv, _, _ = _rms_fwd(lat[:, MLA_Q_LORA:MLA_Q_LORA + MLA_KV_LORA], gkv_ref[...])
        kv = _dot(_bf(nkv), wkvb_ref[...])
        kpe = _rope_fwd(lat[:, 384:512], ck_ref[...], slo, shi)
        for h in range(MLA_HEADS):
            k_ref[:, h * LANES:(h + 1) * LANES] = _bf(kv[:, h * LANES:(h + 1) * LANES] + kpe)
        v_ref[...] = _bf(kv[:, MLA_HEADS * LANES:])

    row = lambda w: pl.BlockSpec((t, w), lambda i: (i, 0))
    full = lambda shp: pl.BlockSpec(shp, lambda i: (0, 0))
    return pl.pallas_call(
        body, name="mla_prep", grid=(seq // t,),
        in_specs=[row(512), full((1, MLA_Q_LORA)), full((1, MLA_KV_LORA)), full(wqb.shape), full(wkvb.shape),
                  row(LANES), row(LANES), row(LANES), row(LANES)],
        out_specs=[row(1024), row(1024), row(512)],
        out_shape=[jax.ShapeDtypeStruct((seq, 1024), BF16), jax.ShapeDtypeStruct((seq, 1024), BF16),
                   jax.ShapeDtypeStruct((seq, 512), BF16)],
        compiler_params=_cparams(("parallel",)),
    )(p32, gq, gkv, wqb, wkvb, *tabs)


def _mla_post(p32, dq, dk, dv, gq, gkv, wqb, wkvb, tabs, *, t):
    seq = p32.shape[0]

    def body(lat_ref, dq_ref, dk_ref, dv_ref, gq_ref, gkv_ref, wqb_ref, wkvb_ref, cq_ref, ck_ref, slo_ref, shi_ref,
             dlat_ref, dwqb_ref, dwkvb_ref, dgq_ref, dgkv_ref):
        @pl.when(pl.program_id(0) == 0)
        def _():
            dwqb_ref[...] = jnp.zeros_like(dwqb_ref)
            dwkvb_ref[...] = jnp.zeros_like(dwkvb_ref)
            dgq_ref[...] = jnp.zeros_like(dgq_ref)
            dgkv_ref[...] = jnp.zeros_like(dgkv_ref)

        lat = lat_ref[...]
        slo, shi = slo_ref[...], shi_ref[...]
        cq = cq_ref[...]
        gq_v, gkv_v = gq_ref[...], gkv_ref[...]
        nq, xq, rq = _rms_fwd(lat[:, 0:MLA_Q_LORA], gq_v)
        nkv, xkv, rkv = _rms_fwd(lat[:, MLA_Q_LORA:MLA_Q_LORA + MLA_KV_LORA], gkv_v)

        dqa = jnp.concatenate(
            [_rope_bwd(dq_ref[:, h * LANES:(h + 1) * LANES], cq, slo, shi) for h in range(MLA_HEADS)], axis=1)
        dqa_b = _bf(dqa)
        dwqb_ref[...] += _dot_tn(_bf(nq), dqa_b)
        dnq = _dot_nt(dqa_b, wqb_ref[...])
        dgq_ref[...] += jnp.sum(dnq * xq, axis=0, keepdims=True)
        dxn = dnq * gq_v
        dcq = rq * (dxn - xq * jnp.mean(dxn * xq, axis=-1, keepdims=True))

        dkf = dk_ref[...]
        dkv_b = _bf(jnp.concatenate([dkf, dv_ref[...]], axis=1))
        dwkvb_ref[...] += _dot_tn(_bf(nkv), dkv_b)
        dnkv = _dot_nt(dkv_b, wkvb_ref[...])
        dgkv_ref[...] += jnp.sum(dnkv * xkv, axis=0, keepdims=True)
        dxn = dnkv * gkv_v
        dckv = rkv * (dxn - xkv * jnp.mean(dxn * xkv, axis=-1, keepdims=True))

        dkpe = dkf[:, 0:LANES]
        for h in range(1, MLA_HEADS):
            dkpe = dkpe + dkf[:, h * LANES:(h + 1) * LANES]
        dkr = _rope_bwd(dkpe, ck_ref[...], slo, shi)
        dlat_ref[...] = _bf(jnp.concatenate([dcq, dckv, dkr], axis=1))

    row = lambda w: pl.BlockSpec((t, w), lambda i: (i, 0))
    full = lambda shp: pl.BlockSpec(shp, lambda i: (0, 0))
    return pl.pallas_call(
        body, name="mla_post", grid=(seq // t,),
        in_specs=[row(512), row(1024), row(1024), row(512), full((1, MLA_Q_LORA)), full((1, MLA_KV_LORA)),
                  full(wqb.shape), full(wkvb.shape), row(LANES), row(LANES), row(LANES), row(LANES)],
        out_specs=[row(512), full(wqb.shape), full(wkvb.shape), full((1, MLA_Q_LORA)), full((1, MLA_KV_LORA))],
        out_shape=[jax.ShapeDtypeStruct((seq, 512), BF16), jax.ShapeDtypeStruct(wqb.shape, F32),
                   jax.ShapeDtypeStruct(wkvb.shape, F32), jax.ShapeDtypeStruct((1, MLA_Q_LORA), F32),
                   jax.ShapeDtypeStruct((1, MLA_KV_LORA), F32)],
        compiler_params=_cparams(("arbitrary",)),
    )(p32, dq, dk, dv, gq, gkv, wqb, wkvb, *tabs)


def _split_bf16(x):
    hi = _bf(x)
    return hi, _bf(x - hi.astype(F32))


def _tri_sum(x, u):
    hi, lo = _split_bf16(x)
    return _dot(hi, u) + _dot(lo, u)


def _softplus(z):
    return jnp.maximum(z, 0.0) + jnp.log(1.0 + jnp.exp(-jnp.abs(z)))


def _head_queries(q, left):
    zero = jnp.zeros_like(q)
    return jnp.where(left, q, zero) * SB_SCALE, jnp.where(left, zero, q) * SB_SCALE


ROW_GROUP = 128
SB_BWD_CHAINS_IN_FLIGHT = 8
ANY_HBM = pl.BlockSpec(memory_space=pltpu.HBM)


class _Exchange:
    def __init__(self, send, landing):
        self.send, self.landing = send, landing

    def start(self):
        self.send.start()

    def wait(self):
        self.landing.wait_recv()
        self.send.wait_send()


class _Rider:
    def __init__(self, operands, out_shapes, sem_shapes, copies):
        self.operands, self.out_shapes, self.sem_shapes, self.copies = list(operands), list(out_shapes), list(sem_shapes), copies


def _call_with_rider(body, rider, *, name, grid, in_specs, out_specs, out_shape, args, semantics, scratch=()):
    scratch = list(scratch)
    if rider is None:
        return pl.pallas_call(body, name=name, grid=grid, in_specs=in_specs, out_specs=out_specs, out_shape=out_shape,
                              scratch_shapes=scratch, compiler_params=_cparams(semantics))(*args)
    n_in, n_out, n_rin, n_rout = len(in_specs), len(out_specs), len(rider.operands), len(rider.out_shapes)

    def full_body(*refs):
        ins, r_ins = refs[:n_in], refs[n_in:n_in + n_rin]
        outs = refs[n_in + n_rin:n_in + n_rin + n_out]
        r_outs = refs[n_in + n_rin + n_out:n_in + n_rin + n_out + n_rout]
        rest = refs[n_in + n_rin + n_out + n_rout:]
        own_scratch, sems = rest[:len(scratch)], rest[len(scratch):]
        first, last = None, None
        for axis, size in enumerate(grid):
            at_start, at_end = pl.program_id(axis) == 0, pl.program_id(axis) == size - 1
            first = at_start if first is None else first & at_start
            last = at_end if last is None else last & at_end

        @pl.when(first)
        def _():
            for cp in rider.copies(r_ins, r_outs, sems):
                cp.start()

        body(*ins, *outs, *own_scratch)

        @pl.when(last)
        def _():
            for cp in rider.copies(r_ins, r_outs, sems):
                cp.wait()

    return pl.pallas_call(
        full_body, name=name, grid=grid, in_specs=list(in_specs) + [ANY_HBM] * n_rin,
        out_specs=list(out_specs) + [ANY_HBM] * n_rout, out_shape=list(out_shape) + rider.out_shapes,
        scratch_shapes=scratch + rider.sem_shapes, compiler_params=_cparams(("arbitrary",) * len(grid)),
    )(*args, *rider.operands)


def _chains(tq):
    return [(h, g) for g in range(tq // ROW_GROUP) for h in range(2)]


def _chain_pattern(g, m, tk, strict):
    r_lo, r_hi = g * ROW_GROUP, (g + 1) * ROW_GROUP - 1
    c_lo, c_hi = m * tk, (m + 1) * tk - 1
    if (c_lo >= r_hi) if strict else (c_lo > r_hi):
        return None
    if (c_hi < r_lo) if strict else (c_hi <= r_lo):
        return True
    rr = lax.broadcasted_iota(jnp.int32, (ROW_GROUP, tk), 0) + r_lo
    cc = lax.broadcasted_iota(jnp.int32, (ROW_GROUP, tk), 1) + c_lo
    return (cc < rr) if strict else (cc <= rr)


def _masked(x, pat, fill=0.0):
    return x if pat is True else jnp.where(pat, x, fill)


def _rows(g):
    return slice(g * ROW_GROUP, (g + 1) * ROW_GROUP)


def _tri_matrix(tk, cmp):
    rr = lax.broadcasted_iota(jnp.int32, (tk, tk), 0)
    cc = lax.broadcasted_iota(jnp.int32, (tk, tk), 1)
    return cmp(rr, cc).astype(BF16)


def _mla_attn_fwd(qp, kp, vp, *, tq, tk, rider=None):
    seq = qp.shape[0]
    nd = tq // tk
    neg = float(np.finfo(np.float32).min)
    chains = _chains(tq)

    def body(q_ref, k_ref, v_ref, o_ref, lse_ref):
        i = pl.program_id(1)
        left = lax.broadcasted_iota(jnp.int32, (tq, LANES), 1) < HALF
        qs = [q_ref[_rows(g), h * LANES:(h + 1) * LANES] for h, g in chains]

        def block(j, carry, m):
            start = pl.multiple_of(j * tk, tk)
            v = v_ref[pl.ds(start, tk), :]
            pats = [True if m is None else _chain_pattern(g, m, tk, False) for _, g in chains]
            live = [n for n, p in enumerate(pats) if p is not None]
            ss = {n: _dot_nt(qs[n], k_ref[pl.ds(start, tk), chains[n][0] * LANES:(chains[n][0] + 1) * LANES]) for n in live}
            new = list(carry)
            for n in live:
                m_old, l_old, acc = carry[n]
                s = _masked(ss[n] * MLA_SCALE, pats[n], neg)
                m_new = jnp.maximum(m_old, jnp.max(s, axis=-1, keepdims=True))
                a = jnp.exp(m_old - m_new)
                p = jnp.exp(s - m_new)
                new[n] = (m_new, a * l_old + jnp.sum(p, axis=-1, keepdims=True), a * acc + _dot(_bf(p), v))
            return tuple(new)

        init = (jnp.full((ROW_GROUP, 1), -1e30, F32), jnp.zeros((ROW_GROUP, 1), F32), jnp.zeros((ROW_GROUP, LANES), F32))
        carry = lax.fori_loop(0, i * nd, lambda j, c: block(j, c, None), (init,) * len(chains))
        for m in range(nd):
            carry = block(i * nd + m, carry, m)
        per_head = []
        for h in range(2):
            mine = [carry[n] for n, (ch, _) in enumerate(chains) if ch == h]
            per_head.append((jnp.concatenate([acc / l for _, l, acc in mine], axis=0),
                             jnp.concatenate([mm + jnp.log(l) for mm, l, _ in mine], axis=0)))
        o_ref[...] = jnp.where(left, per_head[0][0], per_head[1][0])
        lse_ref[...] = jnp.where(left, per_head[0][1], per_head[1][1])

    return _call_with_rider(
        body, rider, name="mla_fwd", grid=(MLA_HEADS // 2, seq // tq),
        in_specs=[pl.BlockSpec((tq, 2 * LANES), lambda p, i: (i, p)), pl.BlockSpec((seq, 2 * LANES), lambda p, i: (0, p)),
                  pl.BlockSpec((seq, LANES), lambda p, i: (0, p))],
        out_specs=[pl.BlockSpec((tq, LANES), lambda p, i: (i, p)), pl.BlockSpec((tq, LANES), lambda p, i: (i, p))],
        out_shape=[jax.ShapeDtypeStruct((seq, 512), F32), jax.ShapeDtypeStruct((seq, 512), F32)],
        args=(qp, kp, vp), semantics=("parallel", "parallel"))


def _mla_attn_bwd(qp, kp, vp, o, lse, do, *, tq, tk, rider=None):
    seq = qp.shape[0]
    nd = tq // tk
    chains = _chains(tq)

    def body(q_ref, k_ref, v_ref, o_ref, lse_ref, do_ref, dq_ref, dk_ref, dv_ref, qt_ref, dot_ref):
        i = pl.program_id(1)

        @pl.when(i == 0)
        def _():
            dk_ref[...] = jnp.zeros_like(dk_ref)
            dv_ref[...] = jnp.zeros_like(dv_ref)

        left = lax.broadcasted_iota(jnp.int32, (tq, LANES), 1) < HALF
        do_f = do_ref[...]
        prod = do_f * o_ref[...]
        lse_v = lse_ref[...]
        do_heads = (_bf(jnp.where(left, do_f, 0.0)), _bf(jnp.where(left, 0.0, do_f)))
        delta_heads = (jnp.sum(jnp.where(left, prod, 0.0), axis=-1, keepdims=True),
                       jnp.sum(jnp.where(left, 0.0, prod), axis=-1, keepdims=True))
        qs = [q_ref[_rows(g), h * LANES:(h + 1) * LANES] for h, g in chains]
        dos = [do_heads[h][_rows(g)] for h, g in chains]
        deltas = [delta_heads[h][_rows(g)] for h, g in chains]
        lses = [lse_v[_rows(g), h * HALF:h * HALF + 1] for h, g in chains]
        for h in range(2):
            qt_ref[h] = q_ref[:, h * LANES:(h + 1) * LANES].T
            dot_ref[h] = do_heads[h].T
        q_t = [qt_ref.at[h] for h in range(2)]
        do_t = [dot_ref.at[h] for h in range(2)]

        def block(j, carry, m):
            start = pl.multiple_of(j * tk, tk)
            v = v_ref[pl.ds(start, tk), :]
            pats = [True if m is None else _chain_pattern(g, m, tk, False) for _, g in chains]
            live = [n for n, p in enumerate(pats) if p is not None]
            ks = [k_ref[pl.ds(start, tk), h * LANES:(h + 1) * LANES] for h in range(2)]
            ss = {n: _dot_nt(qs[n], ks[chains[n][0]]) for n in live}
            dps = {n: _dot_nt(dos[n], v) for n in live}
            new = list(carry)
            ps, dss = {}, {}
            for n in live:
                p = _masked(jnp.exp(ss[n] * MLA_SCALE - lses[n]), pats[n])
                ps[n] = _bf(p)
                dss[n] = _bf(p * (dps[n] - deltas[n]) * MLA_SCALE)
                new[n] = carry[n] + _dot(dss[n], ks[chains[n][0]])
            dv_t, dk_t = None, []
            for h in range(2):
                mine = [n for n in live if chains[n][0] == h]
                first_row = chains[mine[0]][1] * ROW_GROUP
                ds_cat = jnp.concatenate([dss[n] for n in mine], axis=0)
                p_cat = jnp.concatenate([ps[n] for n in mine], axis=0)
                if first_row == 0:
                    q_rows_t, do_rows_t = q_t[h][...], do_t[h][...]
                else:
                    q_rows_t = q_ref[first_row:, h * LANES:(h + 1) * LANES].T
                    do_rows_t = do_heads[h][first_row:].T
                dk_t.append(_dot(q_rows_t, ds_cat))
                term = _dot(do_rows_t, p_cat)
                dv_t = term if dv_t is None else dv_t + term
            back = jnp.concatenate(dk_t + [dv_t], axis=0).T
            dk_ref[pl.ds(start, tk), :] += back[:, :2 * LANES]
            dv_ref[pl.ds(start, tk), :] += back[:, 2 * LANES:]
            return tuple(new)

        zero = jnp.zeros((ROW_GROUP, LANES), F32)
        carry = lax.fori_loop(0, i * nd, lambda j, c: block(j, c, None), (zero,) * len(chains))
        for m in range(nd):
            carry = block(i * nd + m, carry, m)
        for n, (h, g) in enumerate(chains):
            dq_ref[_rows(g), h * LANES:(h + 1) * LANES] = carry[n]

    two_t = pl.BlockSpec((tq, 2 * LANES), lambda p, i: (i, p))
    two_s = pl.BlockSpec((seq, 2 * LANES), lambda p, i: (0, p))
    pair_t = pl.BlockSpec((tq, LANES), lambda p, i: (i, p))
    pair_s = pl.BlockSpec((seq, LANES), lambda p, i: (0, p))
    return _call_with_rider(
        body, rider, name="mla_bwd", grid=(MLA_HEADS // 2, seq // tq),
        in_specs=[two_t, two_s, pair_s, pair_t, pair_t, pair_t],
        out_specs=[two_t, two_s, pair_s],
        out_shape=[jax.ShapeDtypeStruct((seq, 1024), F32), jax.ShapeDtypeStruct((seq, 1024), F32),
                   jax.ShapeDtypeStruct((seq, 512), F32)],
        args=(qp, kp, vp, o, lse, do), semantics=("parallel", "arbitrary"),
        scratch=[pltpu.VMEM((2, LANES, tq), BF16), pltpu.VMEM((2, LANES, tq), BF16)])


def _sb_attn_fwd(pbf, *, tq, tk):
    seq = pbf.shape[0]
    nd = tq // tk
    qb, kb, vb = BLK_QB * 4, BLK_KB * 4, BLK_VB * 4
    chains = _chains(tq)

    def body(q_ref, k_ref, v_ref, o_ref, tot_ref):
        i = pl.program_id(1)
        u_later = _tri_matrix(tk, lambda r, c: r > c)
        left = lax.broadcasted_iota(jnp.int32, (tq, LANES), 1) < HALF
        q_heads = _head_queries(q_ref[...], left)
        qs = [q_heads[h][_rows(g)] for h, g in chains]

        def block(j, carry, m):
            start = pl.multiple_of(j * tk, tk)
            k = k_ref[pl.ds(start, tk), :]
            v = v_ref[pl.ds(start, tk), :]
            pats = [True if m is None else _chain_pattern(g, m, tk, True) for _, g in chains]
            live = [n for n, p in enumerate(pats) if p is not None]
            zs = {n: _dot_nt(qs[n], k) for n in live}
            raws = {n: _softplus(zs[n]) for n in live}
            sps = {n: _masked(raws[n], pats[n]) for n in live}
            laters = {n: _tri_sum(sps[n], u_later) for n in live}
            new = list(carry)
            for n in live:
                c, acc = carry[n]
                a = _masked(jnp.exp(zs[n] - raws[n] - laters[n] - c), pats[n])
                new[n] = (c + laters[n][:, 0:1] + sps[n][:, 0:1], acc + _dot(_bf(a), v))
            return tuple(new)

        init = (jnp.zeros((ROW_GROUP, 1), F32), jnp.zeros((ROW_GROUP, LANES), F32))
        carry = (init,) * len(chains)
        for m in reversed(range(nd)):
            carry = block(i * nd + m, carry, m)
        carry = lax.fori_loop(0, i * nd, lambda jj, cr: block(i * nd - 1 - jj, cr, None), carry)
        per_head = []
        for h in range(2):
            mine = [carry[n] for n, (ch, _) in enumerate(chains) if ch == h]
            per_head.append((jnp.concatenate([acc for _, acc in mine], axis=0), jnp.concatenate([c for c, _ in mine], axis=0)))
        o_ref[...] = jnp.where(left, per_head[0][0], per_head[1][0])
        tot_ref[...] = jnp.where(left, per_head[0][1], per_head[1][1])

    pair_t = pl.BlockSpec((tq, LANES), lambda p, i: (i, p))
    return pl.pallas_call(
        body, name="sb_fwd", grid=(SB_HEADS // 2, seq // tq),
        in_specs=[pl.BlockSpec((tq, LANES), lambda p, i: (i, qb + p)), pl.BlockSpec((seq, LANES), lambda p, i: (0, kb + p)),
                  pl.BlockSpec((seq, LANES), lambda p, i: (0, vb + p))],
        out_specs=[pair_t, pair_t],
        out_shape=[jax.ShapeDtypeStruct((seq, 512), F32), jax.ShapeDtypeStruct((seq, 512), F32)],
        compiler_params=_cparams(("parallel", "parallel")),
    )(pbf, pbf, pbf)


def _sb_attn_bwd(pbf, tot, do, *, tq, tk, rider=None):
    seq = pbf.shape[0]
    nd = tq // tk
    qb, kb, vb = BLK_QB * 4, BLK_KB * 4, BLK_VB * 4
    chains = _chains(tq)
    group = SB_BWD_CHAINS_IN_FLIGHT

    def body(q_ref, k_ref, v_ref, tot_ref, do_ref, dq_ref, dk_ref, dv_ref, qt_ref, dot_ref):
        i = pl.program_id(1)

        @pl.when(i == 0)
        def _():
            dk_ref[...] = jnp.zeros_like(dk_ref)
            dv_ref[...] = jnp.zeros_like(dv_ref)

        u_upto = _tri_matrix(tk, lambda r, c: r <= c)
        u_below = _tri_matrix(tk, lambda r, c: r < c)
        left = lax.broadcasted_iota(jnp.int32, (tq, LANES), 1) < HALF
        q_heads = _head_queries(q_ref[...], left)
        do_f = do_ref[...]
        do_heads = (_bf(jnp.where(left, do_f, 0.0)), _bf(jnp.where(left, 0.0, do_f)))
        tot_v = tot_ref[...]
        qs = [q_heads[h][_rows(g)] for h, g in chains]
        dos = [do_heads[h][_rows(g)] for h, g in chains]
        totals = [tot_v[_rows(g), h * HALF:h * HALF + 1] for h, g in chains]
        qt_ref[...] = jnp.concatenate(qs, axis=0).T
        dot_ref[...] = jnp.concatenate(dos, axis=0).T

        def block(j, carry, m):
            start = pl.multiple_of(j * tk, tk)
            k = k_ref[pl.ds(start, tk), :]
            v = v_ref[pl.ds(start, tk), :]
            pats = [True if m is None else _chain_pattern(g, m, tk, True) for _, g in chains]
            all_live = [n for n, p in enumerate(pats) if p is not None]
            new = list(carry)
            for g0 in range(0, len(all_live), group):
                live = all_live[g0:g0 + group]
                zs = {n: _dot_nt(qs[n], k) for n in live}
                das = {n: _dot_nt(dos[n], v) for n in live}
                raws = {n: _softplus(zs[n]) for n in live}
                sps = {n: _masked(raws[n], pats[n]) for n in live}
                uptos = {n: _tri_sum(sps[n], u_upto) for n in live}
                lbs, a_s, gs = {}, {}, {}
                for n in live:
                    lbs[n] = zs[n] - raws[n]
                    a = _masked(jnp.exp(lbs[n] - (totals[n] - carry[n][0] - uptos[n])), pats[n])
                    a_s[n] = _bf(a)
                    gs[n] = das[n] * a
                belows = {n: _dot(_bf(gs[n]), u_below) for n in live}
                dzs = {}
                for n in live:
                    sp_before, g_before, dq_acc = carry[n]
                    beta = jnp.exp(lbs[n])
                    dz = _masked(gs[n] * (1.0 - beta) - (g_before + belows[n]) * beta, pats[n])
                    dzs[n] = _bf(dz)
                    new[n] = (sp_before + uptos[n][:, tk - 1:tk], g_before + belows[n][:, tk - 1:tk] + gs[n][:, tk - 1:tk],
                              dq_acc + _dot(dzs[n], k))
                dz_cat = jnp.concatenate([dzs[n] for n in live], axis=0)
                a_cat = jnp.concatenate([a_s[n] for n in live], axis=0)
                if len(live) == len(chains):
                    q_rows_t, do_rows_t = qt_ref[...], dot_ref[...]
                else:
                    q_rows_t = jnp.concatenate([qs[n] for n in live], axis=0).T
                    do_rows_t = jnp.concatenate([dos[n] for n in live], axis=0).T
                both = jnp.concatenate([_dot(q_rows_t, dz_cat), _dot(do_rows_t, a_cat)], axis=0).T
                dk_ref[pl.ds(start, tk), :] += both[:, :LANES]
                dv_ref[pl.ds(start, tk), :] += both[:, LANES:]
            return tuple(new)

        zero = jnp.zeros((ROW_GROUP, 1), F32)
        init = (zero, zero, jnp.zeros((ROW_GROUP, LANES), F32))
        carry = lax.fori_loop(0, i * nd, lambda j, cr: block(j, cr, None), (init,) * len(chains))
        for m in range(nd):
            carry = block(i * nd + m, carry, m)
        per_head = [jnp.concatenate([carry[n][2] for n, (ch, _) in enumerate(chains) if ch == h], axis=0) for h in range(2)]
        dq_ref[...] = jnp.where(left, per_head[0], per_head[1]) * SB_SCALE

    pair_t = pl.BlockSpec((tq, LANES), lambda p, i: (i, p))
    pair_s = pl.BlockSpec((seq, LANES), lambda p, i: (0, p))
    return _call_with_rider(
        body, rider, name="sb_bwd", grid=(SB_HEADS // 2, seq // tq),
        in_specs=[pl.BlockSpec((tq, LANES), lambda p, i: (i, qb + p)), pl.BlockSpec((seq, LANES), lambda p, i: (0, kb + p)),
                  pl.BlockSpec((seq, LANES), lambda p, i: (0, vb + p)), pair_t, pair_t],
        out_specs=[pair_t, pair_s, pair_s],
        out_shape=[jax.ShapeDtypeStruct((seq, 512), F32)] * 3,
        args=(pbf, pbf, pbf, tot, do), semantics=("parallel", "arbitrary"),
        scratch=[pltpu.VMEM((LANES, 2 * tq), BF16), pltpu.VMEM((LANES, 2 * tq), BF16)])


def _mem_probs(q, k):
    s = _dot_nt(q, k) * MEM_SCALE
    e = jnp.exp(s - jnp.max(s, axis=-1, keepdims=True))
    return e / jnp.sum(e, axis=-1, keepdims=True)


def _mem_fwd(pbf, mkv, *, t):
    seq = pbf.shape[0]

    def body(q_ref, kv_ref, o_ref):
        for h in range(MEM_HEADS):
            lo, hi = h * LANES, (h + 1) * LANES
            p = _mem_probs(q_ref[:, lo:hi], kv_ref[:, lo:hi])
            o_ref[:, lo:hi] = _dot(_bf(p), kv_ref[:, 512 + lo:512 + hi])

    return pl.pallas_call(
        body, name="mem_fwd", grid=(seq // t,),
        in_specs=[pl.BlockSpec((t, 512), lambda i: (i, BLK_QM)), pl.BlockSpec((MEM_LEN, 1024), lambda i: (0, 0))],
        out_specs=pl.BlockSpec((t, 512), lambda i: (i, 0)),
        out_shape=jax.ShapeDtypeStruct((seq, 512), F32),
        compiler_params=_cparams(("parallel",)),
    )(pbf, mkv)


def _mem_bwd(pbf, mkv, do, *, t):
    seq = pbf.shape[0]

    def body(q_ref, kv_ref, do_ref, dq_ref, dkv_ref):
        @pl.when(pl.program_id(0) == 0)
        def _():
            dkv_ref[...] = jnp.zeros_like(dkv_ref)

        for h in range(MEM_HEADS):
            lo, hi = h * LANES, (h + 1) * LANES
            q, k, v = q_ref[:, lo:hi], kv_ref[:, lo:hi], kv_ref[:, 512 + lo:512 + hi]
            do_h = _bf(do_ref[:, lo:hi])
            p = _mem_probs(q, k)
            dp = _dot_nt(do_h, v)
            ds = _bf(p * (dp - jnp.sum(dp * p, axis=-1, keepdims=True)) * MEM_SCALE)
            dq_ref[:, lo:hi] = _dot(ds, k)
            dkv_ref[:, lo:hi] += _dot_tn(ds, q)
            dkv_ref[:, 512 + lo:512 + hi] += _dot_tn(_bf(p), do_h)

    return pl.pallas_call(
        body, name="mem_bwd", grid=(seq // t,),
        in_specs=[pl.BlockSpec((t, 512), lambda i: (i, BLK_QM)), pl.BlockSpec((MEM_LEN, 1024), lambda i: (0, 0)),
                  pl.BlockSpec((t, 512), lambda i: (i, 0))],
        out_specs=[pl.BlockSpec((t, 512), lambda i: (i, 0)), pl.BlockSpec((MEM_LEN, 1024), lambda i: (0, 0))],
        out_shape=[jax.ShapeDtypeStruct((seq, 512), F32), jax.ShapeDtypeStruct((MEM_LEN, 1024), F32)],
        compiler_params=_cparams(("arbitrary",)),
    )(pbf, mkv, do)


def _mid(x, tgt, o_a, o_b, o_m, p32, wmg, bmg, wba, wbb, wbm, wout, ln_g, ln_b, *, t):
    seq = x.shape[0]
    inv_d = 1.0 / D_MODEL

    def body(x_ref, t_ref, oa_ref, ob_ref, om_ref, ga_ref, gb_ref, gm_ref, wmg_ref, bmg_ref, wba_ref, wbb_ref,
             wbm_ref, wout_ref, lg_ref, lb_ref,
             du_ref, mrg_ref, dgp_ref, ha_ref, hb_ref, hm_ref, dya_ref, dyb_ref, dym_ref, doa_ref, dob_ref, dom_ref,
             dga_ref, dgb_ref, dgm_ref, dgain_ref, dbias_ref, dbmg_ref, loss_ref):
        @pl.when(pl.program_id(0) == 0)
        def _():
            dgain_ref[...] = jnp.zeros_like(dgain_ref)
            dbias_ref[...] = jnp.zeros_like(dbias_ref)
            dbmg_ref[...] = jnp.zeros_like(dbmg_ref)
            loss_ref[...] = jnp.zeros_like(loss_ref)

        xv = x_ref[...]
        gate = _sigmoid(_dot(_bf(xv), wmg_ref[...]) + bmg_ref[...])

        branches = []
        merged = None
        for b, (o_ref, g_ref, w_ref, h_ref) in enumerate(((oa_ref, ga_ref, wba_ref, ha_ref), (ob_ref, gb_ref, wbb_ref, hb_ref),
                                                         (om_ref, gm_ref, wbm_ref, hm_ref))):
            o, gt = o_ref[...], g_ref[...]
            sg = _sigmoid(gt)
            silu = gt * sg
            h = _bf(o * silu)
            h_ref[...] = h
            y = _dot(h, w_ref[...])
            g_b = gate[:, b * D_MODEL:(b + 1) * D_MODEL]
            term = g_b * y
            merged = term if merged is None else merged + term
            branches.append((o, gt, sg, silu, y, g_b))
        mrg_b = _bf(merged)
        mrg_ref[...] = mrg_b

        u = DEEPNORM_ALPHA * xv + _dot(mrg_b, wout_ref[...])
        mu = jnp.mean(u, axis=-1, keepdims=True)
        uc = u - mu
        rstd = lax.rsqrt(jnp.mean(uc * uc, axis=-1, keepdims=True) + LN_EPS)
        xhat = uc * rstd
        lg = lg_ref[...]
        y_out = xhat * lg + lb_ref[...]
        err = y_out - t_ref[...]
        loss_ref[...] += 0.5 * jnp.sum(jnp.mean(err * err, axis=-1, keepdims=True), axis=0, keepdims=True)
        dy = err * inv_d
        dgain_ref[...] += jnp.sum(dy * xhat, axis=0, keepdims=True)
        dbias_ref[...] += jnp.sum(dy, axis=0, keepdims=True)
        dxh = dy * lg
        du = rstd * (dxh - jnp.mean(dxh, axis=-1, keepdims=True) - xhat * jnp.mean(dxh * xhat, axis=-1, keepdims=True))
        du_ref[...] = du

        dmerged = _dot_nt(_bf(du), wout_ref[...])
        outs = ((dya_ref, doa_ref, dga_ref, wba_ref), (dyb_ref, dob_ref, dgb_ref, wbb_ref), (dym_ref, dom_ref, dgm_ref, wbm_ref))
        dgp = []
        for (o, gt, sg, silu, y, g_b), (dy_ref, do_ref, dg_ref, w_ref) in zip(branches, outs):
            dyb = _bf(dmerged * g_b)
            dy_ref[...] = dyb
            dgp.append(dmerged * y * g_b * (1.0 - g_b))
            dh = _dot_nt(dyb, w_ref[...])
            do_ref[...] = dh * silu
            dg_ref[...] = _bf(dh * o * (sg * (1.0 + gt * (1.0 - sg))))
        dgp = jnp.concatenate(dgp, axis=1)
        dgp_ref[...] = _bf(dgp)
        dbmg_ref[...] += jnp.sum(dgp, axis=0, keepdims=True)

    row = lambda w: pl.BlockSpec((t, w), lambda i: (i, 0))
    pblk = lambda c: pl.BlockSpec((t, 512), lambda i: (i, c))
    full = lambda shp: pl.BlockSpec(shp, lambda i: (0, 0))
    sds = jax.ShapeDtypeStruct
    return pl.pallas_call(
        body, name="mid", grid=(seq // t,),
        in_specs=[row(1024), row(1024), row(512), row(512), row(512), pblk(BLK_GATE_A), pblk(BLK_GATE_B), pblk(BLK_GATE_M),
                  full(wmg.shape), full((1, N_MERGE)), full(wba.shape), full(wbb.shape), full(wbm.shape), full(wout.shape),
                  full((1, D_MODEL)), full((1, D_MODEL))],
        out_specs=[row(1024), row(1024), row(N_MERGE), row(512), row(512), row(512), row(1024), row(1024), row(1024),
                   row(512), row(512), row(512), row(512), row(512), row(512),
                   full((1, D_MODEL)), full((1, D_MODEL)), full((1, N_MERGE)), full((1, 1))],
        out_shape=[sds((seq, 1024), F32), sds((seq, 1024), BF16), sds((seq, N_MERGE), BF16),
                   sds((seq, 512), BF16), sds((seq, 512), BF16), sds((seq, 512), BF16),
                   sds((seq, 1024), BF16), sds((seq, 1024), BF16), sds((seq, 1024), BF16),
                   sds((seq, 512), F32), sds((seq, 512), F32), sds((seq, 512), F32),
                   sds((seq, 512), BF16), sds((seq, 512), BF16), sds((seq, 512), BF16),
                   sds((1, D_MODEL), F32), sds((1, D_MODEL), F32), sds((1, N_MERGE), F32), sds((1, 1), F32)],
        compiler_params=_cparams(("arbitrary",)),
    )(x, tgt, o_a, o_b, o_m, p32, p32, p32, wmg, bmg, wba, wbb, wbm, wout, ln_g, ln_b)


def _primed_weights(w):
    w_in = w["w_in"]
    zc = lambda n: jnp.zeros((D_MODEL, n), w_in.dtype)
    w_in_p = jnp.concatenate([w_in[:, 0:384], zc(64), w_in[:, 384:416], zc(32), w_in[:, 416:]], axis=1)
    wqb = jnp.pad(w["w_q_b"].reshape(MLA_Q_LORA, MLA_HEADS, 96), ((0, 0), (0, 0), (0, 32))).reshape(MLA_Q_LORA, 1024)
    kv3 = w["w_kv_b"].reshape(MLA_KV_LORA, MLA_HEADS, 128)
    wk = jnp.pad(kv3[:, :, :MLA_NOPE], ((0, 0), (0, 0), (0, 64))).reshape(MLA_KV_LORA, 1024)
    wv = kv3[:, :, MLA_NOPE:].reshape(MLA_KV_LORA, 512)
    return w_in_p, wqb, jnp.concatenate([wk, wv], axis=1)


PROJ_BLK = 512
N_GATE_BLKS = N_MERGE // PROJ_BLK


def _grad_x(du, dgpre, wmg, d_proj, w_in_p, *, tm, rider=None):
    seq = du.shape[0]
    n_pieces = len(d_proj)

    def body(du_ref, dg_ref, wmg_ref, *rest):
        piece_refs, win_ref, out_ref = rest[:n_pieces], rest[n_pieces], rest[n_pieces + 1]
        d_p = jnp.concatenate([_bf(p_ref[...]) for p_ref in piece_refs], axis=1)
        out_ref[...] = (DEEPNORM_ALPHA * du_ref[...] + _dot_nt(dg_ref[...], wmg_ref[...])) + _dot_nt(d_p, win_ref[...])

    row = lambda w: pl.BlockSpec((tm, w), lambda i: (i, 0))
    whole = lambda a: pl.BlockSpec(a.shape, lambda i: (0, 0))
    return _call_with_rider(
        body, rider, name="grad_x", grid=(seq // tm,),
        in_specs=[row(D_MODEL), row(N_MERGE), whole(wmg)] + [row(PROJ_BLK) for _ in d_proj] + [whole(w_in_p)],
        out_specs=[row(D_MODEL)], out_shape=[jax.ShapeDtypeStruct((seq, D_MODEL), F32)],
        args=(du, dgpre, wmg, *d_proj, w_in_p), semantics=("parallel",))


def _grad_w_in(x, d_proj, *, tk):
    seq = x.shape[0]
    n_pieces = len(d_proj)
    nk = seq // tk

    def body(x_ref, *rest):
        piece_refs, out_ref, acc = rest[:n_pieces], rest[n_pieces], rest[n_pieces + 1]
        j, kk = pl.program_id(0), pl.program_id(1)

        @pl.when(kk == 0)
        def _():
            acc[...] = jnp.zeros_like(acc)

        xb = _bf(x_ref[...])
        for pair in range(n_pieces // 2):
            @pl.when(j == pair)
            def _(pair=pair):
                both = jnp.concatenate([_bf(piece_refs[2 * pair][...]), _bf(piece_refs[2 * pair + 1][...])], axis=1)
                acc[...] += _dot_tn(xb, both)

        @pl.when(kk == nk - 1)
        def _():
            out_ref[...] = acc[...]

    def piece_spec(s):
        return pl.BlockSpec((tk, PROJ_BLK), lambda j, kk: (jnp.where(j == s // 2, kk, 0), 0))

    return pl.pallas_call(
        body, name="grad_w_in", grid=(n_pieces // 2, nk),
        in_specs=[pl.BlockSpec((tk, D_MODEL), lambda j, kk: (kk, 0))] + [piece_spec(s) for s in range(n_pieces)],
        out_specs=pl.BlockSpec((D_MODEL, 2 * PROJ_BLK), lambda j, kk: (0, j)),
        out_shape=jax.ShapeDtypeStruct((D_MODEL, n_pieces * PROJ_BLK), F32),
        scratch_shapes=[pltpu.VMEM((D_MODEL, 2 * PROJ_BLK), F32)],
        compiler_params=_cparams(("parallel", "arbitrary")),
    )(x, *d_proj)


EARLY_NAMES = ("w_mem_kv", "w_branch_mla", "w_branch_sb", "w_branch_mem", "w_merge_gate", "w_out")
LATE_NAMES = ("w_in", "w_q_b", "w_kv_b")


def _remote(src, dst, send_sem, recv_sem, device):
    return pltpu.make_async_remote_copy(src_ref=src, dst_ref=dst, send_sem=send_sem, recv_sem=recv_sem, device_id=device,
                                        device_id_type=MESH_ID)


def _gather_rider(shards):
    n = len(shards)

    def copies(src_refs, out_refs, sems):
        send_sems, recv_sems, local_sems = sems
        x, y, c = _place()
        me = 2 * x + y
        out = []
        for a, (s, o) in enumerate(zip(src_refs, out_refs)):
            out.append(pltpu.make_async_copy(s, o.at[me], local_sems.at[a]))
            for k, (px, py) in enumerate(_other_chips(x, y)):
                out.append(_Exchange(_remote(s, o.at[me], send_sems.at[k, a], recv_sems.at[k, a], (px, py, c)),
                                     _remote(s, o.at[2 * px + py], send_sems.at[k, a], recv_sems.at[k, a], (px, py, c))))
        return out

    return _Rider(shards, [jax.ShapeDtypeStruct((N_CHIPS,) + s.shape, s.dtype) for s in shards],
                  [pltpu.SemaphoreType.DMA((3, n)), pltpu.SemaphoreType.DMA((3, n)), pltpu.SemaphoreType.DMA((n,))], copies)


def _sibling_rider(g4):
    n = len(g4)

    def copies(g_refs, out_refs, sems):
        send_sems, recv_sems = sems
        x, y, c = _place()
        out = []
        for a, (g, o) in enumerate(zip(g_refs, out_refs)):
            half = g.shape[1] // 2
            theirs = pl.ds(pl.multiple_of((1 - c) * half, 8), half)
            cp = _remote(g.at[:, theirs, :], o, send_sems.at[a], recv_sems.at[a], (x, y, 1 - c))
            out.append(_Exchange(cp, cp))
        return out

    return _Rider(g4, [jax.ShapeDtypeStruct((N_CHIPS, g.shape[1] // 2, g.shape[2]), g.dtype) for g in g4],
                  [pltpu.SemaphoreType.DMA((n,)), pltpu.SemaphoreType.DMA((n,))], copies)


def _chips_rider(wire):
    n = len(wire)

    def copies(s_refs, out_refs, sems):
        send_sems, recv_sems = sems
        x, y, c = _place()
        out = []
        for a, (s, o) in enumerate(zip(s_refs, out_refs)):
            for k, (px, py) in enumerate(_other_chips(x, y)):
                cp = _remote(s.at[2 * px + py], o.at[RELATION_XOR[k] - 1], send_sems.at[k, a], recv_sems.at[k, a], (px, py, c))
                out.append(_Exchange(cp, cp))
        return out

    return _Rider(wire, [jax.ShapeDtypeStruct((3,) + s.shape[1:], s.dtype) for s in wire],
                  [pltpu.SemaphoreType.DMA((3, n)), pltpu.SemaphoreType.DMA((3, n))], copies)


def _local_step(x, mem, tgt, w, small, *, tq, tq_sb_bwd, tk, t_row, t_mm, t_wg, rest_shards=None):
    seq = x.shape[0]
    on_mesh = rest_shards is not None
    w_in_p, wqb, wkvb = _primed_weights(w)
    tabs = _rope_tables(seq)

    p32, pbf = _matmul(x, w_in_p, mode="nn", tm=256, tn=IN_WIDTH_P, tk=D_MODEL, out_dtypes=(F32, BF16), name="proj_in")
    qp, kp, vp = _mla_prep(p32, small["q_a_gain"], small["kv_a_gain"], wqb, wkvb, tabs, t=t_row)
    res = _mla_attn_fwd(qp, kp, vp, tq=tq, tk=tk, rider=_gather_rider(rest_shards) if on_mesh else None)
    o_a, lse = res[0], res[1]
    if on_mesh:
        w = dict(w, **{n: _join_chips(n, g) for n, g in zip(EARLY_NAMES, res[2:])})
    wmg, wout = w["w_merge_gate"], w["w_out"]
    wba, wbb, wbm = w["w_branch_mla"], w["w_branch_sb"], w["w_branch_mem"]
    o_b, keep_total = _sb_attn_fwd(pbf, tq=tq, tk=tk)
    (mkv,) = _matmul(mem, w["w_mem_kv"], mode="nn", tm=MEM_LEN, tn=512, tk=D_MODEL, out_dtypes=(BF16,), name="mem_kv")
    o_m = _mem_fwd(pbf, mkv, t=t_row)

    (du, merged, dgpre, h_a, h_b, h_m, dy_a, dy_b, dy_m, do_a, do_b, do_m, dgate_a, dgate_b, dgate_m,
     d_ln_g, d_ln_b, d_bmg, loss) = _mid(x, tgt, o_a, o_b, o_m, p32, wmg, small["b_merge_gate"], wba, wbb, wbm, wout,
                                         small["ln_gain"], small["ln_bias"], t=t_row)

    wg = functools.partial(_matmul, mode="tn", tm=512, tn=1024, out_dtypes=(F32,))
    dq_m, dmkv = _mem_bwd(pbf, mkv, do_m, t=t_row)
    early = {"w_mem_kv": wg(mem, dmkv, tk=MEM_LEN, name="grad_w_mem_kv")[0],
             "w_branch_mla": wg(h_a, dy_a, tk=t_wg, name="grad_w_branch_a")[0],
             "w_branch_sb": wg(h_b, dy_b, tk=t_wg, name="grad_w_branch_b")[0],
             "w_branch_mem": wg(h_m, dy_m, tk=t_wg, name="grad_w_branch_m")[0],
             "w_merge_gate": wg(x, dgpre, tk=t_wg, name="grad_w_merge_gate")[0],
             "w_out": wg(merged, du, tk=t_wg, name="grad_w_out")[0]}

    if on_mesh:
        g4 = [_split_by_chip(n, early[n]) for n in EARLY_NAMES]
        res = _mla_attn_bwd(qp, kp, vp, o_a, lse, do_a, tq=tq, tk=tk, rider=_sibling_rider(g4))
        (dqp, dkp, dvp), got = res[:3], res[3:]
        chipsum, wire = _rs_add_sibling(g4, got, [BF16] * len(g4))
        res = _sb_attn_bwd(pbf, keep_total, do_b, tq=tq_sb_bwd, tk=tk, rider=_chips_rider(wire))
        (dq_b, dk_b, dv_b), parts = res[:3], res[3:]
        early = _rs_add_chips(chipsum, parts)
    else:
        dqp, dkp, dvp = _mla_attn_bwd(qp, kp, vp, o_a, lse, do_a, tq=tq, tk=tk)
        dq_b, dk_b, dv_b = _sb_attn_bwd(pbf, keep_total, do_b, tq=tq_sb_bwd, tk=tk)
    dlat, d_wqb, d_wkvb, d_gq, d_gkv = _mla_post(p32, dqp, dkp, dvp, small["q_a_gain"], small["kv_a_gain"], wqb, wkvb, tabs,
                                                 t=t_row)

    d_proj = [dlat, dgate_a, dq_b, dk_b, dv_b, dgate_b, dq_m, dgate_m]
    d_winp = _grad_w_in(x, d_proj, tk=min(1024, seq))

    d_win = jnp.concatenate([d_winp[:, 0:384], d_winp[:, 448:480], d_winp[:, 512:]], axis=1)
    d_wq = d_wqb.reshape(MLA_Q_LORA, MLA_HEADS, 128)[:, :, :96].reshape(MLA_Q_LORA, 768)
    d_wk = d_wkvb[:, :1024].reshape(MLA_KV_LORA, MLA_HEADS, 128)[:, :, :MLA_NOPE]
    d_wv = d_wkvb[:, 1024:].reshape(MLA_KV_LORA, MLA_HEADS, MLA_V)
    d_wkv = jnp.concatenate([d_wk, d_wv], axis=2).reshape(MLA_KV_LORA, 1024)
    late = {"w_in": d_win, "w_q_b": d_wq, "w_kv_b": d_wkv}
    small_grads = {"q_a_gain": d_gq, "kv_a_gain": d_gkv, "b_merge_gate": d_bmg, "ln_gain": d_ln_g, "ln_bias": d_ln_b}
    if not on_mesh:
        (grad_x,) = _grad_x(du, dgpre, wmg, d_proj, w_in_p, tm=256)
        return loss[0, 0], grad_x, late, small_grads, early

    g4 = [_split_by_chip(n, late[n]) for n in LATE_NAMES]
    g4.append(jnp.broadcast_to(_pack_small(small_grads)[None], (N_CHIPS, SMALL_ROWS, PACK_COLS)))
    got = _rs_to_sibling(g4)
    chipsum, wire = _rs_add_sibling(g4, got, [BF16] * len(LATE_NAMES) + [F32])
    res = _grad_x(du, dgpre, wmg, d_proj, w_in_p, tm=256, rider=_chips_rider(wire))
    late_mine = _rs_add_chips(chipsum, res[1:])
    return loss[0, 0], res[0], late_mine, None, early


def _pack_shards(shards, small):
    flat_small = jnp.concatenate([small[n].reshape(-1) for n, _ in SMALL_SIZES])
    flat_small = jnp.pad(flat_small, (0, 8 * PACK_COLS - SMALL_TOTAL)).reshape(8, PACK_COLS)
    parts = [shards[n].reshape(-1, PACK_COLS) for n, _ in PACK_ROWS[:-1]] + [flat_small]
    return jnp.concatenate(parts, axis=0)


def _unpack_shards(pack, shapes):
    out, r0 = {}, 0
    for n, rows in PACK_ROWS[:-1]:
        out[n] = pack[r0:r0 + rows].reshape(shapes[n])
        r0 += rows
    flat = pack[r0:r0 + 8].reshape(-1)
    small, c0 = {}, 0
    for n, size in SMALL_SIZES:
        small[n] = flat[c0:c0 + size].reshape(1, size)
        c0 += size
    return out, small


def _split_by_chip(name, full):
    r, c = full.shape
    if name in COL_SHARDED:
        s = full.reshape(r, N_CHIPS, c // N_CHIPS).transpose(1, 0, 2)
    else:
        s = full.reshape(N_CHIPS, r // N_CHIPS, c)
    return s.reshape(N_CHIPS, -1, PACK_COLS)


def _join_chips(name, packed4):
    r, c = FULL_SHAPES[name]
    if name in COL_SHARDED:
        return packed4.reshape(N_CHIPS, r, c // N_CHIPS).transpose(1, 0, 2).reshape(r, c)
    return packed4.reshape(r, c)


def _place():
    x, y, c = lax.axis_index("x"), lax.axis_index("y"), lax.axis_index("c")
    return x, y, c


def _other_chips(x, y):
    return ((1 - x, y), (x, 1 - y), (1 - x, 1 - y))


def _gather_weights(wpack):
    rows = wpack.shape[0]
    chunk = rows // 4

    def body(w_ref, out_ref, wb_ref, send_sems, recv_sems, local_sem):
        x, y, c = _place()
        me = 2 * x + y
        for r in range(4):
            wb_ref[r * chunk:(r + 1) * chunk, :] = _bf(w_ref[r * chunk:(r + 1) * chunk, :])
        mine = pltpu.make_async_copy(wb_ref, out_ref.at[me], local_sem)
        mine.start()
        copies = []
        for k, (px, py) in enumerate(_other_chips(x, y)):
            cp = pltpu.make_async_remote_copy(src_ref=wb_ref, dst_ref=out_ref.at[me], send_sem=send_sems.at[k],
                                              recv_sem=recv_sems.at[k], device_id=(px, py, c), device_id_type=MESH_ID)
            cp.start()
            copies.append(cp)
        for k, (px, py) in enumerate(_other_chips(x, y)):
            pltpu.make_async_remote_copy(src_ref=wb_ref, dst_ref=out_ref.at[2 * px + py], send_sem=send_sems.at[k],
                                         recv_sem=recv_sems.at[k], device_id=(px, py, c), device_id_type=MESH_ID).wait_recv()
        for cp in copies:
            cp.wait_send()
        mine.wait()

    return pl.pallas_call(
        body, name="gather_weights",
        in_specs=[pl.BlockSpec(memory_space=pltpu.VMEM)],
        out_specs=pl.BlockSpec(memory_space=pltpu.HBM),
        out_shape=jax.ShapeDtypeStruct((N_CHIPS, rows, PACK_COLS), BF16),
        scratch_shapes=[pltpu.VMEM((rows, PACK_COLS), BF16), pltpu.SemaphoreType.DMA((3,)), pltpu.SemaphoreType.DMA((3,)),
                        pltpu.SemaphoreType.DMA],
        compiler_params=pltpu.CompilerParams(vmem_limit_bytes=VMEM_LIMIT),
    )(wpack)


def _to_sibling_half(gpack):
    def body(g_ref, out_ref, send_sems, recv_sems):
        x, y, c = _place()
        theirs = pl.ds(pl.multiple_of((1 - c) * PACK_HALF, 8), PACK_HALF)
        copies = [pltpu.make_async_remote_copy(src_ref=g_ref.at[j, theirs, :], dst_ref=out_ref.at[j],
                                               send_sem=send_sems.at[j], recv_sem=recv_sems.at[j],
                                               device_id=(x, y, 1 - c), device_id_type=MESH_ID) for j in range(N_CHIPS)]
        for cp in copies:
            cp.start()
        for cp in copies:
            cp.wait()

    return pl.pallas_call(
        body, name="rs_sibling",
        in_specs=[pl.BlockSpec(memory_space=pltpu.HBM)],
        out_specs=pl.BlockSpec(memory_space=pltpu.HBM),
        out_shape=jax.ShapeDtypeStruct((N_CHIPS, PACK_HALF, PACK_COLS), F32),
        scratch_shapes=[pltpu.SemaphoreType.DMA((N_CHIPS,)), pltpu.SemaphoreType.DMA((N_CHIPS,))],
    )(gpack)


def _add_sibling(gpack, got):
    tr = PACK_HALF // 4

    def body(c_ref, g_ref, r_ref, o_ref):
        o_ref[...] = g_ref[...] + r_ref[...]

    grid_spec = pltpu.PrefetchScalarGridSpec(
        num_scalar_prefetch=1, grid=(N_CHIPS, 4),
        in_specs=[pl.BlockSpec((1, tr, PACK_COLS), lambda j, i, c_ref: (j, c_ref[0] * 4 + i, 0)),
                  pl.BlockSpec((1, tr, PACK_COLS), lambda j, i, c_ref: (j, i, 0))],
        out_specs=pl.BlockSpec((1, tr, PACK_COLS), lambda j, i, c_ref: (j, i, 0)))
    return pl.pallas_call(
        body, name="rs_add_sibling", grid_spec=grid_spec,
        out_shape=jax.ShapeDtypeStruct((N_CHIPS, PACK_HALF, PACK_COLS), F32),
        compiler_params=_cparams(("parallel", "parallel")),
    )(lax.axis_index("c").astype(jnp.int32).reshape(1), gpack, got)


def _to_owner_chips(chipsum):
    def body(s_ref, out_ref, send_sems, recv_sems, local_sem):
        x, y, c = _place()
        me = 2 * x + y
        mine = pltpu.make_async_copy(s_ref.at[me], out_ref.at[me], local_sem)
        mine.start()
        copies = []
        for k, (px, py) in enumerate(_other_chips(x, y)):
            cp = pltpu.make_async_remote_copy(src_ref=s_ref.at[2 * px + py], dst_ref=out_ref.at[me], send_sem=send_sems.at[k],
                                              recv_sem=recv_sems.at[k], device_id=(px, py, c), device_id_type=MESH_ID)
            cp.start()
            copies.append(cp)
        for k, (px, py) in enumerate(_other_chips(x, y)):
            pltpu.make_async_remote_copy(src_ref=s_ref.at[me], dst_ref=out_ref.at[2 * px + py], send_sem=send_sems.at[k],
                                         recv_sem=recv_sems.at[k], device_id=(px, py, c), device_id_type=MESH_ID).wait_recv()
        for cp in copies:
            cp.wait_send()
        mine.wait()

    return pl.pallas_call(
        body, name="rs_chips",
        in_specs=[pl.BlockSpec(memory_space=pltpu.HBM)],
        out_specs=pl.BlockSpec(memory_space=pltpu.HBM),
        out_shape=jax.ShapeDtypeStruct(chipsum.shape, F32),
        scratch_shapes=[pltpu.SemaphoreType.DMA((3,)), pltpu.SemaphoreType.DMA((3,)), pltpu.SemaphoreType.DMA],
    )(chipsum)


def _add_chips(parts):
    tr = PACK_HALF // 4

    def body(p_ref, o_ref):
        o_ref[...] = ((p_ref[0] + p_ref[1]) + p_ref[2]) + p_ref[3]

    return pl.pallas_call(
        body, name="rs_add_chips", grid=(4,),
        in_specs=[pl.BlockSpec((N_CHIPS, tr, PACK_COLS), lambda i: (0, i, 0))],
        out_specs=pl.BlockSpec((tr, PACK_COLS), lambda i: (i, 0)),
        out_shape=jax.ShapeDtypeStruct((PACK_HALF, PACK_COLS), F32),
        compiler_params=_cparams(("parallel",)),
    )(parts)


def _swap_halves(half):
    def body(h_ref, out_ref, send_sem, recv_sem, local_sem):
        x, y, c = _place()
        my_rows = pl.ds(pl.multiple_of(c * PACK_HALF, 8), PACK_HALF)
        mine = pltpu.make_async_copy(h_ref, out_ref.at[my_rows, :], local_sem)
        mine.start()
        cp = pltpu.make_async_remote_copy(src_ref=h_ref, dst_ref=out_ref.at[my_rows, :], send_sem=send_sem, recv_sem=recv_sem,
                                          device_id=(x, y, 1 - c), device_id_type=MESH_ID)
        cp.start()
        their_rows = pl.ds(pl.multiple_of((1 - c) * PACK_HALF, 8), PACK_HALF)
        pltpu.make_async_remote_copy(src_ref=h_ref, dst_ref=out_ref.at[their_rows, :], send_sem=send_sem, recv_sem=recv_sem,
                                     device_id=(x, y, 1 - c), device_id_type=MESH_ID).wait_recv()
        cp.wait_send()
        mine.wait()

    return pl.pallas_call(
        body, name="rs_swap_halves",
        in_specs=[pl.BlockSpec(memory_space=pltpu.HBM)],
        out_specs=pl.BlockSpec(memory_space=pltpu.HBM),
        out_shape=jax.ShapeDtypeStruct((PACK_TOTAL, PACK_COLS), F32),
        scratch_shapes=[pltpu.SemaphoreType.DMA, pltpu.SemaphoreType.DMA, pltpu.SemaphoreType.DMA],
    )(half)


def _adamw(w, g, m, v):
    tr = PACK_TOTAL // 8

    def body(w_ref, g_ref, m_ref, v_ref, d_ref, nm_ref, nv_ref):
        gv = g_ref[...]
        m_new = ADAM_B1 * m_ref[...] + (1.0 - ADAM_B1) * gv
        v_new = ADAM_B2 * v_ref[...] + (1.0 - ADAM_B2) * (gv * gv)
        m_hat = m_new / (1.0 - ADAM_B1 ** ADAM_STEP)
        v_hat = v_new / (1.0 - ADAM_B2 ** ADAM_STEP)
        d_ref[...] = -ADAM_LR * (m_hat / (jnp.sqrt(v_hat) + ADAM_EPS) + ADAM_WD * w_ref[...])
        nm_ref[...] = m_new
        nv_ref[...] = v_new

    blk = pl.BlockSpec((tr, PACK_COLS), lambda i: (i, 0))
    return pl.pallas_call(
        body, name="adamw", grid=(8,),
        in_specs=[blk] * 4, out_specs=[blk] * 3,
        out_shape=[jax.ShapeDtypeStruct((PACK_TOTAL, PACK_COLS), F32)] * 3,
        compiler_params=_cparams(("parallel",)),
    )(w, g, m, v)


SMALL_ROWS = 64
ADAM_STEPS_PER_HALF = 4


def _pack_small(d):
    flat = jnp.concatenate([d[n].reshape(-1) for n, _ in SMALL_SIZES])
    return jnp.pad(flat, (0, SMALL_ROWS * PACK_COLS - SMALL_TOTAL)).reshape(SMALL_ROWS, PACK_COLS)


def _unpack_small(a):
    flat, out, c0 = a.reshape(-1), {}, 0
    for n, size in SMALL_SIZES:
        out[n] = flat[c0:c0 + size].reshape(1, size)
        c0 += size
    return out


def _split_by_chip(name, full):
    r, c = full.shape
    if name in COL_SHARDED:
        return full.reshape(r, N_CHIPS, c // N_CHIPS).transpose(1, 0, 2)
    return full.reshape(N_CHIPS, r // N_CHIPS, c)


def _join_chips(name, slots):
    _, r, cs = slots.shape
    if name in COL_SHARDED:
        return slots.transpose(1, 0, 2).reshape(r, N_CHIPS * cs)
    return slots.reshape(N_CHIPS * r, cs)


HBM_SPEC = pl.BlockSpec(memory_space=pltpu.HBM)


def _gather_shards(shards):
    n = len(shards)

    def body(*refs):
        w_refs, out_refs, wb_refs = refs[:n], refs[n:2 * n], refs[2 * n:3 * n]
        send_sems, recv_sems, pass_send_sems, pass_recv_sems, local_sems = refs[3 * n:]
        x, y, c = _place()
        me = 2 * x + y
        sibling = (x, y, 1 - c)

        def halves(ref):
            half = ref.shape[-2] // 2
            return (pl.ds(pl.multiple_of(c * half, 16), half), pl.ds(pl.multiple_of((1 - c) * half, 16), half))
        for w_ref, wb_ref in zip(w_refs, wb_refs):
            rows = w_ref.shape[0]
            chunk = min(rows, 128)

            def cast(i, carry, w_ref=w_ref, wb_ref=wb_ref, chunk=chunk):
                r0 = pl.multiple_of(i * chunk, chunk)
                wb_ref[pl.ds(r0, chunk), :] = _bf(w_ref[pl.ds(r0, chunk), :])
                return carry

            lax.fori_loop(0, rows // chunk, cast, 0)
        sends, locals_ = [], []
        for a, (wb_ref, out_ref) in enumerate(zip(wb_refs, out_refs)):
            mine = pltpu.make_async_copy(wb_ref, out_ref.at[me], local_sems.at[a])
            mine.start()
            locals_.append(mine)
            mine_rows, _ = halves(wb_ref)
            for k, (px, py) in enumerate(_other_chips(x, y)):
                cp = pltpu.make_async_remote_copy(src_ref=wb_ref.at[mine_rows, :], dst_ref=out_ref.at[me, mine_rows, :],
                                                  send_sem=send_sems.at[k, a], recv_sem=recv_sems.at[k, a],
                                                  device_id=(px, py, c), device_id_type=MESH_ID)
                cp.start()
                sends.append(cp)
        for a, (wb_ref, out_ref) in enumerate(zip(wb_refs, out_refs)):
            mine_rows, _ = halves(wb_ref)
            for k, (px, py) in enumerate(_other_chips(x, y)):
                landed = out_ref.at[2 * px + py, mine_rows, :]
                pltpu.make_async_remote_copy(src_ref=wb_ref.at[mine_rows, :], dst_ref=landed, send_sem=send_sems.at[k, a],
                                             recv_sem=recv_sems.at[k, a], device_id=(px, py, c),
                                             device_id_type=MESH_ID).wait_recv()
                cp = pltpu.make_async_remote_copy(src_ref=landed, dst_ref=landed, send_sem=pass_send_sems.at[k, a],
                                                  recv_sem=pass_recv_sems.at[k, a], device_id=sibling, device_id_type=MESH_ID)
                cp.start()
                sends.append(cp)
        for a, (wb_ref, out_ref) in enumerate(zip(wb_refs, out_refs)):
            _, their_rows = halves(wb_ref)
            for k, (px, py) in enumerate(_other_chips(x, y)):
                passed = out_ref.at[2 * px + py, their_rows, :]
                pltpu.make_async_remote_copy(src_ref=passed, dst_ref=passed, send_sem=pass_send_sems.at[k, a],
                                             recv_sem=pass_recv_sems.at[k, a], device_id=sibling,
                                             device_id_type=MESH_ID).wait_recv()
        for cp in sends:
            cp.wait_send()
        for cp in locals_:
            cp.wait()

    return pl.pallas_call(
        body, name="gather_weights",
        in_specs=[pl.BlockSpec(memory_space=pltpu.VMEM)] * n,
        out_specs=[HBM_SPEC] * n,
        out_shape=[jax.ShapeDtypeStruct((N_CHIPS,) + s.shape, BF16) for s in shards],
        scratch_shapes=[pltpu.VMEM(s.shape, BF16) for s in shards]
        + [pltpu.SemaphoreType.DMA((3, n))] * 4 + [pltpu.SemaphoreType.DMA((n,))],
        compiler_params=pltpu.CompilerParams(vmem_limit_bytes=VMEM_LIMIT),
    )(*shards)


def _cast_bf16_list(arrays):
    def body(*refs):
        for a_ref, o_ref in zip(refs[:len(arrays)], refs[len(arrays):]):
            o_ref[...] = _bf(a_ref[...])

    specs = [pl.BlockSpec((a.shape[0] // 4, a.shape[1]), lambda i: (i, 0)) for a in arrays]
    return pl.pallas_call(
        body, name="cast_shards", grid=(4,), in_specs=specs, out_specs=specs,
        out_shape=[jax.ShapeDtypeStruct(a.shape, BF16) for a in arrays],
        compiler_params=_cparams(("parallel",)),
    )(*arrays)


def _rs_to_sibling(g4):
    n = len(g4)

    def body(*refs):
        g_refs, out_refs = refs[:n], refs[n:2 * n]
        send_sems, recv_sems = refs[2 * n:]
        x, y, c = _place()
        copies = []
        for a, (g_ref, out_ref) in enumerate(zip(g_refs, out_refs)):
            half = g_ref.shape[1] // 2
            theirs = pl.ds(pl.multiple_of((1 - c) * half, 8), half)
            copies.append(pltpu.make_async_remote_copy(src_ref=g_ref.at[:, theirs, :], dst_ref=out_ref, send_sem=send_sems.at[a],
                                                       recv_sem=recv_sems.at[a], device_id=(x, y, 1 - c),
                                                       device_id_type=MESH_ID))
        for cp in copies:
            cp.start()
        for cp in copies:
            cp.wait()

    return pl.pallas_call(
        body, name="rs_sibling", in_specs=[HBM_SPEC] * n, out_specs=[HBM_SPEC] * n,
        out_shape=[jax.ShapeDtypeStruct((N_CHIPS, g.shape[1] // 2, g.shape[2]), F32) for g in g4],
        scratch_shapes=[pltpu.SemaphoreType.DMA((n,)), pltpu.SemaphoreType.DMA((n,))],
    )(*g4)


def _rs_add_sibling(g4, got, wire_dtypes):
    n = len(g4)
    narrow = [a for a in range(n) if wire_dtypes[a] != F32]

    def body(c_ref, *refs):
        outs = refs[2 * n:3 * n]
        wires = dict(zip(narrow, refs[3 * n:]))
        for a, (g_ref, r_ref, o_ref) in enumerate(zip(refs[:n], refs[n:2 * n], outs)):
            s = g_ref[...] + r_ref[...]
            o_ref[...] = s
            if a in wires:
                wires[a][...] = s.astype(wires[a].dtype)

    blk = lambda r: (1, r.shape[1], r.shape[2])
    plain = lambda r: pl.BlockSpec(blk(r), lambda j, c_ref: (j, 0, 0))
    grid_spec = pltpu.PrefetchScalarGridSpec(
        num_scalar_prefetch=1, grid=(N_CHIPS,),
        in_specs=[pl.BlockSpec(blk(r), lambda j, c_ref: (j, c_ref[0], 0)) for r in got] + [plain(r) for r in got],
        out_specs=[plain(r) for r in got] + [plain(got[a]) for a in narrow])
    res = pl.pallas_call(
        body, name="rs_add_sibling", grid_spec=grid_spec,
        out_shape=[jax.ShapeDtypeStruct(r.shape, F32) for r in got]
        + [jax.ShapeDtypeStruct(got[a].shape, wire_dtypes[a]) for a in narrow],
        compiler_params=_cparams(("parallel",)),
    )(lax.axis_index("c").astype(jnp.int32).reshape(1), *g4, *got)
    chipsum = list(res[:n])
    wire = list(chipsum)
    for a, w in zip(narrow, res[n:]):
        wire[a] = w
    return chipsum, wire


RELATION_XOR = (2, 1, 3)


def _rs_to_chips(wire):
    n = len(wire)

    def body(*refs):
        s_refs, out_refs = refs[:n], refs[n:2 * n]
        send_sems, recv_sems = refs[2 * n:]
        x, y, c = _place()
        sends = []
        for a, (s_ref, out_ref) in enumerate(zip(s_refs, out_refs)):
            for k, (px, py) in enumerate(_other_chips(x, y)):
                cp = pltpu.make_async_remote_copy(src_ref=s_ref.at[2 * px + py], dst_ref=out_ref.at[RELATION_XOR[k] - 1],
                                                  send_sem=send_sems.at[k, a], recv_sem=recv_sems.at[k, a],
                                                  device_id=(px, py, c), device_id_type=MESH_ID)
                cp.start()
                sends.append(cp)
        for cp in sends:
            cp.wait_recv()
        for cp in sends:
            cp.wait_send()

    return pl.pallas_call(
        body, name="rs_chips", in_specs=[HBM_SPEC] * n, out_specs=[HBM_SPEC] * n,
        out_shape=[jax.ShapeDtypeStruct((3,) + s.shape[1:], s.dtype) for s in wire],
        scratch_shapes=[pltpu.SemaphoreType.DMA((3, n)), pltpu.SemaphoreType.DMA((3, n))],
    )(*wire)


def _rs_add_chips(chipsum, parts):
    n = len(parts)

    def body(me_ref, *refs):
        me = me_ref[0]
        for s_ref, p_ref, o_ref in zip(refs[:n], refs[n:2 * n], refs[2 * n:]):
            own = s_ref[0]
            total = None
            for k in range(N_CHIPS):
                theirs = p_ref[jnp.maximum(jnp.bitwise_xor(me, k) - 1, 0)].astype(F32)
                term = jnp.where(me == k, own, theirs)
                total = term if total is None else total + term
            o_ref[...] = total

    grid_spec = pltpu.PrefetchScalarGridSpec(
        num_scalar_prefetch=1, grid=(2,),
        in_specs=[pl.BlockSpec((1, p.shape[1] // 2, p.shape[2]), lambda i, me_ref: (me_ref[0], i, 0)) for p in parts]
        + [pl.BlockSpec((3, p.shape[1] // 2, p.shape[2]), lambda i, me_ref: (0, i, 0)) for p in parts],
        out_specs=[pl.BlockSpec((p.shape[1] // 2, p.shape[2]), lambda i, me_ref: (i, 0)) for p in parts])
    me = (2 * lax.axis_index("x") + lax.axis_index("y")).astype(jnp.int32).reshape(1)
    return pl.pallas_call(
        body, name="rs_add_chips", grid_spec=grid_spec,
        out_shape=[jax.ShapeDtypeStruct(p.shape[1:], F32) for p in parts],
        compiler_params=_cparams(("parallel",)),
    )(me, *chipsum, *parts)


def _rs_swap_halves(halves):
    n = len(halves)

    def body(*refs):
        h_refs, out_refs = refs[:n], refs[n:2 * n]
        send_sems, recv_sems = refs[2 * n:]
        x, y, c = _place()
        copies = [pltpu.make_async_remote_copy(src_ref=h_ref, dst_ref=out_ref, send_sem=send_sems.at[a], recv_sem=recv_sems.at[a],
                                               device_id=(x, y, 1 - c), device_id_type=MESH_ID)
                  for a, (h_ref, out_ref) in enumerate(zip(h_refs, out_refs))]
        for cp in copies:
            cp.start()
        for cp in copies:
            cp.wait()

    return pl.pallas_call(
        body, name="rs_swap_halves", in_specs=[HBM_SPEC] * n, out_specs=[HBM_SPEC] * n,
        out_shape=[jax.ShapeDtypeStruct(h.shape, F32) for h in halves],
        scratch_shapes=[pltpu.SemaphoreType.DMA((n,)), pltpu.SemaphoreType.DMA((n,))],
    )(*halves)


def _adamw_list(ws, g_mine, g_theirs, ms, vs):
    n = len(ws)

    def body(c_ref, *refs):
        w_refs, gm_refs, gt_refs, m_refs, v_refs = (refs[k * n:(k + 1) * n] for k in range(5))
        g_refs, d_refs, nm_refs, nv_refs = (refs[k * n:(k + 1) * n] for k in range(5, 9))
        mine = (pl.program_id(0) // ADAM_STEPS_PER_HALF) == c_ref[0]
        for a in range(n):
            gv = jnp.where(mine, gm_refs[a][...], gt_refs[a][...])
            g_refs[a][...] = gv
            m_new = ADAM_B1 * m_refs[a][...] + (1.0 - ADAM_B1) * gv
            v_new = ADAM_B2 * v_refs[a][...] + (1.0 - ADAM_B2) * (gv * gv)
            m_hat = m_new / (1.0 - ADAM_B1 ** ADAM_STEP)
            v_hat = v_new / (1.0 - ADAM_B2 ** ADAM_STEP)
            d_refs[a][...] = -ADAM_LR * (m_hat / (jnp.sqrt(v_hat) + ADAM_EPS) + ADAM_WD * w_refs[a][...])
            nm_refs[a][...] = m_new
            nv_refs[a][...] = v_new

    steps = 2 * ADAM_STEPS_PER_HALF
    whole = [pl.BlockSpec((w.shape[0] // steps, w.shape[1]), lambda i, c_ref: (i, 0)) for w in ws]
    half = [pl.BlockSpec((w.shape[0] // steps, w.shape[1]), lambda i, c_ref: (i % ADAM_STEPS_PER_HALF, 0)) for w in ws]
    shapes = [jax.ShapeDtypeStruct(w.shape, F32) for w in ws]
    grid_spec = pltpu.PrefetchScalarGridSpec(num_scalar_prefetch=1, grid=(steps,),
                                             in_specs=whole + half + half + whole + whole, out_specs=whole * 4)
    res = pl.pallas_call(
        body, name="adamw", grid_spec=grid_spec, out_shape=shapes * 4,
        compiler_params=_cparams(("parallel",)),
    )(lax.axis_index("c").astype(jnp.int32).reshape(1), *ws, *g_mine, *g_theirs, *ms, *vs)
    return res[:n], res[n:2 * n], res[2 * n:3 * n], res[3 * n:]


WEIGHT_NAMES = ("w_in", "w_mem_kv", "q_a_gain", "w_q_b", "kv_a_gain", "w_kv_b", "w_branch_mla", "w_branch_sb",
                "w_branch_mem", "w_merge_gate", "b_merge_gate", "w_out", "ln_gain", "ln_bias")
BIG_NAMES = tuple(n for n, _ in PACK_ROWS[:-1])
SMALL_NAMES = tuple(n for n, _ in SMALL_SIZES)


def kernel(x, mem, w_in, w_mem_kv, q_a_gain, w_q_b, kv_a_gain, w_kv_b, w_branch_mla, w_branch_sb, w_branch_mem, w_merge_gate, b_merge_gate, w_out, ln_gain, ln_bias, loss_target, m_w_in, m_w_mem_kv, m_q_a_gain, m_w_q_b, m_kv_a_gain, m_w_kv_b, m_w_branch_mla, m_w_branch_sb, m_w_branch_mem, m_w_merge_gate, m_b_merge_gate, m_w_out, m_ln_gain, m_ln_bias, v_w_in, v_w_mem_kv, v_q_a_gain, v_w_q_b, v_kv_a_gain, v_w_kv_b, v_w_branch_mla, v_w_branch_sb, v_w_branch_mem, v_w_merge_gate, v_b_merge_gate, v_w_out, v_ln_gain, v_ln_bias):
    weights = dict(zip(WEIGHT_NAMES, (w_in, w_mem_kv, q_a_gain, w_q_b, kv_a_gain, w_kv_b, w_branch_mla, w_branch_sb,
                                      w_branch_mem, w_merge_gate, b_merge_gate, w_out, ln_gain, ln_bias)))
    mom1 = dict(zip(WEIGHT_NAMES, (m_w_in, m_w_mem_kv, m_q_a_gain, m_w_q_b, m_kv_a_gain, m_w_kv_b, m_w_branch_mla,
                                   m_w_branch_sb, m_w_branch_mem, m_w_merge_gate, m_b_merge_gate, m_w_out, m_ln_gain,
                                   m_ln_bias)))
    mom2 = dict(zip(WEIGHT_NAMES, (v_w_in, v_w_mem_kv, v_q_a_gain, v_w_q_b, v_kv_a_gain, v_w_kv_b, v_w_branch_mla,
                                   v_w_branch_sb, v_w_branch_mem, v_w_merge_gate, v_b_merge_gate, v_w_out, v_ln_gain,
                                   v_ln_bias)))
    def as_list(d):
        return [d[n][0] for n in BIG_NAMES] + [_pack_small({n: d[n] for n in SMALL_NAMES})]

    w_list, m_list, v_list = as_list(weights), as_list(mom1), as_list(mom2)

    gathered = _gather_shards([weights[n][0] for n in LATE_NAMES])
    first_w = {n: _join_chips(n, g) for n, g in zip(LATE_NAMES, gathered)}
    rest_shards = _cast_bf16_list([weights[n][0] for n in EARLY_NAMES])
    small = {n: weights[n] for n in SMALL_NAMES}

    seq = x.shape[1]
    loss, grad_x, late_mine, _, early_mine = _local_step(
        x[0], mem[0], loss_target[0], first_w, small, tq=min(1024, seq), tq_sb_bwd=512, tk=256, t_row=256, t_mm=512,
        t_wg=min(2048, seq), rest_shards=rest_shards)
    by_name = dict(zip(EARLY_NAMES + LATE_NAMES + ("small",), list(early_mine) + list(late_mine)))
    mine = [by_name[n] for n in BIG_NAMES + ("small",)]
    theirs = _rs_swap_halves(mine)
    g_list, d_list, nm_list, nv_list = _adamw_list(w_list, mine, theirs, m_list, v_list)

    loss = lax.psum(loss, ("x", "y", "c"))
    outs = [loss, grad_x[None]]
    for arrays in (g_list, d_list, nm_list, nv_list):
        big = dict(zip(BIG_NAMES, arrays[:-1]))
        sm = _unpack_small(arrays[-1])
        outs.extend(big[n][None] if n in big else sm[n] for n in WEIGHT_NAMES)
    return tuple(outs)
```

```python
import functools
import math

import numpy as np
import jax
import jax.numpy as jnp
from jax import lax
from jax.experimental import pallas as pl
from jax.experimental.pallas import tpu as pltpu

F32 = jnp.float32
BF16 = jnp.bfloat16
MESH_ID = pl.DeviceIdType.MESH

D_MODEL = 1024
MEM_LEN = 256
MLA_HEADS = 8
MLA_NOPE = 64
MLA_ROPE = 32
MLA_V = 64
MLA_Q_LORA = 256
MLA_KV_LORA = 128
SB_HEADS = 8
SB_HEAD_DIM = 64
MEM_HEADS = 4
MEM_HEAD_DIM = 128
BRANCH_WIDTH = 512
ROPE_BASE = 10000.0
RMS_EPS = 1e-6
LN_EPS = 1e-5
DEEPNORM_ALPHA = 2.0 ** 0.25
MLA_SCALE = 1.0 / math.sqrt(MLA_NOPE + MLA_ROPE)
SB_SCALE = 1.0 / math.sqrt(SB_HEAD_DIM)
MEM_SCALE = 1.0 / math.sqrt(MEM_HEAD_DIM)

ADAM_LR = 0.001
ADAM_B1 = 0.9
ADAM_B2 = 0.999
ADAM_EPS = 1e-08
ADAM_WD = 0.01
ADAM_STEP = 10

LANES = 128
HALF = 64
N_CHIPS = 4
PACK_COLS = 1024
VMEM_LIMIT = 56 * 1024 * 1024

IN_WIDTH_P = 4096
BLK_LAT, BLK_GATE_A, BLK_QB, BLK_KB, BLK_VB, BLK_GATE_B, BLK_QM, BLK_GATE_M = range(8)
N_MERGE = 3 * D_MODEL
CAT_WIDTH = N_MERGE + IN_WIDTH_P

PACK_ROWS = (("w_in", 1000), ("w_mem_kv", 256), ("w_q_b", 48), ("w_kv_b", 32), ("w_branch_mla", 128),
             ("w_branch_sb", 128), ("w_branch_mem", 128), ("w_merge_gate", 768), ("w_out", 256), ("small", 8))
PACK_TOTAL = sum(r for _, r in PACK_ROWS)
PACK_HALF = PACK_TOTAL // 2
SMALL_SIZES = (("q_a_gain", 256), ("kv_a_gain", 128), ("b_merge_gate", 3072), ("ln_gain", 1024), ("ln_bias", 1024))
SMALL_TOTAL = sum(s for _, s in SMALL_SIZES)
COL_SHARDED = ("w_in", "w_q_b", "w_kv_b", "w_branch_mla", "w_branch_sb", "w_branch_mem", "w_merge_gate")
ROW_SHARDED = ("w_mem_kv", "w_out")
FULL_SHAPES = {"w_in": (1024, 4000), "w_mem_kv": (1024, 1024), "w_q_b": (256, 768), "w_kv_b": (128, 1024),
               "w_branch_mla": (512, 1024), "w_branch_sb": (512, 1024), "w_branch_mem": (512, 1024),
               "w_merge_gate": (1024, 3072), "w_out": (1024, 1024)}


def _cparams(sem=None):
    return pltpu.CompilerParams(dimension_semantics=sem, vmem_limit_bytes=VMEM_LIMIT)


def _dot(a, b):
    return jnp.dot(a, b, preferred_element_type=F32)


def _dot_nt(a, b):
    return lax.dot_general(a, b, (((1,), (1,)), ((), ())), preferred_element_type=F32)


def _dot_tn(a, b):
    return lax.dot_general(a, b, (((0,), (0,)), ((), ())), preferred_element_type=F32)


def _bf(x):
    return x.astype(BF16)


def _sigmoid(x):
    return 1.0 / (1.0 + jnp.exp(-x))


def _matmul(a, b, *, mode, tm, tn, tk, out_dtypes, name, add=None, add_scale=1.0, b_block0=0, n=None):
    if mode == "nn":
        (m, k), n = a.shape, b.shape[1]
        a_spec = pl.BlockSpec((tm, tk), lambda i, j, kk: (i, kk))
        b_spec = pl.BlockSpec((tk, tn), lambda i, j, kk: (kk, j))
        dot = _dot
    elif mode == "nt":
        (m, k), n = a.shape, b.shape[0]
        a_spec = pl.BlockSpec((tm, tk), lambda i, j, kk: (i, kk))
        b_spec = pl.BlockSpec((tn, tk), lambda i, j, kk: (j, kk))
        dot = _dot_nt
    else:
        (k, m), n = a.shape, (b.shape[1] if n is None else n)
        a_spec = pl.BlockSpec((tk, tm), lambda i, j, kk: (kk, i))
        b_spec = pl.BlockSpec((tk, tn), lambda i, j, kk: (kk, j + b_block0))
        dot = _dot_tn
    assert m % tm == 0 and n % tn == 0 and k % tk == 0, (name, m, n, k)
    nk = k // tk
    n_out = len(out_dtypes)
    has_add = add is not None

    def body(*refs):
        a_ref, b_ref = refs[0], refs[1]
        add_ref = refs[2] if has_add else None
        outs = refs[2 + has_add: 2 + has_add + n_out]
        acc = refs[-1]
        kk = pl.program_id(2)

        @pl.when(kk == 0)
        def _():
            acc[...] = jnp.zeros_like(acc)

        acc[...] += dot(_bf(a_ref[...]), _bf(b_ref[...]))

        @pl.when(kk == nk - 1)
        def _():
            r = acc[...]
            if has_add:
                r = r + add_scale * add_ref[...]
            for o in outs:
                o[...] = r.astype(o.dtype)

    in_specs = [a_spec, b_spec]
    args = [a, b]
    if has_add:
        in_specs.append(pl.BlockSpec((tm, tn), lambda i, j, kk: (i, j)))
        args.append(add)
    out_spec = pl.BlockSpec((tm, tn), lambda i, j, kk: (i, j))
    res = pl.pallas_call(
        body, name=name, grid=(m // tm, n // tn, nk),
        in_specs=in_specs, out_specs=[out_spec] * n_out,
        out_shape=[jax.ShapeDtypeStruct((m, n), dt) for dt in out_dtypes],
        scratch_shapes=[pltpu.VMEM((tm, tn), F32)],
        compiler_params=_cparams(("parallel", "parallel", "arbitrary")),
    )(*args)
    return res


def _rope_tables(seq):
    half = MLA_ROPE // 2
    freqs = ROPE_BASE ** (-jnp.arange(half, dtype=F32) / half)
    ang = jnp.arange(seq, dtype=jnp.int32).astype(F32)[:, None] * freqs[None, :]
    cos, sin = jnp.cos(ang), jnp.sin(ang)
    z = lambda w: jnp.zeros((seq, w), F32)
    c_q = jnp.concatenate([jnp.ones((seq, MLA_NOPE), F32), cos, cos, z(32)], axis=1)
    c_k = jnp.concatenate([z(MLA_NOPE), cos, cos, z(32)], axis=1)
    s_lo = jnp.concatenate([z(MLA_NOPE), -sin, z(half), z(32)], axis=1)
    s_hi = jnp.concatenate([z(MLA_NOPE), z(half), sin, z(32)], axis=1)
    return c_q, c_k, s_lo, s_hi


def _rope_fwd(x, c, s_lo, s_hi):
    return x * c + pltpu.roll(x, LANES - 16, 1) * s_lo + pltpu.roll(x, 16, 1) * s_hi


def _rope_bwd(d, c, s_lo, s_hi):
    return d * c - pltpu.roll(d, 16, 1) * s_hi - pltpu.roll(d, LANES - 16, 1) * s_lo


def _rms_fwd(x, g):
    r = lax.rsqrt(jnp.mean(x * x, axis=-1, keepdims=True) + RMS_EPS)
    xn = x * r
    return xn * g, xn, r


def _mla_prep(p32, gq, gkv, wqb, wkvb, tabs, *, t):
    seq = p32.shape[0]

    def body(lat_ref, gq_ref, gkv_ref, wqb_ref, wkvb_ref, cq_ref, ck_ref, slo_ref, shi_ref, q_ref, k_ref, v_ref):
        lat = lat_ref[...]
        slo, shi = slo_ref[...], shi_ref[...]
        nq, _, _ = _rms_fwd(lat[:, 0:MLA_Q_LORA], gq_ref[...])
        qa = _dot(_bf(nq), wqb_ref[...])
        cq = cq_ref[...]
        for h in range(MLA_HEADS):
            blk = qa[:, h * LANES:(h + 1) * LANES]
            q_ref[:, h * LANES:(h + 1) * LANES] = _bf(_rope_fwd(blk, cq, slo, shi))
        nkv, _, _ = _rms_fwd(lat[:, MLA_Q_LORA:MLA_Q_LORA + MLA_KV_LORA], gkv_ref[...])
        kv = _dot(_bf(nkv), wkvb_ref[...])
        kpe = _rope_fwd(lat[:, 384:512], ck_ref[...], slo, shi)
        for h in range(MLA_HEADS):
            k_ref[:, h * LANES:(h + 1) * LANES] = _bf(kv[:, h * LANES:(h + 1) * LANES] + kpe)
        v_ref[...] = _bf(kv[:, MLA_HEADS * LANES:])

    row = lambda w: pl.BlockSpec((t, w), lambda i: (i, 0))
    full = lambda shp: pl.BlockSpec(shp, lambda i: (0, 0))
    return pl.pallas_call(
        body, name="mla_prep", grid=(seq // t,),
        in_specs=[row(512), full((1, MLA_Q_LORA)), full((1, MLA_KV_LORA)), full(wqb.shape), full(wkvb.shape),
                  row(LANES), row(LANES), row(LANES), row(LANES)],
        out_specs=[row(1024), row(1024), row(512)],
        out_shape=[jax.ShapeDtypeStruct((seq, 1024), BF16), jax.ShapeDtypeStruct((seq, 1024), BF16),
                   jax.ShapeDtypeStruct((seq, 512), BF16)],
        compiler_params=_cparams(("parallel",)),
    )(p32, gq, gkv, wqb, wkvb, *tabs)


def _mla_post(p32, dq, dk, dv, gq, gkv, wqb, wkvb, tabs, *, t):
    seq = p32.shape[0]

    def body(lat_ref, dq_ref, dk_ref, dv_ref, gq_ref, gkv_ref, wqb_ref, wkvb_ref, cq_ref, ck_ref, slo_ref, shi_ref,
             dlat_ref, dwqb_ref, dwkvb_ref, dgq_ref, dgkv_ref):
        @pl.when(pl.program_id(0) == 0)
        def _():
            dwqb_ref[...] = jnp.zeros_like(dwqb_ref)
            dwkvb_ref[...] = jnp.zeros_like(dwkvb_ref)
            dgq_ref[...] = jnp.zeros_like(dgq_ref)
            dgkv_ref[...] = jnp.zeros_like(dgkv_ref)

        lat = lat_ref[...]
        slo, shi = slo_ref[...], shi_ref[...]
        cq = cq_ref[...]
        gq_v, gkv_v = gq_ref[...], gkv_ref[...]
        nq, xq, rq = _rms_fwd(lat[:, 0:MLA_Q_LORA], gq_v)
        nkv, xkv, rkv = _rms_fwd(lat[:, MLA_Q_LORA:MLA_Q_LORA + MLA_KV_LORA], gkv_v)

        dqa = jnp.concatenate(
            [_rope_bwd(dq_ref[:, h * LANES:(h + 1) * LANES], cq, slo, shi) for h in range(MLA_HEADS)], axis=1)
        dqa_b = _bf(dqa)
        dwqb_ref[...] += _dot_tn(_bf(nq), dqa_b)
        dnq = _dot_nt(dqa_b, wqb_ref[...])
        dgq_ref[...] += jnp.sum(dnq * xq, axis=0, keepdims=True)
        dxn = dnq * gq_v
        dcq = rq * (dxn - xq * jnp.mean(dxn * xq, axis=-1, keepdims=True))

        dkf = dk_ref[...]
        dkv_b = _bf(jnp.concatenate([dkf, dv_ref[...]], axis=1))
        dwkvb_ref[...] += _dot_tn(_bf(nkv), dkv_b)
        dnkv = _dot_nt(dkv_b, wkvb_ref[...])
        dgkv_ref[...] += jnp.sum(dnkv * xkv, axis=0, keepdims=True)
        dxn = dnkv * gkv_v
        dckv = rkv * (dxn - xkv * jnp.mean(dxn * xkv, axis=-1, keepdims=True))

        dkpe = dkf[:, 0:LANES]
        for h in range(1, MLA_HEADS):
            dkpe = dkpe + dkf[:, h * LANES:(h + 1) * LANES]
        dkr = _rope_bwd(dkpe, ck_ref[...], slo, shi)
        dlat_ref[...] = _bf(jnp.concatenate([dcq, dckv, dkr], axis=1))

    row = lambda w: pl.BlockSpec((t, w), lambda i: (i, 0))
    full = lambda shp: pl.BlockSpec(shp, lambda i: (0, 0))
    return pl.pallas_call(
        body, name="mla_post", grid=(seq // t,),
        in_specs=[row(512), row(1024), row(1024), row(512), full((1, MLA_Q_LORA)), full((1, MLA_KV_LORA)),
                  full(wqb.shape), full(wkvb.shape), row(LANES), row(LANES), row(LANES), row(LANES)],
        out_specs=[row(512), full(wqb.shape), full(wkvb.shape), full((1, MLA_Q_LORA)), full((1, MLA_KV_LORA))],
        out_shape=[jax.ShapeDtypeStruct((seq, 512), BF16), jax.ShapeDtypeStruct(wqb.shape, F32),
                   jax.ShapeDtypeStruct(wkvb.shape, F32), jax.ShapeDtypeStruct((1, MLA_Q_LORA), F32),
                   jax.ShapeDtypeStruct((1, MLA_KV_LORA), F32)],
        compiler_params=_cparams(("arbitrary",)),
    )(p32, dq, dk, dv, gq, gkv, wqb, wkvb, *tabs)


def _split_bf16(x):
    hi = _bf(x)
    return hi, _bf(x - hi.astype(F32))


def _tri_sum(x, u):
    hi, lo = _split_bf16(x)
    return _dot(hi, u) + _dot(lo, u)


def _softplus(z):
    return jnp.maximum(z, 0.0) + jnp.log(1.0 + jnp.exp(-jnp.abs(z)))


def _head_queries(q, left):
    zero = jnp.zeros_like(q)
    return jnp.where(left, q, zero) * SB_SCALE, jnp.where(left, zero, q) * SB_SCALE


ROW_GROUP = 128
SB_BWD_CHAINS_IN_FLIGHT = 8
ANY_HBM = pl.BlockSpec(memory_space=pltpu.HBM)


class _Exchange:
    def __init__(self, send, landing):
        self.send, self.landing = send, landing

    def start(self):
        self.send.start()

    def wait(self):
        self.landing.wait_recv()
        self.send.wait_send()


class _Rider:
    def __init__(self, operands, out_shapes, sem_shapes, copies):
        self.operands, self.out_shapes, self.sem_shapes, self.copies = list(operands), list(out_shapes), list(sem_shapes), copies


def _call_with_rider(body, rider, *, name, grid, in_specs, out_specs, out_shape, args, semantics, scratch=()):
    scratch = list(scratch)
    if rider is None:
        return pl.pallas_call(body, name=name, grid=grid, in_specs=in_specs, out_specs=out_specs, out_shape=out_shape,
                              scratch_shapes=scratch, compiler_params=_cparams(semantics))(*args)
    n_in, n_out, n_rin, n_rout = len(in_specs), len(out_specs), len(rider.operands), len(rider.out_shapes)

    def full_body(*refs):
        ins, r_ins = refs[:n_in], refs[n_in:n_in + n_rin]
        outs = refs[n_in + n_rin:n_in + n_rin + n_out]
        r_outs = refs[n_in + n_rin + n_out:n_in + n_rin + n_out + n_rout]
        rest = refs[n_in + n_rin + n_out + n_rout:]
        own_scratch, sems = rest[:len(scratch)], rest[len(scratch):]
        first, last = None, None
        for axis, size in enumerate(grid):
            at_start, at_end = pl.program_id(axis) == 0, pl.program_id(axis) == size - 1
            first = at_start if first is None else first & at_start
            last = at_end if last is None else last & at_end

        @pl.when(first)
        def _():
            for cp in rider.copies(r_ins, r_outs, sems):
                cp.start()

        body(*ins, *outs, *own_scratch)

        @pl.when(last)
        def _():
            for cp in rider.copies(r_ins, r_outs, sems):
                cp.wait()

    return pl.pallas_call(
        full_body, name=name, grid=grid, in_specs=list(in_specs) + [ANY_HBM] * n_rin,
        out_specs=list(out_specs) + [ANY_HBM] * n_rout, out_shape=list(out_shape) + rider.out_shapes,
        scratch_shapes=scratch + rider.sem_shapes, compiler_params=_cparams(("arbitrary",) * len(grid)),
    )(*args, *rider.operands)


def _chains(tq):
    return [(h, g) for g in range(tq // ROW_GROUP) for h in range(2)]


def _chain_pattern(g, m, tk, strict):
    r_lo, r_hi = g * ROW_GROUP, (g + 1) * ROW_GROUP - 1
    c_lo, c_hi = m * tk, (m + 1) * tk - 1
    if (c_lo >= r_hi) if strict else (c_lo > r_hi):
        return None
    if (c_hi < r_lo) if strict else (c_hi <= r_lo):
        return True
    rr = lax.broadcasted_iota(jnp.int32, (ROW_GROUP, tk), 0) + r_lo
    cc = lax.broadcasted_iota(jnp.int32, (ROW_GROUP, tk), 1) + c_lo
    return (cc < rr) if strict else (cc <= rr)


def _masked(x, pat, fill=0.0):
    return x if pat is True else jnp.where(pat, x, fill)


def _rows(g):
    return slice(g * ROW_GROUP, (g + 1) * ROW_GROUP)


def _tri_matrix(tk, cmp):
    rr = lax.broadcasted_iota(jnp.int32, (tk, tk), 0)
    cc = lax.broadcasted_iota(jnp.int32, (tk, tk), 1)
    return cmp(rr, cc).astype(BF16)


def _mla_attn_fwd(qp, kp, vp, *, tq, tk, tk_diag, rider=None):
    seq = qp.shape[0]
    neg = float(np.finfo(np.float32).min)
    chains = _chains(tq)

    def body(q_ref, k_ref, v_ref, o_ref, lse_ref):
        i = pl.program_id(1)
        left = lax.broadcasted_iota(jnp.int32, (tq, LANES), 1) < HALF
        qs = [q_ref[_rows(g), h * LANES:(h + 1) * LANES] for h, g in chains]

        def block(start, carry, m, tk):
            v = v_ref[pl.ds(start, tk), :]
            pats = [True if m is None else _chain_pattern(g, m, tk, False) for _, g in chains]
            live = [n for n, p in enumerate(pats) if p is not None]
            ss = {n: _dot_nt(qs[n], k_ref[pl.ds(start, tk), chains[n][0] * LANES:(chains[n][0] + 1) * LANES]) for n in live}
            new = list(carry)
            for n in live:
                m_old, l_old, acc = carry[n]
                s = _masked(ss[n] * MLA_SCALE, pats[n], neg)
                m_new = jnp.maximum(m_old, jnp.max(s, axis=-1, keepdims=True))
                a = jnp.exp(m_old - m_new)
                p = jnp.exp(s - m_new)
                new[n] = (m_new, a * l_old + jnp.sum(p, axis=-1, keepdims=True), a * acc + _dot(_bf(p), v))
            return tuple(new)

        init = (jnp.full((ROW_GROUP, 1), -1e30, F32), jnp.zeros((ROW_GROUP, 1), F32), jnp.zeros((ROW_GROUP, LANES), F32))
        def two_blocks(j, c):
            c = block(pl.multiple_of(2 * j * tk, tk), c, None, tk)
            return block(pl.multiple_of((2 * j + 1) * tk, tk), c, None, tk)

        carry = lax.fori_loop(0, i * (tq // tk) // 2, two_blocks, (init,) * len(chains))
        for m in range(tq // tk_diag):
            carry = block(pl.multiple_of(i * tq + m * tk_diag, tk_diag), carry, m, tk_diag)
        per_head = []
        for h in range(2):
            mine = [carry[n] for n, (ch, _) in enumerate(chains) if ch == h]
            per_head.append((jnp.concatenate([acc / l for _, l, acc in mine], axis=0),
                             jnp.concatenate([mm + jnp.log(l) for mm, l, _ in mine], axis=0)))
        o_ref[...] = jnp.where(left, per_head[0][0], per_head[1][0])
        lse_ref[...] = jnp.where(left, per_head[0][1], per_head[1][1])

    return _call_with_rider(
        body, rider, name="mla_fwd", grid=(MLA_HEADS // 2, seq // tq),
        in_specs=[pl.BlockSpec((tq, 2 * LANES), lambda p, i: (i, p)), pl.BlockSpec((seq, 2 * LANES), lambda p, i: (0, p)),
                  pl.BlockSpec((seq, LANES), lambda p, i: (0, p))],
        out_specs=[pl.BlockSpec((tq, LANES), lambda p, i: (i, p)), pl.BlockSpec((tq, LANES), lambda p, i: (i, p))],
        out_shape=[jax.ShapeDtypeStruct((seq, 512), F32), jax.ShapeDtypeStruct((seq, 512), F32)],
        args=(qp, kp, vp), semantics=("parallel", "parallel"))


def _mla_attn_bwd(qp, kp, vp, o, lse, do, *, tq, tk, tk_diag, rider=None):
    seq = qp.shape[0]
    chains = _chains(tq)

    def body(q_ref, k_ref, v_ref, o_ref, lse_ref, do_ref, dq_ref, dk_ref, dv_ref, qt_ref, dot_ref):
        i = pl.program_id(1)

        @pl.when(i == 0)
        def _():
            dk_ref[...] = jnp.zeros_like(dk_ref)
            dv_ref[...] = jnp.zeros_like(dv_ref)

        left = lax.broadcasted_iota(jnp.int32, (tq, LANES), 1) < HALF
        do_f = do_ref[...]
        prod = do_f * o_ref[...]
        lse_v = lse_ref[...]
        do_heads = (_bf(jnp.where(left, do_f, 0.0)), _bf(jnp.where(left, 0.0, do_f)))
        delta_heads = (jnp.sum(jnp.where(left, prod, 0.0), axis=-1, keepdims=True),
                       jnp.sum(jnp.where(left, 0.0, prod), axis=-1, keepdims=True))
        qs = [q_ref[_rows(g), h * LANES:(h + 1) * LANES] for h, g in chains]
        dos = [do_heads[h][_rows(g)] for h, g in chains]
        deltas = [delta_heads[h][_rows(g)] for h, g in chains]
        lses = [lse_v[_rows(g), h * HALF:h * HALF + 1] for h, g in chains]
        for h in range(2):
            qt_ref[h] = q_ref[:, h * LANES:(h + 1) * LANES].T
            dot_ref[h] = do_heads[h].T
        q_t = [qt_ref.at[h] for h in range(2)]
        do_t = [dot_ref.at[h] for h in range(2)]

        def block(start, carry, m, tk):
            v = v_ref[pl.ds(start, tk), :]
            pats = [True if m is None else _chain_pattern(g, m, tk, False) for _, g in chains]
            live = [n for n, p in enumerate(pats) if p is not None]
            ks = [k_ref[pl.ds(start, tk), h * LANES:(h + 1) * LANES] for h in range(2)]
            ss = {n: _dot_nt(qs[n], ks[chains[n][0]]) for n in live}
            dps = {n: _dot_nt(dos[n], v) for n in live}
            new = list(carry)
            ps, dss = {}, {}
            for n in live:
                p = _masked(jnp.exp(ss[n] * MLA_SCALE - lses[n]), pats[n])
                ps[n] = _bf(p)
                dss[n] = _bf(p * (dps[n] - deltas[n]) * MLA_SCALE)
                new[n] = carry[n] + _dot(dss[n], ks[chains[n][0]])
            dv_t, dk_t = None, []
            for h in range(2):
                mine = [n for n in live if chains[n][0] == h]
                first_row = chains[mine[0]][1] * ROW_GROUP
                ds_cat = jnp.concatenate([dss[n] for n in mine], axis=0)
                p_cat = jnp.concatenate([ps[n] for n in mine], axis=0)
                if first_row == 0:
                    q_rows_t, do_rows_t = q_t[h][...], do_t[h][...]
                else:
                    q_rows_t = q_ref[first_row:, h * LANES:(h + 1) * LANES].T
                    do_rows_t = do_heads[h][first_row:].T
                dk_t.append(_dot(q_rows_t, ds_cat))
                term = _dot(do_rows_t, p_cat)
                dv_t = term if dv_t is None else dv_t + term
            back = jnp.concatenate(dk_t + [dv_t], axis=0).T
            dk_ref[pl.ds(start, tk), :] += back[:, :2 * LANES]
            dv_ref[pl.ds(start, tk), :] += back[:, 2 * LANES:]
            return tuple(new)

        zero = jnp.zeros((ROW_GROUP, LANES), F32)
        carry = lax.fori_loop(0, i * (tq // tk), lambda j, c: block(pl.multiple_of(j * tk, tk), c, None, tk),
                              (zero,) * len(chains))
        for m in range(tq // tk_diag):
            carry = block(pl.multiple_of(i * tq + m * tk_diag, tk_diag), carry, m, tk_diag)
        for n, (h, g) in enumerate(chains):
            dq_ref[_rows(g), h * LANES:(h + 1) * LANES] = carry[n]

    two_t = pl.BlockSpec((tq, 2 * LANES), lambda p, i: (i, p))
    two_s = pl.BlockSpec((seq, 2 * LANES), lambda p, i: (0, p))
    pair_t = pl.BlockSpec((tq, LANES), lambda p, i: (i, p))
    pair_s = pl.BlockSpec((seq, LANES), lambda p, i: (0, p))
    return _call_with_rider(
        body, rider, name="mla_bwd", grid=(MLA_HEADS // 2, seq // tq),
        in_specs=[two_t, two_s, pair_s, pair_t, pair_t, pair_t],
        out_specs=[two_t, two_s, pair_s],
        out_shape=[jax.ShapeDtypeStruct((seq, 1024), F32), jax.ShapeDtypeStruct((seq, 1024), F32),
                   jax.ShapeDtypeStruct((seq, 512), F32)],
        args=(qp, kp, vp, o, lse, do), semantics=("parallel", "arbitrary"),
        scratch=[pltpu.VMEM((2, LANES, tq), BF16), pltpu.VMEM((2, LANES, tq), BF16)])


def _sb_attn_fwd(pbf, *, tq, tk):
    seq = pbf.shape[0]
    nd = tq // tk
    qb, kb, vb = BLK_QB * 4, BLK_KB * 4, BLK_VB * 4
    chains = _chains(tq)

    def body(q_ref, k_ref, v_ref, o_ref, tot_ref):
        i = pl.program_id(1)
        u_later = _tri_matrix(tk, lambda r, c: r > c)
        left = lax.broadcasted_iota(jnp.int32, (tq, LANES), 1) < HALF
        q_heads = _head_queries(q_ref[...], left)
        qs = [q_heads[h][_rows(g)] for h, g in chains]

        def block(j, carry, m):
            start = pl.multiple_of(j * tk, tk)
            k = k_ref[pl.ds(start, tk), :]
            v = v_ref[pl.ds(start, tk), :]
            pats = [True if m is None else _chain_pattern(g, m, tk, True) for _, g in chains]
            live = [n for n, p in enumerate(pats) if p is not None]
            zs = {n: _dot_nt(qs[n], k) for n in live}
            raws = {n: _softplus(zs[n]) for n in live}
            sps = {n: _masked(raws[n], pats[n]) for n in live}
            laters = {n: _tri_sum(sps[n], u_later) for n in live}
            new = list(carry)
            for n in live:
                c, acc = carry[n]
                a = _masked(jnp.exp(zs[n] - raws[n] - laters[n] - c), pats[n])
                new[n] = (c + laters[n][:, 0:1] + sps[n][:, 0:1], acc + _dot(_bf(a), v))
            return tuple(new)

        init = (jnp.zeros((ROW_GROUP, 1), F32), jnp.zeros((ROW_GROUP, LANES), F32))
        carry = (init,) * len(chains)
        for m in reversed(range(nd)):
            carry = block(i * nd + m, carry, m)
        carry = lax.fori_loop(0, i * nd // 2,
                              lambda jj, cr: block(i * nd - 2 - 2 * jj, block(i * nd - 1 - 2 * jj, cr, None), None), carry)
        per_head = []
        for h in range(2):
            mine = [carry[n] for n, (ch, _) in enumerate(chains) if ch == h]
            per_head.append((jnp.concatenate([acc for _, acc in mine], axis=0), jnp.concatenate([c for c, _ in mine], axis=0)))
        o_ref[...] = jnp.where(left, per_head[0][0], per_head[1][0])
        tot_ref[...] = jnp.where(left, per_head[0][1], per_head[1][1])

    pair_t = pl.BlockSpec((tq, LANES), lambda p, i: (i, p))
    return pl.pallas_call(
        body, name="sb_fwd", grid=(SB_HEADS // 2, seq // tq),
        in_specs=[pl.BlockSpec((tq, LANES), lambda p, i: (i, qb + p)), pl.BlockSpec((seq, LANES), lambda p, i: (0, kb + p)),
                  pl.BlockSpec((seq, LANES), lambda p, i: (0, vb + p))],
        out_specs=[pair_t, pair_t],
        out_shape=[jax.ShapeDtypeStruct((seq, 512), F32), jax.ShapeDtypeStruct((seq, 512), F32)],
        compiler_params=_cparams(("parallel", "parallel")),
    )(pbf, pbf, pbf)


def _sb_attn_bwd(pbf, tot, do, *, tq, tk, rider=None):
    seq = pbf.shape[0]
    nd = tq // tk
    qb, kb, vb = BLK_QB * 4, BLK_KB * 4, BLK_VB * 4
    chains = _chains(tq)
    group = SB_BWD_CHAINS_IN_FLIGHT

    def body(q_ref, k_ref, v_ref, tot_ref, do_ref, dq_ref, dk_ref, dv_ref, qt_ref, dot_ref):
        i = pl.program_id(1)

        @pl.when(i == 0)
        def _():
            dk_ref[...] = jnp.zeros_like(dk_ref)
            dv_ref[...] = jnp.zeros_like(dv_ref)

        u_upto = _tri_matrix(tk, lambda r, c: r <= c)
        u_below = _tri_matrix(tk, lambda r, c: r < c)
        left = lax.broadcasted_iota(jnp.int32, (tq, LANES), 1) < HALF
        q_heads = _head_queries(q_ref[...], left)
        do_f = do_ref[...]
        do_heads = (_bf(jnp.where(left, do_f, 0.0)), _bf(jnp.where(left, 0.0, do_f)))
        tot_v = tot_ref[...]
        qs = [q_heads[h][_rows(g)] for h, g in chains]
        dos = [do_heads[h][_rows(g)] for h, g in chains]
        totals = [tot_v[_rows(g), h * HALF:h * HALF + 1] for h, g in chains]
        qt_ref[...] = jnp.concatenate(qs, axis=0).T
        dot_ref[...] = jnp.concatenate(dos, axis=0).T

        def block(j, carry, m):
            start = pl.multiple_of(j * tk, tk)
            k = k_ref[pl.ds(start, tk), :]
            v = v_ref[pl.ds(start, tk), :]
            pats = [True if m is None else _chain_pattern(g, m, tk, True) for _, g in chains]
            all_live = [n for n, p in enumerate(pats) if p is not None]
            new = list(carry)
            for g0 in range(0, len(all_live), group):
                live = all_live[g0:g0 + group]
                zs = {n: _dot_nt(qs[n], k) for n in live}
                das = {n: _dot_nt(dos[n], v) for n in live}
                raws = {n: _softplus(zs[n]) for n in live}
                sps = {n: _masked(raws[n], pats[n]) for n in live}
                uptos = {n: _tri_sum(sps[n], u_upto) for n in live}
                lbs, a_s, gs = {}, {}, {}
                for n in live:
                    lbs[n] = zs[n] - raws[n]
                    a = _masked(jnp.exp(lbs[n] - (totals[n] - carry[n][0] - uptos[n])), pats[n])
                    a_s[n] = _bf(a)
                    gs[n] = das[n] * a
                belows = {n: _dot(_bf(gs[n]), u_below) for n in live}
                dzs = {}
                for n in live:
                    sp_before, g_before, dq_acc = carry[n]
                    beta = jnp.exp(lbs[n])
                    dz = _masked(gs[n] * (1.0 - beta) - (g_before + belows[n]) * beta, pats[n])
                    dzs[n] = _bf(dz)
                    new[n] = (sp_before + uptos[n][:, tk - 1:tk], g_before + belows[n][:, tk - 1:tk] + gs[n][:, tk - 1:tk],
                              dq_acc + _dot(dzs[n], k))
                dz_cat = jnp.concatenate([dzs[n] for n in live], axis=0)
                a_cat = jnp.concatenate([a_s[n] for n in live], axis=0)
                if len(live) == len(chains):
                    q_rows_t, do_rows_t = qt_ref[...], dot_ref[...]
                else:
                    q_rows_t = jnp.concatenate([qs[n] for n in live], axis=0).T
                    do_rows_t = jnp.concatenate([dos[n] for n in live], axis=0).T
                both = jnp.concatenate([_dot(q_rows_t, dz_cat), _dot(do_rows_t, a_cat)], axis=0).T
                dk_ref[pl.ds(start, tk), :] += both[:, :LANES]
                dv_ref[pl.ds(start, tk), :] += both[:, LANES:]
            return tuple(new)

        zero = jnp.zeros((ROW_GROUP, 1), F32)
        init = (zero, zero, jnp.zeros((ROW_GROUP, LANES), F32))
        carry = lax.fori_loop(0, i * nd // 2, lambda j, cr: block(2 * j + 1, block(2 * j, cr, None), None),
                              (init,) * len(chains))
        for m in range(nd):
            carry = block(i * nd + m, carry, m)
        per_head = [jnp.concatenate([carry[n][2] for n, (ch, _) in enumerate(chains) if ch == h], axis=0) for h in range(2)]
        dq_ref[...] = jnp.where(left, per_head[0], per_head[1]) * SB_SCALE

    pair_t = pl.BlockSpec((tq, LANES), lambda p, i: (i, p))
    pair_s = pl.BlockSpec((seq, LANES), lambda p, i: (0, p))
    return _call_with_rider(
        body, rider, name="sb_bwd", grid=(SB_HEADS // 2, seq // tq),
        in_specs=[pl.BlockSpec((tq, LANES), lambda p, i: (i, qb + p)), pl.BlockSpec((seq, LANES), lambda p, i: (0, kb + p)),
                  pl.BlockSpec((seq, LANES), lambda p, i: (0, vb + p)), pair_t, pair_t],
        out_specs=[pair_t, pair_s, pair_s],
        out_shape=[jax.ShapeDtypeStruct((seq, 512), F32)] * 3,
        args=(pbf, pbf, pbf, tot, do), semantics=("parallel", "arbitrary"),
        scratch=[pltpu.VMEM((LANES, 2 * tq), BF16), pltpu.VMEM((LANES, 2 * tq), BF16)])


def _mem_probs(q, k):
    s = _dot_nt(q, k) * MEM_SCALE
    e = jnp.exp(s - jnp.max(s, axis=-1, keepdims=True))
    return e / jnp.sum(e, axis=-1, keepdims=True)


def _mem_fwd(pbf, mkv, *, t):
    seq = pbf.shape[0]

    def body(q_ref, kv_ref, o_ref):
        for h in range(MEM_HEADS):
            lo, hi = h * LANES, (h + 1) * LANES
            p = _mem_probs(q_ref[:, lo:hi], kv_ref[:, lo:hi])
            o_ref[:, lo:hi] = _dot(_bf(p), kv_ref[:, 512 + lo:512 + hi])

    return pl.pallas_call(
        body, name="mem_fwd", grid=(seq // t,),
        in_specs=[pl.BlockSpec((t, 512), lambda i: (i, BLK_QM)), pl.BlockSpec((MEM_LEN, 1024), lambda i: (0, 0))],
        out_specs=pl.BlockSpec((t, 512), lambda i: (i, 0)),
        out_shape=jax.ShapeDtypeStruct((seq, 512), F32),
        compiler_params=_cparams(("parallel",)),
    )(pbf, mkv)


def _mem_bwd(pbf, mkv, do, *, t):
    seq = pbf.shape[0]

    def body(q_ref, kv_ref, do_ref, dq_ref, dkv_ref):
        @pl.when(pl.program_id(0) == 0)
        def _():
            dkv_ref[...] = jnp.zeros_like(dkv_ref)

        for h in range(MEM_HEADS):
            lo, hi = h * LANES, (h + 1) * LANES
            q, k, v = q_ref[:, lo:hi], kv_ref[:, lo:hi], kv_ref[:, 512 + lo:512 + hi]
            do_h = _bf(do_ref[:, lo:hi])
            p = _mem_probs(q, k)
            dp = _dot_nt(do_h, v)
            ds = _bf(p * (dp - jnp.sum(dp * p, axis=-1, keepdims=True)) * MEM_SCALE)
            dq_ref[:, lo:hi] = _dot(ds, k)
            dkv_ref[:, lo:hi] += _dot_tn(ds, q)
            dkv_ref[:, 512 + lo:512 + hi] += _dot_tn(_bf(p), do_h)

    return pl.pallas_call(
        body, name="mem_bwd", grid=(seq // t,),
        in_specs=[pl.BlockSpec((t, 512), lambda i: (i, BLK_QM)), pl.BlockSpec((MEM_LEN, 1024), lambda i: (0, 0)),
                  pl.BlockSpec((t, 512), lambda i: (i, 0))],
        out_specs=[pl.BlockSpec((t, 512), lambda i: (i, 0)), pl.BlockSpec((MEM_LEN, 1024), lambda i: (0, 0))],
        out_shape=[jax.ShapeDtypeStruct((seq, 512), F32), jax.ShapeDtypeStruct((MEM_LEN, 1024), F32)],
        compiler_params=_cparams(("arbitrary",)),
    )(pbf, mkv, do)


def _mid(x, tgt, o_a, o_b, o_m, p32, wmg, bmg, wba, wbb, wbm, wout, ln_g, ln_b, *, t):
    seq = x.shape[0]
    inv_d = 1.0 / D_MODEL

    def body(x_ref, t_ref, oa_ref, ob_ref, om_ref, ga_ref, gb_ref, gm_ref, wmg_ref, bmg_ref, wba_ref, wbb_ref,
             wbm_ref, wout_ref, lg_ref, lb_ref,
             du_ref, mrg_ref, dgp_ref, ha_ref, hb_ref, hm_ref, dya_ref, dyb_ref, dym_ref, doa_ref, dob_ref, dom_ref,
             dga_ref, dgb_ref, dgm_ref, dgain_ref, dbias_ref, dbmg_ref, loss_ref):
        @pl.when(pl.program_id(0) == 0)
        def _():
            dgain_ref[...] = jnp.zeros_like(dgain_ref)
            dbias_ref[...] = jnp.zeros_like(dbias_ref)
            dbmg_ref[...] = jnp.zeros_like(dbmg_ref)
            loss_ref[...] = jnp.zeros_like(loss_ref)

        xv = x_ref[...]
        gate = _sigmoid(_dot(_bf(xv), wmg_ref[...]) + bmg_ref[...])

        branches = []
        merged = None
        for b, (o_ref, g_ref, w_ref, h_ref) in enumerate(((oa_ref, ga_ref, wba_ref, ha_ref), (ob_ref, gb_ref, wbb_ref, hb_ref),
                                                         (om_ref, gm_ref, wbm_ref, hm_ref))):
            o, gt = o_ref[...], g_ref[...]
            sg = _sigmoid(gt)
            silu = gt * sg
            h = _bf(o * silu)
            h_ref[...] = h
            y = _dot(h, w_ref[...])
            g_b = gate[:, b * D_MODEL:(b + 1) * D_MODEL]
            term = g_b * y
            merged = term if merged is None else merged + term
            branches.append((o, gt, sg, silu, y, g_b))
        mrg_b = _bf(merged)
        mrg_ref[...] = mrg_b

        u = DEEPNORM_ALPHA * xv + _dot(mrg_b, wout_ref[...])
        mu = jnp.mean(u, axis=-1, keepdims=True)
        uc = u - mu
        rstd = lax.rsqrt(jnp.mean(uc * uc, axis=-1, keepdims=True) + LN_EPS)
        xhat = uc * rstd
        lg = lg_ref[...]
        y_out = xhat * lg + lb_ref[...]
        err = y_out - t_ref[...]
        loss_ref[...] += 0.5 * jnp.sum(jnp.mean(err * err, axis=-1, keepdims=True), axis=0, keepdims=True)
        dy = err * inv_d
        dgain_ref[...] += jnp.sum(dy * xhat, axis=0, keepdims=True)
        dbias_ref[...] += jnp.sum(dy, axis=0, keepdims=True)
        dxh = dy * lg
        du = rstd * (dxh - jnp.mean(dxh, axis=-1, keepdims=True) - xhat * jnp.mean(dxh * xhat, axis=-1, keepdims=True))
        du_ref[...] = du

        dmerged = _dot_nt(_bf(du), wout_ref[...])
        outs = ((dya_ref, doa_ref, dga_ref, wba_ref), (dyb_ref, dob_ref, dgb_ref, wbb_ref), (dym_ref, dom_ref, dgm_ref, wbm_ref))
        dgp = []
        for (o, gt, sg, silu, y, g_b), (dy_ref, do_ref, dg_ref, w_ref) in zip(branches, outs):
            dyb = _bf(dmerged * g_b)
            dy_ref[...] = dyb
            dgp.append(dmerged * y * g_b * (1.0 - g_b))
            dh = _dot_nt(dyb, w_ref[...])
            do_ref[...] = dh * silu
            dg_ref[...] = _bf(dh * o * (sg * (1.0 + gt * (1.0 - sg))))
        dgp = jnp.concatenate(dgp, axis=1)
        dgp_ref[...] = _bf(dgp)
        dbmg_ref[...] += jnp.sum(dgp, axis=0, keepdims=True)

    row = lambda w: pl.BlockSpec((t, w), lambda i: (i, 0))
    pblk = lambda c: pl.BlockSpec((t, 512), lambda i: (i, c))
    full = lambda shp: pl.BlockSpec(shp, lambda i: (0, 0))
    sds = jax.ShapeDtypeStruct
    return pl.pallas_call(
        body, name="mid", grid=(seq // t,),
        in_specs=[row(1024), row(1024), row(512), row(512), row(512), pblk(BLK_GATE_A), pblk(BLK_GATE_B), pblk(BLK_GATE_M),
                  full(wmg.shape), full((1, N_MERGE)), full(wba.shape), full(wbb.shape), full(wbm.shape), full(wout.shape),
                  full((1, D_MODEL)), full((1, D_MODEL))],
        out_specs=[row(1024), row(1024), row(N_MERGE), row(512), row(512), row(512), row(1024), row(1024), row(1024),
                   row(512), row(512), row(512), row(512), row(512), row(512),
                   full((1, D_MODEL)), full((1, D_MODEL)), full((1, N_MERGE)), full((1, 1))],
        out_shape=[sds((seq, 1024), F32), sds((seq, 1024), BF16), sds((seq, N_MERGE), BF16),
                   sds((seq, 512), BF16), sds((seq, 512), BF16), sds((seq, 512), BF16),
                   sds((seq, 1024), BF16), sds((seq, 1024), BF16), sds((seq, 1024), BF16),
                   sds((seq, 512), F32), sds((seq, 512), F32), sds((seq, 512), F32),
                   sds((seq, 512), BF16), sds((seq, 512), BF16), sds((seq, 512), BF16),
                   sds((1, D_MODEL), F32), sds((1, D_MODEL), F32), sds((1, N_MERGE), F32), sds((1, 1), F32)],
        compiler_params=_cparams(("arbitrary",)),
    )(x, tgt, o_a, o_b, o_m, p32, p32, p32, wmg, bmg, wba, wbb, wbm, wout, ln_g, ln_b)


def _primed_weights(w):
    w_in = w["w_in"]
    zc = lambda n: jnp.zeros((D_MODEL, n), w_in.dtype)
    w_in_p = jnp.concatenate([w_in[:, 0:384], zc(64), w_in[:, 384:416], zc(32), w_in[:, 416:]], axis=1)
    wqb = jnp.pad(w["w_q_b"].reshape(MLA_Q_LORA, MLA_HEADS, 96), ((0, 0), (0, 0), (0, 32))).reshape(MLA_Q_LORA, 1024)
    kv3 = w["w_kv_b"].reshape(MLA_KV_LORA, MLA_HEADS, 128)
    wk = jnp.pad(kv3[:, :, :MLA_NOPE], ((0, 0), (0, 0), (0, 64))).reshape(MLA_KV_LORA, 1024)
    wv = kv3[:, :, MLA_NOPE:].reshape(MLA_KV_LORA, 512)
    return w_in_p, wqb, jnp.concatenate([wk, wv], axis=1)


PROJ_BLK = 512
N_GATE_BLKS = N_MERGE // PROJ_BLK


def _grad_x(du, dgpre, wmg, d_proj, w_in_p, *, tm, rider=None):
    seq = du.shape[0]
    n_pieces = len(d_proj)

    def body(du_ref, dg_ref, wmg_ref, *rest):
        piece_refs, win_ref, out_ref = rest[:n_pieces], rest[n_pieces], rest[n_pieces + 1]
        d_p = jnp.concatenate([_bf(p_ref[...]) for p_ref in piece_refs], axis=1)
        out_ref[...] = (DEEPNORM_ALPHA * du_ref[...] + _dot_nt(dg_ref[...], wmg_ref[...])) + _dot_nt(d_p, win_ref[...])

    row = lambda w: pl.BlockSpec((tm, w), lambda i: (i, 0))
    whole = lambda a: pl.BlockSpec(a.shape, lambda i: (0, 0))
    return _call_with_rider(
        body, rider, name="grad_x", grid=(seq // tm,),
        in_specs=[row(D_MODEL), row(N_MERGE), whole(wmg)] + [row(PROJ_BLK) for _ in d_proj] + [whole(w_in_p)],
        out_specs=[row(D_MODEL)], out_shape=[jax.ShapeDtypeStruct((seq, D_MODEL), F32)],
        args=(du, dgpre, wmg, *d_proj, w_in_p), semantics=("parallel",))


def _grad_w_in(x, d_proj, *, tk):
    seq = x.shape[0]
    n_pieces = len(d_proj)
    nk = seq // tk

    def body(x_ref, *rest):
        piece_refs, out_ref, acc = rest[:n_pieces], rest[n_pieces], rest[n_pieces + 1]
        j, kk = pl.program_id(0), pl.program_id(1)

        @pl.when(kk == 0)
        def _():
            acc[...] = jnp.zeros_like(acc)

        xb = _bf(x_ref[...])
        for pair in range(n_pieces // 2):
            @pl.when(j == pair)
            def _(pair=pair):
                both = jnp.concatenate([_bf(piece_refs[2 * pair][...]), _bf(piece_refs[2 * pair + 1][...])], axis=1)
                acc[...] += _dot_tn(xb, both)

        @pl.when(kk == nk - 1)
        def _():
            out_ref[...] = acc[...]

    def piece_spec(s):
        return pl.BlockSpec((tk, PROJ_BLK), lambda j, kk: (jnp.where(j == s // 2, kk, 0), 0))

    return pl.pallas_call(
        body, name="grad_w_in", grid=(n_pieces // 2, nk),
        in_specs=[pl.BlockSpec((tk, D_MODEL), lambda j, kk: (kk, 0))] + [piece_spec(s) for s in range(n_pieces)],
        out_specs=pl.BlockSpec((D_MODEL, 2 * PROJ_BLK), lambda j, kk: (0, j)),
        out_shape=jax.ShapeDtypeStruct((D_MODEL, n_pieces * PROJ_BLK), F32),
        scratch_shapes=[pltpu.VMEM((D_MODEL, 2 * PROJ_BLK), F32)],
        compiler_params=_cparams(("parallel", "arbitrary")),
    )(x, *d_proj)


EARLY_NAMES = ("w_mem_kv", "w_branch_mla", "w_branch_sb", "w_branch_mem", "w_merge_gate", "w_out")
LATE_NAMES = ("w_in", "w_q_b", "w_kv_b")


def _remote(src, dst, send_sem, recv_sem, device):
    return pltpu.make_async_remote_copy(src_ref=src, dst_ref=dst, send_sem=send_sem, recv_sem=recv_sem, device_id=device,
                                        device_id_type=MESH_ID)


def _gather_rider(shards):
    n = len(shards)

    def copies(src_refs, out_refs, sems):
        send_sems, recv_sems, local_sems = sems
        x, y, c = _place()
        me = 2 * x + y
        out = []
        for a, (s, o) in enumerate(zip(src_refs, out_refs)):
            out.append(pltpu.make_async_copy(s, o.at[me], local_sems.at[a]))
            for k, (px, py) in enumerate(_other_chips(x, y)):
                out.append(_Exchange(_remote(s, o.at[me], send_sems.at[k, a], recv_sems.at[k, a], (px, py, c)),
                                     _remote(s, o.at[2 * px + py], send_sems.at[k, a], recv_sems.at[k, a], (px, py, c))))
        return out

    return _Rider(shards, [jax.ShapeDtypeStruct((N_CHIPS,) + s.shape, s.dtype) for s in shards],
                  [pltpu.SemaphoreType.DMA((3, n)), pltpu.SemaphoreType.DMA((3, n)), pltpu.SemaphoreType.DMA((n,))], copies)


def _sibling_rider(g4):
    n = len(g4)

    def copies(g_refs, out_refs, sems):
        send_sems, recv_sems = sems
        x, y, c = _place()
        out = []
        for a, (g, o) in enumerate(zip(g_refs, out_refs)):
            half = g.shape[1] // 2
            theirs = pl.ds(pl.multiple_of((1 - c) * half, 8), half)
            cp = _remote(g.at[:, theirs, :], o, send_sems.at[a], recv_sems.at[a], (x, y, 1 - c))
            out.append(_Exchange(cp, cp))
        return out

    return _Rider(g4, [jax.ShapeDtypeStruct((N_CHIPS, g.shape[1] // 2, g.shape[2]), g.dtype) for g in g4],
                  [pltpu.SemaphoreType.DMA((n,)), pltpu.SemaphoreType.DMA((n,))], copies)


def _chips_rider(wire):
    n = len(wire)

    def copies(s_refs, out_refs, sems):
        send_sems, recv_sems = sems
        x, y, c = _place()
        out = []
        for a, (s, o) in enumerate(zip(s_refs, out_refs)):
            for k, (px, py) in enumerate(_other_chips(x, y)):
                cp = _remote(s.at[2 * px + py], o.at[RELATION_XOR[k] - 1], send_sems.at[k, a], recv_sems.at[k, a], (px, py, c))
                out.append(_Exchange(cp, cp))
        return out

    return _Rider(wire, [jax.ShapeDtypeStruct((3,) + s.shape[1:], s.dtype) for s in wire],
                  [pltpu.SemaphoreType.DMA((3, n)), pltpu.SemaphoreType.DMA((3, n))], copies)


def _local_step(x, mem, tgt, w, small, *, tq, tq_sb_bwd, tk, tk_mla, t_row, t_mm, t_wg, rest_shards=None):
    seq = x.shape[0]
    on_mesh = rest_shards is not None
    w_in_p, wqb, wkvb = _primed_weights(w)
    tabs = _rope_tables(seq)

    p32, pbf = _matmul(x, w_in_p, mode="nn", tm=256, tn=IN_WIDTH_P, tk=D_MODEL, out_dtypes=(F32, BF16), name="proj_in")
    qp, kp, vp = _mla_prep(p32, small["q_a_gain"], small["kv_a_gain"], wqb, wkvb, tabs, t=t_row)
    res = _mla_attn_fwd(qp, kp, vp, tq=tq, tk=tk_mla, tk_diag=tk, rider=_gather_rider(rest_shards) if on_mesh else None)
    o_a, lse = res[0], res[1]
    if on_mesh:
        w = dict(w, **{n: _join_chips(n, g) for n, g in zip(EARLY_NAMES, res[2:])})
    wmg, wout = w["w_merge_gate"], w["w_out"]
    wba, wbb, wbm = w["w_branch_mla"], w["w_branch_sb"], w["w_branch_mem"]
    o_b, keep_total = _sb_attn_fwd(pbf, tq=tq, tk=tk)
    (mkv,) = _matmul(mem, w["w_mem_kv"], mode="nn", tm=MEM_LEN, tn=512, tk=D_MODEL, out_dtypes=(BF16,), name="mem_kv")
    o_m = _mem_fwd(pbf, mkv, t=t_row)

    (du, merged, dgpre, h_a, h_b, h_m, dy_a, dy_b, dy_m, do_a, do_b, do_m, dgate_a, dgate_b, dgate_m,
     d_ln_g, d_ln_b, d_bmg, loss) = _mid(x, tgt, o_a, o_b, o_m, p32, wmg, small["b_merge_gate"], wba, wbb, wbm, wout,
                                         small["ln_gain"], small["ln_bias"], t=t_row)

    wg = functools.partial(_matmul, mode="tn", tm=512, tn=1024, out_dtypes=(F32,))
    dq_m, dmkv = _mem_bwd(pbf, mkv, do_m, t=t_row)
    early = {"w_mem_kv": wg(mem, dmkv, tk=MEM_LEN, name="grad_w_mem_kv")[0],
             "w_branch_mla": wg(h_a, dy_a, tk=t_wg, name="grad_w_branch_a")[0],
             "w_branch_sb": wg(h_b, dy_b, tk=t_wg, name="grad_w_branch_b")[0],
             "w_branch_mem": wg(h_m, dy_m, tk=t_wg, name="grad_w_branch_m")[0],
             "w_merge_gate": wg(x, dgpre, tk=t_wg, name="grad_w_merge_gate")[0],
             "w_out": wg(merged, du, tk=t_wg, name="grad_w_out")[0]}

    if on_mesh:
        g4 = [_split_by_chip(n, early[n]) for n in EARLY_NAMES]
        res = _mla_attn_bwd(qp, kp, vp, o_a, lse, do_a, tq=tq, tk=tk_mla, tk_diag=tk, rider=_sibling_rider(g4))
        (dqp, dkp, dvp), got = res[:3], res[3:]
        chipsum, wire = _rs_add_sibling(g4, got, [BF16] * len(g4))
        res = _sb_attn_bwd(pbf, keep_total, do_b, tq=tq_sb_bwd, tk=tk, rider=_chips_rider(wire))
        (dq_b, dk_b, dv_b), parts = res[:3], res[3:]
        early = _rs_add_chips(chipsum, parts)
    else:
        dqp, dkp, dvp = _mla_attn_bwd(qp, kp, vp, o_a, lse, do_a, tq=tq, tk=tk_mla, tk_diag=tk)
        dq_b, dk_b, dv_b = _sb_attn_bwd(pbf, keep_total, do_b, tq=tq_sb_bwd, tk=tk)
    dlat, d_wqb, d_wkvb, d_gq, d_gkv = _mla_post(p32, dqp, dkp, dvp, small["q_a_gain"], small["kv_a_gain"], wqb, wkvb, tabs,
                                                 t=t_row)

    d_proj = [dlat, dgate_a, dq_b, dk_b, dv_b, dgate_b, dq_m, dgate_m]
    d_winp = _grad_w_in(x, d_proj, tk=min(1024, seq))

    d_win = jnp.concatenate([d_winp[:, 0:384], d_winp[:, 448:480], d_winp[:, 512:]], axis=1)
    d_wq = d_wqb.reshape(MLA_Q_LORA, MLA_HEADS, 128)[:, :, :96].reshape(MLA_Q_LORA, 768)
    d_wk = d_wkvb[:, :1024].reshape(MLA_KV_LORA, MLA_HEADS, 128)[:, :, :MLA_NOPE]
    d_wv = d_wkvb[:, 1024:].reshape(MLA_KV_LORA, MLA_HEADS, MLA_V)
    d_wkv = jnp.concatenate([d_wk, d_wv], axis=2).reshape(MLA_KV_LORA, 1024)
    late = {"w_in": d_win, "w_q_b": d_wq, "w_kv_b": d_wkv}
    small_grads = {"q_a_gain": d_gq, "kv_a_gain": d_gkv, "b_merge_gate": d_bmg, "ln_gain": d_ln_g, "ln_bias": d_ln_b}
    if not on_mesh:
        (grad_x,) = _grad_x(du, dgpre, wmg, d_proj, w_in_p, tm=256)
        return loss[0, 0], grad_x, late, small_grads, early

    g4 = [_split_by_chip(n, late[n]) for n in LATE_NAMES]
    g4.append(jnp.broadcast_to(_pack_small(small_grads)[None], (N_CHIPS, SMALL_ROWS, PACK_COLS)))
    got = _rs_to_sibling(g4)
    chipsum, wire = _rs_add_sibling(g4, got, [BF16] * len(LATE_NAMES) + [F32])
    res = _grad_x(du, dgpre, wmg, d_proj, w_in_p, tm=256, rider=_chips_rider(wire))
    late_mine = _rs_add_chips(chipsum, res[1:])
    return loss[0, 0], res[0], late_mine, None, early


def _pack_shards(shards, small):
    flat_small = jnp.concatenate([small[n].reshape(-1) for n, _ in SMALL_SIZES])
    flat_small = jnp.pad(flat_small, (0, 8 * PACK_COLS - SMALL_TOTAL)).reshape(8, PACK_COLS)
    parts = [shards[n].reshape(-1, PACK_COLS) for n, _ in PACK_ROWS[:-1]] + [flat_small]
    return jnp.concatenate(parts, axis=0)


def _unpack_shards(pack, shapes):
    out, r0 = {}, 0
    for n, rows in PACK_ROWS[:-1]:
        out[n] = pack[r0:r0 + rows].reshape(shapes[n])
        r0 += rows
    flat = pack[r0:r0 + 8].reshape(-1)
    small, c0 = {}, 0
    for n, size in SMALL_SIZES:
        small[n] = flat[c0:c0 + size].reshape(1, size)
        c0 += size
    return out, small


def _split_by_chip(name, full):
    r, c = full.shape
    if name in COL_SHARDED:
        s = full.reshape(r, N_CHIPS, c // N_CHIPS).transpose(1, 0, 2)
    else:
        s = full.reshape(N_CHIPS, r // N_CHIPS, c)
    return s.reshape(N_CHIPS, -1, PACK_COLS)


def _join_chips(name, packed4):
    r, c = FULL_SHAPES[name]
    if name in COL_SHARDED:
        return packed4.reshape(N_CHIPS, r, c // N_CHIPS).transpose(1, 0, 2).reshape(r, c)
    return packed4.reshape(r, c)


def _place():
    x, y, c = lax.axis_index("x"), lax.axis_index("y"), lax.axis_index("c")
    return x, y, c


def _other_chips(x, y):
    return ((1 - x, y), (x, 1 - y), (1 - x, 1 - y))


def _gather_weights(wpack):
    rows = wpack.shape[0]
    chunk = rows // 4

    def body(w_ref, out_ref, wb_ref, send_sems, recv_sems, local_sem):
        x, y, c = _place()
        me = 2 * x + y
        for r in range(4):
            wb_ref[r * chunk:(r + 1) * chunk, :] = _bf(w_ref[r * chunk:(r + 1) * chunk, :])
        mine = pltpu.make_async_copy(wb_ref, out_ref.at[me], local_sem)
        mine.start()
        copies = []
        for k, (px, py) in enumerate(_other_chips(x, y)):
            cp = pltpu.make_async_remote_copy(src_ref=wb_ref, dst_ref=out_ref.at[me], send_sem=send_sems.at[k],
                                              recv_sem=recv_sems.at[k], device_id=(px, py, c), device_id_type=MESH_ID)
            cp.start()
            copies.append(cp)
        for k, (px, py) in enumerate(_other_chips(x, y)):
            pltpu.make_async_remote_copy(src_ref=wb_ref, dst_ref=out_ref.at[2 * px + py], send_sem=send_sems.at[k],
                                         recv_sem=recv_sems.at[k], device_id=(px, py, c), device_id_type=MESH_ID).wait_recv()
        for cp in copies:
            cp.wait_send()
        mine.wait()

    return pl.pallas_call(
        body, name="gather_weights",
        in_specs=[pl.BlockSpec(memory_space=pltpu.VMEM)],
        out_specs=pl.BlockSpec(memory_space=pltpu.HBM),
        out_shape=jax.ShapeDtypeStruct((N_CHIPS, rows, PACK_COLS), BF16),
        scratch_shapes=[pltpu.VMEM((rows, PACK_COLS), BF16), pltpu.SemaphoreType.DMA((3,)), pltpu.SemaphoreType.DMA((3,)),
                        pltpu.SemaphoreType.DMA],
        compiler_params=pltpu.CompilerParams(vmem_limit_bytes=VMEM_LIMIT),
    )(wpack)


def _to_sibling_half(gpack):
    def body(g_ref, out_ref, send_sems, recv_sems):
        x, y, c = _place()
        theirs = pl.ds(pl.multiple_of((1 - c) * PACK_HALF, 8), PACK_HALF)
        copies = [pltpu.make_async_remote_copy(src_ref=g_ref.at[j, theirs, :], dst_ref=out_ref.at[j],
                                               send_sem=send_sems.at[j], recv_sem=recv_sems.at[j],
                                               device_id=(x, y, 1 - c), device_id_type=MESH_ID) for j in range(N_CHIPS)]
        for cp in copies:
            cp.start()
        for cp in copies:
            cp.wait()

    return pl.pallas_call(
        body, name="rs_sibling",
        in_specs=[pl.BlockSpec(memory_space=pltpu.HBM)],
        out_specs=pl.BlockSpec(memory_space=pltpu.HBM),
        out_shape=jax.ShapeDtypeStruct((N_CHIPS, PACK_HALF, PACK_COLS), F32),
        scratch_shapes=[pltpu.SemaphoreType.DMA((N_CHIPS,)), pltpu.SemaphoreType.DMA((N_CHIPS,))],
    )(gpack)


def _add_sibling(gpack, got):
    tr = PACK_HALF // 4

    def body(c_ref, g_ref, r_ref, o_ref):
        o_ref[...] = g_ref[...] + r_ref[...]

    grid_spec = pltpu.PrefetchScalarGridSpec(
        num_scalar_prefetch=1, grid=(N_CHIPS, 4),
        in_specs=[pl.BlockSpec((1, tr, PACK_COLS), lambda j, i, c_ref: (j, c_ref[0] * 4 + i, 0)),
                  pl.BlockSpec((1, tr, PACK_COLS), lambda j, i, c_ref: (j, i, 0))],
        out_specs=pl.BlockSpec((1, tr, PACK_COLS), lambda j, i, c_ref: (j, i, 0)))
    return pl.pallas_call(
        body, name="rs_add_sibling", grid_spec=grid_spec,
        out_shape=jax.ShapeDtypeStruct((N_CHIPS, PACK_HALF, PACK_COLS), F32),
        compiler_params=_cparams(("parallel", "parallel")),
    )(lax.axis_index("c").astype(jnp.int32).reshape(1), gpack, got)


def _to_owner_chips(chipsum):
    def body(s_ref, out_ref, send_sems, recv_sems, local_sem):
        x, y, c = _place()
        me = 2 * x + y
        mine = pltpu.make_async_copy(s_ref.at[me], out_ref.at[me], local_sem)
        mine.start()
        copies = []
        for k, (px, py) in enumerate(_other_chips(x, y)):
            cp = pltpu.make_async_remote_copy(src_ref=s_ref.at[2 * px + py], dst_ref=out_ref.at[me], send_sem=send_sems.at[k],
                                              recv_sem=recv_sems.at[k], device_id=(px, py, c), device_id_type=MESH_ID)
            cp.start()
            copies.append(cp)
        for k, (px, py) in enumerate(_other_chips(x, y)):
            pltpu.make_async_remote_copy(src_ref=s_ref.at[me], dst_ref=out_ref.at[2 * px + py], send_sem=send_sems.at[k],
                                         recv_sem=recv_sems.at[k], device_id=(px, py, c), device_id_type=MESH_ID).wait_recv()
        for cp in copies:
            cp.wait_send()
        mine.wait()

    return pl.pallas_call(
        body, name="rs_chips",
        in_specs=[pl.BlockSpec(memory_space=pltpu.HBM)],
        out_specs=pl.BlockSpec(memory_space=pltpu.HBM),
        out_shape=jax.ShapeDtypeStruct(chipsum.shape, F32),
        scratch_shapes=[pltpu.SemaphoreType.DMA((3,)), pltpu.SemaphoreType.DMA((3,)), pltpu.SemaphoreType.DMA],
    )(chipsum)


def _add_chips(parts):
    tr = PACK_HALF // 4

    def body(p_ref, o_ref):
        o_ref[...] = ((p_ref[0] + p_ref[1]) + p_ref[2]) + p_ref[3]

    return pl.pallas_call(
        body, name="rs_add_chips", grid=(4,),
        in_specs=[pl.BlockSpec((N_CHIPS, tr, PACK_COLS), lambda i: (0, i, 0))],
        out_specs=pl.BlockSpec((tr, PACK_COLS), lambda i: (i, 0)),
        out_shape=jax.ShapeDtypeStruct((PACK_HALF, PACK_COLS), F32),
        compiler_params=_cparams(("parallel",)),
    )(parts)


def _swap_halves(half):
    def body(h_ref, out_ref, send_sem, recv_sem, local_sem):
        x, y, c = _place()
        my_rows = pl.ds(pl.multiple_of(c * PACK_HALF, 8), PACK_HALF)
        mine = pltpu.make_async_copy(h_ref, out_ref.at[my_rows, :], local_sem)
        mine.start()
        cp = pltpu.make_async_remote_copy(src_ref=h_ref, dst_ref=out_ref.at[my_rows, :], send_sem=send_sem, recv_sem=recv_sem,
                                          device_id=(x, y, 1 - c), device_id_type=MESH_ID)
        cp.start()
        their_rows = pl.ds(pl.multiple_of((1 - c) * PACK_HALF, 8), PACK_HALF)
        pltpu.make_async_remote_copy(src_ref=h_ref, dst_ref=out_ref.at[their_rows, :], send_sem=send_sem, recv_sem=recv_sem,
                                     device_id=(x, y, 1 - c), device_id_type=MESH_ID).wait_recv()
        cp.wait_send()
        mine.wait()

    return pl.pallas_call(
        body, name="rs_swap_halves",
        in_specs=[pl.BlockSpec(memory_space=pltpu.HBM)],
        out_specs=pl.BlockSpec(memory_space=pltpu.HBM),
        out_shape=jax.ShapeDtypeStruct((PACK_TOTAL, PACK_COLS), F32),
        scratch_shapes=[pltpu.SemaphoreType.DMA, pltpu.SemaphoreType.DMA, pltpu.SemaphoreType.DMA],
    )(half)


def _adamw(w, g, m, v):
    tr = PACK_TOTAL // 8

    def body(w_ref, g_ref, m_ref, v_ref, d_ref, nm_ref, nv_ref):
        gv = g_ref[...]
        m_new = ADAM_B1 * m_ref[...] + (1.0 - ADAM_B1) * gv
        v_new = ADAM_B2 * v_ref[...] + (1.0 - ADAM_B2) * (gv * gv)
        m_hat = m_new / (1.0 - ADAM_B1 ** ADAM_STEP)
        v_hat = v_new / (1.0 - ADAM_B2 ** ADAM_STEP)
        d_ref[...] = -ADAM_LR * (m_hat / (jnp.sqrt(v_hat) + ADAM_EPS) + ADAM_WD * w_ref[...])
        nm_ref[...] = m_new
        nv_ref[...] = v_new

    blk = pl.BlockSpec((tr, PACK_COLS), lambda i: (i, 0))
    return pl.pallas_call(
        body, name="adamw", grid=(8,),
        in_specs=[blk] * 4, out_specs=[blk] * 3,
        out_shape=[jax.ShapeDtypeStruct((PACK_TOTAL, PACK_COLS), F32)] * 3,
        compiler_params=_cparams(("parallel",)),
    )(w, g, m, v)


SMALL_ROWS = 64
ADAM_STEPS_PER_HALF = 4


def _pack_small(d):
    flat = jnp.concatenate([d[n].reshape(-1) for n, _ in SMALL_SIZES])
    return jnp.pad(flat, (0, SMALL_ROWS * PACK_COLS - SMALL_TOTAL)).reshape(SMALL_ROWS, PACK_COLS)


def _unpack_small(a):
    flat, out, c0 = a.reshape(-1), {}, 0
    for n, size in SMALL_SIZES:
        out[n] = flat[c0:c0 + size].reshape(1, size)
        c0 += size
    return out


def _split_by_chip(name, full):
    r, c = full.shape
    if name in COL_SHARDED:
        return full.reshape(r, N_CHIPS, c // N_CHIPS).transpose(1, 0, 2)
    return full.reshape(N_CHIPS, r // N_CHIPS, c)


def _join_chips(name, slots):
    _, r, cs = slots.shape
    if name in COL_SHARDED:
        return slots.transpose(1, 0, 2).reshape(r, N_CHIPS * cs)
    return slots.reshape(N_CHIPS * r, cs)


HBM_SPEC = pl.BlockSpec(memory_space=pltpu.HBM)


def _gather_shards(shards):
    n = len(shards)

    def body(*refs):
        w_refs, out_refs, wb_refs = refs[:n], refs[n:2 * n], refs[2 * n:3 * n]
        send_sems, recv_sems, pass_send_sems, pass_recv_sems, local_sems = refs[3 * n:]
        x, y, c = _place()
        me = 2 * x + y
        sibling = (x, y, 1 - c)

        def halves(ref):
            half = ref.shape[-2] // 2
            return (pl.ds(pl.multiple_of(c * half, 16), half), pl.ds(pl.multiple_of((1 - c) * half, 16), half))
        for w_ref, wb_ref in zip(w_refs, wb_refs):
            rows = w_ref.shape[0]
            chunk = min(rows, 128)

            def cast(i, carry, w_ref=w_ref, wb_ref=wb_ref, chunk=chunk):
                r0 = pl.multiple_of(i * chunk, chunk)
                wb_ref[pl.ds(r0, chunk), :] = _bf(w_ref[pl.ds(r0, chunk), :])
                return carry

            lax.fori_loop(0, rows // chunk, cast, 0)
        sends, locals_ = [], []
        for a, (wb_ref, out_ref) in enumerate(zip(wb_refs, out_refs)):
            mine = pltpu.make_async_copy(wb_ref, out_ref.at[me], local_sems.at[a])
            mine.start()
            locals_.append(mine)
            mine_rows, _ = halves(wb_ref)
            for k, (px, py) in enumerate(_other_chips(x, y)):
                cp = pltpu.make_async_remote_copy(src_ref=wb_ref.at[mine_rows, :], dst_ref=out_ref.at[me, mine_rows, :],
                                                  send_sem=send_sems.at[k, a], recv_sem=recv_sems.at[k, a],
                                                  device_id=(px, py, c), device_id_type=MESH_ID)
                cp.start()
                sends.append(cp)
        for a, (wb_ref, out_ref) in enumerate(zip(wb_refs, out_refs)):
            mine_rows, _ = halves(wb_ref)
            for k, (px, py) in enumerate(_other_chips(x, y)):
                landed = out_ref.at[2 * px + py, mine_rows, :]
                pltpu.make_async_remote_copy(src_ref=wb_ref.at[mine_rows, :], dst_ref=landed, send_sem=send_sems.at[k, a],
                                             recv_sem=recv_sems.at[k, a], device_id=(px, py, c),
                                             device_id_type=MESH_ID).wait_recv()
                cp = pltpu.make_async_remote_copy(src_ref=landed, dst_ref=landed, send_sem=pass_send_sems.at[k, a],
                                                  recv_sem=pass_recv_sems.at[k, a], device_id=sibling, device_id_type=MESH_ID)
                cp.start()
                sends.append(cp)
        for a, (wb_ref, out_ref) in enumerate(zip(wb_refs, out_refs)):
            _, their_rows = halves(wb_ref)
            for k, (px, py) in enumerate(_other_chips(x, y)):
                passed = out_ref.at[2 * px + py, their_rows, :]
                pltpu.make_async_remote_copy(src_ref=passed, dst_ref=passed, send_sem=pass_send_sems.at[k, a],
                                             recv_sem=pass_recv_sems.at[k, a], device_id=sibling,
                                             device_id_type=MESH_ID).wait_recv()
        for cp in sends:
            cp.wait_send()
        for cp in locals_:
            cp.wait()

    return pl.pallas_call(
        body, name="gather_weights",
        in_specs=[pl.BlockSpec(memory_space=pltpu.VMEM)] * n,
        out_specs=[HBM_SPEC] * n,
        out_shape=[jax.ShapeDtypeStruct((N_CHIPS,) + s.shape, BF16) for s in shards],
        scratch_shapes=[pltpu.VMEM(s.shape, BF16) for s in shards]
        + [pltpu.SemaphoreType.DMA((3, n))] * 4 + [pltpu.SemaphoreType.DMA((n,))],
        compiler_params=pltpu.CompilerParams(vmem_limit_bytes=VMEM_LIMIT),
    )(*shards)


def _cast_bf16_list(arrays):
    def body(*refs):
        for a_ref, o_ref in zip(refs[:len(arrays)], refs[len(arrays):]):
            o_ref[...] = _bf(a_ref[...])

    specs = [pl.BlockSpec((a.shape[0] // 4, a.shape[1]), lambda i: (i, 0)) for a in arrays]
    return pl.pallas_call(
        body, name="cast_shards", grid=(4,), in_specs=specs, out_specs=specs,
        out_shape=[jax.ShapeDtypeStruct(a.shape, BF16) for a in arrays],
        compiler_params=_cparams(("parallel",)),
    )(*arrays)


def _rs_to_sibling(g4):
    n = len(g4)

    def body(*refs):
        g_refs, out_refs = refs[:n], refs[n:2 * n]
        send_sems, recv_sems = refs[2 * n:]
        x, y, c = _place()
        copies = []
        for a, (g_ref, out_ref) in enumerate(zip(g_refs, out_refs)):
            half = g_ref.shape[1] // 2
            theirs = pl.ds(pl.multiple_of((1 - c) * half, 8), half)
            copies.append(pltpu.make_async_remote_copy(src_ref=g_ref.at[:, theirs, :], dst_ref=out_ref, send_sem=send_sems.at[a],
                                                       recv_sem=recv_sems.at[a], device_id=(x, y, 1 - c),
                                                       device_id_type=MESH_ID))
        for cp in copies:
            cp.start()
        for cp in copies:
            cp.wait()

    return pl.pallas_call(
        body, name="rs_sibling", in_specs=[HBM_SPEC] * n, out_specs=[HBM_SPEC] * n,
        out_shape=[jax.ShapeDtypeStruct((N_CHIPS, g.shape[1] // 2, g.shape[2]), F32) for g in g4],
        scratch_shapes=[pltpu.SemaphoreType.DMA((n,)), pltpu.SemaphoreType.DMA((n,))],
    )(*g4)


def _rs_add_sibling(g4, got, wire_dtypes):
    n = len(g4)
    narrow = [a for a in range(n) if wire_dtypes[a] != F32]

    def body(c_ref, *refs):
        outs = refs[2 * n:3 * n]
        wires = dict(zip(narrow, refs[3 * n:]))
        for a, (g_ref, r_ref, o_ref) in enumerate(zip(refs[:n], refs[n:2 * n], outs)):
            s = g_ref[...] + r_ref[...]
            o_ref[...] = s
            if a in wires:
                wires[a][...] = s.astype(wires[a].dtype)

    blk = lambda r: (1, r.shape[1], r.shape[2])
    plain = lambda r: pl.BlockSpec(blk(r), lambda j, c_ref: (j, 0, 0))
    grid_spec = pltpu.PrefetchScalarGridSpec(
        num_scalar_prefetch=1, grid=(N_CHIPS,),
        in_specs=[pl.BlockSpec(blk(r), lambda j, c_ref: (j, c_ref[0], 0)) for r in got] + [plain(r) for r in got],
        out_specs=[plain(r) for r in got] + [plain(got[a]) for a in narrow])
    res = pl.pallas_call(
        body, name="rs_add_sibling", grid_spec=grid_spec,
        out_shape=[jax.ShapeDtypeStruct(r.shape, F32) for r in got]
        + [jax.ShapeDtypeStruct(got[a].shape, wire_dtypes[a]) for a in narrow],
        compiler_params=_cparams(("parallel",)),
    )(lax.axis_index("c").astype(jnp.int32).reshape(1), *g4, *got)
    chipsum = list(res[:n])
    wire = list(chipsum)
    for a, w in zip(narrow, res[n:]):
        wire[a] = w
    return chipsum, wire


RELATION_XOR = (2, 1, 3)


def _rs_to_chips(wire):
    n = len(wire)

    def body(*refs):
        s_refs, out_refs = refs[:n], refs[n:2 * n]
        send_sems, recv_sems = refs[2 * n:]
        x, y, c = _place()
        sends = []
        for a, (s_ref, out_ref) in enumerate(zip(s_refs, out_refs)):
            for k, (px, py) in enumerate(_other_chips(x, y)):
                cp = pltpu.make_async_remote_copy(src_ref=s_ref.at[2 * px + py], dst_ref=out_ref.at[RELATION_XOR[k] - 1],
                                                  send_sem=send_sems.at[k, a], recv_sem=recv_sems.at[k, a],
                                                  device_id=(px, py, c), device_id_type=MESH_ID)
                cp.start()
                sends.append(cp)
        for cp in sends:
            cp.wait_recv()
        for cp in sends:
            cp.wait_send()

    return pl.pallas_call(
        body, name="rs_chips", in_specs=[HBM_SPEC] * n, out_specs=[HBM_SPEC] * n,
        out_shape=[jax.ShapeDtypeStruct((3,) + s.shape[1:], s.dtype) for s in wire],
        scratch_shapes=[pltpu.SemaphoreType.DMA((3, n)), pltpu.SemaphoreType.DMA((3, n))],
    )(*wire)


def _rs_add_chips(chipsum, parts):
    n = len(parts)

    def body(me_ref, *refs):
        me = me_ref[0]
        for s_ref, p_ref, o_ref in zip(refs[:n], refs[n:2 * n], refs[2 * n:]):
            own = s_ref[0]
            total = None
            for k in range(N_CHIPS):
                theirs = p_ref[jnp.maximum(jnp.bitwise_xor(me, k) - 1, 0)].astype(F32)
                term = jnp.where(me == k, own, theirs)
                total = term if total is None else total + term
            o_ref[...] = total

    grid_spec = pltpu.PrefetchScalarGridSpec(
        num_scalar_prefetch=1, grid=(2,),
        in_specs=[pl.BlockSpec((1, p.shape[1] // 2, p.shape[2]), lambda i, me_ref: (me_ref[0], i, 0)) for p in parts]
        + [pl.BlockSpec((3, p.shape[1] // 2, p.shape[2]), lambda i, me_ref: (0, i, 0)) for p in parts],
        out_specs=[pl.BlockSpec((p.shape[1] // 2, p.shape[2]), lambda i, me_ref: (i, 0)) for p in parts])
    me = (2 * lax.axis_index("x") + lax.axis_index("y")).astype(jnp.int32).reshape(1)
    return pl.pallas_call(
        body, name="rs_add_chips", grid_spec=grid_spec,
        out_shape=[jax.ShapeDtypeStruct(p.shape[1:], F32) for p in parts],
        compiler_params=_cparams(("parallel",)),
    )(me, *chipsum, *parts)


def _rs_swap_halves(halves):
    n = len(halves)

    def body(*refs):
        h_refs, out_refs = refs[:n], refs[n:2 * n]
        send_sems, recv_sems = refs[2 * n:]
        x, y, c = _place()
        copies = [pltpu.make_async_remote_copy(src_ref=h_ref, dst_ref=out_ref, send_sem=send_sems.at[a], recv_sem=recv_sems.at[a],
                                               device_id=(x, y, 1 - c), device_id_type=MESH_ID)
                  for a, (h_ref, out_ref) in enumerate(zip(h_refs, out_refs))]
        for cp in copies:
            cp.start()
        for cp in copies:
            cp.wait()

    return pl.pallas_call(
        body, name="rs_swap_halves", in_specs=[HBM_SPEC] * n, out_specs=[HBM_SPEC] * n,
        out_shape=[jax.ShapeDtypeStruct(h.shape, F32) for h in halves],
        scratch_shapes=[pltpu.SemaphoreType.DMA((n,)), pltpu.SemaphoreType.DMA((n,))],
    )(*halves)


def _adamw_list(ws, g_mine, g_theirs, ms, vs):
    n = len(ws)

    def body(c_ref, *refs):
        w_refs, gm_refs, gt_refs, m_refs, v_refs = (refs[k * n:(k + 1) * n] for k in range(5))
        g_refs, d_refs, nm_refs, nv_refs = (refs[k * n:(k + 1) * n] for k in range(5, 9))
        mine = (pl.program_id(0) // ADAM_STEPS_PER_HALF) == c_ref[0]
        for a in range(n):
            gv = jnp.where(mine, gm_refs[a][...], gt_refs[a][...])
            g_refs[a][...] = gv
            m_new = ADAM_B1 * m_refs[a][...] + (1.0 - ADAM_B1) * gv
            v_new = ADAM_B2 * v_refs[a][...] + (1.0 - ADAM_B2) * (gv * gv)
            m_hat = m_new / (1.0 - ADAM_B1 ** ADAM_STEP)
            v_hat = v_new / (1.0 - ADAM_B2 ** ADAM_STEP)
            d_refs[a][...] = -ADAM_LR * (m_hat / (jnp.sqrt(v_hat) + ADAM_EPS) + ADAM_WD * w_refs[a][...])
            nm_refs[a][...] = m_new
            nv_refs[a][...] = v_new

    steps = 2 * ADAM_STEPS_PER_HALF
    whole = [pl.BlockSpec((w.shape[0] // steps, w.shape[1]), lambda i, c_ref: (i, 0)) for w in ws]
    half = [pl.BlockSpec((w.shape[0] // steps, w.shape[1]), lambda i, c_ref: (i % ADAM_STEPS_PER_HALF, 0)) for w in ws]
    shapes = [jax.ShapeDtypeStruct(w.shape, F32) for w in ws]
    grid_spec = pltpu.PrefetchScalarGridSpec(num_scalar_prefetch=1, grid=(steps,),
                                             in_specs=whole + half + half + whole + whole, out_specs=whole * 4)
    res = pl.pallas_call(
        body, name="adamw", grid_spec=grid_spec, out_shape=shapes * 4,
        compiler_params=_cparams(("parallel",)),
    )(lax.axis_index("c").astype(jnp.int32).reshape(1), *ws, *g_mine, *g_theirs, *ms, *vs)
    return res[:n], res[n:2 * n], res[2 * n:3 * n], res[3 * n:]


WEIGHT_NAMES = ("w_in", "w_mem_kv", "q_a_gain", "w_q_b", "kv_a_gain", "w_kv_b", "w_branch_mla", "w_branch_sb",
                "w_branch_mem", "w_merge_gate", "b_merge_gate", "w_out", "ln_gain", "ln_bias")
BIG_NAMES = tuple(n for n, _ in PACK_ROWS[:-1])
SMALL_NAMES = tuple(n for n, _ in SMALL_SIZES)


def kernel(x, mem, w_in, w_mem_kv, q_a_gain, w_q_b, kv_a_gain, w_kv_b, w_branch_mla, w_branch_sb, w_branch_mem, w_merge_gate, b_merge_gate, w_out, ln_gain, ln_bias, loss_target, m_w_in, m_w_mem_kv, m_q_a_gain, m_w_q_b, m_kv_a_gain, m_w_kv_b, m_w_branch_mla, m_w_branch_sb, m_w_branch_mem, m_w_merge_gate, m_b_merge_gate, m_w_out, m_ln_gain, m_ln_bias, v_w_in, v_w_mem_kv, v_q_a_gain, v_w_q_b, v_kv_a_gain, v_w_kv_b, v_w_branch_mla, v_w_branch_sb, v_w_branch_mem, v_w_merge_gate, v_b_merge_gate, v_w_out, v_ln_gain, v_ln_bias):
    weights = dict(zip(WEIGHT_NAMES, (w_in, w_mem_kv, q_a_gain, w_q_b, kv_a_gain, w_kv_b, w_branch_mla, w_branch_sb,
                                      w_branch_mem, w_merge_gate, b_merge_gate, w_out, ln_gain, ln_bias)))
    mom1 = dict(zip(WEIGHT_NAMES, (m_w_in, m_w_mem_kv, m_q_a_gain, m_w_q_b, m_kv_a_gain, m_w_kv_b, m_w_branch_mla,
                                   m_w_branch_sb, m_w_branch_mem, m_w_merge_gate, m_b_merge_gate, m_w_out, m_ln_gain,
                                   m_ln_bias)))
    mom2 = dict(zip(WEIGHT_NAMES, (v_w_in, v_w_mem_kv, v_q_a_gain, v_w_q_b, v_kv_a_gain, v_w_kv_b, v_w_branch_mla,
                                   v_w_branch_sb, v_w_branch_mem, v_w_merge_gate, v_b_merge_gate, v_w_out, v_ln_gain,
                                   v_ln_bias)))
    def as_list(d):
        return [d[n][0] for n in BIG_NAMES] + [_pack_small({n: d[n] for n in SMALL_NAMES})]

    w_list, m_list, v_list = as_list(weights), as_list(mom1), as_list(mom2)

    gathered = _gather_shards([weights[n][0] for n in LATE_NAMES])
    first_w = {n: _join_chips(n, g) for n, g in zip(LATE_NAMES, gathered)}
    rest_shards = _cast_bf16_list([weights[n][0] for n in EARLY_NAMES])
    small = {n: weights[n] for n in SMALL_NAMES}

    seq = x.shape[1]
    loss, grad_x, late_mine, _, early_mine = _local_step(
        x[0], mem[0], loss_target[0], first_w, small, tq=min(1024, seq), tq_sb_bwd=512, tk=256, tk_mla=512, t_row=256, t_mm=512,
        t_wg=min(2048, seq), rest_shards=rest_shards)
    by_name = dict(zip(EARLY_NAMES + LATE_NAMES + ("small",), list(early_mine) + list(late_mine)))
    mine = [by_name[n] for n in BIG_NAMES + ("small",)]
    theirs = _rs_swap_halves(mine)
    g_list, d_list, nm_list, nv_list = _adamw_list(w_list, mine, theirs, m_list, v_list)

    loss = lax.psum(loss, ("x", "y", "c"))
    outs = [loss, grad_x[None]]
    for arrays in (g_list, d_list, nm_list, nv_list):
        big = dict(zip(BIG_NAMES, arrays[:-1]))
        sm = _unpack_small(arrays[-1])
        outs.extend(big[n][None] if n in big else sm[n] for n in WEIGHT_NAMES)
    return tuple(outs)
```

```python
import functools
import math

import numpy as np
import jax
import jax.numpy as jnp
from jax import lax
from jax.experimental import pallas as pl
from jax.experimental.pallas import tpu as pltpu

F32 = jnp.float32
BF16 = jnp.bfloat16
MESH_ID = pl.DeviceIdType.MESH

D_MODEL = 1024
MEM_LEN = 256
MLA_HEADS = 8
MLA_NOPE = 64
MLA_ROPE = 32
MLA_V = 64
MLA_Q_LORA = 256
MLA_KV_LORA = 128
SB_HEADS = 8
SB_HEAD_DIM = 64
MEM_HEADS = 4
MEM_HEAD_DIM = 128
ROPE_BASE = 10000.0
RMS_EPS = 1e-6
LN_EPS = 1e-5
DEEPNORM_ALPHA = 2.0 ** 0.25
MLA_SCALE = 1.0 / math.sqrt(MLA_NOPE + MLA_ROPE)
SB_SCALE = 1.0 / math.sqrt(SB_HEAD_DIM)
MEM_SCALE = 1.0 / math.sqrt(MEM_HEAD_DIM)

ADAM_LR = 0.001
ADAM_B1 = 0.9
ADAM_B2 = 0.999
ADAM_EPS = 1e-08
ADAM_WD = 0.01
ADAM_STEP = 10

LANES = 128
HALF = 64
N_CHIPS = 4
PACK_COLS = 1024
VMEM_LIMIT = 56 * 1024 * 1024

IN_WIDTH_P = 4096
BLK_LAT, BLK_GATE_A, BLK_QB, BLK_KB, BLK_VB, BLK_GATE_B, BLK_QM, BLK_GATE_M = range(8)
N_MERGE = 3 * D_MODEL

BIG_NAMES = ("w_in", "w_mem_kv", "w_q_b", "w_kv_b", "w_branch_mla", "w_branch_sb", "w_branch_mem", "w_merge_gate", "w_out")
COL_SHARDED = ("w_in", "w_q_b", "w_kv_b", "w_branch_mla", "w_branch_sb", "w_branch_mem", "w_merge_gate")
SMALL_SIZES = (("q_a_gain", 256), ("kv_a_gain", 128), ("b_merge_gate", 3072), ("ln_gain", 1024), ("ln_bias", 1024))
SMALL_TOTAL = sum(s for _, s in SMALL_SIZES)


def _cparams(sem=None):
    return pltpu.CompilerParams(dimension_semantics=sem, vmem_limit_bytes=VMEM_LIMIT)


def _dot(a, b):
    return jnp.dot(a, b, preferred_element_type=F32)


def _dot_nt(a, b):
    return lax.dot_general(a, b, (((1,), (1,)), ((), ())), preferred_element_type=F32)


def _dot_tn(a, b):
    return lax.dot_general(a, b, (((0,), (0,)), ((), ())), preferred_element_type=F32)


def _bf(x):
    return x.astype(BF16)


def _sigmoid(x):
    return 1.0 / (1.0 + jnp.exp(-x))


def _matmul(a, b, *, mode, tm, tn, tk, out_dtypes, name, add=None, add_scale=1.0, b_block0=0, n=None):
    if mode == "nn":
        (m, k), n = a.shape, b.shape[1]
        a_spec = pl.BlockSpec((tm, tk), lambda i, j, kk: (i, kk))
        b_spec = pl.BlockSpec((tk, tn), lambda i, j, kk: (kk, j))
        dot = _dot
    elif mode == "nt":
        (m, k), n = a.shape, b.shape[0]
        a_spec = pl.BlockSpec((tm, tk), lambda i, j, kk: (i, kk))
        b_spec = pl.BlockSpec((tn, tk), lambda i, j, kk: (j, kk))
        dot = _dot_nt
    else:
        (k, m), n = a.shape, (b.shape[1] if n is None else n)
        a_spec = pl.BlockSpec((tk, tm), lambda i, j, kk: (kk, i))
        b_spec = pl.BlockSpec((tk, tn), lambda i, j, kk: (kk, j + b_block0))
        dot = _dot_tn
    assert m % tm == 0 and n % tn == 0 and k % tk == 0, (name, m, n, k)
    nk = k // tk
    n_out = len(out_dtypes)
    has_add = add is not None

    def body(*refs):
        a_ref, b_ref = refs[0], refs[1]
        add_ref = refs[2] if has_add else None
        outs = refs[2 + has_add: 2 + has_add + n_out]
        acc = refs[-1]
        kk = pl.program_id(2)

        @pl.when(kk == 0)
        def _():
            acc[...] = jnp.zeros_like(acc)

        acc[...] += dot(_bf(a_ref[...]), _bf(b_ref[...]))

        @pl.when(kk == nk - 1)
        def _():
            r = acc[...]
            if has_add:
                r = r + add_scale * add_ref[...]
            for o in outs:
                o[...] = r.astype(o.dtype)

    in_specs = [a_spec, b_spec]
    args = [a, b]
    if has_add:
        in_specs.append(pl.BlockSpec((tm, tn), lambda i, j, kk: (i, j)))
        args.append(add)
    out_spec = pl.BlockSpec((tm, tn), lambda i, j, kk: (i, j))
    res = pl.pallas_call(
        body, name=name, grid=(m // tm, n // tn, nk),
        in_specs=in_specs, out_specs=[out_spec] * n_out,
        out_shape=[jax.ShapeDtypeStruct((m, n), dt) for dt in out_dtypes],
        scratch_shapes=[pltpu.VMEM((tm, tn), F32)],
        compiler_params=_cparams(("parallel", "parallel", "arbitrary")),
    )(*args)
    return res


def _rope_tables(seq):
    half = MLA_ROPE // 2
    freqs = ROPE_BASE ** (-jnp.arange(half, dtype=F32) / half)
    ang = jnp.arange(seq, dtype=jnp.int32).astype(F32)[:, None] * freqs[None, :]
    cos, sin = jnp.cos(ang), jnp.sin(ang)
    z = lambda w: jnp.zeros((seq, w), F32)
    c_q = jnp.concatenate([jnp.ones((seq, MLA_NOPE), F32), cos, cos, z(32)], axis=1)
    c_k = jnp.concatenate([z(MLA_NOPE), cos, cos, z(32)], axis=1)
    s_lo = jnp.concatenate([z(MLA_NOPE), -sin, z(half), z(32)], axis=1)
    s_hi = jnp.concatenate([z(MLA_NOPE), z(half), sin, z(32)], axis=1)
    return c_q, c_k, s_lo, s_hi


def _rope_fwd(x, c, s_lo, s_hi):
    return x * c + pltpu.roll(x, LANES - 16, 1) * s_lo + pltpu.roll(x, 16, 1) * s_hi


def _rope_bwd(d, c, s_lo, s_hi):
    return d * c - pltpu.roll(d, 16, 1) * s_hi - pltpu.roll(d, LANES - 16, 1) * s_lo


def _rms_fwd(x, g):
    r = lax.rsqrt(jnp.mean(x * x, axis=-1, keepdims=True) + RMS_EPS)
    xn = x * r
    return xn * g, xn, r


def _mla_prep(p32, gq, gkv, wqb, wkvb, tabs, *, t):
    seq = p32.shape[0]

    def body(lat_ref, gq_ref, gkv_ref, wqb_ref, wkvb_ref, cq_ref, ck_ref, slo_ref, shi_ref, q_ref, k_ref, v_ref):
        lat = lat_ref[...]
        slo, shi = slo_ref[...], shi_ref[...]
        nq, _, _ = _rms_fwd(lat[:, 0:MLA_Q_LORA], gq_ref[...])
        qa = _dot(_bf(nq), wqb_ref[...])
        cq = cq_ref[...]
        for h in range(MLA_HEADS):
            blk = qa[:, h * LANES:(h + 1) * LANES]
            q_ref[:, h * LANES:(h + 1) * LANES] = _bf(_rope_fwd(blk, cq, slo, shi))
        nkv, _, _ = _rms_fwd(lat[:, MLA_Q_LORA:MLA_Q_LORA + MLA_KV_LORA], gkv_ref[...])
        kv = _dot(_bf(nkv), wkvb_ref[...])
        kpe = _rope_fwd(lat[:, 384:512], ck_ref[...], slo, shi)
        for h in range(MLA_HEADS):
            k_ref[:, h * LANES:(h + 1) * LANES] = _bf(kv[:, h * LANES:(h + 1) * LANES] + kpe)
        v_ref[...] = _bf(kv[:, MLA_HEADS * LANES:])

    row = lambda w: pl.BlockSpec((t, w), lambda i: (i, 0))
    full = lambda shp: pl.BlockSpec(shp, lambda i: (0, 0))
    return pl.pallas_call(
        body, name="mla_prep", grid=(seq // t,),
        in_specs=[row(512), full((1, MLA_Q_LORA)), full((1, MLA_KV_LORA)), full(wqb.shape), full(wkvb.shape),
                  row(LANES), row(LANES), row(LANES), row(LANES)],
        out_specs=[row(1024), row(1024), row(512)],
        out_shape=[jax.ShapeDtypeStruct((seq, 1024), BF16), jax.ShapeDtypeStruct((seq, 1024), BF16),
                   jax.ShapeDtypeStruct((seq, 512), BF16)],
        compiler_params=_cparams(("parallel",)),
    )(p32, gq, gkv, wqb, wkvb, *tabs)


def _mla_post(p32, dq, dk, dv, gq, gkv, wqb, wkvb, tabs, *, t):
    seq = p32.shape[0]

    def body(lat_ref, dq_ref, dk_ref, dv_ref, gq_ref, gkv_ref, wqb_ref, wkvb_ref, cq_ref, ck_ref, slo_ref, shi_ref,
             dlat_ref, dwqb_ref, dwkvb_ref, dgq_ref, dgkv_ref):
        @pl.when(pl.program_id(0) == 0)
        def _():
            dwqb_ref[...] = jnp.zeros_like(dwqb_ref)
            dwkvb_ref[...] = jnp.zeros_like(dwkvb_ref)
            dgq_ref[...] = jnp.zeros_like(dgq_ref)
            dgkv_ref[...] = jnp.zeros_like(dgkv_ref)

        lat = lat_ref[...]
        slo, shi = slo_ref[...], shi_ref[...]
        cq = cq_ref[...]
        gq_v, gkv_v = gq_ref[...], gkv_ref[...]
        nq, xq, rq = _rms_fwd(lat[:, 0:MLA_Q_LORA], gq_v)
        nkv, xkv, rkv = _rms_fwd(lat[:, MLA_Q_LORA:MLA_Q_LORA + MLA_KV_LORA], gkv_v)

        dqa = jnp.concatenate(
            [_rope_bwd(dq_ref[:, h * LANES:(h + 1) * LANES], cq, slo, shi) for h in range(MLA_HEADS)], axis=1)
        dqa_b = _bf(dqa)
        dwqb_ref[...] += _dot_tn(_bf(nq), dqa_b)
        dnq = _dot_nt(dqa_b, wqb_ref[...])
        dgq_ref[...] += jnp.sum(dnq * xq, axis=0, keepdims=True)
        dxn = dnq * gq_v
        dcq = rq * (dxn - xq * jnp.mean(dxn * xq, axis=-1, keepdims=True))

        dkf = dk_ref[...]
        dkv_b = _bf(jnp.concatenate([dkf, dv_ref[...]], axis=1))
        dwkvb_ref[...] += _dot_tn(_bf(nkv), dkv_b)
        dnkv = _dot_nt(dkv_b, wkvb_ref[...])
        dgkv_ref[...] += jnp.sum(dnkv * xkv, axis=0, keepdims=True)
        dxn = dnkv * gkv_v
        dckv = rkv * (dxn - xkv * jnp.mean(dxn * xkv, axis=-1, keepdims=True))

        dkpe = dkf[:, 0:LANES]
        for h in range(1, MLA_HEADS):
            dkpe = dkpe + dkf[:, h * LANES:(h + 1) * LANES]
        dkr = _rope_bwd(dkpe, ck_ref[...], slo, shi)
        dlat_ref[...] = _bf(jnp.concatenate([dcq, dckv, dkr], axis=1))

    row = lambda w: pl.BlockSpec((t, w), lambda i: (i, 0))
    full = lambda shp: pl.BlockSpec(shp, lambda i: (0, 0))
    return pl.pallas_call(
        body, name="mla_post", grid=(seq // t,),
        in_specs=[row(512), row(1024), row(1024), row(512), full((1, MLA_Q_LORA)), full((1, MLA_KV_LORA)),
                  full(wqb.shape), full(wkvb.shape), row(LANES), row(LANES), row(LANES), row(LANES)],
        out_specs=[row(512), full(wqb.shape), full(wkvb.shape), full((1, MLA_Q_LORA)), full((1, MLA_KV_LORA))],
        out_shape=[jax.ShapeDtypeStruct((seq, 512), BF16), jax.ShapeDtypeStruct(wqb.shape, F32),
                   jax.ShapeDtypeStruct(wkvb.shape, F32), jax.ShapeDtypeStruct((1, MLA_Q_LORA), F32),
                   jax.ShapeDtypeStruct((1, MLA_KV_LORA), F32)],
        compiler_params=_cparams(("arbitrary",)),
    )(p32, dq, dk, dv, gq, gkv, wqb, wkvb, *tabs)


def _split_bf16(x):
    hi = _bf(x)
    return hi, _bf(x - hi.astype(F32))


def _tri_sum(x, u):
    hi, lo = _split_bf16(x)
    return _dot(hi, u) + _dot(lo, u)


def _softplus(z):
    return jnp.maximum(z, 0.0) + jnp.log(1.0 + jnp.exp(-jnp.abs(z)))


def _head_queries(q, left):
    zero = jnp.zeros_like(q)
    return jnp.where(left, q, zero) * SB_SCALE, jnp.where(left, zero, q) * SB_SCALE


ROW_GROUP = 128
SB_BWD_CHAINS_IN_FLIGHT = 8
ANY_HBM = pl.BlockSpec(memory_space=pltpu.HBM)


class _Exchange:
    def __init__(self, send, landing):
        self.send, self.landing = send, landing

    def start(self):
        self.send.start()

    def wait(self):
        self.landing.wait_recv()
        self.send.wait_send()


class _Rider:
    def __init__(self, operands, out_shapes, sem_shapes, copies):
        self.operands, self.out_shapes, self.sem_shapes, self.copies = list(operands), list(out_shapes), list(sem_shapes), copies


def _call_with_rider(body, rider, *, name, grid, in_specs, out_specs, out_shape, args, semantics, scratch=()):
    scratch = list(scratch)
    if rider is None:
        return pl.pallas_call(body, name=name, grid=grid, in_specs=in_specs, out_specs=out_specs, out_shape=out_shape,
                              scratch_shapes=scratch, compiler_params=_cparams(semantics))(*args)
    n_in, n_out, n_rin, n_rout = len(in_specs), len(out_specs), len(rider.operands), len(rider.out_shapes)

    def full_body(*refs):
        ins, r_ins = refs[:n_in], refs[n_in:n_in + n_rin]
        outs = refs[n_in + n_rin:n_in + n_rin + n_out]
        r_outs = refs[n_in + n_rin + n_out:n_in + n_rin + n_out + n_rout]
        rest = refs[n_in + n_rin + n_out + n_rout:]
        own_scratch, sems = rest[:len(scratch)], rest[len(scratch):]
        first, last = None, None
        for axis, size in enumerate(grid):
            at_start, at_end = pl.program_id(axis) == 0, pl.program_id(axis) == size - 1
            first = at_start if first is None else first & at_start
            last = at_end if last is None else last & at_end

        @pl.when(first)
        def _():
            for cp in rider.copies(r_ins, r_outs, sems):
                cp.start()

        body(*ins, *outs, *own_scratch)

        @pl.when(last)
        def _():
            for cp in rider.copies(r_ins, r_outs, sems):
                cp.wait()

    return pl.pallas_call(
        full_body, name=name, grid=grid, in_specs=list(in_specs) + [ANY_HBM] * n_rin,
        out_specs=list(out_specs) + [ANY_HBM] * n_rout, out_shape=list(out_shape) + rider.out_shapes,
        scratch_shapes=scratch + rider.sem_shapes, compiler_params=_cparams(("arbitrary",) * len(grid)),
    )(*args, *rider.operands)


def _chains(tq):
    return [(h, g) for g in range(tq // ROW_GROUP) for h in range(2)]


def _chain_pattern(g, m, tk, strict):
    r_lo, r_hi = g * ROW_GROUP, (g + 1) * ROW_GROUP - 1
    c_lo, c_hi = m * tk, (m + 1) * tk - 1
    if (c_lo >= r_hi) if strict else (c_lo > r_hi):
        return None
    if (c_hi < r_lo) if strict else (c_hi <= r_lo):
        return True
    rr = lax.broadcasted_iota(jnp.int32, (ROW_GROUP, tk), 0) + r_lo
    cc = lax.broadcasted_iota(jnp.int32, (ROW_GROUP, tk), 1) + c_lo
    return (cc < rr) if strict else (cc <= rr)


def _masked(x, pat, fill=0.0):
    return x if pat is True else jnp.where(pat, x, fill)


def _rows(g):
    return slice(g * ROW_GROUP, (g + 1) * ROW_GROUP)


def _tri_matrix(tk, cmp):
    rr = lax.broadcasted_iota(jnp.int32, (tk, tk), 0)
    cc = lax.broadcasted_iota(jnp.int32, (tk, tk), 1)
    return cmp(rr, cc).astype(BF16)


def _mla_attn_fwd(qp, kp, vp, *, tq, tk, tk_diag, rider=None):
    seq = qp.shape[0]
    neg = float(np.finfo(np.float32).min)
    chains = _chains(tq)

    def body(q_ref, k_ref, v_ref, o_ref, lse_ref):
        i = pl.program_id(1)
        left = lax.broadcasted_iota(jnp.int32, (tq, LANES), 1) < HALF
        qs = [q_ref[_rows(g), h * LANES:(h + 1) * LANES] for h, g in chains]

        def block(start, carry, m, tk):
            v = v_ref[pl.ds(start, tk), :]
            pats = [True if m is None else _chain_pattern(g, m, tk, False) for _, g in chains]
            live = [n for n, p in enumerate(pats) if p is not None]
            ss = {n: _dot_nt(qs[n], k_ref[pl.ds(start, tk), chains[n][0] * LANES:(chains[n][0] + 1) * LANES]) for n in live}
            new = list(carry)
            for n in live:
                m_old, l_old, acc = carry[n]
                s = _masked(ss[n] * MLA_SCALE, pats[n], neg)
                m_new = jnp.maximum(m_old, jnp.max(s, axis=-1, keepdims=True))
                a = jnp.exp(m_old - m_new)
                p = jnp.exp(s - m_new)
                new[n] = (m_new, a * l_old + jnp.sum(p, axis=-1, keepdims=True), a * acc + _dot(_bf(p), v))
            return tuple(new)

        init = (jnp.full((ROW_GROUP, 1), -1e30, F32), jnp.zeros((ROW_GROUP, 1), F32), jnp.zeros((ROW_GROUP, LANES), F32))
        def two_blocks(j, c):
            c = block(pl.multiple_of(2 * j * tk, tk), c, None, tk)
            return block(pl.multiple_of((2 * j + 1) * tk, tk), c, None, tk)

        carry = lax.fori_loop(0, i * (tq // tk) // 2, two_blocks, (init,) * len(chains))
        for m in range(tq // tk_diag):
            carry = block(pl.multiple_of(i * tq + m * tk_diag, tk_diag), carry, m, tk_diag)
        per_head = []
        for h in range(2):
            mine = [carry[n] for n, (ch, _) in enumerate(chains) if ch == h]
            per_head.append((jnp.concatenate([acc / l for _, l, acc in mine], axis=0),
                             jnp.concatenate([mm + jnp.log(l) for mm, l, _ in mine], axis=0)))
        o_ref[...] = jnp.where(left, per_head[0][0], per_head[1][0])
        lse_ref[...] = jnp.where(left, per_head[0][1], per_head[1][1])

    return _call_with_rider(
        body, rider, name="mla_fwd", grid=(MLA_HEADS // 2, seq // tq),
        in_specs=[pl.BlockSpec((tq, 2 * LANES), lambda p, i: (i, p)), pl.BlockSpec((seq, 2 * LANES), lambda p, i: (0, p)),
                  pl.BlockSpec((seq, LANES), lambda p, i: (0, p))],
        out_specs=[pl.BlockSpec((tq, LANES), lambda p, i: (i, p)), pl.BlockSpec((tq, LANES), lambda p, i: (i, p))],
        out_shape=[jax.ShapeDtypeStruct((seq, 512), F32), jax.ShapeDtypeStruct((seq, 512), F32)],
        args=(qp, kp, vp), semantics=("parallel", "parallel"))


def _mla_attn_bwd(qp, kp, vp, o, lse, do, *, tq, tk, tk_diag, rider=None):
    seq = qp.shape[0]
    chains = _chains(tq)

    def body(q_ref, k_ref, v_ref, o_ref, lse_ref, do_ref, dq_ref, dk_ref, dv_ref, qt_ref, dot_ref):
        i = pl.program_id(1)

        @pl.when(i == 0)
        def _():
            dk_ref[...] = jnp.zeros_like(dk_ref)
            dv_ref[...] = jnp.zeros_like(dv_ref)

        left = lax.broadcasted_iota(jnp.int32, (tq, LANES), 1) < HALF
        do_f = do_ref[...]
        prod = do_f * o_ref[...]
        lse_v = lse_ref[...]
        do_heads = (_bf(jnp.where(left, do_f, 0.0)), _bf(jnp.where(left, 0.0, do_f)))
        delta_heads = (jnp.sum(jnp.where(left, prod, 0.0), axis=-1, keepdims=True),
                       jnp.sum(jnp.where(left, 0.0, prod), axis=-1, keepdims=True))
        qs = [q_ref[_rows(g), h * LANES:(h + 1) * LANES] for h, g in chains]
        dos = [do_heads[h][_rows(g)] for h, g in chains]
        deltas = [delta_heads[h][_rows(g)] for h, g in chains]
        lses = [lse_v[_rows(g), h * HALF:h * HALF + 1] for h, g in chains]
        for h in range(2):
            qt_ref[h] = q_ref[:, h * LANES:(h + 1) * LANES].T
            dot_ref[h] = do_heads[h].T
        q_t = [qt_ref.at[h] for h in range(2)]
        do_t = [dot_ref.at[h] for h in range(2)]

        def block(start, carry, m, tk):
            v = v_ref[pl.ds(start, tk), :]
            pats = [True if m is None else _chain_pattern(g, m, tk, False) for _, g in chains]
            live = [n for n, p in enumerate(pats) if p is not None]
            ks = [k_ref[pl.ds(start, tk), h * LANES:(h + 1) * LANES] for h in range(2)]
            ss = {n: _dot_nt(qs[n], ks[chains[n][0]]) for n in live}
            dps = {n: _dot_nt(dos[n], v) for n in live}
            new = list(carry)
            ps, dss = {}, {}
            for n in live:
                p = _masked(jnp.exp(ss[n] * MLA_SCALE - lses[n]), pats[n])
                ps[n] = _bf(p)
                dss[n] = _bf(p * (dps[n] - deltas[n]) * MLA_SCALE)
                new[n] = carry[n] + _dot(dss[n], ks[chains[n][0]])
            dv_t, dk_t = None, []
            for h in range(2):
                mine = [n for n in live if chains[n][0] == h]
                first_row = chains[mine[0]][1] * ROW_GROUP
                ds_cat = jnp.concatenate([dss[n] for n in mine], axis=0)
                p_cat = jnp.concatenate([ps[n] for n in mine], axis=0)
                if first_row == 0:
                    q_rows_t, do_rows_t = q_t[h][...], do_t[h][...]
                else:
                    q_rows_t = q_ref[first_row:, h * LANES:(h + 1) * LANES].T
                    do_rows_t = do_heads[h][first_row:].T
                dk_t.append(_dot(q_rows_t, ds_cat))
                term = _dot(do_rows_t, p_cat)
                dv_t = term if dv_t is None else dv_t + term
            back = jnp.concatenate(dk_t + [dv_t], axis=0).T
            dk_ref[pl.ds(start, tk), :] += back[:, :2 * LANES]
            dv_ref[pl.ds(start, tk), :] += back[:, 2 * LANES:]
            return tuple(new)

        zero = jnp.zeros((ROW_GROUP, LANES), F32)
        carry = lax.fori_loop(0, i * (tq // tk), lambda j, c: block(pl.multiple_of(j * tk, tk), c, None, tk),
                              (zero,) * len(chains))
        for m in range(tq // tk_diag):
            carry = block(pl.multiple_of(i * tq + m * tk_diag, tk_diag), carry, m, tk_diag)
        for n, (h, g) in enumerate(chains):
            dq_ref[_rows(g), h * LANES:(h + 1) * LANES] = carry[n]

    two_t = pl.BlockSpec((tq, 2 * LANES), lambda p, i: (i, p))
    two_s = pl.BlockSpec((seq, 2 * LANES), lambda p, i: (0, p))
    pair_t = pl.BlockSpec((tq, LANES), lambda p, i: (i, p))
    pair_s = pl.BlockSpec((seq, LANES), lambda p, i: (0, p))
    return _call_with_rider(
        body, rider, name="mla_bwd", grid=(MLA_HEADS // 2, seq // tq),
        in_specs=[two_t, two_s, pair_s, pair_t, pair_t, pair_t],
        out_specs=[two_t, two_s, pair_s],
        out_shape=[jax.ShapeDtypeStruct((seq, 1024), F32), jax.ShapeDtypeStruct((seq, 1024), F32),
                   jax.ShapeDtypeStruct((seq, 512), F32)],
        args=(qp, kp, vp, o, lse, do), semantics=("parallel", "arbitrary"),
        scratch=[pltpu.VMEM((2, LANES, tq), BF16), pltpu.VMEM((2, LANES, tq), BF16)])


def _sb_attn_fwd(pbf, *, tq, tk):
    seq = pbf.shape[0]
    nd = tq // tk
    qb, kb, vb = BLK_QB * 4, BLK_KB * 4, BLK_VB * 4
    chains = _chains(tq)

    def body(q_ref, k_ref, v_ref, o_ref, tot_ref):
        i = pl.program_id(1)
        u_later = _tri_matrix(tk, lambda r, c: r > c)
        left = lax.broadcasted_iota(jnp.int32, (tq, LANES), 1) < HALF
        q_heads = _head_queries(q_ref[...], left)
        qs = [q_heads[h][_rows(g)] for h, g in chains]

        def block(j, carry, m):
            start = pl.multiple_of(j * tk, tk)
            k = k_ref[pl.ds(start, tk), :]
            v = v_ref[pl.ds(start, tk), :]
            pats = [True if m is None else _chain_pattern(g, m, tk, True) for _, g in chains]
            live = [n for n, p in enumerate(pats) if p is not None]
            zs = {n: _dot_nt(qs[n], k) for n in live}
            raws = {n: _softplus(zs[n]) for n in live}
            sps = {n: _masked(raws[n], pats[n]) for n in live}
            laters = {n: _tri_sum(sps[n], u_later) for n in live}
            new = list(carry)
            for n in live:
                c, acc = carry[n]
                a = _masked(jnp.exp(zs[n] - raws[n] - laters[n] - c), pats[n])
                new[n] = (c + laters[n][:, 0:1] + sps[n][:, 0:1], acc + _dot(_bf(a), v))
            return tuple(new)

        init = (jnp.zeros((ROW_GROUP, 1), F32), jnp.zeros((ROW_GROUP, LANES), F32))
        carry = (init,) * len(chains)
        for m in reversed(range(nd)):
            carry = block(i * nd + m, carry, m)
        per_trip = 4 if nd % 4 == 0 else 2

        def trip(jj, cr):
            for u in range(per_trip):
                cr = block(i * nd - 1 - (per_trip * jj + u), cr, None)
            return cr

        carry = lax.fori_loop(0, i * nd // per_trip, trip, carry)
        per_head = []
        for h in range(2):
            mine = [carry[n] for n, (ch, _) in enumerate(chains) if ch == h]
            per_head.append((jnp.concatenate([acc for _, acc in mine], axis=0), jnp.concatenate([c for c, _ in mine], axis=0)))
        o_ref[...] = jnp.where(left, per_head[0][0], per_head[1][0])
        tot_ref[...] = jnp.where(left, per_head[0][1], per_head[1][1])

    pair_t = pl.BlockSpec((tq, LANES), lambda p, i: (i, p))
    return pl.pallas_call(
        body, name="sb_fwd", grid=(SB_HEADS // 2, seq // tq),
        in_specs=[pl.BlockSpec((tq, LANES), lambda p, i: (i, qb + p)), pl.BlockSpec((seq, LANES), lambda p, i: (0, kb + p)),
                  pl.BlockSpec((seq, LANES), lambda p, i: (0, vb + p))],
        out_specs=[pair_t, pair_t],
        out_shape=[jax.ShapeDtypeStruct((seq, 512), F32), jax.ShapeDtypeStruct((seq, 512), F32)],
        compiler_params=_cparams(("parallel", "parallel")),
    )(pbf, pbf, pbf)


def _sb_attn_bwd(pbf, tot, do, *, tq, tk, rider=None):
    seq = pbf.shape[0]
    nd = tq // tk
    qb, kb, vb = BLK_QB * 4, BLK_KB * 4, BLK_VB * 4
    chains = _chains(tq)
    group = SB_BWD_CHAINS_IN_FLIGHT

    def body(q_ref, k_ref, v_ref, tot_ref, do_ref, dq_ref, dk_ref, dv_ref, qt_ref, dot_ref):
        i = pl.program_id(1)

        @pl.when(i == 0)
        def _():
            dk_ref[...] = jnp.zeros_like(dk_ref)
            dv_ref[...] = jnp.zeros_like(dv_ref)

        u_upto = _tri_matrix(tk, lambda r, c: r <= c)
        u_below = _tri_matrix(tk, lambda r, c: r < c)
        left = lax.broadcasted_iota(jnp.int32, (tq, LANES), 1) < HALF
        q_heads = _head_queries(q_ref[...], left)
        do_f = do_ref[...]
        do_heads = (_bf(jnp.where(left, do_f, 0.0)), _bf(jnp.where(left, 0.0, do_f)))
        tot_v = tot_ref[...]
        qs = [q_heads[h][_rows(g)] for h, g in chains]
        dos = [do_heads[h][_rows(g)] for h, g in chains]
        totals = [tot_v[_rows(g), h * HALF:h * HALF + 1] for h, g in chains]
        qt_ref[...] = jnp.concatenate(qs, axis=0).T
        dot_ref[...] = jnp.concatenate(dos, axis=0).T

        def block(j, carry, m):
            start = pl.multiple_of(j * tk, tk)
            k = k_ref[pl.ds(start, tk), :]
            v = v_ref[pl.ds(start, tk), :]
            pats = [True if m is None else _chain_pattern(g, m, tk, True) for _, g in chains]
            all_live = [n for n, p in enumerate(pats) if p is not None]
            new = list(carry)
            for g0 in range(0, len(all_live), group):
                live = all_live[g0:g0 + group]
                zs = {n: _dot_nt(qs[n], k) for n in live}
                das = {n: _dot_nt(dos[n], v) for n in live}
                raws = {n: _softplus(zs[n]) for n in live}
                sps = {n: _masked(raws[n], pats[n]) for n in live}
                uptos = {n: _tri_sum(sps[n], u_upto) for n in live}
                lbs, a_s, gs = {}, {}, {}
                for n in live:
                    lbs[n] = zs[n] - raws[n]
                    a = _masked(jnp.exp(lbs[n] - (totals[n] - carry[n][0] - uptos[n])), pats[n])
                    a_s[n] = _bf(a)
                    gs[n] = das[n] * a
                belows = {n: _dot(_bf(gs[n]), u_below) for n in live}
                dzs = {}
                for n in live:
                    sp_before, g_before, dq_acc = carry[n]
                    beta = jnp.exp(lbs[n])
                    dz = _masked(gs[n] * (1.0 - beta) - (g_before + belows[n]) * beta, pats[n])
                    dzs[n] = _bf(dz)
                    new[n] = (sp_before + uptos[n][:, tk - 1:tk], g_before + belows[n][:, tk - 1:tk] + gs[n][:, tk - 1:tk],
                              dq_acc + _dot(dzs[n], k))
                dz_cat = jnp.concatenate([dzs[n] for n in live], axis=0)
                a_cat = jnp.concatenate([a_s[n] for n in live], axis=0)
                if len(live) == len(chains):
                    q_rows_t, do_rows_t = qt_ref[...], dot_ref[...]
                else:
                    q_rows_t = jnp.concatenate([qs[n] for n in live], axis=0).T
                    do_rows_t = jnp.concatenate([dos[n] for n in live], axis=0).T
                both = jnp.concatenate([_dot(q_rows_t, dz_cat), _dot(do_rows_t, a_cat)], axis=0).T
                dk_ref[pl.ds(start, tk), :] += both[:, :LANES]
                dv_ref[pl.ds(start, tk), :] += both[:, LANES:]
            return tuple(new)

        zero = jnp.zeros((ROW_GROUP, 1), F32)
        init = (zero, zero, jnp.zeros((ROW_GROUP, LANES), F32))
        carry = lax.fori_loop(0, i * nd // 2, lambda j, cr: block(2 * j + 1, block(2 * j, cr, None), None),
                              (init,) * len(chains))
        for m in range(nd):
            carry = block(i * nd + m, carry, m)
        per_head = [jnp.concatenate([carry[n][2] for n, (ch, _) in enumerate(chains) if ch == h], axis=0) for h in range(2)]
        dq_ref[...] = jnp.where(left, per_head[0], per_head[1]) * SB_SCALE

    pair_t = pl.BlockSpec((tq, LANES), lambda p, i: (i, p))
    pair_s = pl.BlockSpec((seq, LANES), lambda p, i: (0, p))
    return _call_with_rider(
        body, rider, name="sb_bwd", grid=(SB_HEADS // 2, seq // tq),
        in_specs=[pl.BlockSpec((tq, LANES), lambda p, i: (i, qb + p)), pl.BlockSpec((seq, LANES), lambda p, i: (0, kb + p)),
                  pl.BlockSpec((seq, LANES), lambda p, i: (0, vb + p)), pair_t, pair_t],
        out_specs=[pair_t, pair_s, pair_s],
        out_shape=[jax.ShapeDtypeStruct((seq, 512), F32)] * 3,
        args=(pbf, pbf, pbf, tot, do), semantics=("parallel", "arbitrary"),
        scratch=[pltpu.VMEM((LANES, 2 * tq), BF16), pltpu.VMEM((LANES, 2 * tq), BF16)])


def _mem_probs(s):
    e = jnp.exp(s - jnp.max(s, axis=-1, keepdims=True))
    return e / jnp.sum(e, axis=-1, keepdims=True)


def _head_lanes(h):
    return slice(h * LANES, (h + 1) * LANES)


def _mem_fwd(pbf, mkv, *, t):
    seq = pbf.shape[0]

    def body(q_ref, kv_ref, o_ref):
        ss = [_dot_nt(q_ref[:, _head_lanes(h)], kv_ref[:, _head_lanes(h)]) * MEM_SCALE for h in range(MEM_HEADS)]
        ps = [_bf(_mem_probs(s)) for s in ss]
        for h, p in enumerate(ps):
            o_ref[:, _head_lanes(h)] = _dot(p, kv_ref[:, _head_lanes(MEM_HEADS + h)])

    return pl.pallas_call(
        body, name="mem_fwd", grid=(seq // t,),
        in_specs=[pl.BlockSpec((t, 512), lambda i: (i, BLK_QM)), pl.BlockSpec((MEM_LEN, 1024), lambda i: (0, 0))],
        out_specs=pl.BlockSpec((t, 512), lambda i: (i, 0)),
        out_shape=jax.ShapeDtypeStruct((seq, 512), F32),
        compiler_params=_cparams(("parallel",)),
    )(pbf, mkv)


def _mem_bwd(pbf, mkv, do, *, t):
    seq = pbf.shape[0]

    def body(q_ref, kv_ref, do_ref, dq_ref, dkv_ref):
        @pl.when(pl.program_id(0) == 0)
        def _():
            dkv_ref[...] = jnp.zeros_like(dkv_ref)

        heads = range(MEM_HEADS)
        qs = [q_ref[:, _head_lanes(h)] for h in heads]
        ks = [kv_ref[:, _head_lanes(h)] for h in heads]
        dos = [_bf(do_ref[:, _head_lanes(h)]) for h in heads]
        ss = [_dot_nt(qs[h], ks[h]) * MEM_SCALE for h in heads]
        dps = [_dot_nt(dos[h], kv_ref[:, _head_lanes(MEM_HEADS + h)]) for h in heads]
        ps = [_mem_probs(s) for s in ss]
        dss = [_bf(ps[h] * (dps[h] - jnp.sum(dps[h] * ps[h], axis=-1, keepdims=True)) * MEM_SCALE) for h in heads]
        for h in heads:
            dq_ref[:, _head_lanes(h)] = _dot(dss[h], ks[h])
        for h in heads:
            dkv_ref[:, _head_lanes(h)] += _dot_tn(dss[h], qs[h])
            dkv_ref[:, _head_lanes(MEM_HEADS + h)] += _dot_tn(_bf(ps[h]), dos[h])

    return pl.pallas_call(
        body, name="mem_bwd", grid=(seq // t,),
        in_specs=[pl.BlockSpec((t, 512), lambda i: (i, BLK_QM)), pl.BlockSpec((MEM_LEN, 1024), lambda i: (0, 0)),
                  pl.BlockSpec((t, 512), lambda i: (i, 0))],
        out_specs=[pl.BlockSpec((t, 512), lambda i: (i, 0)), pl.BlockSpec((MEM_LEN, 1024), lambda i: (0, 0))],
        out_shape=[jax.ShapeDtypeStruct((seq, 512), F32), jax.ShapeDtypeStruct((MEM_LEN, 1024), F32)],
        compiler_params=_cparams(("arbitrary",)),
    )(pbf, mkv, do)


def _mid(x, tgt, o_a, o_b, o_m, p32, wmg, bmg, wba, wbb, wbm, wout, ln_g, ln_b, *, t):
    seq = x.shape[0]
    inv_d = 1.0 / D_MODEL

    def body(x_ref, t_ref, oa_ref, ob_ref, om_ref, ga_ref, gb_ref, gm_ref, wmg_ref, bmg_ref, wba_ref, wbb_ref,
             wbm_ref, wout_ref, lg_ref, lb_ref,
             du_ref, mrg_ref, dgp_ref, ha_ref, hb_ref, hm_ref, dya_ref, dyb_ref, dym_ref, doa_ref, dob_ref, dom_ref,
             dga_ref, dgb_ref, dgm_ref, dgain_ref, dbias_ref, dbmg_ref, loss_ref):
        @pl.when(pl.program_id(0) == 0)
        def _():
            dgain_ref[...] = jnp.zeros_like(dgain_ref)
            dbias_ref[...] = jnp.zeros_like(dbias_ref)
            dbmg_ref[...] = jnp.zeros_like(dbmg_ref)
            loss_ref[...] = jnp.zeros_like(loss_ref)

        xv = x_ref[...]
        gate = _sigmoid(_dot(_bf(xv), wmg_ref[...]) + bmg_ref[...])

        branches = []
        merged = None
        for b, (o_ref, g_ref, w_ref, h_ref) in enumerate(((oa_ref, ga_ref, wba_ref, ha_ref), (ob_ref, gb_ref, wbb_ref, hb_ref),
                                                         (om_ref, gm_ref, wbm_ref, hm_ref))):
            o, gt = o_ref[...], g_ref[...]
            sg = _sigmoid(gt)
            silu = gt * sg
            h = _bf(o * silu)
            h_ref[...] = h
            y = _dot(h, w_ref[...])
            g_b = gate[:, b * D_MODEL:(b + 1) * D_MODEL]
            term = g_b * y
            merged = term if merged is None else merged + term
            branches.append((o, gt, sg, silu, y, g_b))
        mrg_b = _bf(merged)
        mrg_ref[...] = mrg_b

        u = DEEPNORM_ALPHA * xv + _dot(mrg_b, wout_ref[...])
        mu = jnp.mean(u, axis=-1, keepdims=True)
        uc = u - mu
        rstd = lax.rsqrt(jnp.mean(uc * uc, axis=-1, keepdims=True) + LN_EPS)
        xhat = uc * rstd
        lg = lg_ref[...]
        y_out = xhat * lg + lb_ref[...]
        err = y_out - t_ref[...]
        loss_ref[...] += 0.5 * jnp.sum(jnp.mean(err * err, axis=-1, keepdims=True), axis=0, keepdims=True)
        dy = err * inv_d
        dgain_ref[...] += jnp.sum(dy * xhat, axis=0, keepdims=True)
        dbias_ref[...] += jnp.sum(dy, axis=0, keepdims=True)
        dxh = dy * lg
        du = rstd * (dxh - jnp.mean(dxh, axis=-1, keepdims=True) - xhat * jnp.mean(dxh * xhat, axis=-1, keepdims=True))
        du_ref[...] = du

        dmerged = _dot_nt(_bf(du), wout_ref[...])
        outs = ((dya_ref, doa_ref, dga_ref, wba_ref), (dyb_ref, dob_ref, dgb_ref, wbb_ref), (dym_ref, dom_ref, dgm_ref, wbm_ref))
        dgp = []
        for (o, gt, sg, silu, y, g_b), (dy_ref, do_ref, dg_ref, w_ref) in zip(branches, outs):
            dyb = _bf(dmerged * g_b)
            dy_ref[...] = dyb
            dgp.append(dmerged * y * g_b * (1.0 - g_b))
            dh = _dot_nt(dyb, w_ref[...])
            do_ref[...] = dh * silu
            dg_ref[...] = _bf(dh * o * (sg * (1.0 + gt * (1.0 - sg))))
        dgp = jnp.concatenate(dgp, axis=1)
        dgp_ref[...] = _bf(dgp)
        dbmg_ref[...] += jnp.sum(dgp, axis=0, keepdims=True)

    row = lambda w: pl.BlockSpec((t, w), lambda i: (i, 0))
    pblk = lambda c: pl.BlockSpec((t, 512), lambda i: (i, c))
    full = lambda shp: pl.BlockSpec(shp, lambda i: (0, 0))
    sds = jax.ShapeDtypeStruct
    return pl.pallas_call(
        body, name="mid", grid=(seq // t,),
        in_specs=[row(1024), row(1024), row(512), row(512), row(512), pblk(BLK_GATE_A), pblk(BLK_GATE_B), pblk(BLK_GATE_M),
                  full(wmg.shape), full((1, N_MERGE)), full(wba.shape), full(wbb.shape), full(wbm.shape), full(wout.shape),
                  full((1, D_MODEL)), full((1, D_MODEL))],
        out_specs=[row(1024), row(1024), row(N_MERGE), row(512), row(512), row(512), row(1024), row(1024), row(1024),
                   row(512), row(512), row(512), row(512), row(512), row(512),
                   full((1, D_MODEL)), full((1, D_MODEL)), full((1, N_MERGE)), full((1, 1))],
        out_shape=[sds((seq, 1024), F32), sds((seq, 1024), BF16), sds((seq, N_MERGE), BF16),
                   sds((seq, 512), BF16), sds((seq, 512), BF16), sds((seq, 512), BF16),
                   sds((seq, 1024), BF16), sds((seq, 1024), BF16), sds((seq, 1024), BF16),
                   sds((seq, 512), F32), sds((seq, 512), F32), sds((seq, 512), F32),
                   sds((seq, 512), BF16), sds((seq, 512), BF16), sds((seq, 512), BF16),
                   sds((1, D_MODEL), F32), sds((1, D_MODEL), F32), sds((1, N_MERGE), F32), sds((1, 1), F32)],
        compiler_params=_cparams(("arbitrary",)),
    )(x, tgt, o_a, o_b, o_m, p32, p32, p32, wmg, bmg, wba, wbb, wbm, wout, ln_g, ln_b)


def _primed_weights(w):
    w_in = w["w_in"]
    zc = lambda n: jnp.zeros((D_MODEL, n), w_in.dtype)
    w_in_p = jnp.concatenate([w_in[:, 0:384], zc(64), w_in[:, 384:416], zc(32), w_in[:, 416:]], axis=1)
    wqb = jnp.pad(w["w_q_b"].reshape(MLA_Q_LORA, MLA_HEADS, 96), ((0, 0), (0, 0), (0, 32))).reshape(MLA_Q_LORA, 1024)
    kv3 = w["w_kv_b"].reshape(MLA_KV_LORA, MLA_HEADS, 128)
    wk = jnp.pad(kv3[:, :, :MLA_NOPE], ((0, 0), (0, 0), (0, 64))).reshape(MLA_KV_LORA, 1024)
    wv = kv3[:, :, MLA_NOPE:].reshape(MLA_KV_LORA, 512)
    return w_in_p, wqb, jnp.concatenate([wk, wv], axis=1)


PROJ_BLK = 512


def _grad_x(du, dgpre, wmg, d_proj, w_in_p, *, tm, rider=None):
    seq = du.shape[0]
    n_pieces = len(d_proj)

    def body(du_ref, dg_ref, wmg_ref, *rest):
        piece_refs, win_ref, out_ref = rest[:n_pieces], rest[n_pieces], rest[n_pieces + 1]
        d_p = jnp.concatenate([_bf(p_ref[...]) for p_ref in piece_refs], axis=1)
        out_ref[...] = (DEEPNORM_ALPHA * du_ref[...] + _dot_nt(dg_ref[...], wmg_ref[...])) + _dot_nt(d_p, win_ref[...])

    row = lambda w: pl.BlockSpec((tm, w), lambda i: (i, 0))
    whole = lambda a: pl.BlockSpec(a.shape, lambda i: (0, 0))
    return _call_with_rider(
        body, rider, name="grad_x", grid=(seq // tm,),
        in_specs=[row(D_MODEL), row(N_MERGE), whole(wmg)] + [row(PROJ_BLK) for _ in d_proj] + [whole(w_in_p)],
        out_specs=[row(D_MODEL)], out_shape=[jax.ShapeDtypeStruct((seq, D_MODEL), F32)],
        args=(du, dgpre, wmg, *d_proj, w_in_p), semantics=("parallel",))


def _grad_w_in(x, d_proj, *, tk):
    seq = x.shape[0]
    n_pieces = len(d_proj)
    nk = seq // tk

    def body(x_ref, *rest):
        piece_refs, out_ref, acc = rest[:n_pieces], rest[n_pieces], rest[n_pieces + 1]
        j, kk = pl.program_id(0), pl.program_id(1)

        @pl.when(kk == 0)
        def _():
            acc[...] = jnp.zeros_like(acc)

        xb = _bf(x_ref[...])
        for pair in range(n_pieces // 2):
            @pl.when(j == pair)
            def _(pair=pair):
                both = jnp.concatenate([_bf(piece_refs[2 * pair][...]), _bf(piece_refs[2 * pair + 1][...])], axis=1)
                acc[...] += _dot_tn(xb, both)

        @pl.when(kk == nk - 1)
        def _():
            out_ref[...] = acc[...]

    def piece_spec(s):
        return pl.BlockSpec((tk, PROJ_BLK), lambda j, kk: (jnp.where(j == s // 2, kk, 0), 0))

    return pl.pallas_call(
        body, name="grad_w_in", grid=(n_pieces // 2, nk),
        in_specs=[pl.BlockSpec((tk, D_MODEL), lambda j, kk: (kk, 0))] + [piece_spec(s) for s in range(n_pieces)],
        out_specs=pl.BlockSpec((D_MODEL, 2 * PROJ_BLK), lambda j, kk: (0, j)),
        out_shape=jax.ShapeDtypeStruct((D_MODEL, n_pieces * PROJ_BLK), F32),
        scratch_shapes=[pltpu.VMEM((D_MODEL, 2 * PROJ_BLK), F32)],
        compiler_params=_cparams(("parallel", "arbitrary")),
    )(x, *d_proj)


EARLY_NAMES = ("w_mem_kv", "w_branch_mla", "w_branch_sb", "w_branch_mem", "w_merge_gate", "w_out")
LATE_NAMES = ("w_in", "w_q_b", "w_kv_b")


def _remote(src, dst, send_sem, recv_sem, device):
    return pltpu.make_async_remote_copy(src_ref=src, dst_ref=dst, send_sem=send_sem, recv_sem=recv_sem, device_id=device,
                                        device_id_type=MESH_ID)


def _gather_rider(shards):
    n = len(shards)

    def copies(src_refs, out_refs, sems):
        send_sems, recv_sems, local_sems = sems
        x, y, c = _place()
        me = 2 * x + y
        out = []
        for a, (s, o) in enumerate(zip(src_refs, out_refs)):
            out.append(pltpu.make_async_copy(s, o.at[me], local_sems.at[a]))
            for k, (px, py) in enumerate(_other_chips(x, y)):
                out.append(_Exchange(_remote(s, o.at[me], send_sems.at[k, a], recv_sems.at[k, a], (px, py, c)),
                                     _remote(s, o.at[2 * px + py], send_sems.at[k, a], recv_sems.at[k, a], (px, py, c))))
        return out

    return _Rider(shards, [jax.ShapeDtypeStruct((N_CHIPS,) + s.shape, s.dtype) for s in shards],
                  [pltpu.SemaphoreType.DMA((3, n)), pltpu.SemaphoreType.DMA((3, n)), pltpu.SemaphoreType.DMA((n,))], copies)


def _sibling_rider(g4):
    n = len(g4)

    def copies(g_refs, out_refs, sems):
        send_sems, recv_sems = sems
        x, y, c = _place()
        out = []
        for a, (g, o) in enumerate(zip(g_refs, out_refs)):
            half = g.shape[1] // 2
            theirs = pl.ds(pl.multiple_of((1 - c) * half, 8), half)
            cp = _remote(g.at[:, theirs, :], o, send_sems.at[a], recv_sems.at[a], (x, y, 1 - c))
            out.append(_Exchange(cp, cp))
        return out

    return _Rider(g4, [jax.ShapeDtypeStruct((N_CHIPS, g.shape[1] // 2, g.shape[2]), g.dtype) for g in g4],
                  [pltpu.SemaphoreType.DMA((n,)), pltpu.SemaphoreType.DMA((n,))], copies)


def _chips_rider(wire):
    n = len(wire)

    def copies(s_refs, out_refs, sems):
        send_sems, recv_sems = sems
        x, y, c = _place()
        out = []
        for a, (s, o) in enumerate(zip(s_refs, out_refs)):
            for k, (px, py) in enumerate(_other_chips(x, y)):
                cp = _remote(s.at[2 * px + py], o.at[RELATION_XOR[k] - 1], send_sems.at[k, a], recv_sems.at[k, a], (px, py, c))
                out.append(_Exchange(cp, cp))
        return out

    return _Rider(wire, [jax.ShapeDtypeStruct((3,) + s.shape[1:], s.dtype) for s in wire],
                  [pltpu.SemaphoreType.DMA((3, n)), pltpu.SemaphoreType.DMA((3, n))], copies)


def _local_step(x, mem, tgt, w, small, *, tq, tq_sb_bwd, tk, tk_mla, t_row, t_mm, t_wg, rest_shards=None):
    seq = x.shape[0]
    on_mesh = rest_shards is not None
    w_in_p, wqb, wkvb = _primed_weights(w)
    tabs = _rope_tables(seq)

    p32, pbf = _matmul(x, w_in_p, mode="nn", tm=256, tn=IN_WIDTH_P, tk=D_MODEL, out_dtypes=(F32, BF16), name="proj_in")
    qp, kp, vp = _mla_prep(p32, small["q_a_gain"], small["kv_a_gain"], wqb, wkvb, tabs, t=t_row)
    res = _mla_attn_fwd(qp, kp, vp, tq=tq, tk=tk_mla, tk_diag=tk, rider=_gather_rider(rest_shards) if on_mesh else None)
    o_a, lse = res[0], res[1]
    if on_mesh:
        w = dict(w, **{n: _join_chips(n, g) for n, g in zip(EARLY_NAMES, res[2:])})
    wmg, wout = w["w_merge_gate"], w["w_out"]
    wba, wbb, wbm = w["w_branch_mla"], w["w_branch_sb"], w["w_branch_mem"]
    o_b, keep_total = _sb_attn_fwd(pbf, tq=tq, tk=tk)
    (mkv,) = _matmul(mem, w["w_mem_kv"], mode="nn", tm=MEM_LEN, tn=512, tk=D_MODEL, out_dtypes=(BF16,), name="mem_kv")
    o_m = _mem_fwd(pbf, mkv, t=t_row)

    (du, merged, dgpre, h_a, h_b, h_m, dy_a, dy_b, dy_m, do_a, do_b, do_m, dgate_a, dgate_b, dgate_m,
     d_ln_g, d_ln_b, d_bmg, loss) = _mid(x, tgt, o_a, o_b, o_m, p32, wmg, small["b_merge_gate"], wba, wbb, wbm, wout,
                                         small["ln_gain"], small["ln_bias"], t=t_row)

    wg = functools.partial(_matmul, mode="tn", tm=512, tn=1024, out_dtypes=(F32,))
    dq_m, dmkv = _mem_bwd(pbf, mkv, do_m, t=t_row)
    early = {"w_mem_kv": wg(mem, dmkv, tk=MEM_LEN, name="grad_w_mem_kv")[0],
             "w_branch_mla": wg(h_a, dy_a, tk=t_wg, name="grad_w_branch_a")[0],
             "w_branch_sb": wg(h_b, dy_b, tk=t_wg, name="grad_w_branch_b")[0],
             "w_branch_mem": wg(h_m, dy_m, tk=t_wg, name="grad_w_branch_m")[0],
             "w_merge_gate": wg(x, dgpre, tk=t_wg, name="grad_w_merge_gate")[0],
             "w_out": wg(merged, du, tk=t_wg, name="grad_w_out")[0]}

    if on_mesh:
        g4 = [_split_by_chip(n, early[n]) for n in EARLY_NAMES]
        res = _mla_attn_bwd(qp, kp, vp, o_a, lse, do_a, tq=tq, tk=tk_mla, tk_diag=tk, rider=_sibling_rider(g4))
        (dqp, dkp, dvp), got = res[:3], res[3:]
        chipsum, wire = _rs_add_sibling(g4, got, [BF16] * len(g4))
        res = _sb_attn_bwd(pbf, keep_total, do_b, tq=tq_sb_bwd, tk=tk, rider=_chips_rider(wire))
        (dq_b, dk_b, dv_b), parts = res[:3], res[3:]
        early = _rs_add_chips(chipsum, parts)
    else:
        dqp, dkp, dvp = _mla_attn_bwd(qp, kp, vp, o_a, lse, do_a, tq=tq, tk=tk_mla, tk_diag=tk)
        dq_b, dk_b, dv_b = _sb_attn_bwd(pbf, keep_total, do_b, tq=tq_sb_bwd, tk=tk)
    dlat, d_wqb, d_wkvb, d_gq, d_gkv = _mla_post(p32, dqp, dkp, dvp, small["q_a_gain"], small["kv_a_gain"], wqb, wkvb, tabs,
                                                 t=t_row)

    d_proj = [dlat, dgate_a, dq_b, dk_b, dv_b, dgate_b, dq_m, dgate_m]
    d_winp = _grad_w_in(x, d_proj, tk=min(1024, seq))

    d_win = jnp.concatenate([d_winp[:, 0:384], d_winp[:, 448:480], d_winp[:, 512:]], axis=1)
    d_wq = d_wqb.reshape(MLA_Q_LORA, MLA_HEADS, 128)[:, :, :96].reshape(MLA_Q_LORA, 768)
    d_wk = d_wkvb[:, :1024].reshape(MLA_KV_LORA, MLA_HEADS, 128)[:, :, :MLA_NOPE]
    d_wv = d_wkvb[:, 1024:].reshape(MLA_KV_LORA, MLA_HEADS, MLA_V)
    d_wkv = jnp.concatenate([d_wk, d_wv], axis=2).reshape(MLA_KV_LORA, 1024)
    late = {"w_in": d_win, "w_q_b": d_wq, "w_kv_b": d_wkv}
    small_grads = {"q_a_gain": d_gq, "kv_a_gain": d_gkv, "b_merge_gate": d_bmg, "ln_gain": d_ln_g, "ln_bias": d_ln_b}
    if not on_mesh:
        (grad_x,) = _grad_x(du, dgpre, wmg, d_proj, w_in_p, tm=256)
        return loss[0, 0], grad_x, late, small_grads, early

    g4 = [_split_by_chip(n, late[n]) for n in LATE_NAMES]
    g4.append(jnp.broadcast_to(_pack_small(small_grads)[None], (N_CHIPS, SMALL_ROWS, PACK_COLS)))
    got = _rs_to_sibling(g4)
    chipsum, wire = _rs_add_sibling(g4, got, [BF16] * len(LATE_NAMES) + [F32])
    res = _grad_x(du, dgpre, wmg, d_proj, w_in_p, tm=256, rider=_chips_rider(wire))
    late_mine = _rs_add_chips(chipsum, res[1:])
    return loss[0, 0], res[0], late_mine, None, early


def _place():
    x, y, c = lax.axis_index("x"), lax.axis_index("y"), lax.axis_index("c")
    return x, y, c


def _other_chips(x, y):
    return ((1 - x, y), (x, 1 - y), (1 - x, 1 - y))


SMALL_ROWS = 64
ADAM_STEPS_PER_HALF = 4


def _pack_small(d):
    flat = jnp.concatenate([d[n].reshape(-1) for n, _ in SMALL_SIZES])
    return jnp.pad(flat, (0, SMALL_ROWS * PACK_COLS - SMALL_TOTAL)).reshape(SMALL_ROWS, PACK_COLS)


def _unpack_small(a):
    flat, out, c0 = a.reshape(-1), {}, 0
    for n, size in SMALL_SIZES:
        out[n] = flat[c0:c0 + size].reshape(1, size)
        c0 += size
    return out


def _split_by_chip(name, full):
    r, c = full.shape
    if name in COL_SHARDED:
        return full.reshape(r, N_CHIPS, c // N_CHIPS).transpose(1, 0, 2)
    return full.reshape(N_CHIPS, r // N_CHIPS, c)


def _join_chips(name, slots):
    _, r, cs = slots.shape
    if name in COL_SHARDED:
        return slots.transpose(1, 0, 2).reshape(r, N_CHIPS * cs)
    return slots.reshape(N_CHIPS * r, cs)


HBM_SPEC = pl.BlockSpec(memory_space=pltpu.HBM)


def _gather_shards(shards):
    n = len(shards)

    def body(*refs):
        w_refs, out_refs, wb_refs = refs[:n], refs[n:2 * n], refs[2 * n:3 * n]
        send_sems, recv_sems, pass_send_sems, pass_recv_sems, local_sems = refs[3 * n:]
        x, y, c = _place()
        me = 2 * x + y
        sibling = (x, y, 1 - c)

        def halves(ref):
            half = ref.shape[-2] // 2
            return (pl.ds(pl.multiple_of(c * half, 16), half), pl.ds(pl.multiple_of((1 - c) * half, 16), half))
        for w_ref, wb_ref in zip(w_refs, wb_refs):
            rows = w_ref.shape[0]
            chunk = min(rows, 128)

            def cast(i, carry, w_ref=w_ref, wb_ref=wb_ref, chunk=chunk):
                r0 = pl.multiple_of(i * chunk, chunk)
                wb_ref[pl.ds(r0, chunk), :] = _bf(w_ref[pl.ds(r0, chunk), :])
                return carry

            lax.fori_loop(0, rows // chunk, cast, 0)
        sends, locals_ = [], []
        for a, (wb_ref, out_ref) in enumerate(zip(wb_refs, out_refs)):
            mine = pltpu.make_async_copy(wb_ref, out_ref.at[me], local_sems.at[a])
            mine.start()
            locals_.append(mine)
            mine_rows, _ = halves(wb_ref)
            for k, (px, py) in enumerate(_other_chips(x, y)):
                cp = pltpu.make_async_remote_copy(src_ref=wb_ref.at[mine_rows, :], dst_ref=out_ref.at[me, mine_rows, :],
                                                  send_sem=send_sems.at[k, a], recv_sem=recv_sems.at[k, a],
                                                  device_id=(px, py, c), device_id_type=MESH_ID)
                cp.start()
                sends.append(cp)
        for a, (wb_ref, out_ref) in enumerate(zip(wb_refs, out_refs)):
            mine_rows, _ = halves(wb_ref)
            for k, (px, py) in enumerate(_other_chips(x, y)):
                landed = out_ref.at[2 * px + py, mine_rows, :]
                pltpu.make_async_remote_copy(src_ref=wb_ref.at[mine_rows, :], dst_ref=landed, send_sem=send_sems.at[k, a],
                                             recv_sem=recv_sems.at[k, a], device_id=(px, py, c),
                                             device_id_type=MESH_ID).wait_recv()
                cp = pltpu.make_async_remote_copy(src_ref=landed, dst_ref=landed, send_sem=pass_send_sems.at[k, a],
                                                  recv_sem=pass_recv_sems.at[k, a], device_id=sibling, device_id_type=MESH_ID)
                cp.start()
                sends.append(cp)
        for a, (wb_ref, out_ref) in enumerate(zip(wb_refs, out_refs)):
            _, their_rows = halves(wb_ref)
            for k, (px, py) in enumerate(_other_chips(x, y)):
                passed = out_ref.at[2 * px + py, their_rows, :]
                pltpu.make_async_remote_copy(src_ref=passed, dst_ref=passed, send_sem=pass_send_sems.at[k, a],
                                             recv_sem=pass_recv_sems.at[k, a], device_id=sibling,
                                             device_id_type=MESH_ID).wait_recv()
        for cp in sends:
            cp.wait_send()
        for cp in locals_:
            cp.wait()

    return pl.pallas_call(
        body, name="gather_weights",
        in_specs=[pl.BlockSpec(memory_space=pltpu.VMEM)] * n,
        out_specs=[HBM_SPEC] * n,
        out_shape=[jax.ShapeDtypeStruct((N_CHIPS,) + s.shape, BF16) for s in shards],
        scratch_shapes=[pltpu.VMEM(s.shape, BF16) for s in shards]
        + [pltpu.SemaphoreType.DMA((3, n))] * 4 + [pltpu.SemaphoreType.DMA((n,))],
        compiler_params=pltpu.CompilerParams(vmem_limit_bytes=VMEM_LIMIT),
    )(*shards)


def _cast_bf16_list(arrays):
    def body(*refs):
        for a_ref, o_ref in zip(refs[:len(arrays)], refs[len(arrays):]):
            o_ref[...] = _bf(a_ref[...])

    specs = [pl.BlockSpec((a.shape[0] // 4, a.shape[1]), lambda i: (i, 0)) for a in arrays]
    return pl.pallas_call(
        body, name="cast_shards", grid=(4,), in_specs=specs, out_specs=specs,
        out_shape=[jax.ShapeDtypeStruct(a.shape, BF16) for a in arrays],
        compiler_params=_cparams(("parallel",)),
    )(*arrays)


def _rs_to_sibling(g4):
    n = len(g4)

    def body(*refs):
        g_refs, out_refs = refs[:n], refs[n:2 * n]
        send_sems, recv_sems = refs[2 * n:]
        x, y, c = _place()
        copies = []
        for a, (g_ref, out_ref) in enumerate(zip(g_refs, out_refs)):
            half = g_ref.shape[1] // 2
            theirs = pl.ds(pl.multiple_of((1 - c) * half, 8), half)
            copies.append(pltpu.make_async_remote_copy(src_ref=g_ref.at[:, theirs, :], dst_ref=out_ref, send_sem=send_sems.at[a],
                                                       recv_sem=recv_sems.at[a], device_id=(x, y, 1 - c),
                                                       device_id_type=MESH_ID))
        for cp in copies:
            cp.start()
        for cp in copies:
            cp.wait()

    return pl.pallas_call(
        body, name="rs_sibling", in_specs=[HBM_SPEC] * n, out_specs=[HBM_SPEC] * n,
        out_shape=[jax.ShapeDtypeStruct((N_CHIPS, g.shape[1] // 2, g.shape[2]), F32) for g in g4],
        scratch_shapes=[pltpu.SemaphoreType.DMA((n,)), pltpu.SemaphoreType.DMA((n,))],
    )(*g4)


def _rs_add_sibling(g4, got, wire_dtypes):
    n = len(g4)
    narrow = [a for a in range(n) if wire_dtypes[a] != F32]

    def body(c_ref, *refs):
        outs = refs[2 * n:3 * n]
        wires = dict(zip(narrow, refs[3 * n:]))
        for a, (g_ref, r_ref, o_ref) in enumerate(zip(refs[:n], refs[n:2 * n], outs)):
            s = g_ref[...] + r_ref[...]
            o_ref[...] = s
            if a in wires:
                wires[a][...] = s.astype(wires[a].dtype)

    blk = lambda r: (1, r.shape[1], r.shape[2])
    plain = lambda r: pl.BlockSpec(blk(r), lambda j, c_ref: (j, 0, 0))
    grid_spec = pltpu.PrefetchScalarGridSpec(
        num_scalar_prefetch=1, grid=(N_CHIPS,),
        in_specs=[pl.BlockSpec(blk(r), lambda j, c_ref: (j, c_ref[0], 0)) for r in got] + [plain(r) for r in got],
        out_specs=[plain(r) for r in got] + [plain(got[a]) for a in narrow])
    res = pl.pallas_call(
        body, name="rs_add_sibling", grid_spec=grid_spec,
        out_shape=[jax.ShapeDtypeStruct(r.shape, F32) for r in got]
        + [jax.ShapeDtypeStruct(got[a].shape, wire_dtypes[a]) for a in narrow],
        compiler_params=_cparams(("parallel",)),
    )(lax.axis_index("c").astype(jnp.int32).reshape(1), *g4, *got)
    chipsum = list(res[:n])
    wire = list(chipsum)
    for a, w in zip(narrow, res[n:]):
        wire[a] = w
    return chipsum, wire


RELATION_XOR = (2, 1, 3)


def _rs_add_chips(chipsum, parts):
    n = len(parts)

    def body(me_ref, *refs):
        me = me_ref[0]
        for s_ref, p_ref, o_ref in zip(refs[:n], refs[n:2 * n], refs[2 * n:]):
            own = s_ref[0]
            total = None
            for k in range(N_CHIPS):
                theirs = p_ref[jnp.maximum(jnp.bitwise_xor(me, k) - 1, 0)].astype(F32)
                term = jnp.where(me == k, own, theirs)
                total = term if total is None else total + term
            o_ref[...] = total

    grid_spec = pltpu.PrefetchScalarGridSpec(
        num_scalar_prefetch=1, grid=(2,),
        in_specs=[pl.BlockSpec((1, p.shape[1] // 2, p.shape[2]), lambda i, me_ref: (me_ref[0], i, 0)) for p in parts]
        + [pl.BlockSpec((3, p.shape[1] // 2, p.shape[2]), lambda i, me_ref: (0, i, 0)) for p in parts],
        out_specs=[pl.BlockSpec((p.shape[1] // 2, p.shape[2]), lambda i, me_ref: (i, 0)) for p in parts])
    me = (2 * lax.axis_index("x") + lax.axis_index("y")).astype(jnp.int32).reshape(1)
    return pl.pallas_call(
        body, name="rs_add_chips", grid_spec=grid_spec,
        out_shape=[jax.ShapeDtypeStruct(p.shape[1:], F32) for p in parts],
        compiler_params=_cparams(("parallel",)),
    )(me, *chipsum, *parts)


def _rs_swap_halves(halves):
    n = len(halves)

    def body(*refs):
        h_refs, out_refs = refs[:n], refs[n:2 * n]
        send_sems, recv_sems = refs[2 * n:]
        x, y, c = _place()
        copies = [pltpu.make_async_remote_copy(src_ref=h_ref, dst_ref=out_ref, send_sem=send_sems.at[a], recv_sem=recv_sems.at[a],
                                               device_id=(x, y, 1 - c), device_id_type=MESH_ID)
                  for a, (h_ref, out_ref) in enumerate(zip(h_refs, out_refs))]
        for cp in copies:
            cp.start()
        for cp in copies:
            cp.wait()

    return pl.pallas_call(
        body, name="rs_swap_halves", in_specs=[HBM_SPEC] * n, out_specs=[HBM_SPEC] * n,
        out_shape=[jax.ShapeDtypeStruct(h.shape, F32) for h in halves],
        scratch_shapes=[pltpu.SemaphoreType.DMA((n,)), pltpu.SemaphoreType.DMA((n,))],
    )(*halves)


def _adamw_list(ws, g_mine, g_theirs, ms, vs):
    n = len(ws)

    def body(c_ref, *refs):
        w_refs, gm_refs, gt_refs, m_refs, v_refs = (refs[k * n:(k + 1) * n] for k in range(5))
        g_refs, d_refs, nm_refs, nv_refs = (refs[k * n:(k + 1) * n] for k in range(5, 9))
        mine = (pl.program_id(0) // ADAM_STEPS_PER_HALF) == c_ref[0]
        for a in range(n):
            gv = jnp.where(mine, gm_refs[a][...], gt_refs[a][...])
            g_refs[a][...] = gv
            m_new = ADAM_B1 * m_refs[a][...] + (1.0 - ADAM_B1) * gv
            v_new = ADAM_B2 * v_refs[a][...] + (1.0 - ADAM_B2) * (gv * gv)
            m_hat = m_new / (1.0 - ADAM_B1 ** ADAM_STEP)
            v_hat = v_new / (1.0 - ADAM_B2 ** ADAM_STEP)
            d_refs[a][...] = -ADAM_LR * (m_hat / (jnp.sqrt(v_hat) + ADAM_EPS) + ADAM_WD * w_refs[a][...])
            nm_refs[a][...] = m_new
            nv_refs[a][...] = v_new

    steps = 2 * ADAM_STEPS_PER_HALF
    whole = [pl.BlockSpec((w.shape[0] // steps, w.shape[1]), lambda i, c_ref: (i, 0)) for w in ws]
    half = [pl.BlockSpec((w.shape[0] // steps, w.shape[1]), lambda i, c_ref: (i % ADAM_STEPS_PER_HALF, 0)) for w in ws]
    shapes = [jax.ShapeDtypeStruct(w.shape, F32) for w in ws]
    grid_spec = pltpu.PrefetchScalarGridSpec(num_scalar_prefetch=1, grid=(steps,),
                                             in_specs=whole + half + half + whole + whole, out_specs=whole * 4)
    res = pl.pallas_call(
        body, name="adamw", grid_spec=grid_spec, out_shape=shapes * 4,
        compiler_params=_cparams(("parallel",)),
    )(lax.axis_index("c").astype(jnp.int32).reshape(1), *ws, *g_mine, *g_theirs, *ms, *vs)
    return res[:n], res[n:2 * n], res[2 * n:3 * n], res[3 * n:]


WEIGHT_NAMES = ("w_in", "w_mem_kv", "q_a_gain", "w_q_b", "kv_a_gain", "w_kv_b", "w_branch_mla", "w_branch_sb",
                "w_branch_mem", "w_merge_gate", "b_merge_gate", "w_out", "ln_gain", "ln_bias")
SMALL_NAMES = tuple(n for n, _ in SMALL_SIZES)


def kernel(x, mem, w_in, w_mem_kv, q_a_gain, w_q_b, kv_a_gain, w_kv_b, w_branch_mla, w_branch_sb, w_branch_mem, w_merge_gate, b_merge_gate, w_out, ln_gain, ln_bias, loss_target, m_w_in, m_w_mem_kv, m_q_a_gain, m_w_q_b, m_kv_a_gain, m_w_kv_b, m_w_branch_mla, m_w_branch_sb, m_w_branch_mem, m_w_merge_gate, m_b_merge_gate, m_w_out, m_ln_gain, m_ln_bias, v_w_in, v_w_mem_kv, v_q_a_gain, v_w_q_b, v_kv_a_gain, v_w_kv_b, v_w_branch_mla, v_w_branch_sb, v_w_branch_mem, v_w_merge_gate, v_b_merge_gate, v_w_out, v_ln_gain, v_ln_bias):
    weights = dict(zip(WEIGHT_NAMES, (w_in, w_mem_kv, q_a_gain, w_q_b, kv_a_gain, w_kv_b, w_branch_mla, w_branch_sb,
                                      w_branch_mem, w_merge_gate, b_merge_gate, w_out, ln_gain, ln_bias)))
    mom1 = dict(zip(WEIGHT_NAMES, (m_w_in, m_w_mem_kv, m_q_a_gain, m_w_q_b, m_kv_a_gain, m_w_kv_b, m_w_branch_mla,
                                   m_w_branch_sb, m_w_branch_mem, m_w_merge_gate, m_b_merge_gate, m_w_out, m_ln_gain,
                                   m_ln_bias)))
    mom2 = dict(zip(WEIGHT_NAMES, (v_w_in, v_w_mem_kv, v_q_a_gain, v_w_q_b, v_kv_a_gain, v_w_kv_b, v_w_branch_mla,
                                   v_w_branch_sb, v_w_branch_mem, v_w_merge_gate, v_b_merge_gate, v_w_out, v_ln_gain,
                                   v_ln_bias)))
    def as_list(d):
        return [d[n][0] for n in BIG_NAMES] + [_pack_small({n: d[n] for n in SMALL_NAMES})]

    w_list, m_list, v_list = as_list(weights), as_list(mom1), as_list(mom2)

    gathered = _gather_shards([weights[n][0] for n in LATE_NAMES])
    first_w = {n: _join_chips(n, g) for n, g in zip(LATE_NAMES, gathered)}
    rest_shards = _cast_bf16_list([weights[n][0] for n in EARLY_NAMES])
    small = {n: weights[n] for n in SMALL_NAMES}

    seq = x.shape[1]
    loss, grad_x, late_mine, _, early_mine = _local_step(
        x[0], mem[0], loss_target[0], first_w, small, tq=min(1024, seq), tq_sb_bwd=512, tk=256, tk_mla=512, t_row=256, t_mm=512,
        t_wg=min(2048, seq), rest_shards=rest_shards)
    by_name = dict(zip(EARLY_NAMES + LATE_NAMES + ("small",), list(early_mine) + list(late_mine)))
    mine = [by_name[n] for n in BIG_NAMES + ("small",)]
    theirs = _rs_swap_halves(mine)
    g_list, d_list, nm_list, nv_list = _adamw_list(w_list, mine, theirs, m_list, v_list)

    loss = lax.psum(loss, ("x", "y", "c"))
    outs = [loss, grad_x[None]]
    for arrays in (g_list, d_list, nm_list, nv_list):
        big = dict(zip(BIG_NAMES, arrays[:-1]))
        sm = _unpack_small(arrays[-1])
        outs.extend(big[n][None] if n in big else sm[n] for n in WEIGHT_NAMES)
    return tuple(outs)
```

```python
import functools
import math

import numpy as np
import jax
import jax.numpy as jnp
from jax import lax
from jax.experimental import pallas as pl
from jax.experimental.pallas import tpu as pltpu

F32 = jnp.float32
BF16 = jnp.bfloat16
MESH_ID = pl.DeviceIdType.MESH

D_MODEL = 1024
MEM_LEN = 256
MLA_HEADS = 8
MLA_NOPE = 64
MLA_ROPE = 32
MLA_V = 64
MLA_Q_LORA = 256
MLA_KV_LORA = 128
SB_HEADS = 8
SB_HEAD_DIM = 64
MEM_HEADS = 4
MEM_HEAD_DIM = 128
ROPE_BASE = 10000.0
RMS_EPS = 1e-6
LN_EPS = 1e-5
DEEPNORM_ALPHA = 2.0 ** 0.25
MLA_SCALE = 1.0 / math.sqrt(MLA_NOPE + MLA_ROPE)
SB_SCALE = 1.0 / math.sqrt(SB_HEAD_DIM)
MEM_SCALE = 1.0 / math.sqrt(MEM_HEAD_DIM)

ADAM_LR = 0.001
ADAM_B1 = 0.9
ADAM_B2 = 0.999
ADAM_EPS = 1e-08
ADAM_WD = 0.01
ADAM_STEP = 10

LANES = 128
HALF = 64
N_CHIPS = 4
PACK_COLS = 1024
VMEM_LIMIT = 56 * 1024 * 1024

IN_WIDTH_P = 4096
BLK_LAT, BLK_GATE_A, BLK_QB, BLK_KB, BLK_VB, BLK_GATE_B, BLK_QM, BLK_GATE_M = range(8)
N_MERGE = 3 * D_MODEL

BIG_NAMES = ("w_in", "w_mem_kv", "w_q_b", "w_kv_b", "w_branch_mla", "w_branch_sb", "w_branch_mem", "w_merge_gate", "w_out")
COL_SHARDED = ("w_in", "w_q_b", "w_kv_b", "w_branch_mla", "w_branch_sb", "w_branch_mem", "w_merge_gate")
SMALL_SIZES = (("q_a_gain", 256), ("kv_a_gain", 128), ("b_merge_gate", 3072), ("ln_gain", 1024), ("ln_bias", 1024))
SMALL_TOTAL = sum(s for _, s in SMALL_SIZES)


def _cparams(sem=None):
    return pltpu.CompilerParams(dimension_semantics=sem, vmem_limit_bytes=VMEM_LIMIT)


def _dot(a, b):
    return jnp.dot(a, b, preferred_element_type=F32)


def _dot_nt(a, b):
    return lax.dot_general(a, b, (((1,), (1,)), ((), ())), preferred_element_type=F32)


def _dot_tn(a, b):
    return lax.dot_general(a, b, (((0,), (0,)), ((), ())), preferred_element_type=F32)


def _bf(x):
    return x.astype(BF16)


def _sigmoid(x):
    return 1.0 / (1.0 + jnp.exp(-x))


def _matmul(a, b, *, mode, tm, tn, tk, out_dtypes, name, add=None, add_scale=1.0, by_column_block=False):
    if mode == "nn":
        (m, k), n = a.shape, b.shape[1]
        a_spec = pl.BlockSpec((tm, tk), lambda i, j, kk: (i, kk))
        b_spec = pl.BlockSpec((tk, tn), lambda i, j, kk: (kk, j))
        dot = _dot
    elif mode == "nt":
        (m, k), n = a.shape, b.shape[0]
        a_spec = pl.BlockSpec((tm, tk), lambda i, j, kk: (i, kk))
        b_spec = pl.BlockSpec((tn, tk), lambda i, j, kk: (j, kk))
        dot = _dot_nt
    else:
        (k, m), n = a.shape, b.shape[1]
        a_spec = pl.BlockSpec((tk, tm), lambda i, j, kk: (kk, i))
        b_spec = pl.BlockSpec((tk, tn), lambda i, j, kk: (kk, j))
        dot = _dot_tn
    assert m % tm == 0 and n % tn == 0 and k % tk == 0, (name, m, n, k)
    nk = k // tk
    n_out = len(out_dtypes)
    has_add = add is not None

    def body(*refs):
        a_ref, b_ref = refs[0], refs[1]
        add_ref = refs[2] if has_add else None
        outs = refs[2 + has_add: 2 + has_add + n_out]
        acc = refs[-1]
        kk = pl.program_id(2)

        @pl.when(kk == 0)
        def _():
            acc[...] = jnp.zeros_like(acc)

        acc[...] += dot(_bf(a_ref[...]), _bf(b_ref[...]))

        @pl.when(kk == nk - 1)
        def _():
            r = acc[...]
            if has_add:
                r = r + add_scale * add_ref[...]
            for o in outs:
                o[...] = r.astype(o.dtype)

    in_specs = [a_spec, b_spec]
    args = [a, b]
    if has_add:
        in_specs.append(pl.BlockSpec((tm, tn), lambda i, j, kk: (i, j)))
        args.append(add)
    if by_column_block:
        out_spec = pl.BlockSpec((None, tm, tn), lambda i, j, kk: (j, i, 0))
        out_dims = (n // tn, m, tn)
    else:
        out_spec = pl.BlockSpec((tm, tn), lambda i, j, kk: (i, j))
        out_dims = (m, n)
    res = pl.pallas_call(
        body, name=name, grid=(m // tm, n // tn, nk),
        in_specs=in_specs, out_specs=[out_spec] * n_out,
        out_shape=[jax.ShapeDtypeStruct(out_dims, dt) for dt in out_dtypes],
        scratch_shapes=[pltpu.VMEM((tm, tn), F32)],
        compiler_params=_cparams(("parallel", "parallel", "arbitrary")),
    )(*args)
    return res


def _rope_tables(seq):
    half = MLA_ROPE // 2
    freqs = ROPE_BASE ** (-jnp.arange(half, dtype=F32) / half)
    ang = jnp.arange(seq, dtype=jnp.int32).astype(F32)[:, None] * freqs[None, :]
    cos, sin = jnp.cos(ang), jnp.sin(ang)
    z = lambda w: jnp.zeros((seq, w), F32)
    c_q = jnp.concatenate([jnp.ones((seq, MLA_NOPE), F32), cos, cos, z(32)], axis=1)
    c_k = jnp.concatenate([z(MLA_NOPE), cos, cos, z(32)], axis=1)
    s_lo = jnp.concatenate([z(MLA_NOPE), -sin, z(half), z(32)], axis=1)
    s_hi = jnp.concatenate([z(MLA_NOPE), z(half), sin, z(32)], axis=1)
    return c_q, c_k, s_lo, s_hi


def _rope_fwd(x, c, s_lo, s_hi):
    return x * c + pltpu.roll(x, LANES - 16, 1) * s_lo + pltpu.roll(x, 16, 1) * s_hi


def _rope_bwd(d, c, s_lo, s_hi):
    return d * c - pltpu.roll(d, 16, 1) * s_hi - pltpu.roll(d, LANES - 16, 1) * s_lo


def _rms_fwd(x, g):
    r = lax.rsqrt(jnp.mean(x * x, axis=-1, keepdims=True) + RMS_EPS)
    xn = x * r
    return xn * g, xn, r


def _mla_prep(p32, gq, gkv, wqb, wkvb, tabs, *, t):
    seq = p32.shape[0]

    def body(lat_ref, gq_ref, gkv_ref, wqb_ref, wkvb_ref, cq_ref, ck_ref, slo_ref, shi_ref, q_ref, k_ref, v_ref):
        lat = lat_ref[...]
        slo, shi = slo_ref[...], shi_ref[...]
        nq, _, _ = _rms_fwd(lat[:, 0:MLA_Q_LORA], gq_ref[...])
        qa = _dot(_bf(nq), wqb_ref[...])
        cq = cq_ref[...]
        for h in range(MLA_HEADS):
            blk = qa[:, h * LANES:(h + 1) * LANES]
            q_ref[:, h * LANES:(h + 1) * LANES] = _bf(_rope_fwd(blk, cq, slo, shi))
        nkv, _, _ = _rms_fwd(lat[:, MLA_Q_LORA:MLA_Q_LORA + MLA_KV_LORA], gkv_ref[...])
        kv = _dot(_bf(nkv), wkvb_ref[...])
        kpe = _rope_fwd(lat[:, 384:512], ck_ref[...], slo, shi)
        for h in range(MLA_HEADS):
            k_ref[:, h * LANES:(h + 1) * LANES] = _bf(kv[:, h * LANES:(h + 1) * LANES] + kpe)
        v_ref[...] = _bf(kv[:, MLA_HEADS * LANES:])

    row = lambda w: pl.BlockSpec((t, w), lambda i: (i, 0))
    full = lambda shp: pl.BlockSpec(shp, lambda i: (0, 0))
    return pl.pallas_call(
        body, name="mla_prep", grid=(seq // t,),
        in_specs=[row(512), full((1, MLA_Q_LORA)), full((1, MLA_KV_LORA)), full(wqb.shape), full(wkvb.shape),
                  row(LANES), row(LANES), row(LANES), row(LANES)],
        out_specs=[row(1024), row(1024), row(512)],
        out_shape=[jax.ShapeDtypeStruct((seq, 1024), BF16), jax.ShapeDtypeStruct((seq, 1024), BF16),
                   jax.ShapeDtypeStruct((seq, 512), BF16)],
        compiler_params=_cparams(("parallel",)),
    )(p32, gq, gkv, wqb, wkvb, *tabs)


def _mla_post(p32, dq, dk, dv, gq, gkv, wqb, wkvb, tabs, *, t):
    seq = p32.shape[0]

    def body(lat_ref, dq_ref, dk_ref, dv_ref, gq_ref, gkv_ref, wqb_ref, wkvb_ref, cq_ref, ck_ref, slo_ref, shi_ref,
             dlat_ref, dwqb_ref, dwkvb_ref, dgq_ref, dgkv_ref):
        @pl.when(pl.program_id(0) == 0)
        def _():
            dwqb_ref[...] = jnp.zeros_like(dwqb_ref)
            dwkvb_ref[...] = jnp.zeros_like(dwkvb_ref)
            dgq_ref[...] = jnp.zeros_like(dgq_ref)
            dgkv_ref[...] = jnp.zeros_like(dgkv_ref)

        lat = lat_ref[...]
        slo, shi = slo_ref[...], shi_ref[...]
        cq = cq_ref[...]
        gq_v, gkv_v = gq_ref[...], gkv_ref[...]
        nq, xq, rq = _rms_fwd(lat[:, 0:MLA_Q_LORA], gq_v)
        nkv, xkv, rkv = _rms_fwd(lat[:, MLA_Q_LORA:MLA_Q_LORA + MLA_KV_LORA], gkv_v)

        dqa = jnp.concatenate(
            [_rope_bwd(dq_ref[:, h * LANES:(h + 1) * LANES], cq, slo, shi) for h in range(MLA_HEADS)], axis=1)
        dqa_b = _bf(dqa)
        dwqb_ref[...] += _dot_tn(_bf(nq), dqa_b)
        dnq = _dot_nt(dqa_b, wqb_ref[...])
        dgq_ref[...] += jnp.sum(dnq * xq, axis=0, keepdims=True)
        dxn = dnq * gq_v
        dcq = rq * (dxn - xq * jnp.mean(dxn * xq, axis=-1, keepdims=True))

        dkf = dk_ref[...]
        dkv_b = _bf(jnp.concatenate([dkf, dv_ref[...]], axis=1))
        dwkvb_ref[...] += _dot_tn(_bf(nkv), dkv_b)
        dnkv = _dot_nt(dkv_b, wkvb_ref[...])
        dgkv_ref[...] += jnp.sum(dnkv * xkv, axis=0, keepdims=True)
        dxn = dnkv * gkv_v
        dckv = rkv * (dxn - xkv * jnp.mean(dxn * xkv, axis=-1, keepdims=True))

        dkpe = dkf[:, 0:LANES]
        for h in range(1, MLA_HEADS):
            dkpe = dkpe + dkf[:, h * LANES:(h + 1) * LANES]
        dkr = _rope_bwd(dkpe, ck_ref[...], slo, shi)
        dlat_ref[...] = _bf(jnp.concatenate([dcq, dckv, dkr], axis=1))

    row = lambda w: pl.BlockSpec((t, w), lambda i: (i, 0))
    full = lambda shp: pl.BlockSpec(shp, lambda i: (0, 0))
    return pl.pallas_call(
        body, name="mla_post", grid=(seq // t,),
        in_specs=[row(512), row(1024), row(1024), row(512), full((1, MLA_Q_LORA)), full((1, MLA_KV_LORA)),
                  full(wqb.shape), full(wkvb.shape), row(LANES), row(LANES), row(LANES), row(LANES)],
        out_specs=[row(512), full(wqb.shape), full(wkvb.shape), full((1, MLA_Q_LORA)), full((1, MLA_KV_LORA))],
        out_shape=[jax.ShapeDtypeStruct((seq, 512), BF16), jax.ShapeDtypeStruct(wqb.shape, F32),
                   jax.ShapeDtypeStruct(wkvb.shape, F32), jax.ShapeDtypeStruct((1, MLA_Q_LORA), F32),
                   jax.ShapeDtypeStruct((1, MLA_KV_LORA), F32)],
        compiler_params=_cparams(("arbitrary",)),
    )(p32, dq, dk, dv, gq, gkv, wqb, wkvb, *tabs)


def _split_bf16(x):
    hi = _bf(x)
    return hi, _bf(x - hi.astype(F32))


def _tri_sum(x, u):
    hi, lo = _split_bf16(x)
    return _dot(hi, u) + _dot(lo, u)


def _softplus(z):
    return jnp.maximum(z, 0.0) + jnp.log(1.0 + jnp.exp(-jnp.abs(z)))


def _head_queries(q, left):
    zero = jnp.zeros_like(q)
    return jnp.where(left, q, zero) * SB_SCALE, jnp.where(left, zero, q) * SB_SCALE


ROW_GROUP = 128
SB_BWD_CHAINS_IN_FLIGHT = 8
ANY_HBM = pl.BlockSpec(memory_space=pltpu.HBM)


class _Exchange:
    def __init__(self, send, landing):
        self.send, self.landing = send, landing

    def start(self):
        self.send.start()

    def wait(self):
        self.landing.wait_recv()
        self.send.wait_send()


class _Rider:
    def __init__(self, operands, out_shapes, sem_shapes, copies):
        self.operands, self.out_shapes, self.sem_shapes, self.copies = list(operands), list(out_shapes), list(sem_shapes), copies


def _call_with_rider(body, rider, *, name, grid, in_specs, out_specs, out_shape, args, semantics, scratch=()):
    scratch = list(scratch)
    if rider is None:
        return pl.pallas_call(body, name=name, grid=grid, in_specs=in_specs, out_specs=out_specs, out_shape=out_shape,
                              scratch_shapes=scratch, compiler_params=_cparams(semantics))(*args)
    n_in, n_out, n_rin, n_rout = len(in_specs), len(out_specs), len(rider.operands), len(rider.out_shapes)

    def full_body(*refs):
        ins, r_ins = refs[:n_in], refs[n_in:n_in + n_rin]
        outs = refs[n_in + n_rin:n_in + n_rin + n_out]
        r_outs = refs[n_in + n_rin + n_out:n_in + n_rin + n_out + n_rout]
        rest = refs[n_in + n_rin + n_out + n_rout:]
        own_scratch, sems = rest[:len(scratch)], rest[len(scratch):]
        first, last = None, None
        for axis, size in enumerate(grid):
            at_start, at_end = pl.program_id(axis) == 0, pl.program_id(axis) == size - 1
            first = at_start if first is None else first & at_start
            last = at_end if last is None else last & at_end

        @pl.when(first)
        def _():
            for cp in rider.copies(r_ins, r_outs, sems):
                cp.start()

        body(*ins, *outs, *own_scratch)

        @pl.when(last)
        def _():
            for cp in rider.copies(r_ins, r_outs, sems):
                cp.wait()

    return pl.pallas_call(
        full_body, name=name, grid=grid, in_specs=list(in_specs) + [ANY_HBM] * n_rin,
        out_specs=list(out_specs) + [ANY_HBM] * n_rout, out_shape=list(out_shape) + rider.out_shapes,
        scratch_shapes=scratch + rider.sem_shapes, compiler_params=_cparams(("arbitrary",) * len(grid)),
    )(*args, *rider.operands)


def _chains(tq):
    return [(h, g) for g in range(tq // ROW_GROUP) for h in range(2)]


def _chain_pattern(g, m, tk, strict):
    r_lo, r_hi = g * ROW_GROUP, (g + 1) * ROW_GROUP - 1
    c_lo, c_hi = m * tk, (m + 1) * tk - 1
    if (c_lo >= r_hi) if strict else (c_lo > r_hi):
        return None
    if (c_hi < r_lo) if strict else (c_hi <= r_lo):
        return True
    rr = lax.broadcasted_iota(jnp.int32, (ROW_GROUP, tk), 0) + r_lo
    cc = lax.broadcasted_iota(jnp.int32, (ROW_GROUP, tk), 1) + c_lo
    return (cc < rr) if strict else (cc <= rr)


def _masked(x, pat, fill=0.0):
    return x if pat is True else jnp.where(pat, x, fill)


def _rows(g):
    return slice(g * ROW_GROUP, (g + 1) * ROW_GROUP)


def _tri_matrix(tk, cmp):
    rr = lax.broadcasted_iota(jnp.int32, (tk, tk), 0)
    cc = lax.broadcasted_iota(jnp.int32, (tk, tk), 1)
    return cmp(rr, cc).astype(BF16)


def _mla_attn_fwd(qp, kp, vp, *, tq, tk, tk_diag, rider=None):
    seq = qp.shape[0]
    neg = float(np.finfo(np.float32).min)
    chains = _chains(tq)

    def body(q_ref, k_ref, v_ref, o_ref, lse_ref):
        i = pl.program_id(1)
        left = lax.broadcasted_iota(jnp.int32, (tq, LANES), 1) < HALF
        qs = [q_ref[_rows(g), h * LANES:(h + 1) * LANES] for h, g in chains]

        def block(start, carry, m, tk):
            v = v_ref[pl.ds(start, tk), :]
            pats = [True if m is None else _chain_pattern(g, m, tk, False) for _, g in chains]
            live = [n for n, p in enumerate(pats) if p is not None]
            ss = {n: _dot_nt(qs[n], k_ref[pl.ds(start, tk), chains[n][0] * LANES:(chains[n][0] + 1) * LANES]) for n in live}
            new = list(carry)
            for n in live:
                m_old, l_old, acc = carry[n]
                s = _masked(ss[n] * MLA_SCALE, pats[n], neg)
                m_new = jnp.maximum(m_old, jnp.max(s, axis=-1, keepdims=True))
                a = jnp.exp(m_old - m_new)
                p = jnp.exp(s - m_new)
                new[n] = (m_new, a * l_old + jnp.sum(p, axis=-1, keepdims=True), a * acc + _dot(_bf(p), v))
            return tuple(new)

        init = (jnp.full((ROW_GROUP, 1), -1e30, F32), jnp.zeros((ROW_GROUP, 1), F32), jnp.zeros((ROW_GROUP, LANES), F32))
        def two_blocks(j, c):
            c = block(pl.multiple_of(2 * j * tk, tk), c, None, tk)
            return block(pl.multiple_of((2 * j + 1) * tk, tk), c, None, tk)

        carry = lax.fori_loop(0, i * (tq // tk) // 2, two_blocks, (init,) * len(chains))
        for m in range(tq // tk_diag):
            carry = block(pl.multiple_of(i * tq + m * tk_diag, tk_diag), carry, m, tk_diag)
        per_head = []
        for h in range(2):
            mine = [carry[n] for n, (ch, _) in enumerate(chains) if ch == h]
            per_head.append((jnp.concatenate([acc / l for _, l, acc in mine], axis=0),
                             jnp.concatenate([mm + jnp.log(l) for mm, l, _ in mine], axis=0)))
        o_ref[...] = jnp.where(left, per_head[0][0], per_head[1][0])
        lse_ref[...] = jnp.where(left, per_head[0][1], per_head[1][1])

    return _call_with_rider(
        body, rider, name="mla_fwd", grid=(MLA_HEADS // 2, seq // tq),
        in_specs=[pl.BlockSpec((tq, 2 * LANES), lambda p, i: (i, p)), pl.BlockSpec((seq, 2 * LANES), lambda p, i: (0, p)),
                  pl.BlockSpec((seq, LANES), lambda p, i: (0, p))],
        out_specs=[pl.BlockSpec((tq, LANES), lambda p, i: (i, p)), pl.BlockSpec((tq, LANES), lambda p, i: (i, p))],
        out_shape=[jax.ShapeDtypeStruct((seq, 512), F32), jax.ShapeDtypeStruct((seq, 512), F32)],
        args=(qp, kp, vp), semantics=("parallel", "parallel"))


def _mla_attn_bwd(qp, kp, vp, o, lse, do, *, tq, tk, tk_diag, rider=None):
    seq = qp.shape[0]
    chains = _chains(tq)

    def body(q_ref, k_ref, v_ref, o_ref, lse_ref, do_ref, dq_ref, dk_ref, dv_ref, qt_ref, dot_ref):
        i = pl.program_id(1)

        @pl.when(i == 0)
        def _():
            dk_ref[...] = jnp.zeros_like(dk_ref)
            dv_ref[...] = jnp.zeros_like(dv_ref)

        left = lax.broadcasted_iota(jnp.int32, (tq, LANES), 1) < HALF
        do_f = do_ref[...]
        prod = do_f * o_ref[...]
        lse_v = lse_ref[...]
        do_heads = (_bf(jnp.where(left, do_f, 0.0)), _bf(jnp.where(left, 0.0, do_f)))
        delta_heads = (jnp.sum(jnp.where(left, prod, 0.0), axis=-1, keepdims=True),
                       jnp.sum(jnp.where(left, 0.0, prod), axis=-1, keepdims=True))
        qs = [q_ref[_rows(g), h * LANES:(h + 1) * LANES] for h, g in chains]
        dos = [do_heads[h][_rows(g)] for h, g in chains]
        deltas = [delta_heads[h][_rows(g)] for h, g in chains]
        lses = [lse_v[_rows(g), h * HALF:h * HALF + 1] for h, g in chains]
        for h in range(2):
            qt_ref[h] = q_ref[:, h * LANES:(h + 1) * LANES].T
            dot_ref[h] = do_heads[h].T
        q_t = [qt_ref.at[h] for h in range(2)]
        do_t = [dot_ref.at[h] for h in range(2)]

        def block(start, carry, m, tk):
            v = v_ref[pl.ds(start, tk), :]
            pats = [True if m is None else _chain_pattern(g, m, tk, False) for _, g in chains]
            live = [n for n, p in enumerate(pats) if p is not None]
            ks = [k_ref[pl.ds(start, tk), h * LANES:(h + 1) * LANES] for h in range(2)]
            ss = {n: _dot_nt(qs[n], ks[chains[n][0]]) for n in live}
            dps = {n: _dot_nt(dos[n], v) for n in live}
            new = list(carry)
            ps, dss = {}, {}
            for n in live:
                p = _masked(jnp.exp(ss[n] * MLA_SCALE - lses[n]), pats[n])
                ps[n] = _bf(p)
                dss[n] = _bf(p * (dps[n] - deltas[n]) * MLA_SCALE)
                new[n] = carry[n] + _dot(dss[n], ks[chains[n][0]])
            dv_t, dk_t = None, []
            for h in range(2):
                mine = [n for n in live if chains[n][0] == h]
                first_row = chains[mine[0]][1] * ROW_GROUP
                ds_cat = jnp.concatenate([dss[n] for n in mine], axis=0)
                p_cat = jnp.concatenate([ps[n] for n in mine], axis=0)
                if first_row == 0:
                    q_rows_t, do_rows_t = q_t[h][...], do_t[h][...]
                else:
                    q_rows_t = q_ref[first_row:, h * LANES:(h + 1) * LANES].T
                    do_rows_t = do_heads[h][first_row:].T
                dk_t.append(_dot(q_rows_t, ds_cat))
                term = _dot(do_rows_t, p_cat)
                dv_t = term if dv_t is None else dv_t + term
            back = jnp.concatenate(dk_t + [dv_t], axis=0).T
            dk_ref[pl.ds(start, tk), :] += back[:, :2 * LANES]
            dv_ref[pl.ds(start, tk), :] += back[:, 2 * LANES:]
            return tuple(new)

        zero = jnp.zeros((ROW_GROUP, LANES), F32)
        carry = lax.fori_loop(0, i * (tq // tk), lambda j, c: block(pl.multiple_of(j * tk, tk), c, None, tk),
                              (zero,) * len(chains))
        for m in range(tq // tk_diag):
            carry = block(pl.multiple_of(i * tq + m * tk_diag, tk_diag), carry, m, tk_diag)
        for n, (h, g) in enumerate(chains):
            dq_ref[_rows(g), h * LANES:(h + 1) * LANES] = carry[n]

    two_t = pl.BlockSpec((tq, 2 * LANES), lambda p, i: (i, p))
    two_s = pl.BlockSpec((seq, 2 * LANES), lambda p, i: (0, p))
    pair_t = pl.BlockSpec((tq, LANES), lambda p, i: (i, p))
    pair_s = pl.BlockSpec((seq, LANES), lambda p, i: (0, p))
    return _call_with_rider(
        body, rider, name="mla_bwd", grid=(MLA_HEADS // 2, seq // tq),
        in_specs=[two_t, two_s, pair_s, pair_t, pair_t, pair_t],
        out_specs=[two_t, two_s, pair_s],
        out_shape=[jax.ShapeDtypeStruct((seq, 1024), F32), jax.ShapeDtypeStruct((seq, 1024), F32),
                   jax.ShapeDtypeStruct((seq, 512), F32)],
        args=(qp, kp, vp, o, lse, do), semantics=("parallel", "arbitrary"),
        scratch=[pltpu.VMEM((2, LANES, tq), BF16), pltpu.VMEM((2, LANES, tq), BF16)])


def _sb_attn_fwd(pbf, *, tq, tk):
    seq = pbf.shape[0]
    nd = tq // tk
    qb, kb, vb = BLK_QB * 4, BLK_KB * 4, BLK_VB * 4
    chains = _chains(tq)

    def body(q_ref, k_ref, v_ref, o_ref, tot_ref):
        i = pl.program_id(1)
        u_later = _tri_matrix(tk, lambda r, c: r > c)
        left = lax.broadcasted_iota(jnp.int32, (tq, LANES), 1) < HALF
        q_heads = _head_queries(q_ref[...], left)
        qs = [q_heads[h][_rows(g)] for h, g in chains]

        def block(j, carry, m):
            start = pl.multiple_of(j * tk, tk)
            k = k_ref[pl.ds(start, tk), :]
            v = v_ref[pl.ds(start, tk), :]
            pats = [True if m is None else _chain_pattern(g, m, tk, True) for _, g in chains]
            live = [n for n, p in enumerate(pats) if p is not None]
            zs = {n: _dot_nt(qs[n], k) for n in live}
            raws = {n: _softplus(zs[n]) for n in live}
            sps = {n: _masked(raws[n], pats[n]) for n in live}
            laters = {n: _tri_sum(sps[n], u_later) for n in live}
            new = list(carry)
            for n in live:
                c, acc = carry[n]
                a = _masked(jnp.exp(zs[n] - raws[n] - laters[n] - c), pats[n])
                new[n] = (c + laters[n][:, 0:1] + sps[n][:, 0:1], acc + _dot(_bf(a), v))
            return tuple(new)

        init = (jnp.zeros((ROW_GROUP, 1), F32), jnp.zeros((ROW_GROUP, LANES), F32))
        carry = (init,) * len(chains)
        for m in reversed(range(nd)):
            carry = block(i * nd + m, carry, m)
        per_trip = 4 if nd % 4 == 0 else 2

        def trip(jj, cr):
            for u in range(per_trip):
                cr = block(i * nd - 1 - (per_trip * jj + u), cr, None)
            return cr

        carry = lax.fori_loop(0, i * nd // per_trip, trip, carry)
        per_head = []
        for h in range(2):
            mine = [carry[n] for n, (ch, _) in enumerate(chains) if ch == h]
            per_head.append((jnp.concatenate([acc for _, acc in mine], axis=0), jnp.concatenate([c for c, _ in mine], axis=0)))
        o_ref[...] = jnp.where(left, per_head[0][0], per_head[1][0])
        tot_ref[...] = jnp.where(left, per_head[0][1], per_head[1][1])

    pair_t = pl.BlockSpec((tq, LANES), lambda p, i: (i, p))
    return pl.pallas_call(
        body, name="sb_fwd", grid=(SB_HEADS // 2, seq // tq),
        in_specs=[pl.BlockSpec((tq, LANES), lambda p, i: (i, qb + p)), pl.BlockSpec((seq, LANES), lambda p, i: (0, kb + p)),
                  pl.BlockSpec((seq, LANES), lambda p, i: (0, vb + p))],
        out_specs=[pair_t, pair_t],
        out_shape=[jax.ShapeDtypeStruct((seq, 512), F32), jax.ShapeDtypeStruct((seq, 512), F32)],
        compiler_params=_cparams(("parallel", "parallel")),
    )(pbf, pbf, pbf)


def _sb_attn_bwd(pbf, tot, do, *, tq, tk, rider=None):
    seq = pbf.shape[0]
    nd = tq // tk
    qb, kb, vb = BLK_QB * 4, BLK_KB * 4, BLK_VB * 4
    chains = _chains(tq)
    group = SB_BWD_CHAINS_IN_FLIGHT

    def body(q_ref, k_ref, v_ref, tot_ref, do_ref, dq_ref, dk_ref, dv_ref, qt_ref, dot_ref):
        i = pl.program_id(1)

        @pl.when(i == 0)
        def _():
            dk_ref[...] = jnp.zeros_like(dk_ref)
            dv_ref[...] = jnp.zeros_like(dv_ref)

        u_upto = _tri_matrix(tk, lambda r, c: r <= c)
        u_below = _tri_matrix(tk, lambda r, c: r < c)
        left = lax.broadcasted_iota(jnp.int32, (tq, LANES), 1) < HALF
        q_heads = _head_queries(q_ref[...], left)
        do_f = do_ref[...]
        do_heads = (_bf(jnp.where(left, do_f, 0.0)), _bf(jnp.where(left, 0.0, do_f)))
        tot_v = tot_ref[...]
        qs = [q_heads[h][_rows(g)] for h, g in chains]
        dos = [do_heads[h][_rows(g)] for h, g in chains]
        totals = [tot_v[_rows(g), h * HALF:h * HALF + 1] for h, g in chains]
        qt_ref[...] = jnp.concatenate(qs, axis=0).T
        dot_ref[...] = jnp.concatenate(dos, axis=0).T

        def block(j, carry, m):
            start = pl.multiple_of(j * tk, tk)
            k = k_ref[pl.ds(start, tk), :]
            v = v_ref[pl.ds(start, tk), :]
            pats = [True if m is None else _chain_pattern(g, m, tk, True) for _, g in chains]
            all_live = [n for n, p in enumerate(pats) if p is not None]
            new = list(carry)
            for g0 in range(0, len(all_live), group):
                live = all_live[g0:g0 + group]
                zs = {n: _dot_nt(qs[n], k) for n in live}
                das = {n: _dot_nt(dos[n], v) for n in live}
                raws = {n: _softplus(zs[n]) for n in live}
                sps = {n: _masked(raws[n], pats[n]) for n in live}
                uptos = {n: _tri_sum(sps[n], u_upto) for n in live}
                lbs, a_s, gs = {}, {}, {}
                for n in live:
                    lbs[n] = zs[n] - raws[n]
                    a = _masked(jnp.exp(lbs[n] - (totals[n] - carry[n][0] - uptos[n])), pats[n])
                    a_s[n] = _bf(a)
                    gs[n] = das[n] * a
                belows = {n: _dot(_bf(gs[n]), u_below) for n in live}
                dzs = {}
                for n in live:
                    sp_before, g_before, dq_acc = carry[n]
                    beta = jnp.exp(lbs[n])
                    dz = _masked(gs[n] * (1.0 - beta) - (g_before + belows[n]) * beta, pats[n])
                    dzs[n] = _bf(dz)
                    new[n] = (sp_before + uptos[n][:, tk - 1:tk], g_before + belows[n][:, tk - 1:tk] + gs[n][:, tk - 1:tk],
                              dq_acc + _dot(dzs[n], k))
                dz_cat = jnp.concatenate([dzs[n] for n in live], axis=0)
                a_cat = jnp.concatenate([a_s[n] for n in live], axis=0)
                if len(live) == len(chains):
                    q_rows_t, do_rows_t = qt_ref[...], dot_ref[...]
                else:
                    q_rows_t = jnp.concatenate([qs[n] for n in live], axis=0).T
                    do_rows_t = jnp.concatenate([dos[n] for n in live], axis=0).T
                both = jnp.concatenate([_dot(q_rows_t, dz_cat), _dot(do_rows_t, a_cat)], axis=0).T
                dk_ref[pl.ds(start, tk), :] += both[:, :LANES]
                dv_ref[pl.ds(start, tk), :] += both[:, LANES:]
            return tuple(new)

        zero = jnp.zeros((ROW_GROUP, 1), F32)
        init = (zero, zero, jnp.zeros((ROW_GROUP, LANES), F32))
        carry = lax.fori_loop(0, i * nd // 2, lambda j, cr: block(2 * j + 1, block(2 * j, cr, None), None),
                              (init,) * len(chains))
        for m in range(nd):
            carry = block(i * nd + m, carry, m)
        per_head = [jnp.concatenate([carry[n][2] for n, (ch, _) in enumerate(chains) if ch == h], axis=0) for h in range(2)]
        dq_ref[...] = jnp.where(left, per_head[0], per_head[1]) * SB_SCALE

    pair_t = pl.BlockSpec((tq, LANES), lambda p, i: (i, p))
    pair_s = pl.BlockSpec((seq, LANES), lambda p, i: (0, p))
    return _call_with_rider(
        body, rider, name="sb_bwd", grid=(SB_HEADS // 2, seq // tq),
        in_specs=[pl.BlockSpec((tq, LANES), lambda p, i: (i, qb + p)), pl.BlockSpec((seq, LANES), lambda p, i: (0, kb + p)),
                  pl.BlockSpec((seq, LANES), lambda p, i: (0, vb + p)), pair_t, pair_t],
        out_specs=[pair_t, pair_s, pair_s],
        out_shape=[jax.ShapeDtypeStruct((seq, 512), F32)] * 3,
        args=(pbf, pbf, pbf, tot, do), semantics=("parallel", "arbitrary"),
        scratch=[pltpu.VMEM((LANES, 2 * tq), BF16), pltpu.VMEM((LANES, 2 * tq), BF16)])


def _mem_probs(s):
    e = jnp.exp(s - jnp.max(s, axis=-1, keepdims=True))
    return e / jnp.sum(e, axis=-1, keepdims=True)


def _head_lanes(h):
    return slice(h * LANES, (h + 1) * LANES)


def _mem_fwd(pbf, mkv, *, t):
    seq = pbf.shape[0]

    def body(q_ref, kv_ref, o_ref):
        ss = [_dot_nt(q_ref[:, _head_lanes(h)], kv_ref[:, _head_lanes(h)]) * MEM_SCALE for h in range(MEM_HEADS)]
        ps = [_bf(_mem_probs(s)) for s in ss]
        for h, p in enumerate(ps):
            o_ref[:, _head_lanes(h)] = _dot(p, kv_ref[:, _head_lanes(MEM_HEADS + h)])

    return pl.pallas_call(
        body, name="mem_fwd", grid=(seq // t,),
        in_specs=[pl.BlockSpec((t, 512), lambda i: (i, BLK_QM)), pl.BlockSpec((MEM_LEN, 1024), lambda i: (0, 0))],
        out_specs=pl.BlockSpec((t, 512), lambda i: (i, 0)),
        out_shape=jax.ShapeDtypeStruct((seq, 512), F32),
        compiler_params=_cparams(("parallel",)),
    )(pbf, mkv)


def _mem_bwd(pbf, mkv, do, *, t):
    seq = pbf.shape[0]

    def body(q_ref, kv_ref, do_ref, dq_ref, dkv_ref):
        @pl.when(pl.program_id(0) == 0)
        def _():
            dkv_ref[...] = jnp.zeros_like(dkv_ref)

        heads = range(MEM_HEADS)
        qs = [q_ref[:, _head_lanes(h)] for h in heads]
        ks = [kv_ref[:, _head_lanes(h)] for h in heads]
        dos = [_bf(do_ref[:, _head_lanes(h)]) for h in heads]
        ss = [_dot_nt(qs[h], ks[h]) * MEM_SCALE for h in heads]
        dps = [_dot_nt(dos[h], kv_ref[:, _head_lanes(MEM_HEADS + h)]) for h in heads]
        ps = [_mem_probs(s) for s in ss]
        dss = [_bf(ps[h] * (dps[h] - jnp.sum(dps[h] * ps[h], axis=-1, keepdims=True)) * MEM_SCALE) for h in heads]
        for h in heads:
            dq_ref[:, _head_lanes(h)] = _dot(dss[h], ks[h])
        for h in heads:
            dkv_ref[:, _head_lanes(h)] += _dot_tn(dss[h], qs[h])
            dkv_ref[:, _head_lanes(MEM_HEADS + h)] += _dot_tn(_bf(ps[h]), dos[h])

    return pl.pallas_call(
        body, name="mem_bwd", grid=(seq // t,),
        in_specs=[pl.BlockSpec((t, 512), lambda i: (i, BLK_QM)), pl.BlockSpec((MEM_LEN, 1024), lambda i: (0, 0)),
                  pl.BlockSpec((t, 512), lambda i: (i, 0))],
        out_specs=[pl.BlockSpec((t, 512), lambda i: (i, 0)), pl.BlockSpec((MEM_LEN, 1024), lambda i: (0, 0))],
        out_shape=[jax.ShapeDtypeStruct((seq, 512), F32), jax.ShapeDtypeStruct((MEM_LEN, 1024), F32)],
        compiler_params=_cparams(("arbitrary",)),
    )(pbf, mkv, do)


def _mid(x, tgt, o_a, o_b, o_m, p32, wmg, bmg, wba, wbb, wbm, wout, ln_g, ln_b, *, t):
    seq = x.shape[0]
    inv_d = 1.0 / D_MODEL

    def body(x_ref, t_ref, oa_ref, ob_ref, om_ref, ga_ref, gb_ref, gm_ref, wmg_ref, bmg_ref, wba_ref, wbb_ref,
             wbm_ref, wout_ref, lg_ref, lb_ref,
             du_ref, mrg_ref, dgp_ref, ha_ref, hb_ref, hm_ref, dya_ref, dyb_ref, dym_ref, doa_ref, dob_ref, dom_ref,
             dga_ref, dgb_ref, dgm_ref, dgain_ref, dbias_ref, dbmg_ref, loss_ref):
        @pl.when(pl.program_id(0) == 0)
        def _():
            dgain_ref[...] = jnp.zeros_like(dgain_ref)
            dbias_ref[...] = jnp.zeros_like(dbias_ref)
            dbmg_ref[...] = jnp.zeros_like(dbmg_ref)
            loss_ref[...] = jnp.zeros_like(loss_ref)

        xv = x_ref[...]
        gate = _sigmoid(_dot(_bf(xv), wmg_ref[...]) + bmg_ref[...])

        branches = []
        merged = None
        for b, (o_ref, g_ref, w_ref, h_ref) in enumerate(((oa_ref, ga_ref, wba_ref, ha_ref), (ob_ref, gb_ref, wbb_ref, hb_ref),
                                                         (om_ref, gm_ref, wbm_ref, hm_ref))):
            o, gt = o_ref[...], g_ref[...]
            sg = _sigmoid(gt)
            silu = gt * sg
            h = _bf(o * silu)
            h_ref[...] = h
            y = _dot(h, w_ref[...])
            g_b = gate[:, b * D_MODEL:(b + 1) * D_MODEL]
            term = g_b * y
            merged = term if merged is None else merged + term
            branches.append((o, gt, sg, silu, y, g_b))
        mrg_b = _bf(merged)
        mrg_ref[...] = mrg_b

        u = DEEPNORM_ALPHA * xv + _dot(mrg_b, wout_ref[...])
        mu = jnp.mean(u, axis=-1, keepdims=True)
        uc = u - mu
        rstd = lax.rsqrt(jnp.mean(uc * uc, axis=-1, keepdims=True) + LN_EPS)
        xhat = uc * rstd
        lg = lg_ref[...]
        y_out = xhat * lg + lb_ref[...]
        err = y_out - t_ref[...]
        loss_ref[...] += 0.5 * jnp.sum(jnp.mean(err * err, axis=-1, keepdims=True), axis=0, keepdims=True)
        dy = err * inv_d
        dgain_ref[...] += jnp.sum(dy * xhat, axis=0, keepdims=True)
        dbias_ref[...] += jnp.sum(dy, axis=0, keepdims=True)
        dxh = dy * lg
        du = rstd * (dxh - jnp.mean(dxh, axis=-1, keepdims=True) - xhat * jnp.mean(dxh * xhat, axis=-1, keepdims=True))
        du_ref[...] = du

        dmerged = _dot_nt(_bf(du), wout_ref[...])
        outs = ((dya_ref, doa_ref, dga_ref, wba_ref), (dyb_ref, dob_ref, dgb_ref, wbb_ref), (dym_ref, dom_ref, dgm_ref, wbm_ref))
        dgp = []
        for (o, gt, sg, silu, y, g_b), (dy_ref, do_ref, dg_ref, w_ref) in zip(branches, outs):
            dyb = _bf(dmerged * g_b)
            dy_ref[...] = dyb
            dgp.append(dmerged * y * g_b * (1.0 - g_b))
            dh = _dot_nt(dyb, w_ref[...])
            do_ref[...] = dh * silu
            dg_ref[...] = _bf(dh * o * (sg * (1.0 + gt * (1.0 - sg))))
        dgp = jnp.concatenate(dgp, axis=1)
        dgp_ref[...] = _bf(dgp)
        dbmg_ref[...] += jnp.sum(dgp, axis=0, keepdims=True)

    row = lambda w: pl.BlockSpec((t, w), lambda i: (i, 0))
    pblk = lambda c: pl.BlockSpec((t, 512), lambda i: (i, c))
    full = lambda shp: pl.BlockSpec(shp, lambda i: (0, 0))
    sds = jax.ShapeDtypeStruct
    return pl.pallas_call(
        body, name="mid", grid=(seq // t,),
        in_specs=[row(1024), row(1024), row(512), row(512), row(512), pblk(BLK_GATE_A), pblk(BLK_GATE_B), pblk(BLK_GATE_M),
                  full(wmg.shape), full((1, N_MERGE)), full(wba.shape), full(wbb.shape), full(wbm.shape), full(wout.shape),
                  full((1, D_MODEL)), full((1, D_MODEL))],
        out_specs=[row(1024), row(1024), row(N_MERGE), row(512), row(512), row(512), row(1024), row(1024), row(1024),
                   row(512), row(512), row(512), row(512), row(512), row(512),
                   full((1, D_MODEL)), full((1, D_MODEL)), full((1, N_MERGE)), full((1, 1))],
        out_shape=[sds((seq, 1024), F32), sds((seq, 1024), BF16), sds((seq, N_MERGE), BF16),
                   sds((seq, 512), BF16), sds((seq, 512), BF16), sds((seq, 512), BF16),
                   sds((seq, 1024), BF16), sds((seq, 1024), BF16), sds((seq, 1024), BF16),
                   sds((seq, 512), F32), sds((seq, 512), F32), sds((seq, 512), F32),
                   sds((seq, 512), BF16), sds((seq, 512), BF16), sds((seq, 512), BF16),
                   sds((1, D_MODEL), F32), sds((1, D_MODEL), F32), sds((1, N_MERGE), F32), sds((1, 1), F32)],
        compiler_params=_cparams(("arbitrary",)),
    )(x, tgt, o_a, o_b, o_m, p32, p32, p32, wmg, bmg, wba, wbb, wbm, wout, ln_g, ln_b)


def _primed_weights(w):
    w_in = w["w_in"]
    zc = lambda n: jnp.zeros((D_MODEL, n), w_in.dtype)
    w_in_p = jnp.concatenate([w_in[:, 0:384], zc(64), w_in[:, 384:416], zc(32), w_in[:, 416:]], axis=1)
    wqb = jnp.pad(w["w_q_b"].reshape(MLA_Q_LORA, MLA_HEADS, 96), ((0, 0), (0, 0), (0, 32))).reshape(MLA_Q_LORA, 1024)
    kv3 = w["w_kv_b"].reshape(MLA_KV_LORA, MLA_HEADS, 128)
    wk = jnp.pad(kv3[:, :, :MLA_NOPE], ((0, 0), (0, 0), (0, 64))).reshape(MLA_KV_LORA, 1024)
    wv = kv3[:, :, MLA_NOPE:].reshape(MLA_KV_LORA, 512)
    return w_in_p, wqb, jnp.concatenate([wk, wv], axis=1)


PROJ_BLK = 512


def _grad_x(du, dgpre, wmg, d_proj, w_in_p, *, tm, rider=None):
    seq = du.shape[0]
    n_pieces = len(d_proj)

    def body(du_ref, dg_ref, wmg_ref, *rest):
        piece_refs, win_ref, out_ref = rest[:n_pieces], rest[n_pieces], rest[n_pieces + 1]
        d_p = jnp.concatenate([_bf(p_ref[...]) for p_ref in piece_refs], axis=1)
        out_ref[...] = (DEEPNORM_ALPHA * du_ref[...] + _dot_nt(dg_ref[...], wmg_ref[...])) + _dot_nt(d_p, win_ref[...])

    row = lambda w: pl.BlockSpec((tm, w), lambda i: (i, 0))
    whole = lambda a: pl.BlockSpec(a.shape, lambda i: (0, 0))
    return _call_with_rider(
        body, rider, name="grad_x", grid=(seq // tm,),
        in_specs=[row(D_MODEL), row(N_MERGE), whole(wmg)] + [row(PROJ_BLK) for _ in d_proj] + [whole(w_in_p)],
        out_specs=[row(D_MODEL)], out_shape=[jax.ShapeDtypeStruct((seq, D_MODEL), F32)],
        args=(du, dgpre, wmg, *d_proj, w_in_p), semantics=("parallel",))


def _grad_w_in(x, d_proj, *, tk):
    seq = x.shape[0]
    n_pieces = len(d_proj)
    nk = seq // tk

    def body(x_ref, *rest):
        piece_refs, out_ref, acc = rest[:n_pieces], rest[n_pieces], rest[n_pieces + 1]
        j, kk = pl.program_id(0), pl.program_id(1)

        @pl.when(kk == 0)
        def _():
            acc[...] = jnp.zeros_like(acc)

        xb = _bf(x_ref[...])
        for pair in range(n_pieces // 2):
            @pl.when(j == pair)
            def _(pair=pair):
                both = jnp.concatenate([_bf(piece_refs[2 * pair][...]), _bf(piece_refs[2 * pair + 1][...])], axis=1)
                acc[...] += _dot_tn(xb, both)

        @pl.when(kk == nk - 1)
        def _():
            out_ref[...] = acc[...]

    def piece_spec(s):
        return pl.BlockSpec((tk, PROJ_BLK), lambda j, kk: (jnp.where(j == s // 2, kk, 0), 0))

    return pl.pallas_call(
        body, name="grad_w_in", grid=(n_pieces // 2, nk),
        in_specs=[pl.BlockSpec((tk, D_MODEL), lambda j, kk: (kk, 0))] + [piece_spec(s) for s in range(n_pieces)],
        out_specs=pl.BlockSpec((D_MODEL, 2 * PROJ_BLK), lambda j, kk: (0, j)),
        out_shape=jax.ShapeDtypeStruct((D_MODEL, n_pieces * PROJ_BLK), F32),
        scratch_shapes=[pltpu.VMEM((D_MODEL, 2 * PROJ_BLK), F32)],
        compiler_params=_cparams(("parallel", "arbitrary")),
    )(x, *d_proj)


EARLY_NAMES = ("w_mem_kv", "w_branch_mla", "w_branch_sb", "w_branch_mem", "w_merge_gate", "w_out")
LATE_NAMES = ("w_in", "w_q_b", "w_kv_b")


def _remote(src, dst, send_sem, recv_sem, device):
    return pltpu.make_async_remote_copy(src_ref=src, dst_ref=dst, send_sem=send_sem, recv_sem=recv_sem, device_id=device,
                                        device_id_type=MESH_ID)


def _gather_rider(shards):
    n = len(shards)

    def copies(src_refs, out_refs, sems):
        send_sems, recv_sems, local_sems = sems
        x, y, c = _place()
        me = 2 * x + y
        out = []
        for a, (s, o) in enumerate(zip(src_refs, out_refs)):
            out.append(pltpu.make_async_copy(s, o.at[me], local_sems.at[a]))
            for k, (px, py) in enumerate(_other_chips(x, y)):
                out.append(_Exchange(_remote(s, o.at[me], send_sems.at[k, a], recv_sems.at[k, a], (px, py, c)),
                                     _remote(s, o.at[2 * px + py], send_sems.at[k, a], recv_sems.at[k, a], (px, py, c))))
        return out

    return _Rider(shards, [jax.ShapeDtypeStruct((N_CHIPS,) + s.shape, s.dtype) for s in shards],
                  [pltpu.SemaphoreType.DMA((3, n)), pltpu.SemaphoreType.DMA((3, n)), pltpu.SemaphoreType.DMA((n,))], copies)


def _sibling_rider(g4):
    n = len(g4)

    def copies(g_refs, out_refs, sems):
        send_sems, recv_sems = sems
        x, y, c = _place()
        out = []
        for a, (g, o) in enumerate(zip(g_refs, out_refs)):
            half = g.shape[1] // 2
            theirs = pl.ds(pl.multiple_of((1 - c) * half, 8), half)
            cp = _remote(g.at[:, theirs, :], o, send_sems.at[a], recv_sems.at[a], (x, y, 1 - c))
            out.append(_Exchange(cp, cp))
        return out

    return _Rider(g4, [jax.ShapeDtypeStruct((N_CHIPS, g.shape[1] // 2, g.shape[2]), g.dtype) for g in g4],
                  [pltpu.SemaphoreType.DMA((n,)), pltpu.SemaphoreType.DMA((n,))], copies)


def _chips_rider(wire):
    n = len(wire)

    def copies(s_refs, out_refs, sems):
        send_sems, recv_sems = sems
        x, y, c = _place()
        out = []
        for a, (s, o) in enumerate(zip(s_refs, out_refs)):
            for k, (px, py) in enumerate(_other_chips(x, y)):
                cp = _remote(s.at[2 * px + py], o.at[RELATION_XOR[k] - 1], send_sems.at[k, a], recv_sems.at[k, a], (px, py, c))
                out.append(_Exchange(cp, cp))
        return out

    return _Rider(wire, [jax.ShapeDtypeStruct((3,) + s.shape[1:], s.dtype) for s in wire],
                  [pltpu.SemaphoreType.DMA((3, n)), pltpu.SemaphoreType.DMA((3, n))], copies)


def _local_step(x, mem, tgt, w, small, *, tq, tq_sb_bwd, tk, tk_mla, t_row, t_mm, t_wg, rest_shards=None):
    seq = x.shape[0]
    on_mesh = rest_shards is not None
    w_in_p, wqb, wkvb = _primed_weights(w)
    tabs = _rope_tables(seq)

    p32, pbf = _matmul(x, w_in_p, mode="nn", tm=256, tn=IN_WIDTH_P, tk=D_MODEL, out_dtypes=(F32, BF16), name="proj_in")
    qp, kp, vp = _mla_prep(p32, small["q_a_gain"], small["kv_a_gain"], wqb, wkvb, tabs, t=t_row)
    res = _mla_attn_fwd(qp, kp, vp, tq=tq, tk=tk_mla, tk_diag=tk, rider=_gather_rider(rest_shards) if on_mesh else None)
    o_a, lse = res[0], res[1]
    if on_mesh:
        w = dict(w, **{n: _join_chips(n, g) for n, g in zip(EARLY_NAMES, res[2:])})
    wmg, wout = w["w_merge_gate"], w["w_out"]
    wba, wbb, wbm = w["w_branch_mla"], w["w_branch_sb"], w["w_branch_mem"]
    o_b, keep_total = _sb_attn_fwd(pbf, tq=tq, tk=tk)
    (mkv,) = _matmul(mem, w["w_mem_kv"], mode="nn", tm=MEM_LEN, tn=512, tk=D_MODEL, out_dtypes=(BF16,), name="mem_kv")
    o_m = _mem_fwd(pbf, mkv, t=t_row)

    (du, merged, dgpre, h_a, h_b, h_m, dy_a, dy_b, dy_m, do_a, do_b, do_m, dgate_a, dgate_b, dgate_m,
     d_ln_g, d_ln_b, d_bmg, loss) = _mid(x, tgt, o_a, o_b, o_m, p32, wmg, small["b_merge_gate"], wba, wbb, wbm, wout,
                                         small["ln_gain"], small["ln_bias"], t=t_row)

    wg = functools.partial(_matmul, mode="tn", tm=512, out_dtypes=(F32,))
    shard = (lambda width: dict(tn=width // N_CHIPS, by_column_block=True)) if on_mesh else (lambda width: dict(tn=1024))
    dq_m, dmkv = _mem_bwd(pbf, mkv, do_m, t=t_row)
    early = {"w_mem_kv": wg(mem, dmkv, tk=MEM_LEN, tn=1024, name="grad_w_mem_kv")[0],
             "w_branch_mla": wg(h_a, dy_a, tk=t_wg, name="grad_w_branch_a", **shard(D_MODEL))[0],
             "w_branch_sb": wg(h_b, dy_b, tk=t_wg, name="grad_w_branch_b", **shard(D_MODEL))[0],
             "w_branch_mem": wg(h_m, dy_m, tk=t_wg, name="grad_w_branch_m", **shard(D_MODEL))[0],
             "w_merge_gate": wg(x, dgpre, tk=t_wg, name="grad_w_merge_gate", **shard(N_MERGE))[0],
             "w_out": wg(merged, du, tk=t_wg, tn=1024, name="grad_w_out")[0]}

    if on_mesh:
        g4 = [early[n] if early[n].ndim == 3 else _split_by_chip(n, early[n]) for n in EARLY_NAMES]
        res = _mla_attn_bwd(qp, kp, vp, o_a, lse, do_a, tq=tq, tk=tk_mla, tk_diag=tk, rider=_sibling_rider(g4))
        (dqp, dkp, dvp), got = res[:3], res[3:]
        chipsum, wire = _rs_add_sibling(g4, got, [BF16] * len(g4))
        res = _sb_attn_bwd(pbf, keep_total, do_b, tq=tq_sb_bwd, tk=tk, rider=_chips_rider(wire))
        (dq_b, dk_b, dv_b), parts = res[:3], res[3:]
        early = _rs_add_chips(chipsum, parts)
    else:
        dqp, dkp, dvp = _mla_attn_bwd(qp, kp, vp, o_a, lse, do_a, tq=tq, tk=tk_mla, tk_diag=tk)
        dq_b, dk_b, dv_b = _sb_attn_bwd(pbf, keep_total, do_b, tq=tq_sb_bwd, tk=tk)
    dlat, d_wqb, d_wkvb, d_gq, d_gkv = _mla_post(p32, dqp, dkp, dvp, small["q_a_gain"], small["kv_a_gain"], wqb, wkvb, tabs,
                                                 t=t_row)

    d_proj = [dlat, dgate_a, dq_b, dk_b, dv_b, dgate_b, dq_m, dgate_m]
    d_winp = _grad_w_in(x, d_proj, tk=min(1024, seq))

    d_win = jnp.concatenate([d_winp[:, 0:384], d_winp[:, 448:480], d_winp[:, 512:]], axis=1)
    d_wq = d_wqb.reshape(MLA_Q_LORA, MLA_HEADS, 128)[:, :, :96].reshape(MLA_Q_LORA, 768)
    d_wk = d_wkvb[:, :1024].reshape(MLA_KV_LORA, MLA_HEADS, 128)[:, :, :MLA_NOPE]
    d_wv = d_wkvb[:, 1024:].reshape(MLA_KV_LORA, MLA_HEADS, MLA_V)
    d_wkv = jnp.concatenate([d_wk, d_wv], axis=2).reshape(MLA_KV_LORA, 1024)
    late = {"w_in": d_win, "w_q_b": d_wq, "w_kv_b": d_wkv}
    small_grads = {"q_a_gain": d_gq, "kv_a_gain": d_gkv, "b_merge_gate": d_bmg, "ln_gain": d_ln_g, "ln_bias": d_ln_b}
    if not on_mesh:
        (grad_x,) = _grad_x(du, dgpre, wmg, d_proj, w_in_p, tm=256)
        return loss[0, 0], grad_x, late, small_grads, early

    g4 = [_split_by_chip(n, late[n]) for n in LATE_NAMES]
    g4.append(jnp.broadcast_to(_pack_small(small_grads)[None], (N_CHIPS, SMALL_ROWS, PACK_COLS)))
    got = _rs_to_sibling(g4)
    chipsum, wire = _rs_add_sibling(g4, got, [BF16] * len(LATE_NAMES) + [F32])
    res = _grad_x(du, dgpre, wmg, d_proj, w_in_p, tm=256, rider=_chips_rider(wire))
    late_mine = _rs_add_chips(chipsum, res[1:])
    return loss[0, 0], res[0], late_mine, None, early


def _place():
    x, y, c = lax.axis_index("x"), lax.axis_index("y"), lax.axis_index("c")
    return x, y, c


def _other_chips(x, y):
    return ((1 - x, y), (x, 1 - y), (1 - x, 1 - y))


SMALL_ROWS = 64
ADAM_STEPS_PER_HALF = 4


def _pack_small(d):
    flat = jnp.concatenate([d[n].reshape(-1) for n, _ in SMALL_SIZES])
    return jnp.pad(flat, (0, SMALL_ROWS * PACK_COLS - SMALL_TOTAL)).reshape(SMALL_ROWS, PACK_COLS)


def _unpack_small(a):
    flat, out, c0 = a.reshape(-1), {}, 0
    for n, size in SMALL_SIZES:
        out[n] = flat[c0:c0 + size].reshape(1, size)
        c0 += size
    return out


def _split_by_chip(name, full):
    r, c = full.shape
    if name in COL_SHARDED:
        return full.reshape(r, N_CHIPS, c // N_CHIPS).transpose(1, 0, 2)
    return full.reshape(N_CHIPS, r // N_CHIPS, c)


def _join_chips(name, slots):
    _, r, cs = slots.shape
    if name in COL_SHARDED:
        return slots.transpose(1, 0, 2).reshape(r, N_CHIPS * cs)
    return slots.reshape(N_CHIPS * r, cs)


HBM_SPEC = pl.BlockSpec(memory_space=pltpu.HBM)


def _gather_shards(shards):
    n = len(shards)

    def body(*refs):
        w_refs, out_refs, wb_refs = refs[:n], refs[n:2 * n], refs[2 * n:3 * n]
        send_sems, recv_sems, pass_send_sems, pass_recv_sems, local_sems = refs[3 * n:]
        x, y, c = _place()
        me = 2 * x + y
        sibling = (x, y, 1 - c)

        def halves(ref):
            half = ref.shape[-2] // 2
            return (pl.ds(pl.multiple_of(c * half, 16), half), pl.ds(pl.multiple_of((1 - c) * half, 16), half))
        for w_ref, wb_ref in zip(w_refs, wb_refs):
            rows = w_ref.shape[0]
            chunk = min(rows, 128)

            def cast(i, carry, w_ref=w_ref, wb_ref=wb_ref, chunk=chunk):
                r0 = pl.multiple_of(i * chunk, chunk)
                wb_ref[pl.ds(r0, chunk), :] = _bf(w_ref[pl.ds(r0, chunk), :])
                return carry

            lax.fori_loop(0, rows // chunk, cast, 0)
        sends, locals_ = [], []
        for a, (wb_ref, out_ref) in enumerate(zip(wb_refs, out_refs)):
            mine = pltpu.make_async_copy(wb_ref, out_ref.at[me], local_sems.at[a])
            mine.start()
            locals_.append(mine)
            mine_rows, _ = halves(wb_ref)
            for k, (px, py) in enumerate(_other_chips(x, y)):
                cp = pltpu.make_async_remote_copy(src_ref=wb_ref.at[mine_rows, :], dst_ref=out_ref.at[me, mine_rows, :],
                                                  send_sem=send_sems.at[k, a], recv_sem=recv_sems.at[k, a],
                                                  device_id=(px, py, c), device_id_type=MESH_ID)
                cp.start()
                sends.append(cp)
        for a, (wb_ref, out_ref) in enumerate(zip(wb_refs, out_refs)):
            mine_rows, _ = halves(wb_ref)
            for k, (px, py) in enumerate(_other_chips(x, y)):
                landed = out_ref.at[2 * px + py, mine_rows, :]
                pltpu.make_async_remote_copy(src_ref=wb_ref.at[mine_rows, :], dst_ref=landed, send_sem=send_sems.at[k, a],
                                             recv_sem=recv_sems.at[k, a], device_id=(px, py, c),
                                             device_id_type=MESH_ID).wait_recv()
                cp = pltpu.make_async_remote_copy(src_ref=landed, dst_ref=landed, send_sem=pass_send_sems.at[k, a],
                                                  recv_sem=pass_recv_sems.at[k, a], device_id=sibling, device_id_type=MESH_ID)
                cp.start()
                sends.append(cp)
        for a, (wb_ref, out_ref) in enumerate(zip(wb_refs, out_refs)):
            _, their_rows = halves(wb_ref)
            for k, (px, py) in enumerate(_other_chips(x, y)):
                passed = out_ref.at[2 * px + py, their_rows, :]
                pltpu.make_async_remote_copy(src_ref=passed, dst_ref=passed, send_sem=pass_send_sems.at[k, a],
                                             recv_sem=pass_recv_sems.at[k, a], device_id=sibling,
                                             device_id_type=MESH_ID).wait_recv()
        for cp in sends:
            cp.wait_send()
        for cp in locals_:
            cp.wait()

    return pl.pallas_call(
        body, name="gather_weights",
        in_specs=[pl.BlockSpec(memory_space=pltpu.VMEM)] * n,
        out_specs=[HBM_SPEC] * n,
        out_shape=[jax.ShapeDtypeStruct((N_CHIPS,) + s.shape, BF16) for s in shards],
        scratch_shapes=[pltpu.VMEM(s.shape, BF16) for s in shards]
        + [pltpu.SemaphoreType.DMA((3, n))] * 4 + [pltpu.SemaphoreType.DMA((n,))],
        compiler_params=pltpu.CompilerParams(vmem_limit_bytes=VMEM_LIMIT),
    )(*shards)


def _cast_bf16_list(arrays):
    def body(*refs):
        for a_ref, o_ref in zip(refs[:len(arrays)], refs[len(arrays):]):
            o_ref[...] = _bf(a_ref[...])

    specs = [pl.BlockSpec((a.shape[0] // 4, a.shape[1]), lambda i: (i, 0)) for a in arrays]
    return pl.pallas_call(
        body, name="cast_shards", grid=(4,), in_specs=specs, out_specs=specs,
        out_shape=[jax.ShapeDtypeStruct(a.shape, BF16) for a in arrays],
        compiler_params=_cparams(("parallel",)),
    )(*arrays)


def _rs_to_sibling(g4):
    n = len(g4)

    def body(*refs):
        g_refs, out_refs = refs[:n], refs[n:2 * n]
        send_sems, recv_sems = refs[2 * n:]
        x, y, c = _place()
        copies = []
        for a, (g_ref, out_ref) in enumerate(zip(g_refs, out_refs)):
            half = g_ref.shape[1] // 2
            theirs = pl.ds(pl.multiple_of((1 - c) * half, 8), half)
            copies.append(pltpu.make_async_remote_copy(src_ref=g_ref.at[:, theirs, :], dst_ref=out_ref, send_sem=send_sems.at[a],
                                                       recv_sem=recv_sems.at[a], device_id=(x, y, 1 - c),
                                                       device_id_type=MESH_ID))
        for cp in copies:
            cp.start()
        for cp in copies:
            cp.wait()

    return pl.pallas_call(
        body, name="rs_sibling", in_specs=[HBM_SPEC] * n, out_specs=[HBM_SPEC] * n,
        out_shape=[jax.ShapeDtypeStruct((N_CHIPS, g.shape[1] // 2, g.shape[2]), F32) for g in g4],
        scratch_shapes=[pltpu.SemaphoreType.DMA((n,)), pltpu.SemaphoreType.DMA((n,))],
    )(*g4)


def _rs_add_sibling(g4, got, wire_dtypes):
    n = len(g4)
    narrow = [a for a in range(n) if wire_dtypes[a] != F32]

    def body(c_ref, *refs):
        outs = refs[2 * n:3 * n]
        wires = dict(zip(narrow, refs[3 * n:]))
        for a, (g_ref, r_ref, o_ref) in enumerate(zip(refs[:n], refs[n:2 * n], outs)):
            s = g_ref[...] + r_ref[...]
            o_ref[...] = s
            if a in wires:
                wires[a][...] = s.astype(wires[a].dtype)

    blk = lambda r: (1, r.shape[1], r.shape[2])
    plain = lambda r: pl.BlockSpec(blk(r), lambda j, c_ref: (j, 0, 0))
    grid_spec = pltpu.PrefetchScalarGridSpec(
        num_scalar_prefetch=1, grid=(N_CHIPS,),
        in_specs=[pl.BlockSpec(blk(r), lambda j, c_ref: (j, c_ref[0], 0)) for r in got] + [plain(r) for r in got],
        out_specs=[plain(r) for r in got] + [plain(got[a]) for a in narrow])
    res = pl.pallas_call(
        body, name="rs_add_sibling", grid_spec=grid_spec,
        out_shape=[jax.ShapeDtypeStruct(r.shape, F32) for r in got]
        + [jax.ShapeDtypeStruct(got[a].shape, wire_dtypes[a]) for a in narrow],
        compiler_params=_cparams(("parallel",)),
    )(lax.axis_index("c").astype(jnp.int32).reshape(1), *g4, *got)
    chipsum = list(res[:n])
    wire = list(chipsum)
    for a, w in zip(narrow, res[n:]):
        wire[a] = w
    return chipsum, wire


RELATION_XOR = (2, 1, 3)


def _rs_add_chips(chipsum, parts):
    n = len(parts)

    def body(me_ref, *refs):
        me = me_ref[0]
        for s_ref, p_ref, o_ref in zip(refs[:n], refs[n:2 * n], refs[2 * n:]):
            own = s_ref[0]
            total = None
            for k in range(N_CHIPS):
                theirs = p_ref[jnp.maximum(jnp.bitwise_xor(me, k) - 1, 0)].astype(F32)
                term = jnp.where(me == k, own, theirs)
                total = term if total is None else total + term
            o_ref[...] = total

    grid_spec = pltpu.PrefetchScalarGridSpec(
        num_scalar_prefetch=1, grid=(2,),
        in_specs=[pl.BlockSpec((1, p.shape[1] // 2, p.shape[2]), lambda i, me_ref: (me_ref[0], i, 0)) for p in parts]
        + [pl.BlockSpec((3, p.shape[1] // 2, p.shape[2]), lambda i, me_ref: (0, i, 0)) for p in parts],
        out_specs=[pl.BlockSpec((p.shape[1] // 2, p.shape[2]), lambda i, me_ref: (i, 0)) for p in parts])
    me = (2 * lax.axis_index("x") + lax.axis_index("y")).astype(jnp.int32).reshape(1)
    return pl.pallas_call(
        body, name="rs_add_chips", grid_spec=grid_spec,
        out_shape=[jax.ShapeDtypeStruct(p.shape[1:], F32) for p in parts],
        compiler_params=_cparams(("parallel",)),
    )(me, *chipsum, *parts)


def _rs_swap_halves(halves):
    n = len(halves)

    def body(*refs):
        h_refs, out_refs = refs[:n], refs[n:2 * n]
        send_sems, recv_sems = refs[2 * n:]
        x, y, c = _place()
        copies = [pltpu.make_async_remote_copy(src_ref=h_ref, dst_ref=out_ref, send_sem=send_sems.at[a], recv_sem=recv_sems.at[a],
                                               device_id=(x, y, 1 - c), device_id_type=MESH_ID)
                  for a, (h_ref, out_ref) in enumerate(zip(h_refs, out_refs))]
        for cp in copies:
            cp.start()
        for cp in copies:
            cp.wait()

    return pl.pallas_call(
        body, name="rs_swap_halves", in_specs=[HBM_SPEC] * n, out_specs=[HBM_SPEC] * n,
        out_shape=[jax.ShapeDtypeStruct(h.shape, F32) for h in halves],
        scratch_shapes=[pltpu.SemaphoreType.DMA((n,)), pltpu.SemaphoreType.DMA((n,))],
    )(*halves)


def _adamw_list(ws, g_mine, g_theirs, ms, vs):
    n = len(ws)

    def body(c_ref, *refs):
        w_refs, gm_refs, gt_refs, m_refs, v_refs = (refs[k * n:(k + 1) * n] for k in range(5))
        g_refs, d_refs, nm_refs, nv_refs = (refs[k * n:(k + 1) * n] for k in range(5, 9))
        mine = (pl.program_id(0) // ADAM_STEPS_PER_HALF) == c_ref[0]
        for a in range(n):
            gv = jnp.where(mine, gm_refs[a][...], gt_refs[a][...])
            g_refs[a][...] = gv
            m_new = ADAM_B1 * m_refs[a][...] + (1.0 - ADAM_B1) * gv
            v_new = ADAM_B2 * v_refs[a][...] + (1.0 - ADAM_B2) * (gv * gv)
            m_hat = m_new / (1.0 - ADAM_B1 ** ADAM_STEP)
            v_hat = v_new / (1.0 - ADAM_B2 ** ADAM_STEP)
            d_refs[a][...] = -ADAM_LR * (m_hat / (jnp.sqrt(v_hat) + ADAM_EPS) + ADAM_WD * w_refs[a][...])
            nm_refs[a][...] = m_new
            nv_refs[a][...] = v_new

    steps = 2 * ADAM_STEPS_PER_HALF
    whole = [pl.BlockSpec((w.shape[0] // steps, w.shape[1]), lambda i, c_ref: (i, 0)) for w in ws]
    half = [pl.BlockSpec((w.shape[0] // steps, w.shape[1]), lambda i, c_ref: (i % ADAM_STEPS_PER_HALF, 0)) for w in ws]
    shapes = [jax.ShapeDtypeStruct(w.shape, F32) for w in ws]
    grid_spec = pltpu.PrefetchScalarGridSpec(num_scalar_prefetch=1, grid=(steps,),
                                             in_specs=whole + half + half + whole + whole, out_specs=whole * 4)
    res = pl.pallas_call(
        body, name="adamw", grid_spec=grid_spec, out_shape=shapes * 4,
        compiler_params=_cparams(("parallel",)),
    )(lax.axis_index("c").astype(jnp.int32).reshape(1), *ws, *g_mine, *g_theirs, *ms, *vs)
    return res[:n], res[n:2 * n], res[2 * n:3 * n], res[3 * n:]


WEIGHT_NAMES = ("w_in", "w_mem_kv", "q_a_gain", "w_q_b", "kv_a_gain", "w_kv_b", "w_branch_mla", "w_branch_sb",
                "w_branch_mem", "w_merge_gate", "b_merge_gate", "w_out", "ln_gain", "ln_bias")
SMALL_NAMES = tuple(n for n, _ in SMALL_SIZES)


def kernel(x, mem, w_in, w_mem_kv, q_a_gain, w_q_b, kv_a_gain, w_kv_b, w_branch_mla, w_branch_sb, w_branch_mem, w_merge_gate, b_merge_gate, w_out, ln_gain, ln_bias, loss_target, m_w_in, m_w_mem_kv, m_q_a_gain, m_w_q_b, m_kv_a_gain, m_w_kv_b, m_w_branch_mla, m_w_branch_sb, m_w_branch_mem, m_w_merge_gate, m_b_merge_gate, m_w_out, m_ln_gain, m_ln_bias, v_w_in, v_w_mem_kv, v_q_a_gain, v_w_q_b, v_kv_a_gain, v_w_kv_b, v_w_branch_mla, v_w_branch_sb, v_w_branch_mem, v_w_merge_gate, v_b_merge_gate, v_w_out, v_ln_gain, v_ln_bias):
    weights = dict(zip(WEIGHT_NAMES, (w_in, w_mem_kv, q_a_gain, w_q_b, kv_a_gain, w_kv_b, w_branch_mla, w_branch_sb,
                                      w_branch_mem, w_merge_gate, b_merge_gate, w_out, ln_gain, ln_bias)))
    mom1 = dict(zip(WEIGHT_NAMES, (m_w_in, m_w_mem_kv, m_q_a_gain, m_w_q_b, m_kv_a_gain, m_w_kv_b, m_w_branch_mla,
                                   m_w_branch_sb, m_w_branch_mem, m_w_merge_gate, m_b_merge_gate, m_w_out, m_ln_gain,
                                   m_ln_bias)))
    mom2 = dict(zip(WEIGHT_NAMES, (v_w_in, v_w_mem_kv, v_q_a_gain, v_w_q_b, v_kv_a_gain, v_w_kv_b, v_w_branch_mla,
                                   v_w_branch_sb, v_w_branch_mem, v_w_merge_gate, v_b_merge_gate, v_w_out, v_ln_gain,
                                   v_ln_bias)))
    def as_list(d):
        return [d[n][0] for n in BIG_NAMES] + [_pack_small({n: d[n] for n in SMALL_NAMES})]

    w_list, m_list, v_list = as_list(weights), as_list(mom1), as_list(mom2)

    gathered = _gather_shards([weights[n][0] for n in LATE_NAMES])
    first_w = {n: _join_chips(n, g) for n, g in zip(LATE_NAMES, gathered)}
    rest_shards = _cast_bf16_list([weights[n][0] for n in EARLY_NAMES])
    small = {n: weights[n] for n in SMALL_NAMES}

    seq = x.shape[1]
    loss, grad_x, late_mine, _, early_mine = _local_step(
        x[0], mem[0], loss_target[0], first_w, small, tq=min(1024, seq), tq_sb_bwd=512, tk=256, tk_mla=512, t_row=256, t_mm=512,
        t_wg=min(2048, seq), rest_shards=rest_shards)
    by_name = dict(zip(EARLY_NAMES + LATE_NAMES + ("small",), list(early_mine) + list(late_mine)))
    mine = [by_name[n] for n in BIG_NAMES + ("small",)]
    theirs = _rs_swap_halves(mine)
    g_list, d_list, nm_list, nv_list = _adamw_list(w_list, mine, theirs, m_list, v_list)

    loss = lax.psum(loss, ("x", "y", "c"))
    outs = [loss, grad_x[None]]
    for arrays in (g_list, d_list, nm_list, nv_list):
        big = dict(zip(BIG_NAMES, arrays[:-1]))
        sm = _unpack_small(arrays[-1])
        outs.extend(big[n][None] if n in big else sm[n] for n in WEIGHT_NAMES)
    return tuple(outs)
```

```python
import functools
import math

import numpy as np
import jax
import jax.numpy as jnp
from jax import lax
from jax.experimental import pallas as pl
from jax.experimental.pallas import tpu as pltpu

F32 = jnp.float32
BF16 = jnp.bfloat16
MESH_ID = pl.DeviceIdType.MESH

D_MODEL = 1024
MEM_LEN = 256
MLA_HEADS = 8
MLA_NOPE = 64
MLA_ROPE = 32
MLA_V = 64
MLA_Q_LORA = 256
MLA_KV_LORA = 128
SB_HEADS = 8
SB_HEAD_DIM = 64
MEM_HEADS = 4
MEM_HEAD_DIM = 128
ROPE_BASE = 10000.0
RMS_EPS = 1e-6
LN_EPS = 1e-5
DEEPNORM_ALPHA = 2.0 ** 0.25
MLA_SCALE = 1.0 / math.sqrt(MLA_NOPE + MLA_ROPE)
SB_SCALE = 1.0 / math.sqrt(SB_HEAD_DIM)
MEM_SCALE = 1.0 / math.sqrt(MEM_HEAD_DIM)

ADAM_LR = 0.001
ADAM_B1 = 0.9
ADAM_B2 = 0.999
ADAM_EPS = 1e-08
ADAM_WD = 0.01
ADAM_STEP = 10

LANES = 128
HALF = 64
N_CHIPS = 4
PACK_COLS = 1024
VMEM_LIMIT = 56 * 1024 * 1024

IN_WIDTH_P = 4096
BLK_LAT, BLK_GATE_A, BLK_QB, BLK_KB, BLK_VB, BLK_GATE_B, BLK_QM, BLK_GATE_M = range(8)
N_MERGE = 3 * D_MODEL

BIG_NAMES = ("w_in", "w_mem_kv", "w_q_b", "w_kv_b", "w_branch_mla", "w_branch_sb", "w_branch_mem", "w_merge_gate", "w_out")
COL_SHARDED = ("w_in", "w_q_b", "w_kv_b", "w_branch_mla", "w_branch_sb", "w_branch_mem", "w_merge_gate")
SMALL_SIZES = (("q_a_gain", 256), ("kv_a_gain", 128), ("b_merge_gate", 3072), ("ln_gain", 1024), ("ln_bias", 1024))
SMALL_TOTAL = sum(s for _, s in SMALL_SIZES)


def _cparams(sem=None):
    return pltpu.CompilerParams(dimension_semantics=sem, vmem_limit_bytes=VMEM_LIMIT)


def _dot(a, b):
    return jnp.dot(a, b, preferred_element_type=F32)


def _dot_nt(a, b):
    return lax.dot_general(a, b, (((1,), (1,)), ((), ())), preferred_element_type=F32)


def _dot_tn(a, b):
    return lax.dot_general(a, b, (((0,), (0,)), ((), ())), preferred_element_type=F32)


def _bf(x):
    return x.astype(BF16)


def _dot_cols(a, w_ref):
    return jnp.concatenate([_dot(a, w_ref[j]) for j in range(w_ref.shape[0])], axis=1)


def _dot_nt_cols(a, w_ref):
    cs = w_ref.shape[2]
    out = None
    for j in range(w_ref.shape[0]):
        term = _dot_nt(a[:, j * cs:(j + 1) * cs], w_ref[j])
        out = term if out is None else out + term
    return out


def _sigmoid(x):
    return 1.0 / (1.0 + jnp.exp(-x))


def _matmul(a, b, *, mode, tm, tn, tk, out_dtypes, name, add=None, add_scale=1.0, by_column_block=False):
    if mode == "nn":
        (m, k), n = a.shape, b.shape[1]
        a_spec = pl.BlockSpec((tm, tk), lambda i, j, kk: (i, kk))
        b_spec = pl.BlockSpec((tk, tn), lambda i, j, kk: (kk, j))
        dot = _dot
    elif mode == "nt":
        (m, k), n = a.shape, b.shape[0]
        a_spec = pl.BlockSpec((tm, tk), lambda i, j, kk: (i, kk))
        b_spec = pl.BlockSpec((tn, tk), lambda i, j, kk: (j, kk))
        dot = _dot_nt
    else:
        (k, m), n = a.shape, b.shape[1]
        a_spec = pl.BlockSpec((tk, tm), lambda i, j, kk: (kk, i))
        b_spec = pl.BlockSpec((tk, tn), lambda i, j, kk: (kk, j))
        dot = _dot_tn
    assert m % tm == 0 and n % tn == 0 and k % tk == 0, (name, m, n, k)
    nk = k // tk
    n_out = len(out_dtypes)
    has_add = add is not None

    def body(*refs):
        a_ref, b_ref = refs[0], refs[1]
        add_ref = refs[2] if has_add else None
        outs = refs[2 + has_add: 2 + has_add + n_out]
        acc = refs[-1]
        kk = pl.program_id(2)

        @pl.when(kk == 0)
        def _():
            acc[...] = jnp.zeros_like(acc)

        acc[...] += dot(_bf(a_ref[...]), _bf(b_ref[...]))

        @pl.when(kk == nk - 1)
        def _():
            r = acc[...]
            if has_add:
                r = r + add_scale * add_ref[...]
            for o in outs:
                o[...] = r.astype(o.dtype)

    in_specs = [a_spec, b_spec]
    args = [a, b]
    if has_add:
        in_specs.append(pl.BlockSpec((tm, tn), lambda i, j, kk: (i, j)))
        args.append(add)
    if by_column_block:
        out_spec = pl.BlockSpec((None, tm, tn), lambda i, j, kk: (j, i, 0))
        out_dims = (n // tn, m, tn)
    else:
        out_spec = pl.BlockSpec((tm, tn), lambda i, j, kk: (i, j))
        out_dims = (m, n)
    res = pl.pallas_call(
        body, name=name, grid=(m // tm, n // tn, nk),
        in_specs=in_specs, out_specs=[out_spec] * n_out,
        out_shape=[jax.ShapeDtypeStruct(out_dims, dt) for dt in out_dtypes],
        scratch_shapes=[pltpu.VMEM((tm, tn), F32)],
        compiler_params=_cparams(("parallel", "parallel", "arbitrary")),
    )(*args)
    return res


def _rope_tables(seq):
    half = MLA_ROPE // 2
    freqs = ROPE_BASE ** (-jnp.arange(half, dtype=F32) / half)
    ang = jnp.arange(seq, dtype=jnp.int32).astype(F32)[:, None] * freqs[None, :]
    cos, sin = jnp.cos(ang), jnp.sin(ang)
    z = lambda w: jnp.zeros((seq, w), F32)
    c_q = jnp.concatenate([jnp.ones((seq, MLA_NOPE), F32), cos, cos, z(32)], axis=1)
    c_k = jnp.concatenate([z(MLA_NOPE), cos, cos, z(32)], axis=1)
    s_lo = jnp.concatenate([z(MLA_NOPE), -sin, z(half), z(32)], axis=1)
    s_hi = jnp.concatenate([z(MLA_NOPE), z(half), sin, z(32)], axis=1)
    return c_q, c_k, s_lo, s_hi


def _rope_fwd(x, c, s_lo, s_hi):
    return x * c + pltpu.roll(x, LANES - 16, 1) * s_lo + pltpu.roll(x, 16, 1) * s_hi


def _rope_bwd(d, c, s_lo, s_hi):
    return d * c - pltpu.roll(d, 16, 1) * s_hi - pltpu.roll(d, LANES - 16, 1) * s_lo


def _rms_fwd(x, g):
    r = lax.rsqrt(jnp.mean(x * x, axis=-1, keepdims=True) + RMS_EPS)
    xn = x * r
    return xn * g, xn, r


def _mla_prep(p32, gq, gkv, wqb, wkvb, tabs, *, t):
    seq = p32.shape[0]

    def body(lat_ref, gq_ref, gkv_ref, wqb_ref, wkvb_ref, cq_ref, ck_ref, slo_ref, shi_ref, q_ref, k_ref, v_ref):
        lat = lat_ref[...]
        slo, shi = slo_ref[...], shi_ref[...]
        nq, _, _ = _rms_fwd(lat[:, 0:MLA_Q_LORA], gq_ref[...])
        qa = _dot(_bf(nq), wqb_ref[...])
        cq = cq_ref[...]
        for h in range(MLA_HEADS):
            blk = qa[:, h * LANES:(h + 1) * LANES]
            q_ref[:, h * LANES:(h + 1) * LANES] = _bf(_rope_fwd(blk, cq, slo, shi))
        nkv, _, _ = _rms_fwd(lat[:, MLA_Q_LORA:MLA_Q_LORA + MLA_KV_LORA], gkv_ref[...])
        kv = _dot(_bf(nkv), wkvb_ref[...])
        kpe = _rope_fwd(lat[:, 384:512], ck_ref[...], slo, shi)
        for h in range(MLA_HEADS):
            k_ref[:, h * LANES:(h + 1) * LANES] = _bf(kv[:, h * LANES:(h + 1) * LANES] + kpe)
        v_ref[...] = _bf(kv[:, MLA_HEADS * LANES:])

    row = lambda w: pl.BlockSpec((t, w), lambda i: (i, 0))
    full = lambda shp: pl.BlockSpec(shp, lambda i: (0, 0))
    return pl.pallas_call(
        body, name="mla_prep", grid=(seq // t,),
        in_specs=[row(512), full((1, MLA_Q_LORA)), full((1, MLA_KV_LORA)), full(wqb.shape), full(wkvb.shape),
                  row(LANES), row(LANES), row(LANES), row(LANES)],
        out_specs=[row(1024), row(1024), row(512)],
        out_shape=[jax.ShapeDtypeStruct((seq, 1024), BF16), jax.ShapeDtypeStruct((seq, 1024), BF16),
                   jax.ShapeDtypeStruct((seq, 512), BF16)],
        compiler_params=_cparams(("parallel",)),
    )(p32, gq, gkv, wqb, wkvb, *tabs)


def _mla_post(p32, dq, dk, dv, gq, gkv, wqb, wkvb, tabs, *, t):
    seq = p32.shape[0]

    def body(lat_ref, dq_ref, dk_ref, dv_ref, gq_ref, gkv_ref, wqb_ref, wkvb_ref, cq_ref, ck_ref, slo_ref, shi_ref,
             dlat_ref, dwqb_ref, dwkvb_ref, dgq_ref, dgkv_ref):
        @pl.when(pl.program_id(0) == 0)
        def _():
            dwqb_ref[...] = jnp.zeros_like(dwqb_ref)
            dwkvb_ref[...] = jnp.zeros_like(dwkvb_ref)
            dgq_ref[...] = jnp.zeros_like(dgq_ref)
            dgkv_ref[...] = jnp.zeros_like(dgkv_ref)

        lat = lat_ref[...]
        slo, shi = slo_ref[...], shi_ref[...]
        cq = cq_ref[...]
        gq_v, gkv_v = gq_ref[...], gkv_ref[...]
        nq, xq, rq = _rms_fwd(lat[:, 0:MLA_Q_LORA], gq_v)
        nkv, xkv, rkv = _rms_fwd(lat[:, MLA_Q_LORA:MLA_Q_LORA + MLA_KV_LORA], gkv_v)

        dqa = jnp.concatenate(
            [_rope_bwd(dq_ref[:, h * LANES:(h + 1) * LANES], cq, slo, shi) for h in range(MLA_HEADS)], axis=1)
        dqa_b = _bf(dqa)
        dwqb_ref[...] += _dot_tn(_bf(nq), dqa_b)
        dnq = _dot_nt(dqa_b, wqb_ref[...])
        dgq_ref[...] += jnp.sum(dnq * xq, axis=0, keepdims=True)
        dxn = dnq * gq_v
        dcq = rq * (dxn - xq * jnp.mean(dxn * xq, axis=-1, keepdims=True))

        dkf = dk_ref[...]
        dkv_b = _bf(jnp.concatenate([dkf, dv_ref[...]], axis=1))
        dwkvb_ref[...] += _dot_tn(_bf(nkv), dkv_b)
        dnkv = _dot_nt(dkv_b, wkvb_ref[...])
        dgkv_ref[...] += jnp.sum(dnkv * xkv, axis=0, keepdims=True)
        dxn = dnkv * gkv_v
        dckv = rkv * (dxn - xkv * jnp.mean(dxn * xkv, axis=-1, keepdims=True))

        dkpe = dkf[:, 0:LANES]
        for h in range(1, MLA_HEADS):
            dkpe = dkpe + dkf[:, h * LANES:(h + 1) * LANES]
        dkr = _rope_bwd(dkpe, ck_ref[...], slo, shi)
        dlat_ref[...] = _bf(jnp.concatenate([dcq, dckv, dkr], axis=1))

    row = lambda w: pl.BlockSpec((t, w), lambda i: (i, 0))
    full = lambda shp: pl.BlockSpec(shp, lambda i: (0, 0))
    return pl.pallas_call(
        body, name="mla_post", grid=(seq // t,),
        in_specs=[row(512), row(1024), row(1024), row(512), full((1, MLA_Q_LORA)), full((1, MLA_KV_LORA)),
                  full(wqb.shape), full(wkvb.shape), row(LANES), row(LANES), row(LANES), row(LANES)],
        out_specs=[row(512), full(wqb.shape), full(wkvb.shape), full((1, MLA_Q_LORA)), full((1, MLA_KV_LORA))],
        out_shape=[jax.ShapeDtypeStruct((seq, 512), BF16), jax.ShapeDtypeStruct(wqb.shape, F32),
                   jax.ShapeDtypeStruct(wkvb.shape, F32), jax.ShapeDtypeStruct((1, MLA_Q_LORA), F32),
                   jax.ShapeDtypeStruct((1, MLA_KV_LORA), F32)],
        compiler_params=_cparams(("arbitrary",)),
    )(p32, dq, dk, dv, gq, gkv, wqb, wkvb, *tabs)


def _split_bf16(x):
    hi = _bf(x)
    return hi, _bf(x - hi.astype(F32))


def _tri_sum(x, u):
    hi, lo = _split_bf16(x)
    return _dot(hi, u) + _dot(lo, u)


def _softplus(z):
    return jnp.maximum(z, 0.0) + jnp.log(1.0 + jnp.exp(-jnp.abs(z)))


def _head_queries(q, left):
    zero = jnp.zeros_like(q)
    return jnp.where(left, q, zero) * SB_SCALE, jnp.where(left, zero, q) * SB_SCALE


ROW_GROUP = 128
SB_BWD_CHAINS_IN_FLIGHT = 8
ANY_HBM = pl.BlockSpec(memory_space=pltpu.HBM)


class _Exchange:
    def __init__(self, send, landing):
        self.send, self.landing = send, landing

    def start(self):
        self.send.start()

    def wait(self):
        self.landing.wait_recv()
        self.send.wait_send()


class _Rider:
    def __init__(self, operands, out_shapes, sem_shapes, copies):
        self.operands, self.out_shapes, self.sem_shapes, self.copies = list(operands), list(out_shapes), list(sem_shapes), copies


def _call_with_rider(body, rider, *, name, grid, in_specs, out_specs, out_shape, args, semantics, scratch=()):
    scratch = list(scratch)
    if rider is None:
        return pl.pallas_call(body, name=name, grid=grid, in_specs=in_specs, out_specs=out_specs, out_shape=out_shape,
                              scratch_shapes=scratch, compiler_params=_cparams(semantics))(*args)
    n_in, n_out, n_rin, n_rout = len(in_specs), len(out_specs), len(rider.operands), len(rider.out_shapes)

    def full_body(*refs):
        ins, r_ins = refs[:n_in], refs[n_in:n_in + n_rin]
        outs = refs[n_in + n_rin:n_in + n_rin + n_out]
        r_outs = refs[n_in + n_rin + n_out:n_in + n_rin + n_out + n_rout]
        rest = refs[n_in + n_rin + n_out + n_rout:]
        own_scratch, sems = rest[:len(scratch)], rest[len(scratch):]
        first, last = None, None
        for axis, size in enumerate(grid):
            at_start, at_end = pl.program_id(axis) == 0, pl.program_id(axis) == size - 1
            first = at_start if first is None else first & at_start
            last = at_end if last is None else last & at_end

        @pl.when(first)
        def _():
            for cp in rider.copies(r_ins, r_outs, sems):
                cp.start()

        body(*ins, *outs, *own_scratch)

        @pl.when(last)
        def _():
            for cp in rider.copies(r_ins, r_outs, sems):
                cp.wait()

    return pl.pallas_call(
        full_body, name=name, grid=grid, in_specs=list(in_specs) + [ANY_HBM] * n_rin,
        out_specs=list(out_specs) + [ANY_HBM] * n_rout, out_shape=list(out_shape) + rider.out_shapes,
        scratch_shapes=scratch + rider.sem_shapes, compiler_params=_cparams(("arbitrary",) * len(grid)),
    )(*args, *rider.operands)


def _chains(tq):
    return [(h, g) for g in range(tq // ROW_GROUP) for h in range(2)]


def _chain_pattern(g, m, tk, strict):
    r_lo, r_hi = g * ROW_GROUP, (g + 1) * ROW_GROUP - 1
    c_lo, c_hi = m * tk, (m + 1) * tk - 1
    if (c_lo >= r_hi) if strict else (c_lo > r_hi):
        return None
    if (c_hi < r_lo) if strict else (c_hi <= r_lo):
        return True
    rr = lax.broadcasted_iota(jnp.int32, (ROW_GROUP, tk), 0) + r_lo
    cc = lax.broadcasted_iota(jnp.int32, (ROW_GROUP, tk), 1) + c_lo
    return (cc < rr) if strict else (cc <= rr)


def _masked(x, pat, fill=0.0):
    return x if pat is True else jnp.where(pat, x, fill)


def _rows(g):
    return slice(g * ROW_GROUP, (g + 1) * ROW_GROUP)


def _tri_matrix(tk, cmp):
    rr = lax.broadcasted_iota(jnp.int32, (tk, tk), 0)
    cc = lax.broadcasted_iota(jnp.int32, (tk, tk), 1)
    return cmp(rr, cc).astype(BF16)


def _mla_attn_fwd(qp, kp, vp, *, tq, tk, tk_diag, rider=None):
    seq = qp.shape[0]
    neg = float(np.finfo(np.float32).min)
    chains = _chains(tq)

    def body(q_ref, k_ref, v_ref, o_ref, lse_ref):
        i = pl.program_id(1)
        left = lax.broadcasted_iota(jnp.int32, (tq, LANES), 1) < HALF
        qs = [q_ref[_rows(g), h * LANES:(h + 1) * LANES] for h, g in chains]

        def block(start, carry, m, tk):
            v = v_ref[pl.ds(start, tk), :]
            pats = [True if m is None else _chain_pattern(g, m, tk, False) for _, g in chains]
            live = [n for n, p in enumerate(pats) if p is not None]
            ss = {n: _dot_nt(qs[n], k_ref[pl.ds(start, tk), chains[n][0] * LANES:(chains[n][0] + 1) * LANES]) for n in live}
            new = list(carry)
            for n in live:
                m_old, l_old, acc = carry[n]
                s = _masked(ss[n] * MLA_SCALE, pats[n], neg)
                m_new = jnp.maximum(m_old, jnp.max(s, axis=-1, keepdims=True))
                a = jnp.exp(m_old - m_new)
                p = jnp.exp(s - m_new)
                new[n] = (m_new, a * l_old + jnp.sum(p, axis=-1, keepdims=True), a * acc + _dot(_bf(p), v))
            return tuple(new)

        init = (jnp.full((ROW_GROUP, 1), -1e30, F32), jnp.zeros((ROW_GROUP, 1), F32), jnp.zeros((ROW_GROUP, LANES), F32))
        def two_blocks(j, c):
            c = block(pl.multiple_of(2 * j * tk, tk), c, None, tk)
            return block(pl.multiple_of((2 * j + 1) * tk, tk), c, None, tk)

        carry = lax.fori_loop(0, i * (tq // tk) // 2, two_blocks, (init,) * len(chains))
        for m in range(tq // tk_diag):
            carry = block(pl.multiple_of(i * tq + m * tk_diag, tk_diag), carry, m, tk_diag)
        per_head = []
        for h in range(2):
            mine = [carry[n] for n, (ch, _) in enumerate(chains) if ch == h]
            per_head.append((jnp.concatenate([acc / l for _, l, acc in mine], axis=0),
                             jnp.concatenate([mm + jnp.log(l) for mm, l, _ in mine], axis=0)))
        o_ref[...] = jnp.where(left, per_head[0][0], per_head[1][0])
        lse_ref[...] = jnp.where(left, per_head[0][1], per_head[1][1])

    return _call_with_rider(
        body, rider, name="mla_fwd", grid=(MLA_HEADS // 2, seq // tq),
        in_specs=[pl.BlockSpec((tq, 2 * LANES), lambda p, i: (i, p)), pl.BlockSpec((seq, 2 * LANES), lambda p, i: (0, p)),
                  pl.BlockSpec((seq, LANES), lambda p, i: (0, p))],
        out_specs=[pl.BlockSpec((tq, LANES), lambda p, i: (i, p)), pl.BlockSpec((tq, LANES), lambda p, i: (i, p))],
        out_shape=[jax.ShapeDtypeStruct((seq, 512), F32), jax.ShapeDtypeStruct((seq, 512), F32)],
        args=(qp, kp, vp), semantics=("parallel", "parallel"))


def _mla_attn_bwd(qp, kp, vp, o, lse, do, *, tq, tk, tk_diag, rider=None):
    seq = qp.shape[0]
    chains = _chains(tq)

    def body(q_ref, k_ref, v_ref, o_ref, lse_ref, do_ref, dq_ref, dk_ref, dv_ref, qt_ref, dot_ref):
        i = pl.program_id(1)

        @pl.when(i == 0)
        def _():
            dk_ref[...] = jnp.zeros_like(dk_ref)
            dv_ref[...] = jnp.zeros_like(dv_ref)

        left = lax.broadcasted_iota(jnp.int32, (tq, LANES), 1) < HALF
        do_f = do_ref[...]
        prod = do_f * o_ref[...]
        lse_v = lse_ref[...]
        do_heads = (_bf(jnp.where(left, do_f, 0.0)), _bf(jnp.where(left, 0.0, do_f)))
        delta_heads = (jnp.sum(jnp.where(left, prod, 0.0), axis=-1, keepdims=True),
                       jnp.sum(jnp.where(left, 0.0, prod), axis=-1, keepdims=True))
        qs = [q_ref[_rows(g), h * LANES:(h + 1) * LANES] for h, g in chains]
        dos = [do_heads[h][_rows(g)] for h, g in chains]
        deltas = [delta_heads[h][_rows(g)] for h, g in chains]
        lses = [lse_v[_rows(g), h * HALF:h * HALF + 1] for h, g in chains]
        for h in range(2):
            qt_ref[h] = q_ref[:, h * LANES:(h + 1) * LANES].T
            dot_ref[h] = do_heads[h].T
        q_t = [qt_ref.at[h] for h in range(2)]
        do_t = [dot_ref.at[h] for h in range(2)]

        def block(start, carry, m, tk):
            v = v_ref[pl.ds(start, tk), :]
            pats = [True if m is None else _chain_pattern(g, m, tk, False) for _, g in chains]
            live = [n for n, p in enumerate(pats) if p is not None]
            ks = [k_ref[pl.ds(start, tk), h * LANES:(h + 1) * LANES] for h in range(2)]
            ss = {n: _dot_nt(qs[n], ks[chains[n][0]]) for n in live}
            dps = {n: _dot_nt(dos[n], v) for n in live}
            new = list(carry)
            ps, dss = {}, {}
            for n in live:
                p = _masked(jnp.exp(ss[n] * MLA_SCALE - lses[n]), pats[n])
                ps[n] = _bf(p)
                dss[n] = _bf(p * (dps[n] - deltas[n]) * MLA_SCALE)
                new[n] = carry[n] + _dot(dss[n], ks[chains[n][0]])
            dv_t, dk_t = None, []
            for h in range(2):
                mine = [n for n in live if chains[n][0] == h]
                first_row = chains[mine[0]][1] * ROW_GROUP
                ds_cat = jnp.concatenate([dss[n] for n in mine], axis=0)
                p_cat = jnp.concatenate([ps[n] for n in mine], axis=0)
                if first_row == 0:
                    q_rows_t, do_rows_t = q_t[h][...], do_t[h][...]
                else:
                    q_rows_t = q_ref[first_row:, h * LANES:(h + 1) * LANES].T
                    do_rows_t = do_heads[h][first_row:].T
                dk_t.append(_dot(q_rows_t, ds_cat))
                term = _dot(do_rows_t, p_cat)
                dv_t = term if dv_t is None else dv_t + term
            back = jnp.concatenate(dk_t + [dv_t], axis=0).T
            dk_ref[pl.ds(start, tk), :] += back[:, :2 * LANES]
            dv_ref[pl.ds(start, tk), :] += back[:, 2 * LANES:]
            return tuple(new)

        zero = jnp.zeros((ROW_GROUP, LANES), F32)
        carry = lax.fori_loop(0, i * (tq // tk), lambda j, c: block(pl.multiple_of(j * tk, tk), c, None, tk),
                              (zero,) * len(chains))
        for m in range(tq // tk_diag):
            carry = block(pl.multiple_of(i * tq + m * tk_diag, tk_diag), carry, m, tk_diag)
        for n, (h, g) in enumerate(chains):
            dq_ref[_rows(g), h * LANES:(h + 1) * LANES] = carry[n]

    two_t = pl.BlockSpec((tq, 2 * LANES), lambda p, i: (i, p))
    two_s = pl.BlockSpec((seq, 2 * LANES), lambda p, i: (0, p))
    pair_t = pl.BlockSpec((tq, LANES), lambda p, i: (i, p))
    pair_s = pl.BlockSpec((seq, LANES), lambda p, i: (0, p))
    return _call_with_rider(
        body, rider, name="mla_bwd", grid=(MLA_HEADS // 2, seq // tq),
        in_specs=[two_t, two_s, pair_s, pair_t, pair_t, pair_t],
        out_specs=[two_t, two_s, pair_s],
        out_shape=[jax.ShapeDtypeStruct((seq, 1024), F32), jax.ShapeDtypeStruct((seq, 1024), F32),
                   jax.ShapeDtypeStruct((seq, 512), F32)],
        args=(qp, kp, vp, o, lse, do), semantics=("parallel", "arbitrary"),
        scratch=[pltpu.VMEM((2, LANES, tq), BF16), pltpu.VMEM((2, LANES, tq), BF16)])


def _sb_attn_fwd(pbf, *, tq, tk):
    seq = pbf.shape[0]
    nd = tq // tk
    qb, kb, vb = BLK_QB * 4, BLK_KB * 4, BLK_VB * 4
    chains = _chains(tq)

    def body(q_ref, k_ref, v_ref, o_ref, tot_ref):
        i = pl.program_id(1)
        u_later = _tri_matrix(tk, lambda r, c: r > c)
        left = lax.broadcasted_iota(jnp.int32, (tq, LANES), 1) < HALF
        q_heads = _head_queries(q_ref[...], left)
        qs = [q_heads[h][_rows(g)] for h, g in chains]

        def block(j, carry, m):
            start = pl.multiple_of(j * tk, tk)
            k = k_ref[pl.ds(start, tk), :]
            v = v_ref[pl.ds(start, tk), :]
            pats = [True if m is None else _chain_pattern(g, m, tk, True) for _, g in chains]
            live = [n for n, p in enumerate(pats) if p is not None]
            zs = {n: _dot_nt(qs[n], k) for n in live}
            raws = {n: _softplus(zs[n]) for n in live}
            sps = {n: _masked(raws[n], pats[n]) for n in live}
            laters = {n: _tri_sum(sps[n], u_later) for n in live}
            new = list(carry)
            for n in live:
                c, acc = carry[n]
                a = _masked(jnp.exp(zs[n] - raws[n] - laters[n] - c), pats[n])
                new[n] = (c + laters[n][:, 0:1] + sps[n][:, 0:1], acc + _dot(_bf(a), v))
            return tuple(new)

        init = (jnp.zeros((ROW_GROUP, 1), F32), jnp.zeros((ROW_GROUP, LANES), F32))
        carry = (init,) * len(chains)
        for m in reversed(range(nd)):
            carry = block(i * nd + m, carry, m)
        per_trip = 4 if nd % 4 == 0 else 2

        def trip(jj, cr):
            for u in range(per_trip):
                cr = block(i * nd - 1 - (per_trip * jj + u), cr, None)
            return cr

        carry = lax.fori_loop(0, i * nd // per_trip, trip, carry)
        per_head = []
        for h in range(2):
            mine = [carry[n] for n, (ch, _) in enumerate(chains) if ch == h]
            per_head.append((jnp.concatenate([acc for _, acc in mine], axis=0), jnp.concatenate([c for c, _ in mine], axis=0)))
        o_ref[...] = jnp.where(left, per_head[0][0], per_head[1][0])
        tot_ref[...] = jnp.where(left, per_head[0][1], per_head[1][1])

    pair_t = pl.BlockSpec((tq, LANES), lambda p, i: (i, p))
    return pl.pallas_call(
        body, name="sb_fwd", grid=(SB_HEADS // 2, seq // tq),
        in_specs=[pl.BlockSpec((tq, LANES), lambda p, i: (i, qb + p)), pl.BlockSpec((seq, LANES), lambda p, i: (0, kb + p)),
                  pl.BlockSpec((seq, LANES), lambda p, i: (0, vb + p))],
        out_specs=[pair_t, pair_t],
        out_shape=[jax.ShapeDtypeStruct((seq, 512), F32), jax.ShapeDtypeStruct((seq, 512), F32)],
        compiler_params=_cparams(("parallel", "parallel")),
    )(pbf, pbf, pbf)


def _sb_attn_bwd(pbf, tot, do, *, tq, tk, rider=None):
    seq = pbf.shape[0]
    nd = tq // tk
    qb, kb, vb = BLK_QB * 4, BLK_KB * 4, BLK_VB * 4
    chains = _chains(tq)
    group = SB_BWD_CHAINS_IN_FLIGHT

    def body(q_ref, k_ref, v_ref, tot_ref, do_ref, dq_ref, dk_ref, dv_ref, qt_ref, dot_ref):
        i = pl.program_id(1)

        @pl.when(i == 0)
        def _():
            dk_ref[...] = jnp.zeros_like(dk_ref)
            dv_ref[...] = jnp.zeros_like(dv_ref)

        u_upto = _tri_matrix(tk, lambda r, c: r <= c)
        u_below = _tri_matrix(tk, lambda r, c: r < c)
        left = lax.broadcasted_iota(jnp.int32, (tq, LANES), 1) < HALF
        q_heads = _head_queries(q_ref[...], left)
        do_f = do_ref[...]
        do_heads = (_bf(jnp.where(left, do_f, 0.0)), _bf(jnp.where(left, 0.0, do_f)))
        tot_v = tot_ref[...]
        qs = [q_heads[h][_rows(g)] for h, g in chains]
        dos = [do_heads[h][_rows(g)] for h, g in chains]
        totals = [tot_v[_rows(g), h * HALF:h * HALF + 1] for h, g in chains]
        qt_ref[...] = jnp.concatenate(qs, axis=0).T
        dot_ref[...] = jnp.concatenate(dos, axis=0).T

        def block(j, carry, m):
            start = pl.multiple_of(j * tk, tk)
            k = k_ref[pl.ds(start, tk), :]
            v = v_ref[pl.ds(start, tk), :]
            pats = [True if m is None else _chain_pattern(g, m, tk, True) for _, g in chains]
            all_live = [n for n, p in enumerate(pats) if p is not None]
            new = list(carry)
            for g0 in range(0, len(all_live), group):
                live = all_live[g0:g0 + group]
                zs = {n: _dot_nt(qs[n], k) for n in live}
                das = {n: _dot_nt(dos[n], v) for n in live}
                raws = {n: _softplus(zs[n]) for n in live}
                sps = {n: _masked(raws[n], pats[n]) for n in live}
                uptos = {n: _tri_sum(sps[n], u_upto) for n in live}
                lbs, a_s, gs = {}, {}, {}
                for n in live:
                    lbs[n] = zs[n] - raws[n]
                    a = _masked(jnp.exp(lbs[n] - (totals[n] - carry[n][0] - uptos[n])), pats[n])
                    a_s[n] = _bf(a)
                    gs[n] = das[n] * a
                belows = {n: _dot(_bf(gs[n]), u_below) for n in live}
                dzs = {}
                for n in live:
                    sp_before, g_before, dq_acc = carry[n]
                    beta = jnp.exp(lbs[n])
                    dz = _masked(gs[n] * (1.0 - beta) - (g_before + belows[n]) * beta, pats[n])
                    dzs[n] = _bf(dz)
                    new[n] = (sp_before + uptos[n][:, tk - 1:tk], g_before + belows[n][:, tk - 1:tk] + gs[n][:, tk - 1:tk],
                              dq_acc + _dot(dzs[n], k))
                dz_cat = jnp.concatenate([dzs[n] for n in live], axis=0)
                a_cat = jnp.concatenate([a_s[n] for n in live], axis=0)
                if len(live) == len(chains):
                    q_rows_t, do_rows_t = qt_ref[...], dot_ref[...]
                else:
                    q_rows_t = jnp.concatenate([qs[n] for n in live], axis=0).T
                    do_rows_t = jnp.concatenate([dos[n] for n in live], axis=0).T
                both = jnp.concatenate([_dot(q_rows_t, dz_cat), _dot(do_rows_t, a_cat)], axis=0).T
                dk_ref[pl.ds(start, tk), :] += both[:, :LANES]
                dv_ref[pl.ds(start, tk), :] += both[:, LANES:]
            return tuple(new)

        zero = jnp.zeros((ROW_GROUP, 1), F32)
        init = (zero, zero, jnp.zeros((ROW_GROUP, LANES), F32))
        carry = lax.fori_loop(0, i * nd // 2, lambda j, cr: block(2 * j + 1, block(2 * j, cr, None), None),
                              (init,) * len(chains))
        for m in range(nd):
            carry = block(i * nd + m, carry, m)
        per_head = [jnp.concatenate([carry[n][2] for n, (ch, _) in enumerate(chains) if ch == h], axis=0) for h in range(2)]
        dq_ref[...] = jnp.where(left, per_head[0], per_head[1]) * SB_SCALE

    pair_t = pl.BlockSpec((tq, LANES), lambda p, i: (i, p))
    pair_s = pl.BlockSpec((seq, LANES), lambda p, i: (0, p))
    return _call_with_rider(
        body, rider, name="sb_bwd", grid=(SB_HEADS // 2, seq // tq),
        in_specs=[pl.BlockSpec((tq, LANES), lambda p, i: (i, qb + p)), pl.BlockSpec((seq, LANES), lambda p, i: (0, kb + p)),
                  pl.BlockSpec((seq, LANES), lambda p, i: (0, vb + p)), pair_t, pair_t],
        out_specs=[pair_t, pair_s, pair_s],
        out_shape=[jax.ShapeDtypeStruct((seq, 512), F32)] * 3,
        args=(pbf, pbf, pbf, tot, do), semantics=("parallel", "arbitrary"),
        scratch=[pltpu.VMEM((LANES, 2 * tq), BF16), pltpu.VMEM((LANES, 2 * tq), BF16)])


def _mem_probs(s):
    e = jnp.exp(s - jnp.max(s, axis=-1, keepdims=True))
    return e / jnp.sum(e, axis=-1, keepdims=True)


def _head_lanes(h):
    return slice(h * LANES, (h + 1) * LANES)


def _mem_fwd(pbf, mkv, *, t):
    seq = pbf.shape[0]

    def body(q_ref, kv_ref, o_ref):
        ss = [_dot_nt(q_ref[:, _head_lanes(h)], kv_ref[:, _head_lanes(h)]) * MEM_SCALE for h in range(MEM_HEADS)]
        ps = [_bf(_mem_probs(s)) for s in ss]
        for h, p in enumerate(ps):
            o_ref[:, _head_lanes(h)] = _dot(p, kv_ref[:, _head_lanes(MEM_HEADS + h)])

    return pl.pallas_call(
        body, name="mem_fwd", grid=(seq // t,),
        in_specs=[pl.BlockSpec((t, 512), lambda i: (i, BLK_QM)), pl.BlockSpec((MEM_LEN, 1024), lambda i: (0, 0))],
        out_specs=pl.BlockSpec((t, 512), lambda i: (i, 0)),
        out_shape=jax.ShapeDtypeStruct((seq, 512), F32),
        compiler_params=_cparams(("parallel",)),
    )(pbf, mkv)


def _mem_bwd(pbf, mkv, do, *, t):
    seq = pbf.shape[0]

    def body(q_ref, kv_ref, do_ref, dq_ref, dkv_ref):
        @pl.when(pl.program_id(0) == 0)
        def _():
            dkv_ref[...] = jnp.zeros_like(dkv_ref)

        heads = range(MEM_HEADS)
        qs = [q_ref[:, _head_lanes(h)] for h in heads]
        ks = [kv_ref[:, _head_lanes(h)] for h in heads]
        dos = [_bf(do_ref[:, _head_lanes(h)]) for h in heads]
        ss = [_dot_nt(qs[h], ks[h]) * MEM_SCALE for h in heads]
        dps = [_dot_nt(dos[h], kv_ref[:, _head_lanes(MEM_HEADS + h)]) for h in heads]
        ps = [_mem_probs(s) for s in ss]
        dss = [_bf(ps[h] * (dps[h] - jnp.sum(dps[h] * ps[h], axis=-1, keepdims=True)) * MEM_SCALE) for h in heads]
        for h in heads:
            dq_ref[:, _head_lanes(h)] = _dot(dss[h], ks[h])
        for h in heads:
            dkv_ref[:, _head_lanes(h)] += _dot_tn(dss[h], qs[h])
            dkv_ref[:, _head_lanes(MEM_HEADS + h)] += _dot_tn(_bf(ps[h]), dos[h])

    return pl.pallas_call(
        body, name="mem_bwd", grid=(seq // t,),
        in_specs=[pl.BlockSpec((t, 512), lambda i: (i, BLK_QM)), pl.BlockSpec((MEM_LEN, 1024), lambda i: (0, 0)),
                  pl.BlockSpec((t, 512), lambda i: (i, 0))],
        out_specs=[pl.BlockSpec((t, 512), lambda i: (i, 0)), pl.BlockSpec((MEM_LEN, 1024), lambda i: (0, 0))],
        out_shape=[jax.ShapeDtypeStruct((seq, 512), F32), jax.ShapeDtypeStruct((MEM_LEN, 1024), F32)],
        compiler_params=_cparams(("arbitrary",)),
    )(pbf, mkv, do)


def _mid(x, tgt, o_a, o_b, o_m, p32, wmg, bmg, wba, wbb, wbm, wout, ln_g, ln_b, *, t):
    seq = x.shape[0]
    inv_d = 1.0 / D_MODEL

    def body(x_ref, t_ref, oa_ref, ob_ref, om_ref, ga_ref, gb_ref, gm_ref, wmg_ref, bmg_ref, wba_ref, wbb_ref,
             wbm_ref, wout_ref, lg_ref, lb_ref,
             du_ref, mrg_ref, dgp_ref, ha_ref, hb_ref, hm_ref, dya_ref, dyb_ref, dym_ref, doa_ref, dob_ref, dom_ref,
             dga_ref, dgb_ref, dgm_ref, dgain_ref, dbias_ref, dbmg_ref, loss_ref):
        @pl.when(pl.program_id(0) == 0)
        def _():
            dgain_ref[...] = jnp.zeros_like(dgain_ref)
            dbias_ref[...] = jnp.zeros_like(dbias_ref)
            dbmg_ref[...] = jnp.zeros_like(dbmg_ref)
            loss_ref[...] = jnp.zeros_like(loss_ref)

        xv = x_ref[...]
        gate = _sigmoid(_dot_cols(_bf(xv), wmg_ref) + bmg_ref[...])

        branches = []
        merged = None
        for b, (o_ref, g_ref, w_ref, h_ref) in enumerate(((oa_ref, ga_ref, wba_ref, ha_ref), (ob_ref, gb_ref, wbb_ref, hb_ref),
                                                         (om_ref, gm_ref, wbm_ref, hm_ref))):
            o, gt = o_ref[...], g_ref[...]
            sg = _sigmoid(gt)
            silu = gt * sg
            h = _bf(o * silu)
            h_ref[...] = h
            y = _dot_cols(h, w_ref)
            g_b = gate[:, b * D_MODEL:(b + 1) * D_MODEL]
            term = g_b * y
            merged = term if merged is None else merged + term
            branches.append((o, gt, sg, silu, y, g_b))
        mrg_b = _bf(merged)
        mrg_ref[...] = mrg_b

        u = DEEPNORM_ALPHA * xv + _dot(mrg_b, wout_ref[...])
        mu = jnp.mean(u, axis=-1, keepdims=True)
        uc = u - mu
        rstd = lax.rsqrt(jnp.mean(uc * uc, axis=-1, keepdims=True) + LN_EPS)
        xhat = uc * rstd
        lg = lg_ref[...]
        y_out = xhat * lg + lb_ref[...]
        err = y_out - t_ref[...]
        loss_ref[...] += 0.5 * jnp.sum(jnp.mean(err * err, axis=-1, keepdims=True), axis=0, keepdims=True)
        dy = err * inv_d
        dgain_ref[...] += jnp.sum(dy * xhat, axis=0, keepdims=True)
        dbias_ref[...] += jnp.sum(dy, axis=0, keepdims=True)
        dxh = dy * lg
        du = rstd * (dxh - jnp.mean(dxh, axis=-1, keepdims=True) - xhat * jnp.mean(dxh * xhat, axis=-1, keepdims=True))
        du_ref[...] = du

        dmerged = _dot_nt(_bf(du), wout_ref[...])
        outs = ((dya_ref, doa_ref, dga_ref, wba_ref), (dyb_ref, dob_ref, dgb_ref, wbb_ref), (dym_ref, dom_ref, dgm_ref, wbm_ref))
        dgp = []
        for (o, gt, sg, silu, y, g_b), (dy_ref, do_ref, dg_ref, w_ref) in zip(branches, outs):
            dyb = _bf(dmerged * g_b)
            dy_ref[...] = dyb
            dgp.append(dmerged * y * g_b * (1.0 - g_b))
            dh = _dot_nt_cols(dyb, w_ref)
            do_ref[...] = dh * silu
            dg_ref[...] = _bf(dh * o * (sg * (1.0 + gt * (1.0 - sg))))
        dgp = jnp.concatenate(dgp, axis=1)
        dgp_ref[...] = _bf(dgp)
        dbmg_ref[...] += jnp.sum(dgp, axis=0, keepdims=True)

    row = lambda w: pl.BlockSpec((t, w), lambda i: (i, 0))
    pblk = lambda c: pl.BlockSpec((t, 512), lambda i: (i, c))
    full = lambda shp: pl.BlockSpec(shp, lambda i: (0,) * len(shp))
    sds = jax.ShapeDtypeStruct
    return pl.pallas_call(
        body, name="mid", grid=(seq // t,),
        in_specs=[row(1024), row(1024), row(512), row(512), row(512), pblk(BLK_GATE_A), pblk(BLK_GATE_B), pblk(BLK_GATE_M),
                  full(wmg.shape), full((1, N_MERGE)), full(wba.shape), full(wbb.shape), full(wbm.shape), full(wout.shape),
                  full((1, D_MODEL)), full((1, D_MODEL))],
        out_specs=[row(1024), row(1024), row(N_MERGE), row(512), row(512), row(512), row(1024), row(1024), row(1024),
                   row(512), row(512), row(512), row(512), row(512), row(512),
                   full((1, D_MODEL)), full((1, D_MODEL)), full((1, N_MERGE)), full((1, 1))],
        out_shape=[sds((seq, 1024), F32), sds((seq, 1024), BF16), sds((seq, N_MERGE), BF16),
                   sds((seq, 512), BF16), sds((seq, 512), BF16), sds((seq, 512), BF16),
                   sds((seq, 1024), BF16), sds((seq, 1024), BF16), sds((seq, 1024), BF16),
                   sds((seq, 512), F32), sds((seq, 512), F32), sds((seq, 512), F32),
                   sds((seq, 512), BF16), sds((seq, 512), BF16), sds((seq, 512), BF16),
                   sds((1, D_MODEL), F32), sds((1, D_MODEL), F32), sds((1, N_MERGE), F32), sds((1, 1), F32)],
        compiler_params=_cparams(("arbitrary",)),
    )(x, tgt, o_a, o_b, o_m, p32, p32, p32, wmg, bmg, wba, wbb, wbm, wout, ln_g, ln_b)


def _primed_weights(w):
    w_in = w["w_in"]
    zc = lambda n: jnp.zeros((D_MODEL, n), w_in.dtype)
    w_in_p = jnp.concatenate([w_in[:, 0:384], zc(64), w_in[:, 384:416], zc(32), w_in[:, 416:]], axis=1)
    wqb = jnp.pad(w["w_q_b"].reshape(MLA_Q_LORA, MLA_HEADS, 96), ((0, 0), (0, 0), (0, 32))).reshape(MLA_Q_LORA, 1024)
    kv3 = w["w_kv_b"].reshape(MLA_KV_LORA, MLA_HEADS, 128)
    wk = jnp.pad(kv3[:, :, :MLA_NOPE], ((0, 0), (0, 0), (0, 64))).reshape(MLA_KV_LORA, 1024)
    wv = kv3[:, :, MLA_NOPE:].reshape(MLA_KV_LORA, 512)
    return w_in_p, wqb, jnp.concatenate([wk, wv], axis=1)


PROJ_BLK = 512


def _grad_x(du, dgpre, wmg, d_proj, w_in_p, *, tm, rider=None):
    seq = du.shape[0]
    n_pieces = len(d_proj)

    def body(du_ref, dg_ref, wmg_ref, *rest):
        piece_refs, win_ref, out_ref = rest[:n_pieces], rest[n_pieces], rest[n_pieces + 1]
        d_p = jnp.concatenate([_bf(p_ref[...]) for p_ref in piece_refs], axis=1)
        out_ref[...] = (DEEPNORM_ALPHA * du_ref[...] + _dot_nt_cols(dg_ref[...], wmg_ref)) + _dot_nt(d_p, win_ref[...])

    row = lambda w: pl.BlockSpec((tm, w), lambda i: (i, 0))
    whole = lambda a: pl.BlockSpec(a.shape, lambda i: (0,) * a.ndim)
    return _call_with_rider(
        body, rider, name="grad_x", grid=(seq // tm,),
        in_specs=[row(D_MODEL), row(N_MERGE), whole(wmg)] + [row(PROJ_BLK) for _ in d_proj] + [whole(w_in_p)],
        out_specs=[row(D_MODEL)], out_shape=[jax.ShapeDtypeStruct((seq, D_MODEL), F32)],
        args=(du, dgpre, wmg, *d_proj, w_in_p), semantics=("parallel",))


def _grad_w_in(x, d_proj, *, tk):
    seq = x.shape[0]
    n_pieces = len(d_proj)
    nk = seq // tk

    def body(x_ref, *rest):
        piece_refs, out_ref, acc = rest[:n_pieces], rest[n_pieces], rest[n_pieces + 1]
        j, kk = pl.program_id(0), pl.program_id(1)

        @pl.when(kk == 0)
        def _():
            acc[...] = jnp.zeros_like(acc)

        xb = _bf(x_ref[...])
        for pair in range(n_pieces // 2):
            @pl.when(j == pair)
            def _(pair=pair):
                both = jnp.concatenate([_bf(piece_refs[2 * pair][...]), _bf(piece_refs[2 * pair + 1][...])], axis=1)
                acc[...] += _dot_tn(xb, both)

        @pl.when(kk == nk - 1)
        def _():
            out_ref[...] = acc[...]

    def piece_spec(s):
        return pl.BlockSpec((tk, PROJ_BLK), lambda j, kk: (jnp.where(j == s // 2, kk, 0), 0))

    return pl.pallas_call(
        body, name="grad_w_in", grid=(n_pieces // 2, nk),
        in_specs=[pl.BlockSpec((tk, D_MODEL), lambda j, kk: (kk, 0))] + [piece_spec(s) for s in range(n_pieces)],
        out_specs=pl.BlockSpec((D_MODEL, 2 * PROJ_BLK), lambda j, kk: (0, j)),
        out_shape=jax.ShapeDtypeStruct((D_MODEL, n_pieces * PROJ_BLK), F32),
        scratch_shapes=[pltpu.VMEM((D_MODEL, 2 * PROJ_BLK), F32)],
        compiler_params=_cparams(("parallel", "arbitrary")),
    )(x, *d_proj)


EARLY_NAMES = ("w_mem_kv", "w_branch_mla", "w_branch_sb", "w_branch_mem", "w_merge_gate", "w_out")
LATE_NAMES = ("w_in", "w_q_b", "w_kv_b")


def _remote(src, dst, send_sem, recv_sem, device):
    return pltpu.make_async_remote_copy(src_ref=src, dst_ref=dst, send_sem=send_sem, recv_sem=recv_sem, device_id=device,
                                        device_id_type=MESH_ID)


def _gather_rider(shards):
    n = len(shards)

    def copies(src_refs, out_refs, sems):
        send_sems, recv_sems, local_sems = sems
        x, y, c = _place()
        me = 2 * x + y
        out = []
        for a, (s, o) in enumerate(zip(src_refs, out_refs)):
            out.append(pltpu.make_async_copy(s, o.at[me], local_sems.at[a]))
            for k, (px, py) in enumerate(_other_chips(x, y)):
                out.append(_Exchange(_remote(s, o.at[me], send_sems.at[k, a], recv_sems.at[k, a], (px, py, c)),
                                     _remote(s, o.at[2 * px + py], send_sems.at[k, a], recv_sems.at[k, a], (px, py, c))))
        return out

    return _Rider(shards, [jax.ShapeDtypeStruct((N_CHIPS,) + s.shape, s.dtype) for s in shards],
                  [pltpu.SemaphoreType.DMA((3, n)), pltpu.SemaphoreType.DMA((3, n)), pltpu.SemaphoreType.DMA((n,))], copies)


def _sibling_rider(g4):
    n = len(g4)

    def copies(g_refs, out_refs, sems):
        send_sems, recv_sems = sems
        x, y, c = _place()
        out = []
        for a, (g, o) in enumerate(zip(g_refs, out_refs)):
            half = g.shape[1] // 2
            theirs = pl.ds(pl.multiple_of((1 - c) * half, 8), half)
            cp = _remote(g.at[:, theirs, :], o, send_sems.at[a], recv_sems.at[a], (x, y, 1 - c))
            out.append(_Exchange(cp, cp))
        return out

    return _Rider(g4, [jax.ShapeDtypeStruct((N_CHIPS, g.shape[1] // 2, g.shape[2]), g.dtype) for g in g4],
                  [pltpu.SemaphoreType.DMA((n,)), pltpu.SemaphoreType.DMA((n,))], copies)


def _chips_rider(wire):
    n = len(wire)

    def copies(s_refs, out_refs, sems):
        send_sems, recv_sems = sems
        x, y, c = _place()
        out = []
        for a, (s, o) in enumerate(zip(s_refs, out_refs)):
            for k, (px, py) in enumerate(_other_chips(x, y)):
                cp = _remote(s.at[2 * px + py], o.at[RELATION_XOR[k] - 1], send_sems.at[k, a], recv_sems.at[k, a], (px, py, c))
                out.append(_Exchange(cp, cp))
        return out

    return _Rider(wire, [jax.ShapeDtypeStruct((3,) + s.shape[1:], s.dtype) for s in wire],
                  [pltpu.SemaphoreType.DMA((3, n)), pltpu.SemaphoreType.DMA((3, n))], copies)


def _local_step(x, mem, tgt, w, small, *, tq, tq_sb_bwd, tk, tk_mla, t_row, t_mm, t_wg, rest_shards=None):
    seq = x.shape[0]
    on_mesh = rest_shards is not None
    w_in_p, wqb, wkvb = _primed_weights(w)
    tabs = _rope_tables(seq)

    p32, pbf = _matmul(x, w_in_p, mode="nn", tm=256, tn=IN_WIDTH_P, tk=D_MODEL, out_dtypes=(F32, BF16), name="proj_in")
    qp, kp, vp = _mla_prep(p32, small["q_a_gain"], small["kv_a_gain"], wqb, wkvb, tabs, t=t_row)
    res = _mla_attn_fwd(qp, kp, vp, tq=tq, tk=tk_mla, tk_diag=tk, rider=_gather_rider(rest_shards) if on_mesh else None)
    o_a, lse = res[0], res[1]
    if on_mesh:
        w = dict(w, **{n: g if n in COL_SHARDED else _join_chips(n, g) for n, g in zip(EARLY_NAMES, res[2:])})
    else:
        w = dict(w, **{n: _split_by_chip(n, w[n]) for n in EARLY_NAMES if n in COL_SHARDED})
    wmg, wout = w["w_merge_gate"], w["w_out"]
    wba, wbb, wbm = w["w_branch_mla"], w["w_branch_sb"], w["w_branch_mem"]
    o_b, keep_total = _sb_attn_fwd(pbf, tq=tq, tk=tk)
    (mkv,) = _matmul(mem, w["w_mem_kv"], mode="nn", tm=MEM_LEN, tn=512, tk=D_MODEL, out_dtypes=(BF16,), name="mem_kv")
    o_m = _mem_fwd(pbf, mkv, t=t_row)

    (du, merged, dgpre, h_a, h_b, h_m, dy_a, dy_b, dy_m, do_a, do_b, do_m, dgate_a, dgate_b, dgate_m,
     d_ln_g, d_ln_b, d_bmg, loss) = _mid(x, tgt, o_a, o_b, o_m, p32, wmg, small["b_merge_gate"], wba, wbb, wbm, wout,
                                         small["ln_gain"], small["ln_bias"], t=t_row)

    wg = functools.partial(_matmul, mode="tn", tm=512, out_dtypes=(F32,))
    shard = (lambda width: dict(tn=width // N_CHIPS, by_column_block=True)) if on_mesh else (lambda width: dict(tn=1024))
    dq_m, dmkv = _mem_bwd(pbf, mkv, do_m, t=t_row)
    early = {"w_mem_kv": wg(mem, dmkv, tk=MEM_LEN, tn=1024, name="grad_w_mem_kv")[0],
             "w_branch_mla": wg(h_a, dy_a, tk=t_wg, name="grad_w_branch_a", **shard(D_MODEL))[0],
             "w_branch_sb": wg(h_b, dy_b, tk=t_wg, name="grad_w_branch_b", **shard(D_MODEL))[0],
             "w_branch_mem": wg(h_m, dy_m, tk=t_wg, name="grad_w_branch_m", **shard(D_MODEL))[0],
             "w_merge_gate": wg(x, dgpre, tk=t_wg, name="grad_w_merge_gate", **shard(N_MERGE))[0],
             "w_out": wg(merged, du, tk=t_wg, tn=1024, name="grad_w_out")[0]}

    if on_mesh:
        g4 = [early[n] if early[n].ndim == 3 else _split_by_chip(n, early[n]) for n in EARLY_NAMES]
        res = _mla_attn_bwd(qp, kp, vp, o_a, lse, do_a, tq=tq, tk=tk_mla, tk_diag=tk, rider=_sibling_rider(g4))
        (dqp, dkp, dvp), got = res[:3], res[3:]
        chipsum, wire = _rs_add_sibling(g4, got, [BF16] * len(g4))
        res = _sb_attn_bwd(pbf, keep_total, do_b, tq=tq_sb_bwd, tk=tk, rider=_chips_rider(wire))
        (dq_b, dk_b, dv_b), parts = res[:3], res[3:]
        early = _rs_add_chips(chipsum, parts)
    else:
        dqp, dkp, dvp = _mla_attn_bwd(qp, kp, vp, o_a, lse, do_a, tq=tq, tk=tk_mla, tk_diag=tk)
        dq_b, dk_b, dv_b = _sb_attn_bwd(pbf, keep_total, do_b, tq=tq_sb_bwd, tk=tk)
    dlat, d_wqb, d_wkvb, d_gq, d_gkv = _mla_post(p32, dqp, dkp, dvp, small["q_a_gain"], small["kv_a_gain"], wqb, wkvb, tabs,
                                                 t=t_row)

    d_proj = [dlat, dgate_a, dq_b, dk_b, dv_b, dgate_b, dq_m, dgate_m]
    d_winp = _grad_w_in(x, d_proj, tk=min(1024, seq))

    d_win = jnp.concatenate([d_winp[:, 0:384], d_winp[:, 448:480], d_winp[:, 512:]], axis=1)
    d_wq = d_wqb.reshape(MLA_Q_LORA, MLA_HEADS, 128)[:, :, :96].reshape(MLA_Q_LORA, 768)
    d_wk = d_wkvb[:, :1024].reshape(MLA_KV_LORA, MLA_HEADS, 128)[:, :, :MLA_NOPE]
    d_wv = d_wkvb[:, 1024:].reshape(MLA_KV_LORA, MLA_HEADS, MLA_V)
    d_wkv = jnp.concatenate([d_wk, d_wv], axis=2).reshape(MLA_KV_LORA, 1024)
    late = {"w_in": d_win, "w_q_b": d_wq, "w_kv_b": d_wkv}
    small_grads = {"q_a_gain": d_gq, "kv_a_gain": d_gkv, "b_merge_gate": d_bmg, "ln_gain": d_ln_g, "ln_bias": d_ln_b}
    if not on_mesh:
        (grad_x,) = _grad_x(du, dgpre, wmg, d_proj, w_in_p, tm=256)
        return loss[0, 0], grad_x, late, small_grads, early

    g4 = [_split_by_chip(n, late[n]) for n in LATE_NAMES]
    g4.append(jnp.broadcast_to(_pack_small(small_grads)[None], (N_CHIPS, SMALL_ROWS, PACK_COLS)))
    got = _rs_to_sibling(g4)
    chipsum, wire = _rs_add_sibling(g4, got, [BF16] * len(LATE_NAMES) + [F32])
    res = _grad_x(du, dgpre, wmg, d_proj, w_in_p, tm=256, rider=_chips_rider(wire))
    late_mine = _rs_add_chips(chipsum, res[1:])
    return loss[0, 0], res[0], late_mine, None, early


def _place():
    x, y, c = lax.axis_index("x"), lax.axis_index("y"), lax.axis_index("c")
    return x, y, c


def _other_chips(x, y):
    return ((1 - x, y), (x, 1 - y), (1 - x, 1 - y))


SMALL_ROWS = 64
ADAM_STEPS_PER_HALF = 4


def _pack_small(d):
    flat = jnp.concatenate([d[n].reshape(-1) for n, _ in SMALL_SIZES])
    return jnp.pad(flat, (0, SMALL_ROWS * PACK_COLS - SMALL_TOTAL)).reshape(SMALL_ROWS, PACK_COLS)


def _unpack_small(a):
    flat, out, c0 = a.reshape(-1), {}, 0
    for n, size in SMALL_SIZES:
        out[n] = flat[c0:c0 + size].reshape(1, size)
        c0 += size
    return out


def _split_by_chip(name, full):
    r, c = full.shape
    if name in COL_SHARDED:
        return full.reshape(r, N_CHIPS, c // N_CHIPS).transpose(1, 0, 2)
    return full.reshape(N_CHIPS, r // N_CHIPS, c)


def _join_chips(name, slots):
    _, r, cs = slots.shape
    if name in COL_SHARDED:
        return slots.transpose(1, 0, 2).reshape(r, N_CHIPS * cs)
    return slots.reshape(N_CHIPS * r, cs)


HBM_SPEC = pl.BlockSpec(memory_space=pltpu.HBM)


def _gather_shards(shards):
    n = len(shards)

    def body(*refs):
        w_refs, out_refs, wb_refs = refs[:n], refs[n:2 * n], refs[2 * n:3 * n]
        send_sems, recv_sems, pass_send_sems, pass_recv_sems, local_sems = refs[3 * n:]
        x, y, c = _place()
        me = 2 * x + y
        sibling = (x, y, 1 - c)

        def halves(ref):
            half = ref.shape[-2] // 2
            return (pl.ds(pl.multiple_of(c * half, 16), half), pl.ds(pl.multiple_of((1 - c) * half, 16), half))
        for w_ref, wb_ref in zip(w_refs, wb_refs):
            rows = w_ref.shape[0]
            chunk = min(rows, 128)

            def cast(i, carry, w_ref=w_ref, wb_ref=wb_ref, chunk=chunk):
                r0 = pl.multiple_of(i * chunk, chunk)
                wb_ref[pl.ds(r0, chunk), :] = _bf(w_ref[pl.ds(r0, chunk), :])
                return carry

            lax.fori_loop(0, rows // chunk, cast, 0)
        sends, locals_ = [], []
        for a, (wb_ref, out_ref) in enumerate(zip(wb_refs, out_refs)):
            mine = pltpu.make_async_copy(wb_ref, out_ref.at[me], local_sems.at[a])
            mine.start()
            locals_.append(mine)
            mine_rows, _ = halves(wb_ref)
            for k, (px, py) in enumerate(_other_chips(x, y)):
                cp = pltpu.make_async_remote_copy(src_ref=wb_ref.at[mine_rows, :], dst_ref=out_ref.at[me, mine_rows, :],
                                                  send_sem=send_sems.at[k, a], recv_sem=recv_sems.at[k, a],
                                                  device_id=(px, py, c), device_id_type=MESH_ID)
                cp.start()
                sends.append(cp)
        for a, (wb_ref, out_ref) in enumerate(zip(wb_refs, out_refs)):
            mine_rows, _ = halves(wb_ref)
            for k, (px, py) in enumerate(_other_chips(x, y)):
                landed = out_ref.at[2 * px + py, mine_rows, :]
                pltpu.make_async_remote_copy(src_ref=wb_ref.at[mine_rows, :], dst_ref=landed, send_sem=send_sems.at[k, a],
                                             recv_sem=recv_sems.at[k, a], device_id=(px, py, c),
                                             device_id_type=MESH_ID).wait_recv()
                cp = pltpu.make_async_remote_copy(src_ref=landed, dst_ref=landed, send_sem=pass_send_sems.at[k, a],
                                                  recv_sem=pass_recv_sems.at[k, a], device_id=sibling, device_id_type=MESH_ID)
                cp.start()
                sends.append(cp)
        for a, (wb_ref, out_ref) in enumerate(zip(wb_refs, out_refs)):
            _, their_rows = halves(wb_ref)
            for k, (px, py) in enumerate(_other_chips(x, y)):
                passed = out_ref.at[2 * px + py, their_rows, :]
                pltpu.make_async_remote_copy(src_ref=passed, dst_ref=passed, send_sem=pass_send_sems.at[k, a],
                                             recv_sem=pass_recv_sems.at[k, a], device_id=sibling,
                                             device_id_type=MESH_ID).wait_recv()
        for cp in sends:
            cp.wait_send()
        for cp in locals_:
            cp.wait()

    return pl.pallas_call(
        body, name="gather_weights",
        in_specs=[pl.BlockSpec(memory_space=pltpu.VMEM)] * n,
        out_specs=[HBM_SPEC] * n,
        out_shape=[jax.ShapeDtypeStruct((N_CHIPS,) + s.shape, BF16) for s in shards],
        scratch_shapes=[pltpu.VMEM(s.shape, BF16) for s in shards]
        + [pltpu.SemaphoreType.DMA((3, n))] * 4 + [pltpu.SemaphoreType.DMA((n,))],
        compiler_params=pltpu.CompilerParams(vmem_limit_bytes=VMEM_LIMIT),
    )(*shards)


def _cast_bf16_list(arrays):
    def body(*refs):
        for a_ref, o_ref in zip(refs[:len(arrays)], refs[len(arrays):]):
            o_ref[...] = _bf(a_ref[...])

    specs = [pl.BlockSpec((a.shape[0] // 4, a.shape[1]), lambda i: (i, 0)) for a in arrays]
    return pl.pallas_call(
        body, name="cast_shards", grid=(4,), in_specs=specs, out_specs=specs,
        out_shape=[jax.ShapeDtypeStruct(a.shape, BF16) for a in arrays],
        compiler_params=_cparams(("parallel",)),
    )(*arrays)


def _rs_to_sibling(g4):
    n = len(g4)

    def body(*refs):
        g_refs, out_refs = refs[:n], refs[n:2 * n]
        send_sems, recv_sems = refs[2 * n:]
        x, y, c = _place()
        copies = []
        for a, (g_ref, out_ref) in enumerate(zip(g_refs, out_refs)):
            half = g_ref.shape[1] // 2
            theirs = pl.ds(pl.multiple_of((1 - c) * half, 8), half)
            copies.append(pltpu.make_async_remote_copy(src_ref=g_ref.at[:, theirs, :], dst_ref=out_ref, send_sem=send_sems.at[a],
                                                       recv_sem=recv_sems.at[a], device_id=(x, y, 1 - c),
                                                       device_id_type=MESH_ID))
        for cp in copies:
            cp.start()
        for cp in copies:
            cp.wait()

    return pl.pallas_call(
        body, name="rs_sibling", in_specs=[HBM_SPEC] * n, out_specs=[HBM_SPEC] * n,
        out_shape=[jax.ShapeDtypeStruct((N_CHIPS, g.shape[1] // 2, g.shape[2]), F32) for g in g4],
        scratch_shapes=[pltpu.SemaphoreType.DMA((n,)), pltpu.SemaphoreType.DMA((n,))],
    )(*g4)


def _rs_add_sibling(g4, got, wire_dtypes):
    n = len(g4)
    narrow = [a for a in range(n) if wire_dtypes[a] != F32]

    def body(c_ref, *refs):
        outs = refs[2 * n:3 * n]
        wires = dict(zip(narrow, refs[3 * n:]))
        for a, (g_ref, r_ref, o_ref) in enumerate(zip(refs[:n], refs[n:2 * n], outs)):
            s = g_ref[...] + r_ref[...]
            o_ref[...] = s
            if a in wires:
                wires[a][...] = s.astype(wires[a].dtype)

    blk = lambda r: (1, r.shape[1], r.shape[2])
    plain = lambda r: pl.BlockSpec(blk(r), lambda j, c_ref: (j, 0, 0))
    grid_spec = pltpu.PrefetchScalarGridSpec(
        num_scalar_prefetch=1, grid=(N_CHIPS,),
        in_specs=[pl.BlockSpec(blk(r), lambda j, c_ref: (j, c_ref[0], 0)) for r in got] + [plain(r) for r in got],
        out_specs=[plain(r) for r in got] + [plain(got[a]) for a in narrow])
    res = pl.pallas_call(
        body, name="rs_add_sibling", grid_spec=grid_spec,
        out_shape=[jax.ShapeDtypeStruct(r.shape, F32) for r in got]
        + [jax.ShapeDtypeStruct(got[a].shape, wire_dtypes[a]) for a in narrow],
        compiler_params=_cparams(("parallel",)),
    )(lax.axis_index("c").astype(jnp.int32).reshape(1), *g4, *got)
    chipsum = list(res[:n])
    wire = list(chipsum)
    for a, w in zip(narrow, res[n:]):
        wire[a] = w
    return chipsum, wire


RELATION_XOR = (2, 1, 3)


def _rs_add_chips(chipsum, parts):
    n = len(parts)

    def body(me_ref, *refs):
        me = me_ref[0]
        for s_ref, p_ref, o_ref in zip(refs[:n], refs[n:2 * n], refs[2 * n:]):
            own = s_ref[0]
            total = None
            for k in range(N_CHIPS):
                theirs = p_ref[jnp.maximum(jnp.bitwise_xor(me, k) - 1, 0)].astype(F32)
                term = jnp.where(me == k, own, theirs)
                total = term if total is None else total + term
            o_ref[...] = total

    grid_spec = pltpu.PrefetchScalarGridSpec(
        num_scalar_prefetch=1, grid=(2,),
        in_specs=[pl.BlockSpec((1, p.shape[1] // 2, p.shape[2]), lambda i, me_ref: (me_ref[0], i, 0)) for p in parts]
        + [pl.BlockSpec((3, p.shape[1] // 2, p.shape[2]), lambda i, me_ref: (0, i, 0)) for p in parts],
        out_specs=[pl.BlockSpec((p.shape[1] // 2, p.shape[2]), lambda i, me_ref: (i, 0)) for p in parts])
    me = (2 * lax.axis_index("x") + lax.axis_index("y")).astype(jnp.int32).reshape(1)
    return pl.pallas_call(
        body, name="rs_add_chips", grid_spec=grid_spec,
        out_shape=[jax.ShapeDtypeStruct(p.shape[1:], F32) for p in parts],
        compiler_params=_cparams(("parallel",)),
    )(me, *chipsum, *parts)


def _rs_swap_halves(halves):
    n = len(halves)

    def body(*refs):
        h_refs, out_refs = refs[:n], refs[n:2 * n]
        send_sems, recv_sems = refs[2 * n:]
        x, y, c = _place()
        copies = [pltpu.make_async_remote_copy(src_ref=h_ref, dst_ref=out_ref, send_sem=send_sems.at[a], recv_sem=recv_sems.at[a],
                                               device_id=(x, y, 1 - c), device_id_type=MESH_ID)
                  for a, (h_ref, out_ref) in enumerate(zip(h_refs, out_refs))]
        for cp in copies:
            cp.start()
        for cp in copies:
            cp.wait()

    return pl.pallas_call(
        body, name="rs_swap_halves", in_specs=[HBM_SPEC] * n, out_specs=[HBM_SPEC] * n,
        out_shape=[jax.ShapeDtypeStruct(h.shape, F32) for h in halves],
        scratch_shapes=[pltpu.SemaphoreType.DMA((n,)), pltpu.SemaphoreType.DMA((n,))],
    )(*halves)


def _adamw_list(ws, g_mine, g_theirs, ms, vs):
    n = len(ws)

    def body(c_ref, *refs):
        w_refs, gm_refs, gt_refs, m_refs, v_refs = (refs[k * n:(k + 1) * n] for k in range(5))
        g_refs, d_refs, nm_refs, nv_refs = (refs[k * n:(k + 1) * n] for k in range(5, 9))
        mine = (pl.program_id(0) // ADAM_STEPS_PER_HALF) == c_ref[0]
        for a in range(n):
            gv = jnp.where(mine, gm_refs[a][...], gt_refs[a][...])
            g_refs[a][...] = gv
            m_new = ADAM_B1 * m_refs[a][...] + (1.0 - ADAM_B1) * gv
            v_new = ADAM_B2 * v_refs[a][...] + (1.0 - ADAM_B2) * (gv * gv)
            m_hat = m_new / (1.0 - ADAM_B1 ** ADAM_STEP)
            v_hat = v_new / (1.0 - ADAM_B2 ** ADAM_STEP)
            d_refs[a][...] = -ADAM_LR * (m_hat / (jnp.sqrt(v_hat) + ADAM_EPS) + ADAM_WD * w_refs[a][...])
            nm_refs[a][...] = m_new
            nv_refs[a][...] = v_new

    steps = 2 * ADAM_STEPS_PER_HALF
    whole = [pl.BlockSpec((w.shape[0] // steps, w.shape[1]), lambda i, c_ref: (i, 0)) for w in ws]
    half = [pl.BlockSpec((w.shape[0] // steps, w.shape[1]), lambda i, c_ref: (i % ADAM_STEPS_PER_HALF, 0)) for w in ws]
    shapes = [jax.ShapeDtypeStruct(w.shape, F32) for w in ws]
    grid_spec = pltpu.PrefetchScalarGridSpec(num_scalar_prefetch=1, grid=(steps,),
                                             in_specs=whole + half + half + whole + whole, out_specs=whole * 4)
    res = pl.pallas_call(
        body, name="adamw", grid_spec=grid_spec, out_shape=shapes * 4,
        compiler_params=_cparams(("parallel",)),
    )(lax.axis_index("c").astype(jnp.int32).reshape(1), *ws, *g_mine, *g_theirs, *ms, *vs)
    return res[:n], res[n:2 * n], res[2 * n:3 * n], res[3 * n:]


WEIGHT_NAMES = ("w_in", "w_mem_kv", "q_a_gain", "w_q_b", "kv_a_gain", "w_kv_b", "w_branch_mla", "w_branch_sb",
                "w_branch_mem", "w_merge_gate", "b_merge_gate", "w_out", "ln_gain", "ln_bias")
SMALL_NAMES = tuple(n for n, _ in SMALL_SIZES)


def kernel(x, mem, w_in, w_mem_kv, q_a_gain, w_q_b, kv_a_gain, w_kv_b, w_branch_mla, w_branch_sb, w_branch_mem, w_merge_gate, b_merge_gate, w_out, ln_gain, ln_bias, loss_target, m_w_in, m_w_mem_kv, m_q_a_gain, m_w_q_b, m_kv_a_gain, m_w_kv_b, m_w_branch_mla, m_w_branch_sb, m_w_branch_mem, m_w_merge_gate, m_b_merge_gate, m_w_out, m_ln_gain, m_ln_bias, v_w_in, v_w_mem_kv, v_q_a_gain, v_w_q_b, v_kv_a_gain, v_w_kv_b, v_w_branch_mla, v_w_branch_sb, v_w_branch_mem, v_w_merge_gate, v_b_merge_gate, v_w_out, v_ln_gain, v_ln_bias):
    weights = dict(zip(WEIGHT_NAMES, (w_in, w_mem_kv, q_a_gain, w_q_b, kv_a_gain, w_kv_b, w_branch_mla, w_branch_sb,
                                      w_branch_mem, w_merge_gate, b_merge_gate, w_out, ln_gain, ln_bias)))
    mom1 = dict(zip(WEIGHT_NAMES, (m_w_in, m_w_mem_kv, m_q_a_gain, m_w_q_b, m_kv_a_gain, m_w_kv_b, m_w_branch_mla,
                                   m_w_branch_sb, m_w_branch_mem, m_w_merge_gate, m_b_merge_gate, m_w_out, m_ln_gain,
                                   m_ln_bias)))
    mom2 = dict(zip(WEIGHT_NAMES, (v_w_in, v_w_mem_kv, v_q_a_gain, v_w_q_b, v_kv_a_gain, v_w_kv_b, v_w_branch_mla,
                                   v_w_branch_sb, v_w_branch_mem, v_w_merge_gate, v_b_merge_gate, v_w_out, v_ln_gain,
                                   v_ln_bias)))
    def as_list(d):
        return [d[n][0] for n in BIG_NAMES] + [_pack_small({n: d[n] for n in SMALL_NAMES})]

    w_list, m_list, v_list = as_list(weights), as_list(mom1), as_list(mom2)

    gathered = _gather_shards([weights[n][0] for n in LATE_NAMES])
    first_w = {n: _join_chips(n, g) for n, g in zip(LATE_NAMES, gathered)}
    rest_shards = _cast_bf16_list([weights[n][0] for n in EARLY_NAMES])
    small = {n: weights[n] for n in SMALL_NAMES}

    seq = x.shape[1]
    loss, grad_x, late_mine, _, early_mine = _local_step(
        x[0], mem[0], loss_target[0], first_w, small, tq=min(1024, seq), tq_sb_bwd=512, tk=256, tk_mla=512, t_row=256, t_mm=512,
        t_wg=min(2048, seq), rest_shards=rest_shards)
    by_name = dict(zip(EARLY_NAMES + LATE_NAMES + ("small",), list(early_mine) + list(late_mine)))
    mine = [by_name[n] for n in BIG_NAMES + ("small",)]
    theirs = _rs_swap_halves(mine)
    g_list, d_list, nm_list, nv_list = _adamw_list(w_list, mine, theirs, m_list, v_list)

    loss = lax.psum(loss, ("x", "y", "c"))
    outs = [loss, grad_x[None]]
    for arrays in (g_list, d_list, nm_list, nv_list):
        big = dict(zip(BIG_NAMES, arrays[:-1]))
        sm = _unpack_small(arrays[-1])
        outs.extend(big[n][None] if n in big else sm[n] for n in WEIGHT_NAMES)
    return tuple(outs)
```

```python
import functools
import math

import numpy as np
import jax
import jax.numpy as jnp
from jax import lax
from jax.experimental import pallas as pl
from jax.experimental.pallas import tpu as pltpu

F32 = jnp.float32
BF16 = jnp.bfloat16
MESH_ID = pl.DeviceIdType.MESH

D_MODEL = 1024
MEM_LEN = 256
MLA_HEADS = 8
MLA_NOPE = 64
MLA_ROPE = 32
MLA_V = 64
MLA_Q_LORA = 256
MLA_KV_LORA = 128
SB_HEADS = 8
SB_HEAD_DIM = 64
MEM_HEADS = 4
MEM_HEAD_DIM = 128
ROPE_BASE = 10000.0
RMS_EPS = 1e-6
LN_EPS = 1e-5
DEEPNORM_ALPHA = 2.0 ** 0.25
MLA_SCALE = 1.0 / math.sqrt(MLA_NOPE + MLA_ROPE)
SB_SCALE = 1.0 / math.sqrt(SB_HEAD_DIM)
MEM_SCALE = 1.0 / math.sqrt(MEM_HEAD_DIM)

ADAM_LR = 0.001
ADAM_B1 = 0.9
ADAM_B2 = 0.999
ADAM_EPS = 1e-08
ADAM_WD = 0.01
ADAM_STEP = 10

LANES = 128
HALF = 64
N_CHIPS = 4
PACK_COLS = 1024
VMEM_LIMIT = 56 * 1024 * 1024

IN_WIDTH_P = 4096
BLK_LAT, BLK_GATE_A, BLK_QB, BLK_KB, BLK_VB, BLK_GATE_B, BLK_QM, BLK_GATE_M = range(8)
N_MERGE = 3 * D_MODEL

BIG_NAMES = ("w_in", "w_mem_kv", "w_q_b", "w_kv_b", "w_branch_mla", "w_branch_sb", "w_branch_mem", "w_merge_gate", "w_out")
COL_SHARDED = ("w_in", "w_q_b", "w_kv_b", "w_branch_mla", "w_branch_sb", "w_branch_mem", "w_merge_gate")
SMALL_SIZES = (("q_a_gain", 256), ("kv_a_gain", 128), ("b_merge_gate", 3072), ("ln_gain", 1024), ("ln_bias", 1024))
SMALL_TOTAL = sum(s for _, s in SMALL_SIZES)


def _cparams(sem=None):
    return pltpu.CompilerParams(dimension_semantics=sem, vmem_limit_bytes=VMEM_LIMIT)


def _dot(a, b):
    return jnp.dot(a, b, preferred_element_type=F32)


def _dot_nt(a, b):
    return lax.dot_general(a, b, (((1,), (1,)), ((), ())), preferred_element_type=F32)


def _dot_tn(a, b):
    return lax.dot_general(a, b, (((0,), (0,)), ((), ())), preferred_element_type=F32)


def _bf(x):
    return x.astype(BF16)


def _dot_cols(a, w_ref):
    return jnp.concatenate([_dot(a, w_ref[j]) for j in range(w_ref.shape[0])], axis=1)


def _dot_nt_cols(a, w_ref):
    cs = w_ref.shape[2]
    out = None
    for j in range(w_ref.shape[0]):
        term = _dot_nt(a[:, j * cs:(j + 1) * cs], w_ref[j])
        out = term if out is None else out + term
    return out


def _sigmoid(x):
    return 1.0 / (1.0 + jnp.exp(-x))


def _matmul(a, b, *, mode, tm, tn, tk, out_dtypes, name, add=None, add_scale=1.0, by_column_block=False):
    if mode == "nn":
        (m, k), n = a.shape, b.shape[1]
        a_spec = pl.BlockSpec((tm, tk), lambda i, j, kk: (i, kk))
        b_spec = pl.BlockSpec((tk, tn), lambda i, j, kk: (kk, j))
        dot = _dot
    elif mode == "nt":
        (m, k), n = a.shape, b.shape[0]
        a_spec = pl.BlockSpec((tm, tk), lambda i, j, kk: (i, kk))
        b_spec = pl.BlockSpec((tn, tk), lambda i, j, kk: (j, kk))
        dot = _dot_nt
    else:
        (k, m), n = a.shape, b.shape[1]
        a_spec = pl.BlockSpec((tk, tm), lambda i, j, kk: (kk, i))
        b_spec = pl.BlockSpec((tk, tn), lambda i, j, kk: (kk, j))
        dot = _dot_tn
    assert m % tm == 0 and n % tn == 0 and k % tk == 0, (name, m, n, k)
    nk = k // tk
    n_out = len(out_dtypes)
    has_add = add is not None

    def body(*refs):
        a_ref, b_ref = refs[0], refs[1]
        add_ref = refs[2] if has_add else None
        outs = refs[2 + has_add: 2 + has_add + n_out]
        acc = refs[-1]
        kk = pl.program_id(2)

        @pl.when(kk == 0)
        def _():
            acc[...] = jnp.zeros_like(acc)

        acc[...] += dot(_bf(a_ref[...]), _bf(b_ref[...]))

        @pl.when(kk == nk - 1)
        def _():
            r = acc[...]
            if has_add:
                r = r + add_scale * add_ref[...]
            for o in outs:
                o[...] = r.astype(o.dtype)

    in_specs = [a_spec, b_spec]
    args = [a, b]
    if has_add:
        in_specs.append(pl.BlockSpec((tm, tn), lambda i, j, kk: (i, j)))
        args.append(add)
    if by_column_block:
        out_spec = pl.BlockSpec((None, tm, tn), lambda i, j, kk: (j, i, 0))
        out_dims = (n // tn, m, tn)
    else:
        out_spec = pl.BlockSpec((tm, tn), lambda i, j, kk: (i, j))
        out_dims = (m, n)
    res = pl.pallas_call(
        body, name=name, grid=(m // tm, n // tn, nk),
        in_specs=in_specs, out_specs=[out_spec] * n_out,
        out_shape=[jax.ShapeDtypeStruct(out_dims, dt) for dt in out_dtypes],
        scratch_shapes=[pltpu.VMEM((tm, tn), F32)],
        compiler_params=_cparams(("parallel", "parallel", "arbitrary")),
    )(*args)
    return res


def _rope_tables(seq):
    half = MLA_ROPE // 2
    freqs = ROPE_BASE ** (-jnp.arange(half, dtype=F32) / half)
    ang = jnp.arange(seq, dtype=jnp.int32).astype(F32)[:, None] * freqs[None, :]
    cos, sin = jnp.cos(ang), jnp.sin(ang)
    z = lambda w: jnp.zeros((seq, w), F32)
    c_q = jnp.concatenate([jnp.ones((seq, MLA_NOPE), F32), cos, cos, z(32)], axis=1)
    c_k = jnp.concatenate([z(MLA_NOPE), cos, cos, z(32)], axis=1)
    s_lo = jnp.concatenate([z(MLA_NOPE), -sin, z(half), z(32)], axis=1)
    s_hi = jnp.concatenate([z(MLA_NOPE), z(half), sin, z(32)], axis=1)
    return c_q, c_k, s_lo, s_hi


def _rope_fwd(x, c, s_lo, s_hi):
    return x * c + pltpu.roll(x, LANES - 16, 1) * s_lo + pltpu.roll(x, 16, 1) * s_hi


def _rope_bwd(d, c, s_lo, s_hi):
    return d * c - pltpu.roll(d, 16, 1) * s_hi - pltpu.roll(d, LANES - 16, 1) * s_lo


def _rms_fwd(x, g):
    r = lax.rsqrt(jnp.mean(x * x, axis=-1, keepdims=True) + RMS_EPS)
    xn = x * r
    return xn * g, xn, r


def _mla_prep(p32, gq, gkv, wqb, wkvb, tabs, *, t):
    seq = p32.shape[0]

    def body(lat_ref, gq_ref, gkv_ref, wqb_ref, wkvb_ref, cq_ref, ck_ref, slo_ref, shi_ref, q_ref, k_ref, v_ref):
        lat = lat_ref[...]
        slo, shi = slo_ref[...], shi_ref[...]
        nq, _, _ = _rms_fwd(lat[:, 0:MLA_Q_LORA], gq_ref[...])
        qa = _dot(_bf(nq), wqb_ref[...])
        cq = cq_ref[...]
        for h in range(MLA_HEADS):
            blk = qa[:, h * LANES:(h + 1) * LANES]
            q_ref[:, h * LANES:(h + 1) * LANES] = _bf(_rope_fwd(blk, cq, slo, shi))
        nkv, _, _ = _rms_fwd(lat[:, MLA_Q_LORA:MLA_Q_LORA + MLA_KV_LORA], gkv_ref[...])
        kv = _dot(_bf(nkv), wkvb_ref[...])
        kpe = _rope_fwd(lat[:, 384:512], ck_ref[...], slo, shi)
        for h in range(MLA_HEADS):
            k_ref[:, h * LANES:(h + 1) * LANES] = _bf(kv[:, h * LANES:(h + 1) * LANES] + kpe)
        v_ref[...] = _bf(kv[:, MLA_HEADS * LANES:])

    row = lambda w: pl.BlockSpec((t, w), lambda i: (i, 0))
    full = lambda shp: pl.BlockSpec(shp, lambda i: (0, 0))
    return pl.pallas_call(
        body, name="mla_prep", grid=(seq // t,),
        in_specs=[row(512), full((1, MLA_Q_LORA)), full((1, MLA_KV_LORA)), full(wqb.shape), full(wkvb.shape),
                  row(LANES), row(LANES), row(LANES), row(LANES)],
        out_specs=[row(1024), row(1024), row(512)],
        out_shape=[jax.ShapeDtypeStruct((seq, 1024), BF16), jax.ShapeDtypeStruct((seq, 1024), BF16),
                   jax.ShapeDtypeStruct((seq, 512), BF16)],
        compiler_params=_cparams(("parallel",)),
    )(p32, gq, gkv, wqb, wkvb, *tabs)


def _mla_post(p32, dq, dk, dv, gq, gkv, wqb, wkvb, tabs, *, t):
    seq = p32.shape[0]

    def body(lat_ref, dq_ref, dk_ref, dv_ref, gq_ref, gkv_ref, wqb_ref, wkvb_ref, cq_ref, ck_ref, slo_ref, shi_ref,
             dlat_ref, dwqb_ref, dwkvb_ref, dgq_ref, dgkv_ref):
        @pl.when(pl.program_id(0) == 0)
        def _():
            dwqb_ref[...] = jnp.zeros_like(dwqb_ref)
            dwkvb_ref[...] = jnp.zeros_like(dwkvb_ref)
            dgq_ref[...] = jnp.zeros_like(dgq_ref)
            dgkv_ref[...] = jnp.zeros_like(dgkv_ref)

        lat = lat_ref[...]
        slo, shi = slo_ref[...], shi_ref[...]
        cq = cq_ref[...]
        gq_v, gkv_v = gq_ref[...], gkv_ref[...]
        nq, xq, rq = _rms_fwd(lat[:, 0:MLA_Q_LORA], gq_v)
        nkv, xkv, rkv = _rms_fwd(lat[:, MLA_Q_LORA:MLA_Q_LORA + MLA_KV_LORA], gkv_v)

        dqa = jnp.concatenate(
            [_rope_bwd(dq_ref[:, h * LANES:(h + 1) * LANES], cq, slo, shi) for h in range(MLA_HEADS)], axis=1)
        dqa_b = _bf(dqa)
        dwqb_ref[...] += _dot_tn(_bf(nq), dqa_b)
        dnq = _dot_nt(dqa_b, wqb_ref[...])
        dgq_ref[...] += jnp.sum(dnq * xq, axis=0, keepdims=True)
        dxn = dnq * gq_v
        dcq = rq * (dxn - xq * jnp.mean(dxn * xq, axis=-1, keepdims=True))

        dkf = dk_ref[...]
        dkv_b = _bf(jnp.concatenate([dkf, dv_ref[...]], axis=1))
        dwkvb_ref[...] += _dot_tn(_bf(nkv), dkv_b)
        dnkv = _dot_nt(dkv_b, wkvb_ref[...])
        dgkv_ref[...] += jnp.sum(dnkv * xkv, axis=0, keepdims=True)
        dxn = dnkv * gkv_v
        dckv = rkv * (dxn - xkv * jnp.mean(dxn * xkv, axis=-1, keepdims=True))

        dkpe = dkf[:, 0:LANES]
        for h in range(1, MLA_HEADS):
            dkpe = dkpe + dkf[:, h * LANES:(h + 1) * LANES]
        dkr = _rope_bwd(dkpe, ck_ref[...], slo, shi)
        dlat_ref[...] = _bf(jnp.concatenate([dcq, dckv, dkr], axis=1))

    row = lambda w: pl.BlockSpec((t, w), lambda i: (i, 0))
    full = lambda shp: pl.BlockSpec(shp, lambda i: (0, 0))
    return pl.pallas_call(
        body, name="mla_post", grid=(seq // t,),
        in_specs=[row(512), row(1024), row(1024), row(512), full((1, MLA_Q_LORA)), full((1, MLA_KV_LORA)),
                  full(wqb.shape), full(wkvb.shape), row(LANES), row(LANES), row(LANES), row(LANES)],
        out_specs=[row(512), full(wqb.shape), full(wkvb.shape), full((1, MLA_Q_LORA)), full((1, MLA_KV_LORA))],
        out_shape=[jax.ShapeDtypeStruct((seq, 512), BF16), jax.ShapeDtypeStruct(wqb.shape, F32),
                   jax.ShapeDtypeStruct(wkvb.shape, F32), jax.ShapeDtypeStruct((1, MLA_Q_LORA), F32),
                   jax.ShapeDtypeStruct((1, MLA_KV_LORA), F32)],
        compiler_params=_cparams(("arbitrary",)),
    )(p32, dq, dk, dv, gq, gkv, wqb, wkvb, *tabs)


def _split_bf16(x):
    hi = _bf(x)
    return hi, _bf(x - hi.astype(F32))


def _tri_sum(x, u):
    hi, lo = _split_bf16(x)
    return _dot(hi, u) + _dot(lo, u)


def _softplus(z):
    return jnp.maximum(z, 0.0) + jnp.log(1.0 + jnp.exp(-jnp.abs(z)))


def _head_queries(q, left):
    zero = jnp.zeros_like(q)
    return jnp.where(left, q, zero) * SB_SCALE, jnp.where(left, zero, q) * SB_SCALE


ROW_GROUP = 128
SB_BWD_CHAINS_IN_FLIGHT = 8
ANY_HBM = pl.BlockSpec(memory_space=pltpu.HBM)


class _Exchange:
    def __init__(self, send, landing):
        self.send, self.landing = send, landing

    def start(self):
        self.send.start()

    def wait(self):
        self.landing.wait_recv()
        self.send.wait_send()


class _Rider:
    def __init__(self, operands, out_shapes, sem_shapes, copies):
        self.operands, self.out_shapes, self.sem_shapes, self.copies = list(operands), list(out_shapes), list(sem_shapes), copies


def _call_with_rider(body, rider, *, name, grid, in_specs, out_specs, out_shape, args, semantics, scratch=()):
    scratch = list(scratch)
    if rider is None:
        return pl.pallas_call(body, name=name, grid=grid, in_specs=in_specs, out_specs=out_specs, out_shape=out_shape,
                              scratch_shapes=scratch, compiler_params=_cparams(semantics))(*args)
    n_in, n_out, n_rin, n_rout = len(in_specs), len(out_specs), len(rider.operands), len(rider.out_shapes)

    def full_body(*refs):
        ins, r_ins = refs[:n_in], refs[n_in:n_in + n_rin]
        outs = refs[n_in + n_rin:n_in + n_rin + n_out]
        r_outs = refs[n_in + n_rin + n_out:n_in + n_rin + n_out + n_rout]
        rest = refs[n_in + n_rin + n_out + n_rout:]
        own_scratch, sems = rest[:len(scratch)], rest[len(scratch):]
        first, last = None, None
        for axis, size in enumerate(grid):
            at_start, at_end = pl.program_id(axis) == 0, pl.program_id(axis) == size - 1
            first = at_start if first is None else first & at_start
            last = at_end if last is None else last & at_end

        @pl.when(first)
        def _():
            for cp in rider.copies(r_ins, r_outs, sems):
                cp.start()

        body(*ins, *outs, *own_scratch)

        @pl.when(last)
        def _():
            for cp in rider.copies(r_ins, r_outs, sems):
                cp.wait()

    return pl.pallas_call(
        full_body, name=name, grid=grid, in_specs=list(in_specs) + [ANY_HBM] * n_rin,
        out_specs=list(out_specs) + [ANY_HBM] * n_rout, out_shape=list(out_shape) + rider.out_shapes,
        scratch_shapes=scratch + rider.sem_shapes, compiler_params=_cparams(("arbitrary",) * len(grid)),
    )(*args, *rider.operands)


def _chains(tq):
    return [(h, g) for g in range(tq // ROW_GROUP) for h in range(2)]


def _chain_pattern(g, m, tk, strict):
    r_lo, r_hi = g * ROW_GROUP, (g + 1) * ROW_GROUP - 1
    c_lo, c_hi = m * tk, (m + 1) * tk - 1
    if (c_lo >= r_hi) if strict else (c_lo > r_hi):
        return None
    if (c_hi < r_lo) if strict else (c_hi <= r_lo):
        return True
    rr = lax.broadcasted_iota(jnp.int32, (ROW_GROUP, tk), 0) + r_lo
    cc = lax.broadcasted_iota(jnp.int32, (ROW_GROUP, tk), 1) + c_lo
    return (cc < rr) if strict else (cc <= rr)


def _masked(x, pat, fill=0.0):
    return x if pat is True else jnp.where(pat, x, fill)


def _rows(g):
    return slice(g * ROW_GROUP, (g + 1) * ROW_GROUP)


def _tri_matrix(tk, cmp):
    rr = lax.broadcasted_iota(jnp.int32, (tk, tk), 0)
    cc = lax.broadcasted_iota(jnp.int32, (tk, tk), 1)
    return cmp(rr, cc).astype(BF16)


def _mla_attn_fwd(qp, kp, vp, *, tq, tk, tk_diag, rider=None):
    seq = qp.shape[0]
    neg = float(np.finfo(np.float32).min)
    chains = _chains(tq)

    def body(q_ref, k_ref, v_ref, o_ref, lse_ref):
        i = pl.program_id(1)
        left = lax.broadcasted_iota(jnp.int32, (tq, LANES), 1) < HALF
        qs = [q_ref[_rows(g), h * LANES:(h + 1) * LANES] for h, g in chains]

        def block(start, carry, m, tk):
            v = v_ref[pl.ds(start, tk), :]
            pats = [True if m is None else _chain_pattern(g, m, tk, False) for _, g in chains]
            live = [n for n, p in enumerate(pats) if p is not None]
            ss = {n: _dot_nt(qs[n], k_ref[pl.ds(start, tk), chains[n][0] * LANES:(chains[n][0] + 1) * LANES]) for n in live}
            new = list(carry)
            for n in live:
                m_old, l_old, acc = carry[n]
                s = _masked(ss[n] * MLA_SCALE, pats[n], neg)
                m_new = jnp.maximum(m_old, jnp.max(s, axis=-1, keepdims=True))
                a = jnp.exp(m_old - m_new)
                p = jnp.exp(s - m_new)
                new[n] = (m_new, a * l_old + jnp.sum(p, axis=-1, keepdims=True), a * acc + _dot(_bf(p), v))
            return tuple(new)

        init = (jnp.full((ROW_GROUP, 1), -1e30, F32), jnp.zeros((ROW_GROUP, 1), F32), jnp.zeros((ROW_GROUP, LANES), F32))
        def two_blocks(j, c):
            c = block(pl.multiple_of(2 * j * tk, tk), c, None, tk)
            return block(pl.multiple_of((2 * j + 1) * tk, tk), c, None, tk)

        carry = lax.fori_loop(0, i * (tq // tk) // 2, two_blocks, (init,) * len(chains))
        for m in range(tq // tk_diag):
            carry = block(pl.multiple_of(i * tq + m * tk_diag, tk_diag), carry, m, tk_diag)
        per_head = []
        for h in range(2):
            mine = [carry[n] for n, (ch, _) in enumerate(chains) if ch == h]
            per_head.append((jnp.concatenate([acc / l for _, l, acc in mine], axis=0),
                             jnp.concatenate([mm + jnp.log(l) for mm, l, _ in mine], axis=0)))
        o_ref[...] = jnp.where(left, per_head[0][0], per_head[1][0])
        lse_ref[...] = jnp.where(left, per_head[0][1], per_head[1][1])

    return _call_with_rider(
        body, rider, name="mla_fwd", grid=(MLA_HEADS // 2, seq // tq),
        in_specs=[pl.BlockSpec((tq, 2 * LANES), lambda p, i: (i, p)), pl.BlockSpec((seq, 2 * LANES), lambda p, i: (0, p)),
                  pl.BlockSpec((seq, LANES), lambda p, i: (0, p))],
        out_specs=[pl.BlockSpec((tq, LANES), lambda p, i: (i, p)), pl.BlockSpec((tq, LANES), lambda p, i: (i, p))],
        out_shape=[jax.ShapeDtypeStruct((seq, 512), F32), jax.ShapeDtypeStruct((seq, 512), F32)],
        args=(qp, kp, vp), semantics=("parallel", "parallel"))


def _mla_attn_bwd(qp, kp, vp, o, lse, do, *, tq, tk, tk_diag, rider=None):
    seq = qp.shape[0]
    chains = _chains(tq)

    def body(q_ref, k_ref, v_ref, o_ref, lse_ref, do_ref, dq_ref, dk_ref, dv_ref, qt_ref, dot_ref):
        i = pl.program_id(1)

        @pl.when(i == 0)
        def _():
            dk_ref[...] = jnp.zeros_like(dk_ref)
            dv_ref[...] = jnp.zeros_like(dv_ref)

        left = lax.broadcasted_iota(jnp.int32, (tq, LANES), 1) < HALF
        do_f = do_ref[...]
        prod = do_f * o_ref[...]
        lse_v = lse_ref[...]
        do_heads = (_bf(jnp.where(left, do_f, 0.0)), _bf(jnp.where(left, 0.0, do_f)))
        delta_heads = (jnp.sum(jnp.where(left, prod, 0.0), axis=-1, keepdims=True),
                       jnp.sum(jnp.where(left, 0.0, prod), axis=-1, keepdims=True))
        qs = [q_ref[_rows(g), h * LANES:(h + 1) * LANES] for h, g in chains]
        dos = [do_heads[h][_rows(g)] for h, g in chains]
        deltas = [delta_heads[h][_rows(g)] for h, g in chains]
        lses = [lse_v[_rows(g), h * HALF:h * HALF + 1] for h, g in chains]
        for h in range(2):
            qt_ref[h] = q_ref[:, h * LANES:(h + 1) * LANES].T
            dot_ref[h] = do_heads[h].T
        q_t = [qt_ref.at[h] for h in range(2)]
        do_t = [dot_ref.at[h] for h in range(2)]

        def block(start, carry, m, tk):
            v = v_ref[pl.ds(start, tk), :]
            pats = [True if m is None else _chain_pattern(g, m, tk, False) for _, g in chains]
            live = [n for n, p in enumerate(pats) if p is not None]
            ks = [k_ref[pl.ds(start, tk), h * LANES:(h + 1) * LANES] for h in range(2)]
            ss = {n: _dot_nt(qs[n], ks[chains[n][0]]) for n in live}
            dps = {n: _dot_nt(dos[n], v) for n in live}
            new = list(carry)
            ps, dss = {}, {}
            for n in live:
                p = _masked(jnp.exp(ss[n] * MLA_SCALE - lses[n]), pats[n])
                ps[n] = _bf(p)
                dss[n] = _bf(p * (dps[n] - deltas[n]) * MLA_SCALE)
                new[n] = carry[n] + _dot(dss[n], ks[chains[n][0]])
            dv_t, dk_t = None, []
            for h in range(2):
                mine = [n for n in live if chains[n][0] == h]
                first_row = chains[mine[0]][1] * ROW_GROUP
                ds_cat = jnp.concatenate([dss[n] for n in mine], axis=0)
                p_cat = jnp.concatenate([ps[n] for n in mine], axis=0)
                if first_row == 0:
                    q_rows_t, do_rows_t = q_t[h][...], do_t[h][...]
                else:
                    q_rows_t = q_ref[first_row:, h * LANES:(h + 1) * LANES].T
                    do_rows_t = do_heads[h][first_row:].T
                dk_t.append(_dot(q_rows_t, ds_cat))
                term = _dot(do_rows_t, p_cat)
                dv_t = term if dv_t is None else dv_t + term
            back = jnp.concatenate(dk_t + [dv_t], axis=0).T
            dk_ref[pl.ds(start, tk), :] += back[:, :2 * LANES]
            dv_ref[pl.ds(start, tk), :] += back[:, 2 * LANES:]
            return tuple(new)

        zero = jnp.zeros((ROW_GROUP, LANES), F32)
        carry = lax.fori_loop(0, i * (tq // tk), lambda j, c: block(pl.multiple_of(j * tk, tk), c, None, tk),
                              (zero,) * len(chains))
        for m in range(tq // tk_diag):
            carry = block(pl.multiple_of(i * tq + m * tk_diag, tk_diag), carry, m, tk_diag)
        for n, (h, g) in enumerate(chains):
            dq_ref[_rows(g), h * LANES:(h + 1) * LANES] = carry[n]

    two_t = pl.BlockSpec((tq, 2 * LANES), lambda p, i: (i, p))
    two_s = pl.BlockSpec((seq, 2 * LANES), lambda p, i: (0, p))
    pair_t = pl.BlockSpec((tq, LANES), lambda p, i: (i, p))
    pair_s = pl.BlockSpec((seq, LANES), lambda p, i: (0, p))
    return _call_with_rider(
        body, rider, name="mla_bwd", grid=(MLA_HEADS // 2, seq // tq),
        in_specs=[two_t, two_s, pair_s, pair_t, pair_t, pair_t],
        out_specs=[two_t, two_s, pair_s],
        out_shape=[jax.ShapeDtypeStruct((seq, 1024), F32), jax.ShapeDtypeStruct((seq, 1024), F32),
                   jax.ShapeDtypeStruct((seq, 512), F32)],
        args=(qp, kp, vp, o, lse, do), semantics=("parallel", "arbitrary"),
        scratch=[pltpu.VMEM((2, LANES, tq), BF16), pltpu.VMEM((2, LANES, tq), BF16)])


def _sb_attn_fwd(pbf, *, tq, tk):
    seq = pbf.shape[0]
    nd = tq // tk
    qb, kb, vb = BLK_QB * 4, BLK_KB * 4, BLK_VB * 4
    chains = _chains(tq)

    def body(q_ref, k_ref, v_ref, o_ref, tot_ref):
        i = pl.program_id(1)
        u_later = _tri_matrix(tk, lambda r, c: r > c)
        left = lax.broadcasted_iota(jnp.int32, (tq, LANES), 1) < HALF
        q_heads = _head_queries(q_ref[...], left)
        qs = [q_heads[h][_rows(g)] for h, g in chains]

        def block(j, carry, m):
            start = pl.multiple_of(j * tk, tk)
            k = k_ref[pl.ds(start, tk), :]
            v = v_ref[pl.ds(start, tk), :]
            pats = [True if m is None else _chain_pattern(g, m, tk, True) for _, g in chains]
            live = [n for n, p in enumerate(pats) if p is not None]
            zs = {n: _dot_nt(qs[n], k) for n in live}
            raws = {n: _softplus(zs[n]) for n in live}
            sps = {n: _masked(raws[n], pats[n]) for n in live}
            laters = {n: _tri_sum(sps[n], u_later) for n in live}
            new = list(carry)
            for n in live:
                c, acc = carry[n]
                a = _masked(jnp.exp(zs[n] - raws[n] - laters[n] - c), pats[n])
                new[n] = (c + laters[n][:, 0:1] + sps[n][:, 0:1], acc + _dot(_bf(a), v))
            return tuple(new)

        init = (jnp.zeros((ROW_GROUP, 1), F32), jnp.zeros((ROW_GROUP, LANES), F32))
        carry = (init,) * len(chains)
        for m in reversed(range(nd)):
            carry = block(i * nd + m, carry, m)
        per_trip = 4 if nd % 4 == 0 else 2

        def trip(jj, cr):
            for u in range(per_trip):
                cr = block(i * nd - 1 - (per_trip * jj + u), cr, None)
            return cr

        carry = lax.fori_loop(0, i * nd // per_trip, trip, carry)
        per_head = []
        for h in range(2):
            mine = [carry[n] for n, (ch, _) in enumerate(chains) if ch == h]
            per_head.append((jnp.concatenate([acc for _, acc in mine], axis=0), jnp.concatenate([c for c, _ in mine], axis=0)))
        o_ref[...] = jnp.where(left, per_head[0][0], per_head[1][0])
        tot_ref[...] = jnp.where(left, per_head[0][1], per_head[1][1])

    pair_t = pl.BlockSpec((tq, LANES), lambda p, i: (i, p))
    return pl.pallas_call(
        body, name="sb_fwd", grid=(SB_HEADS // 2, seq // tq),
        in_specs=[pl.BlockSpec((tq, LANES), lambda p, i: (i, qb + p)), pl.BlockSpec((seq, LANES), lambda p, i: (0, kb + p)),
                  pl.BlockSpec((seq, LANES), lambda p, i: (0, vb + p))],
        out_specs=[pair_t, pair_t],
        out_shape=[jax.ShapeDtypeStruct((seq, 512), F32), jax.ShapeDtypeStruct((seq, 512), F32)],
        compiler_params=_cparams(("parallel", "parallel")),
    )(pbf, pbf, pbf)


def _sb_attn_bwd(pbf, tot, do, *, tq, tk, rider=None):
    seq = pbf.shape[0]
    nd = tq // tk
    qb, kb, vb = BLK_QB * 4, BLK_KB * 4, BLK_VB * 4
    chains = _chains(tq)
    group = SB_BWD_CHAINS_IN_FLIGHT

    def body(q_ref, k_ref, v_ref, tot_ref, do_ref, dq_ref, dk_ref, dv_ref, qt_ref, dot_ref):
        i = pl.program_id(1)

        @pl.when(i == 0)
        def _():
            dk_ref[...] = jnp.zeros_like(dk_ref)
            dv_ref[...] = jnp.zeros_like(dv_ref)

        u_upto = _tri_matrix(tk, lambda r, c: r <= c)
        u_below = _tri_matrix(tk, lambda r, c: r < c)
        left = lax.broadcasted_iota(jnp.int32, (tq, LANES), 1) < HALF
        q_heads = _head_queries(q_ref[...], left)
        do_f = do_ref[...]
        do_heads = (_bf(jnp.where(left, do_f, 0.0)), _bf(jnp.where(left, 0.0, do_f)))
        tot_v = tot_ref[...]
        qs = [q_heads[h][_rows(g)] for h, g in chains]
        dos = [do_heads[h][_rows(g)] for h, g in chains]
        totals = [tot_v[_rows(g), h * HALF:h * HALF + 1] for h, g in chains]
        qt_ref[...] = jnp.concatenate(qs, axis=0).T
        dot_ref[...] = jnp.concatenate(dos, axis=0).T

        def block(j, carry, m):
            start = pl.multiple_of(j * tk, tk)
            k = k_ref[pl.ds(start, tk), :]
            v = v_ref[pl.ds(start, tk), :]
            pats = [True if m is None else _chain_pattern(g, m, tk, True) for _, g in chains]
            all_live = [n for n, p in enumerate(pats) if p is not None]
            new = list(carry)
            for g0 in range(0, len(all_live), group):
                live = all_live[g0:g0 + group]
                zs = {n: _dot_nt(qs[n], k) for n in live}
                das = {n: _dot_nt(dos[n], v) for n in live}
                raws = {n: _softplus(zs[n]) for n in live}
                sps = {n: _masked(raws[n], pats[n]) for n in live}
                uptos = {n: _tri_sum(sps[n], u_upto) for n in live}
                lbs, a_s, gs = {}, {}, {}
                for n in live:
                    lbs[n] = zs[n] - raws[n]
                    a = _masked(jnp.exp(lbs[n] - (totals[n] - carry[n][0] - uptos[n])), pats[n])
                    a_s[n] = _bf(a)
                    gs[n] = das[n] * a
                belows = {n: _dot(_bf(gs[n]), u_below) for n in live}
                dzs = {}
                for n in live:
                    sp_before, g_before, dq_acc = carry[n]
                    beta = jnp.exp(lbs[n])
                    dz = _masked(gs[n] * (1.0 - beta) - (g_before + belows[n]) * beta, pats[n])
                    dzs[n] = _bf(dz)
                    new[n] = (sp_before + uptos[n][:, tk - 1:tk], g_before + belows[n][:, tk - 1:tk] + gs[n][:, tk - 1:tk],
                              dq_acc + _dot(dzs[n], k))
                dz_cat = jnp.concatenate([dzs[n] for n in live], axis=0)
                a_cat = jnp.concatenate([a_s[n] for n in live], axis=0)
                if len(live) == len(chains):
                    q_rows_t, do_rows_t = qt_ref[...], dot_ref[...]
                else:
                    q_rows_t = jnp.concatenate([qs[n] for n in live], axis=0).T
                    do_rows_t = jnp.concatenate([dos[n] for n in live], axis=0).T
                both = jnp.concatenate([_dot(q_rows_t, dz_cat), _dot(do_rows_t, a_cat)], axis=0).T
                dk_ref[pl.ds(start, tk), :] += both[:, :LANES]
                dv_ref[pl.ds(start, tk), :] += both[:, LANES:]
            return tuple(new)

        zero = jnp.zeros((ROW_GROUP, 1), F32)
        init = (zero, zero, jnp.zeros((ROW_GROUP, LANES), F32))
        carry = lax.fori_loop(0, i * nd // 2, lambda j, cr: block(2 * j + 1, block(2 * j, cr, None), None),
                              (init,) * len(chains))
        for m in range(nd):
            carry = block(i * nd + m, carry, m)
        per_head = [jnp.concatenate([carry[n][2] for n, (ch, _) in enumerate(chains) if ch == h], axis=0) for h in range(2)]
        dq_ref[...] = jnp.where(left, per_head[0], per_head[1]) * SB_SCALE

    pair_t = pl.BlockSpec((tq, LANES), lambda p, i: (i, p))
    pair_s = pl.BlockSpec((seq, LANES), lambda p, i: (0, p))
    return _call_with_rider(
        body, rider, name="sb_bwd", grid=(SB_HEADS // 2, seq // tq),
        in_specs=[pl.BlockSpec((tq, LANES), lambda p, i: (i, qb + p)), pl.BlockSpec((seq, LANES), lambda p, i: (0, kb + p)),
                  pl.BlockSpec((seq, LANES), lambda p, i: (0, vb + p)), pair_t, pair_t],
        out_specs=[pair_t, pair_s, pair_s],
        out_shape=[jax.ShapeDtypeStruct((seq, 512), F32)] * 3,
        args=(pbf, pbf, pbf, tot, do), semantics=("parallel", "arbitrary"),
        scratch=[pltpu.VMEM((LANES, 2 * tq), BF16), pltpu.VMEM((LANES, 2 * tq), BF16)])


def _mem_probs(s):
    e = jnp.exp(s - jnp.max(s, axis=-1, keepdims=True))
    return e / jnp.sum(e, axis=-1, keepdims=True)


def _head_lanes(h):
    return slice(h * LANES, (h + 1) * LANES)


def _mem_fwd(pbf, mkv, *, t):
    seq = pbf.shape[0]

    def body(q_ref, kv_ref, o_ref):
        ss = [_dot_nt(q_ref[:, _head_lanes(h)], kv_ref[:, _head_lanes(h)]) * MEM_SCALE for h in range(MEM_HEADS)]
        ps = [_bf(_mem_probs(s)) for s in ss]
        for h, p in enumerate(ps):
            o_ref[:, _head_lanes(h)] = _dot(p, kv_ref[:, _head_lanes(MEM_HEADS + h)])

    return pl.pallas_call(
        body, name="mem_fwd", grid=(seq // t,),
        in_specs=[pl.BlockSpec((t, 512), lambda i: (i, BLK_QM)), pl.BlockSpec((MEM_LEN, 1024), lambda i: (0, 0))],
        out_specs=pl.BlockSpec((t, 512), lambda i: (i, 0)),
        out_shape=jax.ShapeDtypeStruct((seq, 512), F32),
        compiler_params=_cparams(("parallel",)),
    )(pbf, mkv)


def _mem_bwd(pbf, mkv, do, *, t):
    seq = pbf.shape[0]

    def body(q_ref, kv_ref, do_ref, dq_ref, dkv_ref):
        @pl.when(pl.program_id(0) == 0)
        def _():
            dkv_ref[...] = jnp.zeros_like(dkv_ref)

        heads = range(MEM_HEADS)
        qs = [q_ref[:, _head_lanes(h)] for h in heads]
        ks = [kv_ref[:, _head_lanes(h)] for h in heads]
        dos = [_bf(do_ref[:, _head_lanes(h)]) for h in heads]
        ss = [_dot_nt(qs[h], ks[h]) * MEM_SCALE for h in heads]
        dps = [_dot_nt(dos[h], kv_ref[:, _head_lanes(MEM_HEADS + h)]) for h in heads]
        ps = [_mem_probs(s) for s in ss]
        dss = [_bf(ps[h] * (dps[h] - jnp.sum(dps[h] * ps[h], axis=-1, keepdims=True)) * MEM_SCALE) for h in heads]
        for h in heads:
            dq_ref[:, _head_lanes(h)] = _dot(dss[h], ks[h])
        for h in heads:
            dkv_ref[:, _head_lanes(h)] += _dot_tn(dss[h], qs[h])
            dkv_ref[:, _head_lanes(MEM_HEADS + h)] += _dot_tn(_bf(ps[h]), dos[h])

    return pl.pallas_call(
        body, name="mem_bwd", grid=(seq // t,),
        in_specs=[pl.BlockSpec((t, 512), lambda i: (i, BLK_QM)), pl.BlockSpec((MEM_LEN, 1024), lambda i: (0, 0)),
                  pl.BlockSpec((t, 512), lambda i: (i, 0))],
        out_specs=[pl.BlockSpec((t, 512), lambda i: (i, 0)), pl.BlockSpec((MEM_LEN, 1024), lambda i: (0, 0))],
        out_shape=[jax.ShapeDtypeStruct((seq, 512), F32), jax.ShapeDtypeStruct((MEM_LEN, 1024), F32)],
        compiler_params=_cparams(("arbitrary",)),
    )(pbf, mkv, do)


def _mid(x, tgt, o_a, o_b, o_m, p32, wmg, bmg, wba, wbb, wbm, wout, ln_g, ln_b, *, t):
    seq = x.shape[0]
    inv_d = 1.0 / D_MODEL

    def body(x_ref, t_ref, oa_ref, ob_ref, om_ref, ga_ref, gb_ref, gm_ref, wmg_ref, bmg_ref, wba_ref, wbb_ref,
             wbm_ref, wout_ref, lg_ref, lb_ref,
             du_ref, mrg_ref, dgp_ref, ha_ref, hb_ref, hm_ref, dya_ref, dyb_ref, dym_ref, doa_ref, dob_ref, dom_ref,
             dga_ref, dgb_ref, dgm_ref, dgain_ref, dbias_ref, dbmg_ref, loss_ref):
        @pl.when(pl.program_id(0) == 0)
        def _():
            dgain_ref[...] = jnp.zeros_like(dgain_ref)
            dbias_ref[...] = jnp.zeros_like(dbias_ref)
            dbmg_ref[...] = jnp.zeros_like(dbmg_ref)
            loss_ref[...] = jnp.zeros_like(loss_ref)

        xv = x_ref[...]
        gate = _sigmoid(_dot_cols(_bf(xv), wmg_ref) + bmg_ref[...])

        branches = []
        merged = None
        for b, (o_ref, g_ref, w_ref, h_ref) in enumerate(((oa_ref, ga_ref, wba_ref, ha_ref), (ob_ref, gb_ref, wbb_ref, hb_ref),
                                                         (om_ref, gm_ref, wbm_ref, hm_ref))):
            o, gt = o_ref[...], g_ref[...]
            sg = _sigmoid(gt)
            silu = gt * sg
            h = _bf(o * silu)
            h_ref[...] = h
            y = _dot_cols(h, w_ref)
            g_b = gate[:, b * D_MODEL:(b + 1) * D_MODEL]
            term = g_b * y
            merged = term if merged is None else merged + term
            branches.append((o, gt, sg, silu, y, g_b))
        mrg_b = _bf(merged)
        mrg_ref[...] = mrg_b

        u = DEEPNORM_ALPHA * xv + _dot(mrg_b, wout_ref[...])
        mu = jnp.mean(u, axis=-1, keepdims=True)
        uc = u - mu
        rstd = lax.rsqrt(jnp.mean(uc * uc, axis=-1, keepdims=True) + LN_EPS)
        xhat = uc * rstd
        lg = lg_ref[...]
        y_out = xhat * lg + lb_ref[...]
        err = y_out - t_ref[...]
        loss_ref[...] += 0.5 * jnp.sum(jnp.mean(err * err, axis=-1, keepdims=True), axis=0, keepdims=True)
        dy = err * inv_d
        dgain_ref[...] += jnp.sum(dy * xhat, axis=0, keepdims=True)
        dbias_ref[...] += jnp.sum(dy, axis=0, keepdims=True)
        dxh = dy * lg
        du = rstd * (dxh - jnp.mean(dxh, axis=-1, keepdims=True) - xhat * jnp.mean(dxh * xhat, axis=-1, keepdims=True))
        du_ref[...] = du

        dmerged = _dot_nt(_bf(du), wout_ref[...])
        outs = ((dya_ref, doa_ref, dga_ref, wba_ref), (dyb_ref, dob_ref, dgb_ref, wbb_ref), (dym_ref, dom_ref, dgm_ref, wbm_ref))
        dgp = []
        for (o, gt, sg, silu, y, g_b), (dy_ref, do_ref, dg_ref, w_ref) in zip(branches, outs):
            dyb = _bf(dmerged * g_b)
            dy_ref[...] = dyb
            dgp.append(dmerged * y * g_b * (1.0 - g_b))
            dh = _dot_nt_cols(dyb, w_ref)
            do_ref[...] = dh * silu
            dg_ref[...] = _bf(dh * o * (sg * (1.0 + gt * (1.0 - sg))))
        dgp = jnp.concatenate(dgp, axis=1)
        dgp_ref[...] = _bf(dgp)
        dbmg_ref[...] += jnp.sum(dgp, axis=0, keepdims=True)

    row = lambda w: pl.BlockSpec((t, w), lambda i: (i, 0))
    pblk = lambda c: pl.BlockSpec((t, 512), lambda i: (i, c))
    full = lambda shp: pl.BlockSpec(shp, lambda i: (0,) * len(shp))
    sds = jax.ShapeDtypeStruct
    return pl.pallas_call(
        body, name="mid", grid=(seq // t,),
        in_specs=[row(1024), row(1024), row(512), row(512), row(512), pblk(BLK_GATE_A), pblk(BLK_GATE_B), pblk(BLK_GATE_M),
                  full(wmg.shape), full((1, N_MERGE)), full(wba.shape), full(wbb.shape), full(wbm.shape), full(wout.shape),
                  full((1, D_MODEL)), full((1, D_MODEL))],
        out_specs=[row(1024), row(1024), row(N_MERGE), row(512), row(512), row(512), row(1024), row(1024), row(1024),
                   row(512), row(512), row(512), row(512), row(512), row(512),
                   full((1, D_MODEL)), full((1, D_MODEL)), full((1, N_MERGE)), full((1, 1))],
        out_shape=[sds((seq, 1024), F32), sds((seq, 1024), BF16), sds((seq, N_MERGE), BF16),
                   sds((seq, 512), BF16), sds((seq, 512), BF16), sds((seq, 512), BF16),
                   sds((seq, 1024), BF16), sds((seq, 1024), BF16), sds((seq, 1024), BF16),
                   sds((seq, 512), F32), sds((seq, 512), F32), sds((seq, 512), F32),
                   sds((seq, 512), BF16), sds((seq, 512), BF16), sds((seq, 512), BF16),
                   sds((1, D_MODEL), F32), sds((1, D_MODEL), F32), sds((1, N_MERGE), F32), sds((1, 1), F32)],
        compiler_params=_cparams(("arbitrary",)),
    )(x, tgt, o_a, o_b, o_m, p32, p32, p32, wmg, bmg, wba, wbb, wbm, wout, ln_g, ln_b)


def _primed_weights(w):
    w_in = w["w_in"]
    zc = lambda n: jnp.zeros((D_MODEL, n), w_in.dtype)
    w_in_p = jnp.concatenate([w_in[:, 0:384], zc(64), w_in[:, 384:416], zc(32), w_in[:, 416:]], axis=1)
    wqb = jnp.pad(w["w_q_b"].reshape(MLA_Q_LORA, MLA_HEADS, 96), ((0, 0), (0, 0), (0, 32))).reshape(MLA_Q_LORA, 1024)
    kv3 = w["w_kv_b"].reshape(MLA_KV_LORA, MLA_HEADS, 128)
    wk = jnp.pad(kv3[:, :, :MLA_NOPE], ((0, 0), (0, 0), (0, 64))).reshape(MLA_KV_LORA, 1024)
    wv = kv3[:, :, MLA_NOPE:].reshape(MLA_KV_LORA, 512)
    return w_in_p, wqb, jnp.concatenate([wk, wv], axis=1)


PROJ_BLK = 512


def _grad_x(du, dgpre, wmg, d_proj, w_in_p, *, tm, rider=None):
    seq = du.shape[0]
    n_pieces = len(d_proj)

    def body(du_ref, dg_ref, wmg_ref, *rest):
        piece_refs, win_ref, out_ref = rest[:n_pieces], rest[n_pieces], rest[n_pieces + 1]
        d_p = jnp.concatenate([_bf(p_ref[...]) for p_ref in piece_refs], axis=1)
        out_ref[...] = (DEEPNORM_ALPHA * du_ref[...] + _dot_nt_cols(dg_ref[...], wmg_ref)) + _dot_nt(d_p, win_ref[...])

    row = lambda w: pl.BlockSpec((tm, w), lambda i: (i, 0))
    whole = lambda a: pl.BlockSpec(a.shape, lambda i: (0,) * a.ndim)
    return _call_with_rider(
        body, rider, name="grad_x", grid=(seq // tm,),
        in_specs=[row(D_MODEL), row(N_MERGE), whole(wmg)] + [row(PROJ_BLK) for _ in d_proj] + [whole(w_in_p)],
        out_specs=[row(D_MODEL)], out_shape=[jax.ShapeDtypeStruct((seq, D_MODEL), F32)],
        args=(du, dgpre, wmg, *d_proj, w_in_p), semantics=("parallel",))


def _grad_w_in(x, d_proj, *, tk):
    seq = x.shape[0]
    n_pieces = len(d_proj)
    nk = seq // tk

    def body(x_ref, *rest):
        piece_refs, out_ref, acc = rest[:n_pieces], rest[n_pieces], rest[n_pieces + 1]
        j, kk = pl.program_id(0), pl.program_id(1)

        @pl.when(kk == 0)
        def _():
            acc[...] = jnp.zeros_like(acc)

        xb = _bf(x_ref[...])
        for pair in range(n_pieces // 2):
            @pl.when(j == pair)
            def _(pair=pair):
                both = jnp.concatenate([_bf(piece_refs[2 * pair][...]), _bf(piece_refs[2 * pair + 1][...])], axis=1)
                acc[...] += _dot_tn(xb, both)

        @pl.when(kk == nk - 1)
        def _():
            out_ref[...] = acc[...]

    def piece_spec(s):
        return pl.BlockSpec((tk, PROJ_BLK), lambda j, kk: (jnp.where(j == s // 2, kk, 0), 0))

    return pl.pallas_call(
        body, name="grad_w_in", grid=(n_pieces // 2, nk),
        in_specs=[pl.BlockSpec((tk, D_MODEL), lambda j, kk: (kk, 0))] + [piece_spec(s) for s in range(n_pieces)],
        out_specs=pl.BlockSpec((D_MODEL, 2 * PROJ_BLK), lambda j, kk: (0, j)),
        out_shape=jax.ShapeDtypeStruct((D_MODEL, n_pieces * PROJ_BLK), F32),
        scratch_shapes=[pltpu.VMEM((D_MODEL, 2 * PROJ_BLK), F32)],
        compiler_params=_cparams(("parallel", "arbitrary")),
    )(x, *d_proj)


EARLY_NAMES = ("w_mem_kv", "w_branch_mla", "w_branch_sb", "w_branch_mem", "w_merge_gate", "w_out")
LATE_NAMES = ("w_in", "w_q_b", "w_kv_b")


def _remote(src, dst, send_sem, recv_sem, device):
    return pltpu.make_async_remote_copy(src_ref=src, dst_ref=dst, send_sem=send_sem, recv_sem=recv_sem, device_id=device,
                                        device_id_type=MESH_ID)


def _gather_rider(shards):
    n = len(shards)

    def copies(src_refs, out_refs, sems):
        send_sems, recv_sems, local_sems = sems
        x, y, c = _place()
        me = 2 * x + y
        out = []
        for a, (s, o) in enumerate(zip(src_refs, out_refs)):
            out.append(pltpu.make_async_copy(s, o.at[me], local_sems.at[a]))
            for k, (px, py) in enumerate(_other_chips(x, y)):
                out.append(_Exchange(_remote(s, o.at[me], send_sems.at[k, a], recv_sems.at[k, a], (px, py, c)),
                                     _remote(s, o.at[2 * px + py], send_sems.at[k, a], recv_sems.at[k, a], (px, py, c))))
        return out

    return _Rider(shards, [jax.ShapeDtypeStruct((N_CHIPS,) + s.shape, s.dtype) for s in shards],
                  [pltpu.SemaphoreType.DMA((3, n)), pltpu.SemaphoreType.DMA((3, n)), pltpu.SemaphoreType.DMA((n,))], copies)


def _sibling_rider(g4):
    n = len(g4)

    def copies(g_refs, out_refs, sems):
        send_sems, recv_sems = sems
        x, y, c = _place()
        out = []
        for a, (g, o) in enumerate(zip(g_refs, out_refs)):
            half = g.shape[1] // 2
            theirs = pl.ds(pl.multiple_of((1 - c) * half, 8), half)
            cp = _remote(g.at[:, theirs, :], o, send_sems.at[a], recv_sems.at[a], (x, y, 1 - c))
            out.append(_Exchange(cp, cp))
        return out

    return _Rider(g4, [jax.ShapeDtypeStruct((N_CHIPS, g.shape[1] // 2, g.shape[2]), g.dtype) for g in g4],
                  [pltpu.SemaphoreType.DMA((n,)), pltpu.SemaphoreType.DMA((n,))], copies)


def _chips_rider(wire):
    n = len(wire)

    def copies(s_refs, out_refs, sems):
        send_sems, recv_sems = sems
        x, y, c = _place()
        out = []
        for a, (s, o) in enumerate(zip(s_refs, out_refs)):
            for k, (px, py) in enumerate(_other_chips(x, y)):
                cp = _remote(s.at[2 * px + py], o.at[RELATION_XOR[k] - 1], send_sems.at[k, a], recv_sems.at[k, a], (px, py, c))
                out.append(_Exchange(cp, cp))
        return out

    return _Rider(wire, [jax.ShapeDtypeStruct((3,) + s.shape[1:], s.dtype) for s in wire],
                  [pltpu.SemaphoreType.DMA((3, n)), pltpu.SemaphoreType.DMA((3, n))], copies)


def _local_step(x, mem, tgt, w, small, *, tq, tq_sb_bwd, tk, tk_mla, t_row, t_mm, t_wg, rest_shards=None):
    seq = x.shape[0]
    on_mesh = rest_shards is not None
    w_in_p, wqb, wkvb = _primed_weights(w)
    tabs = _rope_tables(seq)

    p32, pbf = _matmul(x, w_in_p, mode="nn", tm=256, tn=IN_WIDTH_P, tk=D_MODEL, out_dtypes=(F32, BF16), name="proj_in")
    qp, kp, vp = _mla_prep(p32, small["q_a_gain"], small["kv_a_gain"], wqb, wkvb, tabs, t=t_row)
    res = _mla_attn_fwd(qp, kp, vp, tq=tq, tk=tk_mla, tk_diag=tk, rider=_gather_rider(rest_shards) if on_mesh else None)
    o_a, lse = res[0], res[1]
    if on_mesh:
        w = dict(w, **{n: g if n in COL_SHARDED else _join_chips(n, g) for n, g in zip(EARLY_NAMES, res[2:])})
    else:
        w = dict(w, **{n: _split_by_chip(n, w[n]) for n in EARLY_NAMES if n in COL_SHARDED})
    wmg, wout = w["w_merge_gate"], w["w_out"]
    wba, wbb, wbm = w["w_branch_mla"], w["w_branch_sb"], w["w_branch_mem"]
    o_b, keep_total = _sb_attn_fwd(pbf, tq=tq, tk=tk)
    (mkv,) = _matmul(mem, w["w_mem_kv"], mode="nn", tm=MEM_LEN, tn=512, tk=D_MODEL, out_dtypes=(BF16,), name="mem_kv")
    o_m = _mem_fwd(pbf, mkv, t=t_row)

    (du, merged, dgpre, h_a, h_b, h_m, dy_a, dy_b, dy_m, do_a, do_b, do_m, dgate_a, dgate_b, dgate_m,
     d_ln_g, d_ln_b, d_bmg, loss) = _mid(x, tgt, o_a, o_b, o_m, p32, wmg, small["b_merge_gate"], wba, wbb, wbm, wout,
                                         small["ln_gain"], small["ln_bias"], t=t_row)

    wg = functools.partial(_matmul, mode="tn", tm=512, out_dtypes=(F32,))
    shard = (lambda width: dict(tn=width // N_CHIPS, by_column_block=True)) if on_mesh else (lambda width: dict(tn=1024))
    dq_m, dmkv = _mem_bwd(pbf, mkv, do_m, t=t_row)
    early = {"w_mem_kv": wg(mem, dmkv, tk=MEM_LEN, tn=1024, name="grad_w_mem_kv")[0],
             "w_branch_mla": wg(h_a, dy_a, tk=t_wg, name="grad_w_branch_a", **shard(D_MODEL))[0],
             "w_branch_sb": wg(h_b, dy_b, tk=t_wg, name="grad_w_branch_b", **shard(D_MODEL))[0],
             "w_branch_mem": wg(h_m, dy_m, tk=t_wg, name="grad_w_branch_m", **shard(D_MODEL))[0],
             "w_merge_gate": wg(x, dgpre, tk=t_wg, name="grad_w_merge_gate", **shard(N_MERGE))[0],
             "w_out": wg(merged, du, tk=t_wg, tn=1024, name="grad_w_out")[0]}

    if on_mesh:
        g4 = [early[n] if early[n].ndim == 3 else _split_by_chip(n, early[n]) for n in EARLY_NAMES]
        res = _mla_attn_bwd(qp, kp, vp, o_a, lse, do_a, tq=tq, tk=tk_mla, tk_diag=tk, rider=_sibling_rider(g4))
        (dqp, dkp, dvp), got = res[:3], res[3:]
        chipsum, wire = _rs_add_sibling(g4, got, [BF16] * len(g4))
        res = _sb_attn_bwd(pbf, keep_total, do_b, tq=tq_sb_bwd, tk=tk, rider=_chips_rider(wire))
        (dq_b, dk_b, dv_b), parts = res[:3], res[3:]
        early = _rs_add_chips(chipsum, parts)
    else:
        dqp, dkp, dvp = _mla_attn_bwd(qp, kp, vp, o_a, lse, do_a, tq=tq, tk=tk_mla, tk_diag=tk)
        dq_b, dk_b, dv_b = _sb_attn_bwd(pbf, keep_total, do_b, tq=tq_sb_bwd, tk=tk)
    dlat, d_wqb, d_wkvb, d_gq, d_gkv = _mla_post(p32, dqp, dkp, dvp, small["q_a_gain"], small["kv_a_gain"], wqb, wkvb, tabs,
                                                 t=t_row)

    d_proj = [dlat, dgate_a, dq_b, dk_b, dv_b, dgate_b, dq_m, dgate_m]
    d_winp = _grad_w_in(x, d_proj, tk=min(1024, seq))

    d_win = jnp.concatenate([d_winp[:, 0:384], d_winp[:, 448:480], d_winp[:, 512:]], axis=1)
    d_wq = d_wqb.reshape(MLA_Q_LORA, MLA_HEADS, 128)[:, :, :96].reshape(MLA_Q_LORA, 768)
    d_wk = d_wkvb[:, :1024].reshape(MLA_KV_LORA, MLA_HEADS, 128)[:, :, :MLA_NOPE]
    d_wv = d_wkvb[:, 1024:].reshape(MLA_KV_LORA, MLA_HEADS, MLA_V)
    d_wkv = jnp.concatenate([d_wk, d_wv], axis=2).reshape(MLA_KV_LORA, 1024)
    late = {"w_in": d_win, "w_q_b": d_wq, "w_kv_b": d_wkv}
    small_grads = {"q_a_gain": d_gq, "kv_a_gain": d_gkv, "b_merge_gate": d_bmg, "ln_gain": d_ln_g, "ln_bias": d_ln_b}
    if not on_mesh:
        (grad_x,) = _grad_x(du, dgpre, wmg, d_proj, w_in_p, tm=256)
        return loss[0, 0], grad_x, late, small_grads, early

    g4 = [_split_by_chip(n, late[n]) for n in LATE_NAMES]
    g4.append(jnp.broadcast_to(_pack_small(small_grads, scalar=loss)[None], (N_CHIPS, SMALL_ROWS, PACK_COLS)))
    got = _rs_to_sibling(g4)
    chipsum, wire = _rs_add_sibling(g4, got, [BF16] * len(LATE_NAMES) + [F32])
    res = _grad_x(du, dgpre, wmg, d_proj, w_in_p, tm=256, rider=_chips_rider(wire))
    late_mine = _rs_add_chips(chipsum, res[1:])
    return loss[0, 0], res[0], late_mine, None, early


def _place():
    x, y, c = lax.axis_index("x"), lax.axis_index("y"), lax.axis_index("c")
    return x, y, c


def _other_chips(x, y):
    return ((1 - x, y), (x, 1 - y), (1 - x, 1 - y))


SMALL_ROWS = 64
ADAM_STEPS_PER_HALF = 4


def _pack_small(d, scalar=None):
    parts = [d[n].reshape(-1) for n, _ in SMALL_SIZES] + ([] if scalar is None else [scalar.reshape(1)])
    flat = jnp.concatenate(parts)
    return jnp.pad(flat, (0, SMALL_ROWS * PACK_COLS - flat.shape[0])).reshape(SMALL_ROWS, PACK_COLS)


def _unpack_small(a):
    flat, out, c0 = a.reshape(-1), {}, 0
    for n, size in SMALL_SIZES:
        out[n] = flat[c0:c0 + size].reshape(1, size)
        c0 += size
    return out


def _split_by_chip(name, full):
    r, c = full.shape
    if name in COL_SHARDED:
        return full.reshape(r, N_CHIPS, c // N_CHIPS).transpose(1, 0, 2)
    return full.reshape(N_CHIPS, r // N_CHIPS, c)


def _join_chips(name, slots):
    _, r, cs = slots.shape
    if name in COL_SHARDED:
        return slots.transpose(1, 0, 2).reshape(r, N_CHIPS * cs)
    return slots.reshape(N_CHIPS * r, cs)


HBM_SPEC = pl.BlockSpec(memory_space=pltpu.HBM)


def _gather_shards(shards):
    n = len(shards)

    def body(*refs):
        w_refs, out_refs, wb_refs = refs[:n], refs[n:2 * n], refs[2 * n:3 * n]
        send_sems, recv_sems, pass_send_sems, pass_recv_sems, local_sems = refs[3 * n:]
        x, y, c = _place()
        me = 2 * x + y
        sibling = (x, y, 1 - c)

        def halves(ref):
            half = ref.shape[-2] // 2
            return (pl.ds(pl.multiple_of(c * half, 16), half), pl.ds(pl.multiple_of((1 - c) * half, 16), half))
        for w_ref, wb_ref in zip(w_refs, wb_refs):
            rows = w_ref.shape[0]
            chunk = min(rows, 128)

            def cast(i, carry, w_ref=w_ref, wb_ref=wb_ref, chunk=chunk):
                r0 = pl.multiple_of(i * chunk, chunk)
                wb_ref[pl.ds(r0, chunk), :] = _bf(w_ref[pl.ds(r0, chunk), :])
                return carry

            lax.fori_loop(0, rows // chunk, cast, 0)
        sends, locals_ = [], []
        for a, (wb_ref, out_ref) in enumerate(zip(wb_refs, out_refs)):
            mine = pltpu.make_async_copy(wb_ref, out_ref.at[me], local_sems.at[a])
            mine.start()
            locals_.append(mine)
            mine_rows, _ = halves(wb_ref)
            for k, (px, py) in enumerate(_other_chips(x, y)):
                cp = pltpu.make_async_remote_copy(src_ref=wb_ref.at[mine_rows, :], dst_ref=out_ref.at[me, mine_rows, :],
                                                  send_sem=send_sems.at[k, a], recv_sem=recv_sems.at[k, a],
                                                  device_id=(px, py, c), device_id_type=MESH_ID)
                cp.start()
                sends.append(cp)
        for a, (wb_ref, out_ref) in enumerate(zip(wb_refs, out_refs)):
            mine_rows, _ = halves(wb_ref)
            for k, (px, py) in enumerate(_other_chips(x, y)):
                landed = out_ref.at[2 * px + py, mine_rows, :]
                pltpu.make_async_remote_copy(src_ref=wb_ref.at[mine_rows, :], dst_ref=landed, send_sem=send_sems.at[k, a],
                                             recv_sem=recv_sems.at[k, a], device_id=(px, py, c),
                                             device_id_type=MESH_ID).wait_recv()
                cp = pltpu.make_async_remote_copy(src_ref=landed, dst_ref=landed, send_sem=pass_send_sems.at[k, a],
                                                  recv_sem=pass_recv_sems.at[k, a], device_id=sibling, device_id_type=MESH_ID)
                cp.start()
                sends.append(cp)
        for a, (wb_ref, out_ref) in enumerate(zip(wb_refs, out_refs)):
            _, their_rows = halves(wb_ref)
            for k, (px, py) in enumerate(_other_chips(x, y)):
                passed = out_ref.at[2 * px + py, their_rows, :]
                pltpu.make_async_remote_copy(src_ref=passed, dst_ref=passed, send_sem=pass_send_sems.at[k, a],
                                             recv_sem=pass_recv_sems.at[k, a], device_id=sibling,
                                             device_id_type=MESH_ID).wait_recv()
        for cp in sends:
            cp.wait_send()
        for cp in locals_:
            cp.wait()

    return pl.pallas_call(
        body, name="gather_weights",
        in_specs=[pl.BlockSpec(memory_space=pltpu.VMEM)] * n,
        out_specs=[HBM_SPEC] * n,
        out_shape=[jax.ShapeDtypeStruct((N_CHIPS,) + s.shape, BF16) for s in shards],
        scratch_shapes=[pltpu.VMEM(s.shape, BF16) for s in shards]
        + [pltpu.SemaphoreType.DMA((3, n))] * 4 + [pltpu.SemaphoreType.DMA((n,))],
        compiler_params=pltpu.CompilerParams(vmem_limit_bytes=VMEM_LIMIT),
    )(*shards)


def _cast_bf16_list(arrays):
    def body(*refs):
        for a_ref, o_ref in zip(refs[:len(arrays)], refs[len(arrays):]):
            o_ref[...] = _bf(a_ref[...])

    specs = [pl.BlockSpec((a.shape[0] // 4, a.shape[1]), lambda i: (i, 0)) for a in arrays]
    return pl.pallas_call(
        body, name="cast_shards", grid=(4,), in_specs=specs, out_specs=specs,
        out_shape=[jax.ShapeDtypeStruct(a.shape, BF16) for a in arrays],
        compiler_params=_cparams(("parallel",)),
    )(*arrays)


def _rs_to_sibling(g4):
    n = len(g4)

    def body(*refs):
        g_refs, out_refs = refs[:n], refs[n:2 * n]
        send_sems, recv_sems = refs[2 * n:]
        x, y, c = _place()
        copies = []
        for a, (g_ref, out_ref) in enumerate(zip(g_refs, out_refs)):
            half = g_ref.shape[1] // 2
            theirs = pl.ds(pl.multiple_of((1 - c) * half, 8), half)
            copies.append(pltpu.make_async_remote_copy(src_ref=g_ref.at[:, theirs, :], dst_ref=out_ref, send_sem=send_sems.at[a],
                                                       recv_sem=recv_sems.at[a], device_id=(x, y, 1 - c),
                                                       device_id_type=MESH_ID))
        for cp in copies:
            cp.start()
        for cp in copies:
            cp.wait()

    return pl.pallas_call(
        body, name="rs_sibling", in_specs=[HBM_SPEC] * n, out_specs=[HBM_SPEC] * n,
        out_shape=[jax.ShapeDtypeStruct((N_CHIPS, g.shape[1] // 2, g.shape[2]), F32) for g in g4],
        scratch_shapes=[pltpu.SemaphoreType.DMA((n,)), pltpu.SemaphoreType.DMA((n,))],
    )(*g4)


def _rs_add_sibling(g4, got, wire_dtypes):
    n = len(g4)
    narrow = [a for a in range(n) if wire_dtypes[a] != F32]

    def body(c_ref, *refs):
        outs = refs[2 * n:3 * n]
        wires = dict(zip(narrow, refs[3 * n:]))
        for a, (g_ref, r_ref, o_ref) in enumerate(zip(refs[:n], refs[n:2 * n], outs)):
            s = g_ref[...] + r_ref[...]
            o_ref[...] = s
            if a in wires:
                wires[a][...] = s.astype(wires[a].dtype)

    blk = lambda r: (1, r.shape[1], r.shape[2])
    plain = lambda r: pl.BlockSpec(blk(r), lambda j, c_ref: (j, 0, 0))
    grid_spec = pltpu.PrefetchScalarGridSpec(
        num_scalar_prefetch=1, grid=(N_CHIPS,),
        in_specs=[pl.BlockSpec(blk(r), lambda j, c_ref: (j, c_ref[0], 0)) for r in got] + [plain(r) for r in got],
        out_specs=[plain(r) for r in got] + [plain(got[a]) for a in narrow])
    res = pl.pallas_call(
        body, name="rs_add_sibling", grid_spec=grid_spec,
        out_shape=[jax.ShapeDtypeStruct(r.shape, F32) for r in got]
        + [jax.ShapeDtypeStruct(got[a].shape, wire_dtypes[a]) for a in narrow],
        compiler_params=_cparams(("parallel",)),
    )(lax.axis_index("c").astype(jnp.int32).reshape(1), *g4, *got)
    chipsum = list(res[:n])
    wire = list(chipsum)
    for a, w in zip(narrow, res[n:]):
        wire[a] = w
    return chipsum, wire


RELATION_XOR = (2, 1, 3)


def _rs_add_chips(chipsum, parts):
    n = len(parts)

    def body(me_ref, *refs):
        me = me_ref[0]
        for s_ref, p_ref, o_ref in zip(refs[:n], refs[n:2 * n], refs[2 * n:]):
            own = s_ref[0]
            total = None
            for k in range(N_CHIPS):
                theirs = p_ref[jnp.maximum(jnp.bitwise_xor(me, k) - 1, 0)].astype(F32)
                term = jnp.where(me == k, own, theirs)
                total = term if total is None else total + term
            o_ref[...] = total

    grid_spec = pltpu.PrefetchScalarGridSpec(
        num_scalar_prefetch=1, grid=(2,),
        in_specs=[pl.BlockSpec((1, p.shape[1] // 2, p.shape[2]), lambda i, me_ref: (me_ref[0], i, 0)) for p in parts]
        + [pl.BlockSpec((3, p.shape[1] // 2, p.shape[2]), lambda i, me_ref: (0, i, 0)) for p in parts],
        out_specs=[pl.BlockSpec((p.shape[1] // 2, p.shape[2]), lambda i, me_ref: (i, 0)) for p in parts])
    me = (2 * lax.axis_index("x") + lax.axis_index("y")).astype(jnp.int32).reshape(1)
    return pl.pallas_call(
        body, name="rs_add_chips", grid_spec=grid_spec,
        out_shape=[jax.ShapeDtypeStruct(p.shape[1:], F32) for p in parts],
        compiler_params=_cparams(("parallel",)),
    )(me, *chipsum, *parts)


def _rs_swap_halves(halves):
    n = len(halves)

    def body(*refs):
        h_refs, out_refs = refs[:n], refs[n:2 * n]
        send_sems, recv_sems = refs[2 * n:]
        x, y, c = _place()
        copies = [pltpu.make_async_remote_copy(src_ref=h_ref, dst_ref=out_ref, send_sem=send_sems.at[a], recv_sem=recv_sems.at[a],
                                               device_id=(x, y, 1 - c), device_id_type=MESH_ID)
                  for a, (h_ref, out_ref) in enumerate(zip(h_refs, out_refs))]
        for cp in copies:
            cp.start()
        for cp in copies:
            cp.wait()

    return pl.pallas_call(
        body, name="rs_swap_halves", in_specs=[HBM_SPEC] * n, out_specs=[HBM_SPEC] * n,
        out_shape=[jax.ShapeDtypeStruct(h.shape, F32) for h in halves],
        scratch_shapes=[pltpu.SemaphoreType.DMA((n,)), pltpu.SemaphoreType.DMA((n,))],
    )(*halves)


def _adamw_list(ws, g_mine, g_theirs, ms, vs):
    n = len(ws)

    def body(c_ref, *refs):
        w_refs, gm_refs, gt_refs, m_refs, v_refs = (refs[k * n:(k + 1) * n] for k in range(5))
        g_refs, d_refs, nm_refs, nv_refs = (refs[k * n:(k + 1) * n] for k in range(5, 9))
        mine = (pl.program_id(0) // ADAM_STEPS_PER_HALF) == c_ref[0]
        for a in range(n):
            gv = jnp.where(mine, gm_refs[a][...], gt_refs[a][...])
            g_refs[a][...] = gv
            m_new = ADAM_B1 * m_refs[a][...] + (1.0 - ADAM_B1) * gv
            v_new = ADAM_B2 * v_refs[a][...] + (1.0 - ADAM_B2) * (gv * gv)
            m_hat = m_new / (1.0 - ADAM_B1 ** ADAM_STEP)
            v_hat = v_new / (1.0 - ADAM_B2 ** ADAM_STEP)
            d_refs[a][...] = -ADAM_LR * (m_hat / (jnp.sqrt(v_hat) + ADAM_EPS) + ADAM_WD * w_refs[a][...])
            nm_refs[a][...] = m_new
            nv_refs[a][...] = v_new

    steps = 2 * ADAM_STEPS_PER_HALF
    whole = [pl.BlockSpec((w.shape[0] // steps, w.shape[1]), lambda i, c_ref: (i, 0)) for w in ws]
    half = [pl.BlockSpec((w.shape[0] // steps, w.shape[1]), lambda i, c_ref: (i % ADAM_STEPS_PER_HALF, 0)) for w in ws]
    shapes = [jax.ShapeDtypeStruct(w.shape, F32) for w in ws]
    grid_spec = pltpu.PrefetchScalarGridSpec(num_scalar_prefetch=1, grid=(steps,),
                                             in_specs=whole + half + half + whole + whole, out_specs=whole * 4)
    res = pl.pallas_call(
        body, name="adamw", grid_spec=grid_spec, out_shape=shapes * 4,
        compiler_params=_cparams(("parallel",)),
    )(lax.axis_index("c").astype(jnp.int32).reshape(1), *ws, *g_mine, *g_theirs, *ms, *vs)
    return res[:n], res[n:2 * n], res[2 * n:3 * n], res[3 * n:]


WEIGHT_NAMES = ("w_in", "w_mem_kv", "q_a_gain", "w_q_b", "kv_a_gain", "w_kv_b", "w_branch_mla", "w_branch_sb",
                "w_branch_mem", "w_merge_gate", "b_merge_gate", "w_out", "ln_gain", "ln_bias")
SMALL_NAMES = tuple(n for n, _ in SMALL_SIZES)


def kernel(x, mem, w_in, w_mem_kv, q_a_gain, w_q_b, kv_a_gain, w_kv_b, w_branch_mla, w_branch_sb, w_branch_mem, w_merge_gate, b_merge_gate, w_out, ln_gain, ln_bias, loss_target, m_w_in, m_w_mem_kv, m_q_a_gain, m_w_q_b, m_kv_a_gain, m_w_kv_b, m_w_branch_mla, m_w_branch_sb, m_w_branch_mem, m_w_merge_gate, m_b_merge_gate, m_w_out, m_ln_gain, m_ln_bias, v_w_in, v_w_mem_kv, v_q_a_gain, v_w_q_b, v_kv_a_gain, v_w_kv_b, v_w_branch_mla, v_w_branch_sb, v_w_branch_mem, v_w_merge_gate, v_b_merge_gate, v_w_out, v_ln_gain, v_ln_bias):
    weights = dict(zip(WEIGHT_NAMES, (w_in, w_mem_kv, q_a_gain, w_q_b, kv_a_gain, w_kv_b, w_branch_mla, w_branch_sb,
                                      w_branch_mem, w_merge_gate, b_merge_gate, w_out, ln_gain, ln_bias)))
    mom1 = dict(zip(WEIGHT_NAMES, (m_w_in, m_w_mem_kv, m_q_a_gain, m_w_q_b, m_kv_a_gain, m_w_kv_b, m_w_branch_mla,
                                   m_w_branch_sb, m_w_branch_mem, m_w_merge_gate, m_b_merge_gate, m_w_out, m_ln_gain,
                                   m_ln_bias)))
    mom2 = dict(zip(WEIGHT_NAMES, (v_w_in, v_w_mem_kv, v_q_a_gain, v_w_q_b, v_kv_a_gain, v_w_kv_b, v_w_branch_mla,
                                   v_w_branch_sb, v_w_branch_mem, v_w_merge_gate, v_b_merge_gate, v_w_out, v_ln_gain,
                                   v_ln_bias)))
    def as_list(d):
        return [d[n][0] for n in BIG_NAMES] + [_pack_small({n: d[n] for n in SMALL_NAMES})]

    w_list, m_list, v_list = as_list(weights), as_list(mom1), as_list(mom2)

    gathered = _gather_shards([weights[n][0] for n in LATE_NAMES])
    first_w = {n: _join_chips(n, g) for n, g in zip(LATE_NAMES, gathered)}
    rest_shards = _cast_bf16_list([weights[n][0] for n in EARLY_NAMES])
    small = {n: weights[n] for n in SMALL_NAMES}

    seq = x.shape[1]
    _, grad_x, late_mine, _, early_mine = _local_step(
        x[0], mem[0], loss_target[0], first_w, small, tq=min(1024, seq), tq_sb_bwd=512, tk=256, tk_mla=512, t_row=256, t_mm=512,
        t_wg=min(2048, seq), rest_shards=rest_shards)
    by_name = dict(zip(EARLY_NAMES + LATE_NAMES + ("small",), list(early_mine) + list(late_mine)))
    mine = [by_name[n] for n in BIG_NAMES + ("small",)]
    theirs = _rs_swap_halves(mine)
    g_list, d_list, nm_list, nv_list = _adamw_list(w_list, mine, theirs, m_list, v_list)

    loss = g_list[-1].reshape(-1)[SMALL_TOTAL]
    outs = [loss, grad_x[None]]
    for arrays in (g_list, d_list, nm_list, nv_list):
        big = dict(zip(BIG_NAMES, arrays[:-1]))
        sm = _unpack_small(arrays[-1])
        outs.extend(big[n][None] if n in big else sm[n] for n in WEIGHT_NAMES)
    return tuple(outs)
```

```python
import functools
import math

import numpy as np
import jax
import jax.numpy as jnp
from jax import lax
from jax.experimental import pallas as pl
from jax.experimental.pallas import tpu as pltpu

F32 = jnp.float32
BF16 = jnp.bfloat16
MESH_ID = pl.DeviceIdType.MESH

D_MODEL = 1024
MEM_LEN = 256
MLA_HEADS = 8
MLA_NOPE = 64
MLA_ROPE = 32
MLA_V = 64
MLA_Q_LORA = 256
MLA_KV_LORA = 128
SB_HEADS = 8
SB_HEAD_DIM = 64
MEM_HEADS = 4
MEM_HEAD_DIM = 128
ROPE_BASE = 10000.0
RMS_EPS = 1e-6
LN_EPS = 1e-5
DEEPNORM_ALPHA = 2.0 ** 0.25
MLA_SCALE = 1.0 / math.sqrt(MLA_NOPE + MLA_ROPE)
SB_SCALE = 1.0 / math.sqrt(SB_HEAD_DIM)
MEM_SCALE = 1.0 / math.sqrt(MEM_HEAD_DIM)

ADAM_LR = 0.001
ADAM_B1 = 0.9
ADAM_B2 = 0.999
ADAM_EPS = 1e-08
ADAM_WD = 0.01
ADAM_STEP = 10

LANES = 128
HALF = 64
N_CHIPS = 4
PACK_COLS = 1024
VMEM_LIMIT = 56 * 1024 * 1024

IN_WIDTH_P = 4096
BLK_LAT, BLK_GATE_A, BLK_QB, BLK_KB, BLK_VB, BLK_GATE_B, BLK_QM, BLK_GATE_M = range(8)
N_MERGE = 3 * D_MODEL

BIG_NAMES = ("w_in", "w_mem_kv", "w_q_b", "w_kv_b", "w_branch_mla", "w_branch_sb", "w_branch_mem", "w_merge_gate", "w_out")
COL_SHARDED = ("w_in", "w_q_b", "w_kv_b", "w_branch_mla", "w_branch_sb", "w_branch_mem", "w_merge_gate")
SMALL_SIZES = (("q_a_gain", 256), ("kv_a_gain", 128), ("b_merge_gate", 3072), ("ln_gain", 1024), ("ln_bias", 1024))
SMALL_TOTAL = sum(s for _, s in SMALL_SIZES)


def _cparams(sem=None):
    return pltpu.CompilerParams(dimension_semantics=sem, vmem_limit_bytes=VMEM_LIMIT)


def _dot(a, b):
    return jnp.dot(a, b, preferred_element_type=F32)


def _dot_nt(a, b):
    return lax.dot_general(a, b, (((1,), (1,)), ((), ())), preferred_element_type=F32)


def _dot_tn(a, b):
    return lax.dot_general(a, b, (((0,), (0,)), ((), ())), preferred_element_type=F32)


def _bf(x):
    return x.astype(BF16)


def _dot_cols(a, w_ref):
    return jnp.concatenate([_dot(a, w_ref[j]) for j in range(w_ref.shape[0])], axis=1)


def _dot_nt_cols(a, w_ref):
    cs = w_ref.shape[2]
    out = None
    for j in range(w_ref.shape[0]):
        term = _dot_nt(a[:, j * cs:(j + 1) * cs], w_ref[j])
        out = term if out is None else out + term
    return out


def _sigmoid(x):
    return 1.0 / (1.0 + jnp.exp(-x))


def _matmul(a, b, *, mode, tm, tn, tk, out_dtypes, name, add=None, add_scale=1.0, by_column_block=False):
    if mode == "nn":
        (m, k), n = a.shape, b.shape[1]
        a_spec = pl.BlockSpec((tm, tk), lambda i, j, kk: (i, kk))
        b_spec = pl.BlockSpec((tk, tn), lambda i, j, kk: (kk, j))
        dot = _dot
    elif mode == "nt":
        (m, k), n = a.shape, b.shape[0]
        a_spec = pl.BlockSpec((tm, tk), lambda i, j, kk: (i, kk))
        b_spec = pl.BlockSpec((tn, tk), lambda i, j, kk: (j, kk))
        dot = _dot_nt
    else:
        (k, m), n = a.shape, b.shape[1]
        a_spec = pl.BlockSpec((tk, tm), lambda i, j, kk: (kk, i))
        b_spec = pl.BlockSpec((tk, tn), lambda i, j, kk: (kk, j))
        dot = _dot_tn
    assert m % tm == 0 and n % tn == 0 and k % tk == 0, (name, m, n, k)
    nk = k // tk
    n_out = len(out_dtypes)
    has_add = add is not None

    def body(*refs):
        a_ref, b_ref = refs[0], refs[1]
        add_ref = refs[2] if has_add else None
        outs = refs[2 + has_add: 2 + has_add + n_out]
        acc = refs[-1]
        kk = pl.program_id(2)

        @pl.when(kk == 0)
        def _():
            acc[...] = jnp.zeros_like(acc)

        acc[...] += dot(_bf(a_ref[...]), _bf(b_ref[...]))

        @pl.when(kk == nk - 1)
        def _():
            r = acc[...]
            if has_add:
                r = r + add_scale * add_ref[...]
            for o in outs:
                o[...] = r.astype(o.dtype)

    in_specs = [a_spec, b_spec]
    args = [a, b]
    if has_add:
        in_specs.append(pl.BlockSpec((tm, tn), lambda i, j, kk: (i, j)))
        args.append(add)
    if by_column_block:
        out_spec = pl.BlockSpec((None, tm, tn), lambda i, j, kk: (j, i, 0))
        out_dims = (n // tn, m, tn)
    else:
        out_spec = pl.BlockSpec((tm, tn), lambda i, j, kk: (i, j))
        out_dims = (m, n)
    res = pl.pallas_call(
        body, name=name, grid=(m // tm, n // tn, nk),
        in_specs=in_specs, out_specs=[out_spec] * n_out,
        out_shape=[jax.ShapeDtypeStruct(out_dims, dt) for dt in out_dtypes],
        scratch_shapes=[pltpu.VMEM((tm, tn), F32)],
        compiler_params=_cparams(("parallel", "parallel", "arbitrary")),
    )(*args)
    return res


def _rope_tables(seq):
    half = MLA_ROPE // 2
    freqs = ROPE_BASE ** (-jnp.arange(half, dtype=F32) / half)
    ang = jnp.arange(seq, dtype=jnp.int32).astype(F32)[:, None] * freqs[None, :]
    cos, sin = jnp.cos(ang), jnp.sin(ang)
    z = lambda w: jnp.zeros((seq, w), F32)
    c_q = jnp.concatenate([jnp.ones((seq, MLA_NOPE), F32), cos, cos, z(32)], axis=1)
    c_k = jnp.concatenate([z(MLA_NOPE), cos, cos, z(32)], axis=1)
    s_lo = jnp.concatenate([z(MLA_NOPE), -sin, z(half), z(32)], axis=1)
    s_hi = jnp.concatenate([z(MLA_NOPE), z(half), sin, z(32)], axis=1)
    return c_q, c_k, s_lo, s_hi


def _rope_fwd(x, c, s_lo, s_hi):
    return x * c + pltpu.roll(x, LANES - 16, 1) * s_lo + pltpu.roll(x, 16, 1) * s_hi


def _rope_bwd(d, c, s_lo, s_hi):
    return d * c - pltpu.roll(d, 16, 1) * s_hi - pltpu.roll(d, LANES - 16, 1) * s_lo


def _rms_fwd(x, g):
    r = lax.rsqrt(jnp.mean(x * x, axis=-1, keepdims=True) + RMS_EPS)
    xn = x * r
    return xn * g, xn, r


def _mla_prep(p32, gq, gkv, wqb, wkvb, tabs, *, t):
    seq = p32.shape[0]

    def body(lat_ref, gq_ref, gkv_ref, wqb_ref, wkvb_ref, cq_ref, ck_ref, slo_ref, shi_ref, q_ref, k_ref, v_ref):
        lat = lat_ref[...]
        slo, shi = slo_ref[...], shi_ref[...]
        nq, _, _ = _rms_fwd(lat[:, 0:MLA_Q_LORA], gq_ref[...])
        qa = _dot(_bf(nq), wqb_ref[...])
        cq = cq_ref[...]
        for h in range(MLA_HEADS):
            blk = qa[:, h * LANES:(h + 1) * LANES]
            q_ref[:, h * LANES:(h + 1) * LANES] = _bf(_rope_fwd(blk, cq, slo, shi))
        nkv, _, _ = _rms_fwd(lat[:, MLA_Q_LORA:MLA_Q_LORA + MLA_KV_LORA], gkv_ref[...])
        kv = _dot(_bf(nkv), wkvb_ref[...])
        kpe = _rope_fwd(lat[:, 384:512], ck_ref[...], slo, shi)
        for h in range(MLA_HEADS):
            k_ref[:, h * LANES:(h + 1) * LANES] = _bf(kv[:, h * LANES:(h + 1) * LANES] + kpe)
        v_ref[...] = _bf(kv[:, MLA_HEADS * LANES:])

    row = lambda w: pl.BlockSpec((t, w), lambda i: (i, 0))
    full = lambda shp: pl.BlockSpec(shp, lambda i: (0, 0))
    return pl.pallas_call(
        body, name="mla_prep", grid=(seq // t,),
        in_specs=[row(512), full((1, MLA_Q_LORA)), full((1, MLA_KV_LORA)), full(wqb.shape), full(wkvb.shape),
                  row(LANES), row(LANES), row(LANES), row(LANES)],
        out_specs=[row(1024), row(1024), row(512)],
        out_shape=[jax.ShapeDtypeStruct((seq, 1024), BF16), jax.ShapeDtypeStruct((seq, 1024), BF16),
                   jax.ShapeDtypeStruct((seq, 512), BF16)],
        compiler_params=_cparams(("parallel",)),
    )(p32, gq, gkv, wqb, wkvb, *tabs)


def _mla_post(p32, dq, dk, dv, gq, gkv, wqb, wkvb, tabs, *, t):
    seq = p32.shape[0]

    def body(lat_ref, dq_ref, dk_ref, dv_ref, gq_ref, gkv_ref, wqb_ref, wkvb_ref, cq_ref, ck_ref, slo_ref, shi_ref,
             dlat_ref, dwqb_ref, dwkvb_ref, dgq_ref, dgkv_ref):
        @pl.when(pl.program_id(0) == 0)
        def _():
            dwqb_ref[...] = jnp.zeros_like(dwqb_ref)
            dwkvb_ref[...] = jnp.zeros_like(dwkvb_ref)
            dgq_ref[...] = jnp.zeros_like(dgq_ref)
            dgkv_ref[...] = jnp.zeros_like(dgkv_ref)

        lat = lat_ref[...]
        slo, shi = slo_ref[...], shi_ref[...]
        cq = cq_ref[...]
        gq_v, gkv_v = gq_ref[...], gkv_ref[...]
        nq, xq, rq = _rms_fwd(lat[:, 0:MLA_Q_LORA], gq_v)
        nkv, xkv, rkv = _rms_fwd(lat[:, MLA_Q_LORA:MLA_Q_LORA + MLA_KV_LORA], gkv_v)

        dqa = jnp.concatenate(
            [_rope_bwd(dq_ref[:, h * LANES:(h + 1) * LANES], cq, slo, shi) for h in range(MLA_HEADS)], axis=1)
        dqa_b = _bf(dqa)
        dwqb_ref[...] += _dot_tn(_bf(nq), dqa_b)
        dnq = _dot_nt(dqa_b, wqb_ref[...])
        dgq_ref[...] += jnp.sum(dnq * xq, axis=0, keepdims=True)
        dxn = dnq * gq_v
        dcq = rq * (dxn - xq * jnp.mean(dxn * xq, axis=-1, keepdims=True))

        dkf = dk_ref[...]
        dkv_b = _bf(jnp.concatenate([dkf, dv_ref[...]], axis=1))
        dwkvb_ref[...] += _dot_tn(_bf(nkv), dkv_b)
        dnkv = _dot_nt(dkv_b, wkvb_ref[...])
        dgkv_ref[...] += jnp.sum(dnkv * xkv, axis=0, keepdims=True)
        dxn = dnkv * gkv_v
        dckv = rkv * (dxn - xkv * jnp.mean(dxn * xkv, axis=-1, keepdims=True))

        dkpe = dkf[:, 0:LANES]
        for h in range(1, MLA_HEADS):
            dkpe = dkpe + dkf[:, h * LANES:(h + 1) * LANES]
        dkr = _rope_bwd(dkpe, ck_ref[...], slo, shi)
        dlat_ref[...] = _bf(jnp.concatenate([dcq, dckv, dkr], axis=1))

    row = lambda w: pl.BlockSpec((t, w), lambda i: (i, 0))
    full = lambda shp: pl.BlockSpec(shp, lambda i: (0, 0))
    return pl.pallas_call(
        body, name="mla_post", grid=(seq // t,),
        in_specs=[row(512), row(1024), row(1024), row(512), full((1, MLA_Q_LORA)), full((1, MLA_KV_LORA)),
                  full(wqb.shape), full(wkvb.shape), row(LANES), row(LANES), row(LANES), row(LANES)],
        out_specs=[row(512), full(wqb.shape), full(wkvb.shape), full((1, MLA_Q_LORA)), full((1, MLA_KV_LORA))],
        out_shape=[jax.ShapeDtypeStruct((seq, 512), BF16), jax.ShapeDtypeStruct(wqb.shape, F32),
                   jax.ShapeDtypeStruct(wkvb.shape, F32), jax.ShapeDtypeStruct((1, MLA_Q_LORA), F32),
                   jax.ShapeDtypeStruct((1, MLA_KV_LORA), F32)],
        compiler_params=_cparams(("arbitrary",)),
    )(p32, dq, dk, dv, gq, gkv, wqb, wkvb, *tabs)


def _split_bf16(x):
    hi = _bf(x)
    return hi, _bf(x - hi.astype(F32))


def _tri_sum(x, u):
    hi, lo = _split_bf16(x)
    return _dot(hi, u) + _dot(lo, u)


def _softplus(z):
    return jnp.maximum(z, 0.0) + jnp.log(1.0 + jnp.exp(-jnp.abs(z)))


def _head_queries(q, left):
    zero = jnp.zeros_like(q)
    return jnp.where(left, q, zero) * SB_SCALE, jnp.where(left, zero, q) * SB_SCALE


ROW_GROUP = 128
SB_BWD_CHAINS_IN_FLIGHT = 16
ANY_HBM = pl.BlockSpec(memory_space=pltpu.HBM)


class _Exchange:
    def __init__(self, send, landing):
        self.send, self.landing = send, landing

    def start(self):
        self.send.start()

    def wait(self):
        self.landing.wait_recv()
        self.send.wait_send()


class _Rider:
    def __init__(self, operands, out_shapes, sem_shapes, copies):
        self.operands, self.out_shapes, self.sem_shapes, self.copies = list(operands), list(out_shapes), list(sem_shapes), copies


def _call_with_rider(body, rider, *, name, grid, in_specs, out_specs, out_shape, args, semantics, scratch=()):
    scratch = list(scratch)
    if rider is None:
        return pl.pallas_call(body, name=name, grid=grid, in_specs=in_specs, out_specs=out_specs, out_shape=out_shape,
                              scratch_shapes=scratch, compiler_params=_cparams(semantics))(*args)
    n_in, n_out, n_rin, n_rout = len(in_specs), len(out_specs), len(rider.operands), len(rider.out_shapes)

    def full_body(*refs):
        ins, r_ins = refs[:n_in], refs[n_in:n_in + n_rin]
        outs = refs[n_in + n_rin:n_in + n_rin + n_out]
        r_outs = refs[n_in + n_rin + n_out:n_in + n_rin + n_out + n_rout]
        rest = refs[n_in + n_rin + n_out + n_rout:]
        own_scratch, sems = rest[:len(scratch)], rest[len(scratch):]
        first, last = None, None
        for axis, size in enumerate(grid):
            at_start, at_end = pl.program_id(axis) == 0, pl.program_id(axis) == size - 1
            first = at_start if first is None else first & at_start
            last = at_end if last is None else last & at_end

        @pl.when(first)
        def _():
            for cp in rider.copies(r_ins, r_outs, sems):
                cp.start()

        body(*ins, *outs, *own_scratch)

        @pl.when(last)
        def _():
            for cp in rider.copies(r_ins, r_outs, sems):
                cp.wait()

    return pl.pallas_call(
        full_body, name=name, grid=grid, in_specs=list(in_specs) + [ANY_HBM] * n_rin,
        out_specs=list(out_specs) + [ANY_HBM] * n_rout, out_shape=list(out_shape) + rider.out_shapes,
        scratch_shapes=scratch + rider.sem_shapes, compiler_params=_cparams(("arbitrary",) * len(grid)),
    )(*args, *rider.operands)


def _chains(tq):
    return [(h, g) for g in range(tq // ROW_GROUP) for h in range(2)]


def _chain_pattern(g, m, tk, strict):
    r_lo, r_hi = g * ROW_GROUP, (g + 1) * ROW_GROUP - 1
    c_lo, c_hi = m * tk, (m + 1) * tk - 1
    if (c_lo >= r_hi) if strict else (c_lo > r_hi):
        return None
    if (c_hi < r_lo) if strict else (c_hi <= r_lo):
        return True
    rr = lax.broadcasted_iota(jnp.int32, (ROW_GROUP, tk), 0) + r_lo
    cc = lax.broadcasted_iota(jnp.int32, (ROW_GROUP, tk), 1) + c_lo
    return (cc < rr) if strict else (cc <= rr)


def _masked(x, pat, fill=0.0):
    return x if pat is True else jnp.where(pat, x, fill)


def _rows(g):
    return slice(g * ROW_GROUP, (g + 1) * ROW_GROUP)


def _tri_matrix(tk, cmp):
    rr = lax.broadcasted_iota(jnp.int32, (tk, tk), 0)
    cc = lax.broadcasted_iota(jnp.int32, (tk, tk), 1)
    return cmp(rr, cc).astype(BF16)


def _mla_attn_fwd(qp, kp, vp, *, tq, tk, tk_diag, rider=None):
    seq = qp.shape[0]
    neg = float(np.finfo(np.float32).min)
    chains = _chains(tq)

    def body(q_ref, k_ref, v_ref, o_ref, lse_ref):
        i = pl.program_id(1)
        left = lax.broadcasted_iota(jnp.int32, (tq, LANES), 1) < HALF
        qs = [q_ref[_rows(g), h * LANES:(h + 1) * LANES] for h, g in chains]

        def block(start, carry, m, tk):
            v = v_ref[pl.ds(start, tk), :]
            pats = [True if m is None else _chain_pattern(g, m, tk, False) for _, g in chains]
            live = [n for n, p in enumerate(pats) if p is not None]
            ss = {n: _dot_nt(qs[n], k_ref[pl.ds(start, tk), chains[n][0] * LANES:(chains[n][0] + 1) * LANES]) for n in live}
            new = list(carry)
            for n in live:
                m_old, l_old, acc = carry[n]
                s = _masked(ss[n] * MLA_SCALE, pats[n], neg)
                m_new = jnp.maximum(m_old, jnp.max(s, axis=-1, keepdims=True))
                a = jnp.exp(m_old - m_new)
                p = jnp.exp(s - m_new)
                new[n] = (m_new, a * l_old + jnp.sum(p, axis=-1, keepdims=True), a * acc + _dot(_bf(p), v))
            return tuple(new)

        init = (jnp.full((ROW_GROUP, 1), -1e30, F32), jnp.zeros((ROW_GROUP, 1), F32), jnp.zeros((ROW_GROUP, LANES), F32))
        def two_blocks(j, c):
            c = block(pl.multiple_of(2 * j * tk, tk), c, None, tk)
            return block(pl.multiple_of((2 * j + 1) * tk, tk), c, None, tk)

        carry = lax.fori_loop(0, i * (tq // tk) // 2, two_blocks, (init,) * len(chains))
        for m in range(tq // tk_diag):
            carry = block(pl.multiple_of(i * tq + m * tk_diag, tk_diag), carry, m, tk_diag)
        per_head = []
        for h in range(2):
            mine = [carry[n] for n, (ch, _) in enumerate(chains) if ch == h]
            per_head.append((jnp.concatenate([acc / l for _, l, acc in mine], axis=0),
                             jnp.concatenate([mm + jnp.log(l) for mm, l, _ in mine], axis=0)))
        o_ref[...] = jnp.where(left, per_head[0][0], per_head[1][0])
        lse_ref[...] = jnp.where(left, per_head[0][1], per_head[1][1])

    return _call_with_rider(
        body, rider, name="mla_fwd", grid=(MLA_HEADS // 2, seq // tq),
        in_specs=[pl.BlockSpec((tq, 2 * LANES), lambda p, i: (i, p)), pl.BlockSpec((seq, 2 * LANES), lambda p, i: (0, p)),
                  pl.BlockSpec((seq, LANES), lambda p, i: (0, p))],
        out_specs=[pl.BlockSpec((tq, LANES), lambda p, i: (i, p)), pl.BlockSpec((tq, LANES), lambda p, i: (i, p))],
        out_shape=[jax.ShapeDtypeStruct((seq, 512), F32), jax.ShapeDtypeStruct((seq, 512), F32)],
        args=(qp, kp, vp), semantics=("parallel", "parallel"))


def _mla_attn_bwd(qp, kp, vp, o, lse, do, *, tq, tk, tk_diag, rider=None):
    seq = qp.shape[0]
    chains = _chains(tq)

    def body(q_ref, k_ref, v_ref, o_ref, lse_ref, do_ref, dq_ref, dk_ref, dv_ref, qt_ref, dot_ref):
        i = pl.program_id(1)

        @pl.when(i == 0)
        def _():
            dk_ref[...] = jnp.zeros_like(dk_ref)
            dv_ref[...] = jnp.zeros_like(dv_ref)

        left = lax.broadcasted_iota(jnp.int32, (tq, LANES), 1) < HALF
        do_f = do_ref[...]
        prod = do_f * o_ref[...]
        lse_v = lse_ref[...]
        do_heads = (_bf(jnp.where(left, do_f, 0.0)), _bf(jnp.where(left, 0.0, do_f)))
        delta_heads = (jnp.sum(jnp.where(left, prod, 0.0), axis=-1, keepdims=True),
                       jnp.sum(jnp.where(left, 0.0, prod), axis=-1, keepdims=True))
        qs = [q_ref[_rows(g), h * LANES:(h + 1) * LANES] for h, g in chains]
        dos = [do_heads[h][_rows(g)] for h, g in chains]
        deltas = [delta_heads[h][_rows(g)] for h, g in chains]
        lses = [lse_v[_rows(g), h * HALF:h * HALF + 1] for h, g in chains]
        for h in range(2):
            qt_ref[h] = q_ref[:, h * LANES:(h + 1) * LANES].T
            dot_ref[h] = do_heads[h].T
        q_t = [qt_ref.at[h] for h in range(2)]
        do_t = [dot_ref.at[h] for h in range(2)]

        def block(start, carry, m, tk):
            v = v_ref[pl.ds(start, tk), :]
            pats = [True if m is None else _chain_pattern(g, m, tk, False) for _, g in chains]
            live = [n for n, p in enumerate(pats) if p is not None]
            ks = [k_ref[pl.ds(start, tk), h * LANES:(h + 1) * LANES] for h in range(2)]
            ss = {n: _dot_nt(qs[n], ks[chains[n][0]]) for n in live}
            dps = {n: _dot_nt(dos[n], v) for n in live}
            new = list(carry)
            ps, dss = {}, {}
            for n in live:
                p = _masked(jnp.exp(ss[n] * MLA_SCALE - lses[n]), pats[n])
                ps[n] = _bf(p)
                dss[n] = _bf(p * (dps[n] - deltas[n]) * MLA_SCALE)
                new[n] = carry[n] + _dot(dss[n], ks[chains[n][0]])
            dv_t, dk_t = None, []
            for h in range(2):
                mine = [n for n in live if chains[n][0] == h]
                first_row = chains[mine[0]][1] * ROW_GROUP
                ds_cat = jnp.concatenate([dss[n] for n in mine], axis=0)
                p_cat = jnp.concatenate([ps[n] for n in mine], axis=0)
                if first_row == 0:
                    q_rows_t, do_rows_t = q_t[h][...], do_t[h][...]
                else:
                    q_rows_t = q_ref[first_row:, h * LANES:(h + 1) * LANES].T
                    do_rows_t = do_heads[h][first_row:].T
                dk_t.append(_dot(q_rows_t, ds_cat))
                term = _dot(do_rows_t, p_cat)
                dv_t = term if dv_t is None else dv_t + term
            back = jnp.concatenate(dk_t + [dv_t], axis=0).T
            dk_ref[pl.ds(start, tk), :] += back[:, :2 * LANES]
            dv_ref[pl.ds(start, tk), :] += back[:, 2 * LANES:]
            return tuple(new)

        zero = jnp.zeros((ROW_GROUP, LANES), F32)
        carry = lax.fori_loop(0, i * (tq // tk), lambda j, c: block(pl.multiple_of(j * tk, tk), c, None, tk),
                              (zero,) * len(chains))
        for m in range(tq // tk_diag):
            carry = block(pl.multiple_of(i * tq + m * tk_diag, tk_diag), carry, m, tk_diag)
        for n, (h, g) in enumerate(chains):
            dq_ref[_rows(g), h * LANES:(h + 1) * LANES] = carry[n]

    two_t = pl.BlockSpec((tq, 2 * LANES), lambda p, i: (i, p))
    two_s = pl.BlockSpec((seq, 2 * LANES), lambda p, i: (0, p))
    pair_t = pl.BlockSpec((tq, LANES), lambda p, i: (i, p))
    pair_s = pl.BlockSpec((seq, LANES), lambda p, i: (0, p))
    return _call_with_rider(
        body, rider, name="mla_bwd", grid=(MLA_HEADS // 2, seq // tq),
        in_specs=[two_t, two_s, pair_s, pair_t, pair_t, pair_t],
        out_specs=[two_t, two_s, pair_s],
        out_shape=[jax.ShapeDtypeStruct((seq, 1024), F32), jax.ShapeDtypeStruct((seq, 1024), F32),
                   jax.ShapeDtypeStruct((seq, 512), F32)],
        args=(qp, kp, vp, o, lse, do), semantics=("parallel", "arbitrary"),
        scratch=[pltpu.VMEM((2, LANES, tq), BF16), pltpu.VMEM((2, LANES, tq), BF16)])


def _sb_attn_fwd(pbf, *, tq, tk):
    seq = pbf.shape[0]
    nd = tq // tk
    qb, kb, vb = BLK_QB * 4, BLK_KB * 4, BLK_VB * 4
    chains = _chains(tq)

    def body(q_ref, k_ref, v_ref, o_ref, tot_ref):
        i = pl.program_id(1)
        u_later = _tri_matrix(tk, lambda r, c: r > c)
        left = lax.broadcasted_iota(jnp.int32, (tq, LANES), 1) < HALF
        q_heads = _head_queries(q_ref[...], left)
        qs = [q_heads[h][_rows(g)] for h, g in chains]

        def block(j, carry, m):
            start = pl.multiple_of(j * tk, tk)
            k = k_ref[pl.ds(start, tk), :]
            v = v_ref[pl.ds(start, tk), :]
            pats = [True if m is None else _chain_pattern(g, m, tk, True) for _, g in chains]
            live = [n for n, p in enumerate(pats) if p is not None]
            zs = {n: _dot_nt(qs[n], k) for n in live}
            raws = {n: _softplus(zs[n]) for n in live}
            sps = {n: _masked(raws[n], pats[n]) for n in live}
            laters = {n: _tri_sum(sps[n], u_later) for n in live}
            new = list(carry)
            for n in live:
                c, acc = carry[n]
                a = _masked(jnp.exp(zs[n] - raws[n] - laters[n] - c), pats[n])
                new[n] = (c + laters[n][:, 0:1] + sps[n][:, 0:1], acc + _dot(_bf(a), v))
            return tuple(new)

        init = (jnp.zeros((ROW_GROUP, 1), F32), jnp.zeros((ROW_GROUP, LANES), F32))
        carry = (init,) * len(chains)
        for m in reversed(range(nd)):
            carry = block(i * nd + m, carry, m)
        per_trip = 4 if nd % 4 == 0 else 2

        def trip(jj, cr):
            for u in range(per_trip):
                cr = block(i * nd - 1 - (per_trip * jj + u), cr, None)
            return cr

        carry = lax.fori_loop(0, i * nd // per_trip, trip, carry)
        per_head = []
        for h in range(2):
            mine = [carry[n] for n, (ch, _) in enumerate(chains) if ch == h]
            per_head.append((jnp.concatenate([acc for _, acc in mine], axis=0), jnp.concatenate([c for c, _ in mine], axis=0)))
        o_ref[...] = jnp.where(left, per_head[0][0], per_head[1][0])
        tot_ref[...] = jnp.where(left, per_head[0][1], per_head[1][1])

    pair_t = pl.BlockSpec((tq, LANES), lambda p, i: (i, p))
    return pl.pallas_call(
        body, name="sb_fwd", grid=(SB_HEADS // 2, seq // tq),
        in_specs=[pl.BlockSpec((tq, LANES), lambda p, i: (i, qb + p)), pl.BlockSpec((seq, LANES), lambda p, i: (0, kb + p)),
                  pl.BlockSpec((seq, LANES), lambda p, i: (0, vb + p))],
        out_specs=[pair_t, pair_t],
        out_shape=[jax.ShapeDtypeStruct((seq, 512), F32), jax.ShapeDtypeStruct((seq, 512), F32)],
        compiler_params=_cparams(("parallel", "parallel")),
    )(pbf, pbf, pbf)


def _sb_attn_bwd(pbf, tot, do, *, tq, tk, rider=None):
    seq = pbf.shape[0]
    nd = tq // tk
    qb, kb, vb = BLK_QB * 4, BLK_KB * 4, BLK_VB * 4
    chains = _chains(tq)
    group = SB_BWD_CHAINS_IN_FLIGHT

    def body(q_ref, k_ref, v_ref, tot_ref, do_ref, dq_ref, dk_ref, dv_ref, qt_ref, dot_ref):
        i = pl.program_id(1)

        @pl.when(i == 0)
        def _():
            dk_ref[...] = jnp.zeros_like(dk_ref)
            dv_ref[...] = jnp.zeros_like(dv_ref)

        u_upto = _tri_matrix(tk, lambda r, c: r <= c)
        u_below = _tri_matrix(tk, lambda r, c: r < c)
        left = lax.broadcasted_iota(jnp.int32, (tq, LANES), 1) < HALF
        q_heads = _head_queries(q_ref[...], left)
        do_f = do_ref[...]
        do_heads = (_bf(jnp.where(left, do_f, 0.0)), _bf(jnp.where(left, 0.0, do_f)))
        tot_v = tot_ref[...]
        qs = [q_heads[h][_rows(g)] for h, g in chains]
        dos = [do_heads[h][_rows(g)] for h, g in chains]
        totals = [tot_v[_rows(g), h * HALF:h * HALF + 1] for h, g in chains]
        qt_ref[...] = jnp.concatenate(qs, axis=0).T
        dot_ref[...] = jnp.concatenate(dos, axis=0).T

        def block(j, carry, m):
            start = pl.multiple_of(j * tk, tk)
            k = k_ref[pl.ds(start, tk), :]
            v = v_ref[pl.ds(start, tk), :]
            pats = [True if m is None else _chain_pattern(g, m, tk, True) for _, g in chains]
            all_live = [n for n, p in enumerate(pats) if p is not None]
            new = list(carry)
            for g0 in range(0, len(all_live), group):
                live = all_live[g0:g0 + group]
                zs = {n: _dot_nt(qs[n], k) for n in live}
                das = {n: _dot_nt(dos[n], v) for n in live}
                raws = {n: _softplus(zs[n]) for n in live}
                sps = {n: _masked(raws[n], pats[n]) for n in live}
                uptos = {n: _tri_sum(sps[n], u_upto) for n in live}
                lbs, a_s, gs = {}, {}, {}
                for n in live:
                    lbs[n] = zs[n] - raws[n]
                    a = _masked(jnp.exp(lbs[n] - (totals[n] - carry[n][0] - uptos[n])), pats[n])
                    a_s[n] = _bf(a)
                    gs[n] = das[n] * a
                belows = {n: _dot(_bf(gs[n]), u_below) for n in live}
                dzs = {}
                for n in live:
                    sp_before, g_before, dq_acc = carry[n]
                    beta = jnp.exp(lbs[n])
                    dz = _masked(gs[n] * (1.0 - beta) - (g_before + belows[n]) * beta, pats[n])
                    dzs[n] = _bf(dz)
                    new[n] = (sp_before + uptos[n][:, tk - 1:tk], g_before + belows[n][:, tk - 1:tk] + gs[n][:, tk - 1:tk],
                              dq_acc + _dot(dzs[n], k))
                dz_cat = jnp.concatenate([dzs[n] for n in live], axis=0)
                a_cat = jnp.concatenate([a_s[n] for n in live], axis=0)
                if len(live) == len(chains):
                    q_rows_t, do_rows_t = qt_ref[...], dot_ref[...]
                else:
                    q_rows_t = jnp.concatenate([qs[n] for n in live], axis=0).T
                    do_rows_t = jnp.concatenate([dos[n] for n in live], axis=0).T
                both = jnp.concatenate([_dot(q_rows_t, dz_cat), _dot(do_rows_t, a_cat)], axis=0).T
                dk_ref[pl.ds(start, tk), :] += both[:, :LANES]
                dv_ref[pl.ds(start, tk), :] += both[:, LANES:]
            return tuple(new)

        zero = jnp.zeros((ROW_GROUP, 1), F32)
        init = (zero, zero, jnp.zeros((ROW_GROUP, LANES), F32))
        carry = lax.fori_loop(0, i * nd // 2, lambda j, cr: block(2 * j + 1, block(2 * j, cr, None), None),
                              (init,) * len(chains))
        for m in range(nd):
            carry = block(i * nd + m, carry, m)
        per_head = [jnp.concatenate([carry[n][2] for n, (ch, _) in enumerate(chains) if ch == h], axis=0) for h in range(2)]
        dq_ref[...] = jnp.where(left, per_head[0], per_head[1]) * SB_SCALE

    pair_t = pl.BlockSpec((tq, LANES), lambda p, i: (i, p))
    pair_s = pl.BlockSpec((seq, LANES), lambda p, i: (0, p))
    return _call_with_rider(
        body, rider, name="sb_bwd", grid=(SB_HEADS // 2, seq // tq),
        in_specs=[pl.BlockSpec((tq, LANES), lambda p, i: (i, qb + p)), pl.BlockSpec((seq, LANES), lambda p, i: (0, kb + p)),
                  pl.BlockSpec((seq, LANES), lambda p, i: (0, vb + p)), pair_t, pair_t],
        out_specs=[pair_t, pair_s, pair_s],
        out_shape=[jax.ShapeDtypeStruct((seq, 512), F32)] * 3,
        args=(pbf, pbf, pbf, tot, do), semantics=("parallel", "arbitrary"),
        scratch=[pltpu.VMEM((LANES, 2 * tq), BF16), pltpu.VMEM((LANES, 2 * tq), BF16)])


def _mem_probs(s):
    e = jnp.exp(s - jnp.max(s, axis=-1, keepdims=True))
    return e / jnp.sum(e, axis=-1, keepdims=True)


def _head_lanes(h):
    return slice(h * LANES, (h + 1) * LANES)


def _mem_fwd(pbf, mkv, *, t):
    seq = pbf.shape[0]

    def body(q_ref, kv_ref, o_ref):
        ss = [_dot_nt(q_ref[:, _head_lanes(h)], kv_ref[:, _head_lanes(h)]) * MEM_SCALE for h in range(MEM_HEADS)]
        ps = [_bf(_mem_probs(s)) for s in ss]
        for h, p in enumerate(ps):
            o_ref[:, _head_lanes(h)] = _dot(p, kv_ref[:, _head_lanes(MEM_HEADS + h)])

    return pl.pallas_call(
        body, name="mem_fwd", grid=(seq // t,),
        in_specs=[pl.BlockSpec((t, 512), lambda i: (i, BLK_QM)), pl.BlockSpec((MEM_LEN, 1024), lambda i: (0, 0))],
        out_specs=pl.BlockSpec((t, 512), lambda i: (i, 0)),
        out_shape=jax.ShapeDtypeStruct((seq, 512), F32),
        compiler_params=_cparams(("parallel",)),
    )(pbf, mkv)


def _mem_bwd(pbf, mkv, do, *, t):
    seq = pbf.shape[0]

    def body(q_ref, kv_ref, do_ref, dq_ref, dkv_ref):
        @pl.when(pl.program_id(0) == 0)
        def _():
            dkv_ref[...] = jnp.zeros_like(dkv_ref)

        heads = range(MEM_HEADS)
        qs = [q_ref[:, _head_lanes(h)] for h in heads]
        ks = [kv_ref[:, _head_lanes(h)] for h in heads]
        dos = [_bf(do_ref[:, _head_lanes(h)]) for h in heads]
        ss = [_dot_nt(qs[h], ks[h]) * MEM_SCALE for h in heads]
        dps = [_dot_nt(dos[h], kv_ref[:, _head_lanes(MEM_HEADS + h)]) for h in heads]
        ps = [_mem_probs(s) for s in ss]
        dss = [_bf(ps[h] * (dps[h] - jnp.sum(dps[h] * ps[h], axis=-1, keepdims=True)) * MEM_SCALE) for h in heads]
        for h in heads:
            dq_ref[:, _head_lanes(h)] = _dot(dss[h], ks[h])
        for h in heads:
            dkv_ref[:, _head_lanes(h)] += _dot_tn(dss[h], qs[h])
            dkv_ref[:, _head_lanes(MEM_HEADS + h)] += _dot_tn(_bf(ps[h]), dos[h])

    return pl.pallas_call(
        body, name="mem_bwd", grid=(seq // t,),
        in_specs=[pl.BlockSpec((t, 512), lambda i: (i, BLK_QM)), pl.BlockSpec((MEM_LEN, 1024), lambda i: (0, 0)),
                  pl.BlockSpec((t, 512), lambda i: (i, 0))],
        out_specs=[pl.BlockSpec((t, 512), lambda i: (i, 0)), pl.BlockSpec((MEM_LEN, 1024), lambda i: (0, 0))],
        out_shape=[jax.ShapeDtypeStruct((seq, 512), F32), jax.ShapeDtypeStruct((MEM_LEN, 1024), F32)],
        compiler_params=_cparams(("arbitrary",)),
    )(pbf, mkv, do)


def _mid(x, tgt, o_a, o_b, o_m, p32, wmg, bmg, wba, wbb, wbm, wout, ln_g, ln_b, *, t):
    seq = x.shape[0]
    inv_d = 1.0 / D_MODEL

    def body(x_ref, t_ref, oa_ref, ob_ref, om_ref, ga_ref, gb_ref, gm_ref, wmg_ref, bmg_ref, wba_ref, wbb_ref,
             wbm_ref, wout_ref, lg_ref, lb_ref,
             du_ref, mrg_ref, dgp_ref, ha_ref, hb_ref, hm_ref, dya_ref, dyb_ref, dym_ref, doa_ref, dob_ref, dom_ref,
             dga_ref, dgb_ref, dgm_ref, dgain_ref, dbias_ref, dbmg_ref, loss_ref):
        @pl.when(pl.program_id(0) == 0)
        def _():
            dgain_ref[...] = jnp.zeros_like(dgain_ref)
            dbias_ref[...] = jnp.zeros_like(dbias_ref)
            dbmg_ref[...] = jnp.zeros_like(dbmg_ref)
            loss_ref[...] = jnp.zeros_like(loss_ref)

        xv = x_ref[...]
        gate = _sigmoid(_dot_cols(_bf(xv), wmg_ref) + bmg_ref[...])

        branches = []
        merged = None
        for b, (o_ref, g_ref, w_ref, h_ref) in enumerate(((oa_ref, ga_ref, wba_ref, ha_ref), (ob_ref, gb_ref, wbb_ref, hb_ref),
                                                         (om_ref, gm_ref, wbm_ref, hm_ref))):
            o, gt = o_ref[...], g_ref[...]
            sg = _sigmoid(gt)
            silu = gt * sg
            h = _bf(o * silu)
            h_ref[...] = h
            y = _dot_cols(h, w_ref)
            g_b = gate[:, b * D_MODEL:(b + 1) * D_MODEL]
            term = g_b * y
            merged = term if merged is None else merged + term
            branches.append((o, gt, sg, silu, y, g_b))
        mrg_b = _bf(merged)
        mrg_ref[...] = mrg_b

        u = DEEPNORM_ALPHA * xv + _dot(mrg_b, wout_ref[...])
        mu = jnp.mean(u, axis=-1, keepdims=True)
        uc = u - mu
        rstd = lax.rsqrt(jnp.mean(uc * uc, axis=-1, keepdims=True) + LN_EPS)
        xhat = uc * rstd
        lg = lg_ref[...]
        y_out = xhat * lg + lb_ref[...]
        err = y_out - t_ref[...]
        loss_ref[...] += 0.5 * jnp.sum(jnp.mean(err * err, axis=-1, keepdims=True), axis=0, keepdims=True)
        dy = err * inv_d
        dgain_ref[...] += jnp.sum(dy * xhat, axis=0, keepdims=True)
        dbias_ref[...] += jnp.sum(dy, axis=0, keepdims=True)
        dxh = dy * lg
        du = rstd * (dxh - jnp.mean(dxh, axis=-1, keepdims=True) - xhat * jnp.mean(dxh * xhat, axis=-1, keepdims=True))
        du_ref[...] = du

        dmerged = _dot_nt(_bf(du), wout_ref[...])
        outs = ((dya_ref, doa_ref, dga_ref, wba_ref), (dyb_ref, dob_ref, dgb_ref, wbb_ref), (dym_ref, dom_ref, dgm_ref, wbm_ref))
        dgp = []
        for (o, gt, sg, silu, y, g_b), (dy_ref, do_ref, dg_ref, w_ref) in zip(branches, outs):
            dyb = _bf(dmerged * g_b)
            dy_ref[...] = dyb
            dgp.append(dmerged * y * g_b * (1.0 - g_b))
            dh = _dot_nt_cols(dyb, w_ref)
            do_ref[...] = dh * silu
            dg_ref[...] = _bf(dh * o * (sg * (1.0 + gt * (1.0 - sg))))
        dgp = jnp.concatenate(dgp, axis=1)
        dgp_ref[...] = _bf(dgp)
        dbmg_ref[...] += jnp.sum(dgp, axis=0, keepdims=True)

    row = lambda w: pl.BlockSpec((t, w), lambda i: (i, 0))
    pblk = lambda c: pl.BlockSpec((t, 512), lambda i: (i, c))
    full = lambda shp: pl.BlockSpec(shp, lambda i: (0,) * len(shp))
    sds = jax.ShapeDtypeStruct
    return pl.pallas_call(
        body, name="mid", grid=(seq // t,),
        in_specs=[row(1024), row(1024), row(512), row(512), row(512), pblk(BLK_GATE_A), pblk(BLK_GATE_B), pblk(BLK_GATE_M),
                  full(wmg.shape), full((1, N_MERGE)), full(wba.shape), full(wbb.shape), full(wbm.shape), full(wout.shape),
                  full((1, D_MODEL)), full((1, D_MODEL))],
        out_specs=[row(1024), row(1024), row(N_MERGE), row(512), row(512), row(512), row(1024), row(1024), row(1024),
                   row(512), row(512), row(512), row(512), row(512), row(512),
                   full((1, D_MODEL)), full((1, D_MODEL)), full((1, N_MERGE)), full((1, 1))],
        out_shape=[sds((seq, 1024), F32), sds((seq, 1024), BF16), sds((seq, N_MERGE), BF16),
                   sds((seq, 512), BF16), sds((seq, 512), BF16), sds((seq, 512), BF16),
                   sds((seq, 1024), BF16), sds((seq, 1024), BF16), sds((seq, 1024), BF16),
                   sds((seq, 512), F32), sds((seq, 512), F32), sds((seq, 512), F32),
                   sds((seq, 512), BF16), sds((seq, 512), BF16), sds((seq, 512), BF16),
                   sds((1, D_MODEL), F32), sds((1, D_MODEL), F32), sds((1, N_MERGE), F32), sds((1, 1), F32)],
        compiler_params=_cparams(("arbitrary",)),
    )(x, tgt, o_a, o_b, o_m, p32, p32, p32, wmg, bmg, wba, wbb, wbm, wout, ln_g, ln_b)


def _primed_weights(w):
    w_in = w["w_in"]
    zc = lambda n: jnp.zeros((D_MODEL, n), w_in.dtype)
    w_in_p = jnp.concatenate([w_in[:, 0:384], zc(64), w_in[:, 384:416], zc(32), w_in[:, 416:]], axis=1)
    wqb = jnp.pad(w["w_q_b"].reshape(MLA_Q_LORA, MLA_HEADS, 96), ((0, 0), (0, 0), (0, 32))).reshape(MLA_Q_LORA, 1024)
    kv3 = w["w_kv_b"].reshape(MLA_KV_LORA, MLA_HEADS, 128)
    wk = jnp.pad(kv3[:, :, :MLA_NOPE], ((0, 0), (0, 0), (0, 64))).reshape(MLA_KV_LORA, 1024)
    wv = kv3[:, :, MLA_NOPE:].reshape(MLA_KV_LORA, 512)
    return w_in_p, wqb, jnp.concatenate([wk, wv], axis=1)


PROJ_BLK = 512


def _grad_x(du, dgpre, wmg, d_proj, w_in_p, *, tm, rider=None):
    seq = du.shape[0]
    n_pieces = len(d_proj)

    def body(du_ref, dg_ref, wmg_ref, *rest):
        piece_refs, win_ref, out_ref = rest[:n_pieces], rest[n_pieces], rest[n_pieces + 1]
        d_p = jnp.concatenate([_bf(p_ref[...]) for p_ref in piece_refs], axis=1)
        out_ref[...] = (DEEPNORM_ALPHA * du_ref[...] + _dot_nt_cols(dg_ref[...], wmg_ref)) + _dot_nt(d_p, win_ref[...])

    row = lambda w: pl.BlockSpec((tm, w), lambda i: (i, 0))
    whole = lambda a: pl.BlockSpec(a.shape, lambda i: (0,) * a.ndim)
    return _call_with_rider(
        body, rider, name="grad_x", grid=(seq // tm,),
        in_specs=[row(D_MODEL), row(N_MERGE), whole(wmg)] + [row(PROJ_BLK) for _ in d_proj] + [whole(w_in_p)],
        out_specs=[row(D_MODEL)], out_shape=[jax.ShapeDtypeStruct((seq, D_MODEL), F32)],
        args=(du, dgpre, wmg, *d_proj, w_in_p), semantics=("parallel",))


def _grad_w_in(x, d_proj, *, tk):
    seq = x.shape[0]
    n_pieces = len(d_proj)
    nk = seq // tk

    def body(x_ref, *rest):
        piece_refs, out_ref, acc = rest[:n_pieces], rest[n_pieces], rest[n_pieces + 1]
        j, kk = pl.program_id(0), pl.program_id(1)

        @pl.when(kk == 0)
        def _():
            acc[...] = jnp.zeros_like(acc)

        xb = _bf(x_ref[...])
        for pair in range(n_pieces // 2):
            @pl.when(j == pair)
            def _(pair=pair):
                both = jnp.concatenate([_bf(piece_refs[2 * pair][...]), _bf(piece_refs[2 * pair + 1][...])], axis=1)
                acc[...] += _dot_tn(xb, both)

        @pl.when(kk == nk - 1)
        def _():
            out_ref[...] = acc[...]

    def piece_spec(s):
        return pl.BlockSpec((tk, PROJ_BLK), lambda j, kk: (jnp.where(j == s // 2, kk, 0), 0))

    return pl.pallas_call(
        body, name="grad_w_in", grid=(n_pieces // 2, nk),
        in_specs=[pl.BlockSpec((tk, D_MODEL), lambda j, kk: (kk, 0))] + [piece_spec(s) for s in range(n_pieces)],
        out_specs=pl.BlockSpec((D_MODEL, 2 * PROJ_BLK), lambda j, kk: (0, j)),
        out_shape=jax.ShapeDtypeStruct((D_MODEL, n_pieces * PROJ_BLK), F32),
        scratch_shapes=[pltpu.VMEM((D_MODEL, 2 * PROJ_BLK), F32)],
        compiler_params=_cparams(("parallel", "arbitrary")),
    )(x, *d_proj)


EARLY_NAMES = ("w_mem_kv", "w_branch_mla", "w_branch_sb", "w_branch_mem", "w_merge_gate", "w_out")
LATE_NAMES = ("w_in", "w_q_b", "w_kv_b")


def _remote(src, dst, send_sem, recv_sem, device):
    return pltpu.make_async_remote_copy(src_ref=src, dst_ref=dst, send_sem=send_sem, recv_sem=recv_sem, device_id=device,
                                        device_id_type=MESH_ID)


def _gather_rider(shards):
    n = len(shards)

    def copies(src_refs, out_refs, sems):
        send_sems, recv_sems, local_sems = sems
        x, y, c = _place()
        me = 2 * x + y
        out = []
        for a, (s, o) in enumerate(zip(src_refs, out_refs)):
            out.append(pltpu.make_async_copy(s, o.at[me], local_sems.at[a]))
            for k, (px, py) in enumerate(_other_chips(x, y)):
                out.append(_Exchange(_remote(s, o.at[me], send_sems.at[k, a], recv_sems.at[k, a], (px, py, c)),
                                     _remote(s, o.at[2 * px + py], send_sems.at[k, a], recv_sems.at[k, a], (px, py, c))))
        return out

    return _Rider(shards, [jax.ShapeDtypeStruct((N_CHIPS,) + s.shape, s.dtype) for s in shards],
                  [pltpu.SemaphoreType.DMA((3, n)), pltpu.SemaphoreType.DMA((3, n)), pltpu.SemaphoreType.DMA((n,))], copies)


def _sibling_rider(g4):
    n = len(g4)

    def copies(g_refs, out_refs, sems):
        send_sems, recv_sems = sems
        x, y, c = _place()
        out = []
        for a, (g, o) in enumerate(zip(g_refs, out_refs)):
            half = g.shape[1] // 2
            theirs = pl.ds(pl.multiple_of((1 - c) * half, 8), half)
            cp = _remote(g.at[:, theirs, :], o, send_sems.at[a], recv_sems.at[a], (x, y, 1 - c))
            out.append(_Exchange(cp, cp))
        return out

    return _Rider(g4, [jax.ShapeDtypeStruct((N_CHIPS, g.shape[1] // 2, g.shape[2]), g.dtype) for g in g4],
                  [pltpu.SemaphoreType.DMA((n,)), pltpu.SemaphoreType.DMA((n,))], copies)


def _chips_rider(wire):
    n = len(wire)

    def copies(s_refs, out_refs, sems):
        send_sems, recv_sems = sems
        x, y, c = _place()
        out = []
        for a, (s, o) in enumerate(zip(s_refs, out_refs)):
            for k, (px, py) in enumerate(_other_chips(x, y)):
                cp = _remote(s.at[2 * px + py], o.at[RELATION_XOR[k] - 1], send_sems.at[k, a], recv_sems.at[k, a], (px, py, c))
                out.append(_Exchange(cp, cp))
        return out

    return _Rider(wire, [jax.ShapeDtypeStruct((3,) + s.shape[1:], s.dtype) for s in wire],
                  [pltpu.SemaphoreType.DMA((3, n)), pltpu.SemaphoreType.DMA((3, n))], copies)


def _local_step(x, mem, tgt, w, small, *, tq, tq_sb_bwd, tk, tk_mla, t_row, t_mm, t_wg, rest_shards=None):
    seq = x.shape[0]
    on_mesh = rest_shards is not None
    w_in_p, wqb, wkvb = _primed_weights(w)
    tabs = _rope_tables(seq)

    p32, pbf = _matmul(x, w_in_p, mode="nn", tm=256, tn=IN_WIDTH_P, tk=D_MODEL, out_dtypes=(F32, BF16), name="proj_in")
    qp, kp, vp = _mla_prep(p32, small["q_a_gain"], small["kv_a_gain"], wqb, wkvb, tabs, t=t_row)
    res = _mla_attn_fwd(qp, kp, vp, tq=tq, tk=tk_mla, tk_diag=tk, rider=_gather_rider(rest_shards) if on_mesh else None)
    o_a, lse = res[0], res[1]
    if on_mesh:
        w = dict(w, **{n: g if n in COL_SHARDED else _join_chips(n, g) for n, g in zip(EARLY_NAMES, res[2:])})
    else:
        w = dict(w, **{n: _split_by_chip(n, w[n]) for n in EARLY_NAMES if n in COL_SHARDED})
    wmg, wout = w["w_merge_gate"], w["w_out"]
    wba, wbb, wbm = w["w_branch_mla"], w["w_branch_sb"], w["w_branch_mem"]
    o_b, keep_total = _sb_attn_fwd(pbf, tq=tq, tk=tk)
    (mkv,) = _matmul(mem, w["w_mem_kv"], mode="nn", tm=MEM_LEN, tn=512, tk=D_MODEL, out_dtypes=(BF16,), name="mem_kv")
    o_m = _mem_fwd(pbf, mkv, t=t_row)

    (du, merged, dgpre, h_a, h_b, h_m, dy_a, dy_b, dy_m, do_a, do_b, do_m, dgate_a, dgate_b, dgate_m,
     d_ln_g, d_ln_b, d_bmg, loss) = _mid(x, tgt, o_a, o_b, o_m, p32, wmg, small["b_merge_gate"], wba, wbb, wbm, wout,
                                         small["ln_gain"], small["ln_bias"], t=t_row)

    wg = functools.partial(_matmul, mode="tn", tm=512, out_dtypes=(F32,))
    shard = (lambda width: dict(tn=width // N_CHIPS, by_column_block=True)) if on_mesh else (lambda width: dict(tn=1024))
    dq_m, dmkv = _mem_bwd(pbf, mkv, do_m, t=t_row)
    early = {"w_mem_kv": wg(mem, dmkv, tk=MEM_LEN, tn=1024, name="grad_w_mem_kv")[0],
             "w_branch_mla": wg(h_a, dy_a, tk=t_wg, name="grad_w_branch_a", **shard(D_MODEL))[0],
             "w_branch_sb": wg(h_b, dy_b, tk=t_wg, name="grad_w_branch_b", **shard(D_MODEL))[0],
             "w_branch_mem": wg(h_m, dy_m, tk=t_wg, name="grad_w_branch_m", **shard(D_MODEL))[0],
             "w_merge_gate": wg(x, dgpre, tk=t_wg, name="grad_w_merge_gate", **shard(N_MERGE))[0],
             "w_out": wg(merged, du, tk=t_wg, tn=1024, name="grad_w_out")[0]}

    if on_mesh:
        g4 = [early[n] if early[n].ndim == 3 else _split_by_chip(n, early[n]) for n in EARLY_NAMES]
        res = _mla_attn_bwd(qp, kp, vp, o_a, lse, do_a, tq=tq, tk=tk_mla, tk_diag=tk, rider=_sibling_rider(g4))
        (dqp, dkp, dvp), got = res[:3], res[3:]
        chipsum, wire = _rs_add_sibling(g4, got, [BF16] * len(g4))
        res = _sb_attn_bwd(pbf, keep_total, do_b, tq=tq_sb_bwd, tk=tk, rider=_chips_rider(wire))
        (dq_b, dk_b, dv_b), parts = res[:3], res[3:]
        early = _rs_add_chips(chipsum, parts)
    else:
        dqp, dkp, dvp = _mla_attn_bwd(qp, kp, vp, o_a, lse, do_a, tq=tq, tk=tk_mla, tk_diag=tk)
        dq_b, dk_b, dv_b = _sb_attn_bwd(pbf, keep_total, do_b, tq=tq_sb_bwd, tk=tk)
    dlat, d_wqb, d_wkvb, d_gq, d_gkv = _mla_post(p32, dqp, dkp, dvp, small["q_a_gain"], small["kv_a_gain"], wqb, wkvb, tabs,
                                                 t=t_row)

    d_proj = [dlat, dgate_a, dq_b, dk_b, dv_b, dgate_b, dq_m, dgate_m]
    d_winp = _grad_w_in(x, d_proj, tk=min(1024, seq))

    d_win = jnp.concatenate([d_winp[:, 0:384], d_winp[:, 448:480], d_winp[:, 512:]], axis=1)
    d_wq = d_wqb.reshape(MLA_Q_LORA, MLA_HEADS, 128)[:, :, :96].reshape(MLA_Q_LORA, 768)
    d_wk = d_wkvb[:, :1024].reshape(MLA_KV_LORA, MLA_HEADS, 128)[:, :, :MLA_NOPE]
    d_wv = d_wkvb[:, 1024:].reshape(MLA_KV_LORA, MLA_HEADS, MLA_V)
    d_wkv = jnp.concatenate([d_wk, d_wv], axis=2).reshape(MLA_KV_LORA, 1024)
    late = {"w_in": d_win, "w_q_b": d_wq, "w_kv_b": d_wkv}
    small_grads = {"q_a_gain": d_gq, "kv_a_gain": d_gkv, "b_merge_gate": d_bmg, "ln_gain": d_ln_g, "ln_bias": d_ln_b}
    if not on_mesh:
        (grad_x,) = _grad_x(du, dgpre, wmg, d_proj, w_in_p, tm=256)
        return loss[0, 0], grad_x, late, small_grads, early

    g4 = [_split_by_chip(n, late[n]) for n in LATE_NAMES]
    g4.append(jnp.broadcast_to(_pack_small(small_grads, scalar=loss)[None], (N_CHIPS, SMALL_ROWS, PACK_COLS)))
    got = _rs_to_sibling(g4)
    chipsum, wire = _rs_add_sibling(g4, got, [BF16] * len(LATE_NAMES) + [F32])
    res = _grad_x(du, dgpre, wmg, d_proj, w_in_p, tm=256, rider=_chips_rider(wire))
    late_mine = _rs_add_chips(chipsum, res[1:])
    return loss[0, 0], res[0], late_mine, None, early


def _place():
    x, y, c = lax.axis_index("x"), lax.axis_index("y"), lax.axis_index("c")
    return x, y, c


def _other_chips(x, y):
    return ((1 - x, y), (x, 1 - y), (1 - x, 1 - y))


SMALL_ROWS = 64
ADAM_STEPS_PER_HALF = 4


def _pack_small(d, scalar=None):
    parts = [d[n].reshape(-1) for n, _ in SMALL_SIZES] + ([] if scalar is None else [scalar.reshape(1)])
    flat = jnp.concatenate(parts)
    return jnp.pad(flat, (0, SMALL_ROWS * PACK_COLS - flat.shape[0])).reshape(SMALL_ROWS, PACK_COLS)


def _unpack_small(a):
    flat, out, c0 = a.reshape(-1), {}, 0
    for n, size in SMALL_SIZES:
        out[n] = flat[c0:c0 + size].reshape(1, size)
        c0 += size
    return out


def _split_by_chip(name, full):
    r, c = full.shape
    if name in COL_SHARDED:
        return full.reshape(r, N_CHIPS, c // N_CHIPS).transpose(1, 0, 2)
    return full.reshape(N_CHIPS, r // N_CHIPS, c)


def _join_chips(name, slots):
    _, r, cs = slots.shape
    if name in COL_SHARDED:
        return slots.transpose(1, 0, 2).reshape(r, N_CHIPS * cs)
    return slots.reshape(N_CHIPS * r, cs)


HBM_SPEC = pl.BlockSpec(memory_space=pltpu.HBM)


def _gather_shards(shards):
    n = len(shards)

    def body(*refs):
        w_refs, out_refs, wb_refs = refs[:n], refs[n:2 * n], refs[2 * n:3 * n]
        send_sems, recv_sems, pass_send_sems, pass_recv_sems, local_sems = refs[3 * n:]
        x, y, c = _place()
        me = 2 * x + y
        sibling = (x, y, 1 - c)

        def halves(ref):
            half = ref.shape[-2] // 2
            return (pl.ds(pl.multiple_of(c * half, 16), half), pl.ds(pl.multiple_of((1 - c) * half, 16), half))
        for w_ref, wb_ref in zip(w_refs, wb_refs):
            rows = w_ref.shape[0]
            chunk = min(rows, 128)

            def cast(i, carry, w_ref=w_ref, wb_ref=wb_ref, chunk=chunk):
                r0 = pl.multiple_of(i * chunk, chunk)
                wb_ref[pl.ds(r0, chunk), :] = _bf(w_ref[pl.ds(r0, chunk), :])
                return carry

            lax.fori_loop(0, rows // chunk, cast, 0)
        sends, locals_ = [], []
        for a, (wb_ref, out_ref) in enumerate(zip(wb_refs, out_refs)):
            mine = pltpu.make_async_copy(wb_ref, out_ref.at[me], local_sems.at[a])
            mine.start()
            locals_.append(mine)
            mine_rows, _ = halves(wb_ref)
            for k, (px, py) in enumerate(_other_chips(x, y)):
                cp = pltpu.make_async_remote_copy(src_ref=wb_ref.at[mine_rows, :], dst_ref=out_ref.at[me, mine_rows, :],
                                                  send_sem=send_sems.at[k, a], recv_sem=recv_sems.at[k, a],
                                                  device_id=(px, py, c), device_id_type=MESH_ID)
                cp.start()
                sends.append(cp)
        for a, (wb_ref, out_ref) in enumerate(zip(wb_refs, out_refs)):
            mine_rows, _ = halves(wb_ref)
            for k, (px, py) in enumerate(_other_chips(x, y)):
                landed = out_ref.at[2 * px + py, mine_rows, :]
                pltpu.make_async_remote_copy(src_ref=wb_ref.at[mine_rows, :], dst_ref=landed, send_sem=send_sems.at[k, a],
                                             recv_sem=recv_sems.at[k, a], device_id=(px, py, c),
                                             device_id_type=MESH_ID).wait_recv()
                cp = pltpu.make_async_remote_copy(src_ref=landed, dst_ref=landed, send_sem=pass_send_sems.at[k, a],
                                                  recv_sem=pass_recv_sems.at[k, a], device_id=sibling, device_id_type=MESH_ID)
                cp.start()
                sends.append(cp)
        for a, (wb_ref, out_ref) in enumerate(zip(wb_refs, out_refs)):
            _, their_rows = halves(wb_ref)
            for k, (px, py) in enumerate(_other_chips(x, y)):
                passed = out_ref.at[2 * px + py, their_rows, :]
                pltpu.make_async_remote_copy(src_ref=passed, dst_ref=passed, send_sem=pass_send_sems.at[k, a],
                                             recv_sem=pass_recv_sems.at[k, a], device_id=sibling,
                                             device_id_type=MESH_ID).wait_recv()
        for cp in sends:
            cp.wait_send()
        for cp in locals_:
            cp.wait()

    return pl.pallas_call(
        body, name="gather_weights",
        in_specs=[pl.BlockSpec(memory_space=pltpu.VMEM)] * n,
        out_specs=[HBM_SPEC] * n,
        out_shape=[jax.ShapeDtypeStruct((N_CHIPS,) + s.shape, BF16) for s in shards],
        scratch_shapes=[pltpu.VMEM(s.shape, BF16) for s in shards]
        + [pltpu.SemaphoreType.DMA((3, n))] * 4 + [pltpu.SemaphoreType.DMA((n,))],
        compiler_params=pltpu.CompilerParams(vmem_limit_bytes=VMEM_LIMIT),
    )(*shards)


def _cast_bf16_list(arrays):
    def body(*refs):
        for a_ref, o_ref in zip(refs[:len(arrays)], refs[len(arrays):]):
            o_ref[...] = _bf(a_ref[...])

    specs = [pl.BlockSpec((a.shape[0] // 4, a.shape[1]), lambda i: (i, 0)) for a in arrays]
    return pl.pallas_call(
        body, name="cast_shards", grid=(4,), in_specs=specs, out_specs=specs,
        out_shape=[jax.ShapeDtypeStruct(a.shape, BF16) for a in arrays],
        compiler_params=_cparams(("parallel",)),
    )(*arrays)


def _rs_to_sibling(g4):
    n = len(g4)

    def body(*refs):
        g_refs, out_refs = refs[:n], refs[n:2 * n]
        send_sems, recv_sems = refs[2 * n:]
        x, y, c = _place()
        copies = []
        for a, (g_ref, out_ref) in enumerate(zip(g_refs, out_refs)):
            half = g_ref.shape[1] // 2
            theirs = pl.ds(pl.multiple_of((1 - c) * half, 8), half)
            copies.append(pltpu.make_async_remote_copy(src_ref=g_ref.at[:, theirs, :], dst_ref=out_ref, send_sem=send_sems.at[a],
                                                       recv_sem=recv_sems.at[a], device_id=(x, y, 1 - c),
                                                       device_id_type=MESH_ID))
        for cp in copies:
            cp.start()
        for cp in copies:
            cp.wait()

    return pl.pallas_call(
        body, name="rs_sibling", in_specs=[HBM_SPEC] * n, out_specs=[HBM_SPEC] * n,
        out_shape=[jax.ShapeDtypeStruct((N_CHIPS, g.shape[1] // 2, g.shape[2]), F32) for g in g4],
        scratch_shapes=[pltpu.SemaphoreType.DMA((n,)), pltpu.SemaphoreType.DMA((n,))],
    )(*g4)


def _rs_add_sibling(g4, got, wire_dtypes):
    n = len(g4)
    narrow = [a for a in range(n) if wire_dtypes[a] != F32]

    def body(c_ref, *refs):
        outs = refs[2 * n:3 * n]
        wires = dict(zip(narrow, refs[3 * n:]))
        for a, (g_ref, r_ref, o_ref) in enumerate(zip(refs[:n], refs[n:2 * n], outs)):
            s = g_ref[...] + r_ref[...]
            o_ref[...] = s
            if a in wires:
                wires[a][...] = s.astype(wires[a].dtype)

    blk = lambda r: (1, r.shape[1], r.shape[2])
    plain = lambda r: pl.BlockSpec(blk(r), lambda j, c_ref: (j, 0, 0))
    grid_spec = pltpu.PrefetchScalarGridSpec(
        num_scalar_prefetch=1, grid=(N_CHIPS,),
        in_specs=[pl.BlockSpec(blk(r), lambda j, c_ref: (j, c_ref[0], 0)) for r in got] + [plain(r) for r in got],
        out_specs=[plain(r) for r in got] + [plain(got[a]) for a in narrow])
    res = pl.pallas_call(
        body, name="rs_add_sibling", grid_spec=grid_spec,
        out_shape=[jax.ShapeDtypeStruct(r.shape, F32) for r in got]
        + [jax.ShapeDtypeStruct(got[a].shape, wire_dtypes[a]) for a in narrow],
        compiler_params=_cparams(("parallel",)),
    )(lax.axis_index("c").astype(jnp.int32).reshape(1), *g4, *got)
    chipsum = list(res[:n])
    wire = list(chipsum)
    for a, w in zip(narrow, res[n:]):
        wire[a] = w
    return chipsum, wire


RELATION_XOR = (2, 1, 3)


def _rs_add_chips(chipsum, parts):
    n = len(parts)

    def body(me_ref, *refs):
        me = me_ref[0]
        for s_ref, p_ref, o_ref in zip(refs[:n], refs[n:2 * n], refs[2 * n:]):
            own = s_ref[0]
            total = None
            for k in range(N_CHIPS):
                theirs = p_ref[jnp.maximum(jnp.bitwise_xor(me, k) - 1, 0)].astype(F32)
                term = jnp.where(me == k, own, theirs)
                total = term if total is None else total + term
            o_ref[...] = total

    grid_spec = pltpu.PrefetchScalarGridSpec(
        num_scalar_prefetch=1, grid=(2,),
        in_specs=[pl.BlockSpec((1, p.shape[1] // 2, p.shape[2]), lambda i, me_ref: (me_ref[0], i, 0)) for p in parts]
        + [pl.BlockSpec((3, p.shape[1] // 2, p.shape[2]), lambda i, me_ref: (0, i, 0)) for p in parts],
        out_specs=[pl.BlockSpec((p.shape[1] // 2, p.shape[2]), lambda i, me_ref: (i, 0)) for p in parts])
    me = (2 * lax.axis_index("x") + lax.axis_index("y")).astype(jnp.int32).reshape(1)
    return pl.pallas_call(
        body, name="rs_add_chips", grid_spec=grid_spec,
        out_shape=[jax.ShapeDtypeStruct(p.shape[1:], F32) for p in parts],
        compiler_params=_cparams(("parallel",)),
    )(me, *chipsum, *parts)


def _rs_swap_halves(halves):
    n = len(halves)

    def body(*refs):
        h_refs, out_refs = refs[:n], refs[n:2 * n]
        send_sems, recv_sems = refs[2 * n:]
        x, y, c = _place()
        copies = [pltpu.make_async_remote_copy(src_ref=h_ref, dst_ref=out_ref, send_sem=send_sems.at[a], recv_sem=recv_sems.at[a],
                                               device_id=(x, y, 1 - c), device_id_type=MESH_ID)
                  for a, (h_ref, out_ref) in enumerate(zip(h_refs, out_refs))]
        for cp in copies:
            cp.start()
        for cp in copies:
            cp.wait()

    return pl.pallas_call(
        body, name="rs_swap_halves", in_specs=[HBM_SPEC] * n, out_specs=[HBM_SPEC] * n,
        out_shape=[jax.ShapeDtypeStruct(h.shape, F32) for h in halves],
        scratch_shapes=[pltpu.SemaphoreType.DMA((n,)), pltpu.SemaphoreType.DMA((n,))],
    )(*halves)


def _adamw_list(ws, g_mine, g_theirs, ms, vs):
    n = len(ws)

    def body(c_ref, *refs):
        w_refs, gm_refs, gt_refs, m_refs, v_refs = (refs[k * n:(k + 1) * n] for k in range(5))
        g_refs, d_refs, nm_refs, nv_refs = (refs[k * n:(k + 1) * n] for k in range(5, 9))
        mine = (pl.program_id(0) // ADAM_STEPS_PER_HALF) == c_ref[0]
        for a in range(n):
            gv = jnp.where(mine, gm_refs[a][...], gt_refs[a][...])
            g_refs[a][...] = gv
            m_new = ADAM_B1 * m_refs[a][...] + (1.0 - ADAM_B1) * gv
            v_new = ADAM_B2 * v_refs[a][...] + (1.0 - ADAM_B2) * (gv * gv)
            m_hat = m_new / (1.0 - ADAM_B1 ** ADAM_STEP)
            v_hat = v_new / (1.0 - ADAM_B2 ** ADAM_STEP)
            d_refs[a][...] = -ADAM_LR * (m_hat / (jnp.sqrt(v_hat) + ADAM_EPS) + ADAM_WD * w_refs[a][...])
            nm_refs[a][...] = m_new
            nv_refs[a][...] = v_new

    steps = 2 * ADAM_STEPS_PER_HALF
    whole = [pl.BlockSpec((w.shape[0] // steps, w.shape[1]), lambda i, c_ref: (i, 0)) for w in ws]
    half = [pl.BlockSpec((w.shape[0] // steps, w.shape[1]), lambda i, c_ref: (i % ADAM_STEPS_PER_HALF, 0)) for w in ws]
    shapes = [jax.ShapeDtypeStruct(w.shape, F32) for w in ws]
    grid_spec = pltpu.PrefetchScalarGridSpec(num_scalar_prefetch=1, grid=(steps,),
                                             in_specs=whole + half + half + whole + whole, out_specs=whole * 4)
    res = pl.pallas_call(
        body, name="adamw", grid_spec=grid_spec, out_shape=shapes * 4,
        compiler_params=_cparams(("parallel",)),
    )(lax.axis_index("c").astype(jnp.int32).reshape(1), *ws, *g_mine, *g_theirs, *ms, *vs)
    return res[:n], res[n:2 * n], res[2 * n:3 * n], res[3 * n:]


WEIGHT_NAMES = ("w_in", "w_mem_kv", "q_a_gain", "w_q_b", "kv_a_gain", "w_kv_b", "w_branch_mla", "w_branch_sb",
                "w_branch_mem", "w_merge_gate", "b_merge_gate", "w_out", "ln_gain", "ln_bias")
SMALL_NAMES = tuple(n for n, _ in SMALL_SIZES)


def kernel(x, mem, w_in, w_mem_kv, q_a_gain, w_q_b, kv_a_gain, w_kv_b, w_branch_mla, w_branch_sb, w_branch_mem, w_merge_gate, b_merge_gate, w_out, ln_gain, ln_bias, loss_target, m_w_in, m_w_mem_kv, m_q_a_gain, m_w_q_b, m_kv_a_gain, m_w_kv_b, m_w_branch_mla, m_w_branch_sb, m_w_branch_mem, m_w_merge_gate, m_b_merge_gate, m_w_out, m_ln_gain, m_ln_bias, v_w_in, v_w_mem_kv, v_q_a_gain, v_w_q_b, v_kv_a_gain, v_w_kv_b, v_w_branch_mla, v_w_branch_sb, v_w_branch_mem, v_w_merge_gate, v_b_merge_gate, v_w_out, v_ln_gain, v_ln_bias):
    weights = dict(zip(WEIGHT_NAMES, (w_in, w_mem_kv, q_a_gain, w_q_b, kv_a_gain, w_kv_b, w_branch_mla, w_branch_sb,
                                      w_branch_mem, w_merge_gate, b_merge_gate, w_out, ln_gain, ln_bias)))
    mom1 = dict(zip(WEIGHT_NAMES, (m_w_in, m_w_mem_kv, m_q_a_gain, m_w_q_b, m_kv_a_gain, m_w_kv_b, m_w_branch_mla,
                                   m_w_branch_sb, m_w_branch_mem, m_w_merge_gate, m_b_merge_gate, m_w_out, m_ln_gain,
                                   m_ln_bias)))
    mom2 = dict(zip(WEIGHT_NAMES, (v_w_in, v_w_mem_kv, v_q_a_gain, v_w_q_b, v_kv_a_gain, v_w_kv_b, v_w_branch_mla,
                                   v_w_branch_sb, v_w_branch_mem, v_w_merge_gate, v_b_merge_gate, v_w_out, v_ln_gain,
                                   v_ln_bias)))
    def as_list(d):
        return [d[n][0] for n in BIG_NAMES] + [_pack_small({n: d[n] for n in SMALL_NAMES})]

    w_list, m_list, v_list = as_list(weights), as_list(mom1), as_list(mom2)

    gathered = _gather_shards([weights[n][0] for n in LATE_NAMES])
    first_w = {n: _join_chips(n, g) for n, g in zip(LATE_NAMES, gathered)}
    rest_shards = _cast_bf16_list([weights[n][0] for n in EARLY_NAMES])
    small = {n: weights[n] for n in SMALL_NAMES}

    seq = x.shape[1]
    _, grad_x, late_mine, _, early_mine = _local_step(
        x[0], mem[0], loss_target[0], first_w, small, tq=min(1024, seq), tq_sb_bwd=1024, tk=256, tk_mla=512, t_row=256, t_mm=512,
        t_wg=min(2048, seq), rest_shards=rest_shards)
    by_name = dict(zip(EARLY_NAMES + LATE_NAMES + ("small",), list(early_mine) + list(late_mine)))
    mine = [by_name[n] for n in BIG_NAMES + ("small",)]
    theirs = _rs_swap_halves(mine)
    g_list, d_list, nm_list, nv_list = _adamw_list(w_list, mine, theirs, m_list, v_list)

    loss = g_list[-1].reshape(-1)[SMALL_TOTAL]
    outs = [loss, grad_x[None]]
    for arrays in (g_list, d_list, nm_list, nv_list):
        big = dict(zip(BIG_NAMES, arrays[:-1]))
        sm = _unpack_small(arrays[-1])
        outs.extend(big[n][None] if n in big else sm[n] for n in WEIGHT_NAMES)
    return tuple(outs)
```

```python
import functools
import math

import numpy as np
import jax
import jax.numpy as jnp
from jax import lax
from jax.experimental import pallas as pl
from jax.experimental.pallas import tpu as pltpu

F32 = jnp.float32
BF16 = jnp.bfloat16
MESH_ID = pl.DeviceIdType.MESH

D_MODEL = 1024
MEM_LEN = 256
MLA_HEADS = 8
MLA_NOPE = 64
MLA_ROPE = 32
MLA_V = 64
MLA_Q_LORA = 256
MLA_KV_LORA = 128
SB_HEADS = 8
SB_HEAD_DIM = 64
MEM_HEADS = 4
MEM_HEAD_DIM = 128
ROPE_BASE = 10000.0
RMS_EPS = 1e-6
LN_EPS = 1e-5
DEEPNORM_ALPHA = 2.0 ** 0.25
MLA_SCALE = 1.0 / math.sqrt(MLA_NOPE + MLA_ROPE)
SB_SCALE = 1.0 / math.sqrt(SB_HEAD_DIM)
MEM_SCALE = 1.0 / math.sqrt(MEM_HEAD_DIM)

ADAM_LR = 0.001
ADAM_B1 = 0.9
ADAM_B2 = 0.999
ADAM_EPS = 1e-08
ADAM_WD = 0.01
ADAM_STEP = 10

LANES = 128
HALF = 64
N_CHIPS = 4
PACK_COLS = 1024
VMEM_LIMIT = 56 * 1024 * 1024

IN_WIDTH_P = 4096
BLK_LAT, BLK_GATE_A, BLK_QB, BLK_KB, BLK_VB, BLK_GATE_B, BLK_QM, BLK_GATE_M = range(8)
N_MERGE = 3 * D_MODEL

BIG_NAMES = ("w_in", "w_mem_kv", "w_q_b", "w_kv_b", "w_branch_mla", "w_branch_sb", "w_branch_mem", "w_merge_gate", "w_out")
COL_SHARDED = ("w_in", "w_q_b", "w_kv_b", "w_branch_mla", "w_branch_sb", "w_branch_mem", "w_merge_gate")
SMALL_SIZES = (("q_a_gain", 256), ("kv_a_gain", 128), ("b_merge_gate", 3072), ("ln_gain", 1024), ("ln_bias", 1024))
SMALL_TOTAL = sum(s for _, s in SMALL_SIZES)


def _cparams(sem=None):
    return pltpu.CompilerParams(dimension_semantics=sem, vmem_limit_bytes=VMEM_LIMIT)


def _dot(a, b):
    return jnp.dot(a, b, preferred_element_type=F32)


def _dot_nt(a, b):
    return lax.dot_general(a, b, (((1,), (1,)), ((), ())), preferred_element_type=F32)


def _dot_tn(a, b):
    return lax.dot_general(a, b, (((0,), (0,)), ((), ())), preferred_element_type=F32)


def _bf(x):
    return x.astype(BF16)


def _dot_cols(a, w_ref):
    return jnp.concatenate([_dot(a, w_ref[j]) for j in range(w_ref.shape[0])], axis=1)


def _dot_nt_cols(a, w_ref):
    cs = w_ref.shape[2]
    out = None
    for j in range(w_ref.shape[0]):
        term = _dot_nt(a[:, j * cs:(j + 1) * cs], w_ref[j])
        out = term if out is None else out + term
    return out


def _sigmoid(x):
    return 1.0 / (1.0 + jnp.exp(-x))


def _matmul(a, b, *, mode, tm, tn, tk, out_dtypes, name, add=None, add_scale=1.0, by_column_block=False):
    if mode == "nn":
        (m, k), n = a.shape, b.shape[1]
        a_spec = pl.BlockSpec((tm, tk), lambda i, j, kk: (i, kk))
        b_spec = pl.BlockSpec((tk, tn), lambda i, j, kk: (kk, j))
        dot = _dot
    elif mode == "nt":
        (m, k), n = a.shape, b.shape[0]
        a_spec = pl.BlockSpec((tm, tk), lambda i, j, kk: (i, kk))
        b_spec = pl.BlockSpec((tn, tk), lambda i, j, kk: (j, kk))
        dot = _dot_nt
    else:
        (k, m), n = a.shape, b.shape[1]
        a_spec = pl.BlockSpec((tk, tm), lambda i, j, kk: (kk, i))
        b_spec = pl.BlockSpec((tk, tn), lambda i, j, kk: (kk, j))
        dot = _dot_tn
    assert m % tm == 0 and n % tn == 0 and k % tk == 0, (name, m, n, k)
    nk = k // tk
    n_out = len(out_dtypes)
    has_add = add is not None

    def body(*refs):
        a_ref, b_ref = refs[0], refs[1]
        add_ref = refs[2] if has_add else None
        outs = refs[2 + has_add: 2 + has_add + n_out]
        acc = refs[-1]
        kk = pl.program_id(2)

        @pl.when(kk == 0)
        def _():
            acc[...] = jnp.zeros_like(acc)

        acc[...] += dot(_bf(a_ref[...]), _bf(b_ref[...]))

        @pl.when(kk == nk - 1)
        def _():
            r = acc[...]
            if has_add:
                r = r + add_scale * add_ref[...]
            for o in outs:
                o[...] = r.astype(o.dtype)

    in_specs = [a_spec, b_spec]
    args = [a, b]
    if has_add:
        in_specs.append(pl.BlockSpec((tm, tn), lambda i, j, kk: (i, j)))
        args.append(add)
    if by_column_block:
        out_spec = pl.BlockSpec((None, tm, tn), lambda i, j, kk: (j, i, 0))
        out_dims = (n // tn, m, tn)
    else:
        out_spec = pl.BlockSpec((tm, tn), lambda i, j, kk: (i, j))
        out_dims = (m, n)
    res = pl.pallas_call(
        body, name=name, grid=(m // tm, n // tn, nk),
        in_specs=in_specs, out_specs=[out_spec] * n_out,
        out_shape=[jax.ShapeDtypeStruct(out_dims, dt) for dt in out_dtypes],
        scratch_shapes=[pltpu.VMEM((tm, tn), F32)],
        compiler_params=_cparams(("parallel", "parallel", "arbitrary")),
    )(*args)
    return res


def _rope_tables(seq):
    half = MLA_ROPE // 2
    freqs = ROPE_BASE ** (-jnp.arange(half, dtype=F32) / half)
    ang = jnp.arange(seq, dtype=jnp.int32).astype(F32)[:, None] * freqs[None, :]
    cos, sin = jnp.cos(ang), jnp.sin(ang)
    z = lambda w: jnp.zeros((seq, w), F32)
    c_q = jnp.concatenate([jnp.ones((seq, MLA_NOPE), F32), cos, cos, z(32)], axis=1)
    c_k = jnp.concatenate([z(MLA_NOPE), cos, cos, z(32)], axis=1)
    s_lo = jnp.concatenate([z(MLA_NOPE), -sin, z(half), z(32)], axis=1)
    s_hi = jnp.concatenate([z(MLA_NOPE), z(half), sin, z(32)], axis=1)
    return c_q, c_k, s_lo, s_hi


def _rope_fwd(x, c, s_lo, s_hi):
    return x * c + pltpu.roll(x, LANES - 16, 1) * s_lo + pltpu.roll(x, 16, 1) * s_hi


def _rope_bwd(d, c, s_lo, s_hi):
    return d * c - pltpu.roll(d, 16, 1) * s_hi - pltpu.roll(d, LANES - 16, 1) * s_lo


def _rms_fwd(x, g):
    r = lax.rsqrt(jnp.mean(x * x, axis=-1, keepdims=True) + RMS_EPS)
    xn = x * r
    return xn * g, xn, r


def _mla_prep(p32, gq, gkv, wqb, wkvb, tabs, *, t):
    seq = p32.shape[0]

    def body(lat_ref, gq_ref, gkv_ref, wqb_ref, wkvb_ref, cq_ref, ck_ref, slo_ref, shi_ref, q_ref, k_ref, v_ref):
        lat = lat_ref[...]
        slo, shi = slo_ref[...], shi_ref[...]
        nq, _, _ = _rms_fwd(lat[:, 0:MLA_Q_LORA], gq_ref[...])
        qa = _dot(_bf(nq), wqb_ref[...])
        cq = cq_ref[...]
        for h in range(MLA_HEADS):
            blk = qa[:, h * LANES:(h + 1) * LANES]
            q_ref[:, h * LANES:(h + 1) * LANES] = _bf(_rope_fwd(blk, cq, slo, shi))
        nkv, _, _ = _rms_fwd(lat[:, MLA_Q_LORA:MLA_Q_LORA + MLA_KV_LORA], gkv_ref[...])
        kv = _dot(_bf(nkv), wkvb_ref[...])
        kpe = _rope_fwd(lat[:, 384:512], ck_ref[...], slo, shi)
        for h in range(MLA_HEADS):
            k_ref[:, h * LANES:(h + 1) * LANES] = _bf(kv[:, h * LANES:(h + 1) * LANES] + kpe)
        v_ref[...] = _bf(kv[:, MLA_HEADS * LANES:])

    row = lambda w: pl.BlockSpec((t, w), lambda i: (i, 0))
    full = lambda shp: pl.BlockSpec(shp, lambda i: (0, 0))
    return pl.pallas_call(
        body, name="mla_prep", grid=(seq // t,),
        in_specs=[row(512), full((1, MLA_Q_LORA)), full((1, MLA_KV_LORA)), full(wqb.shape), full(wkvb.shape),
                  row(LANES), row(LANES), row(LANES), row(LANES)],
        out_specs=[row(1024), row(1024), row(512)],
        out_shape=[jax.ShapeDtypeStruct((seq, 1024), BF16), jax.ShapeDtypeStruct((seq, 1024), BF16),
                   jax.ShapeDtypeStruct((seq, 512), BF16)],
        compiler_params=_cparams(("parallel",)),
    )(p32, gq, gkv, wqb, wkvb, *tabs)


def _mla_post(p32, dq, dk, dv, gq, gkv, wqb, wkvb, tabs, *, t):
    seq = p32.shape[0]

    def body(lat_ref, dq_ref, dk_ref, dv_ref, gq_ref, gkv_ref, wqb_ref, wkvb_ref, cq_ref, ck_ref, slo_ref, shi_ref,
             dlat_ref, dwqb_ref, dwkvb_ref, dgq_ref, dgkv_ref):
        @pl.when(pl.program_id(0) == 0)
        def _():
            dwqb_ref[...] = jnp.zeros_like(dwqb_ref)
            dwkvb_ref[...] = jnp.zeros_like(dwkvb_ref)
            dgq_ref[...] = jnp.zeros_like(dgq_ref)
            dgkv_ref[...] = jnp.zeros_like(dgkv_ref)

        lat = lat_ref[...]
        slo, shi = slo_ref[...], shi_ref[...]
        cq = cq_ref[...]
        gq_v, gkv_v = gq_ref[...], gkv_ref[...]
        nq, xq, rq = _rms_fwd(lat[:, 0:MLA_Q_LORA], gq_v)
        nkv, xkv, rkv = _rms_fwd(lat[:, MLA_Q_LORA:MLA_Q_LORA + MLA_KV_LORA], gkv_v)

        dqa = jnp.concatenate(
            [_rope_bwd(dq_ref[:, h * LANES:(h + 1) * LANES], cq, slo, shi) for h in range(MLA_HEADS)], axis=1)
        dqa_b = _bf(dqa)
        dwqb_ref[...] += _dot_tn(_bf(nq), dqa_b)
        dnq = _dot_nt(dqa_b, wqb_ref[...])
        dgq_ref[...] += jnp.sum(dnq * xq, axis=0, keepdims=True)
        dxn = dnq * gq_v
        dcq = rq * (dxn - xq * jnp.mean(dxn * xq, axis=-1, keepdims=True))

        dkf = dk_ref[...]
        dkv_b = _bf(jnp.concatenate([dkf, dv_ref[...]], axis=1))
        dwkvb_ref[...] += _dot_tn(_bf(nkv), dkv_b)
        dnkv = _dot_nt(dkv_b, wkvb_ref[...])
        dgkv_ref[...] += jnp.sum(dnkv * xkv, axis=0, keepdims=True)
        dxn = dnkv * gkv_v
        dckv = rkv * (dxn - xkv * jnp.mean(dxn * xkv, axis=-1, keepdims=True))

        dkpe = dkf[:, 0:LANES]
        for h in range(1, MLA_HEADS):
            dkpe = dkpe + dkf[:, h * LANES:(h + 1) * LANES]
        dkr = _rope_bwd(dkpe, ck_ref[...], slo, shi)
        dlat_ref[...] = _bf(jnp.concatenate([dcq, dckv, dkr], axis=1))

    row = lambda w: pl.BlockSpec((t, w), lambda i: (i, 0))
    full = lambda shp: pl.BlockSpec(shp, lambda i: (0, 0))
    return pl.pallas_call(
        body, name="mla_post", grid=(seq // t,),
        in_specs=[row(512), row(1024), row(1024), row(512), full((1, MLA_Q_LORA)), full((1, MLA_KV_LORA)),
                  full(wqb.shape), full(wkvb.shape), row(LANES), row(LANES), row(LANES), row(LANES)],
        out_specs=[row(512), full(wqb.shape), full(wkvb.shape), full((1, MLA_Q_LORA)), full((1, MLA_KV_LORA))],
        out_shape=[jax.ShapeDtypeStruct((seq, 512), BF16), jax.ShapeDtypeStruct(wqb.shape, F32),
                   jax.ShapeDtypeStruct(wkvb.shape, F32), jax.ShapeDtypeStruct((1, MLA_Q_LORA), F32),
                   jax.ShapeDtypeStruct((1, MLA_KV_LORA), F32)],
        compiler_params=_cparams(("arbitrary",)),
    )(p32, dq, dk, dv, gq, gkv, wqb, wkvb, *tabs)


def _split_bf16(x):
    hi = _bf(x)
    return hi, _bf(x - hi.astype(F32))


def _tri_sum(x, u):
    hi, lo = _split_bf16(x)
    return _dot(hi, u) + _dot(lo, u)


def _softplus(z):
    return jnp.maximum(z, 0.0) + jnp.log(1.0 + jnp.exp(-jnp.abs(z)))


def _head_queries(q, left):
    zero = jnp.zeros_like(q)
    return jnp.where(left, q, zero) * SB_SCALE, jnp.where(left, zero, q) * SB_SCALE


ROW_GROUP = 128
SB_BWD_CHAINS_IN_FLIGHT = 16
ANY_HBM = pl.BlockSpec(memory_space=pltpu.HBM)


class _Exchange:
    def __init__(self, send, landing):
        self.send, self.landing = send, landing

    def start(self):
        self.send.start()

    def wait(self):
        self.landing.wait_recv()
        self.send.wait_send()


class _Rider:
    def __init__(self, operands, out_shapes, sem_shapes, copies):
        self.operands, self.out_shapes, self.sem_shapes, self.copies = list(operands), list(out_shapes), list(sem_shapes), copies


def _call_with_rider(body, rider, *, name, grid, in_specs, out_specs, out_shape, args, semantics, scratch=()):
    scratch = list(scratch)
    if rider is None:
        return pl.pallas_call(body, name=name, grid=grid, in_specs=in_specs, out_specs=out_specs, out_shape=out_shape,
                              scratch_shapes=scratch, compiler_params=_cparams(semantics))(*args)
    n_in, n_out, n_rin, n_rout = len(in_specs), len(out_specs), len(rider.operands), len(rider.out_shapes)

    def full_body(*refs):
        ins, r_ins = refs[:n_in], refs[n_in:n_in + n_rin]
        outs = refs[n_in + n_rin:n_in + n_rin + n_out]
        r_outs = refs[n_in + n_rin + n_out:n_in + n_rin + n_out + n_rout]
        rest = refs[n_in + n_rin + n_out + n_rout:]
        own_scratch, sems = rest[:len(scratch)], rest[len(scratch):]
        first, last = None, None
        for axis, size in enumerate(grid):
            at_start, at_end = pl.program_id(axis) == 0, pl.program_id(axis) == size - 1
            first = at_start if first is None else first & at_start
            last = at_end if last is None else last & at_end

        @pl.when(first)
        def _():
            for cp in rider.copies(r_ins, r_outs, sems):
                cp.start()

        body(*ins, *outs, *own_scratch)

        @pl.when(last)
        def _():
            for cp in rider.copies(r_ins, r_outs, sems):
                cp.wait()

    return pl.pallas_call(
        full_body, name=name, grid=grid, in_specs=list(in_specs) + [ANY_HBM] * n_rin,
        out_specs=list(out_specs) + [ANY_HBM] * n_rout, out_shape=list(out_shape) + rider.out_shapes,
        scratch_shapes=scratch + rider.sem_shapes, compiler_params=_cparams(("arbitrary",) * len(grid)),
    )(*args, *rider.operands)


def _chains(tq):
    return [(h, g) for g in range(tq // ROW_GROUP) for h in range(2)]


def _chain_pattern(g, m, tk, strict):
    r_lo, r_hi = g * ROW_GROUP, (g + 1) * ROW_GROUP - 1
    c_lo, c_hi = m * tk, (m + 1) * tk - 1
    if (c_lo >= r_hi) if strict else (c_lo > r_hi):
        return None
    if (c_hi < r_lo) if strict else (c_hi <= r_lo):
        return True
    rr = lax.broadcasted_iota(jnp.int32, (ROW_GROUP, tk), 0) + r_lo
    cc = lax.broadcasted_iota(jnp.int32, (ROW_GROUP, tk), 1) + c_lo
    return (cc < rr) if strict else (cc <= rr)


def _masked(x, pat, fill=0.0):
    return x if pat is True else jnp.where(pat, x, fill)


def _rows(g):
    return slice(g * ROW_GROUP, (g + 1) * ROW_GROUP)


def _tri_matrix(tk, cmp):
    rr = lax.broadcasted_iota(jnp.int32, (tk, tk), 0)
    cc = lax.broadcasted_iota(jnp.int32, (tk, tk), 1)
    return cmp(rr, cc).astype(BF16)


def _mla_attn_fwd(qp, kp, vp, *, tq, tk, tk_diag, rider=None):
    seq = qp.shape[0]
    neg = float(np.finfo(np.float32).min)
    chains = _chains(tq)

    def body(q_ref, k_ref, v_ref, o_ref, lse_ref):
        i = pl.program_id(1)
        left = lax.broadcasted_iota(jnp.int32, (tq, LANES), 1) < HALF
        qs = [q_ref[_rows(g), h * LANES:(h + 1) * LANES] for h, g in chains]

        def block(start, carry, m, tk):
            v = v_ref[pl.ds(start, tk), :]
            pats = [True if m is None else _chain_pattern(g, m, tk, False) for _, g in chains]
            live = [n for n, p in enumerate(pats) if p is not None]
            ss = {n: _dot_nt(qs[n], k_ref[pl.ds(start, tk), chains[n][0] * LANES:(chains[n][0] + 1) * LANES]) for n in live}
            new = list(carry)
            for n in live:
                m_old, l_old, acc = carry[n]
                s = _masked(ss[n] * MLA_SCALE, pats[n], neg)
                m_new = jnp.maximum(m_old, jnp.max(s, axis=-1, keepdims=True))
                a = jnp.exp(m_old - m_new)
                p = jnp.exp(s - m_new)
                new[n] = (m_new, a * l_old + jnp.sum(p, axis=-1, keepdims=True), a * acc + _dot(_bf(p), v))
            return tuple(new)

        init = (jnp.full((ROW_GROUP, 1), -1e30, F32), jnp.zeros((ROW_GROUP, 1), F32), jnp.zeros((ROW_GROUP, LANES), F32))
        def two_blocks(j, c):
            c = block(pl.multiple_of(2 * j * tk, tk), c, None, tk)
            return block(pl.multiple_of((2 * j + 1) * tk, tk), c, None, tk)

        carry = lax.fori_loop(0, i * (tq // tk) // 2, two_blocks, (init,) * len(chains))
        for m in range(tq // tk_diag):
            carry = block(pl.multiple_of(i * tq + m * tk_diag, tk_diag), carry, m, tk_diag)
        per_head = []
        for h in range(2):
            mine = [carry[n] for n, (ch, _) in enumerate(chains) if ch == h]
            per_head.append((jnp.concatenate([acc / l for _, l, acc in mine], axis=0),
                             jnp.concatenate([mm + jnp.log(l) for mm, l, _ in mine], axis=0)))
        o_ref[...] = jnp.where(left, per_head[0][0], per_head[1][0])
        lse_ref[...] = jnp.where(left, per_head[0][1], per_head[1][1])

    return _call_with_rider(
        body, rider, name="mla_fwd", grid=(MLA_HEADS // 2, seq // tq),
        in_specs=[pl.BlockSpec((tq, 2 * LANES), lambda p, i: (i, p)), pl.BlockSpec((seq, 2 * LANES), lambda p, i: (0, p)),
                  pl.BlockSpec((seq, LANES), lambda p, i: (0, p))],
        out_specs=[pl.BlockSpec((tq, LANES), lambda p, i: (i, p)), pl.BlockSpec((tq, LANES), lambda p, i: (i, p))],
        out_shape=[jax.ShapeDtypeStruct((seq, 512), F32), jax.ShapeDtypeStruct((seq, 512), F32)],
        args=(qp, kp, vp), semantics=("parallel", "parallel"))


def _mla_attn_bwd(qp, kp, vp, o, lse, do, *, tq, tk, tk_diag, rider=None):
    seq = qp.shape[0]
    chains = _chains(tq)

    def body(q_ref, k_ref, v_ref, o_ref, lse_ref, do_ref, dq_ref, dk_ref, dv_ref, qt_ref, dot_ref):
        i = pl.program_id(1)

        @pl.when(i == 0)
        def _():
            dk_ref[...] = jnp.zeros_like(dk_ref)
            dv_ref[...] = jnp.zeros_like(dv_ref)

        left = lax.broadcasted_iota(jnp.int32, (tq, LANES), 1) < HALF
        do_f = do_ref[...]
        prod = do_f * o_ref[...]
        lse_v = lse_ref[...]
        do_heads = (_bf(jnp.where(left, do_f, 0.0)), _bf(jnp.where(left, 0.0, do_f)))
        delta_heads = (jnp.sum(jnp.where(left, prod, 0.0), axis=-1, keepdims=True),
                       jnp.sum(jnp.where(left, 0.0, prod), axis=-1, keepdims=True))
        qs = [q_ref[_rows(g), h * LANES:(h + 1) * LANES] for h, g in chains]
        dos = [do_heads[h][_rows(g)] for h, g in chains]
        deltas = [delta_heads[h][_rows(g)] for h, g in chains]
        lses = [lse_v[_rows(g), h * HALF:h * HALF + 1] for h, g in chains]
        for h in range(2):
            qt_ref[h] = q_ref[:, h * LANES:(h + 1) * LANES].T
            dot_ref[h] = do_heads[h].T
        q_t = [qt_ref.at[h] for h in range(2)]
        do_t = [dot_ref.at[h] for h in range(2)]

        def block(start, carry, m, tk):
            v = v_ref[pl.ds(start, tk), :]
            pats = [True if m is None else _chain_pattern(g, m, tk, False) for _, g in chains]
            live = [n for n, p in enumerate(pats) if p is not None]
            ks = [k_ref[pl.ds(start, tk), h * LANES:(h + 1) * LANES] for h in range(2)]
            ss = {n: _dot_nt(qs[n], ks[chains[n][0]]) for n in live}
            dps = {n: _dot_nt(dos[n], v) for n in live}
            new = list(carry)
            ps, dss = {}, {}
            for n in live:
                p = _masked(jnp.exp(ss[n] * MLA_SCALE - lses[n]), pats[n])
                ps[n] = _bf(p)
                dss[n] = _bf(p * (dps[n] - deltas[n]) * MLA_SCALE)
                new[n] = carry[n] + _dot(dss[n], ks[chains[n][0]])
            dv_t, dk_t = None, []
            for h in range(2):
                mine = [n for n in live if chains[n][0] == h]
                first_row = chains[mine[0]][1] * ROW_GROUP
                ds_cat = jnp.concatenate([dss[n] for n in mine], axis=0)
                p_cat = jnp.concatenate([ps[n] for n in mine], axis=0)
                if first_row == 0:
                    q_rows_t, do_rows_t = q_t[h][...], do_t[h][...]
                else:
                    q_rows_t = q_ref[first_row:, h * LANES:(h + 1) * LANES].T
                    do_rows_t = do_heads[h][first_row:].T
                dk_t.append(_dot(q_rows_t, ds_cat))
                term = _dot(do_rows_t, p_cat)
                dv_t = term if dv_t is None else dv_t + term
            back = jnp.concatenate(dk_t + [dv_t], axis=0).T
            dk_ref[pl.ds(start, tk), :] += back[:, :2 * LANES]
            dv_ref[pl.ds(start, tk), :] += back[:, 2 * LANES:]
            return tuple(new)

        zero = jnp.zeros((ROW_GROUP, LANES), F32)
        carry = lax.fori_loop(0, i * (tq // tk), lambda j, c: block(pl.multiple_of(j * tk, tk), c, None, tk),
                              (zero,) * len(chains))
        for m in range(tq // tk_diag):
            carry = block(pl.multiple_of(i * tq + m * tk_diag, tk_diag), carry, m, tk_diag)
        for n, (h, g) in enumerate(chains):
            dq_ref[_rows(g), h * LANES:(h + 1) * LANES] = carry[n]

    two_t = pl.BlockSpec((tq, 2 * LANES), lambda p, i: (i, p))
    two_s = pl.BlockSpec((seq, 2 * LANES), lambda p, i: (0, p))
    pair_t = pl.BlockSpec((tq, LANES), lambda p, i: (i, p))
    pair_s = pl.BlockSpec((seq, LANES), lambda p, i: (0, p))
    return _call_with_rider(
        body, rider, name="mla_bwd", grid=(MLA_HEADS // 2, seq // tq),
        in_specs=[two_t, two_s, pair_s, pair_t, pair_t, pair_t],
        out_specs=[two_t, two_s, pair_s],
        out_shape=[jax.ShapeDtypeStruct((seq, 1024), F32), jax.ShapeDtypeStruct((seq, 1024), F32),
                   jax.ShapeDtypeStruct((seq, 512), F32)],
        args=(qp, kp, vp, o, lse, do), semantics=("parallel", "arbitrary"),
        scratch=[pltpu.VMEM((2, LANES, tq), BF16), pltpu.VMEM((2, LANES, tq), BF16)])


def _sb_attn_fwd(pbf, *, tq, tk):
    seq = pbf.shape[0]
    nd = tq // tk
    qb, kb, vb = BLK_QB * 4, BLK_KB * 4, BLK_VB * 4
    chains = _chains(tq)

    def body(q_ref, k_ref, v_ref, o_ref, tot_ref):
        i = pl.program_id(1)
        u_later = _tri_matrix(tk, lambda r, c: r > c)
        left = lax.broadcasted_iota(jnp.int32, (tq, LANES), 1) < HALF
        q_heads = _head_queries(q_ref[...], left)
        qs = [q_heads[h][_rows(g)] for h, g in chains]

        def block(j, carry, m):
            start = pl.multiple_of(j * tk, tk)
            k = k_ref[pl.ds(start, tk), :]
            v = v_ref[pl.ds(start, tk), :]
            pats = [True if m is None else _chain_pattern(g, m, tk, True) for _, g in chains]
            live = [n for n, p in enumerate(pats) if p is not None]
            zs = {n: _dot_nt(qs[n], k) for n in live}
            raws = {n: _softplus(zs[n]) for n in live}
            sps = {n: _masked(raws[n], pats[n]) for n in live}
            laters = {n: _tri_sum(sps[n], u_later) for n in live}
            new = list(carry)
            for n in live:
                c, acc = carry[n]
                a = _masked(jnp.exp(zs[n] - raws[n] - laters[n] - c), pats[n])
                new[n] = (c + laters[n][:, 0:1] + sps[n][:, 0:1], acc + _dot(_bf(a), v))
            return tuple(new)

        init = (jnp.zeros((ROW_GROUP, 1), F32), jnp.zeros((ROW_GROUP, LANES), F32))
        carry = (init,) * len(chains)
        for m in reversed(range(nd)):
            carry = block(i * nd + m, carry, m)
        per_trip = 4 if nd % 4 == 0 else 2

        def trip(jj, cr):
            for u in range(per_trip):
                cr = block(i * nd - 1 - (per_trip * jj + u), cr, None)
            return cr

        carry = lax.fori_loop(0, i * nd // per_trip, trip, carry)
        per_head = []
        for h in range(2):
            mine = [carry[n] for n, (ch, _) in enumerate(chains) if ch == h]
            per_head.append((jnp.concatenate([acc for _, acc in mine], axis=0), jnp.concatenate([c for c, _ in mine], axis=0)))
        o_ref[...] = jnp.where(left, per_head[0][0], per_head[1][0])
        tot_ref[...] = jnp.where(left, per_head[0][1], per_head[1][1])

    pair_t = pl.BlockSpec((tq, LANES), lambda p, i: (i, p))
    return pl.pallas_call(
        body, name="sb_fwd", grid=(SB_HEADS // 2, seq // tq),
        in_specs=[pl.BlockSpec((tq, LANES), lambda p, i: (i, qb + p)), pl.BlockSpec((seq, LANES), lambda p, i: (0, kb + p)),
                  pl.BlockSpec((seq, LANES), lambda p, i: (0, vb + p))],
        out_specs=[pair_t, pair_t],
        out_shape=[jax.ShapeDtypeStruct((seq, 512), F32), jax.ShapeDtypeStruct((seq, 512), F32)],
        compiler_params=_cparams(("parallel", "parallel")),
    )(pbf, pbf, pbf)


def _sb_attn_bwd(pbf, tot, do, *, tq, tk, rider=None):
    seq = pbf.shape[0]
    nd = tq // tk
    qb, kb, vb = BLK_QB * 4, BLK_KB * 4, BLK_VB * 4
    chains = _chains(tq)
    group = SB_BWD_CHAINS_IN_FLIGHT

    def body(q_ref, k_ref, v_ref, tot_ref, do_ref, dq_ref, dk_ref, dv_ref, qt_ref, dot_ref):
        i = pl.program_id(1)

        @pl.when(i == 0)
        def _():
            dk_ref[...] = jnp.zeros_like(dk_ref)
            dv_ref[...] = jnp.zeros_like(dv_ref)

        u_upto = _tri_matrix(tk, lambda r, c: r <= c)
        u_below = _tri_matrix(tk, lambda r, c: r < c)
        left = lax.broadcasted_iota(jnp.int32, (tq, LANES), 1) < HALF
        q_heads = _head_queries(q_ref[...], left)
        do_f = do_ref[...]
        do_heads = (_bf(jnp.where(left, do_f, 0.0)), _bf(jnp.where(left, 0.0, do_f)))
        tot_v = tot_ref[...]
        qs = [q_heads[h][_rows(g)] for h, g in chains]
        dos = [do_heads[h][_rows(g)] for h, g in chains]
        totals = [tot_v[_rows(g), h * HALF:h * HALF + 1] for h, g in chains]
        qt_ref[...] = jnp.concatenate(qs, axis=0).T
        dot_ref[...] = jnp.concatenate(dos, axis=0).T

        def block(j, carry, m):
            start = pl.multiple_of(j * tk, tk)
            k = k_ref[pl.ds(start, tk), :]
            v = v_ref[pl.ds(start, tk), :]
            pats = [True if m is None else _chain_pattern(g, m, tk, True) for _, g in chains]
            all_live = [n for n, p in enumerate(pats) if p is not None]
            new = list(carry)
            for g0 in range(0, len(all_live), group):
                live = all_live[g0:g0 + group]
                zs = {n: _dot_nt(qs[n], k) for n in live}
                das = {n: _dot_nt(dos[n], v) for n in live}
                raws = {n: _softplus(zs[n]) for n in live}
                sps = {n: _masked(raws[n], pats[n]) for n in live}
                uptos = {n: _tri_sum(sps[n], u_upto) for n in live}
                lbs, a_s, gs = {}, {}, {}
                for n in live:
                    lbs[n] = zs[n] - raws[n]
                    a = _masked(jnp.exp(lbs[n] - (totals[n] - carry[n][0] - uptos[n])), pats[n])
                    a_s[n] = _bf(a)
                    gs[n] = das[n] * a
                belows = {n: _dot(_bf(gs[n]), u_below) for n in live}
                dzs = {}
                for n in live:
                    sp_before, g_before, dq_acc = carry[n]
                    beta = jnp.exp(lbs[n])
                    dz = _masked(gs[n] * (1.0 - beta) - (g_before + belows[n]) * beta, pats[n])
                    dzs[n] = _bf(dz)
                    new[n] = (sp_before + uptos[n][:, tk - 1:tk], g_before + belows[n][:, tk - 1:tk] + gs[n][:, tk - 1:tk],
                              dq_acc + _dot(dzs[n], k))
                dz_cat = jnp.concatenate([dzs[n] for n in live], axis=0)
                a_cat = jnp.concatenate([a_s[n] for n in live], axis=0)
                if len(live) == len(chains):
                    q_rows_t, do_rows_t = qt_ref[...], dot_ref[...]
                else:
                    q_rows_t = jnp.concatenate([qs[n] for n in live], axis=0).T
                    do_rows_t = jnp.concatenate([dos[n] for n in live], axis=0).T
                both = jnp.concatenate([_dot(q_rows_t, dz_cat), _dot(do_rows_t, a_cat)], axis=0).T
                dk_ref[pl.ds(start, tk), :] += both[:, :LANES]
                dv_ref[pl.ds(start, tk), :] += both[:, LANES:]
            return tuple(new)

        zero = jnp.zeros((ROW_GROUP, 1), F32)
        init = (zero, zero, jnp.zeros((ROW_GROUP, LANES), F32))
        per_trip = 4 if nd % 4 == 0 else 2

        def trip(j, cr):
            for u in range(per_trip):
                cr = block(per_trip * j + u, cr, None)
            return cr

        carry = lax.fori_loop(0, i * nd // per_trip, trip, (init,) * len(chains))
        for m in range(nd):
            carry = block(i * nd + m, carry, m)
        per_head = [jnp.concatenate([carry[n][2] for n, (ch, _) in enumerate(chains) if ch == h], axis=0) for h in range(2)]
        dq_ref[...] = jnp.where(left, per_head[0], per_head[1]) * SB_SCALE

    pair_t = pl.BlockSpec((tq, LANES), lambda p, i: (i, p))
    pair_s = pl.BlockSpec((seq, LANES), lambda p, i: (0, p))
    return _call_with_rider(
        body, rider, name="sb_bwd", grid=(SB_HEADS // 2, seq // tq),
        in_specs=[pl.BlockSpec((tq, LANES), lambda p, i: (i, qb + p)), pl.BlockSpec((seq, LANES), lambda p, i: (0, kb + p)),
                  pl.BlockSpec((seq, LANES), lambda p, i: (0, vb + p)), pair_t, pair_t],
        out_specs=[pair_t, pair_s, pair_s],
        out_shape=[jax.ShapeDtypeStruct((seq, 512), F32)] * 3,
        args=(pbf, pbf, pbf, tot, do), semantics=("parallel", "arbitrary"),
        scratch=[pltpu.VMEM((LANES, 2 * tq), BF16), pltpu.VMEM((LANES, 2 * tq), BF16)])


def _mem_probs(s):
    e = jnp.exp(s - jnp.max(s, axis=-1, keepdims=True))
    return e / jnp.sum(e, axis=-1, keepdims=True)


def _head_lanes(h):
    return slice(h * LANES, (h + 1) * LANES)


def _mem_fwd(pbf, mkv, *, t):
    seq = pbf.shape[0]

    def body(q_ref, kv_ref, o_ref):
        ss = [_dot_nt(q_ref[:, _head_lanes(h)], kv_ref[:, _head_lanes(h)]) * MEM_SCALE for h in range(MEM_HEADS)]
        ps = [_bf(_mem_probs(s)) for s in ss]
        for h, p in enumerate(ps):
            o_ref[:, _head_lanes(h)] = _dot(p, kv_ref[:, _head_lanes(MEM_HEADS + h)])

    return pl.pallas_call(
        body, name="mem_fwd", grid=(seq // t,),
        in_specs=[pl.BlockSpec((t, 512), lambda i: (i, BLK_QM)), pl.BlockSpec((MEM_LEN, 1024), lambda i: (0, 0))],
        out_specs=pl.BlockSpec((t, 512), lambda i: (i, 0)),
        out_shape=jax.ShapeDtypeStruct((seq, 512), F32),
        compiler_params=_cparams(("parallel",)),
    )(pbf, mkv)


def _mem_bwd(pbf, mkv, do, *, t):
    seq = pbf.shape[0]

    def body(q_ref, kv_ref, do_ref, dq_ref, dkv_ref):
        @pl.when(pl.program_id(0) == 0)
        def _():
            dkv_ref[...] = jnp.zeros_like(dkv_ref)

        heads = range(MEM_HEADS)
        qs = [q_ref[:, _head_lanes(h)] for h in heads]
        ks = [kv_ref[:, _head_lanes(h)] for h in heads]
        dos = [_bf(do_ref[:, _head_lanes(h)]) for h in heads]
        ss = [_dot_nt(qs[h], ks[h]) * MEM_SCALE for h in heads]
        dps = [_dot_nt(dos[h], kv_ref[:, _head_lanes(MEM_HEADS + h)]) for h in heads]
        ps = [_mem_probs(s) for s in ss]
        dss = [_bf(ps[h] * (dps[h] - jnp.sum(dps[h] * ps[h], axis=-1, keepdims=True)) * MEM_SCALE) for h in heads]
        for h in heads:
            dq_ref[:, _head_lanes(h)] = _dot(dss[h], ks[h])
        for h in heads:
            dkv_ref[:, _head_lanes(h)] += _dot_tn(dss[h], qs[h])
            dkv_ref[:, _head_lanes(MEM_HEADS + h)] += _dot_tn(_bf(ps[h]), dos[h])

    return pl.pallas_call(
        body, name="mem_bwd", grid=(seq // t,),
        in_specs=[pl.BlockSpec((t, 512), lambda i: (i, BLK_QM)), pl.BlockSpec((MEM_LEN, 1024), lambda i: (0, 0)),
                  pl.BlockSpec((t, 512), lambda i: (i, 0))],
        out_specs=[pl.BlockSpec((t, 512), lambda i: (i, 0)), pl.BlockSpec((MEM_LEN, 1024), lambda i: (0, 0))],
        out_shape=[jax.ShapeDtypeStruct((seq, 512), F32), jax.ShapeDtypeStruct((MEM_LEN, 1024), F32)],
        compiler_params=_cparams(("arbitrary",)),
    )(pbf, mkv, do)


def _mid(x, tgt, o_a, o_b, o_m, p32, wmg, bmg, wba, wbb, wbm, wout, ln_g, ln_b, *, t):
    seq = x.shape[0]
    inv_d = 1.0 / D_MODEL

    def body(x_ref, t_ref, oa_ref, ob_ref, om_ref, ga_ref, gb_ref, gm_ref, wmg_ref, bmg_ref, wba_ref, wbb_ref,
             wbm_ref, wout_ref, lg_ref, lb_ref,
             du_ref, mrg_ref, dgp_ref, ha_ref, hb_ref, hm_ref, dya_ref, dyb_ref, dym_ref, doa_ref, dob_ref, dom_ref,
             dga_ref, dgb_ref, dgm_ref, dgain_ref, dbias_ref, dbmg_ref, loss_ref):
        @pl.when(pl.program_id(0) == 0)
        def _():
            dgain_ref[...] = jnp.zeros_like(dgain_ref)
            dbias_ref[...] = jnp.zeros_like(dbias_ref)
            dbmg_ref[...] = jnp.zeros_like(dbmg_ref)
            loss_ref[...] = jnp.zeros_like(loss_ref)

        xv = x_ref[...]
        gate = _sigmoid(_dot_cols(_bf(xv), wmg_ref) + bmg_ref[...])

        branches = []
        merged = None
        for b, (o_ref, g_ref, w_ref, h_ref) in enumerate(((oa_ref, ga_ref, wba_ref, ha_ref), (ob_ref, gb_ref, wbb_ref, hb_ref),
                                                         (om_ref, gm_ref, wbm_ref, hm_ref))):
            o, gt = o_ref[...], g_ref[...]
            sg = _sigmoid(gt)
            silu = gt * sg
            h = _bf(o * silu)
            h_ref[...] = h
            y = _dot_cols(h, w_ref)
            g_b = gate[:, b * D_MODEL:(b + 1) * D_MODEL]
            term = g_b * y
            merged = term if merged is None else merged + term
            branches.append((o, gt, sg, silu, y, g_b))
        mrg_b = _bf(merged)
        mrg_ref[...] = mrg_b

        u = DEEPNORM_ALPHA * xv + _dot(mrg_b, wout_ref[...])
        mu = jnp.mean(u, axis=-1, keepdims=True)
        uc = u - mu
        rstd = lax.rsqrt(jnp.mean(uc * uc, axis=-1, keepdims=True) + LN_EPS)
        xhat = uc * rstd
        lg = lg_ref[...]
        y_out = xhat * lg + lb_ref[...]
        err = y_out - t_ref[...]
        loss_ref[...] += 0.5 * jnp.sum(jnp.mean(err * err, axis=-1, keepdims=True), axis=0, keepdims=True)
        dy = err * inv_d
        dgain_ref[...] += jnp.sum(dy * xhat, axis=0, keepdims=True)
        dbias_ref[...] += jnp.sum(dy, axis=0, keepdims=True)
        dxh = dy * lg
        du = rstd * (dxh - jnp.mean(dxh, axis=-1, keepdims=True) - xhat * jnp.mean(dxh * xhat, axis=-1, keepdims=True))
        du_ref[...] = du

        dmerged = _dot_nt(_bf(du), wout_ref[...])
        outs = ((dya_ref, doa_ref, dga_ref, wba_ref), (dyb_ref, dob_ref, dgb_ref, wbb_ref), (dym_ref, dom_ref, dgm_ref, wbm_ref))
        dgp = []
        for (o, gt, sg, silu, y, g_b), (dy_ref, do_ref, dg_ref, w_ref) in zip(branches, outs):
            dyb = _bf(dmerged * g_b)
            dy_ref[...] = dyb
            dgp.append(dmerged * y * g_b * (1.0 - g_b))
            dh = _dot_nt_cols(dyb, w_ref)
            do_ref[...] = dh * silu
            dg_ref[...] = _bf(dh * o * (sg * (1.0 + gt * (1.0 - sg))))
        dgp = jnp.concatenate(dgp, axis=1)
        dgp_ref[...] = _bf(dgp)
        dbmg_ref[...] += jnp.sum(dgp, axis=0, keepdims=True)

    row = lambda w: pl.BlockSpec((t, w), lambda i: (i, 0))
    pblk = lambda c: pl.BlockSpec((t, 512), lambda i: (i, c))
    full = lambda shp: pl.BlockSpec(shp, lambda i: (0,) * len(shp))
    sds = jax.ShapeDtypeStruct
    return pl.pallas_call(
        body, name="mid", grid=(seq // t,),
        in_specs=[row(1024), row(1024), row(512), row(512), row(512), pblk(BLK_GATE_A), pblk(BLK_GATE_B), pblk(BLK_GATE_M),
                  full(wmg.shape), full((1, N_MERGE)), full(wba.shape), full(wbb.shape), full(wbm.shape), full(wout.shape),
                  full((1, D_MODEL)), full((1, D_MODEL))],
        out_specs=[row(1024), row(1024), row(N_MERGE), row(512), row(512), row(512), row(1024), row(1024), row(1024),
                   row(512), row(512), row(512), row(512), row(512), row(512),
                   full((1, D_MODEL)), full((1, D_MODEL)), full((1, N_MERGE)), full((1, 1))],
        out_shape=[sds((seq, 1024), F32), sds((seq, 1024), BF16), sds((seq, N_MERGE), BF16),
                   sds((seq, 512), BF16), sds((seq, 512), BF16), sds((seq, 512), BF16),
                   sds((seq, 1024), BF16), sds((seq, 1024), BF16), sds((seq, 1024), BF16),
                   sds((seq, 512), F32), sds((seq, 512), F32), sds((seq, 512), F32),
                   sds((seq, 512), BF16), sds((seq, 512), BF16), sds((seq, 512), BF16),
                   sds((1, D_MODEL), F32), sds((1, D_MODEL), F32), sds((1, N_MERGE), F32), sds((1, 1), F32)],
        compiler_params=_cparams(("arbitrary",)),
    )(x, tgt, o_a, o_b, o_m, p32, p32, p32, wmg, bmg, wba, wbb, wbm, wout, ln_g, ln_b)


def _primed_weights(w):
    w_in = w["w_in"]
    zc = lambda n: jnp.zeros((D_MODEL, n), w_in.dtype)
    w_in_p = jnp.concatenate([w_in[:, 0:384], zc(64), w_in[:, 384:416], zc(32), w_in[:, 416:]], axis=1)
    wqb = jnp.pad(w["w_q_b"].reshape(MLA_Q_LORA, MLA_HEADS, 96), ((0, 0), (0, 0), (0, 32))).reshape(MLA_Q_LORA, 1024)
    kv3 = w["w_kv_b"].reshape(MLA_KV_LORA, MLA_HEADS, 128)
    wk = jnp.pad(kv3[:, :, :MLA_NOPE], ((0, 0), (0, 0), (0, 64))).reshape(MLA_KV_LORA, 1024)
    wv = kv3[:, :, MLA_NOPE:].reshape(MLA_KV_LORA, 512)
    return w_in_p, wqb, jnp.concatenate([wk, wv], axis=1)


PROJ_BLK = 512


def _grad_x(du, dgpre, wmg, d_proj, w_in_p, *, tm, rider=None):
    seq = du.shape[0]
    n_pieces = len(d_proj)

    def body(du_ref, dg_ref, wmg_ref, *rest):
        piece_refs, win_ref, out_ref = rest[:n_pieces], rest[n_pieces], rest[n_pieces + 1]
        d_p = jnp.concatenate([_bf(p_ref[...]) for p_ref in piece_refs], axis=1)
        out_ref[...] = (DEEPNORM_ALPHA * du_ref[...] + _dot_nt_cols(dg_ref[...], wmg_ref)) + _dot_nt(d_p, win_ref[...])

    row = lambda w: pl.BlockSpec((tm, w), lambda i: (i, 0))
    whole = lambda a: pl.BlockSpec(a.shape, lambda i: (0,) * a.ndim)
    return _call_with_rider(
        body, rider, name="grad_x", grid=(seq // tm,),
        in_specs=[row(D_MODEL), row(N_MERGE), whole(wmg)] + [row(PROJ_BLK) for _ in d_proj] + [whole(w_in_p)],
        out_specs=[row(D_MODEL)], out_shape=[jax.ShapeDtypeStruct((seq, D_MODEL), F32)],
        args=(du, dgpre, wmg, *d_proj, w_in_p), semantics=("parallel",))


def _grad_w_in(x, d_proj, *, tk):
    seq = x.shape[0]
    n_pieces = len(d_proj)
    nk = seq // tk

    def body(x_ref, *rest):
        piece_refs, out_ref, acc = rest[:n_pieces], rest[n_pieces], rest[n_pieces + 1]
        j, kk = pl.program_id(0), pl.program_id(1)

        @pl.when(kk == 0)
        def _():
            acc[...] = jnp.zeros_like(acc)

        xb = _bf(x_ref[...])
        for pair in range(n_pieces // 2):
            @pl.when(j == pair)
            def _(pair=pair):
                both = jnp.concatenate([_bf(piece_refs[2 * pair][...]), _bf(piece_refs[2 * pair + 1][...])], axis=1)
                acc[...] += _dot_tn(xb, both)

        @pl.when(kk == nk - 1)
        def _():
            out_ref[...] = acc[...]

    def piece_spec(s):
        return pl.BlockSpec((tk, PROJ_BLK), lambda j, kk: (jnp.where(j == s // 2, kk, 0), 0))

    return pl.pallas_call(
        body, name="grad_w_in", grid=(n_pieces // 2, nk),
        in_specs=[pl.BlockSpec((tk, D_MODEL), lambda j, kk: (kk, 0))] + [piece_spec(s) for s in range(n_pieces)],
        out_specs=pl.BlockSpec((D_MODEL, 2 * PROJ_BLK), lambda j, kk: (0, j)),
        out_shape=jax.ShapeDtypeStruct((D_MODEL, n_pieces * PROJ_BLK), F32),
        scratch_shapes=[pltpu.VMEM((D_MODEL, 2 * PROJ_BLK), F32)],
        compiler_params=_cparams(("parallel", "arbitrary")),
    )(x, *d_proj)


EARLY_NAMES = ("w_mem_kv", "w_branch_mla", "w_branch_sb", "w_branch_mem", "w_merge_gate", "w_out")
LATE_NAMES = ("w_in", "w_q_b", "w_kv_b")


def _remote(src, dst, send_sem, recv_sem, device):
    return pltpu.make_async_remote_copy(src_ref=src, dst_ref=dst, send_sem=send_sem, recv_sem=recv_sem, device_id=device,
                                        device_id_type=MESH_ID)


def _gather_rider(shards):
    n = len(shards)

    def copies(src_refs, out_refs, sems):
        send_sems, recv_sems, local_sems = sems
        x, y, c = _place()
        me = 2 * x + y
        out = []
        for a, (s, o) in enumerate(zip(src_refs, out_refs)):
            out.append(pltpu.make_async_copy(s, o.at[me], local_sems.at[a]))
            for k, (px, py) in enumerate(_other_chips(x, y)):
                out.append(_Exchange(_remote(s, o.at[me], send_sems.at[k, a], recv_sems.at[k, a], (px, py, c)),
                                     _remote(s, o.at[2 * px + py], send_sems.at[k, a], recv_sems.at[k, a], (px, py, c))))
        return out

    return _Rider(shards, [jax.ShapeDtypeStruct((N_CHIPS,) + s.shape, s.dtype) for s in shards],
                  [pltpu.SemaphoreType.DMA((3, n)), pltpu.SemaphoreType.DMA((3, n)), pltpu.SemaphoreType.DMA((n,))], copies)


def _sibling_rider(g4):
    n = len(g4)

    def copies(g_refs, out_refs, sems):
        send_sems, recv_sems = sems
        x, y, c = _place()
        out = []
        for a, (g, o) in enumerate(zip(g_refs, out_refs)):
            half = g.shape[1] // 2
            theirs = pl.ds(pl.multiple_of((1 - c) * half, 8), half)
            cp = _remote(g.at[:, theirs, :], o, send_sems.at[a], recv_sems.at[a], (x, y, 1 - c))
            out.append(_Exchange(cp, cp))
        return out

    return _Rider(g4, [jax.ShapeDtypeStruct((N_CHIPS, g.shape[1] // 2, g.shape[2]), g.dtype) for g in g4],
                  [pltpu.SemaphoreType.DMA((n,)), pltpu.SemaphoreType.DMA((n,))], copies)


def _chips_rider(wire):
    n = len(wire)

    def copies(s_refs, out_refs, sems):
        send_sems, recv_sems = sems
        x, y, c = _place()
        out = []
        for a, (s, o) in enumerate(zip(s_refs, out_refs)):
            for k, (px, py) in enumerate(_other_chips(x, y)):
                cp = _remote(s.at[2 * px + py], o.at[RELATION_XOR[k] - 1], send_sems.at[k, a], recv_sems.at[k, a], (px, py, c))
                out.append(_Exchange(cp, cp))
        return out

    return _Rider(wire, [jax.ShapeDtypeStruct((3,) + s.shape[1:], s.dtype) for s in wire],
                  [pltpu.SemaphoreType.DMA((3, n)), pltpu.SemaphoreType.DMA((3, n))], copies)


def _local_step(x, mem, tgt, w, small, *, tq, tq_sb_bwd, tk, tk_mla, t_row, t_mm, t_wg, rest_shards=None):
    seq = x.shape[0]
    on_mesh = rest_shards is not None
    w_in_p, wqb, wkvb = _primed_weights(w)
    tabs = _rope_tables(seq)

    p32, pbf = _matmul(x, w_in_p, mode="nn", tm=256, tn=IN_WIDTH_P, tk=D_MODEL, out_dtypes=(F32, BF16), name="proj_in")
    qp, kp, vp = _mla_prep(p32, small["q_a_gain"], small["kv_a_gain"], wqb, wkvb, tabs, t=t_row)
    res = _mla_attn_fwd(qp, kp, vp, tq=tq, tk=tk_mla, tk_diag=tk, rider=_gather_rider(rest_shards) if on_mesh else None)
    o_a, lse = res[0], res[1]
    if on_mesh:
        w = dict(w, **{n: g if n in COL_SHARDED else _join_chips(n, g) for n, g in zip(EARLY_NAMES, res[2:])})
    else:
        w = dict(w, **{n: _split_by_chip(n, w[n]) for n in EARLY_NAMES if n in COL_SHARDED})
    wmg, wout = w["w_merge_gate"], w["w_out"]
    wba, wbb, wbm = w["w_branch_mla"], w["w_branch_sb"], w["w_branch_mem"]
    o_b, keep_total = _sb_attn_fwd(pbf, tq=tq, tk=tk)
    (mkv,) = _matmul(mem, w["w_mem_kv"], mode="nn", tm=MEM_LEN, tn=512, tk=D_MODEL, out_dtypes=(BF16,), name="mem_kv")
    o_m = _mem_fwd(pbf, mkv, t=t_row)

    (du, merged, dgpre, h_a, h_b, h_m, dy_a, dy_b, dy_m, do_a, do_b, do_m, dgate_a, dgate_b, dgate_m,
     d_ln_g, d_ln_b, d_bmg, loss) = _mid(x, tgt, o_a, o_b, o_m, p32, wmg, small["b_merge_gate"], wba, wbb, wbm, wout,
                                         small["ln_gain"], small["ln_bias"], t=t_row)

    wg = functools.partial(_matmul, mode="tn", tm=512, out_dtypes=(F32,))
    shard = (lambda width: dict(tn=width // N_CHIPS, by_column_block=True)) if on_mesh else (lambda width: dict(tn=1024))
    dq_m, dmkv = _mem_bwd(pbf, mkv, do_m, t=t_row)
    early = {"w_mem_kv": wg(mem, dmkv, tk=MEM_LEN, tn=1024, name="grad_w_mem_kv")[0],
             "w_branch_mla": wg(h_a, dy_a, tk=t_wg, name="grad_w_branch_a", **shard(D_MODEL))[0],
             "w_branch_sb": wg(h_b, dy_b, tk=t_wg, name="grad_w_branch_b", **shard(D_MODEL))[0],
             "w_branch_mem": wg(h_m, dy_m, tk=t_wg, name="grad_w_branch_m", **shard(D_MODEL))[0],
             "w_merge_gate": wg(x, dgpre, tk=t_wg, name="grad_w_merge_gate", **shard(N_MERGE))[0],
             "w_out": wg(merged, du, tk=t_wg, tn=1024, name="grad_w_out")[0]}

    if on_mesh:
        g4 = [early[n] if early[n].ndim == 3 else _split_by_chip(n, early[n]) for n in EARLY_NAMES]
        res = _mla_attn_bwd(qp, kp, vp, o_a, lse, do_a, tq=tq, tk=tk_mla, tk_diag=tk, rider=_sibling_rider(g4))
        (dqp, dkp, dvp), got = res[:3], res[3:]
        chipsum, wire = _rs_add_sibling(g4, got, [BF16] * len(g4))
        res = _sb_attn_bwd(pbf, keep_total, do_b, tq=tq_sb_bwd, tk=tk, rider=_chips_rider(wire))
        (dq_b, dk_b, dv_b), parts = res[:3], res[3:]
        early = _rs_add_chips(chipsum, parts)
    else:
        dqp, dkp, dvp = _mla_attn_bwd(qp, kp, vp, o_a, lse, do_a, tq=tq, tk=tk_mla, tk_diag=tk)
        dq_b, dk_b, dv_b = _sb_attn_bwd(pbf, keep_total, do_b, tq=tq_sb_bwd, tk=tk)
    dlat, d_wqb, d_wkvb, d_gq, d_gkv = _mla_post(p32, dqp, dkp, dvp, small["q_a_gain"], small["kv_a_gain"], wqb, wkvb, tabs,
                                                 t=t_row)

    d_proj = [dlat, dgate_a, dq_b, dk_b, dv_b, dgate_b, dq_m, dgate_m]
    d_winp = _grad_w_in(x, d_proj, tk=min(1024, seq))

    d_win = jnp.concatenate([d_winp[:, 0:384], d_winp[:, 448:480], d_winp[:, 512:]], axis=1)
    d_wq = d_wqb.reshape(MLA_Q_LORA, MLA_HEADS, 128)[:, :, :96].reshape(MLA_Q_LORA, 768)
    d_wk = d_wkvb[:, :1024].reshape(MLA_KV_LORA, MLA_HEADS, 128)[:, :, :MLA_NOPE]
    d_wv = d_wkvb[:, 1024:].reshape(MLA_KV_LORA, MLA_HEADS, MLA_V)
    d_wkv = jnp.concatenate([d_wk, d_wv], axis=2).reshape(MLA_KV_LORA, 1024)
    late = {"w_in": d_win, "w_q_b": d_wq, "w_kv_b": d_wkv}
    small_grads = {"q_a_gain": d_gq, "kv_a_gain": d_gkv, "b_merge_gate": d_bmg, "ln_gain": d_ln_g, "ln_bias": d_ln_b}
    if not on_mesh:
        (grad_x,) = _grad_x(du, dgpre, wmg, d_proj, w_in_p, tm=256)
        return loss[0, 0], grad_x, late, small_grads, early

    g4 = [_split_by_chip(n, late[n]) for n in LATE_NAMES]
    g4.append(jnp.broadcast_to(_pack_small(small_grads, scalar=loss)[None], (N_CHIPS, SMALL_ROWS, PACK_COLS)))
    got = _rs_to_sibling(g4)
    chipsum, wire = _rs_add_sibling(g4, got, [BF16] * len(LATE_NAMES) + [F32])
    res = _grad_x(du, dgpre, wmg, d_proj, w_in_p, tm=256, rider=_chips_rider(wire))
    late_mine = _rs_add_chips(chipsum, res[1:])
    return loss[0, 0], res[0], late_mine, None, early


def _place():
    x, y, c = lax.axis_index("x"), lax.axis_index("y"), lax.axis_index("c")
    return x, y, c


def _other_chips(x, y):
    return ((1 - x, y), (x, 1 - y), (1 - x, 1 - y))


SMALL_ROWS = 64
ADAM_STEPS_PER_HALF = 4


def _pack_small(d, scalar=None):
    parts = [d[n].reshape(-1) for n, _ in SMALL_SIZES] + ([] if scalar is None else [scalar.reshape(1)])
    flat = jnp.concatenate(parts)
    return jnp.pad(flat, (0, SMALL_ROWS * PACK_COLS - flat.shape[0])).reshape(SMALL_ROWS, PACK_COLS)


def _unpack_small(a):
    flat, out, c0 = a.reshape(-1), {}, 0
    for n, size in SMALL_SIZES:
        out[n] = flat[c0:c0 + size].reshape(1, size)
        c0 += size
    return out


def _split_by_chip(name, full):
    r, c = full.shape
    if name in COL_SHARDED:
        return full.reshape(r, N_CHIPS, c // N_CHIPS).transpose(1, 0, 2)
    return full.reshape(N_CHIPS, r // N_CHIPS, c)


def _join_chips(name, slots):
    _, r, cs = slots.shape
    if name in COL_SHARDED:
        return slots.transpose(1, 0, 2).reshape(r, N_CHIPS * cs)
    return slots.reshape(N_CHIPS * r, cs)


HBM_SPEC = pl.BlockSpec(memory_space=pltpu.HBM)


def _gather_shards(shards):
    n = len(shards)

    def body(*refs):
        w_refs, out_refs, wb_refs = refs[:n], refs[n:2 * n], refs[2 * n:3 * n]
        send_sems, recv_sems, pass_send_sems, pass_recv_sems, local_sems = refs[3 * n:]
        x, y, c = _place()
        me = 2 * x + y
        sibling = (x, y, 1 - c)

        def halves(ref):
            half = ref.shape[-2] // 2
            return (pl.ds(pl.multiple_of(c * half, 16), half), pl.ds(pl.multiple_of((1 - c) * half, 16), half))
        for w_ref, wb_ref in zip(w_refs, wb_refs):
            rows = w_ref.shape[0]
            chunk = min(rows, 128)

            def cast(i, carry, w_ref=w_ref, wb_ref=wb_ref, chunk=chunk):
                r0 = pl.multiple_of(i * chunk, chunk)
                wb_ref[pl.ds(r0, chunk), :] = _bf(w_ref[pl.ds(r0, chunk), :])
                return carry

            lax.fori_loop(0, rows // chunk, cast, 0)
        sends, locals_ = [], []
        for a, (wb_ref, out_ref) in enumerate(zip(wb_refs, out_refs)):
            mine = pltpu.make_async_copy(wb_ref, out_ref.at[me], local_sems.at[a])
            mine.start()
            locals_.append(mine)
            mine_rows, _ = halves(wb_ref)
            for k, (px, py) in enumerate(_other_chips(x, y)):
                cp = pltpu.make_async_remote_copy(src_ref=wb_ref.at[mine_rows, :], dst_ref=out_ref.at[me, mine_rows, :],
                                                  send_sem=send_sems.at[k, a], recv_sem=recv_sems.at[k, a],
                                                  device_id=(px, py, c), device_id_type=MESH_ID)
                cp.start()
                sends.append(cp)
        for a, (wb_ref, out_ref) in enumerate(zip(wb_refs, out_refs)):
            mine_rows, _ = halves(wb_ref)
            for k, (px, py) in enumerate(_other_chips(x, y)):
                landed = out_ref.at[2 * px + py, mine_rows, :]
                pltpu.make_async_remote_copy(src_ref=wb_ref.at[mine_rows, :], dst_ref=landed, send_sem=send_sems.at[k, a],
                                             recv_sem=recv_sems.at[k, a], device_id=(px, py, c),
                                             device_id_type=MESH_ID).wait_recv()
                cp = pltpu.make_async_remote_copy(src_ref=landed, dst_ref=landed, send_sem=pass_send_sems.at[k, a],
                                                  recv_sem=pass_recv_sems.at[k, a], device_id=sibling, device_id_type=MESH_ID)
                cp.start()
                sends.append(cp)
        for a, (wb_ref, out_ref) in enumerate(zip(wb_refs, out_refs)):
            _, their_rows = halves(wb_ref)
            for k, (px, py) in enumerate(_other_chips(x, y)):
                passed = out_ref.at[2 * px + py, their_rows, :]
                pltpu.make_async_remote_copy(src_ref=passed, dst_ref=passed, send_sem=pass_send_sems.at[k, a],
                                             recv_sem=pass_recv_sems.at[k, a], device_id=sibling,
                                             device_id_type=MESH_ID).wait_recv()
        for cp in sends:
            cp.wait_send()
        for cp in locals_:
            cp.wait()

    return pl.pallas_call(
        body, name="gather_weights",
        in_specs=[pl.BlockSpec(memory_space=pltpu.VMEM)] * n,
        out_specs=[HBM_SPEC] * n,
        out_shape=[jax.ShapeDtypeStruct((N_CHIPS,) + s.shape, BF16) for s in shards],
        scratch_shapes=[pltpu.VMEM(s.shape, BF16) for s in shards]
        + [pltpu.SemaphoreType.DMA((3, n))] * 4 + [pltpu.SemaphoreType.DMA((n,))],
        compiler_params=pltpu.CompilerParams(vmem_limit_bytes=VMEM_LIMIT),
    )(*shards)


def _cast_bf16_list(arrays):
    def body(*refs):
        for a_ref, o_ref in zip(refs[:len(arrays)], refs[len(arrays):]):
            o_ref[...] = _bf(a_ref[...])

    specs = [pl.BlockSpec((a.shape[0] // 4, a.shape[1]), lambda i: (i, 0)) for a in arrays]
    return pl.pallas_call(
        body, name="cast_shards", grid=(4,), in_specs=specs, out_specs=specs,
        out_shape=[jax.ShapeDtypeStruct(a.shape, BF16) for a in arrays],
        compiler_params=_cparams(("parallel",)),
    )(*arrays)


def _rs_to_sibling(g4):
    n = len(g4)

    def body(*refs):
        g_refs, out_refs = refs[:n], refs[n:2 * n]
        send_sems, recv_sems = refs[2 * n:]
        x, y, c = _place()
        copies = []
        for a, (g_ref, out_ref) in enumerate(zip(g_refs, out_refs)):
            half = g_ref.shape[1] // 2
            theirs = pl.ds(pl.multiple_of((1 - c) * half, 8), half)
            copies.append(pltpu.make_async_remote_copy(src_ref=g_ref.at[:, theirs, :], dst_ref=out_ref, send_sem=send_sems.at[a],
                                                       recv_sem=recv_sems.at[a], device_id=(x, y, 1 - c),
                                                       device_id_type=MESH_ID))
        for cp in copies:
            cp.start()
        for cp in copies:
            cp.wait()

    return pl.pallas_call(
        body, name="rs_sibling", in_specs=[HBM_SPEC] * n, out_specs=[HBM_SPEC] * n,
        out_shape=[jax.ShapeDtypeStruct((N_CHIPS, g.shape[1] // 2, g.shape[2]), F32) for g in g4],
        scratch_shapes=[pltpu.SemaphoreType.DMA((n,)), pltpu.SemaphoreType.DMA((n,))],
    )(*g4)


def _rs_add_sibling(g4, got, wire_dtypes):
    n = len(g4)
    narrow = [a for a in range(n) if wire_dtypes[a] != F32]

    def body(c_ref, *refs):
        outs = refs[2 * n:3 * n]
        wires = dict(zip(narrow, refs[3 * n:]))
        for a, (g_ref, r_ref, o_ref) in enumerate(zip(refs[:n], refs[n:2 * n], outs)):
            s = g_ref[...] + r_ref[...]
            o_ref[...] = s
            if a in wires:
                wires[a][...] = s.astype(wires[a].dtype)

    blk = lambda r: (1, r.shape[1], r.shape[2])
    plain = lambda r: pl.BlockSpec(blk(r), lambda j, c_ref: (j, 0, 0))
    grid_spec = pltpu.PrefetchScalarGridSpec(
        num_scalar_prefetch=1, grid=(N_CHIPS,),
        in_specs=[pl.BlockSpec(blk(r), lambda j, c_ref: (j, c_ref[0], 0)) for r in got] + [plain(r) for r in got],
        out_specs=[plain(r) for r in got] + [plain(got[a]) for a in narrow])
    res = pl.pallas_call(
        body, name="rs_add_sibling", grid_spec=grid_spec,
        out_shape=[jax.ShapeDtypeStruct(r.shape, F32) for r in got]
        + [jax.ShapeDtypeStruct(got[a].shape, wire_dtypes[a]) for a in narrow],
        compiler_params=_cparams(("parallel",)),
    )(lax.axis_index("c").astype(jnp.int32).reshape(1), *g4, *got)
    chipsum = list(res[:n])
    wire = list(chipsum)
    for a, w in zip(narrow, res[n:]):
        wire[a] = w
    return chipsum, wire


RELATION_XOR = (2, 1, 3)


def _rs_add_chips(chipsum, parts):
    n = len(parts)

    def body(me_ref, *refs):
        me = me_ref[0]
        for s_ref, p_ref, o_ref in zip(refs[:n], refs[n:2 * n], refs[2 * n:]):
            own = s_ref[0]
            total = None
            for k in range(N_CHIPS):
                theirs = p_ref[jnp.maximum(jnp.bitwise_xor(me, k) - 1, 0)].astype(F32)
                term = jnp.where(me == k, own, theirs)
                total = term if total is None else total + term
            o_ref[...] = total

    grid_spec = pltpu.PrefetchScalarGridSpec(
        num_scalar_prefetch=1, grid=(2,),
        in_specs=[pl.BlockSpec((1, p.shape[1] // 2, p.shape[2]), lambda i, me_ref: (me_ref[0], i, 0)) for p in parts]
        + [pl.BlockSpec((3, p.shape[1] // 2, p.shape[2]), lambda i, me_ref: (0, i, 0)) for p in parts],
        out_specs=[pl.BlockSpec((p.shape[1] // 2, p.shape[2]), lambda i, me_ref: (i, 0)) for p in parts])
    me = (2 * lax.axis_index("x") + lax.axis_index("y")).astype(jnp.int32).reshape(1)
    return pl.pallas_call(
        body, name="rs_add_chips", grid_spec=grid_spec,
        out_shape=[jax.ShapeDtypeStruct(p.shape[1:], F32) for p in parts],
        compiler_params=_cparams(("parallel",)),
    )(me, *chipsum, *parts)


def _rs_swap_halves(halves):
    n = len(halves)

    def body(*refs):
        h_refs, out_refs = refs[:n], refs[n:2 * n]
        send_sems, recv_sems = refs[2 * n:]
        x, y, c = _place()
        copies = [pltpu.make_async_remote_copy(src_ref=h_ref, dst_ref=out_ref, send_sem=send_sems.at[a], recv_sem=recv_sems.at[a],
                                               device_id=(x, y, 1 - c), device_id_type=MESH_ID)
                  for a, (h_ref, out_ref) in enumerate(zip(h_refs, out_refs))]
        for cp in copies:
            cp.start()
        for cp in copies:
            cp.wait()

    return pl.pallas_call(
        body, name="rs_swap_halves", in_specs=[HBM_SPEC] * n, out_specs=[HBM_SPEC] * n,
        out_shape=[jax.ShapeDtypeStruct(h.shape, F32) for h in halves],
        scratch_shapes=[pltpu.SemaphoreType.DMA((n,)), pltpu.SemaphoreType.DMA((n,))],
    )(*halves)


def _adamw_list(ws, g_mine, g_theirs, ms, vs):
    n = len(ws)

    def body(c_ref, *refs):
        w_refs, gm_refs, gt_refs, m_refs, v_refs = (refs[k * n:(k + 1) * n] for k in range(5))
        g_refs, d_refs, nm_refs, nv_refs = (refs[k * n:(k + 1) * n] for k in range(5, 9))
        mine = (pl.program_id(0) // ADAM_STEPS_PER_HALF) == c_ref[0]
        for a in range(n):
            gv = jnp.where(mine, gm_refs[a][...], gt_refs[a][...])
            g_refs[a][...] = gv
            m_new = ADAM_B1 * m_refs[a][...] + (1.0 - ADAM_B1) * gv
            v_new = ADAM_B2 * v_refs[a][...] + (1.0 - ADAM_B2) * (gv * gv)
            m_hat = m_new / (1.0 - ADAM_B1 ** ADAM_STEP)
            v_hat = v_new / (1.0 - ADAM_B2 ** ADAM_STEP)
            d_refs[a][...] = -ADAM_LR * (m_hat / (jnp.sqrt(v_hat) + ADAM_EPS) + ADAM_WD * w_refs[a][...])
            nm_refs[a][...] = m_new
            nv_refs[a][...] = v_new

    steps = 2 * ADAM_STEPS_PER_HALF
    whole = [pl.BlockSpec((w.shape[0] // steps, w.shape[1]), lambda i, c_ref: (i, 0)) for w in ws]
    half = [pl.BlockSpec((w.shape[0] // steps, w.shape[1]), lambda i, c_ref: (i % ADAM_STEPS_PER_HALF, 0)) for w in ws]
    shapes = [jax.ShapeDtypeStruct(w.shape, F32) for w in ws]
    grid_spec = pltpu.PrefetchScalarGridSpec(num_scalar_prefetch=1, grid=(steps,),
                                             in_specs=whole + half + half + whole + whole, out_specs=whole * 4)
    res = pl.pallas_call(
        body, name="adamw", grid_spec=grid_spec, out_shape=shapes * 4,
        compiler_params=_cparams(("parallel",)),
    )(lax.axis_index("c").astype(jnp.int32).reshape(1), *ws, *g_mine, *g_theirs, *ms, *vs)
    return res[:n], res[n:2 * n], res[2 * n:3 * n], res[3 * n:]


WEIGHT_NAMES = ("w_in", "w_mem_kv", "q_a_gain", "w_q_b", "kv_a_gain", "w_kv_b", "w_branch_mla", "w_branch_sb",
                "w_branch_mem", "w_merge_gate", "b_merge_gate", "w_out", "ln_gain", "ln_bias")
SMALL_NAMES = tuple(n for n, _ in SMALL_SIZES)


def kernel(x, mem, w_in, w_mem_kv, q_a_gain, w_q_b, kv_a_gain, w_kv_b, w_branch_mla, w_branch_sb, w_branch_mem, w_merge_gate, b_merge_gate, w_out, ln_gain, ln_bias, loss_target, m_w_in, m_w_mem_kv, m_q_a_gain, m_w_q_b, m_kv_a_gain, m_w_kv_b, m_w_branch_mla, m_w_branch_sb, m_w_branch_mem, m_w_merge_gate, m_b_merge_gate, m_w_out, m_ln_gain, m_ln_bias, v_w_in, v_w_mem_kv, v_q_a_gain, v_w_q_b, v_kv_a_gain, v_w_kv_b, v_w_branch_mla, v_w_branch_sb, v_w_branch_mem, v_w_merge_gate, v_b_merge_gate, v_w_out, v_ln_gain, v_ln_bias):
    weights = dict(zip(WEIGHT_NAMES, (w_in, w_mem_kv, q_a_gain, w_q_b, kv_a_gain, w_kv_b, w_branch_mla, w_branch_sb,
                                      w_branch_mem, w_merge_gate, b_merge_gate, w_out, ln_gain, ln_bias)))
    mom1 = dict(zip(WEIGHT_NAMES, (m_w_in, m_w_mem_kv, m_q_a_gain, m_w_q_b, m_kv_a_gain, m_w_kv_b, m_w_branch_mla,
                                   m_w_branch_sb, m_w_branch_mem, m_w_merge_gate, m_b_merge_gate, m_w_out, m_ln_gain,
                                   m_ln_bias)))
    mom2 = dict(zip(WEIGHT_NAMES, (v_w_in, v_w_mem_kv, v_q_a_gain, v_w_q_b, v_kv_a_gain, v_w_kv_b, v_w_branch_mla,
                                   v_w_branch_sb, v_w_branch_mem, v_w_merge_gate, v_b_merge_gate, v_w_out, v_ln_gain,
                                   v_ln_bias)))
    def as_list(d):
        return [d[n][0] for n in BIG_NAMES] + [_pack_small({n: d[n] for n in SMALL_NAMES})]

    w_list, m_list, v_list = as_list(weights), as_list(mom1), as_list(mom2)

    gathered = _gather_shards([weights[n][0] for n in LATE_NAMES])
    first_w = {n: _join_chips(n, g) for n, g in zip(LATE_NAMES, gathered)}
    rest_shards = _cast_bf16_list([weights[n][0] for n in EARLY_NAMES])
    small = {n: weights[n] for n in SMALL_NAMES}

    seq = x.shape[1]
    _, grad_x, late_mine, _, early_mine = _local_step(
        x[0], mem[0], loss_target[0], first_w, small, tq=min(1024, seq), tq_sb_bwd=1024, tk=256, tk_mla=512, t_row=256, t_mm=512,
        t_wg=min(2048, seq), rest_shards=rest_shards)
    by_name = dict(zip(EARLY_NAMES + LATE_NAMES + ("small",), list(early_mine) + list(late_mine)))
    mine = [by_name[n] for n in BIG_NAMES + ("small",)]
    theirs = _rs_swap_halves(mine)
    g_list, d_list, nm_list, nv_list = _adamw_list(w_list, mine, theirs, m_list, v_list)

    loss = g_list[-1].reshape(-1)[SMALL_TOTAL]
    outs = [loss, grad_x[None]]
    for arrays in (g_list, d_list, nm_list, nv_list):
        big = dict(zip(BIG_NAMES, arrays[:-1]))
        sm = _unpack_small(arrays[-1])
        outs.extend(big[n][None] if n in big else sm[n] for n in WEIGHT_NAMES)
    return tuple(outs)
```

```python
import functools
import math

import numpy as np
import jax
import jax.numpy as jnp
from jax import lax
from jax.experimental import pallas as pl
from jax.experimental.pallas import tpu as pltpu

F32 = jnp.float32
BF16 = jnp.bfloat16
MESH_ID = pl.DeviceIdType.MESH

D_MODEL = 1024
MEM_LEN = 256
MLA_HEADS = 8
MLA_NOPE = 64
MLA_ROPE = 32
MLA_V = 64
MLA_Q_LORA = 256
MLA_KV_LORA = 128
SB_HEADS = 8
SB_HEAD_DIM = 64
MEM_HEADS = 4
MEM_HEAD_DIM = 128
ROPE_BASE = 10000.0
RMS_EPS = 1e-6
LN_EPS = 1e-5
DEEPNORM_ALPHA = 2.0 ** 0.25
MLA_SCALE = 1.0 / math.sqrt(MLA_NOPE + MLA_ROPE)
SB_SCALE = 1.0 / math.sqrt(SB_HEAD_DIM)
MEM_SCALE = 1.0 / math.sqrt(MEM_HEAD_DIM)

ADAM_LR = 0.001
ADAM_B1 = 0.9
ADAM_B2 = 0.999
ADAM_EPS = 1e-08
ADAM_WD = 0.01
ADAM_STEP = 10

LANES = 128
HALF = 64
N_CHIPS = 4
PACK_COLS = 1024
VMEM_LIMIT = 56 * 1024 * 1024

IN_WIDTH_P = 4096
BLK_LAT, BLK_GATE_A, BLK_QB, BLK_KB, BLK_VB, BLK_GATE_B, BLK_QM, BLK_GATE_M = range(8)
P32_POS = {blk: n for n, blk in enumerate((BLK_LAT, BLK_GATE_A, BLK_GATE_B, BLK_GATE_M))}
PBF_POS = {blk: n for n, blk in enumerate((BLK_QB, BLK_KB, BLK_VB, BLK_QM))}
N_MERGE = 3 * D_MODEL

BIG_NAMES = ("w_in", "w_mem_kv", "w_q_b", "w_kv_b", "w_branch_mla", "w_branch_sb", "w_branch_mem", "w_merge_gate", "w_out")
COL_SHARDED = ("w_in", "w_q_b", "w_kv_b", "w_branch_mla", "w_branch_sb", "w_branch_mem", "w_merge_gate")
SMALL_SIZES = (("q_a_gain", 256), ("kv_a_gain", 128), ("b_merge_gate", 3072), ("ln_gain", 1024), ("ln_bias", 1024))
SMALL_TOTAL = sum(s for _, s in SMALL_SIZES)


def _cparams(sem=None):
    return pltpu.CompilerParams(dimension_semantics=sem, vmem_limit_bytes=VMEM_LIMIT)


def _dot(a, b):
    return jnp.dot(a, b, preferred_element_type=F32)


def _dot_nt(a, b):
    return lax.dot_general(a, b, (((1,), (1,)), ((), ())), preferred_element_type=F32)


def _dot_tn(a, b):
    return lax.dot_general(a, b, (((0,), (0,)), ((), ())), preferred_element_type=F32)


def _bf(x):
    return x.astype(BF16)


def _dot_cols(a, w_ref):
    return jnp.concatenate([_dot(a, w_ref[j]) for j in range(w_ref.shape[0])], axis=1)


def _dot_nt_cols(a, w_ref):
    cs = w_ref.shape[2]
    out = None
    for j in range(w_ref.shape[0]):
        term = _dot_nt(a[:, j * cs:(j + 1) * cs], w_ref[j])
        out = term if out is None else out + term
    return out


def _sigmoid(x):
    return 1.0 / (1.0 + jnp.exp(-x))


def _matmul(a, b, *, mode, tm, tn, tk, out_dtypes, name, add=None, add_scale=1.0, by_column_block=False):
    if mode == "nn":
        (m, k), n = a.shape, b.shape[1]
        a_spec = pl.BlockSpec((tm, tk), lambda i, j, kk: (i, kk))
        b_spec = pl.BlockSpec((tk, tn), lambda i, j, kk: (kk, j))
        dot = _dot
    elif mode == "nt":
        (m, k), n = a.shape, b.shape[0]
        a_spec = pl.BlockSpec((tm, tk), lambda i, j, kk: (i, kk))
        b_spec = pl.BlockSpec((tn, tk), lambda i, j, kk: (j, kk))
        dot = _dot_nt
    else:
        (k, m), n = a.shape, b.shape[1]
        a_spec = pl.BlockSpec((tk, tm), lambda i, j, kk: (kk, i))
        b_spec = pl.BlockSpec((tk, tn), lambda i, j, kk: (kk, j))
        dot = _dot_tn
    assert m % tm == 0 and n % tn == 0 and k % tk == 0, (name, m, n, k)
    nk = k // tk
    n_out = len(out_dtypes)
    has_add = add is not None

    def body(*refs):
        a_ref, b_ref = refs[0], refs[1]
        add_ref = refs[2] if has_add else None
        outs = refs[2 + has_add: 2 + has_add + n_out]
        acc = refs[-1]
        kk = pl.program_id(2)

        @pl.when(kk == 0)
        def _():
            acc[...] = jnp.zeros_like(acc)

        acc[...] += dot(_bf(a_ref[...]), _bf(b_ref[...]))

        @pl.when(kk == nk - 1)
        def _():
            r = acc[...]
            if has_add:
                r = r + add_scale * add_ref[...]
            for o in outs:
                o[...] = r.astype(o.dtype)

    in_specs = [a_spec, b_spec]
    args = [a, b]
    if has_add:
        in_specs.append(pl.BlockSpec((tm, tn), lambda i, j, kk: (i, j)))
        args.append(add)
    if by_column_block:
        out_spec = pl.BlockSpec((None, tm, tn), lambda i, j, kk: (j, i, 0))
        out_dims = (n // tn, m, tn)
    else:
        out_spec = pl.BlockSpec((tm, tn), lambda i, j, kk: (i, j))
        out_dims = (m, n)
    res = pl.pallas_call(
        body, name=name, grid=(m // tm, n // tn, nk),
        in_specs=in_specs, out_specs=[out_spec] * n_out,
        out_shape=[jax.ShapeDtypeStruct(out_dims, dt) for dt in out_dtypes],
        scratch_shapes=[pltpu.VMEM((tm, tn), F32)],
        compiler_params=_cparams(("parallel", "parallel", "arbitrary")),
    )(*args)
    return res


def _rope_tables(seq):
    half = MLA_ROPE // 2
    freqs = ROPE_BASE ** (-jnp.arange(half, dtype=F32) / half)
    ang = jnp.arange(seq, dtype=jnp.int32).astype(F32)[:, None] * freqs[None, :]
    cos, sin = jnp.cos(ang), jnp.sin(ang)
    z = lambda w: jnp.zeros((seq, w), F32)
    c_q = jnp.concatenate([jnp.ones((seq, MLA_NOPE), F32), cos, cos, z(32)], axis=1)
    c_k = jnp.concatenate([z(MLA_NOPE), cos, cos, z(32)], axis=1)
    s_lo = jnp.concatenate([z(MLA_NOPE), -sin, z(half), z(32)], axis=1)
    s_hi = jnp.concatenate([z(MLA_NOPE), z(half), sin, z(32)], axis=1)
    return c_q, c_k, s_lo, s_hi


def _rope_fwd(x, c, s_lo, s_hi):
    return x * c + pltpu.roll(x, LANES - 16, 1) * s_lo + pltpu.roll(x, 16, 1) * s_hi


def _rope_bwd(d, c, s_lo, s_hi):
    return d * c - pltpu.roll(d, 16, 1) * s_hi - pltpu.roll(d, LANES - 16, 1) * s_lo


def _rms_fwd(x, g):
    r = lax.rsqrt(jnp.mean(x * x, axis=-1, keepdims=True) + RMS_EPS)
    xn = x * r
    return xn * g, xn, r


def _mla_prep(p32, gq, gkv, wqb, wkvb, tabs, *, t):
    seq = p32.shape[0]

    def body(lat_ref, gq_ref, gkv_ref, wqb_ref, wkvb_ref, cq_ref, ck_ref, slo_ref, shi_ref, q_ref, k_ref, v_ref):
        lat = lat_ref[...]
        slo, shi = slo_ref[...], shi_ref[...]
        nq, _, _ = _rms_fwd(lat[:, 0:MLA_Q_LORA], gq_ref[...])
        qa = _dot(_bf(nq), wqb_ref[...])
        cq = cq_ref[...]
        for h in range(MLA_HEADS):
            blk = qa[:, h * LANES:(h + 1) * LANES]
            q_ref[:, h * LANES:(h + 1) * LANES] = _bf(_rope_fwd(blk, cq, slo, shi))
        nkv, _, _ = _rms_fwd(lat[:, MLA_Q_LORA:MLA_Q_LORA + MLA_KV_LORA], gkv_ref[...])
        kv = _dot(_bf(nkv), wkvb_ref[...])
        kpe = _rope_fwd(lat[:, 384:512], ck_ref[...], slo, shi)
        for h in range(MLA_HEADS):
            k_ref[:, h * LANES:(h + 1) * LANES] = _bf(kv[:, h * LANES:(h + 1) * LANES] + kpe)
        v_ref[...] = _bf(kv[:, MLA_HEADS * LANES:])

    row = lambda w: pl.BlockSpec((t, w), lambda i: (i, 0))
    full = lambda shp: pl.BlockSpec(shp, lambda i: (0, 0))
    return pl.pallas_call(
        body, name="mla_prep", grid=(seq // t,),
        in_specs=[row(512), full((1, MLA_Q_LORA)), full((1, MLA_KV_LORA)), full(wqb.shape), full(wkvb.shape),
                  row(LANES), row(LANES), row(LANES), row(LANES)],
        out_specs=[row(1024), row(1024), row(512)],
        out_shape=[jax.ShapeDtypeStruct((seq, 1024), BF16), jax.ShapeDtypeStruct((seq, 1024), BF16),
                   jax.ShapeDtypeStruct((seq, 512), BF16)],
        compiler_params=_cparams(("parallel",)),
    )(p32, gq, gkv, wqb, wkvb, *tabs)


def _mla_post(p32, dq, dk, dv, gq, gkv, wqb, wkvb, tabs, *, t):
    seq = p32.shape[0]

    def body(lat_ref, dq_ref, dk_ref, dv_ref, gq_ref, gkv_ref, wqb_ref, wkvb_ref, cq_ref, ck_ref, slo_ref, shi_ref,
             dlat_ref, dwqb_ref, dwkvb_ref, dgq_ref, dgkv_ref):
        @pl.when(pl.program_id(0) == 0)
        def _():
            dwqb_ref[...] = jnp.zeros_like(dwqb_ref)
            dwkvb_ref[...] = jnp.zeros_like(dwkvb_ref)
            dgq_ref[...] = jnp.zeros_like(dgq_ref)
            dgkv_ref[...] = jnp.zeros_like(dgkv_ref)

        lat = lat_ref[...]
        slo, shi = slo_ref[...], shi_ref[...]
        cq = cq_ref[...]
        gq_v, gkv_v = gq_ref[...], gkv_ref[...]
        nq, xq, rq = _rms_fwd(lat[:, 0:MLA_Q_LORA], gq_v)
        nkv, xkv, rkv = _rms_fwd(lat[:, MLA_Q_LORA:MLA_Q_LORA + MLA_KV_LORA], gkv_v)

        dqa = jnp.concatenate(
            [_rope_bwd(dq_ref[:, h * LANES:(h + 1) * LANES], cq, slo, shi) for h in range(MLA_HEADS)], axis=1)
        dqa_b = _bf(dqa)
        dwqb_ref[...] += _dot_tn(_bf(nq), dqa_b)
        dnq = _dot_nt(dqa_b, wqb_ref[...])
        dgq_ref[...] += jnp.sum(dnq * xq, axis=0, keepdims=True)
        dxn = dnq * gq_v
        dcq = rq * (dxn - xq * jnp.mean(dxn * xq, axis=-1, keepdims=True))

        dkf = dk_ref[...]
        dkv_b = _bf(jnp.concatenate([dkf, dv_ref[...]], axis=1))
        dwkvb_ref[...] += _dot_tn(_bf(nkv), dkv_b)
        dnkv = _dot_nt(dkv_b, wkvb_ref[...])
        dgkv_ref[...] += jnp.sum(dnkv * xkv, axis=0, keepdims=True)
        dxn = dnkv * gkv_v
        dckv = rkv * (dxn - xkv * jnp.mean(dxn * xkv, axis=-1, keepdims=True))

        dkpe = dkf[:, 0:LANES]
        for h in range(1, MLA_HEADS):
            dkpe = dkpe + dkf[:, h * LANES:(h + 1) * LANES]
        dkr = _rope_bwd(dkpe, ck_ref[...], slo, shi)
        dlat_ref[...] = _bf(jnp.concatenate([dcq, dckv, dkr], axis=1))

    row = lambda w: pl.BlockSpec((t, w), lambda i: (i, 0))
    full = lambda shp: pl.BlockSpec(shp, lambda i: (0, 0))
    return pl.pallas_call(
        body, name="mla_post", grid=(seq // t,),
        in_specs=[row(512), row(1024), row(1024), row(512), full((1, MLA_Q_LORA)), full((1, MLA_KV_LORA)),
                  full(wqb.shape), full(wkvb.shape), row(LANES), row(LANES), row(LANES), row(LANES)],
        out_specs=[row(512), full(wqb.shape), full(wkvb.shape), full((1, MLA_Q_LORA)), full((1, MLA_KV_LORA))],
        out_shape=[jax.ShapeDtypeStruct((seq, 512), BF16), jax.ShapeDtypeStruct(wqb.shape, F32),
                   jax.ShapeDtypeStruct(wkvb.shape, F32), jax.ShapeDtypeStruct((1, MLA_Q_LORA), F32),
                   jax.ShapeDtypeStruct((1, MLA_KV_LORA), F32)],
        compiler_params=_cparams(("arbitrary",)),
    )(p32, dq, dk, dv, gq, gkv, wqb, wkvb, *tabs)


def _split_bf16(x):
    hi = _bf(x)
    return hi, _bf(x - hi.astype(F32))


def _tri_sum(x, u):
    hi, lo = _split_bf16(x)
    return _dot(hi, u) + _dot(lo, u)


def _softplus(z):
    return jnp.maximum(z, 0.0) + jnp.log(1.0 + jnp.exp(-jnp.abs(z)))


def _head_queries(q, left):
    zero = jnp.zeros_like(q)
    return jnp.where(left, q, zero) * SB_SCALE, jnp.where(left, zero, q) * SB_SCALE


ROW_GROUP = 128
SB_BWD_CHAINS_IN_FLIGHT = 16
ANY_HBM = pl.BlockSpec(memory_space=pltpu.HBM)


class _Exchange:
    def __init__(self, send, landing):
        self.send, self.landing = send, landing

    def start(self):
        self.send.start()

    def wait(self):
        self.landing.wait_recv()
        self.send.wait_send()


class _Rider:
    def __init__(self, operands, out_shapes, sem_shapes, copies):
        self.operands, self.out_shapes, self.sem_shapes, self.copies = list(operands), list(out_shapes), list(sem_shapes), copies


def _call_with_rider(body, rider, *, name, grid, in_specs, out_specs, out_shape, args, semantics, scratch=()):
    scratch = list(scratch)
    if rider is None:
        return pl.pallas_call(body, name=name, grid=grid, in_specs=in_specs, out_specs=out_specs, out_shape=out_shape,
                              scratch_shapes=scratch, compiler_params=_cparams(semantics))(*args)
    n_in, n_out, n_rin, n_rout = len(in_specs), len(out_specs), len(rider.operands), len(rider.out_shapes)

    def full_body(*refs):
        ins, r_ins = refs[:n_in], refs[n_in:n_in + n_rin]
        outs = refs[n_in + n_rin:n_in + n_rin + n_out]
        r_outs = refs[n_in + n_rin + n_out:n_in + n_rin + n_out + n_rout]
        rest = refs[n_in + n_rin + n_out + n_rout:]
        own_scratch, sems = rest[:len(scratch)], rest[len(scratch):]
        first, last = None, None
        for axis, size in enumerate(grid):
            at_start, at_end = pl.program_id(axis) == 0, pl.program_id(axis) == size - 1
            first = at_start if first is None else first & at_start
            last = at_end if last is None else last & at_end

        @pl.when(first)
        def _():
            for cp in rider.copies(r_ins, r_outs, sems):
                cp.start()

        body(*ins, *outs, *own_scratch)

        @pl.when(last)
        def _():
            for cp in rider.copies(r_ins, r_outs, sems):
                cp.wait()

    return pl.pallas_call(
        full_body, name=name, grid=grid, in_specs=list(in_specs) + [ANY_HBM] * n_rin,
        out_specs=list(out_specs) + [ANY_HBM] * n_rout, out_shape=list(out_shape) + rider.out_shapes,
        scratch_shapes=scratch + rider.sem_shapes, compiler_params=_cparams(("arbitrary",) * len(grid)),
    )(*args, *rider.operands)


def _chains(tq):
    return [(h, g) for g in range(tq // ROW_GROUP) for h in range(2)]


def _chain_pattern(g, m, tk, strict):
    r_lo, r_hi = g * ROW_GROUP, (g + 1) * ROW_GROUP - 1
    c_lo, c_hi = m * tk, (m + 1) * tk - 1
    if (c_lo >= r_hi) if strict else (c_lo > r_hi):
        return None
    if (c_hi < r_lo) if strict else (c_hi <= r_lo):
        return True
    rr = lax.broadcasted_iota(jnp.int32, (ROW_GROUP, tk), 0) + r_lo
    cc = lax.broadcasted_iota(jnp.int32, (ROW_GROUP, tk), 1) + c_lo
    return (cc < rr) if strict else (cc <= rr)


def _masked(x, pat, fill=0.0):
    return x if pat is True else jnp.where(pat, x, fill)


def _rows(g):
    return slice(g * ROW_GROUP, (g + 1) * ROW_GROUP)


def _tri_matrix(tk, cmp):
    rr = lax.broadcasted_iota(jnp.int32, (tk, tk), 0)
    cc = lax.broadcasted_iota(jnp.int32, (tk, tk), 1)
    return cmp(rr, cc).astype(BF16)


def _mla_attn_fwd(qp, kp, vp, *, tq, tk, tk_diag, rider=None):
    seq = qp.shape[0]
    neg = float(np.finfo(np.float32).min)
    chains = _chains(tq)

    def body(q_ref, k_ref, v_ref, o_ref, lse_ref):
        i = pl.program_id(1)
        left = lax.broadcasted_iota(jnp.int32, (tq, LANES), 1) < HALF
        qs = [q_ref[_rows(g), h * LANES:(h + 1) * LANES] for h, g in chains]

        def block(start, carry, m, tk):
            v = v_ref[pl.ds(start, tk), :]
            pats = [True if m is None else _chain_pattern(g, m, tk, False) for _, g in chains]
            live = [n for n, p in enumerate(pats) if p is not None]
            ss = {n: _dot_nt(qs[n], k_ref[pl.ds(start, tk), chains[n][0] * LANES:(chains[n][0] + 1) * LANES]) for n in live}
            new = list(carry)
            for n in live:
                m_old, l_old, acc = carry[n]
                s = _masked(ss[n] * MLA_SCALE, pats[n], neg)
                m_new = jnp.maximum(m_old, jnp.max(s, axis=-1, keepdims=True))
                a = jnp.exp(m_old - m_new)
                p = jnp.exp(s - m_new)
                new[n] = (m_new, a * l_old + jnp.sum(p, axis=-1, keepdims=True), a * acc + _dot(_bf(p), v))
            return tuple(new)

        init = (jnp.full((ROW_GROUP, 1), -1e30, F32), jnp.zeros((ROW_GROUP, 1), F32), jnp.zeros((ROW_GROUP, LANES), F32))
        def two_blocks(j, c):
            c = block(pl.multiple_of(2 * j * tk, tk), c, None, tk)
            return block(pl.multiple_of((2 * j + 1) * tk, tk), c, None, tk)

        carry = lax.fori_loop(0, i * (tq // tk) // 2, two_blocks, (init,) * len(chains))
        for m in range(tq // tk_diag):
            carry = block(pl.multiple_of(i * tq + m * tk_diag, tk_diag), carry, m, tk_diag)
        per_head = []
        for h in range(2):
            mine = [carry[n] for n, (ch, _) in enumerate(chains) if ch == h]
            per_head.append((jnp.concatenate([acc / l for _, l, acc in mine], axis=0),
                             jnp.concatenate([mm + jnp.log(l) for mm, l, _ in mine], axis=0)))
        o_ref[...] = jnp.where(left, per_head[0][0], per_head[1][0])
        lse_ref[...] = jnp.where(left, per_head[0][1], per_head[1][1])

    return _call_with_rider(
        body, rider, name="mla_fwd", grid=(MLA_HEADS // 2, seq // tq),
        in_specs=[pl.BlockSpec((tq, 2 * LANES), lambda p, i: (i, p)), pl.BlockSpec((seq, 2 * LANES), lambda p, i: (0, p)),
                  pl.BlockSpec((seq, LANES), lambda p, i: (0, p))],
        out_specs=[pl.BlockSpec((tq, LANES), lambda p, i: (i, p)), pl.BlockSpec((tq, LANES), lambda p, i: (i, p))],
        out_shape=[jax.ShapeDtypeStruct((seq, 512), F32), jax.ShapeDtypeStruct((seq, 512), F32)],
        args=(qp, kp, vp), semantics=("parallel", "parallel"))


def _mla_attn_bwd(qp, kp, vp, o, lse, do, *, tq, tk, tk_diag, rider=None):
    seq = qp.shape[0]
    chains = _chains(tq)

    def body(q_ref, k_ref, v_ref, o_ref, lse_ref, do_ref, dq_ref, dk_ref, dv_ref, qt_ref, dot_ref):
        i = pl.program_id(1)

        @pl.when(i == 0)
        def _():
            dk_ref[...] = jnp.zeros_like(dk_ref)
            dv_ref[...] = jnp.zeros_like(dv_ref)

        left = lax.broadcasted_iota(jnp.int32, (tq, LANES), 1) < HALF
        do_f = do_ref[...]
        prod = do_f * o_ref[...]
        lse_v = lse_ref[...]
        do_heads = (_bf(jnp.where(left, do_f, 0.0)), _bf(jnp.where(left, 0.0, do_f)))
        delta_heads = (jnp.sum(jnp.where(left, prod, 0.0), axis=-1, keepdims=True),
                       jnp.sum(jnp.where(left, 0.0, prod), axis=-1, keepdims=True))
        qs = [q_ref[_rows(g), h * LANES:(h + 1) * LANES] for h, g in chains]
        dos = [do_heads[h][_rows(g)] for h, g in chains]
        deltas = [delta_heads[h][_rows(g)] for h, g in chains]
        lses = [lse_v[_rows(g), h * HALF:h * HALF + 1] for h, g in chains]
        for h in range(2):
            qt_ref[h] = q_ref[:, h * LANES:(h + 1) * LANES].T
            dot_ref[h] = do_heads[h].T
        q_t = [qt_ref.at[h] for h in range(2)]
        do_t = [dot_ref.at[h] for h in range(2)]

        def block(start, carry, m, tk):
            v = v_ref[pl.ds(start, tk), :]
            pats = [True if m is None else _chain_pattern(g, m, tk, False) for _, g in chains]
            live = [n for n, p in enumerate(pats) if p is not None]
            ks = [k_ref[pl.ds(start, tk), h * LANES:(h + 1) * LANES] for h in range(2)]
            ss = {n: _dot_nt(qs[n], ks[chains[n][0]]) for n in live}
            dps = {n: _dot_nt(dos[n], v) for n in live}
            new = list(carry)
            ps, dss = {}, {}
            for n in live:
                p = _masked(jnp.exp(ss[n] * MLA_SCALE - lses[n]), pats[n])
                ps[n] = _bf(p)
                dss[n] = _bf(p * (dps[n] - deltas[n]) * MLA_SCALE)
                new[n] = carry[n] + _dot(dss[n], ks[chains[n][0]])
            dv_t, dk_t = None, []
            for h in range(2):
                mine = [n for n in live if chains[n][0] == h]
                first_row = chains[mine[0]][1] * ROW_GROUP
                ds_cat = jnp.concatenate([dss[n] for n in mine], axis=0)
                p_cat = jnp.concatenate([ps[n] for n in mine], axis=0)
                if first_row == 0:
                    q_rows_t, do_rows_t = q_t[h][...], do_t[h][...]
                else:
                    q_rows_t = q_ref[first_row:, h * LANES:(h + 1) * LANES].T
                    do_rows_t = do_heads[h][first_row:].T
                dk_t.append(_dot(q_rows_t, ds_cat))
                term = _dot(do_rows_t, p_cat)
                dv_t = term if dv_t is None else dv_t + term
            back = jnp.concatenate(dk_t + [dv_t], axis=0).T
            dk_ref[pl.ds(start, tk), :] += back[:, :2 * LANES]
            dv_ref[pl.ds(start, tk), :] += back[:, 2 * LANES:]
            return tuple(new)

        zero = jnp.zeros((ROW_GROUP, LANES), F32)
        carry = lax.fori_loop(0, i * (tq // tk), lambda j, c: block(pl.multiple_of(j * tk, tk), c, None, tk),
                              (zero,) * len(chains))
        for m in range(tq // tk_diag):
            carry = block(pl.multiple_of(i * tq + m * tk_diag, tk_diag), carry, m, tk_diag)
        for n, (h, g) in enumerate(chains):
            dq_ref[_rows(g), h * LANES:(h + 1) * LANES] = carry[n]

    two_t = pl.BlockSpec((tq, 2 * LANES), lambda p, i: (i, p))
    two_s = pl.BlockSpec((seq, 2 * LANES), lambda p, i: (0, p))
    pair_t = pl.BlockSpec((tq, LANES), lambda p, i: (i, p))
    pair_s = pl.BlockSpec((seq, LANES), lambda p, i: (0, p))
    return _call_with_rider(
        body, rider, name="mla_bwd", grid=(MLA_HEADS // 2, seq // tq),
        in_specs=[two_t, two_s, pair_s, pair_t, pair_t, pair_t],
        out_specs=[two_t, two_s, pair_s],
        out_shape=[jax.ShapeDtypeStruct((seq, 1024), F32), jax.ShapeDtypeStruct((seq, 1024), F32),
                   jax.ShapeDtypeStruct((seq, 512), F32)],
        args=(qp, kp, vp, o, lse, do), semantics=("parallel", "arbitrary"),
        scratch=[pltpu.VMEM((2, LANES, tq), BF16), pltpu.VMEM((2, LANES, tq), BF16)])


def _sb_attn_fwd(pbf, *, tq, tk):
    seq = pbf.shape[0]
    nd = tq // tk
    qb, kb, vb = PBF_POS[BLK_QB] * 4, PBF_POS[BLK_KB] * 4, PBF_POS[BLK_VB] * 4
    chains = _chains(tq)

    def body(q_ref, k_ref, v_ref, o_ref, tot_ref):
        i = pl.program_id(1)
        u_later = _tri_matrix(tk, lambda r, c: r > c)
        left = lax.broadcasted_iota(jnp.int32, (tq, LANES), 1) < HALF
        q_heads = _head_queries(q_ref[...], left)
        qs = [q_heads[h][_rows(g)] for h, g in chains]

        def block(j, carry, m):
            start = pl.multiple_of(j * tk, tk)
            k = k_ref[pl.ds(start, tk), :]
            v = v_ref[pl.ds(start, tk), :]
            pats = [True if m is None else _chain_pattern(g, m, tk, True) for _, g in chains]
            live = [n for n, p in enumerate(pats) if p is not None]
            zs = {n: _dot_nt(qs[n], k) for n in live}
            raws = {n: _softplus(zs[n]) for n in live}
            sps = {n: _masked(raws[n], pats[n]) for n in live}
            laters = {n: _tri_sum(sps[n], u_later) for n in live}
            new = list(carry)
            for n in live:
                c, acc = carry[n]
                a = _masked(jnp.exp(zs[n] - raws[n] - laters[n] - c), pats[n])
                new[n] = (c + laters[n][:, 0:1] + sps[n][:, 0:1], acc + _dot(_bf(a), v))
            return tuple(new)

        init = (jnp.zeros((ROW_GROUP, 1), F32), jnp.zeros((ROW_GROUP, LANES), F32))
        carry = (init,) * len(chains)
        for m in reversed(range(nd)):
            carry = block(i * nd + m, carry, m)
        per_trip = 4 if nd % 4 == 0 else 2

        def trip(jj, cr):
            for u in range(per_trip):
                cr = block(i * nd - 1 - (per_trip * jj + u), cr, None)
            return cr

        carry = lax.fori_loop(0, i * nd // per_trip, trip, carry)
        per_head = []
        for h in range(2):
            mine = [carry[n] for n, (ch, _) in enumerate(chains) if ch == h]
            per_head.append((jnp.concatenate([acc for _, acc in mine], axis=0), jnp.concatenate([c for c, _ in mine], axis=0)))
        o_ref[...] = jnp.where(left, per_head[0][0], per_head[1][0])
        tot_ref[...] = jnp.where(left, per_head[0][1], per_head[1][1])

    pair_t = pl.BlockSpec((tq, LANES), lambda p, i: (i, p))
    return pl.pallas_call(
        body, name="sb_fwd", grid=(SB_HEADS // 2, seq // tq),
        in_specs=[pl.BlockSpec((tq, LANES), lambda p, i: (i, qb + p)), pl.BlockSpec((seq, LANES), lambda p, i: (0, kb + p)),
                  pl.BlockSpec((seq, LANES), lambda p, i: (0, vb + p))],
        out_specs=[pair_t, pair_t],
        out_shape=[jax.ShapeDtypeStruct((seq, 512), F32), jax.ShapeDtypeStruct((seq, 512), F32)],
        compiler_params=_cparams(("parallel", "parallel")),
    )(pbf, pbf, pbf)


def _sb_attn_bwd(pbf, tot, do, *, tq, tk, rider=None):
    seq = pbf.shape[0]
    nd = tq // tk
    qb, kb, vb = PBF_POS[BLK_QB] * 4, PBF_POS[BLK_KB] * 4, PBF_POS[BLK_VB] * 4
    chains = _chains(tq)
    group = SB_BWD_CHAINS_IN_FLIGHT

    def body(q_ref, k_ref, v_ref, tot_ref, do_ref, dq_ref, dk_ref, dv_ref, qt_ref, dot_ref):
        i = pl.program_id(1)

        @pl.when(i == 0)
        def _():
            dk_ref[...] = jnp.zeros_like(dk_ref)
            dv_ref[...] = jnp.zeros_like(dv_ref)

        u_upto = _tri_matrix(tk, lambda r, c: r <= c)
        u_below = _tri_matrix(tk, lambda r, c: r < c)
        left = lax.broadcasted_iota(jnp.int32, (tq, LANES), 1) < HALF
        q_heads = _head_queries(q_ref[...], left)
        do_f = do_ref[...]
        do_heads = (_bf(jnp.where(left, do_f, 0.0)), _bf(jnp.where(left, 0.0, do_f)))
        tot_v = tot_ref[...]
        qs = [q_heads[h][_rows(g)] for h, g in chains]
        dos = [do_heads[h][_rows(g)] for h, g in chains]
        totals = [tot_v[_rows(g), h * HALF:h * HALF + 1] for h, g in chains]
        qt_ref[...] = jnp.concatenate(qs, axis=0).T
        dot_ref[...] = jnp.concatenate(dos, axis=0).T

        def block(j, carry, m):
            start = pl.multiple_of(j * tk, tk)
            k = k_ref[pl.ds(start, tk), :]
            v = v_ref[pl.ds(start, tk), :]
            pats = [True if m is None else _chain_pattern(g, m, tk, True) for _, g in chains]
            all_live = [n for n, p in enumerate(pats) if p is not None]
            new = list(carry)
            for g0 in range(0, len(all_live), group):
                live = all_live[g0:g0 + group]
                zs = {n: _dot_nt(qs[n], k) for n in live}
                das = {n: _dot_nt(dos[n], v) for n in live}
                raws = {n: _softplus(zs[n]) for n in live}
                sps = {n: _masked(raws[n], pats[n]) for n in live}
                uptos = {n: _tri_sum(sps[n], u_upto) for n in live}
                lbs, a_s, gs = {}, {}, {}
                for n in live:
                    lbs[n] = zs[n] - raws[n]
                    a = _masked(jnp.exp(lbs[n] - (totals[n] - carry[n][0] - uptos[n])), pats[n])
                    a_s[n] = _bf(a)
                    gs[n] = das[n] * a
                belows = {n: _dot(_bf(gs[n]), u_below) for n in live}
                dzs = {}
                for n in live:
                    sp_before, g_before, dq_acc = carry[n]
                    beta = jnp.exp(lbs[n])
                    dz = _masked(gs[n] * (1.0 - beta) - (g_before + belows[n]) * beta, pats[n])
                    dzs[n] = _bf(dz)
                    new[n] = (sp_before + uptos[n][:, tk - 1:tk], g_before + belows[n][:, tk - 1:tk] + gs[n][:, tk - 1:tk],
                              dq_acc + _dot(dzs[n], k))
                dz_cat = jnp.concatenate([dzs[n] for n in live], axis=0)
                a_cat = jnp.concatenate([a_s[n] for n in live], axis=0)
                if len(live) == len(chains):
                    q_rows_t, do_rows_t = qt_ref[...], dot_ref[...]
                else:
                    q_rows_t = jnp.concatenate([qs[n] for n in live], axis=0).T
                    do_rows_t = jnp.concatenate([dos[n] for n in live], axis=0).T
                both = jnp.concatenate([_dot(q_rows_t, dz_cat), _dot(do_rows_t, a_cat)], axis=0).T
                dk_ref[pl.ds(start, tk), :] += both[:, :LANES]
                dv_ref[pl.ds(start, tk), :] += both[:, LANES:]
            return tuple(new)

        zero = jnp.zeros((ROW_GROUP, 1), F32)
        init = (zero, zero, jnp.zeros((ROW_GROUP, LANES), F32))
        per_trip = 4 if nd % 4 == 0 else 2

        def trip(j, cr):
            for u in range(per_trip):
                cr = block(per_trip * j + u, cr, None)
            return cr

        carry = lax.fori_loop(0, i * nd // per_trip, trip, (init,) * len(chains))
        for m in range(nd):
            carry = block(i * nd + m, carry, m)
        per_head = [jnp.concatenate([carry[n][2] for n, (ch, _) in enumerate(chains) if ch == h], axis=0) for h in range(2)]
        dq_ref[...] = jnp.where(left, per_head[0], per_head[1]) * SB_SCALE

    pair_t = pl.BlockSpec((tq, LANES), lambda p, i: (i, p))
    pair_s = pl.BlockSpec((seq, LANES), lambda p, i: (0, p))
    return _call_with_rider(
        body, rider, name="sb_bwd", grid=(SB_HEADS // 2, seq // tq),
        in_specs=[pl.BlockSpec((tq, LANES), lambda p, i: (i, qb + p)), pl.BlockSpec((seq, LANES), lambda p, i: (0, kb + p)),
                  pl.BlockSpec((seq, LANES), lambda p, i: (0, vb + p)), pair_t, pair_t],
        out_specs=[pair_t, pair_s, pair_s],
        out_shape=[jax.ShapeDtypeStruct((seq, 512), F32)] * 3,
        args=(pbf, pbf, pbf, tot, do), semantics=("parallel", "arbitrary"),
        scratch=[pltpu.VMEM((LANES, 2 * tq), BF16), pltpu.VMEM((LANES, 2 * tq), BF16)])


def _mem_probs(s):
    e = jnp.exp(s - jnp.max(s, axis=-1, keepdims=True))
    return e / jnp.sum(e, axis=-1, keepdims=True)


def _head_lanes(h):
    return slice(h * LANES, (h + 1) * LANES)


def _mem_fwd(pbf, mkv, *, t):
    seq = pbf.shape[0]

    def body(q_ref, kv_ref, o_ref):
        ss = [_dot_nt(q_ref[:, _head_lanes(h)], kv_ref[:, _head_lanes(h)]) * MEM_SCALE for h in range(MEM_HEADS)]
        ps = [_bf(_mem_probs(s)) for s in ss]
        for h, p in enumerate(ps):
            o_ref[:, _head_lanes(h)] = _dot(p, kv_ref[:, _head_lanes(MEM_HEADS + h)])

    return pl.pallas_call(
        body, name="mem_fwd", grid=(seq // t,),
        in_specs=[pl.BlockSpec((t, 512), lambda i: (i, PBF_POS[BLK_QM])), pl.BlockSpec((MEM_LEN, 1024), lambda i: (0, 0))],
        out_specs=pl.BlockSpec((t, 512), lambda i: (i, 0)),
        out_shape=jax.ShapeDtypeStruct((seq, 512), F32),
        compiler_params=_cparams(("parallel",)),
    )(pbf, mkv)


def _mem_bwd(pbf, mkv, do, *, t):
    seq = pbf.shape[0]

    def body(q_ref, kv_ref, do_ref, dq_ref, dkv_ref):
        @pl.when(pl.program_id(0) == 0)
        def _():
            dkv_ref[...] = jnp.zeros_like(dkv_ref)

        heads = range(MEM_HEADS)
        qs = [q_ref[:, _head_lanes(h)] for h in heads]
        ks = [kv_ref[:, _head_lanes(h)] for h in heads]
        dos = [_bf(do_ref[:, _head_lanes(h)]) for h in heads]
        ss = [_dot_nt(qs[h], ks[h]) * MEM_SCALE for h in heads]
        dps = [_dot_nt(dos[h], kv_ref[:, _head_lanes(MEM_HEADS + h)]) for h in heads]
        ps = [_mem_probs(s) for s in ss]
        dss = [_bf(ps[h] * (dps[h] - jnp.sum(dps[h] * ps[h], axis=-1, keepdims=True)) * MEM_SCALE) for h in heads]
        for h in heads:
            dq_ref[:, _head_lanes(h)] = _dot(dss[h], ks[h])
        for h in heads:
            dkv_ref[:, _head_lanes(h)] += _dot_tn(dss[h], qs[h])
            dkv_ref[:, _head_lanes(MEM_HEADS + h)] += _dot_tn(_bf(ps[h]), dos[h])

    return pl.pallas_call(
        body, name="mem_bwd", grid=(seq // t,),
        in_specs=[pl.BlockSpec((t, 512), lambda i: (i, PBF_POS[BLK_QM])), pl.BlockSpec((MEM_LEN, 1024), lambda i: (0, 0)),
                  pl.BlockSpec((t, 512), lambda i: (i, 0))],
        out_specs=[pl.BlockSpec((t, 512), lambda i: (i, 0)), pl.BlockSpec((MEM_LEN, 1024), lambda i: (0, 0))],
        out_shape=[jax.ShapeDtypeStruct((seq, 512), F32), jax.ShapeDtypeStruct((MEM_LEN, 1024), F32)],
        compiler_params=_cparams(("arbitrary",)),
    )(pbf, mkv, do)


def _mid(x, tgt, o_a, o_b, o_m, p32, wmg, bmg, wba, wbb, wbm, wout, ln_g, ln_b, *, t):
    seq = x.shape[0]
    inv_d = 1.0 / D_MODEL

    def body(x_ref, t_ref, oa_ref, ob_ref, om_ref, ga_ref, gb_ref, gm_ref, wmg_ref, bmg_ref, wba_ref, wbb_ref,
             wbm_ref, wout_ref, lg_ref, lb_ref,
             du_ref, mrg_ref, dgp_ref, ha_ref, hb_ref, hm_ref, dya_ref, dyb_ref, dym_ref, doa_ref, dob_ref, dom_ref,
             dga_ref, dgb_ref, dgm_ref, dgain_ref, dbias_ref, dbmg_ref, loss_ref):
        @pl.when(pl.program_id(0) == 0)
        def _():
            dgain_ref[...] = jnp.zeros_like(dgain_ref)
            dbias_ref[...] = jnp.zeros_like(dbias_ref)
            dbmg_ref[...] = jnp.zeros_like(dbmg_ref)
            loss_ref[...] = jnp.zeros_like(loss_ref)

        xv = x_ref[...]
        gate = _sigmoid(_dot_cols(_bf(xv), wmg_ref) + bmg_ref[...])

        branches = []
        merged = None
        for b, (o_ref, g_ref, w_ref, h_ref) in enumerate(((oa_ref, ga_ref, wba_ref, ha_ref), (ob_ref, gb_ref, wbb_ref, hb_ref),
                                                         (om_ref, gm_ref, wbm_ref, hm_ref))):
            o, gt = o_ref[...], g_ref[...]
            sg = _sigmoid(gt)
            silu = gt * sg
            h = _bf(o * silu)
            h_ref[...] = h
            y = _dot_cols(h, w_ref)
            g_b = gate[:, b * D_MODEL:(b + 1) * D_MODEL]
            term = g_b * y
            merged = term if merged is None else merged + term
            branches.append((o, gt, sg, silu, y, g_b))
        mrg_b = _bf(merged)
        mrg_ref[...] = mrg_b

        u = DEEPNORM_ALPHA * xv + _dot(mrg_b, wout_ref[...])
        mu = jnp.mean(u, axis=-1, keepdims=True)
        uc = u - mu
        rstd = lax.rsqrt(jnp.mean(uc * uc, axis=-1, keepdims=True) + LN_EPS)
        xhat = uc * rstd
        lg = lg_ref[...]
        y_out = xhat * lg + lb_ref[...]
        err = y_out - t_ref[...]
        loss_ref[...] += 0.5 * jnp.sum(jnp.mean(err * err, axis=-1, keepdims=True), axis=0, keepdims=True)
        dy = err * inv_d
        dgain_ref[...] += jnp.sum(dy * xhat, axis=0, keepdims=True)
        dbias_ref[...] += jnp.sum(dy, axis=0, keepdims=True)
        dxh = dy * lg
        du = rstd * (dxh - jnp.mean(dxh, axis=-1, keepdims=True) - xhat * jnp.mean(dxh * xhat, axis=-1, keepdims=True))
        du_ref[...] = du

        dmerged = _dot_nt(_bf(du), wout_ref[...])
        outs = ((dya_ref, doa_ref, dga_ref, wba_ref), (dyb_ref, dob_ref, dgb_ref, wbb_ref), (dym_ref, dom_ref, dgm_ref, wbm_ref))
        dgp = []
        for (o, gt, sg, silu, y, g_b), (dy_ref, do_ref, dg_ref, w_ref) in zip(branches, outs):
            dyb = _bf(dmerged * g_b)
            dy_ref[...] = dyb
            dgp.append(dmerged * y * g_b * (1.0 - g_b))
            dh = _dot_nt_cols(dyb, w_ref)
            do_ref[...] = dh * silu
            dg_ref[...] = _bf(dh * o * (sg * (1.0 + gt * (1.0 - sg))))
        dgp = jnp.concatenate(dgp, axis=1)
        dgp_ref[...] = _bf(dgp)
        dbmg_ref[...] += jnp.sum(dgp, axis=0, keepdims=True)

    row = lambda w: pl.BlockSpec((t, w), lambda i: (i, 0))
    pblk = lambda c: pl.BlockSpec((t, 512), lambda i: (i, c))
    full = lambda shp: pl.BlockSpec(shp, lambda i: (0,) * len(shp))
    sds = jax.ShapeDtypeStruct
    return pl.pallas_call(
        body, name="mid", grid=(seq // t,),
        in_specs=[row(1024), row(1024), row(512), row(512), row(512), pblk(P32_POS[BLK_GATE_A]), pblk(P32_POS[BLK_GATE_B]), pblk(P32_POS[BLK_GATE_M]),
                  full(wmg.shape), full((1, N_MERGE)), full(wba.shape), full(wbb.shape), full(wbm.shape), full(wout.shape),
                  full((1, D_MODEL)), full((1, D_MODEL))],
        out_specs=[row(1024), row(1024), row(N_MERGE), row(512), row(512), row(512), row(1024), row(1024), row(1024),
                   row(512), row(512), row(512), row(512), row(512), row(512),
                   full((1, D_MODEL)), full((1, D_MODEL)), full((1, N_MERGE)), full((1, 1))],
        out_shape=[sds((seq, 1024), F32), sds((seq, 1024), BF16), sds((seq, N_MERGE), BF16),
                   sds((seq, 512), BF16), sds((seq, 512), BF16), sds((seq, 512), BF16),
                   sds((seq, 1024), BF16), sds((seq, 1024), BF16), sds((seq, 1024), BF16),
                   sds((seq, 512), F32), sds((seq, 512), F32), sds((seq, 512), F32),
                   sds((seq, 512), BF16), sds((seq, 512), BF16), sds((seq, 512), BF16),
                   sds((1, D_MODEL), F32), sds((1, D_MODEL), F32), sds((1, N_MERGE), F32), sds((1, 1), F32)],
        compiler_params=_cparams(("arbitrary",)),
    )(x, tgt, o_a, o_b, o_m, p32, p32, p32, wmg, bmg, wba, wbb, wbm, wout, ln_g, ln_b)


def _primed_weights(w):
    w_in = w["w_in"]
    zc = lambda n: jnp.zeros((D_MODEL, n), w_in.dtype)
    w_in_p = jnp.concatenate([w_in[:, 0:384], zc(64), w_in[:, 384:416], zc(32), w_in[:, 416:]], axis=1)
    wqb = jnp.pad(w["w_q_b"].reshape(MLA_Q_LORA, MLA_HEADS, 96), ((0, 0), (0, 0), (0, 32))).reshape(MLA_Q_LORA, 1024)
    kv3 = w["w_kv_b"].reshape(MLA_KV_LORA, MLA_HEADS, 128)
    wk = jnp.pad(kv3[:, :, :MLA_NOPE], ((0, 0), (0, 0), (0, 64))).reshape(MLA_KV_LORA, 1024)
    wv = kv3[:, :, MLA_NOPE:].reshape(MLA_KV_LORA, 512)
    return w_in_p, wqb, jnp.concatenate([wk, wv], axis=1)


PROJ_BLK = 512


def _proj_in(x, w_in_p, *, tm):
    seq = x.shape[0]

    def body(x_ref, w_ref, p32_ref, pbf_ref):
        xb = _bf(x_ref[...])
        for blk in range(IN_WIDTH_P // PROJ_BLK):
            r = _dot(xb, w_ref[:, blk * PROJ_BLK:(blk + 1) * PROJ_BLK])
            if blk in P32_POS:
                p32_ref[:, P32_POS[blk] * PROJ_BLK:(P32_POS[blk] + 1) * PROJ_BLK] = r
            else:
                pbf_ref[:, PBF_POS[blk] * PROJ_BLK:(PBF_POS[blk] + 1) * PROJ_BLK] = _bf(r)

    row = lambda w: pl.BlockSpec((tm, w), lambda i: (i, 0))
    return pl.pallas_call(
        body, name="proj_in", grid=(seq // tm,),
        in_specs=[row(D_MODEL), pl.BlockSpec(w_in_p.shape, lambda i: (0, 0))],
        out_specs=[row(len(P32_POS) * PROJ_BLK), row(len(PBF_POS) * PROJ_BLK)],
        out_shape=[jax.ShapeDtypeStruct((seq, len(P32_POS) * PROJ_BLK), F32),
                   jax.ShapeDtypeStruct((seq, len(PBF_POS) * PROJ_BLK), BF16)],
        compiler_params=_cparams(("parallel",)),
    )(x, w_in_p)


def _grad_x(du, dgpre, wmg, d_proj, w_in_p, *, tm, rider=None):
    seq = du.shape[0]
    n_pieces = len(d_proj)

    def body(du_ref, dg_ref, wmg_ref, *rest):
        piece_refs, win_ref, out_ref = rest[:n_pieces], rest[n_pieces], rest[n_pieces + 1]
        d_p = jnp.concatenate([_bf(p_ref[...]) for p_ref in piece_refs], axis=1)
        out_ref[...] = (DEEPNORM_ALPHA * du_ref[...] + _dot_nt_cols(dg_ref[...], wmg_ref)) + _dot_nt(d_p, win_ref[...])

    row = lambda w: pl.BlockSpec((tm, w), lambda i: (i, 0))
    whole = lambda a: pl.BlockSpec(a.shape, lambda i: (0,) * a.ndim)
    return _call_with_rider(
        body, rider, name="grad_x", grid=(seq // tm,),
        in_specs=[row(D_MODEL), row(N_MERGE), whole(wmg)] + [row(PROJ_BLK) for _ in d_proj] + [whole(w_in_p)],
        out_specs=[row(D_MODEL)], out_shape=[jax.ShapeDtypeStruct((seq, D_MODEL), F32)],
        args=(du, dgpre, wmg, *d_proj, w_in_p), semantics=("parallel",))


def _grad_w_in(x, d_proj, *, tk):
    seq = x.shape[0]
    n_pieces = len(d_proj)
    nk = seq // tk

    def body(x_ref, *rest):
        piece_refs, out_ref, acc = rest[:n_pieces], rest[n_pieces], rest[n_pieces + 1]
        j, kk = pl.program_id(0), pl.program_id(1)

        @pl.when(kk == 0)
        def _():
            acc[...] = jnp.zeros_like(acc)

        xb = _bf(x_ref[...])
        for pair in range(n_pieces // 2):
            @pl.when(j == pair)
            def _(pair=pair):
                both = jnp.concatenate([_bf(piece_refs[2 * pair][...]), _bf(piece_refs[2 * pair + 1][...])], axis=1)
                acc[...] += _dot_tn(xb, both)

        @pl.when(kk == nk - 1)
        def _():
            out_ref[...] = acc[...]

    def piece_spec(s):
        return pl.BlockSpec((tk, PROJ_BLK), lambda j, kk: (jnp.where(j == s // 2, kk, 0), 0))

    return pl.pallas_call(
        body, name="grad_w_in", grid=(n_pieces // 2, nk),
        in_specs=[pl.BlockSpec((tk, D_MODEL), lambda j, kk: (kk, 0))] + [piece_spec(s) for s in range(n_pieces)],
        out_specs=pl.BlockSpec((D_MODEL, 2 * PROJ_BLK), lambda j, kk: (0, j)),
        out_shape=jax.ShapeDtypeStruct((D_MODEL, n_pieces * PROJ_BLK), F32),
        scratch_shapes=[pltpu.VMEM((D_MODEL, 2 * PROJ_BLK), F32)],
        compiler_params=_cparams(("parallel", "arbitrary")),
    )(x, *d_proj)


EARLY_NAMES = ("w_mem_kv", "w_branch_mla", "w_branch_sb", "w_branch_mem", "w_merge_gate", "w_out")
LATE_NAMES = ("w_in", "w_q_b", "w_kv_b")


def _remote(src, dst, send_sem, recv_sem, device):
    return pltpu.make_async_remote_copy(src_ref=src, dst_ref=dst, send_sem=send_sem, recv_sem=recv_sem, device_id=device,
                                        device_id_type=MESH_ID)


def _gather_rider(shards):
    n = len(shards)

    def copies(src_refs, out_refs, sems):
        send_sems, recv_sems, local_sems = sems
        x, y, c = _place()
        me = 2 * x + y
        out = []
        for a, (s, o) in enumerate(zip(src_refs, out_refs)):
            out.append(pltpu.make_async_copy(s, o.at[me], local_sems.at[a]))
            for k, (px, py) in enumerate(_other_chips(x, y)):
                out.append(_Exchange(_remote(s, o.at[me], send_sems.at[k, a], recv_sems.at[k, a], (px, py, c)),
                                     _remote(s, o.at[2 * px + py], send_sems.at[k, a], recv_sems.at[k, a], (px, py, c))))
        return out

    return _Rider(shards, [jax.ShapeDtypeStruct((N_CHIPS,) + s.shape, s.dtype) for s in shards],
                  [pltpu.SemaphoreType.DMA((3, n)), pltpu.SemaphoreType.DMA((3, n)), pltpu.SemaphoreType.DMA((n,))], copies)


def _sibling_rider(g4):
    n = len(g4)

    def copies(g_refs, out_refs, sems):
        send_sems, recv_sems = sems
        x, y, c = _place()
        out = []
        for a, (g, o) in enumerate(zip(g_refs, out_refs)):
            half = g.shape[1] // 2
            theirs = pl.ds(pl.multiple_of((1 - c) * half, 8), half)
            cp = _remote(g.at[:, theirs, :], o, send_sems.at[a], recv_sems.at[a], (x, y, 1 - c))
            out.append(_Exchange(cp, cp))
        return out

    return _Rider(g4, [jax.ShapeDtypeStruct((N_CHIPS, g.shape[1] // 2, g.shape[2]), g.dtype) for g in g4],
                  [pltpu.SemaphoreType.DMA((n,)), pltpu.SemaphoreType.DMA((n,))], copies)


def _chips_rider(wire):
    n = len(wire)

    def copies(s_refs, out_refs, sems):
        send_sems, recv_sems = sems
        x, y, c = _place()
        out = []
        for a, (s, o) in enumerate(zip(s_refs, out_refs)):
            for k, (px, py) in enumerate(_other_chips(x, y)):
                cp = _remote(s.at[2 * px + py], o.at[RELATION_XOR[k] - 1], send_sems.at[k, a], recv_sems.at[k, a], (px, py, c))
                out.append(_Exchange(cp, cp))
        return out

    return _Rider(wire, [jax.ShapeDtypeStruct((3,) + s.shape[1:], s.dtype) for s in wire],
                  [pltpu.SemaphoreType.DMA((3, n)), pltpu.SemaphoreType.DMA((3, n))], copies)


def _local_step(x, mem, tgt, w, small, *, tq, tq_sb_bwd, tk, tk_mla, t_row, t_wg, rest_shards=None):
    seq = x.shape[0]
    on_mesh = rest_shards is not None
    w_in_p, wqb, wkvb = _primed_weights(w)
    tabs = _rope_tables(seq)

    p32, pbf = _proj_in(x, w_in_p, tm=256)
    qp, kp, vp = _mla_prep(p32, small["q_a_gain"], small["kv_a_gain"], wqb, wkvb, tabs, t=t_row)
    res = _mla_attn_fwd(qp, kp, vp, tq=tq, tk=tk_mla, tk_diag=tk, rider=_gather_rider(rest_shards) if on_mesh else None)
    o_a, lse = res[0], res[1]
    if on_mesh:
        w = dict(w, **{n: g if n in COL_SHARDED else _join_chips(n, g) for n, g in zip(EARLY_NAMES, res[2:])})
    else:
        w = dict(w, **{n: _split_by_chip(n, w[n]) for n in EARLY_NAMES if n in COL_SHARDED})
    wmg, wout = w["w_merge_gate"], w["w_out"]
    wba, wbb, wbm = w["w_branch_mla"], w["w_branch_sb"], w["w_branch_mem"]
    o_b, keep_total = _sb_attn_fwd(pbf, tq=tq, tk=tk)
    (mkv,) = _matmul(mem, w["w_mem_kv"], mode="nn", tm=MEM_LEN, tn=512, tk=D_MODEL, out_dtypes=(BF16,), name="mem_kv")
    o_m = _mem_fwd(pbf, mkv, t=t_row)

    (du, merged, dgpre, h_a, h_b, h_m, dy_a, dy_b, dy_m, do_a, do_b, do_m, dgate_a, dgate_b, dgate_m,
     d_ln_g, d_ln_b, d_bmg, loss) = _mid(x, tgt, o_a, o_b, o_m, p32, wmg, small["b_merge_gate"], wba, wbb, wbm, wout,
                                         small["ln_gain"], small["ln_bias"], t=t_row)

    wg = functools.partial(_matmul, mode="tn", tm=512, out_dtypes=(F32,))
    shard = (lambda width: dict(tn=width // N_CHIPS, by_column_block=True)) if on_mesh else (lambda width: dict(tn=1024))
    dq_m, dmkv = _mem_bwd(pbf, mkv, do_m, t=t_row)
    early = {"w_mem_kv": wg(mem, dmkv, tk=MEM_LEN, tn=1024, name="grad_w_mem_kv")[0],
             "w_branch_mla": wg(h_a, dy_a, tk=t_wg, name="grad_w_branch_a", **shard(D_MODEL))[0],
             "w_branch_sb": wg(h_b, dy_b, tk=t_wg, name="grad_w_branch_b", **shard(D_MODEL))[0],
             "w_branch_mem": wg(h_m, dy_m, tk=t_wg, name="grad_w_branch_m", **shard(D_MODEL))[0],
             "w_merge_gate": wg(x, dgpre, tk=t_wg, name="grad_w_merge_gate", **shard(N_MERGE))[0],
             "w_out": wg(merged, du, tk=t_wg, tn=1024, name="grad_w_out")[0]}

    if on_mesh:
        g4 = [early[n] if early[n].ndim == 3 else _split_by_chip(n, early[n]) for n in EARLY_NAMES]
        res = _mla_attn_bwd(qp, kp, vp, o_a, lse, do_a, tq=tq, tk=tk_mla, tk_diag=tk, rider=_sibling_rider(g4))
        (dqp, dkp, dvp), got = res[:3], res[3:]
        chipsum, wire = _rs_add_sibling(g4, got, [BF16] * len(g4))
        res = _sb_attn_bwd(pbf, keep_total, do_b, tq=tq_sb_bwd, tk=tk, rider=_chips_rider(wire))
        (dq_b, dk_b, dv_b), parts = res[:3], res[3:]
        early = _rs_add_chips(chipsum, parts)
    else:
        dqp, dkp, dvp = _mla_attn_bwd(qp, kp, vp, o_a, lse, do_a, tq=tq, tk=tk_mla, tk_diag=tk)
        dq_b, dk_b, dv_b = _sb_attn_bwd(pbf, keep_total, do_b, tq=tq_sb_bwd, tk=tk)
    dlat, d_wqb, d_wkvb, d_gq, d_gkv = _mla_post(p32, dqp, dkp, dvp, small["q_a_gain"], small["kv_a_gain"], wqb, wkvb, tabs,
                                                 t=t_row)

    d_proj = [dlat, dgate_a, dq_b, dk_b, dv_b, dgate_b, dq_m, dgate_m]
    d_winp = _grad_w_in(x, d_proj, tk=min(1024, seq))

    d_win = jnp.concatenate([d_winp[:, 0:384], d_winp[:, 448:480], d_winp[:, 512:]], axis=1)
    d_wq = d_wqb.reshape(MLA_Q_LORA, MLA_HEADS, 128)[:, :, :96].reshape(MLA_Q_LORA, 768)
    d_wk = d_wkvb[:, :1024].reshape(MLA_KV_LORA, MLA_HEADS, 128)[:, :, :MLA_NOPE]
    d_wv = d_wkvb[:, 1024:].reshape(MLA_KV_LORA, MLA_HEADS, MLA_V)
    d_wkv = jnp.concatenate([d_wk, d_wv], axis=2).reshape(MLA_KV_LORA, 1024)
    late = {"w_in": d_win, "w_q_b": d_wq, "w_kv_b": d_wkv}
    small_grads = {"q_a_gain": d_gq, "kv_a_gain": d_gkv, "b_merge_gate": d_bmg, "ln_gain": d_ln_g, "ln_bias": d_ln_b}
    if not on_mesh:
        (grad_x,) = _grad_x(du, dgpre, wmg, d_proj, w_in_p, tm=256)
        return loss[0, 0], grad_x, late, small_grads, early

    g4 = [_split_by_chip(n, late[n]) for n in LATE_NAMES]
    g4.append(jnp.broadcast_to(_pack_small(small_grads, scalar=loss)[None], (N_CHIPS, SMALL_ROWS, PACK_COLS)))
    got = _rs_to_sibling(g4)
    chipsum, wire = _rs_add_sibling(g4, got, [BF16] * len(LATE_NAMES) + [F32])
    res = _grad_x(du, dgpre, wmg, d_proj, w_in_p, tm=256, rider=_chips_rider(wire))
    late_mine = _rs_add_chips(chipsum, res[1:])
    return loss[0, 0], res[0], late_mine, None, early


def _place():
    x, y, c = lax.axis_index("x"), lax.axis_index("y"), lax.axis_index("c")
    return x, y, c


def _other_chips(x, y):
    return ((1 - x, y), (x, 1 - y), (1 - x, 1 - y))


SMALL_ROWS = 64
ADAM_STEPS_PER_HALF = 4


def _pack_small(d, scalar=None):
    parts = [d[n].reshape(-1) for n, _ in SMALL_SIZES] + ([] if scalar is None else [scalar.reshape(1)])
    flat = jnp.concatenate(parts)
    return jnp.pad(flat, (0, SMALL_ROWS * PACK_COLS - flat.shape[0])).reshape(SMALL_ROWS, PACK_COLS)


def _unpack_small(a):
    flat, out, c0 = a.reshape(-1), {}, 0
    for n, size in SMALL_SIZES:
        out[n] = flat[c0:c0 + size].reshape(1, size)
        c0 += size
    return out


def _split_by_chip(name, full):
    r, c = full.shape
    if name in COL_SHARDED:
        return full.reshape(r, N_CHIPS, c // N_CHIPS).transpose(1, 0, 2)
    return full.reshape(N_CHIPS, r // N_CHIPS, c)


def _join_chips(name, slots):
    _, r, cs = slots.shape
    if name in COL_SHARDED:
        return slots.transpose(1, 0, 2).reshape(r, N_CHIPS * cs)
    return slots.reshape(N_CHIPS * r, cs)


HBM_SPEC = pl.BlockSpec(memory_space=pltpu.HBM)


def _gather_shards(shards):
    n = len(shards)

    def body(*refs):
        w_refs, out_refs, wb_refs = refs[:n], refs[n:2 * n], refs[2 * n:3 * n]
        send_sems, recv_sems, pass_send_sems, pass_recv_sems, local_sems = refs[3 * n:]
        x, y, c = _place()
        me = 2 * x + y
        sibling = (x, y, 1 - c)

        def halves(ref):
            half = ref.shape[-2] // 2
            return (pl.ds(pl.multiple_of(c * half, 16), half), pl.ds(pl.multiple_of((1 - c) * half, 16), half))
        for w_ref, wb_ref in zip(w_refs, wb_refs):
            rows = w_ref.shape[0]
            chunk = min(rows, 128)

            def cast(i, carry, w_ref=w_ref, wb_ref=wb_ref, chunk=chunk):
                r0 = pl.multiple_of(i * chunk, chunk)
                wb_ref[pl.ds(r0, chunk), :] = _bf(w_ref[pl.ds(r0, chunk), :])
                return carry

            lax.fori_loop(0, rows // chunk, cast, 0)
        sends, locals_ = [], []
        for a, (wb_ref, out_ref) in enumerate(zip(wb_refs, out_refs)):
            mine = pltpu.make_async_copy(wb_ref, out_ref.at[me], local_sems.at[a])
            mine.start()
            locals_.append(mine)
            mine_rows, _ = halves(wb_ref)
            for k, (px, py) in enumerate(_other_chips(x, y)):
                cp = pltpu.make_async_remote_copy(src_ref=wb_ref.at[mine_rows, :], dst_ref=out_ref.at[me, mine_rows, :],
                                                  send_sem=send_sems.at[k, a], recv_sem=recv_sems.at[k, a],
                                                  device_id=(px, py, c), device_id_type=MESH_ID)
                cp.start()
                sends.append(cp)
        for a, (wb_ref, out_ref) in enumerate(zip(wb_refs, out_refs)):
            mine_rows, _ = halves(wb_ref)
            for k, (px, py) in enumerate(_other_chips(x, y)):
                landed = out_ref.at[2 * px + py, mine_rows, :]
                pltpu.make_async_remote_copy(src_ref=wb_ref.at[mine_rows, :], dst_ref=landed, send_sem=send_sems.at[k, a],
                                             recv_sem=recv_sems.at[k, a], device_id=(px, py, c),
                                             device_id_type=MESH_ID).wait_recv()
                cp = pltpu.make_async_remote_copy(src_ref=landed, dst_ref=landed, send_sem=pass_send_sems.at[k, a],
                                                  recv_sem=pass_recv_sems.at[k, a], device_id=sibling, device_id_type=MESH_ID)
                cp.start()
                sends.append(cp)
        for a, (wb_ref, out_ref) in enumerate(zip(wb_refs, out_refs)):
            _, their_rows = halves(wb_ref)
            for k, (px, py) in enumerate(_other_chips(x, y)):
                passed = out_ref.at[2 * px + py, their_rows, :]
                pltpu.make_async_remote_copy(src_ref=passed, dst_ref=passed, send_sem=pass_send_sems.at[k, a],
                                             recv_sem=pass_recv_sems.at[k, a], device_id=sibling,
                                             device_id_type=MESH_ID).wait_recv()
        for cp in sends:
            cp.wait_send()
        for cp in locals_:
            cp.wait()

    return pl.pallas_call(
        body, name="gather_weights",
        in_specs=[pl.BlockSpec(memory_space=pltpu.VMEM)] * n,
        out_specs=[HBM_SPEC] * n,
        out_shape=[jax.ShapeDtypeStruct((N_CHIPS,) + s.shape, BF16) for s in shards],
        scratch_shapes=[pltpu.VMEM(s.shape, BF16) for s in shards]
        + [pltpu.SemaphoreType.DMA((3, n))] * 4 + [pltpu.SemaphoreType.DMA((n,))],
        compiler_params=pltpu.CompilerParams(vmem_limit_bytes=VMEM_LIMIT),
    )(*shards)


def _cast_bf16_list(arrays):
    def body(*refs):
        for a_ref, o_ref in zip(refs[:len(arrays)], refs[len(arrays):]):
            o_ref[...] = _bf(a_ref[...])

    specs = [pl.BlockSpec((a.shape[0] // 4, a.shape[1]), lambda i: (i, 0)) for a in arrays]
    return pl.pallas_call(
        body, name="cast_shards", grid=(4,), in_specs=specs, out_specs=specs,
        out_shape=[jax.ShapeDtypeStruct(a.shape, BF16) for a in arrays],
        compiler_params=_cparams(("parallel",)),
    )(*arrays)


def _rs_to_sibling(g4):
    n = len(g4)

    def body(*refs):
        g_refs, out_refs = refs[:n], refs[n:2 * n]
        send_sems, recv_sems = refs[2 * n:]
        x, y, c = _place()
        copies = []
        for a, (g_ref, out_ref) in enumerate(zip(g_refs, out_refs)):
            half = g_ref.shape[1] // 2
            theirs = pl.ds(pl.multiple_of((1 - c) * half, 8), half)
            copies.append(pltpu.make_async_remote_copy(src_ref=g_ref.at[:, theirs, :], dst_ref=out_ref, send_sem=send_sems.at[a],
                                                       recv_sem=recv_sems.at[a], device_id=(x, y, 1 - c),
                                                       device_id_type=MESH_ID))
        for cp in copies:
            cp.start()
        for cp in copies:
            cp.wait()

    return pl.pallas_call(
        body, name="rs_sibling", in_specs=[HBM_SPEC] * n, out_specs=[HBM_SPEC] * n,
        out_shape=[jax.ShapeDtypeStruct((N_CHIPS, g.shape[1] // 2, g.shape[2]), F32) for g in g4],
        scratch_shapes=[pltpu.SemaphoreType.DMA((n,)), pltpu.SemaphoreType.DMA((n,))],
    )(*g4)


def _rs_add_sibling(g4, got, wire_dtypes):
    n = len(g4)
    narrow = [a for a in range(n) if wire_dtypes[a] != F32]

    def body(c_ref, *refs):
        outs = refs[2 * n:3 * n]
        wires = dict(zip(narrow, refs[3 * n:]))
        for a, (g_ref, r_ref, o_ref) in enumerate(zip(refs[:n], refs[n:2 * n], outs)):
            s = g_ref[...] + r_ref[...]
            o_ref[...] = s
            if a in wires:
                wires[a][...] = s.astype(wires[a].dtype)

    blk = lambda r: (1, r.shape[1], r.shape[2])
    plain = lambda r: pl.BlockSpec(blk(r), lambda j, c_ref: (j, 0, 0))
    grid_spec = pltpu.PrefetchScalarGridSpec(
        num_scalar_prefetch=1, grid=(N_CHIPS,),
        in_specs=[pl.BlockSpec(blk(r), lambda j, c_ref: (j, c_ref[0], 0)) for r in got] + [plain(r) for r in got],
        out_specs=[plain(r) for r in got] + [plain(got[a]) for a in narrow])
    res = pl.pallas_call(
        body, name="rs_add_sibling", grid_spec=grid_spec,
        out_shape=[jax.ShapeDtypeStruct(r.shape, F32) for r in got]
        + [jax.ShapeDtypeStruct(got[a].shape, wire_dtypes[a]) for a in narrow],
        compiler_params=_cparams(("parallel",)),
    )(lax.axis_index("c").astype(jnp.int32).reshape(1), *g4, *got)
    chipsum = list(res[:n])
    wire = list(chipsum)
    for a, w in zip(narrow, res[n:]):
        wire[a] = w
    return chipsum, wire


RELATION_XOR = (2, 1, 3)


def _rs_add_chips(chipsum, parts):
    n = len(parts)

    def body(me_ref, *refs):
        me = me_ref[0]
        for s_ref, p_ref, o_ref in zip(refs[:n], refs[n:2 * n], refs[2 * n:]):
            own = s_ref[0]
            total = None
            for k in range(N_CHIPS):
                theirs = p_ref[jnp.maximum(jnp.bitwise_xor(me, k) - 1, 0)].astype(F32)
                term = jnp.where(me == k, own, theirs)
                total = term if total is None else total + term
            o_ref[...] = total

    grid_spec = pltpu.PrefetchScalarGridSpec(
        num_scalar_prefetch=1, grid=(2,),
        in_specs=[pl.BlockSpec((1, p.shape[1] // 2, p.shape[2]), lambda i, me_ref: (me_ref[0], i, 0)) for p in parts]
        + [pl.BlockSpec((3, p.shape[1] // 2, p.shape[2]), lambda i, me_ref: (0, i, 0)) for p in parts],
        out_specs=[pl.BlockSpec((p.shape[1] // 2, p.shape[2]), lambda i, me_ref: (i, 0)) for p in parts])
    me = (2 * lax.axis_index("x") + lax.axis_index("y")).astype(jnp.int32).reshape(1)
    return pl.pallas_call(
        body, name="rs_add_chips", grid_spec=grid_spec,
        out_shape=[jax.ShapeDtypeStruct(p.shape[1:], F32) for p in parts],
        compiler_params=_cparams(("parallel",)),
    )(me, *chipsum, *parts)


def _rs_swap_halves(halves):
    n = len(halves)

    def body(*refs):
        h_refs, out_refs = refs[:n], refs[n:2 * n]
        send_sems, recv_sems = refs[2 * n:]
        x, y, c = _place()
        copies = [pltpu.make_async_remote_copy(src_ref=h_ref, dst_ref=out_ref, send_sem=send_sems.at[a], recv_sem=recv_sems.at[a],
                                               device_id=(x, y, 1 - c), device_id_type=MESH_ID)
                  for a, (h_ref, out_ref) in enumerate(zip(h_refs, out_refs))]
        for cp in copies:
            cp.start()
        for cp in copies:
            cp.wait()

    return pl.pallas_call(
        body, name="rs_swap_halves", in_specs=[HBM_SPEC] * n, out_specs=[HBM_SPEC] * n,
        out_shape=[jax.ShapeDtypeStruct(h.shape, F32) for h in halves],
        scratch_shapes=[pltpu.SemaphoreType.DMA((n,)), pltpu.SemaphoreType.DMA((n,))],
    )(*halves)


def _adamw_list(ws, g_mine, g_theirs, ms, vs):
    n = len(ws)

    def body(c_ref, *refs):
        w_refs, gm_refs, gt_refs, m_refs, v_refs = (refs[k * n:(k + 1) * n] for k in range(5))
        g_refs, d_refs, nm_refs, nv_refs = (refs[k * n:(k + 1) * n] for k in range(5, 9))
        mine = (pl.program_id(0) // ADAM_STEPS_PER_HALF) == c_ref[0]
        for a in range(n):
            gv = jnp.where(mine, gm_refs[a][...], gt_refs[a][...])
            g_refs[a][...] = gv
            m_new = ADAM_B1 * m_refs[a][...] + (1.0 - ADAM_B1) * gv
            v_new = ADAM_B2 * v_refs[a][...] + (1.0 - ADAM_B2) * (gv * gv)
            m_hat = m_new / (1.0 - ADAM_B1 ** ADAM_STEP)
            v_hat = v_new / (1.0 - ADAM_B2 ** ADAM_STEP)
            d_refs[a][...] = -ADAM_LR * (m_hat / (jnp.sqrt(v_hat) + ADAM_EPS) + ADAM_WD * w_refs[a][...])
            nm_refs[a][...] = m_new
            nv_refs[a][...] = v_new

    steps = 2 * ADAM_STEPS_PER_HALF
    whole = [pl.BlockSpec((w.shape[0] // steps, w.shape[1]), lambda i, c_ref: (i, 0)) for w in ws]
    half = [pl.BlockSpec((w.shape[0] // steps, w.shape[1]), lambda i, c_ref: (i % ADAM_STEPS_PER_HALF, 0)) for w in ws]
    shapes = [jax.ShapeDtypeStruct(w.shape, F32) for w in ws]
    grid_spec = pltpu.PrefetchScalarGridSpec(num_scalar_prefetch=1, grid=(steps,),
                                             in_specs=whole + half + half + whole + whole, out_specs=whole * 4)
    res = pl.pallas_call(
        body, name="adamw", grid_spec=grid_spec, out_shape=shapes * 4,
        compiler_params=_cparams(("parallel",)),
    )(lax.axis_index("c").astype(jnp.int32).reshape(1), *ws, *g_mine, *g_theirs, *ms, *vs)
    return res[:n], res[n:2 * n], res[2 * n:3 * n], res[3 * n:]


WEIGHT_NAMES = ("w_in", "w_mem_kv", "q_a_gain", "w_q_b", "kv_a_gain", "w_kv_b", "w_branch_mla", "w_branch_sb",
                "w_branch_mem", "w_merge_gate", "b_merge_gate", "w_out", "ln_gain", "ln_bias")
SMALL_NAMES = tuple(n for n, _ in SMALL_SIZES)


def kernel(x, mem, w_in, w_mem_kv, q_a_gain, w_q_b, kv_a_gain, w_kv_b, w_branch_mla, w_branch_sb, w_branch_mem, w_merge_gate, b_merge_gate, w_out, ln_gain, ln_bias, loss_target, m_w_in, m_w_mem_kv, m_q_a_gain, m_w_q_b, m_kv_a_gain, m_w_kv_b, m_w_branch_mla, m_w_branch_sb, m_w_branch_mem, m_w_merge_gate, m_b_merge_gate, m_w_out, m_ln_gain, m_ln_bias, v_w_in, v_w_mem_kv, v_q_a_gain, v_w_q_b, v_kv_a_gain, v_w_kv_b, v_w_branch_mla, v_w_branch_sb, v_w_branch_mem, v_w_merge_gate, v_b_merge_gate, v_w_out, v_ln_gain, v_ln_bias):
    weights = dict(zip(WEIGHT_NAMES, (w_in, w_mem_kv, q_a_gain, w_q_b, kv_a_gain, w_kv_b, w_branch_mla, w_branch_sb,
                                      w_branch_mem, w_merge_gate, b_merge_gate, w_out, ln_gain, ln_bias)))
    mom1 = dict(zip(WEIGHT_NAMES, (m_w_in, m_w_mem_kv, m_q_a_gain, m_w_q_b, m_kv_a_gain, m_w_kv_b, m_w_branch_mla,
                                   m_w_branch_sb, m_w_branch_mem, m_w_merge_gate, m_b_merge_gate, m_w_out, m_ln_gain,
                                   m_ln_bias)))
    mom2 = dict(zip(WEIGHT_NAMES, (v_w_in, v_w_mem_kv, v_q_a_gain, v_w_q_b, v_kv_a_gain, v_w_kv_b, v_w_branch_mla,
                                   v_w_branch_sb, v_w_branch_mem, v_w_merge_gate, v_b_merge_gate, v_w_out, v_ln_gain,
                                   v_ln_bias)))
    def as_list(d):
        return [d[n][0] for n in BIG_NAMES] + [_pack_small({n: d[n] for n in SMALL_NAMES})]

    w_list, m_list, v_list = as_list(weights), as_list(mom1), as_list(mom2)

    gathered = _gather_shards([weights[n][0] for n in LATE_NAMES])
    first_w = {n: _join_chips(n, g) for n, g in zip(LATE_NAMES, gathered)}
    rest_shards = _cast_bf16_list([weights[n][0] for n in EARLY_NAMES])
    small = {n: weights[n] for n in SMALL_NAMES}

    seq = x.shape[1]
    _, grad_x, late_mine, _, early_mine = _local_step(
        x[0], mem[0], loss_target[0], first_w, small, tq=min(1024, seq), tq_sb_bwd=min(1024, seq), tk=256, tk_mla=512,
        t_row=256, t_wg=min(2048, seq), rest_shards=rest_shards)
    by_name = dict(zip(EARLY_NAMES + LATE_NAMES + ("small",), list(early_mine) + list(late_mine)))
    mine = [by_name[n] for n in BIG_NAMES + ("small",)]
    theirs = _rs_swap_halves(mine)
    g_list, d_list, nm_list, nv_list = _adamw_list(w_list, mine, theirs, m_list, v_list)

    loss = g_list[-1].reshape(-1)[SMALL_TOTAL]
    outs = [loss, grad_x[None]]
    for arrays in (g_list, d_list, nm_list, nv_list):
        big = dict(zip(BIG_NAMES, arrays[:-1]))
        sm = _unpack_small(arrays[-1])
        outs.extend(big[n][None] if n in big else sm[n] for n in WEIGHT_NAMES)
    return tuple(outs)
```

```python
import functools
import math

import numpy as np
import jax
import jax.numpy as jnp
from jax import lax
from jax.experimental import pallas as pl
from jax.experimental.pallas import tpu as pltpu

F32 = jnp.float32
BF16 = jnp.bfloat16
MESH_ID = pl.DeviceIdType.MESH

D_MODEL = 1024
MEM_LEN = 256
MLA_HEADS = 8
MLA_NOPE = 64
MLA_ROPE = 32
MLA_V = 64
MLA_Q_LORA = 256
MLA_KV_LORA = 128
SB_HEADS = 8
SB_HEAD_DIM = 64
MEM_HEADS = 4
MEM_HEAD_DIM = 128
ROPE_BASE = 10000.0
RMS_EPS = 1e-6
LN_EPS = 1e-5
DEEPNORM_ALPHA = 2.0 ** 0.25
MLA_SCALE = 1.0 / math.sqrt(MLA_NOPE + MLA_ROPE)
SB_SCALE = 1.0 / math.sqrt(SB_HEAD_DIM)
MEM_SCALE = 1.0 / math.sqrt(MEM_HEAD_DIM)

ADAM_LR = 0.001
ADAM_B1 = 0.9
ADAM_B2 = 0.999
ADAM_EPS = 1e-08
ADAM_WD = 0.01
ADAM_STEP = 10

LANES = 128
HALF = 64
N_CHIPS = 4
PACK_COLS = 1024
VMEM_LIMIT = 56 * 1024 * 1024

IN_WIDTH_P = 4096
BLK_LAT, BLK_GATE_A, BLK_QB, BLK_KB, BLK_VB, BLK_GATE_B, BLK_QM, BLK_GATE_M = range(8)
P32_POS = {blk: n for n, blk in enumerate((BLK_LAT, BLK_GATE_A, BLK_GATE_B, BLK_GATE_M))}
PBF_POS = {blk: n for n, blk in enumerate((BLK_QB, BLK_KB, BLK_VB, BLK_QM))}
N_MERGE = 3 * D_MODEL

BIG_NAMES = ("w_in", "w_mem_kv", "w_q_b", "w_kv_b", "w_branch_mla", "w_branch_sb", "w_branch_mem", "w_merge_gate", "w_out")
COL_SHARDED = ("w_in", "w_q_b", "w_kv_b", "w_branch_mla", "w_branch_sb", "w_branch_mem", "w_merge_gate")
SMALL_SIZES = (("q_a_gain", 256), ("kv_a_gain", 128), ("b_merge_gate", 3072), ("ln_gain", 1024), ("ln_bias", 1024))
SMALL_TOTAL = sum(s for _, s in SMALL_SIZES)


def _cparams(sem=None):
    return pltpu.CompilerParams(dimension_semantics=sem, vmem_limit_bytes=VMEM_LIMIT)


def _dot(a, b):
    return jnp.dot(a, b, preferred_element_type=F32)


def _dot_nt(a, b):
    return lax.dot_general(a, b, (((1,), (1,)), ((), ())), preferred_element_type=F32)


def _dot_tn(a, b):
    return lax.dot_general(a, b, (((0,), (0,)), ((), ())), preferred_element_type=F32)


def _bf(x):
    return x.astype(BF16)


def _dot_cols(a, w_ref):
    return jnp.concatenate([_dot(a, w_ref[j]) for j in range(w_ref.shape[0])], axis=1)


def _dot_nt_cols(a, w_ref):
    cs = w_ref.shape[2]
    out = None
    for j in range(w_ref.shape[0]):
        term = _dot_nt(a[:, j * cs:(j + 1) * cs], w_ref[j])
        out = term if out is None else out + term
    return out


def _sigmoid(x):
    return 1.0 / (1.0 + jnp.exp(-x))


def _matmul(a, b, *, mode, tm, tn, tk, out_dtypes, name, add=None, add_scale=1.0, by_column_block=False):
    if mode == "nn":
        (m, k), n = a.shape, b.shape[1]
        a_spec = pl.BlockSpec((tm, tk), lambda i, j, kk: (i, kk))
        b_spec = pl.BlockSpec((tk, tn), lambda i, j, kk: (kk, j))
        dot = _dot
    elif mode == "nt":
        (m, k), n = a.shape, b.shape[0]
        a_spec = pl.BlockSpec((tm, tk), lambda i, j, kk: (i, kk))
        b_spec = pl.BlockSpec((tn, tk), lambda i, j, kk: (j, kk))
        dot = _dot_nt
    else:
        (k, m), n = a.shape, b.shape[1]
        a_spec = pl.BlockSpec((tk, tm), lambda i, j, kk: (kk, i))
        b_spec = pl.BlockSpec((tk, tn), lambda i, j, kk: (kk, j))
        dot = _dot_tn
    assert m % tm == 0 and n % tn == 0 and k % tk == 0, (name, m, n, k)
    nk = k // tk
    n_out = len(out_dtypes)
    has_add = add is not None

    def body(*refs):
        a_ref, b_ref = refs[0], refs[1]
        add_ref = refs[2] if has_add else None
        outs = refs[2 + has_add: 2 + has_add + n_out]
        acc = refs[-1]
        kk = pl.program_id(2)

        @pl.when(kk == 0)
        def _():
            acc[...] = jnp.zeros_like(acc)

        acc[...] += dot(_bf(a_ref[...]), _bf(b_ref[...]))

        @pl.when(kk == nk - 1)
        def _():
            r = acc[...]
            if has_add:
                r = r + add_scale * add_ref[...]
            for o in outs:
                o[...] = r.astype(o.dtype)

    in_specs = [a_spec, b_spec]
    args = [a, b]
    if has_add:
        in_specs.append(pl.BlockSpec((tm, tn), lambda i, j, kk: (i, j)))
        args.append(add)
    if by_column_block:
        out_spec = pl.BlockSpec((None, tm, tn), lambda i, j, kk: (j, i, 0))
        out_dims = (n // tn, m, tn)
    else:
        out_spec = pl.BlockSpec((tm, tn), lambda i, j, kk: (i, j))
        out_dims = (m, n)
    res = pl.pallas_call(
        body, name=name, grid=(m // tm, n // tn, nk),
        in_specs=in_specs, out_specs=[out_spec] * n_out,
        out_shape=[jax.ShapeDtypeStruct(out_dims, dt) for dt in out_dtypes],
        scratch_shapes=[pltpu.VMEM((tm, tn), F32)],
        compiler_params=_cparams(("parallel", "parallel", "arbitrary")),
    )(*args)
    return res


def _rope_tables(seq):
    half = MLA_ROPE // 2
    freqs = ROPE_BASE ** (-jnp.arange(half, dtype=F32) / half)
    ang = jnp.arange(seq, dtype=jnp.int32).astype(F32)[:, None] * freqs[None, :]
    cos, sin = jnp.cos(ang), jnp.sin(ang)
    z = lambda w: jnp.zeros((seq, w), F32)
    c_q = jnp.concatenate([jnp.ones((seq, MLA_NOPE), F32), cos, cos, z(32)], axis=1)
    c_k = jnp.concatenate([z(MLA_NOPE), cos, cos, z(32)], axis=1)
    s_lo = jnp.concatenate([z(MLA_NOPE), -sin, z(half), z(32)], axis=1)
    s_hi = jnp.concatenate([z(MLA_NOPE), z(half), sin, z(32)], axis=1)
    return c_q, c_k, s_lo, s_hi


def _rope_fwd(x, c, s_lo, s_hi):
    return x * c + pltpu.roll(x, LANES - 16, 1) * s_lo + pltpu.roll(x, 16, 1) * s_hi


def _rope_bwd(d, c, s_lo, s_hi):
    return d * c - pltpu.roll(d, 16, 1) * s_hi - pltpu.roll(d, LANES - 16, 1) * s_lo


def _rms_fwd(x, g):
    r = lax.rsqrt(jnp.mean(x * x, axis=-1, keepdims=True) + RMS_EPS)
    xn = x * r
    return xn * g, xn, r


def _mla_prep(p32, gq, gkv, wqb, wkvb, tabs, *, t):
    seq = p32.shape[0]

    def body(lat_ref, gq_ref, gkv_ref, wqb_ref, wkvb_ref, cq_ref, ck_ref, slo_ref, shi_ref, q_ref, k_ref, v_ref):
        lat = lat_ref[...]
        slo, shi = slo_ref[...], shi_ref[...]
        nq, _, _ = _rms_fwd(lat[:, 0:MLA_Q_LORA], gq_ref[...])
        qa = _dot(_bf(nq), wqb_ref[...])
        cq = cq_ref[...]
        for h in range(MLA_HEADS):
            blk = qa[:, h * LANES:(h + 1) * LANES]
            q_ref[:, h * LANES:(h + 1) * LANES] = _bf(_rope_fwd(blk, cq, slo, shi))
        nkv, _, _ = _rms_fwd(lat[:, MLA_Q_LORA:MLA_Q_LORA + MLA_KV_LORA], gkv_ref[...])
        kv = _dot(_bf(nkv), wkvb_ref[...])
        kpe = _rope_fwd(lat[:, 384:512], ck_ref[...], slo, shi)
        for h in range(MLA_HEADS):
            k_ref[:, h * LANES:(h + 1) * LANES] = _bf(kv[:, h * LANES:(h + 1) * LANES] + kpe)
        v_ref[...] = _bf(kv[:, MLA_HEADS * LANES:])

    row = lambda w: pl.BlockSpec((t, w), lambda i: (i, 0))
    full = lambda shp: pl.BlockSpec(shp, lambda i: (0, 0))
    return pl.pallas_call(
        body, name="mla_prep", grid=(seq // t,),
        in_specs=[row(512), full((1, MLA_Q_LORA)), full((1, MLA_KV_LORA)), full(wqb.shape), full(wkvb.shape),
                  row(LANES), row(LANES), row(LANES), row(LANES)],
        out_specs=[row(1024), row(1024), row(512)],
        out_shape=[jax.ShapeDtypeStruct((seq, 1024), BF16), jax.ShapeDtypeStruct((seq, 1024), BF16),
                   jax.ShapeDtypeStruct((seq, 512), BF16)],
        compiler_params=_cparams(("parallel",)),
    )(p32, gq, gkv, wqb, wkvb, *tabs)


def _mla_post(p32, dq, dk, dv, gq, gkv, wqb, wkvb, tabs, *, t):
    seq = p32.shape[0]

    def body(lat_ref, dq_ref, dk_ref, dv_ref, gq_ref, gkv_ref, wqb_ref, wkvb_ref, cq_ref, ck_ref, slo_ref, shi_ref,
             dlat_ref, dwqb_ref, dwkvb_ref, dgq_ref, dgkv_ref):
        @pl.when(pl.program_id(0) == 0)
        def _():
            dwqb_ref[...] = jnp.zeros_like(dwqb_ref)
            dwkvb_ref[...] = jnp.zeros_like(dwkvb_ref)
            dgq_ref[...] = jnp.zeros_like(dgq_ref)
            dgkv_ref[...] = jnp.zeros_like(dgkv_ref)

        lat = lat_ref[...]
        slo, shi = slo_ref[...], shi_ref[...]
        cq = cq_ref[...]
        gq_v, gkv_v = gq_ref[...], gkv_ref[...]
        nq, xq, rq = _rms_fwd(lat[:, 0:MLA_Q_LORA], gq_v)
        nkv, xkv, rkv = _rms_fwd(lat[:, MLA_Q_LORA:MLA_Q_LORA + MLA_KV_LORA], gkv_v)

        dqa = jnp.concatenate(
            [_rope_bwd(dq_ref[:, h * LANES:(h + 1) * LANES], cq, slo, shi) for h in range(MLA_HEADS)], axis=1)
        dqa_b = _bf(dqa)
        dwqb_ref[...] += _dot_tn(_bf(nq), dqa_b)
        dnq = _dot_nt(dqa_b, wqb_ref[...])
        dgq_ref[...] += jnp.sum(dnq * xq, axis=0, keepdims=True)
        dxn = dnq * gq_v
        dcq = rq * (dxn - xq * jnp.mean(dxn * xq, axis=-1, keepdims=True))

        dkf = dk_ref[...]
        dkv_b = _bf(jnp.concatenate([dkf, dv_ref[...]], axis=1))
        dwkvb_ref[...] += _dot_tn(_bf(nkv), dkv_b)
        dnkv = _dot_nt(dkv_b, wkvb_ref[...])
        dgkv_ref[...] += jnp.sum(dnkv * xkv, axis=0, keepdims=True)
        dxn = dnkv * gkv_v
        dckv = rkv * (dxn - xkv * jnp.mean(dxn * xkv, axis=-1, keepdims=True))

        dkpe = dkf[:, 0:LANES]
        for h in range(1, MLA_HEADS):
            dkpe = dkpe + dkf[:, h * LANES:(h + 1) * LANES]
        dkr = _rope_bwd(dkpe, ck_ref[...], slo, shi)
        dlat_ref[...] = _bf(jnp.concatenate([dcq, dckv, dkr], axis=1))

    row = lambda w: pl.BlockSpec((t, w), lambda i: (i, 0))
    full = lambda shp: pl.BlockSpec(shp, lambda i: (0, 0))
    return pl.pallas_call(
        body, name="mla_post", grid=(seq // t,),
        in_specs=[row(512), row(1024), row(1024), row(512), full((1, MLA_Q_LORA)), full((1, MLA_KV_LORA)),
                  full(wqb.shape), full(wkvb.shape), row(LANES), row(LANES), row(LANES), row(LANES)],
        out_specs=[row(512), full(wqb.shape), full(wkvb.shape), full((1, MLA_Q_LORA)), full((1, MLA_KV_LORA))],
        out_shape=[jax.ShapeDtypeStruct((seq, 512), BF16), jax.ShapeDtypeStruct(wqb.shape, F32),
                   jax.ShapeDtypeStruct(wkvb.shape, F32), jax.ShapeDtypeStruct((1, MLA_Q_LORA), F32),
                   jax.ShapeDtypeStruct((1, MLA_KV_LORA), F32)],
        compiler_params=_cparams(("arbitrary",)),
    )(p32, dq, dk, dv, gq, gkv, wqb, wkvb, *tabs)


def _split_bf16(x):
    hi = _bf(x)
    return hi, _bf(x - hi.astype(F32))


def _tri_sum(x, u):
    hi, lo = _split_bf16(x)
    return _dot(hi, u) + _dot(lo, u)


def _softplus(z):
    return jnp.maximum(z, 0.0) + jnp.log(1.0 + jnp.exp(-jnp.abs(z)))


def _head_queries(q, left):
    zero = jnp.zeros_like(q)
    return jnp.where(left, q, zero) * SB_SCALE, jnp.where(left, zero, q) * SB_SCALE


ROW_GROUP = 128
SB_BWD_CHAINS_IN_FLIGHT = 16
ANY_HBM = pl.BlockSpec(memory_space=pltpu.HBM)


class _Exchange:
    def __init__(self, send, landing):
        self.send, self.landing = send, landing

    def start(self):
        self.send.start()

    def wait(self):
        self.landing.wait_recv()
        self.send.wait_send()


class _Rider:
    def __init__(self, operands, out_shapes, sem_shapes, copies):
        self.operands, self.out_shapes, self.sem_shapes, self.copies = list(operands), list(out_shapes), list(sem_shapes), copies


def _call_with_rider(body, rider, *, name, grid, in_specs, out_specs, out_shape, args, semantics, scratch=()):
    scratch = list(scratch)
    if rider is None:
        return pl.pallas_call(body, name=name, grid=grid, in_specs=in_specs, out_specs=out_specs, out_shape=out_shape,
                              scratch_shapes=scratch, compiler_params=_cparams(semantics))(*args)
    n_in, n_out, n_rin, n_rout = len(in_specs), len(out_specs), len(rider.operands), len(rider.out_shapes)

    def full_body(*refs):
        ins, r_ins = refs[:n_in], refs[n_in:n_in + n_rin]
        outs = refs[n_in + n_rin:n_in + n_rin + n_out]
        r_outs = refs[n_in + n_rin + n_out:n_in + n_rin + n_out + n_rout]
        rest = refs[n_in + n_rin + n_out + n_rout:]
        own_scratch, sems = rest[:len(scratch)], rest[len(scratch):]
        first, last = None, None
        for axis, size in enumerate(grid):
            at_start, at_end = pl.program_id(axis) == 0, pl.program_id(axis) == size - 1
            first = at_start if first is None else first & at_start
            last = at_end if last is None else last & at_end

        @pl.when(first)
        def _():
            for cp in rider.copies(r_ins, r_outs, sems):
                cp.start()

        body(*ins, *outs, *own_scratch)

        @pl.when(last)
        def _():
            for cp in rider.copies(r_ins, r_outs, sems):
                cp.wait()

    return pl.pallas_call(
        full_body, name=name, grid=grid, in_specs=list(in_specs) + [ANY_HBM] * n_rin,
        out_specs=list(out_specs) + [ANY_HBM] * n_rout, out_shape=list(out_shape) + rider.out_shapes,
        scratch_shapes=scratch + rider.sem_shapes, compiler_params=_cparams(("arbitrary",) * len(grid)),
    )(*args, *rider.operands)


def _chains(tq):
    return [(h, g) for g in range(tq // ROW_GROUP) for h in range(2)]


def _chain_pattern(g, m, tk, strict):
    r_lo, r_hi = g * ROW_GROUP, (g + 1) * ROW_GROUP - 1
    c_lo, c_hi = m * tk, (m + 1) * tk - 1
    if (c_lo >= r_hi) if strict else (c_lo > r_hi):
        return None
    if (c_hi < r_lo) if strict else (c_hi <= r_lo):
        return True
    rr = lax.broadcasted_iota(jnp.int32, (ROW_GROUP, tk), 0) + r_lo
    cc = lax.broadcasted_iota(jnp.int32, (ROW_GROUP, tk), 1) + c_lo
    return (cc < rr) if strict else (cc <= rr)


def _masked(x, pat, fill=0.0):
    return x if pat is True else jnp.where(pat, x, fill)


def _rows(g):
    return slice(g * ROW_GROUP, (g + 1) * ROW_GROUP)


def _tri_matrix(tk, cmp):
    rr = lax.broadcasted_iota(jnp.int32, (tk, tk), 0)
    cc = lax.broadcasted_iota(jnp.int32, (tk, tk), 1)
    return cmp(rr, cc).astype(BF16)


def _mla_attn_fwd(qp, kp, vp, *, tq, tk, tk_diag, rider=None):
    seq = qp.shape[0]
    neg = float(np.finfo(np.float32).min)
    chains = _chains(tq)

    def body(q_ref, k_ref, v_ref, o_ref, lse_ref):
        i = pl.program_id(1)
        left = lax.broadcasted_iota(jnp.int32, (tq, LANES), 1) < HALF
        qs = [q_ref[_rows(g), h * LANES:(h + 1) * LANES] for h, g in chains]

        def block(start, carry, m, tk):
            v = v_ref[pl.ds(start, tk), :]
            pats = [True if m is None else _chain_pattern(g, m, tk, False) for _, g in chains]
            live = [n for n, p in enumerate(pats) if p is not None]
            ss = {n: _dot_nt(qs[n], k_ref[pl.ds(start, tk), chains[n][0] * LANES:(chains[n][0] + 1) * LANES]) for n in live}
            new = list(carry)
            for n in live:
                m_old, l_old, acc = carry[n]
                s = _masked(ss[n] * MLA_SCALE, pats[n], neg)
                m_new = jnp.maximum(m_old, jnp.max(s, axis=-1, keepdims=True))
                a = jnp.exp(m_old - m_new)
                p = jnp.exp(s - m_new)
                new[n] = (m_new, a * l_old + jnp.sum(p, axis=-1, keepdims=True), a * acc + _dot(_bf(p), v))
            return tuple(new)

        init = (jnp.full((ROW_GROUP, 1), -1e30, F32), jnp.zeros((ROW_GROUP, 1), F32), jnp.zeros((ROW_GROUP, LANES), F32))
        def two_blocks(j, c):
            c = block(pl.multiple_of(2 * j * tk, tk), c, None, tk)
            return block(pl.multiple_of((2 * j + 1) * tk, tk), c, None, tk)

        carry = lax.fori_loop(0, i * (tq // tk) // 2, two_blocks, (init,) * len(chains))
        for m in range(tq // tk_diag):
            carry = block(pl.multiple_of(i * tq + m * tk_diag, tk_diag), carry, m, tk_diag)
        per_head = []
        for h in range(2):
            mine = [carry[n] for n, (ch, _) in enumerate(chains) if ch == h]
            per_head.append((jnp.concatenate([acc / l for _, l, acc in mine], axis=0),
                             jnp.concatenate([mm + jnp.log(l) for mm, l, _ in mine], axis=0)))
        o_ref[...] = jnp.where(left, per_head[0][0], per_head[1][0])
        lse_ref[...] = jnp.where(left, per_head[0][1], per_head[1][1])

    return _call_with_rider(
        body, rider, name="mla_fwd", grid=(MLA_HEADS // 2, seq // tq),
        in_specs=[pl.BlockSpec((tq, 2 * LANES), lambda p, i: (i, p)), pl.BlockSpec((seq, 2 * LANES), lambda p, i: (0, p)),
                  pl.BlockSpec((seq, LANES), lambda p, i: (0, p))],
        out_specs=[pl.BlockSpec((tq, LANES), lambda p, i: (i, p)), pl.BlockSpec((tq, LANES), lambda p, i: (i, p))],
        out_shape=[jax.ShapeDtypeStruct((seq, 512), F32), jax.ShapeDtypeStruct((seq, 512), F32)],
        args=(qp, kp, vp), semantics=("parallel", "parallel"))


def _mla_attn_bwd(qp, kp, vp, o, lse, do, *, tq, tk, tk_diag, rider=None):
    seq = qp.shape[0]
    chains = _chains(tq)

    def body(q_ref, k_ref, v_ref, o_ref, lse_ref, do_ref, dq_ref, dk_ref, dv_ref, qt_ref, dot_ref):
        i = pl.program_id(1)

        @pl.when(i == 0)
        def _():
            dk_ref[...] = jnp.zeros_like(dk_ref)
            dv_ref[...] = jnp.zeros_like(dv_ref)

        left = lax.broadcasted_iota(jnp.int32, (tq, LANES), 1) < HALF
        do_f = do_ref[...]
        prod = do_f * o_ref[...]
        lse_v = lse_ref[...]
        do_heads = (_bf(jnp.where(left, do_f, 0.0)), _bf(jnp.where(left, 0.0, do_f)))
        delta_heads = (jnp.sum(jnp.where(left, prod, 0.0), axis=-1, keepdims=True),
                       jnp.sum(jnp.where(left, 0.0, prod), axis=-1, keepdims=True))
        qs = [q_ref[_rows(g), h * LANES:(h + 1) * LANES] for h, g in chains]
        dos = [do_heads[h][_rows(g)] for h, g in chains]
        deltas = [delta_heads[h][_rows(g)] for h, g in chains]
        lses = [lse_v[_rows(g), h * HALF:h * HALF + 1] for h, g in chains]
        for h in range(2):
            qt_ref[h] = q_ref[:, h * LANES:(h + 1) * LANES].T
            dot_ref[h] = do_heads[h].T
        q_t = [qt_ref.at[h] for h in range(2)]
        do_t = [dot_ref.at[h] for h in range(2)]

        def block(start, carry, m, tk):
            v = v_ref[pl.ds(start, tk), :]
            pats = [True if m is None else _chain_pattern(g, m, tk, False) for _, g in chains]
            live = [n for n, p in enumerate(pats) if p is not None]
            ks = [k_ref[pl.ds(start, tk), h * LANES:(h + 1) * LANES] for h in range(2)]
            ss = {n: _dot_nt(qs[n], ks[chains[n][0]]) for n in live}
            dps = {n: _dot_nt(dos[n], v) for n in live}
            new = list(carry)
            ps, dss = {}, {}
            for n in live:
                p = _masked(jnp.exp(ss[n] * MLA_SCALE - lses[n]), pats[n])
                ps[n] = _bf(p)
                dss[n] = _bf(p * (dps[n] - deltas[n]) * MLA_SCALE)
                new[n] = carry[n] + _dot(dss[n], ks[chains[n][0]])
            dv_t, dk_t = None, []
            for h in range(2):
                mine = [n for n in live if chains[n][0] == h]
                first_row = chains[mine[0]][1] * ROW_GROUP
                ds_cat = jnp.concatenate([dss[n] for n in mine], axis=0)
                p_cat = jnp.concatenate([ps[n] for n in mine], axis=0)
                if first_row == 0:
                    q_rows_t, do_rows_t = q_t[h][...], do_t[h][...]
                else:
                    q_rows_t = q_ref[first_row:, h * LANES:(h + 1) * LANES].T
                    do_rows_t = do_heads[h][first_row:].T
                dk_t.append(_dot(q_rows_t, ds_cat))
                term = _dot(do_rows_t, p_cat)
                dv_t = term if dv_t is None else dv_t + term
            back = jnp.concatenate(dk_t + [dv_t], axis=0).T
            dk_ref[pl.ds(start, tk), :] += back[:, :2 * LANES]
            dv_ref[pl.ds(start, tk), :] += back[:, 2 * LANES:]
            return tuple(new)

        zero = jnp.zeros((ROW_GROUP, LANES), F32)
        carry = lax.fori_loop(0, i * (tq // tk), lambda j, c: block(pl.multiple_of(j * tk, tk), c, None, tk),
                              (zero,) * len(chains))
        for m in range(tq // tk_diag):
            carry = block(pl.multiple_of(i * tq + m * tk_diag, tk_diag), carry, m, tk_diag)
        for n, (h, g) in enumerate(chains):
            dq_ref[_rows(g), h * LANES:(h + 1) * LANES] = carry[n]

    two_t = pl.BlockSpec((tq, 2 * LANES), lambda p, i: (i, p))
    two_s = pl.BlockSpec((seq, 2 * LANES), lambda p, i: (0, p))
    pair_t = pl.BlockSpec((tq, LANES), lambda p, i: (i, p))
    pair_s = pl.BlockSpec((seq, LANES), lambda p, i: (0, p))
    return _call_with_rider(
        body, rider, name="mla_bwd", grid=(MLA_HEADS // 2, seq // tq),
        in_specs=[two_t, two_s, pair_s, pair_t, pair_t, pair_t],
        out_specs=[two_t, two_s, pair_s],
        out_shape=[jax.ShapeDtypeStruct((seq, 1024), F32), jax.ShapeDtypeStruct((seq, 1024), F32),
                   jax.ShapeDtypeStruct((seq, 512), F32)],
        args=(qp, kp, vp, o, lse, do), semantics=("parallel", "arbitrary"),
        scratch=[pltpu.VMEM((2, LANES, tq), BF16), pltpu.VMEM((2, LANES, tq), BF16)])


def _sb_attn_fwd(pbf, *, tq, tk):
    seq = pbf.shape[0]
    nd = tq // tk
    qb, kb, vb = PBF_POS[BLK_QB] * 4, PBF_POS[BLK_KB] * 4, PBF_POS[BLK_VB] * 4
    chains = _chains(tq)

    def body(q_ref, k_ref, v_ref, o_ref, tot_ref):
        i = pl.program_id(1)
        u_later = _tri_matrix(tk, lambda r, c: r > c)
        left = lax.broadcasted_iota(jnp.int32, (tq, LANES), 1) < HALF
        q_heads = _head_queries(q_ref[...], left)
        qs = [q_heads[h][_rows(g)] for h, g in chains]

        def block(j, carry, m):
            start = pl.multiple_of(j * tk, tk)
            k = k_ref[pl.ds(start, tk), :]
            v = v_ref[pl.ds(start, tk), :]
            pats = [True if m is None else _chain_pattern(g, m, tk, True) for _, g in chains]
            live = [n for n, p in enumerate(pats) if p is not None]
            zs = {n: _dot_nt(qs[n], k) for n in live}
            raws = {n: _softplus(zs[n]) for n in live}
            sps = {n: _masked(raws[n], pats[n]) for n in live}
            laters = {n: _tri_sum(sps[n], u_later) for n in live}
            new = list(carry)
            for n in live:
                c, acc = carry[n]
                a = _masked(jnp.exp(zs[n] - raws[n] - laters[n] - c), pats[n])
                new[n] = (c + laters[n][:, 0:1] + sps[n][:, 0:1], acc + _dot(_bf(a), v))
            return tuple(new)

        init = (jnp.zeros((ROW_GROUP, 1), F32), jnp.zeros((ROW_GROUP, LANES), F32))
        carry = (init,) * len(chains)
        for m in reversed(range(nd)):
            carry = block(i * nd + m, carry, m)
        per_trip = 4 if nd % 4 == 0 else 2

        def trip(jj, cr):
            for u in range(per_trip):
                cr = block(i * nd - 1 - (per_trip * jj + u), cr, None)
            return cr

        carry = lax.fori_loop(0, i * nd // per_trip, trip, carry)
        per_head = []
        for h in range(2):
            mine = [carry[n] for n, (ch, _) in enumerate(chains) if ch == h]
            per_head.append((jnp.concatenate([acc for _, acc in mine], axis=0), jnp.concatenate([c for c, _ in mine], axis=0)))
        o_ref[...] = jnp.where(left, per_head[0][0], per_head[1][0])
        tot_ref[...] = jnp.where(left, per_head[0][1], per_head[1][1])

    pair_t = pl.BlockSpec((tq, LANES), lambda p, i: (i, p))
    return pl.pallas_call(
        body, name="sb_fwd", grid=(SB_HEADS // 2, seq // tq),
        in_specs=[pl.BlockSpec((tq, LANES), lambda p, i: (i, qb + p)), pl.BlockSpec((seq, LANES), lambda p, i: (0, kb + p)),
                  pl.BlockSpec((seq, LANES), lambda p, i: (0, vb + p))],
        out_specs=[pair_t, pair_t],
        out_shape=[jax.ShapeDtypeStruct((seq, 512), F32), jax.ShapeDtypeStruct((seq, 512), F32)],
        compiler_params=_cparams(("parallel", "parallel")),
    )(pbf, pbf, pbf)


def _sb_attn_bwd(pbf, tot, do, *, tq, tk, rider=None):
    seq = pbf.shape[0]
    nd = tq // tk
    qb, kb, vb = PBF_POS[BLK_QB] * 4, PBF_POS[BLK_KB] * 4, PBF_POS[BLK_VB] * 4
    chains = _chains(tq)
    group = SB_BWD_CHAINS_IN_FLIGHT

    def body(q_ref, k_ref, v_ref, tot_ref, do_ref, dq_ref, dk_ref, dv_ref, qt_ref, dot_ref):
        i = pl.program_id(1)

        @pl.when(i == 0)
        def _():
            dk_ref[...] = jnp.zeros_like(dk_ref)
            dv_ref[...] = jnp.zeros_like(dv_ref)

        u_upto = _tri_matrix(tk, lambda r, c: r <= c)
        u_below = _tri_matrix(tk, lambda r, c: r < c)
        left = lax.broadcasted_iota(jnp.int32, (tq, LANES), 1) < HALF
        q_heads = _head_queries(q_ref[...], left)
        do_f = do_ref[...]
        do_heads = (_bf(jnp.where(left, do_f, 0.0)), _bf(jnp.where(left, 0.0, do_f)))
        tot_v = tot_ref[...]
        qs = [q_heads[h][_rows(g)] for h, g in chains]
        dos = [do_heads[h][_rows(g)] for h, g in chains]
        totals = [tot_v[_rows(g), h * HALF:h * HALF + 1] for h, g in chains]
        qt_ref[...] = jnp.concatenate(qs, axis=0).T
        dot_ref[...] = jnp.concatenate(dos, axis=0).T

        def block(j, carry, m):
            start = pl.multiple_of(j * tk, tk)
            k = k_ref[pl.ds(start, tk), :]
            v = v_ref[pl.ds(start, tk), :]
            pats = [True if m is None else _chain_pattern(g, m, tk, True) for _, g in chains]
            all_live = [n for n, p in enumerate(pats) if p is not None]
            new = list(carry)
            for g0 in range(0, len(all_live), group):
                live = all_live[g0:g0 + group]
                zs = {n: _dot_nt(qs[n], k) for n in live}
                das = {n: _dot_nt(dos[n], v) for n in live}
                raws = {n: _softplus(zs[n]) for n in live}
                sps = {n: _masked(raws[n], pats[n]) for n in live}
                uptos = {n: _tri_sum(sps[n], u_upto) for n in live}
                lbs, a_s, gs = {}, {}, {}
                for n in live:
                    lbs[n] = zs[n] - raws[n]
                    a = _masked(jnp.exp(lbs[n] - (totals[n] - carry[n][0] - uptos[n])), pats[n])
                    a_s[n] = _bf(a)
                    gs[n] = das[n] * a
                belows = {n: _dot(_bf(gs[n]), u_below) for n in live}
                dzs = {}
                for n in live:
                    sp_before, g_before, dq_acc = carry[n]
                    beta = jnp.exp(lbs[n])
                    dz = _masked(gs[n] * (1.0 - beta) - (g_before + belows[n]) * beta, pats[n])
                    dzs[n] = _bf(dz)
                    new[n] = (sp_before + uptos[n][:, tk - 1:tk], g_before + belows[n][:, tk - 1:tk] + gs[n][:, tk - 1:tk],
                              dq_acc + _dot(dzs[n], k))
                dz_cat = jnp.concatenate([dzs[n] for n in live], axis=0)
                a_cat = jnp.concatenate([a_s[n] for n in live], axis=0)
                if len(live) == len(chains):
                    q_rows_t, do_rows_t = qt_ref[...], dot_ref[...]
                else:
                    q_rows_t = jnp.concatenate([qs[n] for n in live], axis=0).T
                    do_rows_t = jnp.concatenate([dos[n] for n in live], axis=0).T
                both = jnp.concatenate([_dot(q_rows_t, dz_cat), _dot(do_rows_t, a_cat)], axis=0).T
                dk_ref[pl.ds(start, tk), :] += both[:, :LANES]
                dv_ref[pl.ds(start, tk), :] += both[:, LANES:]
            return tuple(new)

        zero = jnp.zeros((ROW_GROUP, 1), F32)
        init = (zero, zero, jnp.zeros((ROW_GROUP, LANES), F32))
        per_trip = 4 if nd % 4 == 0 else 2

        def trip(j, cr):
            for u in range(per_trip):
                cr = block(per_trip * j + u, cr, None)
            return cr

        carry = lax.fori_loop(0, i * nd // per_trip, trip, (init,) * len(chains))
        for m in range(nd):
            carry = block(i * nd + m, carry, m)
        per_head = [jnp.concatenate([carry[n][2] for n, (ch, _) in enumerate(chains) if ch == h], axis=0) for h in range(2)]
        dq_ref[...] = jnp.where(left, per_head[0], per_head[1]) * SB_SCALE

    pair_t = pl.BlockSpec((tq, LANES), lambda p, i: (i, p))
    pair_s = pl.BlockSpec((seq, LANES), lambda p, i: (0, p))
    return _call_with_rider(
        body, rider, name="sb_bwd", grid=(SB_HEADS // 2, seq // tq),
        in_specs=[pl.BlockSpec((tq, LANES), lambda p, i: (i, qb + p)), pl.BlockSpec((seq, LANES), lambda p, i: (0, kb + p)),
                  pl.BlockSpec((seq, LANES), lambda p, i: (0, vb + p)), pair_t, pair_t],
        out_specs=[pair_t, pair_s, pair_s],
        out_shape=[jax.ShapeDtypeStruct((seq, 512), F32)] * 3,
        args=(pbf, pbf, pbf, tot, do), semantics=("parallel", "arbitrary"),
        scratch=[pltpu.VMEM((LANES, 2 * tq), BF16), pltpu.VMEM((LANES, 2 * tq), BF16)])


def _mem_probs(s):
    e = jnp.exp(s - jnp.max(s, axis=-1, keepdims=True))
    return e / jnp.sum(e, axis=-1, keepdims=True)


def _head_lanes(h):
    return slice(h * LANES, (h + 1) * LANES)


def _mem_fwd(pbf, mkv, *, t):
    seq = pbf.shape[0]

    def body(q_ref, kv_ref, o_ref):
        ss = [_dot_nt(q_ref[:, _head_lanes(h)], kv_ref[:, _head_lanes(h)]) * MEM_SCALE for h in range(MEM_HEADS)]
        ps = [_bf(_mem_probs(s)) for s in ss]
        for h, p in enumerate(ps):
            o_ref[:, _head_lanes(h)] = _dot(p, kv_ref[:, _head_lanes(MEM_HEADS + h)])

    return pl.pallas_call(
        body, name="mem_fwd", grid=(seq // t,),
        in_specs=[pl.BlockSpec((t, 512), lambda i: (i, PBF_POS[BLK_QM])), pl.BlockSpec((MEM_LEN, 1024), lambda i: (0, 0))],
        out_specs=pl.BlockSpec((t, 512), lambda i: (i, 0)),
        out_shape=jax.ShapeDtypeStruct((seq, 512), F32),
        compiler_params=_cparams(("parallel",)),
    )(pbf, mkv)


def _mem_bwd(pbf, mkv, do, *, t):
    seq = pbf.shape[0]

    def body(q_ref, kv_ref, do_ref, dq_ref, dkv_ref):
        @pl.when(pl.program_id(0) == 0)
        def _():
            dkv_ref[...] = jnp.zeros_like(dkv_ref)

        heads = range(MEM_HEADS)
        qs = [q_ref[:, _head_lanes(h)] for h in heads]
        ks = [kv_ref[:, _head_lanes(h)] for h in heads]
        dos = [_bf(do_ref[:, _head_lanes(h)]) for h in heads]
        ss = [_dot_nt(qs[h], ks[h]) * MEM_SCALE for h in heads]
        dps = [_dot_nt(dos[h], kv_ref[:, _head_lanes(MEM_HEADS + h)]) for h in heads]
        ps = [_mem_probs(s) for s in ss]
        dss = [_bf(ps[h] * (dps[h] - jnp.sum(dps[h] * ps[h], axis=-1, keepdims=True)) * MEM_SCALE) for h in heads]
        for h in heads:
            dq_ref[:, _head_lanes(h)] = _dot(dss[h], ks[h])
        for h in heads:
            dkv_ref[:, _head_lanes(h)] += _dot_tn(dss[h], qs[h])
            dkv_ref[:, _head_lanes(MEM_HEADS + h)] += _dot_tn(_bf(ps[h]), dos[h])

    return pl.pallas_call(
        body, name="mem_bwd", grid=(seq // t,),
        in_specs=[pl.BlockSpec((t, 512), lambda i: (i, PBF_POS[BLK_QM])), pl.BlockSpec((MEM_LEN, 1024), lambda i: (0, 0)),
                  pl.BlockSpec((t, 512), lambda i: (i, 0))],
        out_specs=[pl.BlockSpec((t, 512), lambda i: (i, 0)), pl.BlockSpec((MEM_LEN, 1024), lambda i: (0, 0))],
        out_shape=[jax.ShapeDtypeStruct((seq, 512), F32), jax.ShapeDtypeStruct((MEM_LEN, 1024), F32)],
        compiler_params=_cparams(("arbitrary",)),
    )(pbf, mkv, do)


def _mid(x, tgt, o_a, o_b, o_m, p32, wmg, bmg, wba, wbb, wbm, wout, ln_g, ln_b, *, t):
    seq = x.shape[0]
    inv_d = 1.0 / D_MODEL

    def body(x_ref, t_ref, oa_ref, ob_ref, om_ref, ga_ref, gb_ref, gm_ref, wmg_ref, bmg_ref, wba_ref, wbb_ref,
             wbm_ref, wout_ref, lg_ref, lb_ref,
             du_ref, mrg_ref, dgp_ref, ha_ref, hb_ref, hm_ref, dya_ref, dyb_ref, dym_ref, doa_ref, dob_ref, dom_ref,
             dga_ref, dgb_ref, dgm_ref, dgain_ref, dbias_ref, dbmg_ref, loss_ref):
        @pl.when(pl.program_id(0) == 0)
        def _():
            dgain_ref[...] = jnp.zeros_like(dgain_ref)
            dbias_ref[...] = jnp.zeros_like(dbias_ref)
            dbmg_ref[...] = jnp.zeros_like(dbmg_ref)
            loss_ref[...] = jnp.zeros_like(loss_ref)

        xv = x_ref[...]
        gate = _sigmoid(_dot_cols(_bf(xv), wmg_ref) + bmg_ref[...])

        branches = []
        merged = None
        for b, (o_ref, g_ref, w_ref, h_ref) in enumerate(((oa_ref, ga_ref, wba_ref, ha_ref), (ob_ref, gb_ref, wbb_ref, hb_ref),
                                                         (om_ref, gm_ref, wbm_ref, hm_ref))):
            o, gt = o_ref[...], g_ref[...]
            sg = _sigmoid(gt)
            silu = gt * sg
            h = _bf(o * silu)
            h_ref[...] = h
            y = _dot_cols(h, w_ref)
            g_b = gate[:, b * D_MODEL:(b + 1) * D_MODEL]
            term = g_b * y
            merged = term if merged is None else merged + term
            branches.append((o, gt, sg, silu, y, g_b))
        mrg_b = _bf(merged)
        mrg_ref[...] = mrg_b

        u = DEEPNORM_ALPHA * xv + _dot(mrg_b, wout_ref[...])
        mu = jnp.mean(u, axis=-1, keepdims=True)
        uc = u - mu
        rstd = lax.rsqrt(jnp.mean(uc * uc, axis=-1, keepdims=True) + LN_EPS)
        xhat = uc * rstd
        lg = lg_ref[...]
        y_out = xhat * lg + lb_ref[...]
        err = y_out - t_ref[...]
        loss_ref[...] += 0.5 * jnp.sum(jnp.mean(err * err, axis=-1, keepdims=True), axis=0, keepdims=True)
        dy = err * inv_d
        dgain_ref[...] += jnp.sum(dy * xhat, axis=0, keepdims=True)
        dbias_ref[...] += jnp.sum(dy, axis=0, keepdims=True)
        dxh = dy * lg
        du = rstd * (dxh - jnp.mean(dxh, axis=-1, keepdims=True) - xhat * jnp.mean(dxh * xhat, axis=-1, keepdims=True))
        du_ref[...] = du

        dmerged = _dot_nt(_bf(du), wout_ref[...])
        outs = ((dya_ref, doa_ref, dga_ref, wba_ref), (dyb_ref, dob_ref, dgb_ref, wbb_ref), (dym_ref, dom_ref, dgm_ref, wbm_ref))
        dgp = []
        for (o, gt, sg, silu, y, g_b), (dy_ref, do_ref, dg_ref, w_ref) in zip(branches, outs):
            dyb = _bf(dmerged * g_b)
            dy_ref[...] = dyb
            dgp.append(dmerged * y * g_b * (1.0 - g_b))
            dh = _dot_nt_cols(dyb, w_ref)
            do_ref[...] = dh * silu
            dg_ref[...] = _bf(dh * o * (sg * (1.0 + gt * (1.0 - sg))))
        dgp = jnp.concatenate(dgp, axis=1)
        dgp_ref[...] = _bf(dgp)
        dbmg_ref[...] += jnp.sum(dgp, axis=0, keepdims=True)

    row = lambda w: pl.BlockSpec((t, w), lambda i: (i, 0))
    pblk = lambda c: pl.BlockSpec((t, 512), lambda i: (i, c))
    full = lambda shp: pl.BlockSpec(shp, lambda i: (0,) * len(shp))
    sds = jax.ShapeDtypeStruct
    return pl.pallas_call(
        body, name="mid", grid=(seq // t,),
        in_specs=[row(1024), row(1024), row(512), row(512), row(512), pblk(P32_POS[BLK_GATE_A]), pblk(P32_POS[BLK_GATE_B]), pblk(P32_POS[BLK_GATE_M]),
                  full(wmg.shape), full((1, N_MERGE)), full(wba.shape), full(wbb.shape), full(wbm.shape), full(wout.shape),
                  full((1, D_MODEL)), full((1, D_MODEL))],
        out_specs=[row(1024), row(1024), row(N_MERGE), row(512), row(512), row(512), row(1024), row(1024), row(1024),
                   row(512), row(512), row(512), row(512), row(512), row(512),
                   full((1, D_MODEL)), full((1, D_MODEL)), full((1, N_MERGE)), full((1, 1))],
        out_shape=[sds((seq, 1024), F32), sds((seq, 1024), BF16), sds((seq, N_MERGE), BF16),
                   sds((seq, 512), BF16), sds((seq, 512), BF16), sds((seq, 512), BF16),
                   sds((seq, 1024), BF16), sds((seq, 1024), BF16), sds((seq, 1024), BF16),
                   sds((seq, 512), F32), sds((seq, 512), F32), sds((seq, 512), F32),
                   sds((seq, 512), BF16), sds((seq, 512), BF16), sds((seq, 512), BF16),
                   sds((1, D_MODEL), F32), sds((1, D_MODEL), F32), sds((1, N_MERGE), F32), sds((1, 1), F32)],
        compiler_params=_cparams(("arbitrary",)),
    )(x, tgt, o_a, o_b, o_m, p32, p32, p32, wmg, bmg, wba, wbb, wbm, wout, ln_g, ln_b)


def _primed_weights(w):
    w_in = w["w_in"]
    zc = lambda n: jnp.zeros((D_MODEL, n), w_in.dtype)
    w_in_p = jnp.concatenate([w_in[:, 0:384], zc(64), w_in[:, 384:416], zc(32), w_in[:, 416:]], axis=1)
    wqb = jnp.pad(w["w_q_b"].reshape(MLA_Q_LORA, MLA_HEADS, 96), ((0, 0), (0, 0), (0, 32))).reshape(MLA_Q_LORA, 1024)
    kv3 = w["w_kv_b"].reshape(MLA_KV_LORA, MLA_HEADS, 128)
    wk = jnp.pad(kv3[:, :, :MLA_NOPE], ((0, 0), (0, 0), (0, 64))).reshape(MLA_KV_LORA, 1024)
    wv = kv3[:, :, MLA_NOPE:].reshape(MLA_KV_LORA, 512)
    return w_in_p, wqb, jnp.concatenate([wk, wv], axis=1)


PROJ_BLK = 512


def _proj_in(x, w_in_p, *, tm):
    seq = x.shape[0]

    def body(x_ref, w_ref, p32_ref, pbf_ref):
        xb = _bf(x_ref[...])
        for blk in range(IN_WIDTH_P // PROJ_BLK):
            r = _dot(xb, w_ref[:, blk * PROJ_BLK:(blk + 1) * PROJ_BLK])
            if blk in P32_POS:
                p32_ref[:, P32_POS[blk] * PROJ_BLK:(P32_POS[blk] + 1) * PROJ_BLK] = r
            else:
                pbf_ref[:, PBF_POS[blk] * PROJ_BLK:(PBF_POS[blk] + 1) * PROJ_BLK] = _bf(r)

    row = lambda w: pl.BlockSpec((tm, w), lambda i: (i, 0))
    return pl.pallas_call(
        body, name="proj_in", grid=(seq // tm,),
        in_specs=[row(D_MODEL), pl.BlockSpec(w_in_p.shape, lambda i: (0, 0))],
        out_specs=[row(len(P32_POS) * PROJ_BLK), row(len(PBF_POS) * PROJ_BLK)],
        out_shape=[jax.ShapeDtypeStruct((seq, len(P32_POS) * PROJ_BLK), F32),
                   jax.ShapeDtypeStruct((seq, len(PBF_POS) * PROJ_BLK), BF16)],
        compiler_params=_cparams(("parallel",)),
    )(x, w_in_p)


def _grad_x(du, dgpre, wmg, d_proj, w_in_p, *, tm, rider=None):
    seq = du.shape[0]
    n_pieces = len(d_proj)

    def body(du_ref, dg_ref, wmg_ref, *rest):
        piece_refs, win_ref, out_ref = rest[:n_pieces], rest[n_pieces], rest[n_pieces + 1]
        d_p = jnp.concatenate([_bf(p_ref[...]) for p_ref in piece_refs], axis=1)
        out_ref[...] = (DEEPNORM_ALPHA * du_ref[...] + _dot_nt_cols(dg_ref[...], wmg_ref)) + _dot_nt(d_p, win_ref[...])

    row = lambda w: pl.BlockSpec((tm, w), lambda i: (i, 0))
    whole = lambda a: pl.BlockSpec(a.shape, lambda i: (0,) * a.ndim)
    return _call_with_rider(
        body, rider, name="grad_x", grid=(seq // tm,),
        in_specs=[row(D_MODEL), row(N_MERGE), whole(wmg)] + [row(PROJ_BLK) for _ in d_proj] + [whole(w_in_p)],
        out_specs=[row(D_MODEL)], out_shape=[jax.ShapeDtypeStruct((seq, D_MODEL), F32)],
        args=(du, dgpre, wmg, *d_proj, w_in_p), semantics=("parallel",))


def _grad_w_in(x, d_proj, *, tk):
    seq = x.shape[0]
    n_pieces = len(d_proj)
    nk = seq // tk

    def body(x_ref, *rest):
        piece_refs, out_ref, acc = rest[:n_pieces], rest[n_pieces], rest[n_pieces + 1]
        j, kk = pl.program_id(0), pl.program_id(1)

        @pl.when(kk == 0)
        def _():
            acc[...] = jnp.zeros_like(acc)

        xb = _bf(x_ref[...])
        for pair in range(n_pieces // 2):
            @pl.when(j == pair)
            def _(pair=pair):
                both = jnp.concatenate([_bf(piece_refs[2 * pair][...]), _bf(piece_refs[2 * pair + 1][...])], axis=1)
                acc[...] += _dot_tn(xb, both)

        @pl.when(kk == nk - 1)
        def _():
            out_ref[...] = acc[...]

    def piece_spec(s):
        return pl.BlockSpec((tk, PROJ_BLK), lambda j, kk: (jnp.where(j == s // 2, kk, 0), 0))

    return pl.pallas_call(
        body, name="grad_w_in", grid=(n_pieces // 2, nk),
        in_specs=[pl.BlockSpec((tk, D_MODEL), lambda j, kk: (kk, 0))] + [piece_spec(s) for s in range(n_pieces)],
        out_specs=pl.BlockSpec((D_MODEL, 2 * PROJ_BLK), lambda j, kk: (0, j)),
        out_shape=jax.ShapeDtypeStruct((D_MODEL, n_pieces * PROJ_BLK), F32),
        scratch_shapes=[pltpu.VMEM((D_MODEL, 2 * PROJ_BLK), F32)],
        compiler_params=_cparams(("parallel", "arbitrary")),
    )(x, *d_proj)


EARLY_NAMES = ("w_mem_kv", "w_branch_mla", "w_branch_sb", "w_branch_mem", "w_merge_gate", "w_out")
LATE_NAMES = ("w_in", "w_q_b", "w_kv_b")


def _remote(src, dst, send_sem, recv_sem, device):
    return pltpu.make_async_remote_copy(src_ref=src, dst_ref=dst, send_sem=send_sem, recv_sem=recv_sem, device_id=device,
                                        device_id_type=MESH_ID)


def _gather_rider(shards):
    n = len(shards)

    def copies(src_refs, out_refs, sems):
        send_sems, recv_sems, local_sems = sems
        x, y, c = _place()
        me = 2 * x + y
        out = []
        for a, (s, o) in enumerate(zip(src_refs, out_refs)):
            out.append(pltpu.make_async_copy(s, o.at[me], local_sems.at[a]))
            for k, (px, py) in enumerate(_other_chips(x, y)):
                out.append(_Exchange(_remote(s, o.at[me], send_sems.at[k, a], recv_sems.at[k, a], (px, py, c)),
                                     _remote(s, o.at[2 * px + py], send_sems.at[k, a], recv_sems.at[k, a], (px, py, c))))
        return out

    return _Rider(shards, [jax.ShapeDtypeStruct((N_CHIPS,) + s.shape, s.dtype) for s in shards],
                  [pltpu.SemaphoreType.DMA((3, n)), pltpu.SemaphoreType.DMA((3, n)), pltpu.SemaphoreType.DMA((n,))], copies)


def _sibling_rider(g4):
    n = len(g4)

    def copies(g_refs, out_refs, sems):
        send_sems, recv_sems = sems
        x, y, c = _place()
        out = []
        for a, (g, o) in enumerate(zip(g_refs, out_refs)):
            half = g.shape[1] // 2
            theirs = pl.ds(pl.multiple_of((1 - c) * half, 8), half)
            cp = _remote(g.at[:, theirs, :], o, send_sems.at[a], recv_sems.at[a], (x, y, 1 - c))
            out.append(_Exchange(cp, cp))
        return out

    return _Rider(g4, [jax.ShapeDtypeStruct((N_CHIPS, g.shape[1] // 2, g.shape[2]), g.dtype) for g in g4],
                  [pltpu.SemaphoreType.DMA((n,)), pltpu.SemaphoreType.DMA((n,))], copies)


def _chips_rider(wire):
    n = len(wire)

    def copies(s_refs, out_refs, sems):
        send_sems, recv_sems = sems
        x, y, c = _place()
        out = []
        for a, (s, o) in enumerate(zip(s_refs, out_refs)):
            for k, (px, py) in enumerate(_other_chips(x, y)):
                cp = _remote(s.at[2 * px + py], o.at[RELATION_XOR[k] - 1], send_sems.at[k, a], recv_sems.at[k, a], (px, py, c))
                out.append(_Exchange(cp, cp))
        return out

    return _Rider(wire, [jax.ShapeDtypeStruct((3,) + s.shape[1:], s.dtype) for s in wire],
                  [pltpu.SemaphoreType.DMA((3, n)), pltpu.SemaphoreType.DMA((3, n))], copies)


def _local_step(x, mem, tgt, w, small, *, tq, tq_sb_bwd, tk, tk_mla, t_row, t_wg, rest_shards=None):
    seq = x.shape[0]
    on_mesh = rest_shards is not None
    w_in_p, wqb, wkvb = _primed_weights(w)
    tabs = _rope_tables(seq)

    p32, pbf = _proj_in(x, w_in_p, tm=256)
    t_light = min(2 * t_row, seq)
    qp, kp, vp = _mla_prep(p32, small["q_a_gain"], small["kv_a_gain"], wqb, wkvb, tabs, t=t_light)
    res = _mla_attn_fwd(qp, kp, vp, tq=tq, tk=tk_mla, tk_diag=tk, rider=_gather_rider(rest_shards) if on_mesh else None)
    o_a, lse = res[0], res[1]
    if on_mesh:
        w = dict(w, **{n: g if n in COL_SHARDED else _join_chips(n, g) for n, g in zip(EARLY_NAMES, res[2:])})
    else:
        w = dict(w, **{n: _split_by_chip(n, w[n]) for n in EARLY_NAMES if n in COL_SHARDED})
    wmg, wout = w["w_merge_gate"], w["w_out"]
    wba, wbb, wbm = w["w_branch_mla"], w["w_branch_sb"], w["w_branch_mem"]
    o_b, keep_total = _sb_attn_fwd(pbf, tq=tq, tk=tk)
    (mkv,) = _matmul(mem, w["w_mem_kv"], mode="nn", tm=MEM_LEN, tn=512, tk=D_MODEL, out_dtypes=(BF16,), name="mem_kv")
    o_m = _mem_fwd(pbf, mkv, t=t_light)

    (du, merged, dgpre, h_a, h_b, h_m, dy_a, dy_b, dy_m, do_a, do_b, do_m, dgate_a, dgate_b, dgate_m,
     d_ln_g, d_ln_b, d_bmg, loss) = _mid(x, tgt, o_a, o_b, o_m, p32, wmg, small["b_merge_gate"], wba, wbb, wbm, wout,
                                         small["ln_gain"], small["ln_bias"], t=t_row)

    wg = functools.partial(_matmul, mode="tn", tm=512, out_dtypes=(F32,))
    shard = (lambda width: dict(tn=width // N_CHIPS, by_column_block=True)) if on_mesh else (lambda width: dict(tn=1024))
    dq_m, dmkv = _mem_bwd(pbf, mkv, do_m, t=t_light)
    early = {"w_mem_kv": wg(mem, dmkv, tk=MEM_LEN, tn=1024, name="grad_w_mem_kv")[0],
             "w_branch_mla": wg(h_a, dy_a, tk=t_wg, name="grad_w_branch_a", **shard(D_MODEL))[0],
             "w_branch_sb": wg(h_b, dy_b, tk=t_wg, name="grad_w_branch_b", **shard(D_MODEL))[0],
             "w_branch_mem": wg(h_m, dy_m, tk=t_wg, name="grad_w_branch_m", **shard(D_MODEL))[0],
             "w_merge_gate": wg(x, dgpre, tk=t_wg, name="grad_w_merge_gate", **shard(N_MERGE))[0],
             "w_out": wg(merged, du, tk=t_wg, tn=1024, name="grad_w_out")[0]}

    if on_mesh:
        g4 = [early[n] if early[n].ndim == 3 else _split_by_chip(n, early[n]) for n in EARLY_NAMES]
        res = _mla_attn_bwd(qp, kp, vp, o_a, lse, do_a, tq=tq, tk=tk_mla, tk_diag=tk, rider=_sibling_rider(g4))
        (dqp, dkp, dvp), got = res[:3], res[3:]
        chipsum, wire = _rs_add_sibling(g4, got, [BF16] * len(g4))
        res = _sb_attn_bwd(pbf, keep_total, do_b, tq=tq_sb_bwd, tk=tk, rider=_chips_rider(wire))
        (dq_b, dk_b, dv_b), parts = res[:3], res[3:]
        early = _rs_add_chips(chipsum, parts)
    else:
        dqp, dkp, dvp = _mla_attn_bwd(qp, kp, vp, o_a, lse, do_a, tq=tq, tk=tk_mla, tk_diag=tk)
        dq_b, dk_b, dv_b = _sb_attn_bwd(pbf, keep_total, do_b, tq=tq_sb_bwd, tk=tk)
    dlat, d_wqb, d_wkvb, d_gq, d_gkv = _mla_post(p32, dqp, dkp, dvp, small["q_a_gain"], small["kv_a_gain"], wqb, wkvb, tabs,
                                                 t=t_light)

    d_proj = [dlat, dgate_a, dq_b, dk_b, dv_b, dgate_b, dq_m, dgate_m]
    d_winp = _grad_w_in(x, d_proj, tk=min(1024, seq))

    d_win = jnp.concatenate([d_winp[:, 0:384], d_winp[:, 448:480], d_winp[:, 512:]], axis=1)
    d_wq = d_wqb.reshape(MLA_Q_LORA, MLA_HEADS, 128)[:, :, :96].reshape(MLA_Q_LORA, 768)
    d_wk = d_wkvb[:, :1024].reshape(MLA_KV_LORA, MLA_HEADS, 128)[:, :, :MLA_NOPE]
    d_wv = d_wkvb[:, 1024:].reshape(MLA_KV_LORA, MLA_HEADS, MLA_V)
    d_wkv = jnp.concatenate([d_wk, d_wv], axis=2).reshape(MLA_KV_LORA, 1024)
    late = {"w_in": d_win, "w_q_b": d_wq, "w_kv_b": d_wkv}
    small_grads = {"q_a_gain": d_gq, "kv_a_gain": d_gkv, "b_merge_gate": d_bmg, "ln_gain": d_ln_g, "ln_bias": d_ln_b}
    if not on_mesh:
        (grad_x,) = _grad_x(du, dgpre, wmg, d_proj, w_in_p, tm=256)
        return loss[0, 0], grad_x, late, small_grads, early

    g4 = [_split_by_chip(n, late[n]) for n in LATE_NAMES]
    g4.append(jnp.broadcast_to(_pack_small(small_grads, scalar=loss)[None], (N_CHIPS, SMALL_ROWS, PACK_COLS)))
    got = _rs_to_sibling(g4)
    chipsum, wire = _rs_add_sibling(g4, got, [BF16] * len(LATE_NAMES) + [F32])
    res = _grad_x(du, dgpre, wmg, d_proj, w_in_p, tm=256, rider=_chips_rider(wire))
    late_mine = _rs_add_chips(chipsum, res[1:])
    return loss[0, 0], res[0], late_mine, None, early


def _place():
    x, y, c = lax.axis_index("x"), lax.axis_index("y"), lax.axis_index("c")
    return x, y, c


def _other_chips(x, y):
    return ((1 - x, y), (x, 1 - y), (1 - x, 1 - y))


SMALL_ROWS = 64
ADAM_STEPS_PER_HALF = 4


def _pack_small(d, scalar=None):
    parts = [d[n].reshape(-1) for n, _ in SMALL_SIZES] + ([] if scalar is None else [scalar.reshape(1)])
    flat = jnp.concatenate(parts)
    return jnp.pad(flat, (0, SMALL_ROWS * PACK_COLS - flat.shape[0])).reshape(SMALL_ROWS, PACK_COLS)


def _unpack_small(a):
    flat, out, c0 = a.reshape(-1), {}, 0
    for n, size in SMALL_SIZES:
        out[n] = flat[c0:c0 + size].reshape(1, size)
        c0 += size
    return out


def _split_by_chip(name, full):
    r, c = full.shape
    if name in COL_SHARDED:
        return full.reshape(r, N_CHIPS, c // N_CHIPS).transpose(1, 0, 2)
    return full.reshape(N_CHIPS, r // N_CHIPS, c)


def _join_chips(name, slots):
    _, r, cs = slots.shape
    if name in COL_SHARDED:
        return slots.transpose(1, 0, 2).reshape(r, N_CHIPS * cs)
    return slots.reshape(N_CHIPS * r, cs)


HBM_SPEC = pl.BlockSpec(memory_space=pltpu.HBM)


def _gather_shards(shards):
    n = len(shards)

    def body(*refs):
        w_refs, out_refs, wb_refs = refs[:n], refs[n:2 * n], refs[2 * n:3 * n]
        send_sems, recv_sems, pass_send_sems, pass_recv_sems, local_sems = refs[3 * n:]
        x, y, c = _place()
        me = 2 * x + y
        sibling = (x, y, 1 - c)

        def halves(ref):
            half = ref.shape[-2] // 2
            return (pl.ds(pl.multiple_of(c * half, 16), half), pl.ds(pl.multiple_of((1 - c) * half, 16), half))
        for w_ref, wb_ref in zip(w_refs, wb_refs):
            rows = w_ref.shape[0]
            chunk = min(rows, 128)

            def cast(i, carry, w_ref=w_ref, wb_ref=wb_ref, chunk=chunk):
                r0 = pl.multiple_of(i * chunk, chunk)
                wb_ref[pl.ds(r0, chunk), :] = _bf(w_ref[pl.ds(r0, chunk), :])
                return carry

            lax.fori_loop(0, rows // chunk, cast, 0)
        sends, locals_ = [], []
        for a, (wb_ref, out_ref) in enumerate(zip(wb_refs, out_refs)):
            mine = pltpu.make_async_copy(wb_ref, out_ref.at[me], local_sems.at[a])
            mine.start()
            locals_.append(mine)
            mine_rows, _ = halves(wb_ref)
            for k, (px, py) in enumerate(_other_chips(x, y)):
                cp = pltpu.make_async_remote_copy(src_ref=wb_ref.at[mine_rows, :], dst_ref=out_ref.at[me, mine_rows, :],
                                                  send_sem=send_sems.at[k, a], recv_sem=recv_sems.at[k, a],
                                                  device_id=(px, py, c), device_id_type=MESH_ID)
                cp.start()
                sends.append(cp)
        for a, (wb_ref, out_ref) in enumerate(zip(wb_refs, out_refs)):
            mine_rows, _ = halves(wb_ref)
            for k, (px, py) in enumerate(_other_chips(x, y)):
                landed = out_ref.at[2 * px + py, mine_rows, :]
                pltpu.make_async_remote_copy(src_ref=wb_ref.at[mine_rows, :], dst_ref=landed, send_sem=send_sems.at[k, a],
                                             recv_sem=recv_sems.at[k, a], device_id=(px, py, c),
                                             device_id_type=MESH_ID).wait_recv()
                cp = pltpu.make_async_remote_copy(src_ref=landed, dst_ref=landed, send_sem=pass_send_sems.at[k, a],
                                                  recv_sem=pass_recv_sems.at[k, a], device_id=sibling, device_id_type=MESH_ID)
                cp.start()
                sends.append(cp)
        for a, (wb_ref, out_ref) in enumerate(zip(wb_refs, out_refs)):
            _, their_rows = halves(wb_ref)
            for k, (px, py) in enumerate(_other_chips(x, y)):
                passed = out_ref.at[2 * px + py, their_rows, :]
                pltpu.make_async_remote_copy(src_ref=passed, dst_ref=passed, send_sem=pass_send_sems.at[k, a],
                                             recv_sem=pass_recv_sems.at[k, a], device_id=sibling,
                                             device_id_type=MESH_ID).wait_recv()
        for cp in sends:
            cp.wait_send()
        for cp in locals_:
            cp.wait()

    return pl.pallas_call(
        body, name="gather_weights",
        in_specs=[pl.BlockSpec(memory_space=pltpu.VMEM)] * n,
        out_specs=[HBM_SPEC] * n,
        out_shape=[jax.ShapeDtypeStruct((N_CHIPS,) + s.shape, BF16) for s in shards],
        scratch_shapes=[pltpu.VMEM(s.shape, BF16) for s in shards]
        + [pltpu.SemaphoreType.DMA((3, n))] * 4 + [pltpu.SemaphoreType.DMA((n,))],
        compiler_params=pltpu.CompilerParams(vmem_limit_bytes=VMEM_LIMIT),
    )(*shards)


def _cast_bf16_list(arrays):
    def body(*refs):
        for a_ref, o_ref in zip(refs[:len(arrays)], refs[len(arrays):]):
            o_ref[...] = _bf(a_ref[...])

    specs = [pl.BlockSpec((a.shape[0] // 4, a.shape[1]), lambda i: (i, 0)) for a in arrays]
    return pl.pallas_call(
        body, name="cast_shards", grid=(4,), in_specs=specs, out_specs=specs,
        out_shape=[jax.ShapeDtypeStruct(a.shape, BF16) for a in arrays],
        compiler_params=_cparams(("parallel",)),
    )(*arrays)


def _rs_to_sibling(g4):
    n = len(g4)

    def body(*refs):
        g_refs, out_refs = refs[:n], refs[n:2 * n]
        send_sems, recv_sems = refs[2 * n:]
        x, y, c = _place()
        copies = []
        for a, (g_ref, out_ref) in enumerate(zip(g_refs, out_refs)):
            half = g_ref.shape[1] // 2
            theirs = pl.ds(pl.multiple_of((1 - c) * half, 8), half)
            copies.append(pltpu.make_async_remote_copy(src_ref=g_ref.at[:, theirs, :], dst_ref=out_ref, send_sem=send_sems.at[a],
                                                       recv_sem=recv_sems.at[a], device_id=(x, y, 1 - c),
                                                       device_id_type=MESH_ID))
        for cp in copies:
            cp.start()
        for cp in copies:
            cp.wait()

    return pl.pallas_call(
        body, name="rs_sibling", in_specs=[HBM_SPEC] * n, out_specs=[HBM_SPEC] * n,
        out_shape=[jax.ShapeDtypeStruct((N_CHIPS, g.shape[1] // 2, g.shape[2]), F32) for g in g4],
        scratch_shapes=[pltpu.SemaphoreType.DMA((n,)), pltpu.SemaphoreType.DMA((n,))],
    )(*g4)


def _rs_add_sibling(g4, got, wire_dtypes):
    n = len(g4)
    narrow = [a for a in range(n) if wire_dtypes[a] != F32]

    def body(c_ref, *refs):
        outs = refs[2 * n:3 * n]
        wires = dict(zip(narrow, refs[3 * n:]))
        for a, (g_ref, r_ref, o_ref) in enumerate(zip(refs[:n], refs[n:2 * n], outs)):
            s = g_ref[...] + r_ref[...]
            o_ref[...] = s
            if a in wires:
                wires[a][...] = s.astype(wires[a].dtype)

    blk = lambda r: (1, r.shape[1], r.shape[2])
    plain = lambda r: pl.BlockSpec(blk(r), lambda j, c_ref: (j, 0, 0))
    grid_spec = pltpu.PrefetchScalarGridSpec(
        num_scalar_prefetch=1, grid=(N_CHIPS,),
        in_specs=[pl.BlockSpec(blk(r), lambda j, c_ref: (j, c_ref[0], 0)) for r in got] + [plain(r) for r in got],
        out_specs=[plain(r) for r in got] + [plain(got[a]) for a in narrow])
    res = pl.pallas_call(
        body, name="rs_add_sibling", grid_spec=grid_spec,
        out_shape=[jax.ShapeDtypeStruct(r.shape, F32) for r in got]
        + [jax.ShapeDtypeStruct(got[a].shape, wire_dtypes[a]) for a in narrow],
        compiler_params=_cparams(("parallel",)),
    )(lax.axis_index("c").astype(jnp.int32).reshape(1), *g4, *got)
    chipsum = list(res[:n])
    wire = list(chipsum)
    for a, w in zip(narrow, res[n:]):
        wire[a] = w
    return chipsum, wire


RELATION_XOR = (2, 1, 3)


def _rs_add_chips(chipsum, parts):
    n = len(parts)

    def body(me_ref, *refs):
        me = me_ref[0]
        for s_ref, p_ref, o_ref in zip(refs[:n], refs[n:2 * n], refs[2 * n:]):
            own = s_ref[0]
            total = None
            for k in range(N_CHIPS):
                theirs = p_ref[jnp.maximum(jnp.bitwise_xor(me, k) - 1, 0)].astype(F32)
                term = jnp.where(me == k, own, theirs)
                total = term if total is None else total + term
            o_ref[...] = total

    grid_spec = pltpu.PrefetchScalarGridSpec(
        num_scalar_prefetch=1, grid=(2,),
        in_specs=[pl.BlockSpec((1, p.shape[1] // 2, p.shape[2]), lambda i, me_ref: (me_ref[0], i, 0)) for p in parts]
        + [pl.BlockSpec((3, p.shape[1] // 2, p.shape[2]), lambda i, me_ref: (0, i, 0)) for p in parts],
        out_specs=[pl.BlockSpec((p.shape[1] // 2, p.shape[2]), lambda i, me_ref: (i, 0)) for p in parts])
    me = (2 * lax.axis_index("x") + lax.axis_index("y")).astype(jnp.int32).reshape(1)
    return pl.pallas_call(
        body, name="rs_add_chips", grid_spec=grid_spec,
        out_shape=[jax.ShapeDtypeStruct(p.shape[1:], F32) for p in parts],
        compiler_params=_cparams(("parallel",)),
    )(me, *chipsum, *parts)


def _rs_swap_halves(halves):
    n = len(halves)

    def body(*refs):
        h_refs, out_refs = refs[:n], refs[n:2 * n]
        send_sems, recv_sems = refs[2 * n:]
        x, y, c = _place()
        copies = [pltpu.make_async_remote_copy(src_ref=h_ref, dst_ref=out_ref, send_sem=send_sems.at[a], recv_sem=recv_sems.at[a],
                                               device_id=(x, y, 1 - c), device_id_type=MESH_ID)
                  for a, (h_ref, out_ref) in enumerate(zip(h_refs, out_refs))]
        for cp in copies:
            cp.start()
        for cp in copies:
            cp.wait()

    return pl.pallas_call(
        body, name="rs_swap_halves", in_specs=[HBM_SPEC] * n, out_specs=[HBM_SPEC] * n,
        out_shape=[jax.ShapeDtypeStruct(h.shape, F32) for h in halves],
        scratch_shapes=[pltpu.SemaphoreType.DMA((n,)), pltpu.SemaphoreType.DMA((n,))],
    )(*halves)


def _adamw_list(ws, g_mine, g_theirs, ms, vs):
    n = len(ws)

    def body(c_ref, *refs):
        w_refs, gm_refs, gt_refs, m_refs, v_refs = (refs[k * n:(k + 1) * n] for k in range(5))
        g_refs, d_refs, nm_refs, nv_refs = (refs[k * n:(k + 1) * n] for k in range(5, 9))
        mine = (pl.program_id(0) // ADAM_STEPS_PER_HALF) == c_ref[0]
        for a in range(n):
            gv = jnp.where(mine, gm_refs[a][...], gt_refs[a][...])
            g_refs[a][...] = gv
            m_new = ADAM_B1 * m_refs[a][...] + (1.0 - ADAM_B1) * gv
            v_new = ADAM_B2 * v_refs[a][...] + (1.0 - ADAM_B2) * (gv * gv)
            m_hat = m_new / (1.0 - ADAM_B1 ** ADAM_STEP)
            v_hat = v_new / (1.0 - ADAM_B2 ** ADAM_STEP)
            d_refs[a][...] = -ADAM_LR * (m_hat / (jnp.sqrt(v_hat) + ADAM_EPS) + ADAM_WD * w_refs[a][...])
            nm_refs[a][...] = m_new
            nv_refs[a][...] = v_new

    steps = 2 * ADAM_STEPS_PER_HALF
    whole = [pl.BlockSpec((w.shape[0] // steps, w.shape[1]), lambda i, c_ref: (i, 0)) for w in ws]
    half = [pl.BlockSpec((w.shape[0] // steps, w.shape[1]), lambda i, c_ref: (i % ADAM_STEPS_PER_HALF, 0)) for w in ws]
    shapes = [jax.ShapeDtypeStruct(w.shape, F32) for w in ws]
    grid_spec = pltpu.PrefetchScalarGridSpec(num_scalar_prefetch=1, grid=(steps,),
                                             in_specs=whole + half + half + whole + whole, out_specs=whole * 4)
    res = pl.pallas_call(
        body, name="adamw", grid_spec=grid_spec, out_shape=shapes * 4,
        compiler_params=_cparams(("parallel",)),
    )(lax.axis_index("c").astype(jnp.int32).reshape(1), *ws, *g_mine, *g_theirs, *ms, *vs)
    return res[:n], res[n:2 * n], res[2 * n:3 * n], res[3 * n:]


WEIGHT_NAMES = ("w_in", "w_mem_kv", "q_a_gain", "w_q_b", "kv_a_gain", "w_kv_b", "w_branch_mla", "w_branch_sb",
                "w_branch_mem", "w_merge_gate", "b_merge_gate", "w_out", "ln_gain", "ln_bias")
SMALL_NAMES = tuple(n for n, _ in SMALL_SIZES)


def kernel(x, mem, w_in, w_mem_kv, q_a_gain, w_q_b, kv_a_gain, w_kv_b, w_branch_mla, w_branch_sb, w_branch_mem, w_merge_gate, b_merge_gate, w_out, ln_gain, ln_bias, loss_target, m_w_in, m_w_mem_kv, m_q_a_gain, m_w_q_b, m_kv_a_gain, m_w_kv_b, m_w_branch_mla, m_w_branch_sb, m_w_branch_mem, m_w_merge_gate, m_b_merge_gate, m_w_out, m_ln_gain, m_ln_bias, v_w_in, v_w_mem_kv, v_q_a_gain, v_w_q_b, v_kv_a_gain, v_w_kv_b, v_w_branch_mla, v_w_branch_sb, v_w_branch_mem, v_w_merge_gate, v_b_merge_gate, v_w_out, v_ln_gain, v_ln_bias):
    weights = dict(zip(WEIGHT_NAMES, (w_in, w_mem_kv, q_a_gain, w_q_b, kv_a_gain, w_kv_b, w_branch_mla, w_branch_sb,
                                      w_branch_mem, w_merge_gate, b_merge_gate, w_out, ln_gain, ln_bias)))
    mom1 = dict(zip(WEIGHT_NAMES, (m_w_in, m_w_mem_kv, m_q_a_gain, m_w_q_b, m_kv_a_gain, m_w_kv_b, m_w_branch_mla,
                                   m_w_branch_sb, m_w_branch_mem, m_w_merge_gate, m_b_merge_gate, m_w_out, m_ln_gain,
                                   m_ln_bias)))
    mom2 = dict(zip(WEIGHT_NAMES, (v_w_in, v_w_mem_kv, v_q_a_gain, v_w_q_b, v_kv_a_gain, v_w_kv_b, v_w_branch_mla,
                                   v_w_branch_sb, v_w_branch_mem, v_w_merge_gate, v_b_merge_gate, v_w_out, v_ln_gain,
                                   v_ln_bias)))
    def as_list(d):
        return [d[n][0] for n in BIG_NAMES] + [_pack_small({n: d[n] for n in SMALL_NAMES})]

    w_list, m_list, v_list = as_list(weights), as_list(mom1), as_list(mom2)

    gathered = _gather_shards([weights[n][0] for n in LATE_NAMES])
    first_w = {n: _join_chips(n, g) for n, g in zip(LATE_NAMES, gathered)}
    rest_shards = _cast_bf16_list([weights[n][0] for n in EARLY_NAMES])
    small = {n: weights[n] for n in SMALL_NAMES}

    seq = x.shape[1]
    _, grad_x, late_mine, _, early_mine = _local_step(
        x[0], mem[0], loss_target[0], first_w, small, tq=min(1024, seq), tq_sb_bwd=min(1024, seq), tk=256, tk_mla=512,
        t_row=256, t_wg=min(2048, seq), rest_shards=rest_shards)
    by_name = dict(zip(EARLY_NAMES + LATE_NAMES + ("small",), list(early_mine) + list(late_mine)))
    mine = [by_name[n] for n in BIG_NAMES + ("small",)]
    theirs = _rs_swap_halves(mine)
    g_list, d_list, nm_list, nv_list = _adamw_list(w_list, mine, theirs, m_list, v_list)

    loss = g_list[-1].reshape(-1)[SMALL_TOTAL]
    outs = [loss, grad_x[None]]
    for arrays in (g_list, d_list, nm_list, nv_list):
        big = dict(zip(BIG_NAMES, arrays[:-1]))
        sm = _unpack_small(arrays[-1])
        outs.extend(big[n][None] if n in big else sm[n] for n in WEIGHT_NAMES)
    return tuple(outs)
```

```python
import functools
import math

import numpy as np
import jax
import jax.numpy as jnp
from jax import lax
from jax.experimental import pallas as pl
from jax.experimental.pallas import tpu as pltpu

F32 = jnp.float32
BF16 = jnp.bfloat16
MESH_ID = pl.DeviceIdType.MESH

D_MODEL = 1024
MEM_LEN = 256
MLA_HEADS = 8
MLA_NOPE = 64
MLA_ROPE = 32
MLA_V = 64
MLA_Q_LORA = 256
MLA_KV_LORA = 128
SB_HEADS = 8
SB_HEAD_DIM = 64
MEM_HEADS = 4
MEM_HEAD_DIM = 128
ROPE_BASE = 10000.0
RMS_EPS = 1e-6
LN_EPS = 1e-5
DEEPNORM_ALPHA = 2.0 ** 0.25
MLA_SCALE = 1.0 / math.sqrt(MLA_NOPE + MLA_ROPE)
SB_SCALE = 1.0 / math.sqrt(SB_HEAD_DIM)
MEM_SCALE = 1.0 / math.sqrt(MEM_HEAD_DIM)

ADAM_LR = 0.001
ADAM_B1 = 0.9
ADAM_B2 = 0.999
ADAM_EPS = 1e-08
ADAM_WD = 0.01
ADAM_STEP = 10

LANES = 128
HALF = 64
N_CHIPS = 4
PACK_COLS = 1024
VMEM_LIMIT = 56 * 1024 * 1024

IN_WIDTH_P = 4096
BLK_LAT, BLK_GATE_A, BLK_QB, BLK_KB, BLK_VB, BLK_GATE_B, BLK_QM, BLK_GATE_M = range(8)
P32_POS = {blk: n for n, blk in enumerate((BLK_LAT, BLK_GATE_A, BLK_GATE_B, BLK_GATE_M))}
PBF_POS = {blk: n for n, blk in enumerate((BLK_QB, BLK_KB, BLK_VB, BLK_QM))}
N_MERGE = 3 * D_MODEL

BIG_NAMES = ("w_in", "w_mem_kv", "w_q_b", "w_kv_b", "w_branch_mla", "w_branch_sb", "w_branch_mem", "w_merge_gate", "w_out")
COL_SHARDED = ("w_in", "w_q_b", "w_kv_b", "w_branch_mla", "w_branch_sb", "w_branch_mem", "w_merge_gate")
SMALL_SIZES = (("q_a_gain", 256), ("kv_a_gain", 128), ("b_merge_gate", 3072), ("ln_gain", 1024), ("ln_bias", 1024))
SMALL_TOTAL = sum(s for _, s in SMALL_SIZES)


def _cparams(sem=None):
    return pltpu.CompilerParams(dimension_semantics=sem, vmem_limit_bytes=VMEM_LIMIT)


def _dot(a, b):
    return jnp.dot(a, b, preferred_element_type=F32)


def _dot_nt(a, b):
    return lax.dot_general(a, b, (((1,), (1,)), ((), ())), preferred_element_type=F32)


def _dot_tn(a, b):
    return lax.dot_general(a, b, (((0,), (0,)), ((), ())), preferred_element_type=F32)


def _bf(x):
    return x.astype(BF16)


def _dot_cols(a, w_ref):
    return jnp.concatenate([_dot(a, w_ref[j]) for j in range(w_ref.shape[0])], axis=1)


def _dot_nt_cols(a, w_ref):
    cs = w_ref.shape[2]
    out = None
    for j in range(w_ref.shape[0]):
        term = _dot_nt(a[:, j * cs:(j + 1) * cs], w_ref[j])
        out = term if out is None else out + term
    return out


def _sigmoid(x):
    return 1.0 / (1.0 + jnp.exp(-x))


def _matmul(a, b, *, mode, tm, tn, tk, out_dtypes, name, add=None, add_scale=1.0, by_column_block=False):
    if mode == "nn":
        (m, k), n = a.shape, b.shape[1]
        a_spec = pl.BlockSpec((tm, tk), lambda i, j, kk: (i, kk))
        b_spec = pl.BlockSpec((tk, tn), lambda i, j, kk: (kk, j))
        dot = _dot
    elif mode == "nt":
        (m, k), n = a.shape, b.shape[0]
        a_spec = pl.BlockSpec((tm, tk), lambda i, j, kk: (i, kk))
        b_spec = pl.BlockSpec((tn, tk), lambda i, j, kk: (j, kk))
        dot = _dot_nt
    else:
        (k, m), n = a.shape, b.shape[1]
        a_spec = pl.BlockSpec((tk, tm), lambda i, j, kk: (kk, i))
        b_spec = pl.BlockSpec((tk, tn), lambda i, j, kk: (kk, j))
        dot = _dot_tn
    assert m % tm == 0 and n % tn == 0 and k % tk == 0, (name, m, n, k)
    nk = k // tk
    n_out = len(out_dtypes)
    has_add = add is not None

    def body(*refs):
        a_ref, b_ref = refs[0], refs[1]
        add_ref = refs[2] if has_add else None
        outs = refs[2 + has_add: 2 + has_add + n_out]
        acc = refs[-1]
        kk = pl.program_id(2)

        @pl.when(kk == 0)
        def _():
            acc[...] = jnp.zeros_like(acc)

        acc[...] += dot(_bf(a_ref[...]), _bf(b_ref[...]))

        @pl.when(kk == nk - 1)
        def _():
            r = acc[...]
            if has_add:
                r = r + add_scale * add_ref[...]
            for o in outs:
                o[...] = r.astype(o.dtype)

    in_specs = [a_spec, b_spec]
    args = [a, b]
    if has_add:
        in_specs.append(pl.BlockSpec((tm, tn), lambda i, j, kk: (i, j)))
        args.append(add)
    if by_column_block:
        out_spec = pl.BlockSpec((None, tm, tn), lambda i, j, kk: (j, i, 0))
        out_dims = (n // tn, m, tn)
    else:
        out_spec = pl.BlockSpec((tm, tn), lambda i, j, kk: (i, j))
        out_dims = (m, n)
    res = pl.pallas_call(
        body, name=name, grid=(m // tm, n // tn, nk),
        in_specs=in_specs, out_specs=[out_spec] * n_out,
        out_shape=[jax.ShapeDtypeStruct(out_dims, dt) for dt in out_dtypes],
        scratch_shapes=[pltpu.VMEM((tm, tn), F32)],
        compiler_params=_cparams(("parallel", "parallel", "arbitrary")),
    )(*args)
    return res


def _rope_tables(seq):
    half = MLA_ROPE // 2
    freqs = ROPE_BASE ** (-jnp.arange(half, dtype=F32) / half)
    ang = jnp.arange(seq, dtype=jnp.int32).astype(F32)[:, None] * freqs[None, :]
    cos, sin = jnp.cos(ang), jnp.sin(ang)
    z = lambda w: jnp.zeros((seq, w), F32)
    c_q = jnp.concatenate([jnp.ones((seq, MLA_NOPE), F32), cos, cos, z(32)], axis=1)
    c_k = jnp.concatenate([z(MLA_NOPE), cos, cos, z(32)], axis=1)
    s_lo = jnp.concatenate([z(MLA_NOPE), -sin, z(half), z(32)], axis=1)
    s_hi = jnp.concatenate([z(MLA_NOPE), z(half), sin, z(32)], axis=1)
    return c_q, c_k, s_lo, s_hi


def _rope_fwd(x, c, s_lo, s_hi):
    return x * c + pltpu.roll(x, LANES - 16, 1) * s_lo + pltpu.roll(x, 16, 1) * s_hi


def _rope_bwd(d, c, s_lo, s_hi):
    return d * c - pltpu.roll(d, 16, 1) * s_hi - pltpu.roll(d, LANES - 16, 1) * s_lo


def _rms_fwd(x, g):
    r = lax.rsqrt(jnp.mean(x * x, axis=-1, keepdims=True) + RMS_EPS)
    xn = x * r
    return xn * g, xn, r


def _mla_prep(p32, gq, gkv, wqb, wkvb, tabs, *, t):
    seq = p32.shape[0]

    def body(lat_ref, gq_ref, gkv_ref, wqb_ref, wkvb_ref, cq_ref, ck_ref, slo_ref, shi_ref, q_ref, k_ref, v_ref):
        lat = lat_ref[...]
        slo, shi = slo_ref[...], shi_ref[...]
        nq, _, _ = _rms_fwd(lat[:, 0:MLA_Q_LORA], gq_ref[...])
        qa = _dot(_bf(nq), wqb_ref[...])
        cq = cq_ref[...]
        for h in range(MLA_HEADS):
            blk = qa[:, h * LANES:(h + 1) * LANES]
            q_ref[:, h * LANES:(h + 1) * LANES] = _bf(_rope_fwd(blk, cq, slo, shi))
        nkv, _, _ = _rms_fwd(lat[:, MLA_Q_LORA:MLA_Q_LORA + MLA_KV_LORA], gkv_ref[...])
        kv = _dot(_bf(nkv), wkvb_ref[...])
        kpe = _rope_fwd(lat[:, 384:512], ck_ref[...], slo, shi)
        for h in range(MLA_HEADS):
            k_ref[:, h * LANES:(h + 1) * LANES] = _bf(kv[:, h * LANES:(h + 1) * LANES] + kpe)
        v_ref[...] = _bf(kv[:, MLA_HEADS * LANES:])

    row = lambda w: pl.BlockSpec((t, w), lambda i: (i, 0))
    full = lambda shp: pl.BlockSpec(shp, lambda i: (0, 0))
    return pl.pallas_call(
        body, name="mla_prep", grid=(seq // t,),
        in_specs=[row(512), full((1, MLA_Q_LORA)), full((1, MLA_KV_LORA)), full(wqb.shape), full(wkvb.shape),
                  row(LANES), row(LANES), row(LANES), row(LANES)],
        out_specs=[row(1024), row(1024), row(512)],
        out_shape=[jax.ShapeDtypeStruct((seq, 1024), BF16), jax.ShapeDtypeStruct((seq, 1024), BF16),
                   jax.ShapeDtypeStruct((seq, 512), BF16)],
        compiler_params=_cparams(("parallel",)),
    )(p32, gq, gkv, wqb, wkvb, *tabs)


def _mla_post(p32, dq, dk, dv, gq, gkv, wqb, wkvb, tabs, *, t):
    seq = p32.shape[0]

    def body(lat_ref, dq_ref, dk_ref, dv_ref, gq_ref, gkv_ref, wqb_ref, wkvb_ref, cq_ref, ck_ref, slo_ref, shi_ref,
             dlat_ref, dwqb_ref, dwkvb_ref, dgq_ref, dgkv_ref):
        @pl.when(pl.program_id(0) == 0)
        def _():
            dwqb_ref[...] = jnp.zeros_like(dwqb_ref)
            dwkvb_ref[...] = jnp.zeros_like(dwkvb_ref)
            dgq_ref[...] = jnp.zeros_like(dgq_ref)
            dgkv_ref[...] = jnp.zeros_like(dgkv_ref)

        lat = lat_ref[...]
        slo, shi = slo_ref[...], shi_ref[...]
        cq = cq_ref[...]
        gq_v, gkv_v = gq_ref[...], gkv_ref[...]
        nq, xq, rq = _rms_fwd(lat[:, 0:MLA_Q_LORA], gq_v)
        nkv, xkv, rkv = _rms_fwd(lat[:, MLA_Q_LORA:MLA_Q_LORA + MLA_KV_LORA], gkv_v)

        dqa = jnp.concatenate(
            [_rope_bwd(dq_ref[:, h * LANES:(h + 1) * LANES], cq, slo, shi) for h in range(MLA_HEADS)], axis=1)
        dqa_b = _bf(dqa)
        dwqb_ref[...] += _dot_tn(_bf(nq), dqa_b)
        dnq = _dot_nt(dqa_b, wqb_ref[...])
        dgq_ref[...] += jnp.sum(dnq * xq, axis=0, keepdims=True)
        dxn = dnq * gq_v
        dcq = rq * (dxn - xq * jnp.mean(dxn * xq, axis=-1, keepdims=True))

        dkf = dk_ref[...]
        dkv_b = _bf(jnp.concatenate([dkf, dv_ref[...]], axis=1))
        dwkvb_ref[...] += _dot_tn(_bf(nkv), dkv_b)
        dnkv = _dot_nt(dkv_b, wkvb_ref[...])
        dgkv_ref[...] += jnp.sum(dnkv * xkv, axis=0, keepdims=True)
        dxn = dnkv * gkv_v
        dckv = rkv * (dxn - xkv * jnp.mean(dxn * xkv, axis=-1, keepdims=True))

        dkpe = dkf[:, 0:LANES]
        for h in range(1, MLA_HEADS):
            dkpe = dkpe + dkf[:, h * LANES:(h + 1) * LANES]
        dkr = _rope_bwd(dkpe, ck_ref[...], slo, shi)
        dlat_ref[...] = _bf(jnp.concatenate([dcq, dckv, dkr], axis=1))

    row = lambda w: pl.BlockSpec((t, w), lambda i: (i, 0))
    full = lambda shp: pl.BlockSpec(shp, lambda i: (0, 0))
    return pl.pallas_call(
        body, name="mla_post", grid=(seq // t,),
        in_specs=[row(512), row(1024), row(1024), row(512), full((1, MLA_Q_LORA)), full((1, MLA_KV_LORA)),
                  full(wqb.shape), full(wkvb.shape), row(LANES), row(LANES), row(LANES), row(LANES)],
        out_specs=[row(512), full(wqb.shape), full(wkvb.shape), full((1, MLA_Q_LORA)), full((1, MLA_KV_LORA))],
        out_shape=[jax.ShapeDtypeStruct((seq, 512), BF16), jax.ShapeDtypeStruct(wqb.shape, F32),
                   jax.ShapeDtypeStruct(wkvb.shape, F32), jax.ShapeDtypeStruct((1, MLA_Q_LORA), F32),
                   jax.ShapeDtypeStruct((1, MLA_KV_LORA), F32)],
        compiler_params=_cparams(("arbitrary",)),
    )(p32, dq, dk, dv, gq, gkv, wqb, wkvb, *tabs)


def _split_bf16(x):
    hi = _bf(x)
    return hi, _bf(x - hi.astype(F32))


def _tri_sum(x, u):
    hi, lo = _split_bf16(x)
    return _dot(hi, u) + _dot(lo, u)


def _softplus(z):
    return jnp.maximum(z, 0.0) + jnp.log(1.0 + jnp.exp(-jnp.abs(z)))


def _head_queries(q, left):
    zero = jnp.zeros_like(q)
    return jnp.where(left, q, zero) * SB_SCALE, jnp.where(left, zero, q) * SB_SCALE


ROW_GROUP = 128
SB_BWD_CHAINS_IN_FLIGHT = 16
ANY_HBM = pl.BlockSpec(memory_space=pltpu.HBM)


class _Exchange:
    def __init__(self, send, landing):
        self.send, self.landing = send, landing

    def start(self):
        self.send.start()

    def wait(self):
        self.landing.wait_recv()
        self.send.wait_send()


class _Rider:
    def __init__(self, operands, out_shapes, sem_shapes, copies):
        self.operands, self.out_shapes, self.sem_shapes, self.copies = list(operands), list(out_shapes), list(sem_shapes), copies


def _call_with_rider(body, rider, *, name, grid, in_specs, out_specs, out_shape, args, semantics, scratch=()):
    scratch = list(scratch)
    if rider is None:
        return pl.pallas_call(body, name=name, grid=grid, in_specs=in_specs, out_specs=out_specs, out_shape=out_shape,
                              scratch_shapes=scratch, compiler_params=_cparams(semantics))(*args)
    n_in, n_out, n_rin, n_rout = len(in_specs), len(out_specs), len(rider.operands), len(rider.out_shapes)

    def full_body(*refs):
        ins, r_ins = refs[:n_in], refs[n_in:n_in + n_rin]
        outs = refs[n_in + n_rin:n_in + n_rin + n_out]
        r_outs = refs[n_in + n_rin + n_out:n_in + n_rin + n_out + n_rout]
        rest = refs[n_in + n_rin + n_out + n_rout:]
        own_scratch, sems = rest[:len(scratch)], rest[len(scratch):]
        first, last = None, None
        for axis, size in enumerate(grid):
            at_start, at_end = pl.program_id(axis) == 0, pl.program_id(axis) == size - 1
            first = at_start if first is None else first & at_start
            last = at_end if last is None else last & at_end

        @pl.when(first)
        def _():
            for cp in rider.copies(r_ins, r_outs, sems):
                cp.start()

        body(*ins, *outs, *own_scratch)

        @pl.when(last)
        def _():
            for cp in rider.copies(r_ins, r_outs, sems):
                cp.wait()

    return pl.pallas_call(
        full_body, name=name, grid=grid, in_specs=list(in_specs) + [ANY_HBM] * n_rin,
        out_specs=list(out_specs) + [ANY_HBM] * n_rout, out_shape=list(out_shape) + rider.out_shapes,
        scratch_shapes=scratch + rider.sem_shapes, compiler_params=_cparams(("arbitrary",) * len(grid)),
    )(*args, *rider.operands)


def _chains(tq):
    return [(h, g) for g in range(tq // ROW_GROUP) for h in range(2)]


def _chain_pattern(g, m, tk, strict):
    r_lo, r_hi = g * ROW_GROUP, (g + 1) * ROW_GROUP - 1
    c_lo, c_hi = m * tk, (m + 1) * tk - 1
    if (c_lo >= r_hi) if strict else (c_lo > r_hi):
        return None
    if (c_hi < r_lo) if strict else (c_hi <= r_lo):
        return True
    rr = lax.broadcasted_iota(jnp.int32, (ROW_GROUP, tk), 0) + r_lo
    cc = lax.broadcasted_iota(jnp.int32, (ROW_GROUP, tk), 1) + c_lo
    return (cc < rr) if strict else (cc <= rr)


def _masked(x, pat, fill=0.0):
    return x if pat is True else jnp.where(pat, x, fill)


def _rows(g):
    return slice(g * ROW_GROUP, (g + 1) * ROW_GROUP)


def _tri_matrix(tk, cmp):
    rr = lax.broadcasted_iota(jnp.int32, (tk, tk), 0)
    cc = lax.broadcasted_iota(jnp.int32, (tk, tk), 1)
    return cmp(rr, cc).astype(BF16)


def _mla_attn_fwd(qp, kp, vp, *, tq, tk, tk_diag, rider=None):
    seq = qp.shape[0]
    neg = float(np.finfo(np.float32).min)
    chains = _chains(tq)

    def body(q_ref, k_ref, v_ref, o_ref, lse_ref):
        i = pl.program_id(1)
        left = lax.broadcasted_iota(jnp.int32, (tq, LANES), 1) < HALF
        qs = [q_ref[_rows(g), h * LANES:(h + 1) * LANES] for h, g in chains]

        def block(start, carry, m, tk):
            v = v_ref[pl.ds(start, tk), :]
            pats = [True if m is None else _chain_pattern(g, m, tk, False) for _, g in chains]
            live = [n for n, p in enumerate(pats) if p is not None]
            ss = {n: _dot_nt(qs[n], k_ref[pl.ds(start, tk), chains[n][0] * LANES:(chains[n][0] + 1) * LANES]) for n in live}
            new = list(carry)
            for n in live:
                m_old, l_old, acc = carry[n]
                s = _masked(ss[n] * MLA_SCALE, pats[n], neg)
                m_new = jnp.maximum(m_old, jnp.max(s, axis=-1, keepdims=True))
                a = jnp.exp(m_old - m_new)
                p = jnp.exp(s - m_new)
                new[n] = (m_new, a * l_old + jnp.sum(p, axis=-1, keepdims=True), a * acc + _dot(_bf(p), v))
            return tuple(new)

        init = (jnp.full((ROW_GROUP, 1), -1e30, F32), jnp.zeros((ROW_GROUP, 1), F32), jnp.zeros((ROW_GROUP, LANES), F32))
        def two_blocks(j, c):
            c = block(pl.multiple_of(2 * j * tk, tk), c, None, tk)
            return block(pl.multiple_of((2 * j + 1) * tk, tk), c, None, tk)

        carry = lax.fori_loop(0, i * (tq // tk) // 2, two_blocks, (init,) * len(chains))
        for m in range(tq // tk_diag):
            carry = block(pl.multiple_of(i * tq + m * tk_diag, tk_diag), carry, m, tk_diag)
        per_head = []
        for h in range(2):
            mine = [carry[n] for n, (ch, _) in enumerate(chains) if ch == h]
            per_head.append((jnp.concatenate([acc / l for _, l, acc in mine], axis=0),
                             jnp.concatenate([mm + jnp.log(l) for mm, l, _ in mine], axis=0)))
        o_ref[...] = jnp.where(left, per_head[0][0], per_head[1][0])
        lse_ref[...] = jnp.where(left, per_head[0][1], per_head[1][1])

    return _call_with_rider(
        body, rider, name="mla_fwd", grid=(MLA_HEADS // 2, seq // tq),
        in_specs=[pl.BlockSpec((tq, 2 * LANES), lambda p, i: (i, p)), pl.BlockSpec((seq, 2 * LANES), lambda p, i: (0, p)),
                  pl.BlockSpec((seq, LANES), lambda p, i: (0, p))],
        out_specs=[pl.BlockSpec((tq, LANES), lambda p, i: (i, p)), pl.BlockSpec((tq, LANES), lambda p, i: (i, p))],
        out_shape=[jax.ShapeDtypeStruct((seq, 512), F32), jax.ShapeDtypeStruct((seq, 512), F32)],
        args=(qp, kp, vp), semantics=("parallel", "parallel"))


def _mla_attn_bwd(qp, kp, vp, o, lse, do, *, tq, tk, tk_diag, rider=None):
    seq = qp.shape[0]
    chains = _chains(tq)

    def body(q_ref, k_ref, v_ref, o_ref, lse_ref, do_ref, dq_ref, dk_ref, dv_ref, qt_ref, dot_ref):
        i = pl.program_id(1)

        @pl.when(i == 0)
        def _():
            dk_ref[...] = jnp.zeros_like(dk_ref)
            dv_ref[...] = jnp.zeros_like(dv_ref)

        left = lax.broadcasted_iota(jnp.int32, (tq, LANES), 1) < HALF
        do_f = do_ref[...]
        prod = do_f * o_ref[...]
        lse_v = lse_ref[...]
        do_heads = (_bf(jnp.where(left, do_f, 0.0)), _bf(jnp.where(left, 0.0, do_f)))
        delta_heads = (jnp.sum(jnp.where(left, prod, 0.0), axis=-1, keepdims=True),
                       jnp.sum(jnp.where(left, 0.0, prod), axis=-1, keepdims=True))
        qs = [q_ref[_rows(g), h * LANES:(h + 1) * LANES] for h, g in chains]
        dos = [do_heads[h][_rows(g)] for h, g in chains]
        deltas = [delta_heads[h][_rows(g)] for h, g in chains]
        lses = [lse_v[_rows(g), h * HALF:h * HALF + 1] for h, g in chains]
        for h in range(2):
            qt_ref[h] = q_ref[:, h * LANES:(h + 1) * LANES].T
            dot_ref[h] = do_heads[h].T
        q_t = [qt_ref.at[h] for h in range(2)]
        do_t = [dot_ref.at[h] for h in range(2)]

        def block(start, carry, m, tk):
            v = v_ref[pl.ds(start, tk), :]
            pats = [True if m is None else _chain_pattern(g, m, tk, False) for _, g in chains]
            live = [n for n, p in enumerate(pats) if p is not None]
            ks = [k_ref[pl.ds(start, tk), h * LANES:(h + 1) * LANES] for h in range(2)]
            ss = {n: _dot_nt(qs[n], ks[chains[n][0]]) for n in live}
            dps = {n: _dot_nt(dos[n], v) for n in live}
            new = list(carry)
            ps, dss = {}, {}
            for n in live:
                p = _masked(jnp.exp(ss[n] * MLA_SCALE - lses[n]), pats[n])
                ps[n] = _bf(p)
                dss[n] = _bf(p * (dps[n] - deltas[n]) * MLA_SCALE)
                new[n] = carry[n] + _dot(dss[n], ks[chains[n][0]])
            dv_t, dk_t = None, []
            for h in range(2):
                mine = [n for n in live if chains[n][0] == h]
                first_row = chains[mine[0]][1] * ROW_GROUP
                ds_cat = jnp.concatenate([dss[n] for n in mine], axis=0)
                p_cat = jnp.concatenate([ps[n] for n in mine], axis=0)
                if first_row == 0:
                    q_rows_t, do_rows_t = q_t[h][...], do_t[h][...]
                else:
                    q_rows_t = q_ref[first_row:, h * LANES:(h + 1) * LANES].T
                    do_rows_t = do_heads[h][first_row:].T
                dk_t.append(_dot(q_rows_t, ds_cat))
                term = _dot(do_rows_t, p_cat)
                dv_t = term if dv_t is None else dv_t + term
            back = jnp.concatenate(dk_t + [dv_t], axis=0).T
            dk_ref[pl.ds(start, tk), :] += back[:, :2 * LANES]
            dv_ref[pl.ds(start, tk), :] += back[:, 2 * LANES:]
            return tuple(new)

        zero = jnp.zeros((ROW_GROUP, LANES), F32)
        carry = lax.fori_loop(0, i * (tq // tk), lambda j, c: block(pl.multiple_of(j * tk, tk), c, None, tk),
                              (zero,) * len(chains))
        for m in range(tq // tk_diag):
            carry = block(pl.multiple_of(i * tq + m * tk_diag, tk_diag), carry, m, tk_diag)
        for n, (h, g) in enumerate(chains):
            dq_ref[_rows(g), h * LANES:(h + 1) * LANES] = carry[n]

    two_t = pl.BlockSpec((tq, 2 * LANES), lambda p, i: (i, p))
    two_s = pl.BlockSpec((seq, 2 * LANES), lambda p, i: (0, p))
    pair_t = pl.BlockSpec((tq, LANES), lambda p, i: (i, p))
    pair_s = pl.BlockSpec((seq, LANES), lambda p, i: (0, p))
    return _call_with_rider(
        body, rider, name="mla_bwd", grid=(MLA_HEADS // 2, seq // tq),
        in_specs=[two_t, two_s, pair_s, pair_t, pair_t, pair_t],
        out_specs=[two_t, two_s, pair_s],
        out_shape=[jax.ShapeDtypeStruct((seq, 1024), F32), jax.ShapeDtypeStruct((seq, 1024), F32),
                   jax.ShapeDtypeStruct((seq, 512), F32)],
        args=(qp, kp, vp, o, lse, do), semantics=("parallel", "arbitrary"),
        scratch=[pltpu.VMEM((2, LANES, tq), BF16), pltpu.VMEM((2, LANES, tq), BF16)])


def _sb_attn_fwd(pbf, *, tq, tk):
    seq = pbf.shape[0]
    nd = tq // tk
    qb, kb, vb = PBF_POS[BLK_QB] * 4, PBF_POS[BLK_KB] * 4, PBF_POS[BLK_VB] * 4
    chains = _chains(tq)

    def body(q_ref, k_ref, v_ref, o_ref, tot_ref):
        i = pl.program_id(1)
        u_later = _tri_matrix(tk, lambda r, c: r > c)
        left = lax.broadcasted_iota(jnp.int32, (tq, LANES), 1) < HALF
        q_heads = _head_queries(q_ref[...], left)
        qs = [q_heads[h][_rows(g)] for h, g in chains]

        def block(j, carry, m):
            start = pl.multiple_of(j * tk, tk)
            k = k_ref[pl.ds(start, tk), :]
            v = v_ref[pl.ds(start, tk), :]
            pats = [True if m is None else _chain_pattern(g, m, tk, True) for _, g in chains]
            live = [n for n, p in enumerate(pats) if p is not None]
            zs = {n: _dot_nt(qs[n], k) for n in live}
            raws = {n: _softplus(zs[n]) for n in live}
            sps = {n: _masked(raws[n], pats[n]) for n in live}
            laters = {n: _tri_sum(sps[n], u_later) for n in live}
            new = list(carry)
            for n in live:
                c, acc = carry[n]
                a = _masked(jnp.exp(zs[n] - raws[n] - laters[n] - c), pats[n])
                new[n] = (c + laters[n][:, 0:1] + sps[n][:, 0:1], acc + _dot(_bf(a), v))
            return tuple(new)

        init = (jnp.zeros((ROW_GROUP, 1), F32), jnp.zeros((ROW_GROUP, LANES), F32))
        carry = (init,) * len(chains)
        for m in reversed(range(nd)):
            carry = block(i * nd + m, carry, m)
        per_trip = 4 if nd % 4 == 0 else 2

        def trip(jj, cr):
            for u in range(per_trip):
                cr = block(i * nd - 1 - (per_trip * jj + u), cr, None)
            return cr

        carry = lax.fori_loop(0, i * nd // per_trip, trip, carry)
        per_head = []
        for h in range(2):
            mine = [carry[n] for n, (ch, _) in enumerate(chains) if ch == h]
            per_head.append((jnp.concatenate([acc for _, acc in mine], axis=0), jnp.concatenate([c for c, _ in mine], axis=0)))
        o_ref[...] = jnp.where(left, per_head[0][0], per_head[1][0])
        tot_ref[...] = jnp.where(left, per_head[0][1], per_head[1][1])

    pair_t = pl.BlockSpec((tq, LANES), lambda p, i: (i, p))
    return pl.pallas_call(
        body, name="sb_fwd", grid=(SB_HEADS // 2, seq // tq),
        in_specs=[pl.BlockSpec((tq, LANES), lambda p, i: (i, qb + p)), pl.BlockSpec((seq, LANES), lambda p, i: (0, kb + p)),
                  pl.BlockSpec((seq, LANES), lambda p, i: (0, vb + p))],
        out_specs=[pair_t, pair_t],
        out_shape=[jax.ShapeDtypeStruct((seq, 512), F32), jax.ShapeDtypeStruct((seq, 512), F32)],
        compiler_params=_cparams(("parallel", "parallel")),
    )(pbf, pbf, pbf)


def _sb_attn_bwd(pbf, tot, do, *, tq, tk, rider=None):
    seq = pbf.shape[0]
    nd = tq // tk
    qb, kb, vb = PBF_POS[BLK_QB] * 4, PBF_POS[BLK_KB] * 4, PBF_POS[BLK_VB] * 4
    chains = _chains(tq)
    group = SB_BWD_CHAINS_IN_FLIGHT

    def body(q_ref, k_ref, v_ref, tot_ref, do_ref, dq_ref, dk_ref, dv_ref, qt_ref, dot_ref):
        i = pl.program_id(1)

        @pl.when(i == 0)
        def _():
            dk_ref[...] = jnp.zeros_like(dk_ref)
            dv_ref[...] = jnp.zeros_like(dv_ref)

        u_upto = _tri_matrix(tk, lambda r, c: r <= c)
        u_below = _tri_matrix(tk, lambda r, c: r < c)
        left = lax.broadcasted_iota(jnp.int32, (tq, LANES), 1) < HALF
        q_heads = _head_queries(q_ref[...], left)
        do_f = do_ref[...]
        do_heads = (_bf(jnp.where(left, do_f, 0.0)), _bf(jnp.where(left, 0.0, do_f)))
        tot_v = tot_ref[...]
        qs = [q_heads[h][_rows(g)] for h, g in chains]
        dos = [do_heads[h][_rows(g)] for h, g in chains]
        totals = [tot_v[_rows(g), h * HALF:h * HALF + 1] for h, g in chains]
        qt_ref[...] = jnp.concatenate(qs, axis=0).T
        dot_ref[...] = jnp.concatenate(dos, axis=0).T

        def block(j, carry, m):
            start = pl.multiple_of(j * tk, tk)
            k = k_ref[pl.ds(start, tk), :]
            v = v_ref[pl.ds(start, tk), :]
            pats = [True if m is None else _chain_pattern(g, m, tk, True) for _, g in chains]
            all_live = [n for n, p in enumerate(pats) if p is not None]
            new = list(carry)
            for g0 in range(0, len(all_live), group):
                live = all_live[g0:g0 + group]
                zs = {n: _dot_nt(qs[n], k) for n in live}
                das = {n: _dot_nt(dos[n], v) for n in live}
                raws = {n: _softplus(zs[n]) for n in live}
                sps = {n: _masked(raws[n], pats[n]) for n in live}
                uptos = {n: _tri_sum(sps[n], u_upto) for n in live}
                lbs, a_s, gs = {}, {}, {}
                for n in live:
                    lbs[n] = zs[n] - raws[n]
                    a = _masked(jnp.exp(lbs[n] - (totals[n] - carry[n][0] - uptos[n])), pats[n])
                    a_s[n] = _bf(a)
                    gs[n] = das[n] * a
                belows = {n: _dot(_bf(gs[n]), u_below) for n in live}
                dzs = {}
                for n in live:
                    sp_before, g_before, dq_acc = carry[n]
                    beta = jnp.exp(lbs[n])
                    dz = _masked(gs[n] * (1.0 - beta) - (g_before + belows[n]) * beta, pats[n])
                    dzs[n] = _bf(dz)
                    new[n] = (sp_before + uptos[n][:, tk - 1:tk], g_before + belows[n][:, tk - 1:tk] + gs[n][:, tk - 1:tk],
                              dq_acc + _dot(dzs[n], k))
                dz_cat = jnp.concatenate([dzs[n] for n in live], axis=0)
                a_cat = jnp.concatenate([a_s[n] for n in live], axis=0)
                if len(live) == len(chains):
                    q_rows_t, do_rows_t = qt_ref[...], dot_ref[...]
                else:
                    q_rows_t = jnp.concatenate([qs[n] for n in live], axis=0).T
                    do_rows_t = jnp.concatenate([dos[n] for n in live], axis=0).T
                both = jnp.concatenate([_dot(q_rows_t, dz_cat), _dot(do_rows_t, a_cat)], axis=0).T
                dk_ref[pl.ds(start, tk), :] += both[:, :LANES]
                dv_ref[pl.ds(start, tk), :] += both[:, LANES:]
            return tuple(new)

        zero = jnp.zeros((ROW_GROUP, 1), F32)
        init = (zero, zero, jnp.zeros((ROW_GROUP, LANES), F32))
        per_trip = 4 if nd % 4 == 0 else 2

        def trip(j, cr):
            for u in range(per_trip):
                cr = block(per_trip * j + u, cr, None)
            return cr

        carry = lax.fori_loop(0, i * nd // per_trip, trip, (init,) * len(chains))
        for m in range(nd):
            carry = block(i * nd + m, carry, m)
        per_head = [jnp.concatenate([carry[n][2] for n, (ch, _) in enumerate(chains) if ch == h], axis=0) for h in range(2)]
        dq_ref[...] = jnp.where(left, per_head[0], per_head[1]) * SB_SCALE

    pair_t = pl.BlockSpec((tq, LANES), lambda p, i: (i, p))
    pair_s = pl.BlockSpec((seq, LANES), lambda p, i: (0, p))
    return _call_with_rider(
        body, rider, name="sb_bwd", grid=(SB_HEADS // 2, seq // tq),
        in_specs=[pl.BlockSpec((tq, LANES), lambda p, i: (i, qb + p)), pl.BlockSpec((seq, LANES), lambda p, i: (0, kb + p)),
                  pl.BlockSpec((seq, LANES), lambda p, i: (0, vb + p)), pair_t, pair_t],
        out_specs=[pair_t, pair_s, pair_s],
        out_shape=[jax.ShapeDtypeStruct((seq, 512), F32)] * 3,
        args=(pbf, pbf, pbf, tot, do), semantics=("parallel", "arbitrary"),
        scratch=[pltpu.VMEM((LANES, 2 * tq), BF16), pltpu.VMEM((LANES, 2 * tq), BF16)])


def _mem_probs(s):
    e = jnp.exp(s - jnp.max(s, axis=-1, keepdims=True))
    return e / jnp.sum(e, axis=-1, keepdims=True)


def _head_lanes(h):
    return slice(h * LANES, (h + 1) * LANES)


def _mem_fwd(pbf, mkv, *, t):
    seq = pbf.shape[0]

    def body(q_ref, kv_ref, o_ref):
        ss = [_dot_nt(q_ref[:, _head_lanes(h)], kv_ref[:, _head_lanes(h)]) * MEM_SCALE for h in range(MEM_HEADS)]
        ps = [_bf(_mem_probs(s)) for s in ss]
        for h, p in enumerate(ps):
            o_ref[:, _head_lanes(h)] = _dot(p, kv_ref[:, _head_lanes(MEM_HEADS + h)])

    return pl.pallas_call(
        body, name="mem_fwd", grid=(seq // t,),
        in_specs=[pl.BlockSpec((t, 512), lambda i: (i, PBF_POS[BLK_QM])), pl.BlockSpec((MEM_LEN, 1024), lambda i: (0, 0))],
        out_specs=pl.BlockSpec((t, 512), lambda i: (i, 0)),
        out_shape=jax.ShapeDtypeStruct((seq, 512), F32),
        compiler_params=_cparams(("parallel",)),
    )(pbf, mkv)


def _mem_bwd(pbf, mkv, do, *, t):
    seq = pbf.shape[0]

    def body(q_ref, kv_ref, do_ref, dq_ref, dkv_ref):
        @pl.when(pl.program_id(0) == 0)
        def _():
            dkv_ref[...] = jnp.zeros_like(dkv_ref)

        heads = range(MEM_HEADS)
        qs = [q_ref[:, _head_lanes(h)] for h in heads]
        ks = [kv_ref[:, _head_lanes(h)] for h in heads]
        dos = [_bf(do_ref[:, _head_lanes(h)]) for h in heads]
        ss = [_dot_nt(qs[h], ks[h]) * MEM_SCALE for h in heads]
        dps = [_dot_nt(dos[h], kv_ref[:, _head_lanes(MEM_HEADS + h)]) for h in heads]
        ps = [_mem_probs(s) for s in ss]
        dss = [_bf(ps[h] * (dps[h] - jnp.sum(dps[h] * ps[h], axis=-1, keepdims=True)) * MEM_SCALE) for h in heads]
        for h in heads:
            dq_ref[:, _head_lanes(h)] = _dot(dss[h], ks[h])
        for h in heads:
            dkv_ref[:, _head_lanes(h)] += _dot_tn(dss[h], qs[h])
            dkv_ref[:, _head_lanes(MEM_HEADS + h)] += _dot_tn(_bf(ps[h]), dos[h])

    return pl.pallas_call(
        body, name="mem_bwd", grid=(seq // t,),
        in_specs=[pl.BlockSpec((t, 512), lambda i: (i, PBF_POS[BLK_QM])), pl.BlockSpec((MEM_LEN, 1024), lambda i: (0, 0)),
                  pl.BlockSpec((t, 512), lambda i: (i, 0))],
        out_specs=[pl.BlockSpec((t, 512), lambda i: (i, 0)), pl.BlockSpec((MEM_LEN, 1024), lambda i: (0, 0))],
        out_shape=[jax.ShapeDtypeStruct((seq, 512), F32), jax.ShapeDtypeStruct((MEM_LEN, 1024), F32)],
        compiler_params=_cparams(("arbitrary",)),
    )(pbf, mkv, do)


def _mid(x, tgt, o_a, o_b, o_m, p32, wmg, bmg, wba, wbb, wbm, wout, ln_g, ln_b, *, t):
    seq = x.shape[0]
    inv_d = 1.0 / D_MODEL

    def body(x_ref, t_ref, oa_ref, ob_ref, om_ref, ga_ref, gb_ref, gm_ref, wmg_ref, bmg_ref, wba_ref, wbb_ref,
             wbm_ref, wout_ref, lg_ref, lb_ref,
             du_ref, mrg_ref, dgp_ref, ha_ref, hb_ref, hm_ref, dya_ref, dyb_ref, dym_ref, doa_ref, dob_ref, dom_ref,
             dga_ref, dgb_ref, dgm_ref, dgain_ref, dbias_ref, dbmg_ref, loss_ref):
        @pl.when(pl.program_id(0) == 0)
        def _():
            dgain_ref[...] = jnp.zeros_like(dgain_ref)
            dbias_ref[...] = jnp.zeros_like(dbias_ref)
            dbmg_ref[...] = jnp.zeros_like(dbmg_ref)
            loss_ref[...] = jnp.zeros_like(loss_ref)

        xv = x_ref[...]
        gate = _sigmoid(_dot_cols(_bf(xv), wmg_ref) + bmg_ref[...])

        branches = []
        merged = None
        for b, (o_ref, g_ref, w_ref, h_ref) in enumerate(((oa_ref, ga_ref, wba_ref, ha_ref), (ob_ref, gb_ref, wbb_ref, hb_ref),
                                                         (om_ref, gm_ref, wbm_ref, hm_ref))):
            o, gt = o_ref[...], g_ref[...]
            sg = _sigmoid(gt)
            silu = gt * sg
            h = _bf(o * silu)
            h_ref[...] = h
            y = _dot_cols(h, w_ref)
            g_b = gate[:, b * D_MODEL:(b + 1) * D_MODEL]
            term = g_b * y
            merged = term if merged is None else merged + term
            branches.append((o, gt, sg, silu, y, g_b))
        mrg_b = _bf(merged)
        mrg_ref[...] = mrg_b

        u = DEEPNORM_ALPHA * xv + _dot(mrg_b, wout_ref[...])
        mu = jnp.mean(u, axis=-1, keepdims=True)
        uc = u - mu
        rstd = lax.rsqrt(jnp.mean(uc * uc, axis=-1, keepdims=True) + LN_EPS)
        xhat = uc * rstd
        lg = lg_ref[...]
        y_out = xhat * lg + lb_ref[...]
        err = y_out - t_ref[...]
        loss_ref[...] += 0.5 * jnp.sum(jnp.mean(err * err, axis=-1, keepdims=True), axis=0, keepdims=True)
        dy = err * inv_d
        dgain_ref[...] += jnp.sum(dy * xhat, axis=0, keepdims=True)
        dbias_ref[...] += jnp.sum(dy, axis=0, keepdims=True)
        dxh = dy * lg
        du = rstd * (dxh - jnp.mean(dxh, axis=-1, keepdims=True) - xhat * jnp.mean(dxh * xhat, axis=-1, keepdims=True))
        du_ref[...] = du

        dmerged = _dot_nt(_bf(du), wout_ref[...])
        outs = ((dya_ref, doa_ref, dga_ref, wba_ref), (dyb_ref, dob_ref, dgb_ref, wbb_ref), (dym_ref, dom_ref, dgm_ref, wbm_ref))
        dgp = []
        for (o, gt, sg, silu, y, g_b), (dy_ref, do_ref, dg_ref, w_ref) in zip(branches, outs):
            dyb = _bf(dmerged * g_b)
            dy_ref[...] = dyb
            dgp.append(dmerged * y * g_b * (1.0 - g_b))
            dh = _dot_nt_cols(dyb, w_ref)
            do_ref[...] = dh * silu
            dg_ref[...] = _bf(dh * o * (sg * (1.0 + gt * (1.0 - sg))))
        dgp = jnp.concatenate(dgp, axis=1)
        dgp_ref[...] = _bf(dgp)
        dbmg_ref[...] += jnp.sum(dgp, axis=0, keepdims=True)

    row = lambda w: pl.BlockSpec((t, w), lambda i: (i, 0))
    pblk = lambda c: pl.BlockSpec((t, 512), lambda i: (i, c))
    full = lambda shp: pl.BlockSpec(shp, lambda i: (0,) * len(shp))
    sds = jax.ShapeDtypeStruct
    return pl.pallas_call(
        body, name="mid", grid=(seq // t,),
        in_specs=[row(1024), row(1024), row(512), row(512), row(512), pblk(P32_POS[BLK_GATE_A]), pblk(P32_POS[BLK_GATE_B]), pblk(P32_POS[BLK_GATE_M]),
                  full(wmg.shape), full((1, N_MERGE)), full(wba.shape), full(wbb.shape), full(wbm.shape), full(wout.shape),
                  full((1, D_MODEL)), full((1, D_MODEL))],
        out_specs=[row(1024), row(1024), row(N_MERGE), row(512), row(512), row(512), row(1024), row(1024), row(1024),
                   row(512), row(512), row(512), row(512), row(512), row(512),
                   full((1, D_MODEL)), full((1, D_MODEL)), full((1, N_MERGE)), full((1, 1))],
        out_shape=[sds((seq, 1024), F32), sds((seq, 1024), BF16), sds((seq, N_MERGE), BF16),
                   sds((seq, 512), BF16), sds((seq, 512), BF16), sds((seq, 512), BF16),
                   sds((seq, 1024), BF16), sds((seq, 1024), BF16), sds((seq, 1024), BF16),
                   sds((seq, 512), F32), sds((seq, 512), F32), sds((seq, 512), F32),
                   sds((seq, 512), BF16), sds((seq, 512), BF16), sds((seq, 512), BF16),
                   sds((1, D_MODEL), F32), sds((1, D_MODEL), F32), sds((1, N_MERGE), F32), sds((1, 1), F32)],
        compiler_params=_cparams(("arbitrary",)),
    )(x, tgt, o_a, o_b, o_m, p32, p32, p32, wmg, bmg, wba, wbb, wbm, wout, ln_g, ln_b)


def _primed_weights(w):
    w_in = w["w_in"]
    zc = lambda n: jnp.zeros((D_MODEL, n), w_in.dtype)
    w_in_p = jnp.concatenate([w_in[:, 0:384], zc(64), w_in[:, 384:416], zc(32), w_in[:, 416:]], axis=1)
    wqb = jnp.pad(w["w_q_b"].reshape(MLA_Q_LORA, MLA_HEADS, 96), ((0, 0), (0, 0), (0, 32))).reshape(MLA_Q_LORA, 1024)
    kv3 = w["w_kv_b"].reshape(MLA_KV_LORA, MLA_HEADS, 128)
    wk = jnp.pad(kv3[:, :, :MLA_NOPE], ((0, 0), (0, 0), (0, 64))).reshape(MLA_KV_LORA, 1024)
    wv = kv3[:, :, MLA_NOPE:].reshape(MLA_KV_LORA, 512)
    return w_in_p, wqb, jnp.concatenate([wk, wv], axis=1)


PROJ_BLK = 512


def _proj_in(x, w_in_p, *, tm):
    seq = x.shape[0]

    def body(x_ref, w_ref, p32_ref, pbf_ref):
        xb = _bf(x_ref[...])
        for blk in range(IN_WIDTH_P // PROJ_BLK):
            r = _dot(xb, w_ref[:, blk * PROJ_BLK:(blk + 1) * PROJ_BLK])
            if blk in P32_POS:
                p32_ref[:, P32_POS[blk] * PROJ_BLK:(P32_POS[blk] + 1) * PROJ_BLK] = r
            else:
                pbf_ref[:, PBF_POS[blk] * PROJ_BLK:(PBF_POS[blk] + 1) * PROJ_BLK] = _bf(r)

    row = lambda w: pl.BlockSpec((tm, w), lambda i: (i, 0))
    return pl.pallas_call(
        body, name="proj_in", grid=(seq // tm,),
        in_specs=[row(D_MODEL), pl.BlockSpec(w_in_p.shape, lambda i: (0, 0))],
        out_specs=[row(len(P32_POS) * PROJ_BLK), row(len(PBF_POS) * PROJ_BLK)],
        out_shape=[jax.ShapeDtypeStruct((seq, len(P32_POS) * PROJ_BLK), F32),
                   jax.ShapeDtypeStruct((seq, len(PBF_POS) * PROJ_BLK), BF16)],
        compiler_params=_cparams(("parallel",)),
    )(x, w_in_p)


def _grad_x(du, dgpre, wmg, d_proj, w_in_p, *, tm, rider=None):
    seq = du.shape[0]
    n_pieces = len(d_proj)

    def body(du_ref, dg_ref, wmg_ref, *rest):
        piece_refs, win_ref, out_ref = rest[:n_pieces], rest[n_pieces], rest[n_pieces + 1]
        d_p = jnp.concatenate([_bf(p_ref[...]) for p_ref in piece_refs], axis=1)
        out_ref[...] = (DEEPNORM_ALPHA * du_ref[...] + _dot_nt_cols(dg_ref[...], wmg_ref)) + _dot_nt(d_p, win_ref[...])

    row = lambda w: pl.BlockSpec((tm, w), lambda i: (i, 0))
    whole = lambda a: pl.BlockSpec(a.shape, lambda i: (0,) * a.ndim)
    return _call_with_rider(
        body, rider, name="grad_x", grid=(seq // tm,),
        in_specs=[row(D_MODEL), row(N_MERGE), whole(wmg)] + [row(PROJ_BLK) for _ in d_proj] + [whole(w_in_p)],
        out_specs=[row(D_MODEL)], out_shape=[jax.ShapeDtypeStruct((seq, D_MODEL), F32)],
        args=(du, dgpre, wmg, *d_proj, w_in_p), semantics=("parallel",))


def _grad_w_in(x, d_proj, *, tk):
    seq = x.shape[0]
    n_pieces = len(d_proj)
    nk = seq // tk

    def body(x_ref, *rest):
        piece_refs, out_ref, acc = rest[:n_pieces], rest[n_pieces], rest[n_pieces + 1]
        j, kk = pl.program_id(0), pl.program_id(1)

        @pl.when(kk == 0)
        def _():
            acc[...] = jnp.zeros_like(acc)

        xb = _bf(x_ref[...])
        for pair in range(n_pieces // 2):
            @pl.when(j == pair)
            def _(pair=pair):
                both = jnp.concatenate([_bf(piece_refs[2 * pair][...]), _bf(piece_refs[2 * pair + 1][...])], axis=1)
                acc[...] += _dot_tn(xb, both)

        @pl.when(kk == nk - 1)
        def _():
            out_ref[...] = acc[...]

    def piece_spec(s):
        return pl.BlockSpec((tk, PROJ_BLK), lambda j, kk: (jnp.where(j == s // 2, kk, 0), 0))

    return pl.pallas_call(
        body, name="grad_w_in", grid=(n_pieces // 2, nk),
        in_specs=[pl.BlockSpec((tk, D_MODEL), lambda j, kk: (kk, 0))] + [piece_spec(s) for s in range(n_pieces)],
        out_specs=pl.BlockSpec((D_MODEL, 2 * PROJ_BLK), lambda j, kk: (0, j)),
        out_shape=jax.ShapeDtypeStruct((D_MODEL, n_pieces * PROJ_BLK), F32),
        scratch_shapes=[pltpu.VMEM((D_MODEL, 2 * PROJ_BLK), F32)],
        compiler_params=_cparams(("parallel", "arbitrary")),
    )(x, *d_proj)


EARLY_NAMES = ("w_mem_kv", "w_branch_mla", "w_branch_sb", "w_branch_mem", "w_merge_gate", "w_out")
LATE_NAMES = ("w_in", "w_q_b", "w_kv_b")


def _remote(src, dst, send_sem, recv_sem, device):
    return pltpu.make_async_remote_copy(src_ref=src, dst_ref=dst, send_sem=send_sem, recv_sem=recv_sem, device_id=device,
                                        device_id_type=MESH_ID)


def _gather_rider(shards):
    n = len(shards)

    def copies(src_refs, out_refs, sems):
        send_sems, recv_sems, local_sems = sems
        x, y, c = _place()
        me = 2 * x + y
        out = []
        for a, (s, o) in enumerate(zip(src_refs, out_refs)):
            out.append(pltpu.make_async_copy(s, o.at[me], local_sems.at[a]))
            for k, (px, py) in enumerate(_other_chips(x, y)):
                out.append(_Exchange(_remote(s, o.at[me], send_sems.at[k, a], recv_sems.at[k, a], (px, py, c)),
                                     _remote(s, o.at[2 * px + py], send_sems.at[k, a], recv_sems.at[k, a], (px, py, c))))
        return out

    return _Rider(shards, [jax.ShapeDtypeStruct((N_CHIPS,) + s.shape, s.dtype) for s in shards],
                  [pltpu.SemaphoreType.DMA((3, n)), pltpu.SemaphoreType.DMA((3, n)), pltpu.SemaphoreType.DMA((n,))], copies)


def _sibling_rider(g4):
    n = len(g4)

    def copies(g_refs, out_refs, sems):
        send_sems, recv_sems = sems
        x, y, c = _place()
        out = []
        for a, (g, o) in enumerate(zip(g_refs, out_refs)):
            half = g.shape[1] // 2
            theirs = pl.ds(pl.multiple_of((1 - c) * half, 8), half)
            cp = _remote(g.at[:, theirs, :], o, send_sems.at[a], recv_sems.at[a], (x, y, 1 - c))
            out.append(_Exchange(cp, cp))
        return out

    return _Rider(g4, [jax.ShapeDtypeStruct((N_CHIPS, g.shape[1] // 2, g.shape[2]), g.dtype) for g in g4],
                  [pltpu.SemaphoreType.DMA((n,)), pltpu.SemaphoreType.DMA((n,))], copies)


def _chips_rider(wire):
    n = len(wire)

    def copies(s_refs, out_refs, sems):
        send_sems, recv_sems = sems
        x, y, c = _place()
        out = []
        for a, (s, o) in enumerate(zip(s_refs, out_refs)):
            for k, (px, py) in enumerate(_other_chips(x, y)):
                cp = _remote(s.at[2 * px + py], o.at[RELATION_XOR[k] - 1], send_sems.at[k, a], recv_sems.at[k, a], (px, py, c))
                out.append(_Exchange(cp, cp))
        return out

    return _Rider(wire, [jax.ShapeDtypeStruct((3,) + s.shape[1:], s.dtype) for s in wire],
                  [pltpu.SemaphoreType.DMA((3, n)), pltpu.SemaphoreType.DMA((3, n))], copies)


def _local_step(x, mem, tgt, w, small, *, tq, tq_sb_bwd, tk, tk_mla, t_row, t_wg, rest_shards=None):
    seq = x.shape[0]
    on_mesh = rest_shards is not None
    w_in_p, wqb, wkvb = _primed_weights(w)
    tabs = _rope_tables(seq)

    p32, pbf = _proj_in(x, w_in_p, tm=256)
    t_light = min(2 * t_row, seq)
    qp, kp, vp = _mla_prep(p32, small["q_a_gain"], small["kv_a_gain"], wqb, wkvb, tabs, t=t_light)
    res = _mla_attn_fwd(qp, kp, vp, tq=tq, tk=tk_mla, tk_diag=tk, rider=_gather_rider(rest_shards) if on_mesh else None)
    o_a, lse = res[0], res[1]
    if on_mesh:
        w = dict(w, **{n: g if n in COL_SHARDED else _join_chips(n, g) for n, g in zip(EARLY_NAMES, res[2:])})
    else:
        w = dict(w, **{n: _split_by_chip(n, w[n]) for n in EARLY_NAMES if n in COL_SHARDED})
    wmg, wout = w["w_merge_gate"], w["w_out"]
    wba, wbb, wbm = w["w_branch_mla"], w["w_branch_sb"], w["w_branch_mem"]
    o_b, keep_total = _sb_attn_fwd(pbf, tq=tq, tk=tk)
    (mkv,) = _matmul(mem, w["w_mem_kv"], mode="nn", tm=MEM_LEN, tn=512, tk=D_MODEL, out_dtypes=(BF16,), name="mem_kv")
    o_m = _mem_fwd(pbf, mkv, t=t_light)

    (du, merged, dgpre, h_a, h_b, h_m, dy_a, dy_b, dy_m, do_a, do_b, do_m, dgate_a, dgate_b, dgate_m,
     d_ln_g, d_ln_b, d_bmg, loss) = _mid(x, tgt, o_a, o_b, o_m, p32, wmg, small["b_merge_gate"], wba, wbb, wbm, wout,
                                         small["ln_gain"], small["ln_bias"], t=t_row)

    wg = functools.partial(_matmul, mode="tn", tm=512, out_dtypes=(F32,))
    shard = (lambda width: dict(tn=width // N_CHIPS, by_column_block=True)) if on_mesh else (lambda width: dict(tn=1024))
    dq_m, dmkv = _mem_bwd(pbf, mkv, do_m, t=t_light)
    early = {"w_mem_kv": wg(mem, dmkv, tk=MEM_LEN, tn=1024, name="grad_w_mem_kv")[0],
             "w_branch_mla": wg(h_a, dy_a, tk=t_wg, name="grad_w_branch_a", **shard(D_MODEL))[0],
             "w_branch_sb": wg(h_b, dy_b, tk=t_wg, name="grad_w_branch_b", **shard(D_MODEL))[0],
             "w_branch_mem": wg(h_m, dy_m, tk=t_wg, name="grad_w_branch_m", **shard(D_MODEL))[0],
             "w_merge_gate": wg(x, dgpre, tk=t_wg, tm=D_MODEL, name="grad_w_merge_gate", **shard(N_MERGE))[0],
             "w_out": wg(merged, du, tk=t_wg, tn=1024, name="grad_w_out")[0]}

    if on_mesh:
        g4 = [early[n] if early[n].ndim == 3 else _split_by_chip(n, early[n]) for n in EARLY_NAMES]
        res = _mla_attn_bwd(qp, kp, vp, o_a, lse, do_a, tq=tq, tk=tk_mla, tk_diag=tk, rider=_sibling_rider(g4))
        (dqp, dkp, dvp), got = res[:3], res[3:]
        chipsum, wire = _rs_add_sibling(g4, got, [BF16] * len(g4))
        res = _sb_attn_bwd(pbf, keep_total, do_b, tq=tq_sb_bwd, tk=tk, rider=_chips_rider(wire))
        (dq_b, dk_b, dv_b), parts = res[:3], res[3:]
        early = _rs_add_chips(chipsum, parts)
    else:
        dqp, dkp, dvp = _mla_attn_bwd(qp, kp, vp, o_a, lse, do_a, tq=tq, tk=tk_mla, tk_diag=tk)
        dq_b, dk_b, dv_b = _sb_attn_bwd(pbf, keep_total, do_b, tq=tq_sb_bwd, tk=tk)
    dlat, d_wqb, d_wkvb, d_gq, d_gkv = _mla_post(p32, dqp, dkp, dvp, small["q_a_gain"], small["kv_a_gain"], wqb, wkvb, tabs,
                                                 t=t_light)

    d_proj = [dlat, dgate_a, dq_b, dk_b, dv_b, dgate_b, dq_m, dgate_m]
    d_winp = _grad_w_in(x, d_proj, tk=min(1024, seq))

    d_win = jnp.concatenate([d_winp[:, 0:384], d_winp[:, 448:480], d_winp[:, 512:]], axis=1)
    d_wq = d_wqb.reshape(MLA_Q_LORA, MLA_HEADS, 128)[:, :, :96].reshape(MLA_Q_LORA, 768)
    d_wk = d_wkvb[:, :1024].reshape(MLA_KV_LORA, MLA_HEADS, 128)[:, :, :MLA_NOPE]
    d_wv = d_wkvb[:, 1024:].reshape(MLA_KV_LORA, MLA_HEADS, MLA_V)
    d_wkv = jnp.concatenate([d_wk, d_wv], axis=2).reshape(MLA_KV_LORA, 1024)
    late = {"w_in": d_win, "w_q_b": d_wq, "w_kv_b": d_wkv}
    small_grads = {"q_a_gain": d_gq, "kv_a_gain": d_gkv, "b_merge_gate": d_bmg, "ln_gain": d_ln_g, "ln_bias": d_ln_b}
    if not on_mesh:
        (grad_x,) = _grad_x(du, dgpre, wmg, d_proj, w_in_p, tm=256)
        return loss[0, 0], grad_x, late, small_grads, early

    g4 = [_split_by_chip(n, late[n]) for n in LATE_NAMES]
    g4.append(jnp.broadcast_to(_pack_small(small_grads, scalar=loss)[None], (N_CHIPS, SMALL_ROWS, PACK_COLS)))
    got = _rs_to_sibling(g4)
    chipsum, wire = _rs_add_sibling(g4, got, [BF16] * len(LATE_NAMES) + [F32])
    res = _grad_x(du, dgpre, wmg, d_proj, w_in_p, tm=256, rider=_chips_rider(wire))
    late_mine = _rs_add_chips(chipsum, res[1:])
    return loss[0, 0], res[0], late_mine, None, early


def _place():
    x, y, c = lax.axis_index("x"), lax.axis_index("y"), lax.axis_index("c")
    return x, y, c


def _other_chips(x, y):
    return ((1 - x, y), (x, 1 - y), (1 - x, 1 - y))


SMALL_ROWS = 64
ADAM_STEPS_PER_HALF = 4


def _pack_small(d, scalar=None):
    parts = [d[n].reshape(-1) for n, _ in SMALL_SIZES] + ([] if scalar is None else [scalar.reshape(1)])
    flat = jnp.concatenate(parts)
    return jnp.pad(flat, (0, SMALL_ROWS * PACK_COLS - flat.shape[0])).reshape(SMALL_ROWS, PACK_COLS)


def _unpack_small(a):
    flat, out, c0 = a.reshape(-1), {}, 0
    for n, size in SMALL_SIZES:
        out[n] = flat[c0:c0 + size].reshape(1, size)
        c0 += size
    return out


def _split_by_chip(name, full):
    r, c = full.shape
    if name in COL_SHARDED:
        return full.reshape(r, N_CHIPS, c // N_CHIPS).transpose(1, 0, 2)
    return full.reshape(N_CHIPS, r // N_CHIPS, c)


def _join_chips(name, slots):
    _, r, cs = slots.shape
    if name in COL_SHARDED:
        return slots.transpose(1, 0, 2).reshape(r, N_CHIPS * cs)
    return slots.reshape(N_CHIPS * r, cs)


HBM_SPEC = pl.BlockSpec(memory_space=pltpu.HBM)


def _gather_shards(shards):
    n = len(shards)

    def body(*refs):
        w_refs, out_refs, wb_refs = refs[:n], refs[n:2 * n], refs[2 * n:3 * n]
        send_sems, recv_sems, pass_send_sems, pass_recv_sems, local_sems = refs[3 * n:]
        x, y, c = _place()
        me = 2 * x + y
        sibling = (x, y, 1 - c)

        def halves(ref):
            half = ref.shape[-2] // 2
            return (pl.ds(pl.multiple_of(c * half, 16), half), pl.ds(pl.multiple_of((1 - c) * half, 16), half))
        for w_ref, wb_ref in zip(w_refs, wb_refs):
            rows = w_ref.shape[0]
            chunk = min(rows, 128)

            def cast(i, carry, w_ref=w_ref, wb_ref=wb_ref, chunk=chunk):
                r0 = pl.multiple_of(i * chunk, chunk)
                wb_ref[pl.ds(r0, chunk), :] = _bf(w_ref[pl.ds(r0, chunk), :])
                return carry

            lax.fori_loop(0, rows // chunk, cast, 0)
        sends, locals_ = [], []
        for a, (wb_ref, out_ref) in enumerate(zip(wb_refs, out_refs)):
            mine = pltpu.make_async_copy(wb_ref, out_ref.at[me], local_sems.at[a])
            mine.start()
            locals_.append(mine)
            mine_rows, _ = halves(wb_ref)
            for k, (px, py) in enumerate(_other_chips(x, y)):
                cp = pltpu.make_async_remote_copy(src_ref=wb_ref.at[mine_rows, :], dst_ref=out_ref.at[me, mine_rows, :],
                                                  send_sem=send_sems.at[k, a], recv_sem=recv_sems.at[k, a],
                                                  device_id=(px, py, c), device_id_type=MESH_ID)
                cp.start()
                sends.append(cp)
        for a, (wb_ref, out_ref) in enumerate(zip(wb_refs, out_refs)):
            mine_rows, _ = halves(wb_ref)
            for k, (px, py) in enumerate(_other_chips(x, y)):
                landed = out_ref.at[2 * px + py, mine_rows, :]
                pltpu.make_async_remote_copy(src_ref=wb_ref.at[mine_rows, :], dst_ref=landed, send_sem=send_sems.at[k, a],
                                             recv_sem=recv_sems.at[k, a], device_id=(px, py, c),
                                             device_id_type=MESH_ID).wait_recv()
                cp = pltpu.make_async_remote_copy(src_ref=landed, dst_ref=landed, send_sem=pass_send_sems.at[k, a],
                                                  recv_sem=pass_recv_sems.at[k, a], device_id=sibling, device_id_type=MESH_ID)
                cp.start()
                sends.append(cp)
        for a, (wb_ref, out_ref) in enumerate(zip(wb_refs, out_refs)):
            _, their_rows = halves(wb_ref)
            for k, (px, py) in enumerate(_other_chips(x, y)):
                passed = out_ref.at[2 * px + py, their_rows, :]
                pltpu.make_async_remote_copy(src_ref=passed, dst_ref=passed, send_sem=pass_send_sems.at[k, a],
                                             recv_sem=pass_recv_sems.at[k, a], device_id=sibling,
                                             device_id_type=MESH_ID).wait_recv()
        for cp in sends:
            cp.wait_send()
        for cp in locals_:
            cp.wait()

    return pl.pallas_call(
        body, name="gather_weights",
        in_specs=[pl.BlockSpec(memory_space=pltpu.VMEM)] * n,
        out_specs=[HBM_SPEC] * n,
        out_shape=[jax.ShapeDtypeStruct((N_CHIPS,) + s.shape, BF16) for s in shards],
        scratch_shapes=[pltpu.VMEM(s.shape, BF16) for s in shards]
        + [pltpu.SemaphoreType.DMA((3, n))] * 4 + [pltpu.SemaphoreType.DMA((n,))],
        compiler_params=pltpu.CompilerParams(vmem_limit_bytes=VMEM_LIMIT),
    )(*shards)


def _cast_bf16_list(arrays):
    def body(*refs):
        for a_ref, o_ref in zip(refs[:len(arrays)], refs[len(arrays):]):
            o_ref[...] = _bf(a_ref[...])

    specs = [pl.BlockSpec((a.shape[0] // 4, a.shape[1]), lambda i: (i, 0)) for a in arrays]
    return pl.pallas_call(
        body, name="cast_shards", grid=(4,), in_specs=specs, out_specs=specs,
        out_shape=[jax.ShapeDtypeStruct(a.shape, BF16) for a in arrays],
        compiler_params=_cparams(("parallel",)),
    )(*arrays)


def _rs_to_sibling(g4):
    n = len(g4)

    def body(*refs):
        g_refs, out_refs = refs[:n], refs[n:2 * n]
        send_sems, recv_sems = refs[2 * n:]
        x, y, c = _place()
        copies = []
        for a, (g_ref, out_ref) in enumerate(zip(g_refs, out_refs)):
            half = g_ref.shape[1] // 2
            theirs = pl.ds(pl.multiple_of((1 - c) * half, 8), half)
            copies.append(pltpu.make_async_remote_copy(src_ref=g_ref.at[:, theirs, :], dst_ref=out_ref, send_sem=send_sems.at[a],
                                                       recv_sem=recv_sems.at[a], device_id=(x, y, 1 - c),
                                                       device_id_type=MESH_ID))
        for cp in copies:
            cp.start()
        for cp in copies:
            cp.wait()

    return pl.pallas_call(
        body, name="rs_sibling", in_specs=[HBM_SPEC] * n, out_specs=[HBM_SPEC] * n,
        out_shape=[jax.ShapeDtypeStruct((N_CHIPS, g.shape[1] // 2, g.shape[2]), F32) for g in g4],
        scratch_shapes=[pltpu.SemaphoreType.DMA((n,)), pltpu.SemaphoreType.DMA((n,))],
    )(*g4)


def _rs_add_sibling(g4, got, wire_dtypes):
    n = len(g4)
    narrow = [a for a in range(n) if wire_dtypes[a] != F32]

    def body(c_ref, *refs):
        outs = refs[2 * n:3 * n]
        wires = dict(zip(narrow, refs[3 * n:]))
        for a, (g_ref, r_ref, o_ref) in enumerate(zip(refs[:n], refs[n:2 * n], outs)):
            s = g_ref[...] + r_ref[...]
            o_ref[...] = s
            if a in wires:
                wires[a][...] = s.astype(wires[a].dtype)

    blk = lambda r: (1, r.shape[1], r.shape[2])
    plain = lambda r: pl.BlockSpec(blk(r), lambda j, c_ref: (j, 0, 0))
    grid_spec = pltpu.PrefetchScalarGridSpec(
        num_scalar_prefetch=1, grid=(N_CHIPS,),
        in_specs=[pl.BlockSpec(blk(r), lambda j, c_ref: (j, c_ref[0], 0)) for r in got] + [plain(r) for r in got],
        out_specs=[plain(r) for r in got] + [plain(got[a]) for a in narrow])
    res = pl.pallas_call(
        body, name="rs_add_sibling", grid_spec=grid_spec,
        out_shape=[jax.ShapeDtypeStruct(r.shape, F32) for r in got]
        + [jax.ShapeDtypeStruct(got[a].shape, wire_dtypes[a]) for a in narrow],
        compiler_params=_cparams(("parallel",)),
    )(lax.axis_index("c").astype(jnp.int32).reshape(1), *g4, *got)
    chipsum = list(res[:n])
    wire = list(chipsum)
    for a, w in zip(narrow, res[n:]):
        wire[a] = w
    return chipsum, wire


RELATION_XOR = (2, 1, 3)


def _rs_add_chips(chipsum, parts):
    n = len(parts)

    def body(me_ref, *refs):
        me = me_ref[0]
        for s_ref, p_ref, o_ref in zip(refs[:n], refs[n:2 * n], refs[2 * n:]):
            own = s_ref[0]
            total = None
            for k in range(N_CHIPS):
                theirs = p_ref[jnp.maximum(jnp.bitwise_xor(me, k) - 1, 0)].astype(F32)
                term = jnp.where(me == k, own, theirs)
                total = term if total is None else total + term
            o_ref[...] = total

    grid_spec = pltpu.PrefetchScalarGridSpec(
        num_scalar_prefetch=1, grid=(2,),
        in_specs=[pl.BlockSpec((1, p.shape[1] // 2, p.shape[2]), lambda i, me_ref: (me_ref[0], i, 0)) for p in parts]
        + [pl.BlockSpec((3, p.shape[1] // 2, p.shape[2]), lambda i, me_ref: (0, i, 0)) for p in parts],
        out_specs=[pl.BlockSpec((p.shape[1] // 2, p.shape[2]), lambda i, me_ref: (i, 0)) for p in parts])
    me = (2 * lax.axis_index("x") + lax.axis_index("y")).astype(jnp.int32).reshape(1)
    return pl.pallas_call(
        body, name="rs_add_chips", grid_spec=grid_spec,
        out_shape=[jax.ShapeDtypeStruct(p.shape[1:], F32) for p in parts],
        compiler_params=_cparams(("parallel",)),
    )(me, *chipsum, *parts)


def _rs_swap_halves(halves):
    n = len(halves)

    def body(*refs):
        h_refs, out_refs = refs[:n], refs[n:2 * n]
        send_sems, recv_sems = refs[2 * n:]
        x, y, c = _place()
        copies = [pltpu.make_async_remote_copy(src_ref=h_ref, dst_ref=out_ref, send_sem=send_sems.at[a], recv_sem=recv_sems.at[a],
                                               device_id=(x, y, 1 - c), device_id_type=MESH_ID)
                  for a, (h_ref, out_ref) in enumerate(zip(h_refs, out_refs))]
        for cp in copies:
            cp.start()
        for cp in copies:
            cp.wait()

    return pl.pallas_call(
        body, name="rs_swap_halves", in_specs=[HBM_SPEC] * n, out_specs=[HBM_SPEC] * n,
        out_shape=[jax.ShapeDtypeStruct(h.shape, F32) for h in halves],
        scratch_shapes=[pltpu.SemaphoreType.DMA((n,)), pltpu.SemaphoreType.DMA((n,))],
    )(*halves)


def _adamw_list(ws, g_mine, g_theirs, ms, vs):
    n = len(ws)

    def body(c_ref, *refs):
        w_refs, gm_refs, gt_refs, m_refs, v_refs = (refs[k * n:(k + 1) * n] for k in range(5))
        g_refs, d_refs, nm_refs, nv_refs = (refs[k * n:(k + 1) * n] for k in range(5, 9))
        mine = (pl.program_id(0) // ADAM_STEPS_PER_HALF) == c_ref[0]
        for a in range(n):
            gv = jnp.where(mine, gm_refs[a][...], gt_refs[a][...])
            g_refs[a][...] = gv
            m_new = ADAM_B1 * m_refs[a][...] + (1.0 - ADAM_B1) * gv
            v_new = ADAM_B2 * v_refs[a][...] + (1.0 - ADAM_B2) * (gv * gv)
            m_hat = m_new / (1.0 - ADAM_B1 ** ADAM_STEP)
            v_hat = v_new / (1.0 - ADAM_B2 ** ADAM_STEP)
            d_refs[a][...] = -ADAM_LR * (m_hat / (jnp.sqrt(v_hat) + ADAM_EPS) + ADAM_WD * w_refs[a][...])
            nm_refs[a][...] = m_new
            nv_refs[a][...] = v_new

    steps = 2 * ADAM_STEPS_PER_HALF
    whole = [pl.BlockSpec((w.shape[0] // steps, w.shape[1]), lambda i, c_ref: (i, 0)) for w in ws]
    half = [pl.BlockSpec((w.shape[0] // steps, w.shape[1]), lambda i, c_ref: (i % ADAM_STEPS_PER_HALF, 0)) for w in ws]
    shapes = [jax.ShapeDtypeStruct(w.shape, F32) for w in ws]
    grid_spec = pltpu.PrefetchScalarGridSpec(num_scalar_prefetch=1, grid=(steps,),
                                             in_specs=whole + half + half + whole + whole, out_specs=whole * 4)
    res = pl.pallas_call(
        body, name="adamw", grid_spec=grid_spec, out_shape=shapes * 4,
        compiler_params=_cparams(("parallel",)),
    )(lax.axis_index("c").astype(jnp.int32).reshape(1), *ws, *g_mine, *g_theirs, *ms, *vs)
    return res[:n], res[n:2 * n], res[2 * n:3 * n], res[3 * n:]


WEIGHT_NAMES = ("w_in", "w_mem_kv", "q_a_gain", "w_q_b", "kv_a_gain", "w_kv_b", "w_branch_mla", "w_branch_sb",
                "w_branch_mem", "w_merge_gate", "b_merge_gate", "w_out", "ln_gain", "ln_bias")
SMALL_NAMES = tuple(n for n, _ in SMALL_SIZES)


def kernel(x, mem, w_in, w_mem_kv, q_a_gain, w_q_b, kv_a_gain, w_kv_b, w_branch_mla, w_branch_sb, w_branch_mem, w_merge_gate, b_merge_gate, w_out, ln_gain, ln_bias, loss_target, m_w_in, m_w_mem_kv, m_q_a_gain, m_w_q_b, m_kv_a_gain, m_w_kv_b, m_w_branch_mla, m_w_branch_sb, m_w_branch_mem, m_w_merge_gate, m_b_merge_gate, m_w_out, m_ln_gain, m_ln_bias, v_w_in, v_w_mem_kv, v_q_a_gain, v_w_q_b, v_kv_a_gain, v_w_kv_b, v_w_branch_mla, v_w_branch_sb, v_w_branch_mem, v_w_merge_gate, v_b_merge_gate, v_w_out, v_ln_gain, v_ln_bias):
    weights = dict(zip(WEIGHT_NAMES, (w_in, w_mem_kv, q_a_gain, w_q_b, kv_a_gain, w_kv_b, w_branch_mla, w_branch_sb,
                                      w_branch_mem, w_merge_gate, b_merge_gate, w_out, ln_gain, ln_bias)))
    mom1 = dict(zip(WEIGHT_NAMES, (m_w_in, m_w_mem_kv, m_q_a_gain, m_w_q_b, m_kv_a_gain, m_w_kv_b, m_w_branch_mla,
                                   m_w_branch_sb, m_w_branch_mem, m_w_merge_gate, m_b_merge_gate, m_w_out, m_ln_gain,
                                   m_ln_bias)))
    mom2 = dict(zip(WEIGHT_NAMES, (v_w_in, v_w_mem_kv, v_q_a_gain, v_w_q_b, v_kv_a_gain, v_w_kv_b, v_w_branch_mla,
                                   v_w_branch_sb, v_w_branch_mem, v_w_merge_gate, v_b_merge_gate, v_w_out, v_ln_gain,
                                   v_ln_bias)))
    def as_list(d):
        return [d[n][0] for n in BIG_NAMES] + [_pack_small({n: d[n] for n in SMALL_NAMES})]

    w_list, m_list, v_list = as_list(weights), as_list(mom1), as_list(mom2)

    gathered = _gather_shards([weights[n][0] for n in LATE_NAMES])
    first_w = {n: _join_chips(n, g) for n, g in zip(LATE_NAMES, gathered)}
    rest_shards = _cast_bf16_list([weights[n][0] for n in EARLY_NAMES])
    small = {n: weights[n] for n in SMALL_NAMES}

    seq = x.shape[1]
    _, grad_x, late_mine, _, early_mine = _local_step(
        x[0], mem[0], loss_target[0], first_w, small, tq=min(1024, seq), tq_sb_bwd=min(1024, seq), tk=256, tk_mla=512,
        t_row=256, t_wg=min(2048, seq), rest_shards=rest_shards)
    by_name = dict(zip(EARLY_NAMES + LATE_NAMES + ("small",), list(early_mine) + list(late_mine)))
    mine = [by_name[n] for n in BIG_NAMES + ("small",)]
    theirs = _rs_swap_halves(mine)
    g_list, d_list, nm_list, nv_list = _adamw_list(w_list, mine, theirs, m_list, v_list)

    loss = g_list[-1].reshape(-1)[SMALL_TOTAL]
    outs = [loss, grad_x[None]]
    for arrays in (g_list, d_list, nm_list, nv_list):
        big = dict(zip(BIG_NAMES, arrays[:-1]))
        sm = _unpack_small(arrays[-1])
        outs.extend(big[n][None] if n in big else sm[n] for n in WEIGHT_NAMES)
    return tuple(outs)
```

```python
import functools
import math

import numpy as np
import jax
import jax.numpy as jnp
from jax import lax
from jax.experimental import pallas as pl
from jax.experimental.pallas import tpu as pltpu

F32 = jnp.float32
BF16 = jnp.bfloat16
MESH_ID = pl.DeviceIdType.MESH

D_MODEL = 1024
MEM_LEN = 256
MLA_HEADS = 8
MLA_NOPE = 64
MLA_ROPE = 32
MLA_V = 64
MLA_Q_LORA = 256
MLA_KV_LORA = 128
SB_HEADS = 8
SB_HEAD_DIM = 64
MEM_HEADS = 4
MEM_HEAD_DIM = 128
ROPE_BASE = 10000.0
RMS_EPS = 1e-6
LN_EPS = 1e-5
DEEPNORM_ALPHA = 2.0 ** 0.25
MLA_SCALE = 1.0 / math.sqrt(MLA_NOPE + MLA_ROPE)
SB_SCALE = 1.0 / math.sqrt(SB_HEAD_DIM)
MEM_SCALE = 1.0 / math.sqrt(MEM_HEAD_DIM)

ADAM_LR = 0.001
ADAM_B1 = 0.9
ADAM_B2 = 0.999
ADAM_EPS = 1e-08
ADAM_WD = 0.01
ADAM_STEP = 10

LANES = 128
HALF = 64
N_CHIPS = 4
PACK_COLS = 1024
VMEM_LIMIT = 56 * 1024 * 1024

IN_WIDTH_P = 4096
BLK_LAT, BLK_GATE_A, BLK_QB, BLK_KB, BLK_VB, BLK_GATE_B, BLK_QM, BLK_GATE_M = range(8)
P32_POS = {blk: n for n, blk in enumerate((BLK_LAT, BLK_GATE_A, BLK_GATE_B, BLK_GATE_M))}
PBF_POS = {blk: n for n, blk in enumerate((BLK_QB, BLK_KB, BLK_VB, BLK_QM))}
N_MERGE = 3 * D_MODEL

BIG_NAMES = ("w_in", "w_mem_kv", "w_q_b", "w_kv_b", "w_branch_mla", "w_branch_sb", "w_branch_mem", "w_merge_gate", "w_out")
COL_SHARDED = ("w_in", "w_q_b", "w_kv_b", "w_branch_mla", "w_branch_sb", "w_branch_mem", "w_merge_gate")
SMALL_SIZES = (("q_a_gain", 256), ("kv_a_gain", 128), ("b_merge_gate", 3072), ("ln_gain", 1024), ("ln_bias", 1024))
SMALL_TOTAL = sum(s for _, s in SMALL_SIZES)


def _cparams(sem=None):
    return pltpu.CompilerParams(dimension_semantics=sem, vmem_limit_bytes=VMEM_LIMIT)


def _dot(a, b):
    return jnp.dot(a, b, preferred_element_type=F32)


def _dot_nt(a, b):
    return lax.dot_general(a, b, (((1,), (1,)), ((), ())), preferred_element_type=F32)


def _dot_tn(a, b):
    return lax.dot_general(a, b, (((0,), (0,)), ((), ())), preferred_element_type=F32)


def _bf(x):
    return x.astype(BF16)


def _dot_cols(a, w_ref):
    return jnp.concatenate([_dot(a, w_ref[j]) for j in range(w_ref.shape[0])], axis=1)


def _dot_nt_cols(a, w_ref):
    cs = w_ref.shape[2]
    out = None
    for j in range(w_ref.shape[0]):
        term = _dot_nt(a[:, j * cs:(j + 1) * cs], w_ref[j])
        out = term if out is None else out + term
    return out


def _sigmoid(x):
    return 1.0 / (1.0 + jnp.exp(-x))


def _matmul(a, b, *, mode, tm, tn, tk, out_dtypes, name, add=None, add_scale=1.0, by_column_block=False):
    if mode == "nn":
        (m, k), n = a.shape, b.shape[1]
        a_spec = pl.BlockSpec((tm, tk), lambda i, j, kk: (i, kk))
        b_spec = pl.BlockSpec((tk, tn), lambda i, j, kk: (kk, j))
        dot = _dot
    elif mode == "nt":
        (m, k), n = a.shape, b.shape[0]
        a_spec = pl.BlockSpec((tm, tk), lambda i, j, kk: (i, kk))
        b_spec = pl.BlockSpec((tn, tk), lambda i, j, kk: (j, kk))
        dot = _dot_nt
    else:
        (k, m), n = a.shape, b.shape[1]
        a_spec = pl.BlockSpec((tk, tm), lambda i, j, kk: (kk, i))
        b_spec = pl.BlockSpec((tk, tn), lambda i, j, kk: (kk, j))
        dot = _dot_tn
    assert m % tm == 0 and n % tn == 0 and k % tk == 0, (name, m, n, k)
    nk = k // tk
    n_out = len(out_dtypes)
    has_add = add is not None

    def body(*refs):
        a_ref, b_ref = refs[0], refs[1]
        add_ref = refs[2] if has_add else None
        outs = refs[2 + has_add: 2 + has_add + n_out]
        acc = refs[-1]
        kk = pl.program_id(2)

        @pl.when(kk == 0)
        def _():
            acc[...] = jnp.zeros_like(acc)

        acc[...] += dot(_bf(a_ref[...]), _bf(b_ref[...]))

        @pl.when(kk == nk - 1)
        def _():
            r = acc[...]
            if has_add:
                r = r + add_scale * add_ref[...]
            for o in outs:
                o[...] = r.astype(o.dtype)

    in_specs = [a_spec, b_spec]
    args = [a, b]
    if has_add:
        in_specs.append(pl.BlockSpec((tm, tn), lambda i, j, kk: (i, j)))
        args.append(add)
    if by_column_block:
        out_spec = pl.BlockSpec((None, tm, tn), lambda i, j, kk: (j, i, 0))
        out_dims = (n // tn, m, tn)
    else:
        out_spec = pl.BlockSpec((tm, tn), lambda i, j, kk: (i, j))
        out_dims = (m, n)
    res = pl.pallas_call(
        body, name=name, grid=(m // tm, n // tn, nk),
        in_specs=in_specs, out_specs=[out_spec] * n_out,
        out_shape=[jax.ShapeDtypeStruct(out_dims, dt) for dt in out_dtypes],
        scratch_shapes=[pltpu.VMEM((tm, tn), F32)],
        compiler_params=_cparams(("parallel", "parallel", "arbitrary")),
    )(*args)
    return res


def _rope_tables(seq):
    half = MLA_ROPE // 2
    freqs = ROPE_BASE ** (-jnp.arange(half, dtype=F32) / half)
    ang = jnp.arange(seq, dtype=jnp.int32).astype(F32)[:, None] * freqs[None, :]
    cos, sin = jnp.cos(ang), jnp.sin(ang)
    z = lambda w: jnp.zeros((seq, w), F32)
    c_q = jnp.concatenate([jnp.ones((seq, MLA_NOPE), F32), cos, cos, z(32)], axis=1)
    c_k = jnp.concatenate([z(MLA_NOPE), cos, cos, z(32)], axis=1)
    s_lo = jnp.concatenate([z(MLA_NOPE), -sin, z(half), z(32)], axis=1)
    s_hi = jnp.concatenate([z(MLA_NOPE), z(half), sin, z(32)], axis=1)
    return c_q, c_k, s_lo, s_hi


def _rope_fwd(x, c, s_lo, s_hi):
    return x * c + pltpu.roll(x, LANES - 16, 1) * s_lo + pltpu.roll(x, 16, 1) * s_hi


def _rope_bwd(d, c, s_lo, s_hi):
    return d * c - pltpu.roll(d, 16, 1) * s_hi - pltpu.roll(d, LANES - 16, 1) * s_lo


def _rms_fwd(x, g):
    r = lax.rsqrt(jnp.mean(x * x, axis=-1, keepdims=True) + RMS_EPS)
    xn = x * r
    return xn * g, xn, r


def _mla_prep(p32, gq, gkv, wqb, wkvb, tabs, *, t):
    seq = p32.shape[0]

    def body(lat_ref, gq_ref, gkv_ref, wqb_ref, wkvb_ref, cq_ref, ck_ref, slo_ref, shi_ref, q_ref, k_ref, v_ref):
        lat = lat_ref[...]
        slo, shi = slo_ref[...], shi_ref[...]
        nq, _, _ = _rms_fwd(lat[:, 0:MLA_Q_LORA], gq_ref[...])
        qa = _dot(_bf(nq), wqb_ref[...])
        cq = cq_ref[...]
        for h in range(MLA_HEADS):
            blk = qa[:, h * LANES:(h + 1) * LANES]
            q_ref[:, h * LANES:(h + 1) * LANES] = _bf(_rope_fwd(blk, cq, slo, shi))
        nkv, _, _ = _rms_fwd(lat[:, MLA_Q_LORA:MLA_Q_LORA + MLA_KV_LORA], gkv_ref[...])
        kv = _dot(_bf(nkv), wkvb_ref[...])
        kpe = _rope_fwd(lat[:, 384:512], ck_ref[...], slo, shi)
        for h in range(MLA_HEADS):
            k_ref[:, h * LANES:(h + 1) * LANES] = _bf(kv[:, h * LANES:(h + 1) * LANES] + kpe)
        v_ref[...] = _bf(kv[:, MLA_HEADS * LANES:])

    row = lambda w: pl.BlockSpec((t, w), lambda i: (i, 0))
    full = lambda shp: pl.BlockSpec(shp, lambda i: (0, 0))
    return pl.pallas_call(
        body, name="mla_prep", grid=(seq // t,),
        in_specs=[row(512), full((1, MLA_Q_LORA)), full((1, MLA_KV_LORA)), full(wqb.shape), full(wkvb.shape),
                  row(LANES), row(LANES), row(LANES), row(LANES)],
        out_specs=[row(1024), row(1024), row(512)],
        out_shape=[jax.ShapeDtypeStruct((seq, 1024), BF16), jax.ShapeDtypeStruct((seq, 1024), BF16),
                   jax.ShapeDtypeStruct((seq, 512), BF16)],
        compiler_params=_cparams(("parallel",)),
    )(p32, gq, gkv, wqb, wkvb, *tabs)


def _mla_post(p32, dq, dk, dv, gq, gkv, wqb, wkvb, tabs, *, t):
    seq = p32.shape[0]

    def body(lat_ref, dq_ref, dk_ref, dv_ref, gq_ref, gkv_ref, wqb_ref, wkvb_ref, cq_ref, ck_ref, slo_ref, shi_ref,
             dlat_ref, dwqb_ref, dwkvb_ref, dgq_ref, dgkv_ref):
        @pl.when(pl.program_id(0) == 0)
        def _():
            dwqb_ref[...] = jnp.zeros_like(dwqb_ref)
            dwkvb_ref[...] = jnp.zeros_like(dwkvb_ref)
            dgq_ref[...] = jnp.zeros_like(dgq_ref)
            dgkv_ref[...] = jnp.zeros_like(dgkv_ref)

        lat = lat_ref[...]
        slo, shi = slo_ref[...], shi_ref[...]
        cq = cq_ref[...]
        gq_v, gkv_v = gq_ref[...], gkv_ref[...]
        nq, xq, rq = _rms_fwd(lat[:, 0:MLA_Q_LORA], gq_v)
        nkv, xkv, rkv = _rms_fwd(lat[:, MLA_Q_LORA:MLA_Q_LORA + MLA_KV_LORA], gkv_v)

        dqa = jnp.concatenate(
            [_rope_bwd(dq_ref[:, h * LANES:(h + 1) * LANES], cq, slo, shi) for h in range(MLA_HEADS)], axis=1)
        dqa_b = _bf(dqa)
        dwqb_ref[...] += _dot_tn(_bf(nq), dqa_b)
        dnq = _dot_nt(dqa_b, wqb_ref[...])
        dgq_ref[...] += jnp.sum(dnq * xq, axis=0, keepdims=True)
        dxn = dnq * gq_v
        dcq = rq * (dxn - xq * jnp.mean(dxn * xq, axis=-1, keepdims=True))

        dkf = dk_ref[...]
        dkv_b = _bf(jnp.concatenate([dkf, dv_ref[...]], axis=1))
        dwkvb_ref[...] += _dot_tn(_bf(nkv), dkv_b)
        dnkv = _dot_nt(dkv_b, wkvb_ref[...])
        dgkv_ref[...] += jnp.sum(dnkv * xkv, axis=0, keepdims=True)
        dxn = dnkv * gkv_v
        dckv = rkv * (dxn - xkv * jnp.mean(dxn * xkv, axis=-1, keepdims=True))

        dkpe = dkf[:, 0:LANES]
        for h in range(1, MLA_HEADS):
            dkpe = dkpe + dkf[:, h * LANES:(h + 1) * LANES]
        dkr = _rope_bwd(dkpe, ck_ref[...], slo, shi)
        dlat_ref[...] = _bf(jnp.concatenate([dcq, dckv, dkr], axis=1))

    row = lambda w: pl.BlockSpec((t, w), lambda i: (i, 0))
    full = lambda shp: pl.BlockSpec(shp, lambda i: (0, 0))
    return pl.pallas_call(
        body, name="mla_post", grid=(seq // t,),
        in_specs=[row(512), row(1024), row(1024), row(512), full((1, MLA_Q_LORA)), full((1, MLA_KV_LORA)),
                  full(wqb.shape), full(wkvb.shape), row(LANES), row(LANES), row(LANES), row(LANES)],
        out_specs=[row(512), full(wqb.shape), full(wkvb.shape), full((1, MLA_Q_LORA)), full((1, MLA_KV_LORA))],
        out_shape=[jax.ShapeDtypeStruct((seq, 512), BF16), jax.ShapeDtypeStruct(wqb.shape, F32),
                   jax.ShapeDtypeStruct(wkvb.shape, F32), jax.ShapeDtypeStruct((1, MLA_Q_LORA), F32),
                   jax.ShapeDtypeStruct((1, MLA_KV_LORA), F32)],
        compiler_params=_cparams(("arbitrary",)),
    )(p32, dq, dk, dv, gq, gkv, wqb, wkvb, *tabs)


def _split_bf16(x):
    hi = _bf(x)
    return hi, _bf(x - hi.astype(F32))


def _tri_sum(x, u):
    hi, lo = _split_bf16(x)
    return _dot(hi, u) + _dot(lo, u)


def _softplus(z):
    return jnp.maximum(z, 0.0) + jnp.log(1.0 + jnp.exp(-jnp.abs(z)))


def _head_queries(q, left):
    zero = jnp.zeros_like(q)
    return jnp.where(left, q, zero) * SB_SCALE, jnp.where(left, zero, q) * SB_SCALE


ROW_GROUP = 128
SB_BWD_CHAINS_IN_FLIGHT = 16
ANY_HBM = pl.BlockSpec(memory_space=pltpu.HBM)


class _Exchange:
    def __init__(self, send, landing):
        self.send, self.landing = send, landing

    def start(self):
        self.send.start()

    def wait(self):
        self.landing.wait_recv()
        self.send.wait_send()


class _Rider:
    def __init__(self, operands, out_shapes, sem_shapes, copies):
        self.operands, self.out_shapes, self.sem_shapes, self.copies = list(operands), list(out_shapes), list(sem_shapes), copies


def _call_with_rider(body, rider, *, name, grid, in_specs, out_specs, out_shape, args, semantics, scratch=()):
    scratch = list(scratch)
    if rider is None:
        return pl.pallas_call(body, name=name, grid=grid, in_specs=in_specs, out_specs=out_specs, out_shape=out_shape,
                              scratch_shapes=scratch, compiler_params=_cparams(semantics))(*args)
    n_in, n_out, n_rin, n_rout = len(in_specs), len(out_specs), len(rider.operands), len(rider.out_shapes)

    def full_body(*refs):
        ins, r_ins = refs[:n_in], refs[n_in:n_in + n_rin]
        outs = refs[n_in + n_rin:n_in + n_rin + n_out]
        r_outs = refs[n_in + n_rin + n_out:n_in + n_rin + n_out + n_rout]
        rest = refs[n_in + n_rin + n_out + n_rout:]
        own_scratch, sems = rest[:len(scratch)], rest[len(scratch):]
        first, last = None, None
        for axis, size in enumerate(grid):
            at_start, at_end = pl.program_id(axis) == 0, pl.program_id(axis) == size - 1
            first = at_start if first is None else first & at_start
            last = at_end if last is None else last & at_end

        @pl.when(first)
        def _():
            for cp in rider.copies(r_ins, r_outs, sems):
                cp.start()

        body(*ins, *outs, *own_scratch)

        @pl.when(last)
        def _():
            for cp in rider.copies(r_ins, r_outs, sems):
                cp.wait()

    return pl.pallas_call(
        full_body, name=name, grid=grid, in_specs=list(in_specs) + [ANY_HBM] * n_rin,
        out_specs=list(out_specs) + [ANY_HBM] * n_rout, out_shape=list(out_shape) + rider.out_shapes,
        scratch_shapes=scratch + rider.sem_shapes, compiler_params=_cparams(("arbitrary",) * len(grid)),
    )(*args, *rider.operands)


def _chains(tq):
    return [(h, g) for g in range(tq // ROW_GROUP) for h in range(2)]


def _chain_pattern(g, m, tk, strict):
    r_lo, r_hi = g * ROW_GROUP, (g + 1) * ROW_GROUP - 1
    c_lo, c_hi = m * tk, (m + 1) * tk - 1
    if (c_lo >= r_hi) if strict else (c_lo > r_hi):
        return None
    if (c_hi < r_lo) if strict else (c_hi <= r_lo):
        return True
    rr = lax.broadcasted_iota(jnp.int32, (ROW_GROUP, tk), 0) + r_lo
    cc = lax.broadcasted_iota(jnp.int32, (ROW_GROUP, tk), 1) + c_lo
    return (cc < rr) if strict else (cc <= rr)


def _masked(x, pat, fill=0.0):
    return x if pat is True else jnp.where(pat, x, fill)


def _rows(g):
    return slice(g * ROW_GROUP, (g + 1) * ROW_GROUP)


def _tri_matrix(tk, cmp):
    rr = lax.broadcasted_iota(jnp.int32, (tk, tk), 0)
    cc = lax.broadcasted_iota(jnp.int32, (tk, tk), 1)
    return cmp(rr, cc).astype(BF16)


def _mla_attn_fwd(qp, kp, vp, *, tq, tk, tk_diag, rider=None):
    seq = qp.shape[0]
    neg = float(np.finfo(np.float32).min)
    chains = _chains(tq)

    def body(q_ref, k_ref, v_ref, o_ref, lse_ref):
        i = pl.program_id(1)
        left = lax.broadcasted_iota(jnp.int32, (tq, LANES), 1) < HALF
        qs = [q_ref[_rows(g), h * LANES:(h + 1) * LANES] for h, g in chains]

        def block(start, carry, m, tk):
            v = v_ref[pl.ds(start, tk), :]
            pats = [True if m is None else _chain_pattern(g, m, tk, False) for _, g in chains]
            live = [n for n, p in enumerate(pats) if p is not None]
            ss = {n: _dot_nt(qs[n], k_ref[pl.ds(start, tk), chains[n][0] * LANES:(chains[n][0] + 1) * LANES]) for n in live}
            new = list(carry)
            for n in live:
                m_old, l_old, acc = carry[n]
                s = _masked(ss[n] * MLA_SCALE, pats[n], neg)
                m_new = jnp.maximum(m_old, jnp.max(s, axis=-1, keepdims=True))
                a = jnp.exp(m_old - m_new)
                p = jnp.exp(s - m_new)
                new[n] = (m_new, a * l_old + jnp.sum(p, axis=-1, keepdims=True), a * acc + _dot(_bf(p), v))
            return tuple(new)

        init = (jnp.full((ROW_GROUP, 1), -1e30, F32), jnp.zeros((ROW_GROUP, 1), F32), jnp.zeros((ROW_GROUP, LANES), F32))
        def two_blocks(j, c):
            c = block(pl.multiple_of(2 * j * tk, tk), c, None, tk)
            return block(pl.multiple_of((2 * j + 1) * tk, tk), c, None, tk)

        carry = lax.fori_loop(0, i * (tq // tk) // 2, two_blocks, (init,) * len(chains))
        for m in range(tq // tk_diag):
            carry = block(pl.multiple_of(i * tq + m * tk_diag, tk_diag), carry, m, tk_diag)
        per_head = []
        for h in range(2):
            mine = [carry[n] for n, (ch, _) in enumerate(chains) if ch == h]
            per_head.append((jnp.concatenate([acc / l for _, l, acc in mine], axis=0),
                             jnp.concatenate([mm + jnp.log(l) for mm, l, _ in mine], axis=0)))
        o_ref[...] = jnp.where(left, per_head[0][0], per_head[1][0])
        lse_ref[...] = jnp.where(left, per_head[0][1], per_head[1][1])

    return _call_with_rider(
        body, rider, name="mla_fwd", grid=(MLA_HEADS // 2, seq // tq),
        in_specs=[pl.BlockSpec((tq, 2 * LANES), lambda p, i: (i, p)), pl.BlockSpec((seq, 2 * LANES), lambda p, i: (0, p)),
                  pl.BlockSpec((seq, LANES), lambda p, i: (0, p))],
        out_specs=[pl.BlockSpec((tq, LANES), lambda p, i: (i, p)), pl.BlockSpec((tq, LANES), lambda p, i: (i, p))],
        out_shape=[jax.ShapeDtypeStruct((seq, 512), F32), jax.ShapeDtypeStruct((seq, 512), F32)],
        args=(qp, kp, vp), semantics=("parallel", "parallel"))


def _mla_attn_fwd_state(qp, kp, vp, *, tq, tk, tk_diag, rider=None):
    seq = qp.shape[0]
    neg = float(np.finfo(np.float32).min)
    chains = _chains(tq)
    n_rows = len(chains) * ROW_GROUP

    def body(q_ref, k_ref, v_ref, o_ref, lse_ref, m_ref, l_ref, acc_ref):
        i = pl.program_id(1)
        left = lax.broadcasted_iota(jnp.int32, (tq, LANES), 1) < HALF
        qs = [q_ref[_rows(g), h * LANES:(h + 1) * LANES] for h, g in chains]
        m_ref[...] = jnp.full(m_ref.shape, -1e30, F32)
        l_ref[...] = jnp.zeros_like(l_ref)
        acc_ref[...] = jnp.zeros_like(acc_ref)

        def block(start, m, tk):
            v = v_ref[pl.ds(start, tk), :]
            pats = [True if m is None else _chain_pattern(g, m, tk, False) for _, g in chains]
            live = [n for n, p in enumerate(pats) if p is not None]
            ss = {n: _dot_nt(qs[n], k_ref[pl.ds(start, tk), chains[n][0] * LANES:(chains[n][0] + 1) * LANES]) for n in live}
            for n in live:
                mine = _rows(n)
                m_old = m_ref[mine, :]
                s = _masked(ss[n] * MLA_SCALE, pats[n], neg)
                m_new = jnp.maximum(m_old, jnp.max(s, axis=-1, keepdims=True))
                a = jnp.exp(m_old - m_new)
                p = jnp.exp(s - m_new)
                m_ref[mine, :] = m_new
                l_ref[mine, :] = a * l_ref[mine, :] + jnp.sum(p, axis=-1, keepdims=True)
                acc_ref[mine, :] = a * acc_ref[mine, :] + _dot(_bf(p), v)

        def two_blocks(j, c):
            block(pl.multiple_of(2 * j * tk, tk), None, tk)
            block(pl.multiple_of((2 * j + 1) * tk, tk), None, tk)
            return c

        lax.fori_loop(0, i * (tq // tk) // 2, two_blocks, 0)
        for m in range(tq // tk_diag):
            block(pl.multiple_of(i * tq + m * tk_diag, tk_diag), m, tk_diag)
        per_head = []
        for h in range(2):
            mine = [_rows(n) for n, (ch, _) in enumerate(chains) if ch == h]
            per_head.append((jnp.concatenate([acc_ref[r, :] / l_ref[r, :] for r in mine], axis=0),
                             jnp.concatenate([m_ref[r, :] + jnp.log(l_ref[r, :]) for r in mine], axis=0)))
        o_ref[...] = jnp.where(left, per_head[0][0], per_head[1][0])
        lse_ref[...] = jnp.where(left, per_head[0][1], per_head[1][1])

    return _call_with_rider(
        body, rider, name="mla_fwd", grid=(MLA_HEADS // 2, seq // tq),
        in_specs=[pl.BlockSpec((tq, 2 * LANES), lambda p, i: (i, p)), pl.BlockSpec((seq, 2 * LANES), lambda p, i: (0, p)),
                  pl.BlockSpec((seq, LANES), lambda p, i: (0, p))],
        out_specs=[pl.BlockSpec((tq, LANES), lambda p, i: (i, p)), pl.BlockSpec((tq, LANES), lambda p, i: (i, p))],
        out_shape=[jax.ShapeDtypeStruct((seq, 512), F32), jax.ShapeDtypeStruct((seq, 512), F32)],
        args=(qp, kp, vp), semantics=("parallel", "parallel"),
        scratch=[pltpu.VMEM((n_rows, 1), F32), pltpu.VMEM((n_rows, 1), F32), pltpu.VMEM((n_rows, LANES), F32)])


def _mla_attn_bwd(qp, kp, vp, o, lse, do, *, tq, tk, tk_diag, rider=None):
    seq = qp.shape[0]
    chains = _chains(tq)

    def body(q_ref, k_ref, v_ref, o_ref, lse_ref, do_ref, dq_ref, dk_ref, dv_ref, qt_ref, dot_ref):
        i = pl.program_id(1)

        @pl.when(i == 0)
        def _():
            dk_ref[...] = jnp.zeros_like(dk_ref)
            dv_ref[...] = jnp.zeros_like(dv_ref)

        left = lax.broadcasted_iota(jnp.int32, (tq, LANES), 1) < HALF
        do_f = do_ref[...]
        prod = do_f * o_ref[...]
        lse_v = lse_ref[...]
        do_heads = (_bf(jnp.where(left, do_f, 0.0)), _bf(jnp.where(left, 0.0, do_f)))
        delta_heads = (jnp.sum(jnp.where(left, prod, 0.0), axis=-1, keepdims=True),
                       jnp.sum(jnp.where(left, 0.0, prod), axis=-1, keepdims=True))
        qs = [q_ref[_rows(g), h * LANES:(h + 1) * LANES] for h, g in chains]
        dos = [do_heads[h][_rows(g)] for h, g in chains]
        deltas = [delta_heads[h][_rows(g)] for h, g in chains]
        lses = [lse_v[_rows(g), h * HALF:h * HALF + 1] for h, g in chains]
        for h in range(2):
            qt_ref[h] = q_ref[:, h * LANES:(h + 1) * LANES].T
            dot_ref[h] = do_heads[h].T
        q_t = [qt_ref.at[h] for h in range(2)]
        do_t = [dot_ref.at[h] for h in range(2)]

        def block(start, carry, m, tk):
            v = v_ref[pl.ds(start, tk), :]
            pats = [True if m is None else _chain_pattern(g, m, tk, False) for _, g in chains]
            live = [n for n, p in enumerate(pats) if p is not None]
            ks = [k_ref[pl.ds(start, tk), h * LANES:(h + 1) * LANES] for h in range(2)]
            ss = {n: _dot_nt(qs[n], ks[chains[n][0]]) for n in live}
            dps = {n: _dot_nt(dos[n], v) for n in live}
            new = list(carry)
            ps, dss = {}, {}
            for n in live:
                p = _masked(jnp.exp(ss[n] * MLA_SCALE - lses[n]), pats[n])
                ps[n] = _bf(p)
                dss[n] = _bf(p * (dps[n] - deltas[n]) * MLA_SCALE)
                new[n] = carry[n] + _dot(dss[n], ks[chains[n][0]])
            dv_t, dk_t = None, []
            for h in range(2):
                mine = [n for n in live if chains[n][0] == h]
                first_row = chains[mine[0]][1] * ROW_GROUP
                ds_cat = jnp.concatenate([dss[n] for n in mine], axis=0)
                p_cat = jnp.concatenate([ps[n] for n in mine], axis=0)
                if first_row == 0:
                    q_rows_t, do_rows_t = q_t[h][...], do_t[h][...]
                else:
                    q_rows_t = q_ref[first_row:, h * LANES:(h + 1) * LANES].T
                    do_rows_t = do_heads[h][first_row:].T
                dk_t.append(_dot(q_rows_t, ds_cat))
                term = _dot(do_rows_t, p_cat)
                dv_t = term if dv_t is None else dv_t + term
            back = jnp.concatenate(dk_t + [dv_t], axis=0).T
            dk_ref[pl.ds(start, tk), :] += back[:, :2 * LANES]
            dv_ref[pl.ds(start, tk), :] += back[:, 2 * LANES:]
            return tuple(new)

        zero = jnp.zeros((ROW_GROUP, LANES), F32)
        carry = lax.fori_loop(0, i * (tq // tk), lambda j, c: block(pl.multiple_of(j * tk, tk), c, None, tk),
                              (zero,) * len(chains))
        for m in range(tq // tk_diag):
            carry = block(pl.multiple_of(i * tq + m * tk_diag, tk_diag), carry, m, tk_diag)
        for n, (h, g) in enumerate(chains):
            dq_ref[_rows(g), h * LANES:(h + 1) * LANES] = carry[n]

    two_t = pl.BlockSpec((tq, 2 * LANES), lambda p, i: (i, p))
    two_s = pl.BlockSpec((seq, 2 * LANES), lambda p, i: (0, p))
    pair_t = pl.BlockSpec((tq, LANES), lambda p, i: (i, p))
    pair_s = pl.BlockSpec((seq, LANES), lambda p, i: (0, p))
    return _call_with_rider(
        body, rider, name="mla_bwd", grid=(MLA_HEADS // 2, seq // tq),
        in_specs=[two_t, two_s, pair_s, pair_t, pair_t, pair_t],
        out_specs=[two_t, two_s, pair_s],
        out_shape=[jax.ShapeDtypeStruct((seq, 1024), F32), jax.ShapeDtypeStruct((seq, 1024), F32),
                   jax.ShapeDtypeStruct((seq, 512), F32)],
        args=(qp, kp, vp, o, lse, do), semantics=("parallel", "arbitrary"),
        scratch=[pltpu.VMEM((2, LANES, tq), BF16), pltpu.VMEM((2, LANES, tq), BF16)])


def _sb_attn_fwd(pbf, *, tq, tk):
    seq = pbf.shape[0]
    nd = tq // tk
    qb, kb, vb = PBF_POS[BLK_QB] * 4, PBF_POS[BLK_KB] * 4, PBF_POS[BLK_VB] * 4
    chains = _chains(tq)

    def body(q_ref, k_ref, v_ref, o_ref, tot_ref):
        i = pl.program_id(1)
        u_later = _tri_matrix(tk, lambda r, c: r > c)
        left = lax.broadcasted_iota(jnp.int32, (tq, LANES), 1) < HALF
        q_heads = _head_queries(q_ref[...], left)
        qs = [q_heads[h][_rows(g)] for h, g in chains]

        def block(j, carry, m):
            start = pl.multiple_of(j * tk, tk)
            k = k_ref[pl.ds(start, tk), :]
            v = v_ref[pl.ds(start, tk), :]
            pats = [True if m is None else _chain_pattern(g, m, tk, True) for _, g in chains]
            live = [n for n, p in enumerate(pats) if p is not None]
            zs = {n: _dot_nt(qs[n], k) for n in live}
            raws = {n: _softplus(zs[n]) for n in live}
            sps = {n: _masked(raws[n], pats[n]) for n in live}
            laters = {n: _tri_sum(sps[n], u_later) for n in live}
            new = list(carry)
            for n in live:
                c, acc = carry[n]
                a = _masked(jnp.exp(zs[n] - raws[n] - laters[n] - c), pats[n])
                new[n] = (c + laters[n][:, 0:1] + sps[n][:, 0:1], acc + _dot(_bf(a), v))
            return tuple(new)

        init = (jnp.zeros((ROW_GROUP, 1), F32), jnp.zeros((ROW_GROUP, LANES), F32))
        carry = (init,) * len(chains)
        for m in reversed(range(nd)):
            carry = block(i * nd + m, carry, m)
        per_trip = 4 if nd % 4 == 0 else 2

        def trip(jj, cr):
            for u in range(per_trip):
                cr = block(i * nd - 1 - (per_trip * jj + u), cr, None)
            return cr

        carry = lax.fori_loop(0, i * nd // per_trip, trip, carry)
        per_head = []
        for h in range(2):
            mine = [carry[n] for n, (ch, _) in enumerate(chains) if ch == h]
            per_head.append((jnp.concatenate([acc for _, acc in mine], axis=0), jnp.concatenate([c for c, _ in mine], axis=0)))
        o_ref[...] = jnp.where(left, per_head[0][0], per_head[1][0])
        tot_ref[...] = jnp.where(left, per_head[0][1], per_head[1][1])

    pair_t = pl.BlockSpec((tq, LANES), lambda p, i: (i, p))
    return pl.pallas_call(
        body, name="sb_fwd", grid=(SB_HEADS // 2, seq // tq),
        in_specs=[pl.BlockSpec((tq, LANES), lambda p, i: (i, qb + p)), pl.BlockSpec((seq, LANES), lambda p, i: (0, kb + p)),
                  pl.BlockSpec((seq, LANES), lambda p, i: (0, vb + p))],
        out_specs=[pair_t, pair_t],
        out_shape=[jax.ShapeDtypeStruct((seq, 512), F32), jax.ShapeDtypeStruct((seq, 512), F32)],
        compiler_params=_cparams(("parallel", "parallel")),
    )(pbf, pbf, pbf)


def _sb_attn_bwd(pbf, tot, do, *, tq, tk, rider=None):
    seq = pbf.shape[0]
    nd = tq // tk
    qb, kb, vb = PBF_POS[BLK_QB] * 4, PBF_POS[BLK_KB] * 4, PBF_POS[BLK_VB] * 4
    chains = _chains(tq)
    group = SB_BWD_CHAINS_IN_FLIGHT

    def body(q_ref, k_ref, v_ref, tot_ref, do_ref, dq_ref, dk_ref, dv_ref, qt_ref, dot_ref):
        i = pl.program_id(1)

        @pl.when(i == 0)
        def _():
            dk_ref[...] = jnp.zeros_like(dk_ref)
            dv_ref[...] = jnp.zeros_like(dv_ref)

        u_upto = _tri_matrix(tk, lambda r, c: r <= c)
        u_below = _tri_matrix(tk, lambda r, c: r < c)
        left = lax.broadcasted_iota(jnp.int32, (tq, LANES), 1) < HALF
        q_heads = _head_queries(q_ref[...], left)
        do_f = do_ref[...]
        do_heads = (_bf(jnp.where(left, do_f, 0.0)), _bf(jnp.where(left, 0.0, do_f)))
        tot_v = tot_ref[...]
        qs = [q_heads[h][_rows(g)] for h, g in chains]
        dos = [do_heads[h][_rows(g)] for h, g in chains]
        totals = [tot_v[_rows(g), h * HALF:h * HALF + 1] for h, g in chains]
        qt_ref[...] = jnp.concatenate(qs, axis=0).T
        dot_ref[...] = jnp.concatenate(dos, axis=0).T

        def block(j, carry, m):
            start = pl.multiple_of(j * tk, tk)
            k = k_ref[pl.ds(start, tk), :]
            v = v_ref[pl.ds(start, tk), :]
            pats = [True if m is None else _chain_pattern(g, m, tk, True) for _, g in chains]
            all_live = [n for n, p in enumerate(pats) if p is not None]
            new = list(carry)
            for g0 in range(0, len(all_live), group):
                live = all_live[g0:g0 + group]
                zs = {n: _dot_nt(qs[n], k) for n in live}
                das = {n: _dot_nt(dos[n], v) for n in live}
                raws = {n: _softplus(zs[n]) for n in live}
                sps = {n: _masked(raws[n], pats[n]) for n in live}
                uptos = {n: _tri_sum(sps[n], u_upto) for n in live}
                lbs, a_s, gs = {}, {}, {}
                for n in live:
                    lbs[n] = zs[n] - raws[n]
                    a = _masked(jnp.exp(lbs[n] - (totals[n] - carry[n][0] - uptos[n])), pats[n])
                    a_s[n] = _bf(a)
                    gs[n] = das[n] * a
                belows = {n: _dot(_bf(gs[n]), u_below) for n in live}
                dzs = {}
                for n in live:
                    sp_before, g_before, dq_acc = carry[n]
                    beta = jnp.exp(lbs[n])
                    dz = _masked(gs[n] * (1.0 - beta) - (g_before + belows[n]) * beta, pats[n])
                    dzs[n] = _bf(dz)
                    new[n] = (sp_before + uptos[n][:, tk - 1:tk], g_before + belows[n][:, tk - 1:tk] + gs[n][:, tk - 1:tk],
                              dq_acc + _dot(dzs[n], k))
                dz_cat = jnp.concatenate([dzs[n] for n in live], axis=0)
                a_cat = jnp.concatenate([a_s[n] for n in live], axis=0)
                if len(live) == len(chains):
                    q_rows_t, do_rows_t = qt_ref[...], dot_ref[...]
                else:
                    q_rows_t = jnp.concatenate([qs[n] for n in live], axis=0).T
                    do_rows_t = jnp.concatenate([dos[n] for n in live], axis=0).T
                both = jnp.concatenate([_dot(q_rows_t, dz_cat), _dot(do_rows_t, a_cat)], axis=0).T
                dk_ref[pl.ds(start, tk), :] += both[:, :LANES]
                dv_ref[pl.ds(start, tk), :] += both[:, LANES:]
            return tuple(new)

        zero = jnp.zeros((ROW_GROUP, 1), F32)
        init = (zero, zero, jnp.zeros((ROW_GROUP, LANES), F32))
        per_trip = 4 if nd % 4 == 0 else 2

        def trip(j, cr):
            for u in range(per_trip):
                cr = block(per_trip * j + u, cr, None)
            return cr

        carry = lax.fori_loop(0, i * nd // per_trip, trip, (init,) * len(chains))
        for m in range(nd):
            carry = block(i * nd + m, carry, m)
        per_head = [jnp.concatenate([carry[n][2] for n, (ch, _) in enumerate(chains) if ch == h], axis=0) for h in range(2)]
        dq_ref[...] = jnp.where(left, per_head[0], per_head[1]) * SB_SCALE

    pair_t = pl.BlockSpec((tq, LANES), lambda p, i: (i, p))
    pair_s = pl.BlockSpec((seq, LANES), lambda p, i: (0, p))
    return _call_with_rider(
        body, rider, name="sb_bwd", grid=(SB_HEADS // 2, seq // tq),
        in_specs=[pl.BlockSpec((tq, LANES), lambda p, i: (i, qb + p)), pl.BlockSpec((seq, LANES), lambda p, i: (0, kb + p)),
                  pl.BlockSpec((seq, LANES), lambda p, i: (0, vb + p)), pair_t, pair_t],
        out_specs=[pair_t, pair_s, pair_s],
        out_shape=[jax.ShapeDtypeStruct((seq, 512), F32)] * 3,
        args=(pbf, pbf, pbf, tot, do), semantics=("parallel", "arbitrary"),
        scratch=[pltpu.VMEM((LANES, 2 * tq), BF16), pltpu.VMEM((LANES, 2 * tq), BF16)])


def _mem_probs(s):
    e = jnp.exp(s - jnp.max(s, axis=-1, keepdims=True))
    return e / jnp.sum(e, axis=-1, keepdims=True)


def _head_lanes(h):
    return slice(h * LANES, (h + 1) * LANES)


def _mem_fwd(pbf, mkv, *, t):
    seq = pbf.shape[0]

    def body(q_ref, kv_ref, o_ref):
        ss = [_dot_nt(q_ref[:, _head_lanes(h)], kv_ref[:, _head_lanes(h)]) * MEM_SCALE for h in range(MEM_HEADS)]
        ps = [_bf(_mem_probs(s)) for s in ss]
        for h, p in enumerate(ps):
            o_ref[:, _head_lanes(h)] = _dot(p, kv_ref[:, _head_lanes(MEM_HEADS + h)])

    return pl.pallas_call(
        body, name="mem_fwd", grid=(seq // t,),
        in_specs=[pl.BlockSpec((t, 512), lambda i: (i, PBF_POS[BLK_QM])), pl.BlockSpec((MEM_LEN, 1024), lambda i: (0, 0))],
        out_specs=pl.BlockSpec((t, 512), lambda i: (i, 0)),
        out_shape=jax.ShapeDtypeStruct((seq, 512), F32),
        compiler_params=_cparams(("parallel",)),
    )(pbf, mkv)


def _mem_bwd(pbf, mkv, do, *, t):
    seq = pbf.shape[0]

    def body(q_ref, kv_ref, do_ref, dq_ref, dkv_ref):
        @pl.when(pl.program_id(0) == 0)
        def _():
            dkv_ref[...] = jnp.zeros_like(dkv_ref)

        heads = range(MEM_HEADS)
        qs = [q_ref[:, _head_lanes(h)] for h in heads]
        ks = [kv_ref[:, _head_lanes(h)] for h in heads]
        dos = [_bf(do_ref[:, _head_lanes(h)]) for h in heads]
        ss = [_dot_nt(qs[h], ks[h]) * MEM_SCALE for h in heads]
        dps = [_dot_nt(dos[h], kv_ref[:, _head_lanes(MEM_HEADS + h)]) for h in heads]
        ps = [_mem_probs(s) for s in ss]
        dss = [_bf(ps[h] * (dps[h] - jnp.sum(dps[h] * ps[h], axis=-1, keepdims=True)) * MEM_SCALE) for h in heads]
        for h in heads:
            dq_ref[:, _head_lanes(h)] = _dot(dss[h], ks[h])
        for h in heads:
            dkv_ref[:, _head_lanes(h)] += _dot_tn(dss[h], qs[h])
            dkv_ref[:, _head_lanes(MEM_HEADS + h)] += _dot_tn(_bf(ps[h]), dos[h])

    return pl.pallas_call(
        body, name="mem_bwd", grid=(seq // t,),
        in_specs=[pl.BlockSpec((t, 512), lambda i: (i, PBF_POS[BLK_QM])), pl.BlockSpec((MEM_LEN, 1024), lambda i: (0, 0)),
                  pl.BlockSpec((t, 512), lambda i: (i, 0))],
        out_specs=[pl.BlockSpec((t, 512), lambda i: (i, 0)), pl.BlockSpec((MEM_LEN, 1024), lambda i: (0, 0))],
        out_shape=[jax.ShapeDtypeStruct((seq, 512), F32), jax.ShapeDtypeStruct((MEM_LEN, 1024), F32)],
        compiler_params=_cparams(("arbitrary",)),
    )(pbf, mkv, do)


def _mid(x, tgt, o_a, o_b, o_m, p32, wmg, bmg, wba, wbb, wbm, wout, ln_g, ln_b, *, t):
    seq = x.shape[0]
    inv_d = 1.0 / D_MODEL

    def body(x_ref, t_ref, oa_ref, ob_ref, om_ref, ga_ref, gb_ref, gm_ref, wmg_ref, bmg_ref, wba_ref, wbb_ref,
             wbm_ref, wout_ref, lg_ref, lb_ref,
             du_ref, mrg_ref, dgp_ref, ha_ref, hb_ref, hm_ref, dya_ref, dyb_ref, dym_ref, doa_ref, dob_ref, dom_ref,
             dga_ref, dgb_ref, dgm_ref, dgain_ref, dbias_ref, dbmg_ref, loss_ref):
        @pl.when(pl.program_id(0) == 0)
        def _():
            dgain_ref[...] = jnp.zeros_like(dgain_ref)
            dbias_ref[...] = jnp.zeros_like(dbias_ref)
            dbmg_ref[...] = jnp.zeros_like(dbmg_ref)
            loss_ref[...] = jnp.zeros_like(loss_ref)

        xv = x_ref[...]
        gate = _sigmoid(_dot_cols(_bf(xv), wmg_ref) + bmg_ref[...])

        branches = []
        merged = None
        for b, (o_ref, g_ref, w_ref, h_ref) in enumerate(((oa_ref, ga_ref, wba_ref, ha_ref), (ob_ref, gb_ref, wbb_ref, hb_ref),
                                                         (om_ref, gm_ref, wbm_ref, hm_ref))):
            o, gt = o_ref[...], g_ref[...]
            sg = _sigmoid(gt)
            silu = gt * sg
            h = _bf(o * silu)
            h_ref[...] = h
            y = _dot_cols(h, w_ref)
            g_b = gate[:, b * D_MODEL:(b + 1) * D_MODEL]
            term = g_b * y
            merged = term if merged is None else merged + term
            branches.append((o, gt, sg, silu, y, g_b))
        mrg_b = _bf(merged)
        mrg_ref[...] = mrg_b

        u = DEEPNORM_ALPHA * xv + _dot(mrg_b, wout_ref[...])
        mu = jnp.mean(u, axis=-1, keepdims=True)
        uc = u - mu
        rstd = lax.rsqrt(jnp.mean(uc * uc, axis=-1, keepdims=True) + LN_EPS)
        xhat = uc * rstd
        lg = lg_ref[...]
        y_out = xhat * lg + lb_ref[...]
        err = y_out - t_ref[...]
        loss_ref[...] += 0.5 * jnp.sum(jnp.mean(err * err, axis=-1, keepdims=True), axis=0, keepdims=True)
        dy = err * inv_d
        dgain_ref[...] += jnp.sum(dy * xhat, axis=0, keepdims=True)
        dbias_ref[...] += jnp.sum(dy, axis=0, keepdims=True)
        dxh = dy * lg
        du = rstd * (dxh - jnp.mean(dxh, axis=-1, keepdims=True) - xhat * jnp.mean(dxh * xhat, axis=-1, keepdims=True))
        du_ref[...] = du

        dmerged = _dot_nt(_bf(du), wout_ref[...])
        outs = ((dya_ref, doa_ref, dga_ref, wba_ref), (dyb_ref, dob_ref, dgb_ref, wbb_ref), (dym_ref, dom_ref, dgm_ref, wbm_ref))
        dgp = []
        for (o, gt, sg, silu, y, g_b), (dy_ref, do_ref, dg_ref, w_ref) in zip(branches, outs):
            dyb = _bf(dmerged * g_b)
            dy_ref[...] = dyb
            dgp.append(dmerged * y * g_b * (1.0 - g_b))
            dh = _dot_nt_cols(dyb, w_ref)
            do_ref[...] = dh * silu
            dg_ref[...] = _bf(dh * o * (sg * (1.0 + gt * (1.0 - sg))))
        dgp = jnp.concatenate(dgp, axis=1)
        dgp_ref[...] = _bf(dgp)
        dbmg_ref[...] += jnp.sum(dgp, axis=0, keepdims=True)

    row = lambda w: pl.BlockSpec((t, w), lambda i: (i, 0))
    pblk = lambda c: pl.BlockSpec((t, 512), lambda i: (i, c))
    full = lambda shp: pl.BlockSpec(shp, lambda i: (0,) * len(shp))
    sds = jax.ShapeDtypeStruct
    return pl.pallas_call(
        body, name="mid", grid=(seq // t,),
        in_specs=[row(1024), row(1024), row(512), row(512), row(512), pblk(P32_POS[BLK_GATE_A]), pblk(P32_POS[BLK_GATE_B]), pblk(P32_POS[BLK_GATE_M]),
                  full(wmg.shape), full((1, N_MERGE)), full(wba.shape), full(wbb.shape), full(wbm.shape), full(wout.shape),
                  full((1, D_MODEL)), full((1, D_MODEL))],
        out_specs=[row(1024), row(1024), row(N_MERGE), row(512), row(512), row(512), row(1024), row(1024), row(1024),
                   row(512), row(512), row(512), row(512), row(512), row(512),
                   full((1, D_MODEL)), full((1, D_MODEL)), full((1, N_MERGE)), full((1, 1))],
        out_shape=[sds((seq, 1024), F32), sds((seq, 1024), BF16), sds((seq, N_MERGE), BF16),
                   sds((seq, 512), BF16), sds((seq, 512), BF16), sds((seq, 512), BF16),
                   sds((seq, 1024), BF16), sds((seq, 1024), BF16), sds((seq, 1024), BF16),
                   sds((seq, 512), F32), sds((seq, 512), F32), sds((seq, 512), F32),
                   sds((seq, 512), BF16), sds((seq, 512), BF16), sds((seq, 512), BF16),
                   sds((1, D_MODEL), F32), sds((1, D_MODEL), F32), sds((1, N_MERGE), F32), sds((1, 1), F32)],
        compiler_params=_cparams(("arbitrary",)),
    )(x, tgt, o_a, o_b, o_m, p32, p32, p32, wmg, bmg, wba, wbb, wbm, wout, ln_g, ln_b)


def _primed_weights(w):
    w_in = w["w_in"]
    zc = lambda n: jnp.zeros((D_MODEL, n), w_in.dtype)
    w_in_p = jnp.concatenate([w_in[:, 0:384], zc(64), w_in[:, 384:416], zc(32), w_in[:, 416:]], axis=1)
    wqb = jnp.pad(w["w_q_b"].reshape(MLA_Q_LORA, MLA_HEADS, 96), ((0, 0), (0, 0), (0, 32))).reshape(MLA_Q_LORA, 1024)
    kv3 = w["w_kv_b"].reshape(MLA_KV_LORA, MLA_HEADS, 128)
    wk = jnp.pad(kv3[:, :, :MLA_NOPE], ((0, 0), (0, 0), (0, 64))).reshape(MLA_KV_LORA, 1024)
    wv = kv3[:, :, MLA_NOPE:].reshape(MLA_KV_LORA, 512)
    return w_in_p, wqb, jnp.concatenate([wk, wv], axis=1)


PROJ_BLK = 512


def _proj_in(x, w_in_p, *, tm):
    seq = x.shape[0]

    def body(x_ref, w_ref, p32_ref, pbf_ref):
        xb = _bf(x_ref[...])
        for blk in range(IN_WIDTH_P // PROJ_BLK):
            r = _dot(xb, w_ref[:, blk * PROJ_BLK:(blk + 1) * PROJ_BLK])
            if blk in P32_POS:
                p32_ref[:, P32_POS[blk] * PROJ_BLK:(P32_POS[blk] + 1) * PROJ_BLK] = r
            else:
                pbf_ref[:, PBF_POS[blk] * PROJ_BLK:(PBF_POS[blk] + 1) * PROJ_BLK] = _bf(r)

    row = lambda w: pl.BlockSpec((tm, w), lambda i: (i, 0))
    return pl.pallas_call(
        body, name="proj_in", grid=(seq // tm,),
        in_specs=[row(D_MODEL), pl.BlockSpec(w_in_p.shape, lambda i: (0, 0))],
        out_specs=[row(len(P32_POS) * PROJ_BLK), row(len(PBF_POS) * PROJ_BLK)],
        out_shape=[jax.ShapeDtypeStruct((seq, len(P32_POS) * PROJ_BLK), F32),
                   jax.ShapeDtypeStruct((seq, len(PBF_POS) * PROJ_BLK), BF16)],
        compiler_params=_cparams(("parallel",)),
    )(x, w_in_p)


def _grad_x(du, dgpre, wmg, d_proj, w_in_p, *, tm, rider=None):
    seq = du.shape[0]
    n_pieces = len(d_proj)

    def body(du_ref, dg_ref, wmg_ref, *rest):
        piece_refs, win_ref, out_ref = rest[:n_pieces], rest[n_pieces], rest[n_pieces + 1]
        d_p = jnp.concatenate([_bf(p_ref[...]) for p_ref in piece_refs], axis=1)
        out_ref[...] = (DEEPNORM_ALPHA * du_ref[...] + _dot_nt_cols(dg_ref[...], wmg_ref)) + _dot_nt(d_p, win_ref[...])

    row = lambda w: pl.BlockSpec((tm, w), lambda i: (i, 0))
    whole = lambda a: pl.BlockSpec(a.shape, lambda i: (0,) * a.ndim)
    return _call_with_rider(
        body, rider, name="grad_x", grid=(seq // tm,),
        in_specs=[row(D_MODEL), row(N_MERGE), whole(wmg)] + [row(PROJ_BLK) for _ in d_proj] + [whole(w_in_p)],
        out_specs=[row(D_MODEL)], out_shape=[jax.ShapeDtypeStruct((seq, D_MODEL), F32)],
        args=(du, dgpre, wmg, *d_proj, w_in_p), semantics=("parallel",))


def _grad_w_in(x, d_proj, *, tk):
    seq = x.shape[0]
    n_pieces = len(d_proj)
    nk = seq // tk

    def body(x_ref, *rest):
        piece_refs, out_ref, acc = rest[:n_pieces], rest[n_pieces], rest[n_pieces + 1]
        j, kk = pl.program_id(0), pl.program_id(1)

        @pl.when(kk == 0)
        def _():
            acc[...] = jnp.zeros_like(acc)

        xb = _bf(x_ref[...])
        for pair in range(n_pieces // 2):
            @pl.when(j == pair)
            def _(pair=pair):
                both = jnp.concatenate([_bf(piece_refs[2 * pair][...]), _bf(piece_refs[2 * pair + 1][...])], axis=1)
                acc[...] += _dot_tn(xb, both)

        @pl.when(kk == nk - 1)
        def _():
            out_ref[...] = acc[...]

    def piece_spec(s):
        return pl.BlockSpec((tk, PROJ_BLK), lambda j, kk: (jnp.where(j == s // 2, kk, 0), 0))

    return pl.pallas_call(
        body, name="grad_w_in", grid=(n_pieces // 2, nk),
        in_specs=[pl.BlockSpec((tk, D_MODEL), lambda j, kk: (kk, 0))] + [piece_spec(s) for s in range(n_pieces)],
        out_specs=pl.BlockSpec((D_MODEL, 2 * PROJ_BLK), lambda j, kk: (0, j)),
        out_shape=jax.ShapeDtypeStruct((D_MODEL, n_pieces * PROJ_BLK), F32),
        scratch_shapes=[pltpu.VMEM((D_MODEL, 2 * PROJ_BLK), F32)],
        compiler_params=_cparams(("parallel", "arbitrary")),
    )(x, *d_proj)


EARLY_NAMES = ("w_mem_kv", "w_branch_mla", "w_branch_sb", "w_branch_mem", "w_merge_gate", "w_out")
LATE_NAMES = ("w_in", "w_q_b", "w_kv_b")


def _remote(src, dst, send_sem, recv_sem, device):
    return pltpu.make_async_remote_copy(src_ref=src, dst_ref=dst, send_sem=send_sem, recv_sem=recv_sem, device_id=device,
                                        device_id_type=MESH_ID)


def _gather_rider(shards):
    n = len(shards)

    def copies(src_refs, out_refs, sems):
        send_sems, recv_sems, local_sems = sems
        x, y, c = _place()
        me = 2 * x + y
        out = []
        for a, (s, o) in enumerate(zip(src_refs, out_refs)):
            out.append(pltpu.make_async_copy(s, o.at[me], local_sems.at[a]))
            for k, (px, py) in enumerate(_other_chips(x, y)):
                out.append(_Exchange(_remote(s, o.at[me], send_sems.at[k, a], recv_sems.at[k, a], (px, py, c)),
                                     _remote(s, o.at[2 * px + py], send_sems.at[k, a], recv_sems.at[k, a], (px, py, c))))
        return out

    return _Rider(shards, [jax.ShapeDtypeStruct((N_CHIPS,) + s.shape, s.dtype) for s in shards],
                  [pltpu.SemaphoreType.DMA((3, n)), pltpu.SemaphoreType.DMA((3, n)), pltpu.SemaphoreType.DMA((n,))], copies)


def _sibling_rider(g4):
    n = len(g4)

    def copies(g_refs, out_refs, sems):
        send_sems, recv_sems = sems
        x, y, c = _place()
        out = []
        for a, (g, o) in enumerate(zip(g_refs, out_refs)):
            half = g.shape[1] // 2
            theirs = pl.ds(pl.multiple_of((1 - c) * half, 8), half)
            cp = _remote(g.at[:, theirs, :], o, send_sems.at[a], recv_sems.at[a], (x, y, 1 - c))
            out.append(_Exchange(cp, cp))
        return out

    return _Rider(g4, [jax.ShapeDtypeStruct((N_CHIPS, g.shape[1] // 2, g.shape[2]), g.dtype) for g in g4],
                  [pltpu.SemaphoreType.DMA((n,)), pltpu.SemaphoreType.DMA((n,))], copies)


def _chips_rider(wire):
    n = len(wire)

    def copies(s_refs, out_refs, sems):
        send_sems, recv_sems = sems
        x, y, c = _place()
        out = []
        for a, (s, o) in enumerate(zip(s_refs, out_refs)):
            for k, (px, py) in enumerate(_other_chips(x, y)):
                cp = _remote(s.at[2 * px + py], o.at[RELATION_XOR[k] - 1], send_sems.at[k, a], recv_sems.at[k, a], (px, py, c))
                out.append(_Exchange(cp, cp))
        return out

    return _Rider(wire, [jax.ShapeDtypeStruct((3,) + s.shape[1:], s.dtype) for s in wire],
                  [pltpu.SemaphoreType.DMA((3, n)), pltpu.SemaphoreType.DMA((3, n))], copies)


def _local_step(x, mem, tgt, w, small, *, tq, tq_sb_bwd, tk, tk_mla, t_row, t_wg, rest_shards=None):
    seq = x.shape[0]
    on_mesh = rest_shards is not None
    w_in_p, wqb, wkvb = _primed_weights(w)
    tabs = _rope_tables(seq)

    p32, pbf = _proj_in(x, w_in_p, tm=256)
    t_light = min(2 * t_row, seq)
    qp, kp, vp = _mla_prep(p32, small["q_a_gain"], small["kv_a_gain"], wqb, wkvb, tabs, t=t_light)
    res = _mla_attn_fwd_state(qp, kp, vp, tq=tq, tk=tk_mla, tk_diag=tk, rider=_gather_rider(rest_shards) if on_mesh else None)
    o_a, lse = res[0], res[1]
    if on_mesh:
        w = dict(w, **{n: g if n in COL_SHARDED else _join_chips(n, g) for n, g in zip(EARLY_NAMES, res[2:])})
    else:
        w = dict(w, **{n: _split_by_chip(n, w[n]) for n in EARLY_NAMES if n in COL_SHARDED})
    wmg, wout = w["w_merge_gate"], w["w_out"]
    wba, wbb, wbm = w["w_branch_mla"], w["w_branch_sb"], w["w_branch_mem"]
    o_b, keep_total = _sb_attn_fwd(pbf, tq=tq, tk=tk)
    (mkv,) = _matmul(mem, w["w_mem_kv"], mode="nn", tm=MEM_LEN, tn=512, tk=D_MODEL, out_dtypes=(BF16,), name="mem_kv")
    o_m = _mem_fwd(pbf, mkv, t=t_light)

    (du, merged, dgpre, h_a, h_b, h_m, dy_a, dy_b, dy_m, do_a, do_b, do_m, dgate_a, dgate_b, dgate_m,
     d_ln_g, d_ln_b, d_bmg, loss) = _mid(x, tgt, o_a, o_b, o_m, p32, wmg, small["b_merge_gate"], wba, wbb, wbm, wout,
                                         small["ln_gain"], small["ln_bias"], t=t_row)

    wg = functools.partial(_matmul, mode="tn", tm=512, out_dtypes=(F32,))
    shard = (lambda width: dict(tn=width // N_CHIPS, by_column_block=True)) if on_mesh else (lambda width: dict(tn=1024))
    dq_m, dmkv = _mem_bwd(pbf, mkv, do_m, t=t_light)
    early = {"w_mem_kv": wg(mem, dmkv, tk=MEM_LEN, tn=1024, name="grad_w_mem_kv")[0],
             "w_branch_mla": wg(h_a, dy_a, tk=t_wg, name="grad_w_branch_a", **shard(D_MODEL))[0],
             "w_branch_sb": wg(h_b, dy_b, tk=t_wg, name="grad_w_branch_b", **shard(D_MODEL))[0],
             "w_branch_mem": wg(h_m, dy_m, tk=t_wg, name="grad_w_branch_m", **shard(D_MODEL))[0],
             "w_merge_gate": wg(x, dgpre, tk=t_wg, tm=D_MODEL, name="grad_w_merge_gate", **shard(N_MERGE))[0],
             "w_out": wg(merged, du, tk=t_wg, tn=1024, name="grad_w_out")[0]}

    if on_mesh:
        g4 = [early[n] if early[n].ndim == 3 else _split_by_chip(n, early[n]) for n in EARLY_NAMES]
        res = _mla_attn_bwd(qp, kp, vp, o_a, lse, do_a, tq=tq, tk=tk_mla, tk_diag=tk, rider=_sibling_rider(g4))
        (dqp, dkp, dvp), got = res[:3], res[3:]
        chipsum, wire = _rs_add_sibling(g4, got, [BF16] * len(g4))
        res = _sb_attn_bwd(pbf, keep_total, do_b, tq=tq_sb_bwd, tk=tk, rider=_chips_rider(wire))
        (dq_b, dk_b, dv_b), parts = res[:3], res[3:]
        early = _rs_add_chips(chipsum, parts)
    else:
        dqp, dkp, dvp = _mla_attn_bwd(qp, kp, vp, o_a, lse, do_a, tq=tq, tk=tk_mla, tk_diag=tk)
        dq_b, dk_b, dv_b = _sb_attn_bwd(pbf, keep_total, do_b, tq=tq_sb_bwd, tk=tk)
    dlat, d_wqb, d_wkvb, d_gq, d_gkv = _mla_post(p32, dqp, dkp, dvp, small["q_a_gain"], small["kv_a_gain"], wqb, wkvb, tabs,
                                                 t=t_light)

    d_proj = [dlat, dgate_a, dq_b, dk_b, dv_b, dgate_b, dq_m, dgate_m]
    d_winp = _grad_w_in(x, d_proj, tk=min(1024, seq))

    d_win = jnp.concatenate([d_winp[:, 0:384], d_winp[:, 448:480], d_winp[:, 512:]], axis=1)
    d_wq = d_wqb.reshape(MLA_Q_LORA, MLA_HEADS, 128)[:, :, :96].reshape(MLA_Q_LORA, 768)
    d_wk = d_wkvb[:, :1024].reshape(MLA_KV_LORA, MLA_HEADS, 128)[:, :, :MLA_NOPE]
    d_wv = d_wkvb[:, 1024:].reshape(MLA_KV_LORA, MLA_HEADS, MLA_V)
    d_wkv = jnp.concatenate([d_wk, d_wv], axis=2).reshape(MLA_KV_LORA, 1024)
    late = {"w_in": d_win, "w_q_b": d_wq, "w_kv_b": d_wkv}
    small_grads = {"q_a_gain": d_gq, "kv_a_gain": d_gkv, "b_merge_gate": d_bmg, "ln_gain": d_ln_g, "ln_bias": d_ln_b}
    if not on_mesh:
        (grad_x,) = _grad_x(du, dgpre, wmg, d_proj, w_in_p, tm=256)
        return loss[0, 0], grad_x, late, small_grads, early

    g4 = [_split_by_chip(n, late[n]) for n in LATE_NAMES]
    g4.append(jnp.broadcast_to(_pack_small(small_grads, scalar=loss)[None], (N_CHIPS, SMALL_ROWS, PACK_COLS)))
    got = _rs_to_sibling(g4)
    chipsum, wire = _rs_add_sibling(g4, got, [BF16] * len(LATE_NAMES) + [F32])
    res = _grad_x(du, dgpre, wmg, d_proj, w_in_p, tm=256, rider=_chips_rider(wire))
    late_mine = _rs_add_chips(chipsum, res[1:])
    return loss[0, 0], res[0], late_mine, None, early


def _place():
    x, y, c = lax.axis_index("x"), lax.axis_index("y"), lax.axis_index("c")
    return x, y, c


def _other_chips(x, y):
    return ((1 - x, y), (x, 1 - y), (1 - x, 1 - y))


SMALL_ROWS = 64
ADAM_STEPS_PER_HALF = 4


def _pack_small(d, scalar=None):
    parts = [d[n].reshape(-1) for n, _ in SMALL_SIZES] + ([] if scalar is None else [scalar.reshape(1)])
    flat = jnp.concatenate(parts)
    return jnp.pad(flat, (0, SMALL_ROWS * PACK_COLS - flat.shape[0])).reshape(SMALL_ROWS, PACK_COLS)


def _unpack_small(a):
    flat, out, c0 = a.reshape(-1), {}, 0
    for n, size in SMALL_SIZES:
        out[n] = flat[c0:c0 + size].reshape(1, size)
        c0 += size
    return out


def _split_by_chip(name, full):
    r, c = full.shape
    if name in COL_SHARDED:
        return full.reshape(r, N_CHIPS, c // N_CHIPS).transpose(1, 0, 2)
    return full.reshape(N_CHIPS, r // N_CHIPS, c)


def _join_chips(name, slots):
    _, r, cs = slots.shape
    if name in COL_SHARDED:
        return slots.transpose(1, 0, 2).reshape(r, N_CHIPS * cs)
    return slots.reshape(N_CHIPS * r, cs)


HBM_SPEC = pl.BlockSpec(memory_space=pltpu.HBM)


def _gather_shards(shards):
    n = len(shards)

    def body(*refs):
        w_refs, out_refs, wb_refs = refs[:n], refs[n:2 * n], refs[2 * n:3 * n]
        send_sems, recv_sems, pass_send_sems, pass_recv_sems, local_sems = refs[3 * n:]
        x, y, c = _place()
        me = 2 * x + y
        sibling = (x, y, 1 - c)

        def halves(ref):
            half = ref.shape[-2] // 2
            return (pl.ds(pl.multiple_of(c * half, 16), half), pl.ds(pl.multiple_of((1 - c) * half, 16), half))
        for w_ref, wb_ref in zip(w_refs, wb_refs):
            rows = w_ref.shape[0]
            chunk = min(rows, 128)

            def cast(i, carry, w_ref=w_ref, wb_ref=wb_ref, chunk=chunk):
                r0 = pl.multiple_of(i * chunk, chunk)
                wb_ref[pl.ds(r0, chunk), :] = _bf(w_ref[pl.ds(r0, chunk), :])
                return carry

            lax.fori_loop(0, rows // chunk, cast, 0)
        sends, locals_ = [], []
        for a, (wb_ref, out_ref) in enumerate(zip(wb_refs, out_refs)):
            mine = pltpu.make_async_copy(wb_ref, out_ref.at[me], local_sems.at[a])
            mine.start()
            locals_.append(mine)
            mine_rows, _ = halves(wb_ref)
            for k, (px, py) in enumerate(_other_chips(x, y)):
                cp = pltpu.make_async_remote_copy(src_ref=wb_ref.at[mine_rows, :], dst_ref=out_ref.at[me, mine_rows, :],
                                                  send_sem=send_sems.at[k, a], recv_sem=recv_sems.at[k, a],
                                                  device_id=(px, py, c), device_id_type=MESH_ID)
                cp.start()
                sends.append(cp)
        for a, (wb_ref, out_ref) in enumerate(zip(wb_refs, out_refs)):
            mine_rows, _ = halves(wb_ref)
            for k, (px, py) in enumerate(_other_chips(x, y)):
                landed = out_ref.at[2 * px + py, mine_rows, :]
                pltpu.make_async_remote_copy(src_ref=wb_ref.at[mine_rows, :], dst_ref=landed, send_sem=send_sems.at[k, a],
                                             recv_sem=recv_sems.at[k, a], device_id=(px, py, c),
                                             device_id_type=MESH_ID).wait_recv()
                cp = pltpu.make_async_remote_copy(src_ref=landed, dst_ref=landed, send_sem=pass_send_sems.at[k, a],
                                                  recv_sem=pass_recv_sems.at[k, a], device_id=sibling, device_id_type=MESH_ID)
                cp.start()
                sends.append(cp)
        for a, (wb_ref, out_ref) in enumerate(zip(wb_refs, out_refs)):
            _, their_rows = halves(wb_ref)
            for k, (px, py) in enumerate(_other_chips(x, y)):
                passed = out_ref.at[2 * px + py, their_rows, :]
                pltpu.make_async_remote_copy(src_ref=passed, dst_ref=passed, send_sem=pass_send_sems.at[k, a],
                                             recv_sem=pass_recv_sems.at[k, a], device_id=sibling,
                                             device_id_type=MESH_ID).wait_recv()
        for cp in sends:
            cp.wait_send()
        for cp in locals_:
            cp.wait()

    return pl.pallas_call(
        body, name="gather_weights",
        in_specs=[pl.BlockSpec(memory_space=pltpu.VMEM)] * n,
        out_specs=[HBM_SPEC] * n,
        out_shape=[jax.ShapeDtypeStruct((N_CHIPS,) + s.shape, BF16) for s in shards],
        scratch_shapes=[pltpu.VMEM(s.shape, BF16) for s in shards]
        + [pltpu.SemaphoreType.DMA((3, n))] * 4 + [pltpu.SemaphoreType.DMA((n,))],
        compiler_params=pltpu.CompilerParams(vmem_limit_bytes=VMEM_LIMIT),
    )(*shards)


def _cast_bf16_list(arrays):
    def body(*refs):
        for a_ref, o_ref in zip(refs[:len(arrays)], refs[len(arrays):]):
            o_ref[...] = _bf(a_ref[...])

    specs = [pl.BlockSpec((a.shape[0] // 4, a.shape[1]), lambda i: (i, 0)) for a in arrays]
    return pl.pallas_call(
        body, name="cast_shards", grid=(4,), in_specs=specs, out_specs=specs,
        out_shape=[jax.ShapeDtypeStruct(a.shape, BF16) for a in arrays],
        compiler_params=_cparams(("parallel",)),
    )(*arrays)


def _rs_to_sibling(g4):
    n = len(g4)

    def body(*refs):
        g_refs, out_refs = refs[:n], refs[n:2 * n]
        send_sems, recv_sems = refs[2 * n:]
        x, y, c = _place()
        copies = []
        for a, (g_ref, out_ref) in enumerate(zip(g_refs, out_refs)):
            half = g_ref.shape[1] // 2
            theirs = pl.ds(pl.multiple_of((1 - c) * half, 8), half)
            copies.append(pltpu.make_async_remote_copy(src_ref=g_ref.at[:, theirs, :], dst_ref=out_ref, send_sem=send_sems.at[a],
                                                       recv_sem=recv_sems.at[a], device_id=(x, y, 1 - c),
                                                       device_id_type=MESH_ID))
        for cp in copies:
            cp.start()
        for cp in copies:
            cp.wait()

    return pl.pallas_call(
        body, name="rs_sibling", in_specs=[HBM_SPEC] * n, out_specs=[HBM_SPEC] * n,
        out_shape=[jax.ShapeDtypeStruct((N_CHIPS, g.shape[1] // 2, g.shape[2]), F32) for g in g4],
        scratch_shapes=[pltpu.SemaphoreType.DMA((n,)), pltpu.SemaphoreType.DMA((n,))],
    )(*g4)


def _rs_add_sibling(g4, got, wire_dtypes):
    n = len(g4)
    narrow = [a for a in range(n) if wire_dtypes[a] != F32]

    def body(c_ref, *refs):
        outs = refs[2 * n:3 * n]
        wires = dict(zip(narrow, refs[3 * n:]))
        for a, (g_ref, r_ref, o_ref) in enumerate(zip(refs[:n], refs[n:2 * n], outs)):
            s = g_ref[...] + r_ref[...]
            o_ref[...] = s
            if a in wires:
                wires[a][...] = s.astype(wires[a].dtype)

    blk = lambda r: (1, r.shape[1], r.shape[2])
    plain = lambda r: pl.BlockSpec(blk(r), lambda j, c_ref: (j, 0, 0))
    grid_spec = pltpu.PrefetchScalarGridSpec(
        num_scalar_prefetch=1, grid=(N_CHIPS,),
        in_specs=[pl.BlockSpec(blk(r), lambda j, c_ref: (j, c_ref[0], 0)) for r in got] + [plain(r) for r in got],
        out_specs=[plain(r) for r in got] + [plain(got[a]) for a in narrow])
    res = pl.pallas_call(
        body, name="rs_add_sibling", grid_spec=grid_spec,
        out_shape=[jax.ShapeDtypeStruct(r.shape, F32) for r in got]
        + [jax.ShapeDtypeStruct(got[a].shape, wire_dtypes[a]) for a in narrow],
        compiler_params=_cparams(("parallel",)),
    )(lax.axis_index("c").astype(jnp.int32).reshape(1), *g4, *got)
    chipsum = list(res[:n])
    wire = list(chipsum)
    for a, w in zip(narrow, res[n:]):
        wire[a] = w
    return chipsum, wire


RELATION_XOR = (2, 1, 3)


def _rs_add_chips(chipsum, parts):
    n = len(parts)

    def body(me_ref, *refs):
        me = me_ref[0]
        for s_ref, p_ref, o_ref in zip(refs[:n], refs[n:2 * n], refs[2 * n:]):
            own = s_ref[0]
            total = None
            for k in range(N_CHIPS):
                theirs = p_ref[jnp.maximum(jnp.bitwise_xor(me, k) - 1, 0)].astype(F32)
                term = jnp.where(me == k, own, theirs)
                total = term if total is None else total + term
            o_ref[...] = total

    grid_spec = pltpu.PrefetchScalarGridSpec(
        num_scalar_prefetch=1, grid=(2,),
        in_specs=[pl.BlockSpec((1, p.shape[1] // 2, p.shape[2]), lambda i, me_ref: (me_ref[0], i, 0)) for p in parts]
        + [pl.BlockSpec((3, p.shape[1] // 2, p.shape[2]), lambda i, me_ref: (0, i, 0)) for p in parts],
        out_specs=[pl.BlockSpec((p.shape[1] // 2, p.shape[2]), lambda i, me_ref: (i, 0)) for p in parts])
    me = (2 * lax.axis_index("x") + lax.axis_index("y")).astype(jnp.int32).reshape(1)
    return pl.pallas_call(
        body, name="rs_add_chips", grid_spec=grid_spec,
        out_shape=[jax.ShapeDtypeStruct(p.shape[1:], F32) for p in parts],
        compiler_params=_cparams(("parallel",)),
    )(me, *chipsum, *parts)


def _rs_swap_halves(halves):
    n = len(halves)

    def body(*refs):
        h_refs, out_refs = refs[:n], refs[n:2 * n]
        send_sems, recv_sems = refs[2 * n:]
        x, y, c = _place()
        copies = [pltpu.make_async_remote_copy(src_ref=h_ref, dst_ref=out_ref, send_sem=send_sems.at[a], recv_sem=recv_sems.at[a],
                                               device_id=(x, y, 1 - c), device_id_type=MESH_ID)
                  for a, (h_ref, out_ref) in enumerate(zip(h_refs, out_refs))]
        for cp in copies:
            cp.start()
        for cp in copies:
            cp.wait()

    return pl.pallas_call(
        body, name="rs_swap_halves", in_specs=[HBM_SPEC] * n, out_specs=[HBM_SPEC] * n,
        out_shape=[jax.ShapeDtypeStruct(h.shape, F32) for h in halves],
        scratch_shapes=[pltpu.SemaphoreType.DMA((n,)), pltpu.SemaphoreType.DMA((n,))],
    )(*halves)


def _adamw_list(ws, g_mine, g_theirs, ms, vs):
    n = len(ws)

    def body(c_ref, *refs):
        w_refs, gm_refs, gt_refs, m_refs, v_refs = (refs[k * n:(k + 1) * n] for k in range(5))
        g_refs, d_refs, nm_refs, nv_refs = (refs[k * n:(k + 1) * n] for k in range(5, 9))
        mine = (pl.program_id(0) // ADAM_STEPS_PER_HALF) == c_ref[0]
        for a in range(n):
            gv = jnp.where(mine, gm_refs[a][...], gt_refs[a][...])
            g_refs[a][...] = gv
            m_new = ADAM_B1 * m_refs[a][...] + (1.0 - ADAM_B1) * gv
            v_new = ADAM_B2 * v_refs[a][...] + (1.0 - ADAM_B2) * (gv * gv)
            m_hat = m_new / (1.0 - ADAM_B1 ** ADAM_STEP)
            v_hat = v_new / (1.0 - ADAM_B2 ** ADAM_STEP)
            d_refs[a][...] = -ADAM_LR * (m_hat / (jnp.sqrt(v_hat) + ADAM_EPS) + ADAM_WD * w_refs[a][...])
            nm_refs[a][...] = m_new
            nv_refs[a][...] = v_new

    steps = 2 * ADAM_STEPS_PER_HALF
    whole = [pl.BlockSpec((w.shape[0] // steps, w.shape[1]), lambda i, c_ref: (i, 0)) for w in ws]
    half = [pl.BlockSpec((w.shape[0] // steps, w.shape[1]), lambda i, c_ref: (i % ADAM_STEPS_PER_HALF, 0)) for w in ws]
    shapes = [jax.ShapeDtypeStruct(w.shape, F32) for w in ws]
    grid_spec = pltpu.PrefetchScalarGridSpec(num_scalar_prefetch=1, grid=(steps,),
                                             in_specs=whole + half + half + whole + whole, out_specs=whole * 4)
    res = pl.pallas_call(
        body, name="adamw", grid_spec=grid_spec, out_shape=shapes * 4,
        compiler_params=_cparams(("parallel",)),
    )(lax.axis_index("c").astype(jnp.int32).reshape(1), *ws, *g_mine, *g_theirs, *ms, *vs)
    return res[:n], res[n:2 * n], res[2 * n:3 * n], res[3 * n:]


WEIGHT_NAMES = ("w_in", "w_mem_kv", "q_a_gain", "w_q_b", "kv_a_gain", "w_kv_b", "w_branch_mla", "w_branch_sb",
                "w_branch_mem", "w_merge_gate", "b_merge_gate", "w_out", "ln_gain", "ln_bias")
SMALL_NAMES = tuple(n for n, _ in SMALL_SIZES)


def kernel(x, mem, w_in, w_mem_kv, q_a_gain, w_q_b, kv_a_gain, w_kv_b, w_branch_mla, w_branch_sb, w_branch_mem, w_merge_gate, b_merge_gate, w_out, ln_gain, ln_bias, loss_target, m_w_in, m_w_mem_kv, m_q_a_gain, m_w_q_b, m_kv_a_gain, m_w_kv_b, m_w_branch_mla, m_w_branch_sb, m_w_branch_mem, m_w_merge_gate, m_b_merge_gate, m_w_out, m_ln_gain, m_ln_bias, v_w_in, v_w_mem_kv, v_q_a_gain, v_w_q_b, v_kv_a_gain, v_w_kv_b, v_w_branch_mla, v_w_branch_sb, v_w_branch_mem, v_w_merge_gate, v_b_merge_gate, v_w_out, v_ln_gain, v_ln_bias):
    weights = dict(zip(WEIGHT_NAMES, (w_in, w_mem_kv, q_a_gain, w_q_b, kv_a_gain, w_kv_b, w_branch_mla, w_branch_sb,
                                      w_branch_mem, w_merge_gate, b_merge_gate, w_out, ln_gain, ln_bias)))
    mom1 = dict(zip(WEIGHT_NAMES, (m_w_in, m_w_mem_kv, m_q_a_gain, m_w_q_b, m_kv_a_gain, m_w_kv_b, m_w_branch_mla,
                                   m_w_branch_sb, m_w_branch_mem, m_w_merge_gate, m_b_merge_gate, m_w_out, m_ln_gain,
                                   m_ln_bias)))
    mom2 = dict(zip(WEIGHT_NAMES, (v_w_in, v_w_mem_kv, v_q_a_gain, v_w_q_b, v_kv_a_gain, v_w_kv_b, v_w_branch_mla,
                                   v_w_branch_sb, v_w_branch_mem, v_w_merge_gate, v_b_merge_gate, v_w_out, v_ln_gain,
                                   v_ln_bias)))
    def as_list(d):
        return [d[n][0] for n in BIG_NAMES] + [_pack_small({n: d[n] for n in SMALL_NAMES})]

    w_list, m_list, v_list = as_list(weights), as_list(mom1), as_list(mom2)

    gathered = _gather_shards([weights[n][0] for n in LATE_NAMES])
    first_w = {n: _join_chips(n, g) for n, g in zip(LATE_NAMES, gathered)}
    rest_shards = _cast_bf16_list([weights[n][0] for n in EARLY_NAMES])
    small = {n: weights[n] for n in SMALL_NAMES}

    seq = x.shape[1]
    _, grad_x, late_mine, _, early_mine = _local_step(
        x[0], mem[0], loss_target[0], first_w, small, tq=min(1024, seq), tq_sb_bwd=min(1024, seq), tk=256, tk_mla=512,
        t_row=256, t_wg=min(2048, seq), rest_shards=rest_shards)
    by_name = dict(zip(EARLY_NAMES + LATE_NAMES + ("small",), list(early_mine) + list(late_mine)))
    mine = [by_name[n] for n in BIG_NAMES + ("small",)]
    theirs = _rs_swap_halves(mine)
    g_list, d_list, nm_list, nv_list = _adamw_list(w_list, mine, theirs, m_list, v_list)

    loss = g_list[-1].reshape(-1)[SMALL_TOTAL]
    outs = [loss, grad_x[None]]
    for arrays in (g_list, d_list, nm_list, nv_list):
        big = dict(zip(BIG_NAMES, arrays[:-1]))
        sm = _unpack_small(arrays[-1])
        outs.extend(big[n][None] if n in big else sm[n] for n in WEIGHT_NAMES)
    return tuple(outs)
```
